```python
import math
import jax
import jax.numpy as jnp
from jax import lax
import numpy as np

D_MODEL = 2048
BATCH = 1
SEQ = 8192
DEPTH = 4

N_MEM = 256
Q_BLOCK = 128
RMS_EPS = 1e-6
NEG_INF = -1e30
MAX_POS_OFFSET = 1024

S5_WIDTH = 1024
S5_GROUP = 16
S5_GROUPS = S5_WIDTH // S5_GROUP
S5_STATE = 64
S5_DT_MIN = 1e-3
S5_DT_MAX = 1e-1

FOX_HEADS = 8
FOX_HEAD_DIM = 128
FOX_WIDTH = FOX_HEADS * FOX_HEAD_DIM
FOX_FORGET_BIAS = 2.0

MEM_HEADS = 4
MEM_HEAD_DIM = 128
MEM_WIDTH = MEM_HEADS * MEM_HEAD_DIM

MLA_HEADS = 8
MLA_Q_RANK = 512
MLA_KV_RANK = 512
MLA_NOPE = 128
MLA_ROPE = 64
MLA_V = 128
MLA_WIDTH = MLA_HEADS * MLA_V
ROPE_THETA = 10000.0

NSA_HEADS = 8
NSA_KV_GROUPS = 2
NSA_HEAD_DIM = 128
NSA_WIDTH = NSA_HEADS * NSA_HEAD_DIM
NSA_KV = NSA_KV_GROUPS * NSA_HEAD_DIM
NSA_CMP_LEN = 32
NSA_CMP_STRIDE = 16
NSA_CMP_HIDDEN = 256
NSA_SLC_BLOCK = 64
NSA_SLC_TOPK = 16
NSA_WINDOW = 512
FORCE_SCORE = 1e6

T5_BUCKETS = 32
T5_MAX_DIST = 1024

N_EVEN = (DEPTH + 1) // 2
N_ODD = DEPTH // 2
EVEN_SPLITS = (S5_WIDTH, S5_WIDTH, FOX_WIDTH, FOX_WIDTH, FOX_WIDTH, FOX_HEADS, FOX_WIDTH, MEM_WIDTH, MEM_WIDTH)
ODD_SPLITS = (MLA_Q_RANK, MLA_KV_RANK, MLA_ROPE, MLA_WIDTH, NSA_WIDTH, NSA_KV, NSA_KV, NSA_KV, NSA_KV, NSA_KV, NSA_KV, 3 * NSA_HEADS, NSA_WIDTH, MEM_WIDTH, MEM_WIDTH)
EVEN_IN = sum(EVEN_SPLITS)
ODD_IN = sum(ODD_SPLITS)
MIX_WIDTH = S5_WIDTH + FOX_WIDTH + MEM_WIDTH

kernel_name = 'hybrid_s5_fox_mla_nsa_trunk'


def rms_norm(x, g):
    xf = x.astype(jnp.float32)
    y = xf * lax.rsqrt(jnp.mean(xf * xf, axis=-1, keepdims=True) + RMS_EPS)
    return (y * g.astype(jnp.float32)).astype(x.dtype)


def split_cols(h, sizes):
    return jnp.split(h, np.cumsum(sizes)[:-1].tolist(), axis=-1)


def rope(x, positions):
    half = x.shape[-1] // 2
    inv_freq = ROPE_THETA ** (-jnp.arange(half, dtype=jnp.float32) / half)
    ang = positions.astype(jnp.float32)[..., None] * inv_freq
    cos = jnp.cos(ang)[:, :, None, :]
    sin = jnp.sin(ang)[:, :, None, :]
    x1 = x[..., :half].astype(jnp.float32)
    x2 = x[..., half:].astype(jnp.float32)
    return jnp.concatenate([x1 * cos - x2 * sin, x1 * sin + x2 * cos], axis=-1).astype(x.dtype)


def t5_bucket(dist):
    n = jnp.maximum(dist, 0)
    max_exact = T5_BUCKETS // 2
    log_ratio = jnp.log(jnp.maximum(n, 1).astype(jnp.float32) / max_exact) / math.log(T5_MAX_DIST / max_exact)
    large = jnp.minimum(max_exact + (log_ratio * (T5_BUCKETS - max_exact)).astype(jnp.int32), T5_BUCKETS - 1)
    return jnp.where(n < max_exact, n, large)


def causal_block_attention(q, k, v, scale, log_decay=None):
    B_, L, H, _ = q.shape
    nblk = L // Q_BLOCK
    k_idx = jnp.arange(L)
    qb = q.reshape(B_, nblk, Q_BLOCK, H, -1).swapaxes(0, 1)
    cb = None if log_decay is None else log_decay.reshape(B_, nblk, Q_BLOCK, H).swapaxes(0, 1)
    ck = None if log_decay is None else log_decay.transpose(0, 2, 1)[:, :, None, :]

    def one_block(args):
        qi, ci, blk = args
        s = jnp.einsum('bqhd,bkhd->bhqk', qi, k, preferred_element_type=jnp.float32) * scale
        if ci is not None:
            s = s + ci.transpose(0, 2, 1)[..., None] - ck
        t = blk * Q_BLOCK + jnp.arange(Q_BLOCK)
        s = jnp.where(k_idx[None, :] <= t[:, None], s, NEG_INF)
        p = jax.nn.softmax(s, axis=-1)
        return jnp.einsum('bhqk,bkhd->bqhd', p.astype(v.dtype), v)

    o = lax.map(one_block, (qb, cb, jnp.arange(nblk)))
    return o.swapaxes(0, 1).reshape(B_, L, H, v.shape[-1])


def memory_attention(q, mem_kv):
    B_, L, _ = q.shape
    M = mem_kv.shape[1]
    qh = q.reshape(B_, L, MEM_HEADS, MEM_HEAD_DIM)
    km, vm = jnp.split(mem_kv, 2, axis=-1)
    km = km.reshape(B_, M, MEM_HEADS, MEM_HEAD_DIM)
    vm = vm.reshape(B_, M, MEM_HEADS, MEM_HEAD_DIM)
    s = jnp.einsum('blhd,bmhd->bhlm', qh, km, preferred_element_type=jnp.float32) * MEM_HEAD_DIM ** -0.5
    p = jax.nn.softmax(s, axis=-1)
    return jnp.einsum('bhlm,bmhd->blhd', p.astype(vm.dtype), vm).reshape(B_, L, MEM_WIDTH)


def s5_mixer(u, lam_re, lam_im, log_dt, b_re, b_im, c_re, c_im, d_skip, w_glu):
    f32 = jnp.float32
    B_, L, _ = u.shape
    uf = u.astype(f32).reshape(B_, L, S5_GROUPS, S5_GROUP)
    dt = jnp.exp(log_dt.astype(f32))[:, None]
    lr = lam_re.astype(f32)
    li = lam_im.astype(f32)
    mag = jnp.exp(lr * dt)
    ab_re = mag * jnp.cos(li * dt)
    ab_im = mag * jnp.sin(li * dt)
    den = lr * lr + li * li
    nr = ab_re - 1.0
    f_re = (nr * lr + ab_im * li) / den
    f_im = (ab_im * lr - nr * li) / den
    br = b_re.astype(f32)
    bim = b_im.astype(f32)
    bb_re = f_re[..., None] * br - f_im[..., None] * bim
    bb_im = f_re[..., None] * bim + f_im[..., None] * br
    bu_re = jnp.einsum('gpc,blgc->blgp', bb_re, uf)
    bu_im = jnp.einsum('gpc,blgc->blgp', bb_im, uf)
    a_re = jnp.broadcast_to(ab_re, bu_re.shape)
    a_im = jnp.broadcast_to(ab_im, bu_im.shape)

    def combine(e1, e2):
        a1r, a1i, b1r, b1i = e1
        a2r, a2i, b2r, b2i = e2
        return (a2r * a1r - a2i * a1i, a2r * a1i + a2i * a1r,
                a2r * b1r - a2i * b1i + b2r, a2r * b1i + a2i * b1r + b2i)

    _, _, xr, xi = lax.associative_scan(combine, (a_re, a_im, bu_re, bu_im), axis=1)
    y = (jnp.einsum('gcp,blgp->blgc', c_re.astype(f32), xr)
         - jnp.einsum('gcp,blgp->blgc', c_im.astype(f32), xi)
         + d_skip.astype(f32) * uf).reshape(B_, L, S5_WIDTH)
    z = jax.nn.gelu(y)
    z = z * jax.nn.sigmoid(z @ w_glu.astype(f32))
    return z.astype(u.dtype)


def mla_attention(c_q, c_kv, k_rope, positions, g_cq, g_ckv, w_uq, w_ukv):
    B_, L, _ = c_q.shape
    q = (rms_norm(c_q, g_cq) @ w_uq).reshape(B_, L, MLA_HEADS, MLA_NOPE + MLA_ROPE)
    q = jnp.concatenate([q[..., :MLA_NOPE], rope(q[..., MLA_NOPE:], positions)], axis=-1)
    kv = (rms_norm(c_kv, g_ckv) @ w_ukv).reshape(B_, L, MLA_HEADS, MLA_NOPE + MLA_V)
    kr = rope(k_rope[:, :, None, :], positions)
    k = jnp.concatenate([kv[..., :MLA_NOPE], jnp.broadcast_to(kr, (B_, L, MLA_HEADS, MLA_ROPE))], axis=-1)
    v = kv[..., MLA_NOPE:]
    o = causal_block_attention(q, k, v, (MLA_NOPE + MLA_ROPE) ** -0.5)
    return o.reshape(B_, L, MLA_WIDTH)


def nsa_attention(q, k_cmp, v_cmp, k_slc, v_slc, k_win, v_win, gate_logits, positions, t5_table, cmp_pe, cmp_w1, cmp_w2):
    f32 = jnp.float32
    B_, L, _ = q.shape
    G = NSA_KV_GROUPS
    R = NSA_HEADS // NSA_KV_GROUPS
    dh = NSA_HEAD_DIM
    scale = dh ** -0.5
    nblk = L // Q_BLOCK
    qh = q.reshape(B_, L, G, R, dh)
    k_cmp, v_cmp, k_slc, v_slc, k_win, v_win = [t.reshape(B_, L, G, dh) for t in (k_cmp, v_cmp, k_slc, v_slc, k_win, v_win)]
    gates = jax.nn.sigmoid(gate_logits.astype(f32)).reshape(B_, L, G, R, 3)

    n_cmp = (L - NSA_CMP_LEN) // NSA_CMP_STRIDE + 1
    cmp_idx = np.arange(n_cmp)[:, None] * NSA_CMP_STRIDE + np.arange(NSA_CMP_LEN)[None, :]

    def compress(t, pe, w1, w2):
        blocks = t[:, cmp_idx] + pe[None, None, :, None, :]
        blocks = blocks.transpose(0, 1, 3, 2, 4).reshape(B_, n_cmp, G, NSA_CMP_LEN * dh)
        return jax.nn.gelu(blocks @ w1) @ w2

    kc = compress(k_cmp, cmp_pe[0], cmp_w1[0], cmp_w2[0])
    vc = compress(v_cmp, cmp_pe[1], cmp_w1[1], cmp_w2[1])
    cmp_end = cmp_idx[:, -1]
    cmp_end_j = jnp.asarray(cmp_end)
    pos_cmp = positions[:, cmp_end]

    n_slc = L // NSA_SLC_BLOCK
    n_top = min(NSA_SLC_TOPK, n_slc)
    cs = np.arange(n_cmp) * NSA_CMP_STRIDE
    ss = np.arange(n_slc) * NSA_SLC_BLOCK
    ov = np.clip(np.minimum(cs[:, None] + NSA_CMP_LEN, ss[None, :] + NSA_SLC_BLOCK) - np.maximum(cs[:, None], ss[None, :]), 0, None) / NSA_CMP_LEN
    overlap = jnp.asarray(ov, jnp.float32)
    k_slc_g = k_slc.transpose(0, 2, 1, 3)
    v_slc_g = v_slc.transpose(0, 2, 1, 3)

    k_win_p = jnp.pad(k_win, ((0, 0), (NSA_WINDOW, 0), (0, 0), (0, 0)))
    v_win_p = jnp.pad(v_win, ((0, 0), (NSA_WINDOW, 0), (0, 0), (0, 0)))
    pos_p = jnp.pad(positions, ((0, 0), (NSA_WINDOW, 0)))

    table_gr = t5_table.reshape(T5_BUCKETS, G, R)
    bi = jnp.arange(B_)[:, None, None, None]
    gi = jnp.arange(G)[None, :, None, None]

    def head_bias(dist):
        return table_gr[t5_bucket(dist)].transpose(0, 3, 4, 1, 2).astype(f32)

    def one_block(args):
        qi, gq, pos_q, blk = args
        t = blk * Q_BLOCK + jnp.arange(Q_BLOCK)
        s = jnp.einsum('bqgrd,bngd->bgrqn', qi, kc, preferred_element_type=f32) * scale
        s = s + head_bias(pos_q[:, :, None] - pos_cmp[:, None, :])
        valid = cmp_end_j[None, :] <= t[:, None]
        p_c = jnp.where(valid, jax.nn.softmax(jnp.where(valid, s, NEG_INF), axis=-1), 0.0)
        o_c = jnp.einsum('bgrqn,bngd->bqgrd', p_c.astype(vc.dtype), vc)
        imp = jnp.einsum('bgrqn,nj->bgqj', p_c, overlap)
        j = jnp.arange(n_slc)
        cur = (t // NSA_SLC_BLOCK)[:, None]
        forced = (j == 0) | (j == cur) | (j == cur - 1)
        score = jnp.where(forced, FORCE_SCORE, jnp.where(j > cur, -1.0, imp))
        _, sel = lax.top_k(score, n_top)
        tok = (sel[..., None] * NSA_SLC_BLOCK + jnp.arange(NSA_SLC_BLOCK)).reshape(B_, G, Q_BLOCK, n_top * NSA_SLC_BLOCK)
        ks = k_slc_g[bi, gi, tok]
        vs = v_slc_g[bi, gi, tok]
        s = jnp.einsum('bqgrd,bgqsd->bgrqs', qi, ks, preferred_element_type=f32) * scale
        dist = pos_q[:, None, :, None] - positions[bi, tok]
        s = s + table_gr[t5_bucket(dist), gi].transpose(0, 1, 4, 2, 3).astype(f32)
        smask = (tok <= t[None, None, :, None])[:, :, None]
        p_s = jax.nn.softmax(jnp.where(smask, s, NEG_INF), axis=-1)
        o_s = jnp.einsum('bgrqs,bgqsd->bqgrd', p_s.astype(vs.dtype), vs)
        kw = lax.dynamic_slice_in_dim(k_win_p, blk * Q_BLOCK, Q_BLOCK + NSA_WINDOW, axis=1)
        vw = lax.dynamic_slice_in_dim(v_win_p, blk * Q_BLOCK, Q_BLOCK + NSA_WINDOW, axis=1)
        pk = lax.dynamic_slice_in_dim(pos_p, blk * Q_BLOCK, Q_BLOCK + NSA_WINDOW, axis=1)
        kidx = blk * Q_BLOCK - NSA_WINDOW + jnp.arange(Q_BLOCK + NSA_WINDOW)
        band = (kidx[None, :] >= 0) & (kidx[None, :] <= t[:, None]) & (t[:, None] - kidx[None, :] < NSA_WINDOW)
        s = jnp.einsum('bqgrd,bkgd->bgrqk', qi, kw, preferred_element_type=f32) * scale
        s = s + head_bias(pos_q[:, :, None] - pk[:, None, :])
        p_w = jax.nn.softmax(jnp.where(band, s, NEG_INF), axis=-1)
        o_w = jnp.einsum('bgrqk,bkgd->bqgrd', p_w.astype(vw.dtype), vw)
        o = gq[..., 0:1] * o_c + gq[..., 1:2] * o_s + gq[..., 2:3] * o_w
        return o.astype(qi.dtype)

    qb = qh.reshape(B_, nblk, Q_BLOCK, G, R, dh).swapaxes(0, 1)
    gb = gates.reshape(B_, nblk, Q_BLOCK, G, R, 3).swapaxes(0, 1)
    pb = positions.reshape(B_, nblk, Q_BLOCK).swapaxes(0, 1)
    o = lax.map(one_block, (qb, gb, pb, jnp.arange(nblk)))
    return o.swapaxes(0, 1).reshape(B_, L, NSA_WIDTH)


def even_layer(xn, mem_kv, w_in, lam_re, lam_im, log_dt, b_re, b_im, c_re, c_im, d_skip, w_glu, b_f):
    B_, L, _ = xn.shape
    u, g_s5, q, k, v, f_logit, g_fox, q_mem, g_mem = split_cols(xn @ w_in, EVEN_SPLITS)
    y_s5 = s5_mixer(u, lam_re, lam_im, log_dt, b_re, b_im, c_re, c_im, d_skip, w_glu) * jax.nn.silu(g_s5)
    log_f = jax.nn.log_sigmoid((f_logit + b_f).astype(jnp.float32))
    cum = jnp.cumsum(log_f, axis=1)
    heads = lambda t: t.reshape(B_, L, FOX_HEADS, FOX_HEAD_DIM)
    y_fox = causal_block_attention(heads(q), heads(k), heads(v), FOX_HEAD_DIM ** -0.5, cum).reshape(B_, L, FOX_WIDTH)
    y_fox = y_fox * jax.nn.silu(g_fox)
    y_mem = memory_attention(q_mem, mem_kv) * jax.nn.silu(g_mem)
    return jnp.concatenate([y_s5, y_fox.astype(y_s5.dtype), y_mem], axis=-1)


def odd_layer(xn, mem_kv, positions, t5_table, w_in, g_cq, g_ckv, w_uq, w_ukv, cmp_pe, cmp_w1, cmp_w2):
    (c_q, c_kv, k_rope, g_mla, q_nsa, k_cmp, v_cmp, k_slc, v_slc, k_win, v_win,
     nsa_gates, g_nsa, q_mem, g_mem) = split_cols(xn @ w_in, ODD_SPLITS)
    y_mla = mla_attention(c_q, c_kv, k_rope, positions, g_cq, g_ckv, w_uq, w_ukv) * jax.nn.silu(g_mla)
    y_nsa = nsa_attention(q_nsa, k_cmp, v_cmp, k_slc, v_slc, k_win, v_win, nsa_gates, positions, t5_table,
                          cmp_pe, cmp_w1, cmp_w2) * jax.nn.silu(g_nsa)
    y_mem = memory_attention(q_mem, mem_kv) * jax.nn.silu(g_mem)
    return jnp.concatenate([y_mla, y_nsa, y_mem], axis=-1)


def setup_inputs(seed: int = 0) -> dict:
    key = jax.random.key(seed)
    ks = jax.random.split(key, 32)
    f32 = jnp.float32

    def nrm(k, shape, scale):
        return scale * jax.random.normal(k, shape, f32)

    x = nrm(ks[0], (BATCH, SEQ, D_MODEL), 1.0)
    mem = nrm(ks[1], (BATCH, N_MEM, D_MODEL), 1.0)
    positions = jax.random.randint(ks[2], (BATCH, 1), 0, MAX_POS_OFFSET, jnp.int32) + jnp.arange(SEQ, dtype=jnp.int32)[None, :]
    norm_g = 1.0 + nrm(ks[3], (DEPTH, D_MODEL), 0.02)
    mem_norm_g = 1.0 + nrm(ks[4], (D_MODEL,), 0.02)
    final_norm_g = 1.0 + nrm(ks[5], (D_MODEL,), 0.02)
    t5_table = nrm(ks[6], (T5_BUCKETS, NSA_HEADS), 0.5)
    w_out = nrm(ks[7], (DEPTH, MIX_WIDTH, D_MODEL), MIX_WIDTH ** -0.5)
    mem_w_kv = nrm(ks[8], (DEPTH, D_MODEL, 2 * MEM_WIDTH), D_MODEL ** -0.5)
    even_w_in = nrm(ks[9], (N_EVEN, D_MODEL, EVEN_IN), D_MODEL ** -0.5)
    s5_lam_re = -0.5 * jnp.exp(nrm(ks[10], (N_EVEN, S5_GROUPS, S5_STATE), 0.05))
    s5_lam_im = math.pi * jnp.arange(S5_STATE, dtype=f32) + nrm(ks[11], (N_EVEN, S5_GROUPS, S5_STATE), 0.05)
    s5_log_dt = jax.random.uniform(ks[12], (N_EVEN, S5_GROUPS), f32, math.log(S5_DT_MIN), math.log(S5_DT_MAX))
    s5_b_re = nrm(ks[13], (N_EVEN, S5_GROUPS, S5_STATE, S5_GROUP), (2 * S5_GROUP) ** -0.5)
    s5_b_im = nrm(ks[14], (N_EVEN, S5_GROUPS, S5_STATE, S5_GROUP), (2 * S5_GROUP) ** -0.5)
    s5_c_re = nrm(ks[15], (N_EVEN, S5_GROUPS, S5_GROUP, S5_STATE), S5_STATE ** -0.5)
    s5_c_im = nrm(ks[16], (N_EVEN, S5_GROUPS, S5_GROUP, S5_STATE), S5_STATE ** -0.5)
    s5_d = nrm(ks[17], (N_EVEN, S5_GROUPS, S5_GROUP), 1.0)
    s5_w_glu = nrm(ks[18], (N_EVEN, S5_WIDTH, S5_WIDTH), S5_WIDTH ** -0.5)
    fox_b_f = FOX_FORGET_BIAS + nrm(ks[19], (N_EVEN, FOX_HEADS), 0.1)
    odd_w_in = nrm(ks[20], (N_ODD, D_MODEL, ODD_IN), D_MODEL ** -0.5)
    mla_g_cq = 1.0 + nrm(ks[21], (N_ODD, MLA_Q_RANK), 0.02)
    mla_g_ckv = 1.0 + nrm(ks[22], (N_ODD, MLA_KV_RANK), 0.02)
    mla_w_uq = nrm(ks[23], (N_ODD, MLA_Q_RANK, MLA_HEADS * (MLA_NOPE + MLA_ROPE)), MLA_Q_RANK ** -0.5)
    mla_w_ukv = nrm(ks[24], (N_ODD, MLA_KV_RANK, MLA_HEADS * (MLA_NOPE + MLA_V)), MLA_KV_RANK ** -0.5)
    nsa_cmp_pe = nrm(ks[25], (N_ODD, 2, NSA_CMP_LEN, NSA_HEAD_DIM), 0.5)
    nsa_cmp_w1 = nrm(ks[26], (N_ODD, 2, NSA_CMP_LEN * NSA_HEAD_DIM, NSA_CMP_HIDDEN), (NSA_CMP_LEN * NSA_HEAD_DIM) ** -0.5)
    nsa_cmp_w2 = nrm(ks[27], (N_ODD, 2, NSA_CMP_HIDDEN, NSA_HEAD_DIM), NSA_CMP_HIDDEN ** -0.5)
    return {'x': x, 'mem': mem, 'positions': positions, 'norm_g': norm_g, 'mem_norm_g': mem_norm_g,
            'final_norm_g': final_norm_g, 't5_table': t5_table, 'w_out': w_out, 'mem_w_kv': mem_w_kv,
            'even_w_in': even_w_in, 's5_lam_re': s5_lam_re, 's5_lam_im': s5_lam_im, 's5_log_dt': s5_log_dt,
            's5_b_re': s5_b_re, 's5_b_im': s5_b_im, 's5_c_re': s5_c_re, 's5_c_im': s5_c_im, 's5_d': s5_d,
            's5_w_glu': s5_w_glu, 'fox_b_f': fox_b_f, 'odd_w_in': odd_w_in, 'mla_g_cq': mla_g_cq,
            'mla_g_ckv': mla_g_ckv, 'mla_w_uq': mla_w_uq, 'mla_w_ukv': mla_w_ukv, 'nsa_cmp_pe': nsa_cmp_pe,
            'nsa_cmp_w1': nsa_cmp_w1, 'nsa_cmp_w2': nsa_cmp_w2}


def reference(x, mem, positions, norm_g, mem_norm_g, final_norm_g, t5_table, w_out, mem_w_kv,
              even_w_in, s5_lam_re, s5_lam_im, s5_log_dt, s5_b_re, s5_b_im, s5_c_re, s5_c_im, s5_d,
              s5_w_glu, fox_b_f, odd_w_in, mla_g_cq, mla_g_ckv, mla_w_uq, mla_w_ukv, nsa_cmp_pe,
              nsa_cmp_w1, nsa_cmp_w2):
    h = x
    mem_n = rms_norm(mem, mem_norm_g)
    for layer in range(DEPTH):
        xn = rms_norm(h, norm_g[layer])
        mem_kv = mem_n @ mem_w_kv[layer]
        i = layer // 2
        if layer % 2 == 0:
            mixed = even_layer(xn, mem_kv, even_w_in[i], s5_lam_re[i], s5_lam_im[i], s5_log_dt[i], s5_b_re[i],
                               s5_b_im[i], s5_c_re[i], s5_c_im[i], s5_d[i], s5_w_glu[i], fox_b_f[i])
        else:
            mixed = odd_layer(xn, mem_kv, positions, t5_table, odd_w_in[i], mla_g_cq[i], mla_g_ckv[i],
                              mla_w_uq[i], mla_w_ukv[i], nsa_cmp_pe[i], nsa_cmp_w1[i], nsa_cmp_w2[i])
        h = h + (mixed @ w_out[layer]).astype(h.dtype)
    return rms_norm(h, final_norm_g)
```

```python
import functools
import math

import numpy as np
import jax
import jax.numpy as jnp
from jax import lax
from jax.experimental import pallas as pl
from jax.experimental.pallas import tpu as pltpu

F32 = jnp.float32
BF16 = jnp.bfloat16
I32 = jnp.int32

D_MODEL = 2048
DEPTH = 4
N_MEM = 256
RMS_EPS = 1e-6
NEG_INF = -1e30

S5_WIDTH = 1024
S5_GROUP = 16
S5_GROUPS = S5_WIDTH // S5_GROUP
S5_STATE = 64
FOX_HEADS = 8
FOX_HEAD_DIM = 128
FOX_WIDTH = FOX_HEADS * FOX_HEAD_DIM
MEM_HEADS = 4
MEM_HEAD_DIM = 128
MEM_WIDTH = MEM_HEADS * MEM_HEAD_DIM
MLA_HEADS = 8
MLA_Q_RANK = 512
MLA_KV_RANK = 512
MLA_NOPE = 128
MLA_ROPE = 64
MLA_V = 128
MLA_WIDTH = MLA_HEADS * MLA_V
ROPE_THETA = 10000.0
NSA_HEADS = 8
NSA_KV_GROUPS = 2
NSA_REP = NSA_HEADS // NSA_KV_GROUPS
NSA_HEAD_DIM = 128
NSA_WIDTH = NSA_HEADS * NSA_HEAD_DIM
NSA_KV = NSA_KV_GROUPS * NSA_HEAD_DIM
NSA_CMP_LEN = 32
NSA_CMP_STRIDE = 16
NSA_CMP_HIDDEN = 256
NSA_SLC_BLOCK = 64
NSA_SLC_TOPK = 16
NSA_WINDOW = 512
FORCE_SCORE = 1e6
T5_BUCKETS = 32
T5_MAX_DIST = 1024

EVEN_SPLITS = (S5_WIDTH, S5_WIDTH, FOX_WIDTH, FOX_WIDTH, FOX_WIDTH, FOX_HEADS, FOX_WIDTH, MEM_WIDTH, MEM_WIDTH)
ODD_SPLITS = (MLA_Q_RANK, MLA_KV_RANK, MLA_ROPE, MLA_WIDTH, NSA_WIDTH, NSA_KV, NSA_KV, NSA_KV, NSA_KV, NSA_KV,
              NSA_KV, 3 * NSA_HEADS, NSA_WIDTH, MEM_WIDTH, MEM_WIDTH)

LANES = 128
SUBLANES = 8
QB = 128
VMEM_LIMIT_BYTES = 56 * 1024 * 1024

EVEN_ORDER = (("u", 0, 1024), ("g_s5", 1, 1024), ("q", 2, 1024), ("k", 3, 1024), ("v", 4, 1024),
              ("g_fox", 6, 1024), ("q_mem", 7, 512), ("g_mem", 8, 512), ("f", 5, 128))
ODD_ORDER = (("q_nsa", 4, 1024), ("g_nsa", 12, 1024), ("g_mla", 3, 1024), ("c_q", 0, 512), ("c_kv", 1, 512),
             ("q_mem", 13, 512), ("g_mem", 14, 512), ("k_cmp", 5, 256), ("v_cmp", 6, 256), ("k_slc", 7, 256),
             ("v_slc", 8, 256), ("k_win", 9, 256), ("v_win", 10, 256), ("k_rope", 2, 128), ("gates", 11, 128))


def _layout(order):
    off, out = 0, {}
    for name, _, width in order:
        assert off % width == 0
        out[name] = off
        off += width
    return out, off


EVEN_OFF, EVEN_N = _layout(EVEN_ORDER)
ODD_OFF, ODD_N = _layout(ODD_ORDER)


def _reorder_w_in(w, splits, order):
    starts = np.concatenate([[0], np.cumsum(splits)])
    cols = []
    for _, idx, width in order:
        seg = w[:, int(starts[idx]):int(starts[idx + 1])]
        pad = width - seg.shape[1]
        if pad:
            seg = jnp.pad(seg, ((0, 0), (0, pad)))
        cols.append(seg)
    return jnp.concatenate(cols, axis=1).astype(BF16)


def _cparams(*sem):
    return pltpu.CompilerParams(dimension_semantics=sem, vmem_limit_bytes=VMEM_LIMIT_BYTES)


def _silu(g):
    return g * jax.nn.sigmoid(g)


def _pick(n, cands):
    for c in cands:
        if n % c == 0:
            return c
    raise ValueError(f"no tile for {n} in {cands}")


def _norm_matmul_kernel(x_ref, g_ref, w_ref, o_ref, xn_ref):
    @pl.when(pl.program_id(1) == 0)
    def _():
        x = x_ref[...]
        ms = jnp.mean(x * x, axis=-1, keepdims=True)
        xn_ref[...] = (x * lax.rsqrt(ms + RMS_EPS) * g_ref[...]).astype(BF16)

    o_ref[...] = jnp.dot(xn_ref[...], w_ref[...], preferred_element_type=F32).astype(o_ref.dtype)


def _norm_matmul(x, g, w, *, x_cb=0, tm, tn, name):
    m = x.shape[0]
    k, n = w.shape
    return pl.pallas_call(
        _norm_matmul_kernel,
        grid=(m // tm, n // tn),
        in_specs=[pl.BlockSpec((tm, k), lambda i, j: (i, x_cb)),
                  pl.BlockSpec((1, k), lambda i, j: (0, 0)),
                  pl.BlockSpec((k, tn), lambda i, j: (0, j))],
        out_specs=pl.BlockSpec((tm, tn), lambda i, j: (i, j)),
        out_shape=jax.ShapeDtypeStruct((m, n), F32),
        scratch_shapes=[pltpu.VMEM((tm, k), BF16)],
        compiler_params=_cparams("parallel", "arbitrary"),
        name=name,
    )(x, g.reshape(1, k), w)


def _out_proj_kernel(h_ref, *refs):
    o_ref = refs[-1]
    n = (len(refs) - 1) // 2
    acc = h_ref[...]
    for y_ref, w_ref in zip(refs[:n], refs[n:2 * n]):
        acc = acc + jnp.dot(y_ref[...], w_ref[...], preferred_element_type=F32)
    o_ref[...] = acc


def _out_proj(h, ys, ws, *, tm, tn):
    m, n = h.shape
    in_specs = [pl.BlockSpec((tm, tn), lambda i, j: (i, j))]
    in_specs += [pl.BlockSpec((tm, y.shape[1]), lambda i, j: (i, 0)) for y in ys]
    in_specs += [pl.BlockSpec((w.shape[0], tn), lambda i, j: (0, j)) for w in ws]
    return pl.pallas_call(
        _out_proj_kernel,
        grid=(m // tm, n // tn),
        in_specs=in_specs,
        out_specs=pl.BlockSpec((tm, tn), lambda i, j: (i, j)),
        out_shape=jax.ShapeDtypeStruct((m, n), F32),
        compiler_params=_cparams("parallel", "arbitrary"),
        name="out_proj",
    )(h, *ys, *ws)


def _final_norm_kernel(x_ref, g_ref, o_ref):
    x = x_ref[...]
    ms = jnp.mean(x * x, axis=-1, keepdims=True)
    o_ref[...] = x * lax.rsqrt(ms + RMS_EPS) * g_ref[...]


def _final_norm(h, g, *, tm):
    m, n = h.shape
    return pl.pallas_call(
        _final_norm_kernel,
        grid=(m // tm,),
        in_specs=[pl.BlockSpec((tm, n), lambda i: (i, 0)), pl.BlockSpec((1, n), lambda i: (0, 0))],
        out_specs=pl.BlockSpec((tm, n), lambda i: (i, 0)),
        out_shape=jax.ShapeDtypeStruct((m, n), F32),
        compiler_params=_cparams("parallel"),
        name="final_norm",
    )(h, g.reshape(1, n))


def _flash_kernel(*refs, scale, t, has_decay):
    if has_decay:
        q_ref, k_ref, v_ref, g_ref, cq_ref, ck_ref, o_ref, m_ref, l_ref, acc_ref = refs
    else:
        q_ref, k_ref, v_ref, g_ref, o_ref, m_ref, l_ref, acc_ref = refs
    h = pl.program_id(0)
    qi = pl.program_id(1)
    q = q_ref[...].astype(BF16)
    m_ref[...] = jnp.full(m_ref.shape, NEG_INF, F32)
    l_ref[...] = jnp.zeros(l_ref.shape, F32)
    acc_ref[...] = jnp.zeros(acc_ref.shape, F32)
    if has_decay:
        lane = lax.broadcasted_iota(I32, cq_ref.shape, 1)
        ci = jnp.sum(jnp.where(lane == h, cq_ref[...], 0.0), axis=1, keepdims=True)

    def step(j, masked):
        start = pl.multiple_of(j * t, t)
        k = k_ref[pl.ds(start, t), :].astype(BF16)
        v = v_ref[pl.ds(start, t), :].astype(BF16)
        s = lax.dot_general(q, k, (((1,), (1,)), ((), ())), preferred_element_type=F32) * scale
        if has_decay:
            s = s + ci - ck_ref[pl.ds(h, 1), pl.ds(start, t)]
        if masked:
            row = lax.broadcasted_iota(I32, (t, t), 0)
            col = lax.broadcasted_iota(I32, (t, t), 1)
            s = jnp.where(col <= row, s, NEG_INF)
        m_prev = m_ref[...]
        m_new = jnp.maximum(m_prev, jnp.max(s, axis=1, keepdims=True))
        alpha = jnp.exp(m_prev - m_new)
        p = jnp.exp(s - m_new)
        l_ref[...] = alpha * l_ref[...] + jnp.sum(p, axis=1, keepdims=True)
        acc_ref[...] = alpha * acc_ref[...] + jnp.dot(p.astype(BF16), v, preferred_element_type=F32)
        m_ref[...] = m_new

    def body(j, carry):
        step(j, False)
        return carry

    lax.fori_loop(0, qi, body, 0)
    step(qi, True)
    o_ref[...] = (acc_ref[...] / l_ref[...] * _silu(g_ref[...])).astype(o_ref.dtype)


def _flash(q_arr, q_cb, k_arr, k_cb, v_arr, v_cb, g_arr, g_cb, *, heads, dk, dv, scale, t, decay=None, name):
    seq = q_arr.shape[0]
    in_specs = [pl.BlockSpec((t, dk), lambda h, i: (i, q_cb + h)),
                pl.BlockSpec((seq, dk), lambda h, i: (0, k_cb + h)),
                pl.BlockSpec((seq, dv), lambda h, i: (0, v_cb + h)),
                pl.BlockSpec((t, dv), lambda h, i: (i, g_cb + h))]
    args = [q_arr, k_arr, v_arr, g_arr]
    if decay is not None:
        cum, cum_t = decay
        in_specs += [pl.BlockSpec((t, LANES), lambda h, i: (i, 0)),
                     pl.BlockSpec((SUBLANES, seq), lambda h, i: (0, 0))]
        args += [cum, cum_t]
    return pl.pallas_call(
        functools.partial(_flash_kernel, scale=scale, t=t, has_decay=decay is not None),
        grid=(heads, seq // t),
        in_specs=in_specs,
        out_specs=pl.BlockSpec((t, dv), lambda h, i: (i, h)),
        out_shape=jax.ShapeDtypeStruct((seq, heads * dv), BF16),
        scratch_shapes=[pltpu.VMEM((t, 1), F32), pltpu.VMEM((t, 1), F32), pltpu.VMEM((t, dv), F32)],
        compiler_params=_cparams("parallel", "arbitrary"),
        name=name,
    )(*args)


def _decay_kernel(f_ref, b_ref, c_ref, ct_ref, carry_ref, *, t):
    i = pl.program_id(0)

    @pl.when(i == 0)
    def _():
        carry_ref[...] = jnp.zeros(carry_ref.shape, F32)

    x = f_ref[...] + b_ref[...]
    lf = jnp.minimum(x, 0.0) - jnp.log1p(jnp.exp(-jnp.abs(x)))
    row = lax.broadcasted_iota(I32, lf.shape, 0)
    s = 1
    while s < t:
        lf = lf + jnp.where(row >= s, pltpu.roll(lf, s, 0), 0.0)
        s *= 2
    lf = lf + carry_ref[...]
    carry_ref[...] = lf[t - 1:t, :]
    c_ref[...] = lf
    ct_ref[...] = lf.T[:SUBLANES, :]


def _decay(proj, f_cb, b_f, *, t):
    seq = proj.shape[0]
    b = jnp.pad(b_f.reshape(1, FOX_HEADS), ((0, 0), (0, LANES - FOX_HEADS)))
    return pl.pallas_call(
        functools.partial(_decay_kernel, t=t),
        grid=(seq // t,),
        in_specs=[pl.BlockSpec((t, LANES), lambda i: (i, f_cb)), pl.BlockSpec((1, LANES), lambda i: (0, 0))],
        out_specs=[pl.BlockSpec((t, LANES), lambda i: (i, 0)), pl.BlockSpec((SUBLANES, t), lambda i: (0, i))],
        out_shape=[jax.ShapeDtypeStruct((seq, LANES), F32), jax.ShapeDtypeStruct((SUBLANES, seq), F32)],
        scratch_shapes=[pltpu.VMEM((1, LANES), F32)],
        compiler_params=_cparams("arbitrary"),
        name="fox_decay",
    )(proj, b)


def _mem_attn_kernel(q_ref, k_ref, v_ref, g_ref, o_ref):
    q = q_ref[...].astype(BF16)
    k = k_ref[...].astype(BF16)
    s = lax.dot_general(q, k, (((1,), (1,)), ((), ())), preferred_element_type=F32) * (MEM_HEAD_DIM ** -0.5)
    m = jnp.max(s, axis=1, keepdims=True)
    p = jnp.exp(s - m)
    l = jnp.sum(p, axis=1, keepdims=True)
    o = jnp.dot(p.astype(BF16), v_ref[...].astype(BF16), preferred_element_type=F32) / l
    o_ref[...] = (o * _silu(g_ref[...])).astype(o_ref.dtype)


def _mem_attn(proj, q_cb, g_cb, mem_kv, *, t):
    seq = proj.shape[0]
    nm = mem_kv.shape[0]
    d = MEM_HEAD_DIM
    return pl.pallas_call(
        _mem_attn_kernel,
        grid=(MEM_HEADS, seq // t),
        in_specs=[pl.BlockSpec((t, d), lambda h, i: (i, q_cb + h)),
                  pl.BlockSpec((nm, d), lambda h, i: (0, h)),
                  pl.BlockSpec((nm, d), lambda h, i: (0, MEM_HEADS + h)),
                  pl.BlockSpec((t, d), lambda h, i: (i, g_cb + h))],
        out_specs=pl.BlockSpec((t, d), lambda h, i: (i, h)),
        out_shape=jax.ShapeDtypeStruct((seq, MEM_WIDTH), BF16),
        compiler_params=_cparams("parallel", "arbitrary"),
        name="mem_attn",
    )(proj, mem_kv, mem_kv, proj)


S5_TILE_GROUPS = LANES // S5_GROUP
S5_TILE_STATES = S5_TILE_GROUPS * S5_STATE
S5_TILES = S5_GROUPS // S5_TILE_GROUPS


def _s5_scan_kernel(u_ref, b_ref, c_ref, d_ref, tab_ref, z_ref, bu_ref, carry_ref, *, tc):
    ns = S5_TILE_STATES

    @pl.when(pl.program_id(1) == 0)
    def _():
        carry_ref[...] = jnp.zeros(carry_ref.shape, F32)

    u = u_ref[...]
    bu_ref[...] = jnp.dot(u.astype(BF16), b_ref[...], preferred_element_type=F32)
    steps = [(1, tab_ref[0], tab_ref[1]), (2, tab_ref[2], tab_ref[3]), (4, tab_ref[4], tab_ref[5])]
    pr = tab_ref[6]
    pi = tab_ref[7]

    def body(i, carry):
        cr, ci = carry
        r0 = pl.multiple_of(i * SUBLANES, SUBLANES)
        xr = bu_ref[pl.ds(r0, SUBLANES), 0:ns]
        xi = bu_ref[pl.ds(r0, SUBLANES), ns:2 * ns]
        for s, ar, ai in steps:
            sr = pltpu.roll(xr, s, 0)
            si = pltpu.roll(xi, s, 0)
            xr, xi = xr + ar * sr - ai * si, xi + ar * si + ai * sr
        xr, xi = xr + pr * cr - pi * ci, xi + pr * ci + pi * cr
        bu_ref[pl.ds(r0, SUBLANES), 0:ns] = xr
        bu_ref[pl.ds(r0, SUBLANES), ns:2 * ns] = xi
        return xr[SUBLANES - 1:SUBLANES, :], xi[SUBLANES - 1:SUBLANES, :]

    cr, ci = lax.fori_loop(0, tc // SUBLANES, body, (carry_ref[0:1, 0:ns], carry_ref[0:1, ns:2 * ns]))
    carry_ref[0:1, 0:ns] = cr
    carry_ref[0:1, ns:2 * ns] = ci
    y = jnp.dot(bu_ref[...].astype(BF16), c_ref[...], preferred_element_type=F32) + d_ref[...] * u
    z_ref[...] = jax.nn.gelu(y)


def _s5_prepare(lam_re, lam_im, log_dt, b_re, b_im, c_re, c_im):
    dt = jnp.exp(log_dt.astype(F32))[:, None]
    lr = lam_re.astype(F32)
    li = lam_im.astype(F32)
    mag = jnp.exp(lr * dt)
    ab_re = mag * jnp.cos(li * dt)
    ab_im = mag * jnp.sin(li * dt)
    den = lr * lr + li * li
    nr = ab_re - 1.0
    f_re = (nr * lr + ab_im * li) / den
    f_im = (ab_im * lr - nr * li) / den
    br = b_re.astype(F32)
    bim = b_im.astype(F32)
    bb_re = f_re[..., None] * br - f_im[..., None] * bim
    bb_im = f_re[..., None] * bim + f_im[..., None] * br
    eye = jnp.eye(S5_TILE_GROUPS, dtype=F32)

    def blockdiag_in(bb):
        t = bb.reshape(S5_TILES, S5_TILE_GROUPS, S5_STATE, S5_GROUP)
        m = jnp.einsum("jgpc,gh->jgchp", t, eye)
        return m.reshape(S5_TILES, LANES, S5_TILE_STATES)

    def blockdiag_out(cc):
        t = cc.reshape(S5_TILES, S5_TILE_GROUPS, S5_GROUP, S5_STATE)
        m = jnp.einsum("jgcp,gh->jgphc", t, eye)
        return m.reshape(S5_TILES, S5_TILE_STATES, LANES)

    b_cat = jnp.concatenate([blockdiag_in(bb_re), blockdiag_in(bb_im)], axis=2).astype(BF16)
    c_cat = jnp.concatenate([blockdiag_out(c_re.astype(F32)), -blockdiag_out(c_im.astype(F32))], axis=1).astype(BF16)

    a_r = ab_re.reshape(S5_TILES, 1, S5_TILE_STATES)
    a_i = ab_im.reshape(S5_TILES, 1, S5_TILE_STATES)

    def cmul(xr, xi, yr, yi):
        return xr * yr - xi * yi, xr * yi + xi * yr

    a2 = cmul(a_r, a_i, a_r, a_i)
    a4 = cmul(*a2, *a2)
    row = jnp.arange(SUBLANES)[None, :, None]
    tabs = []
    for s, (pr_, pi_) in ((1, (a_r, a_i)), (2, a2), (4, a4)):
        tabs.append(jnp.where(row >= s, pr_, 0.0))
        tabs.append(jnp.where(row >= s, pi_, 0.0))
    pw = [(a_r, a_i)]
    for _ in range(SUBLANES - 1):
        pw.append(cmul(*pw[-1], a_r, a_i))
    tabs.append(jnp.concatenate([p[0] for p in pw], axis=1))
    tabs.append(jnp.concatenate([p[1] for p in pw], axis=1))
    tab = jnp.stack([jnp.broadcast_to(t, (S5_TILES, SUBLANES, S5_TILE_STATES)) for t in tabs], axis=1)
    return b_cat, c_cat, tab.astype(F32)


def _s5_scan(proj, u_cb, b_cat, c_cat, d_skip, tab, *, tc):
    seq = proj.shape[0]
    ns = S5_TILE_STATES
    d = d_skip.astype(F32).reshape(S5_TILES, 1, LANES)
    return pl.pallas_call(
        functools.partial(_s5_scan_kernel, tc=tc),
        grid=(S5_TILES, seq // tc),
        in_specs=[pl.BlockSpec((tc, LANES), lambda j, c: (c, u_cb + j)),
                  pl.BlockSpec((None, LANES, 2 * ns), lambda j, c: (j, 0, 0)),
                  pl.BlockSpec((None, 2 * ns, LANES), lambda j, c: (j, 0, 0)),
                  pl.BlockSpec((None, 1, LANES), lambda j, c: (j, 0, 0)),
                  pl.BlockSpec((None, 8, SUBLANES, ns), lambda j, c: (j, 0, 0, 0))],
        out_specs=pl.BlockSpec((tc, LANES), lambda j, c: (c, j)),
        out_shape=jax.ShapeDtypeStruct((seq, S5_WIDTH), F32),
        scratch_shapes=[pltpu.VMEM((tc, 2 * ns), F32), pltpu.VMEM((SUBLANES, 2 * ns), F32)],
        compiler_params=_cparams("parallel", "arbitrary"),
        name="s5_scan",
    )(proj, b_cat, c_cat, d, tab)


def _s5_glu_kernel(z_ref, w_ref, g_ref, o_ref, *, tn):
    j = pl.program_id(1)
    z = z_ref[...]
    a = jnp.dot(z.astype(BF16), w_ref[...], preferred_element_type=F32)
    zc = z_ref[:, pl.ds(pl.multiple_of(j * tn, tn), tn)]
    o_ref[...] = (zc * jax.nn.sigmoid(a) * _silu(g_ref[...])).astype(o_ref.dtype)


def _s5_glu(z, w_glu, proj, g_cb, *, tm, tn):
    seq, n = z.shape
    return pl.pallas_call(
        functools.partial(_s5_glu_kernel, tn=tn),
        grid=(seq // tm, n // tn),
        in_specs=[pl.BlockSpec((tm, n), lambda i, j: (i, 0)),
                  pl.BlockSpec((n, tn), lambda i, j: (0, j)),
                  pl.BlockSpec((tm, tn), lambda i, j: (i, g_cb * (n // tn) + j))],
        out_specs=pl.BlockSpec((tm, tn), lambda i, j: (i, j)),
        out_shape=jax.ShapeDtypeStruct((seq, n), BF16),
        compiler_params=_cparams("parallel", "arbitrary"),
        name="s5_glu",
    )(z, w_glu, proj)


def _rope_tables(pos, freq):
    ang = pos * freq
    lane = lax.broadcasted_iota(I32, ang.shape, 1)
    half = MLA_ROPE // 2
    cos = jnp.cos(ang)
    sin = jnp.sin(ang)
    c = jnp.where(lane < MLA_ROPE, cos, 0.0)
    s1 = jnp.where(lane < half, -sin, 0.0)
    s2 = jnp.where((lane >= half) & (lane < MLA_ROPE), sin, 0.0)
    return c, s1, s2


def _rope_apply(x, c, s1, s2):
    half = MLA_ROPE // 2
    return x * c + pltpu.roll(x, LANES - half, 1) * s1 + pltpu.roll(x, half, 1) * s2


def _mla_prep_kernel(qf_ref, kvf_ref, kr_ref, pos_ref, freq_ref, q_ref, k_ref, v_ref):
    c, s1, s2 = _rope_tables(pos_ref[...], freq_ref[...])
    kr = _rope_apply(kr_ref[...], c, s1, s2).astype(BF16)
    for h in range(MLA_HEADS):
        b = 2 * LANES * h
        q_ref[:, b:b + LANES] = qf_ref[:, b:b + LANES].astype(BF16)
        q_ref[:, b + LANES:b + 2 * LANES] = _rope_apply(qf_ref[:, b + LANES:b + 2 * LANES], c, s1, s2).astype(BF16)
        k_ref[:, b:b + LANES] = kvf_ref[:, b:b + LANES].astype(BF16)
        k_ref[:, b + LANES:b + 2 * LANES] = kr
        v_ref[:, LANES * h:LANES * (h + 1)] = kvf_ref[:, b + LANES:b + 2 * LANES].astype(BF16)


def _mla_prep(qf, kvf, proj, kr_cb, pos_col, freq, *, t):
    seq = qf.shape[0]
    w = 2 * LANES * MLA_HEADS
    return pl.pallas_call(
        _mla_prep_kernel,
        grid=(seq // t,),
        in_specs=[pl.BlockSpec((t, w), lambda i: (i, 0)),
                  pl.BlockSpec((t, w), lambda i: (i, 0)),
                  pl.BlockSpec((t, LANES), lambda i: (i, kr_cb)),
                  pl.BlockSpec((t, 1), lambda i: (i, 0)),
                  pl.BlockSpec((1, LANES), lambda i: (0, 0))],
        out_specs=[pl.BlockSpec((t, w), lambda i: (i, 0)),
                   pl.BlockSpec((t, w), lambda i: (i, 0)),
                   pl.BlockSpec((t, MLA_WIDTH), lambda i: (i, 0))],
        out_shape=[jax.ShapeDtypeStruct((seq, w), BF16), jax.ShapeDtypeStruct((seq, w), BF16),
                   jax.ShapeDtypeStruct((seq, MLA_WIDTH), BF16)],
        compiler_params=_cparams("parallel"),
        name="mla_prep",
    )(qf, kvf, proj, pos_col, freq)


def _t5_bucket(dist):
    n = jnp.maximum(dist, 0)
    max_exact = T5_BUCKETS // 2
    log_ratio = jnp.log(jnp.maximum(n, 1).astype(F32) / max_exact) / math.log(T5_MAX_DIST / max_exact)
    large = jnp.minimum(max_exact + (log_ratio * (T5_BUCKETS - max_exact)).astype(I32), T5_BUCKETS - 1)
    return jnp.where(n < max_exact, n, large)


def _t5_lookup(table_row, bucket):
    tab = jnp.broadcast_to(table_row, bucket.shape)
    return jnp.take_along_axis(tab, bucket, axis=1)


def _nsa_cmp_kernel(x_ref, pe_ref, w1_ref, w2_ref, o_ref, *, nc):
    half = NSA_CMP_LEN // 2
    d = NSA_HEAD_DIM
    u = jnp.zeros((nc, NSA_CMP_HIDDEN), F32)
    v = jnp.zeros((nc, NSA_CMP_HIDDEN), F32)
    for r in range(half):
        a = x_ref[pl.ds(r, nc, stride=NSA_CMP_STRIDE), :]
        u = u + jnp.dot((a + pe_ref[r:r + 1, :]).astype(BF16), w1_ref[r * d:(r + 1) * d, :],
                        preferred_element_type=F32)
        v = v + jnp.dot((a + pe_ref[half + r:half + r + 1, :]).astype(BF16),
                        w1_ref[(half + r) * d:(half + r + 1) * d, :], preferred_element_type=F32)
    hid = u + pltpu.roll(v, nc - 1, 0)
    o_ref[...] = jnp.dot(jax.nn.gelu(hid).astype(BF16), w2_ref[...], preferred_element_type=F32).astype(o_ref.dtype)


def _nsa_compress(proj, k_cb, pe, w1, w2):
    seq = proj.shape[0]
    nc = seq // NSA_CMP_STRIDE
    d = NSA_HEAD_DIM
    g = NSA_KV_GROUPS
    return pl.pallas_call(
        functools.partial(_nsa_cmp_kernel, nc=nc),
        grid=(2, g),
        in_specs=[pl.BlockSpec((seq, d), lambda a, b: (0, k_cb + a * g + b)),
                  pl.BlockSpec((None, NSA_CMP_LEN, d), lambda a, b: (a, 0, 0)),
                  pl.BlockSpec((None, NSA_CMP_LEN * d, NSA_CMP_HIDDEN), lambda a, b: (a, 0, 0)),
                  pl.BlockSpec((None, NSA_CMP_HIDDEN, d), lambda a, b: (a, 0, 0))],
        out_specs=pl.BlockSpec((None, None, nc, d), lambda a, b: (a, b, 0, 0)),
        out_shape=jax.ShapeDtypeStruct((2, g, nc, d), BF16),
        compiler_params=_cparams("parallel", "arbitrary"),
        name="nsa_compress",
    )(proj, pe, w1, w2)


def _nsa_select_kernel(q_ref, kc_ref, vc_ref, posq_ref, posc_ref, tab_ref, gate_ref, ov_ref,
                       oc_ref, sel_ref, *, nc, n_slc):
    qi = pl.program_id(0)
    d = NSA_HEAD_DIM
    scale = d ** -0.5
    t = qi * QB + lax.broadcasted_iota(I32, (QB, 1), 0)
    n_idx = lax.broadcasted_iota(I32, (1, nc), 1)
    cmp_end = n_idx * NSA_CMP_STRIDE + (NSA_CMP_LEN - 1)
    valid = cmp_end <= t
    dist = posq_ref[...] - posc_ref[...]
    buckets = [_t5_bucket(dist[:, c * LANES:(c + 1) * LANES]) for c in range(nc // LANES)]
    gates = jax.nn.sigmoid(gate_ref[...])
    ov = ov_ref[...]
    jl = lax.broadcasted_iota(I32, (QB, LANES), 1)
    cur = t // NSA_SLC_BLOCK
    forced = (jl == 0) | (jl == cur) | (jl == cur - 1)
    js = lax.broadcasted_iota(I32, (LANES, QB), 0).astype(F32)
    for g in range(NSA_KV_GROUPS):
        kc = kc_ref[g]
        vc = vc_ref[g]
        psum = jnp.zeros((QB, nc), F32)
        for r in range(NSA_REP):
            h = g * NSA_REP + r
            q = q_ref[:, h * d:(h + 1) * d].astype(BF16)
            s = lax.dot_general(q, kc, (((1,), (1,)), ((), ())), preferred_element_type=F32) * scale
            bias = jnp.concatenate([_t5_lookup(tab_ref[h:h + 1, :], b) for b in buckets], axis=1)
            s = jnp.where(valid, s + bias, NEG_INF)
            m = jnp.max(s, axis=1, keepdims=True)
            e = jnp.exp(s - m)
            p = jnp.where(valid, e / jnp.sum(e, axis=1, keepdims=True), 0.0)
            o = jnp.dot(p.astype(BF16), vc, preferred_element_type=F32)
            oc_ref[:, h * d:(h + 1) * d] = gates[:, 3 * h:3 * h + 1] * o
            psum = psum + p
        p_hi = psum.astype(BF16)
        p_lo = (psum - p_hi.astype(F32)).astype(BF16)
        imp = jnp.dot(p_hi, ov, preferred_element_type=F32) + jnp.dot(p_lo, ov, preferred_element_type=F32)
        score = jnp.where(forced, FORCE_SCORE, jnp.where(jl > cur, -1.0, imp))
        score = jnp.where(jl < n_slc, score, -2.0)
        st = score.T
        sel = jnp.zeros((LANES, QB), F32)
        for _ in range(NSA_SLC_TOPK):
            mx = jnp.max(st, axis=0, keepdims=True)
            first = jnp.min(jnp.where(st == mx, js, float(LANES)), axis=0, keepdims=True)
            hit = js == first
            sel = jnp.where(hit, 1.0, sel)
            st = jnp.where(hit, -3e38, st)
        sel_ref[g] = sel.T.astype(sel_ref.dtype)


def _nsa_select(proj, q_cb, gate_cb, kc, vc, pos_col, pos_cmp, tab_t, ov, *, n_slc):
    seq = proj.shape[0]
    nc = kc.shape[1]
    g = NSA_KV_GROUPS
    d = NSA_HEAD_DIM
    return pl.pallas_call(
        functools.partial(_nsa_select_kernel, nc=nc, n_slc=n_slc),
        grid=(seq // QB,),
        in_specs=[pl.BlockSpec((QB, NSA_WIDTH), lambda i: (i, q_cb)),
                  pl.BlockSpec((g, nc, d), lambda i: (0, 0, 0)),
                  pl.BlockSpec((g, nc, d), lambda i: (0, 0, 0)),
                  pl.BlockSpec((QB, 1), lambda i: (i, 0)),
                  pl.BlockSpec((1, nc), lambda i: (0, 0)),
                  pl.BlockSpec((SUBLANES, LANES), lambda i: (0, 0)),
                  pl.BlockSpec((QB, LANES), lambda i: (i, gate_cb)),
                  pl.BlockSpec((nc, LANES), lambda i: (0, 0))],
        out_specs=[pl.BlockSpec((QB, NSA_WIDTH), lambda i: (i, 0)),
                   pl.BlockSpec((g, QB, LANES), lambda i: (0, i, 0))],
        out_shape=[jax.ShapeDtypeStruct((seq, NSA_WIDTH), F32),
                   jax.ShapeDtypeStruct((g, seq, LANES), BF16)],
        compiler_params=_cparams("parallel"),
        name="nsa_select",
    )(proj, kc, vc, pos_col, pos_cmp, tab_t, proj, ov)


def _nsa_slc_kernel(q_ref, k_ref, v_ref, sel_ref, posq_ref, posk_ref, tab_ref, o_ref, m_ref, l_ref, acc_ref):
    g = pl.program_id(0)
    qi = pl.program_id(1)
    d = NSA_HEAD_DIM
    scale = d ** -0.5
    sel = sel_ref[...]
    pos_q = posq_ref[...]
    m_ref[...] = jnp.full(m_ref.shape, NEG_INF, F32)
    l_ref[...] = jnp.zeros(l_ref.shape, F32)
    acc_ref[...] = jnp.zeros(acc_ref.shape, F32)
    blk = lax.broadcasted_iota(I32, (LANES, QB), 0)
    key_blk = lax.broadcasted_iota(I32, (LANES, QB), 1) // NSA_SLC_BLOCK
    row = lax.broadcasted_iota(I32, (QB, QB), 0)
    col = lax.broadcasted_iota(I32, (QB, QB), 1)
    per_tile = QB // NSA_SLC_BLOCK

    def step(j, diagonal):
        start = pl.multiple_of(j * QB, QB)
        k = k_ref[pl.ds(start, QB), :].astype(BF16)
        v = v_ref[pl.ds(start, QB), :].astype(BF16)
        expand = jnp.where(blk == key_blk + j * per_tile, 1.0, 0.0).astype(BF16)
        mask = jnp.dot(sel, expand, preferred_element_type=F32) > 0.5
        if diagonal:
            mask = mask & (col <= row)
        bucket = _t5_bucket(pos_q - posk_ref[:, pl.ds(start, QB)])
        for r in range(NSA_REP):
            q = q_ref[:, r * d:(r + 1) * d].astype(BF16)
            s = lax.dot_general(q, k, (((1,), (1,)), ((), ())), preferred_element_type=F32) * scale
            s = s + _t5_lookup(tab_ref[pl.ds(g * NSA_REP + r, 1), :], bucket)
            s = jnp.where(mask, s, NEG_INF)
            m_prev = m_ref[r]
            m_new = jnp.maximum(m_prev, jnp.max(s, axis=1, keepdims=True))
            alpha = jnp.exp(m_prev - m_new)
            p = jnp.exp(s - m_new)
            l_ref[r] = alpha * l_ref[r] + jnp.sum(p, axis=1, keepdims=True)
            acc_ref[r] = alpha * acc_ref[r] + jnp.dot(p.astype(BF16), v, preferred_element_type=F32)
            m_ref[r] = m_new

    def body(j, carry):
        step(j, False)
        return carry

    lax.fori_loop(0, qi, body, 0)
    step(qi, True)
    for r in range(NSA_REP):
        o_ref[:, r * d:(r + 1) * d] = acc_ref[r] / l_ref[r]


def _nsa_slc(proj, q_cb, k_cb, v_cb, sel, pos_col, pos_row, tab_t):
    seq = proj.shape[0]
    d = NSA_HEAD_DIM
    gw = NSA_REP * d
    return pl.pallas_call(
        _nsa_slc_kernel,
        grid=(NSA_KV_GROUPS, seq // QB),
        in_specs=[pl.BlockSpec((QB, gw), lambda g, i: (i, q_cb + g)),
                  pl.BlockSpec((seq, d), lambda g, i: (0, k_cb + g)),
                  pl.BlockSpec((seq, d), lambda g, i: (0, v_cb + g)),
                  pl.BlockSpec((None, QB, LANES), lambda g, i: (g, i, 0)),
                  pl.BlockSpec((QB, 1), lambda g, i: (i, 0)),
                  pl.BlockSpec((1, seq), lambda g, i: (0, 0)),
                  pl.BlockSpec((SUBLANES, LANES), lambda g, i: (0, 0))],
        out_specs=pl.BlockSpec((QB, gw), lambda g, i: (i, g)),
        out_shape=jax.ShapeDtypeStruct((seq, NSA_WIDTH), F32),
        scratch_shapes=[pltpu.VMEM((NSA_REP, QB, 1), F32), pltpu.VMEM((NSA_REP, QB, 1), F32),
                        pltpu.VMEM((NSA_REP, QB, d), F32)],
        compiler_params=_cparams("parallel", "arbitrary"),
        name="nsa_slc",
    )(proj, proj, proj, sel, pos_col, pos_row, tab_t)


NSA_WIN_TILES = NSA_WINDOW // QB + 1


def _nsa_win_kernel(*refs):
    nt = NSA_WIN_TILES
    q_ref = refs[0]
    k_refs = refs[1:1 + nt]
    v_refs = refs[1 + nt:1 + 2 * nt]
    pk_refs = refs[1 + 2 * nt:1 + 3 * nt]
    posq_ref, tab_ref, gate_ref, gout_ref, oc_ref, os_ref, o_ref = refs[1 + 3 * nt:]
    qi = pl.program_id(0)
    d = NSA_HEAD_DIM
    scale = d ** -0.5
    t = qi * QB + lax.broadcasted_iota(I32, (QB, 1), 0)
    pos_q = posq_ref[...]
    lane = lax.broadcasted_iota(I32, (1, QB), 1)
    bands, buckets = [], []
    for jj in range(nt):
        kidx = (qi - (nt - 1) + jj) * QB + lane
        bands.append((kidx >= 0) & (kidx <= t) & (t - kidx < NSA_WINDOW))
        buckets.append(_t5_bucket(pos_q - pk_refs[jj][...]))
    band = jnp.concatenate(bands, axis=1)
    gates = jax.nn.sigmoid(gate_ref[...])
    for g in range(NSA_KV_GROUPS):
        k = jnp.concatenate([kr[:, g * d:(g + 1) * d].astype(BF16) for kr in k_refs], axis=0)
        v = jnp.concatenate([vr[:, g * d:(g + 1) * d].astype(BF16) for vr in v_refs], axis=0)
        for r in range(NSA_REP):
            h = g * NSA_REP + r
            hs = slice(h * d, (h + 1) * d)
            q = q_ref[:, hs].astype(BF16)
            s = lax.dot_general(q, k, (((1,), (1,)), ((), ())), preferred_element_type=F32) * scale
            bias = jnp.concatenate([_t5_lookup(tab_ref[h:h + 1, :], b) for b in buckets], axis=1)
            s = jnp.where(band, s + bias, NEG_INF)
            m = jnp.max(s, axis=1, keepdims=True)
            p = jnp.exp(s - m)
            o_w = jnp.dot(p.astype(BF16), v, preferred_element_type=F32) / jnp.sum(p, axis=1, keepdims=True)
            o = oc_ref[:, hs] + gates[:, 3 * h + 1:3 * h + 2] * os_ref[:, hs] + gates[:, 3 * h + 2:3 * h + 3] * o_w
            o_ref[:, hs] = (o * _silu(gout_ref[:, hs])).astype(o_ref.dtype)


def _nsa_win(proj, q_cb, k_cb, v_cb, gate_cb, gout_cb, oc, o_s, pos_col, pos_row, tab_t):
    seq = proj.shape[0]
    nt = NSA_WIN_TILES

    def band_rows(jj, cb):
        return pl.BlockSpec((QB, NSA_KV), lambda i: (jnp.maximum(i - (nt - 1) + jj, 0), cb))

    def band_pos(jj):
        return pl.BlockSpec((1, QB), lambda i: (0, jnp.maximum(i - (nt - 1) + jj, 0)))

    in_specs = [pl.BlockSpec((QB, NSA_WIDTH), lambda i: (i, q_cb))]
    in_specs += [band_rows(jj, k_cb) for jj in range(nt)]
    in_specs += [band_rows(jj, v_cb) for jj in range(nt)]
    in_specs += [band_pos(jj) for jj in range(nt)]
    in_specs += [pl.BlockSpec((QB, 1), lambda i: (i, 0)),
                 pl.BlockSpec((SUBLANES, LANES), lambda i: (0, 0)),
                 pl.BlockSpec((QB, LANES), lambda i: (i, gate_cb)),
                 pl.BlockSpec((QB, NSA_WIDTH), lambda i: (i, gout_cb)),
                 pl.BlockSpec((QB, NSA_WIDTH), lambda i: (i, 0)),
                 pl.BlockSpec((QB, NSA_WIDTH), lambda i: (i, 0))]
    args = [proj] + [proj] * nt + [proj] * nt + [pos_row] * nt + [pos_col, tab_t, proj, proj, oc, o_s]
    return pl.pallas_call(
        _nsa_win_kernel,
        grid=(seq // QB,),
        in_specs=in_specs,
        out_specs=pl.BlockSpec((QB, NSA_WIDTH), lambda i: (i, 0)),
        out_shape=jax.ShapeDtypeStruct((seq, NSA_WIDTH), BF16),
        compiler_params=_cparams("parallel"),
        name="nsa_win",
    )(*args)


def _even_layer(h, mem_kv, norm_g, w_in, s5, w_glu, b_f, tiles):
    proj = _norm_matmul(h, norm_g, w_in, tm=tiles["tm"], tn=_pick(EVEN_N, (512, 384, 256, 128)), name="in_proj_even")
    off = EVEN_OFF
    b_cat, c_cat, tab, d_skip = s5
    z = _s5_scan(proj, off["u"] // LANES, b_cat, c_cat, d_skip, tab, tc=tiles["s5_tc"])
    y_s5 = _s5_glu(z, w_glu, proj, off["g_s5"] // S5_WIDTH, tm=tiles["tm"], tn=512)
    decay = _decay(proj, off["f"] // LANES, b_f, t=tiles["decay_t"])
    d = FOX_HEAD_DIM
    y_fox = _flash(proj, off["q"] // d, proj, off["k"] // d, proj, off["v"] // d, proj, off["g_fox"] // d,
                   heads=FOX_HEADS, dk=d, dv=d, scale=d ** -0.5, t=tiles["attn_t"], decay=decay, name="fox_attn")
    y_mem = _mem_attn(proj, off["q_mem"] // MEM_HEAD_DIM, off["g_mem"] // MEM_HEAD_DIM, mem_kv, t=tiles["mem_t"])
    return y_s5, y_fox, y_mem


def _odd_layer(h, mem_kv, norm_g, w_in, mla, nsa, pos, tiles):
    proj = _norm_matmul(h, norm_g, w_in, tm=tiles["tm"], tn=_pick(ODD_N, (512, 384, 256, 128)), name="in_proj_odd")
    off = ODD_OFF
    g_cq, g_ckv, w_uq, w_ukv, freq = mla
    pos_col, pos_col_f, pos_row, pos_cmp = pos
    qf =_norm_matmul(proj, g_cq, w_uq, x_cb=off["c_q"] // MLA_Q_RANK, tm=tiles["tm"], tn=512, name="mla_q_up")
    kvf = _norm_matmul(proj, g_ckv, w_ukv, x_cb=off["c_kv"] // MLA_KV_RANK, tm=tiles["tm"], tn=512, name="mla_kv_up")
    q_r, k_r, v_r = _mla_prep(qf, kvf, proj, off["k_rope"] // LANES, pos_col_f, freq, t=tiles["prep_t"])
    y_mla = _flash(q_r, 0, k_r, 0, v_r, 0, proj, off["g_mla"] // MLA_V, heads=MLA_HEADS, dk=2 * LANES, dv=MLA_V,
                   scale=(MLA_NOPE + MLA_ROPE) ** -0.5, t=tiles["attn_t"], name="mla_attn")
    pe, w1, w2, tab_t, ov, n_slc = nsa
    kvc = _nsa_compress(proj, off["k_cmp"] // NSA_HEAD_DIM, pe, w1, w2)
    oc, sel = _nsa_select(proj, off["q_nsa"] // NSA_WIDTH, off["gates"] // LANES, kvc[0], kvc[1], pos_col, pos_cmp,
                          tab_t, ov, n_slc=n_slc)
    gw = NSA_REP * NSA_HEAD_DIM
    o_s = _nsa_slc(proj, off["q_nsa"] // gw, off["k_slc"] // NSA_HEAD_DIM, off["v_slc"] // NSA_HEAD_DIM, sel,
                   pos_col, pos_row, tab_t)
    y_nsa = _nsa_win(proj, off["q_nsa"] // NSA_WIDTH, off["k_win"] // NSA_KV, off["v_win"] // NSA_KV,
                     off["gates"] // LANES, off["g_nsa"] // NSA_WIDTH, oc, o_s, pos_col, pos_row, tab_t)
    y_mem = _mem_attn(proj, off["q_mem"] // MEM_HEAD_DIM, off["g_mem"] // MEM_HEAD_DIM, mem_kv, t=tiles["mem_t"])
    return y_mla, y_nsa, y_mem


def _tiles(seq):
    return {"tm": min(seq, 1024), "s5_tc": min(seq, 1024), "decay_t": min(seq, 512), "attn_t": min(seq, 256),
            "mem_t": min(seq, 512), "prep_t": min(seq, 256)}


def _context(positions, t5_table, seq):
    pos = positions[0]
    pos_col = pos.reshape(seq, 1)
    pos_row = pos.reshape(1, seq)
    nc = seq // NSA_CMP_STRIDE
    pos_cmp = jnp.pad(pos[NSA_CMP_LEN - 1::NSA_CMP_STRIDE], (0, 1)).reshape(1, nc)
    half = MLA_ROPE // 2
    inv_freq = ROPE_THETA ** (-jnp.arange(half, dtype=F32) / half)
    freq = jnp.concatenate([inv_freq, inv_freq, jnp.zeros((LANES - MLA_ROPE,), F32)]).reshape(1, LANES)
    tab_t = jnp.pad(t5_table.astype(F32).T, ((0, SUBLANES - NSA_HEADS), (0, LANES - T5_BUCKETS)))
    n_slc = seq // NSA_SLC_BLOCK
    cs = np.arange(nc) * NSA_CMP_STRIDE
    ss = np.arange(LANES) * NSA_SLC_BLOCK
    ov_np = np.clip(np.minimum(cs[:, None] + NSA_CMP_LEN, ss[None, :] + NSA_SLC_BLOCK)
                    - np.maximum(cs[:, None], ss[None, :]), 0, None) / NSA_CMP_LEN
    ov_np[nc - 1, :] = 0.0
    ov_np[:, n_slc:] = 0.0
    return {"pos": (pos_col, pos_col.astype(F32), pos_row, pos_cmp), "freq": freq, "tab_t": tab_t,
            "ov": jnp.asarray(ov_np, BF16), "n_slc": n_slc}


def _odd_params(i, mla_g_cq, mla_g_ckv, mla_w_uq, mla_w_ukv, nsa_cmp_pe, nsa_cmp_w1, nsa_cmp_w2, ctx):
    dq = MLA_NOPE + MLA_ROPE
    w_uq = mla_w_uq[i].reshape(MLA_Q_RANK, MLA_HEADS, dq)
    w_uq = jnp.pad(w_uq, ((0, 0), (0, 0), (0, 2 * LANES - dq))).reshape(MLA_Q_RANK, -1).astype(BF16)
    mla = (mla_g_cq[i], mla_g_ckv[i], w_uq, mla_w_ukv[i].astype(BF16), ctx["freq"])
    nsa = (nsa_cmp_pe[i].astype(F32), nsa_cmp_w1[i].astype(BF16), nsa_cmp_w2[i].astype(BF16), ctx["tab_t"],
           ctx["ov"], ctx["n_slc"])
    return mla, nsa


def kernel(x, mem, positions, norm_g, mem_norm_g, final_norm_g, t5_table, w_out, mem_w_kv, even_w_in, s5_lam_re,
           s5_lam_im, s5_log_dt, s5_b_re, s5_b_im, s5_c_re, s5_c_im, s5_d, s5_w_glu, fox_b_f, odd_w_in, mla_g_cq,
           mla_g_ckv, mla_w_uq, mla_w_ukv, nsa_cmp_pe, nsa_cmp_w1, nsa_cmp_w2):
    batch, seq, _ = x.shape
    assert batch == 1 and seq % 1024 == 0 and seq // NSA_SLC_BLOCK <= LANES
    depth = norm_g.shape[0]
    tiles = _tiles(seq)
    ctx = _context(positions, t5_table, seq)
    h = x[0]
    mem_w = mem_w_kv.astype(BF16)
    w_out_b = w_out.astype(BF16)
    for layer in range(depth):
        i = layer // 2
        mem_kv = _norm_matmul(mem[0], mem_norm_g, mem_w[layer], tm=mem.shape[1], tn=512, name="mem_kv")
        if layer % 2 == 0:
            w_in = _reorder_w_in(even_w_in[i], EVEN_SPLITS, EVEN_ORDER)
            b_cat, c_cat, tab = _s5_prepare(s5_lam_re[i], s5_lam_im[i], s5_log_dt[i], s5_b_re[i], s5_b_im[i],
                                            s5_c_re[i], s5_c_im[i])
            ys = _even_layer(h, mem_kv, norm_g[layer], w_in, (b_cat, c_cat, tab, s5_d[i]),
                             s5_w_glu[i].astype(BF16), fox_b_f[i], tiles)
            widths = (S5_WIDTH, FOX_WIDTH, MEM_WIDTH)
        else:
            w_in = _reorder_w_in(odd_w_in[i], ODD_SPLITS, ODD_ORDER)
            mla, nsa = _odd_params(i, mla_g_cq, mla_g_ckv, mla_w_uq, mla_w_ukv, nsa_cmp_pe, nsa_cmp_w1, nsa_cmp_w2,
                                   ctx)
            ys = _odd_layer(h, mem_kv, norm_g[layer], w_in, mla, nsa, ctx["pos"], tiles)
            widths = (MLA_WIDTH, NSA_WIDTH, MEM_WIDTH)
        starts = np.concatenate([[0], np.cumsum(widths)])
        ws = [w_out_b[layer, int(starts[k]):int(starts[k + 1])] for k in range(3)]
        h = _out_proj(h, ys, ws, tm=tiles["tm"], tn=512)
    return _final_norm(h, final_norm_g, tm=tiles["mem_t"])[None]
```

```python
import functools
import math

import numpy as np
import jax
import jax.numpy as jnp
from jax import lax
from jax.experimental import pallas as pl
from jax.experimental.pallas import tpu as pltpu

F32 = jnp.float32
BF16 = jnp.bfloat16
I32 = jnp.int32

D_MODEL = 2048
DEPTH = 4
N_MEM = 256
RMS_EPS = 1e-6
NEG_INF = -1e30

S5_WIDTH = 1024
S5_GROUP = 16
S5_GROUPS = S5_WIDTH // S5_GROUP
S5_STATE = 64
FOX_HEADS = 8
FOX_HEAD_DIM = 128
FOX_WIDTH = FOX_HEADS * FOX_HEAD_DIM
MEM_HEADS = 4
MEM_HEAD_DIM = 128
MEM_WIDTH = MEM_HEADS * MEM_HEAD_DIM
MLA_HEADS = 8
MLA_Q_RANK = 512
MLA_KV_RANK = 512
MLA_NOPE = 128
MLA_ROPE = 64
MLA_V = 128
MLA_WIDTH = MLA_HEADS * MLA_V
ROPE_THETA = 10000.0
NSA_HEADS = 8
NSA_KV_GROUPS = 2
NSA_REP = NSA_HEADS // NSA_KV_GROUPS
NSA_HEAD_DIM = 128
NSA_WIDTH = NSA_HEADS * NSA_HEAD_DIM
NSA_KV = NSA_KV_GROUPS * NSA_HEAD_DIM
NSA_CMP_LEN = 32
NSA_CMP_STRIDE = 16
NSA_CMP_HIDDEN = 256
NSA_SLC_BLOCK = 64
NSA_SLC_TOPK = 16
NSA_WINDOW = 512
FORCE_SCORE = 1e6
T5_BUCKETS = 32
T5_MAX_DIST = 1024

EVEN_SPLITS = (S5_WIDTH, S5_WIDTH, FOX_WIDTH, FOX_WIDTH, FOX_WIDTH, FOX_HEADS, FOX_WIDTH, MEM_WIDTH, MEM_WIDTH)
ODD_SPLITS = (MLA_Q_RANK, MLA_KV_RANK, MLA_ROPE, MLA_WIDTH, NSA_WIDTH, NSA_KV, NSA_KV, NSA_KV, NSA_KV, NSA_KV,
              NSA_KV, 3 * NSA_HEADS, NSA_WIDTH, MEM_WIDTH, MEM_WIDTH)

LANES = 128
SUBLANES = 8
QB = 128
VMEM_LIMIT_BYTES = 56 * 1024 * 1024

EVEN_B_ORDER = (("q", 2, 1024), ("k", 3, 1024), ("v", 4, 1024), ("q_mem", 7, 512))
EVEN_F_ORDER = (("u", 0, 1024), ("g_s5", 1, 1024), ("g_fox", 6, 1024), ("g_mem", 8, 512), ("f", 5, 128),
                ("pad", None, 128))
ODD_B_ORDER = (("q_nsa", 4, 1024), ("q_mem", 13, 512), ("k_cmp", 5, 256), ("v_cmp", 6, 256), ("k_slc", 7, 256),
               ("v_slc", 8, 256), ("k_win", 9, 256), ("v_win", 10, 256))
ODD_F_ORDER = (("g_nsa", 12, 1024), ("g_mla", 3, 1024), ("c_q", 0, 512), ("c_kv", 1, 512), ("g_mem", 14, 512),
               ("k_rope", 2, 128), ("gates", 11, 128))


def _layout(order):
    off, out = 0, {}
    for name, _, width in order:
        assert off % width == 0
        out[name] = off
        off += width
    return out, off


EVEN_B_OFF, EVEN_B_N = _layout(EVEN_B_ORDER)
EVEN_F_OFF, EVEN_F_N = _layout(EVEN_F_ORDER)
ODD_B_OFF, ODD_B_N = _layout(ODD_B_ORDER)
ODD_F_OFF, ODD_F_N = _layout(ODD_F_ORDER)


def _reorder_w_in(w, splits, order):
    starts = np.concatenate([[0], np.cumsum(splits)])
    cols = []
    for _, idx, width in order:
        if idx is None:
            cols.append(jnp.zeros((w.shape[0], width), w.dtype))
            continue
        seg = w[:, int(starts[idx]):int(starts[idx + 1])]
        pad = width - seg.shape[1]
        if pad:
            seg = jnp.pad(seg, ((0, 0), (0, pad)))
        cols.append(seg)
    return jnp.concatenate(cols, axis=1).astype(BF16)


def _cparams(*sem):
    return pltpu.CompilerParams(dimension_semantics=sem, vmem_limit_bytes=VMEM_LIMIT_BYTES)


def _silu(g):
    return g * jax.nn.sigmoid(g)


def _pick(n, cands):
    for c in cands:
        if n % c == 0:
            return c
    raise ValueError(f"no tile for {n} in {cands}")


def _norm_matmul_kernel(x_ref, g_ref, w_ref, o_ref, xn_ref):
    @pl.when(pl.program_id(1) == 0)
    def _():
        x = x_ref[...]
        ms = jnp.mean(x * x, axis=-1, keepdims=True)
        xn_ref[...] = (x * lax.rsqrt(ms + RMS_EPS) * g_ref[...]).astype(BF16)

    o_ref[...] = jnp.dot(xn_ref[...], w_ref[...], preferred_element_type=F32).astype(o_ref.dtype)


def _norm_matmul(x, g, w, *, x_cb=0, tm, tn, name, out_dtype=F32):
    m = x.shape[0]
    k, n = w.shape
    return pl.pallas_call(
        _norm_matmul_kernel,
        grid=(m // tm, n // tn),
        in_specs=[pl.BlockSpec((tm, k), lambda i, j: (i, x_cb)),
                  pl.BlockSpec((1, k), lambda i, j: (0, 0)),
                  pl.BlockSpec((k, tn), lambda i, j: (0, j))],
        out_specs=pl.BlockSpec((tm, tn), lambda i, j: (i, j)),
        out_shape=jax.ShapeDtypeStruct((m, n), out_dtype),
        scratch_shapes=[pltpu.VMEM((tm, k), BF16)],
        compiler_params=_cparams("parallel", "arbitrary"),
        name=name,
    )(x, g.reshape(1, k), w)


def _out_proj_kernel(h_ref, *refs):
    o_ref = refs[-1]
    n = (len(refs) - 1) // 2
    acc = h_ref[...]
    for y_ref, w_ref in zip(refs[:n], refs[n:2 * n]):
        acc = acc + jnp.dot(y_ref[...], w_ref[...], preferred_element_type=F32)
    o_ref[...] = acc


def _out_proj(h, ys, ws, *, tm, tn):
    m, n = h.shape
    in_specs = [pl.BlockSpec((tm, tn), lambda i, j: (i, j))]
    in_specs += [pl.BlockSpec((tm, y.shape[1]), lambda i, j: (i, 0)) for y in ys]
    in_specs += [pl.BlockSpec((w.shape[0], tn), lambda i, j: (0, j)) for w in ws]
    return pl.pallas_call(
        _out_proj_kernel,
        grid=(m // tm, n // tn),
        in_specs=in_specs,
        out_specs=pl.BlockSpec((tm, tn), lambda i, j: (i, j)),
        out_shape=jax.ShapeDtypeStruct((m, n), F32),
        compiler_params=_cparams("parallel", "arbitrary"),
        name="out_proj",
    )(h, *ys, *ws)


def _final_norm_kernel(x_ref, g_ref, o_ref):
    x = x_ref[...]
    ms = jnp.mean(x * x, axis=-1, keepdims=True)
    o_ref[...] = x * lax.rsqrt(ms + RMS_EPS) * g_ref[...]


def _final_norm(h, g, *, tm):
    m, n = h.shape
    return pl.pallas_call(
        _final_norm_kernel,
        grid=(m // tm,),
        in_specs=[pl.BlockSpec((tm, n), lambda i: (i, 0)), pl.BlockSpec((1, n), lambda i: (0, 0))],
        out_specs=pl.BlockSpec((tm, n), lambda i: (i, 0)),
        out_shape=jax.ShapeDtypeStruct((m, n), F32),
        compiler_params=_cparams("parallel"),
        name="final_norm",
    )(h, g.reshape(1, n))


ONES_ROWS = 16


def _build_vt(v_ref, vt_ref, hh, dv, seq, chunk):
    def body(c, carry):
        st = pl.multiple_of(c * chunk, chunk)
        vt_ref[hh, 0:dv, pl.ds(st, chunk)] = v_ref[pl.ds(st, chunk), hh * dv:(hh + 1) * dv].astype(F32).T.astype(BF16)
        return carry

    lax.fori_loop(0, seq // chunk, body, 0)
    vt_ref[hh, dv:dv + ONES_ROWS, :] = jnp.ones((ONES_ROWS, seq), BF16)


def _flash_kernel(*refs, scale, tq, tk, hp, dk, dv, has_decay):
    if has_decay:
        q_ref, k_ref, v_ref, g_ref, cq_ref, ck_ref, o_ref, vt_ref, m_ref, acc_ref = refs
    else:
        q_ref, k_ref, v_ref, g_ref, o_ref, vt_ref, m_ref, acc_ref = refs
    hb = pl.program_id(0)
    qi = pl.program_id(1)
    seq = k_ref.shape[0]

    @pl.when(qi == 0)
    def _():
        for hh in range(hp):
            _build_vt(v_ref, vt_ref, hh, dv, seq, tk)

    m_ref[...] = jnp.full(m_ref.shape, NEG_INF, F32)
    acc_ref[...] = jnp.zeros(acc_ref.shape, F32)
    n_full = (qi * tq) // tk

    def step(j, masked):
        start = pl.multiple_of(j * tk, tk)
        for hh in range(hp):
            k = k_ref[pl.ds(start, tk), hh * dk:(hh + 1) * dk]
            q = q_ref[:, hh * dk:(hh + 1) * dk]
            s = lax.dot_general(k, q, (((1,), (1,)), ((), ())), preferred_element_type=F32) * scale
            if has_decay:
                ck = ck_ref[hh, pl.ds(start, tk), :]
                s = s + cq_ref[pl.ds(hb * hp + hh, 1), :] - jnp.concatenate([ck] * (tq // LANES), axis=1)
            if masked:
                key = start + lax.broadcasted_iota(I32, (tk, tq), 0)
                qry = qi * tq + lax.broadcasted_iota(I32, (tk, tq), 1)
                s = jnp.where(key <= qry, s, NEG_INF)
            m_prev = m_ref[hh]
            m_new = jnp.maximum(m_prev, jnp.max(s, axis=0, keepdims=True))
            alpha = jnp.exp(m_prev - m_new)
            p = jnp.exp(s - m_new).astype(BF16)
            acc_ref[hh] = alpha * acc_ref[hh] + jnp.dot(vt_ref[hh, :, pl.ds(start, tk)], p,
                                                         preferred_element_type=F32)
            m_ref[hh] = m_new

    def body(j, carry):
        step(j, False)
        return carry

    lax.fori_loop(0, n_full, body, 0)
    step(n_full, True)
    for hh in range(hp):
        a = acc_ref[hh]
        o = (a[0:dv, :] / a[dv:dv + 1, :]).T
        o_ref[:, hh * dv:(hh + 1) * dv] = (o * _silu(g_ref[:, hh * dv:(hh + 1) * dv])).astype(o_ref.dtype)


def _flash(q_arr, q_off, k_arr, k_off, v_arr, v_off, g_arr, g_off, *, heads, dk, dv, scale, tq, tk, hp,
           decay=None, name):
    seq = q_arr.shape[0]
    once = pl.Buffered(1)
    in_specs = [pl.BlockSpec((tq, hp * dk), lambda h, i: (i, q_off // (hp * dk) + h)),
                pl.BlockSpec((seq, hp * dk), lambda h, i: (0, k_off // (hp * dk) + h), pipeline_mode=once),
                pl.BlockSpec((seq, hp * dv), lambda h, i: (0, v_off // (hp * dv) + h), pipeline_mode=once),
                pl.BlockSpec((tq, hp * dv), lambda h, i: (i, g_off // (hp * dv) + h))]
    args = [q_arr, k_arr, v_arr, g_arr]
    if decay is not None:
        cum_t, cum_rep = decay
        in_specs += [pl.BlockSpec((SUBLANES, tq), lambda h, i: (0, i)),
                     pl.BlockSpec((hp, seq, LANES), lambda h, i: (h, 0, 0), pipeline_mode=once)]
        args += [cum_t, cum_rep]
    return pl.pallas_call(
        functools.partial(_flash_kernel, scale=scale, tq=tq, tk=tk, hp=hp, dk=dk, dv=dv, has_decay=decay is not None),
        grid=(heads // hp, seq // tq),
        in_specs=in_specs,
        out_specs=pl.BlockSpec((tq, hp * dv), lambda h, i: (i, h)),
        out_shape=jax.ShapeDtypeStruct((seq, heads * dv), BF16),
        scratch_shapes=[pltpu.VMEM((hp, dv + ONES_ROWS, seq), BF16), pltpu.VMEM((hp, 1, tq), F32),
                        pltpu.VMEM((hp, dv + ONES_ROWS, tq), F32)],
        compiler_params=_cparams("parallel", "arbitrary"),
        name=name,
    )(*args)


def _decay_kernel(f_ref, b_ref, ct_ref, cr_ref, carry_ref, *, t):
    i = pl.program_id(0)

    @pl.when(i == 0)
    def _():
        carry_ref[...] = jnp.zeros(carry_ref.shape, F32)

    x = f_ref[...] + b_ref[...]
    lf = jnp.minimum(x, 0.0) - jnp.log1p(jnp.exp(-jnp.abs(x)))
    row = lax.broadcasted_iota(I32, lf.shape, 0)
    s = 1
    while s < t:
        lf = lf + jnp.where(row >= s, pltpu.roll(lf, s, 0), 0.0)
        s *= 2
    lf = lf + carry_ref[...]
    carry_ref[...] = lf[t - 1:t, :]
    ct_ref[...] = lf.T[:FOX_HEADS, :]
    for h in range(FOX_HEADS):
        cr_ref[h] = jnp.broadcast_to(lf[:, h:h + 1], (t, LANES))


def _decay(proj, f_cb, b_f, *, t):
    seq = proj.shape[0]
    b = jnp.pad(b_f.reshape(1, FOX_HEADS), ((0, 0), (0, LANES - FOX_HEADS)))
    return pl.pallas_call(
        functools.partial(_decay_kernel, t=t),
        grid=(seq // t,),
        in_specs=[pl.BlockSpec((t, LANES), lambda i: (i, f_cb)), pl.BlockSpec((1, LANES), lambda i: (0, 0))],
        out_specs=[pl.BlockSpec((FOX_HEADS, t), lambda i: (0, i)),
                   pl.BlockSpec((FOX_HEADS, t, LANES), lambda i: (0, i, 0))],
        out_shape=[jax.ShapeDtypeStruct((FOX_HEADS, seq), F32), jax.ShapeDtypeStruct((FOX_HEADS, seq, LANES), F32)],
        scratch_shapes=[pltpu.VMEM((1, LANES), F32)],
        compiler_params=_cparams("arbitrary"),
        name="fox_decay",
    )(proj, b)


def _mem_attn_kernel(q_ref, k_ref, v_ref, g_ref, o_ref):
    k = k_ref[...].astype(BF16)
    s = lax.dot_general(q_ref[...], k, (((1,), (1,)), ((), ())), preferred_element_type=F32) * (MEM_HEAD_DIM ** -0.5)
    m = jnp.max(s, axis=1, keepdims=True)
    p = jnp.exp(s - m)
    l = jnp.sum(p, axis=1, keepdims=True)
    o = jnp.dot(p.astype(BF16), v_ref[...].astype(BF16), preferred_element_type=F32) / l
    o_ref[...] = (o * _silu(g_ref[...])).astype(o_ref.dtype)


def _mem_attn(proj_b, q_cb, proj_f, g_cb, mem_kv, *, t):
    seq = proj_b.shape[0]
    nm = mem_kv.shape[0]
    d = MEM_HEAD_DIM
    return pl.pallas_call(
        _mem_attn_kernel,
        grid=(MEM_HEADS, seq // t),
        in_specs=[pl.BlockSpec((t, d), lambda h, i: (i, q_cb + h)),
                  pl.BlockSpec((nm, d), lambda h, i: (0, h)),
                  pl.BlockSpec((nm, d), lambda h, i: (0, MEM_HEADS + h)),
                  pl.BlockSpec((t, d), lambda h, i: (i, g_cb + h))],
        out_specs=pl.BlockSpec((t, d), lambda h, i: (i, h)),
        out_shape=jax.ShapeDtypeStruct((seq, MEM_WIDTH), BF16),
        compiler_params=_cparams("parallel", "arbitrary"),
        name="mem_attn",
    )(proj_b, mem_kv, mem_kv, proj_f)


S5_TILE_GROUPS = LANES // S5_GROUP
S5_TILE_STATES = S5_TILE_GROUPS * S5_STATE
S5_TILES = S5_GROUPS // S5_TILE_GROUPS


def _s5_scan_kernel(u_ref, b_ref, c_ref, d_ref, tab_ref, z_ref, bu_ref, carry_ref, *, tc):
    ns = S5_TILE_STATES

    @pl.when(pl.program_id(1) == 0)
    def _():
        carry_ref[...] = jnp.zeros(carry_ref.shape, F32)

    u = u_ref[...]
    bu_ref[...] = jnp.dot(u.astype(BF16), b_ref[...], preferred_element_type=F32)
    steps = [(1, tab_ref[0], tab_ref[1]), (2, tab_ref[2], tab_ref[3]), (4, tab_ref[4], tab_ref[5])]
    pr = tab_ref[6]
    pi = tab_ref[7]

    def body(i, carry):
        cr, ci = carry
        r0 = pl.multiple_of(i * SUBLANES, SUBLANES)
        xr = bu_ref[pl.ds(r0, SUBLANES), 0:ns]
        xi = bu_ref[pl.ds(r0, SUBLANES), ns:2 * ns]
        for s, ar, ai in steps:
            sr = pltpu.roll(xr, s, 0)
            si = pltpu.roll(xi, s, 0)
            xr, xi = xr + ar * sr - ai * si, xi + ar * si + ai * sr
        xr, xi = xr + pr * cr - pi * ci, xi + pr * ci + pi * cr
        bu_ref[pl.ds(r0, SUBLANES), 0:ns] = xr
        bu_ref[pl.ds(r0, SUBLANES), ns:2 * ns] = xi
        return xr[SUBLANES - 1:SUBLANES, :], xi[SUBLANES - 1:SUBLANES, :]

    cr, ci = lax.fori_loop(0, tc // SUBLANES, body, (carry_ref[0:1, 0:ns], carry_ref[0:1, ns:2 * ns]))
    carry_ref[0:1, 0:ns] = cr
    carry_ref[0:1, ns:2 * ns] = ci
    y = jnp.dot(bu_ref[...].astype(BF16), c_ref[...], preferred_element_type=F32) + d_ref[...] * u
    z_ref[...] = jax.nn.gelu(y)


def _s5_prepare(lam_re, lam_im, log_dt, b_re, b_im, c_re, c_im):
    dt = jnp.exp(log_dt.astype(F32))[:, None]
    lr = lam_re.astype(F32)
    li = lam_im.astype(F32)
    mag = jnp.exp(lr * dt)
    ab_re = mag * jnp.cos(li * dt)
    ab_im = mag * jnp.sin(li * dt)
    den = lr * lr + li * li
    nr = ab_re - 1.0
    f_re = (nr * lr + ab_im * li) / den
    f_im = (ab_im * lr - nr * li) / den
    br = b_re.astype(F32)
    bim = b_im.astype(F32)
    bb_re = f_re[..., None] * br - f_im[..., None] * bim
    bb_im = f_re[..., None] * bim + f_im[..., None] * br
    eye = jnp.eye(S5_TILE_GROUPS, dtype=F32)

    def blockdiag_in(bb):
        t = bb.reshape(S5_TILES, S5_TILE_GROUPS, S5_STATE, S5_GROUP)
        m = jnp.einsum("jgpc,gh->jgchp", t, eye)
        return m.reshape(S5_TILES, LANES, S5_TILE_STATES)

    def blockdiag_out(cc):
        t = cc.reshape(S5_TILES, S5_TILE_GROUPS, S5_GROUP, S5_STATE)
        m = jnp.einsum("jgcp,gh->jgphc", t, eye)
        return m.reshape(S5_TILES, S5_TILE_STATES, LANES)

    b_cat = jnp.concatenate([blockdiag_in(bb_re), blockdiag_in(bb_im)], axis=2).astype(BF16)
    c_cat = jnp.concatenate([blockdiag_out(c_re.astype(F32)), -blockdiag_out(c_im.astype(F32))], axis=1).astype(BF16)

    a_r = ab_re.reshape(S5_TILES, 1, S5_TILE_STATES)
    a_i = ab_im.reshape(S5_TILES, 1, S5_TILE_STATES)

    def cmul(xr, xi, yr, yi):
        return xr * yr - xi * yi, xr * yi + xi * yr

    a2 = cmul(a_r, a_i, a_r, a_i)
    a4 = cmul(*a2, *a2)
    row = jnp.arange(SUBLANES)[None, :, None]
    tabs = []
    for s, (pr_, pi_) in ((1, (a_r, a_i)), (2, a2), (4, a4)):
        tabs.append(jnp.where(row >= s, pr_, 0.0))
        tabs.append(jnp.where(row >= s, pi_, 0.0))
    pw = [(a_r, a_i)]
    for _ in range(SUBLANES - 1):
        pw.append(cmul(*pw[-1], a_r, a_i))
    tabs.append(jnp.concatenate([p[0] for p in pw], axis=1))
    tabs.append(jnp.concatenate([p[1] for p in pw], axis=1))
    tab = jnp.stack([jnp.broadcast_to(t, (S5_TILES, SUBLANES, S5_TILE_STATES)) for t in tabs], axis=1)
    return b_cat, c_cat, tab.astype(F32)


def _s5_scan(proj, u_cb, b_cat, c_cat, d_skip, tab, *, tc):
    seq = proj.shape[0]
    ns = S5_TILE_STATES
    d = d_skip.astype(F32).reshape(S5_TILES, 1, LANES)
    return pl.pallas_call(
        functools.partial(_s5_scan_kernel, tc=tc),
        grid=(S5_TILES, seq // tc),
        in_specs=[pl.BlockSpec((tc, LANES), lambda j, c: (c, u_cb + j)),
                  pl.BlockSpec((None, LANES, 2 * ns), lambda j, c: (j, 0, 0)),
                  pl.BlockSpec((None, 2 * ns, LANES), lambda j, c: (j, 0, 0)),
                  pl.BlockSpec((None, 1, LANES), lambda j, c: (j, 0, 0)),
                  pl.BlockSpec((None, 8, SUBLANES, ns), lambda j, c: (j, 0, 0, 0))],
        out_specs=pl.BlockSpec((tc, LANES), lambda j, c: (c, j)),
        out_shape=jax.ShapeDtypeStruct((seq, S5_WIDTH), F32),
        scratch_shapes=[pltpu.VMEM((tc, 2 * ns), F32), pltpu.VMEM((SUBLANES, 2 * ns), F32)],
        compiler_params=_cparams("parallel", "arbitrary"),
        name="s5_scan",
    )(proj, b_cat, c_cat, d, tab)


def _s5_glu_kernel(z_ref, w_ref, g_ref, o_ref, *, tn):
    j = pl.program_id(1)
    z = z_ref[...]
    a = jnp.dot(z.astype(BF16), w_ref[...], preferred_element_type=F32)
    zc = z_ref[:, pl.ds(pl.multiple_of(j * tn, tn), tn)]
    o_ref[...] = (zc * jax.nn.sigmoid(a) * _silu(g_ref[...])).astype(o_ref.dtype)


def _s5_glu(z, w_glu, proj, g_cb, *, tm, tn):
    seq, n = z.shape
    return pl.pallas_call(
        functools.partial(_s5_glu_kernel, tn=tn),
        grid=(seq // tm, n // tn),
        in_specs=[pl.BlockSpec((tm, n), lambda i, j: (i, 0)),
                  pl.BlockSpec((n, tn), lambda i, j: (0, j)),
                  pl.BlockSpec((tm, tn), lambda i, j: (i, g_cb * (n // tn) + j))],
        out_specs=pl.BlockSpec((tm, tn), lambda i, j: (i, j)),
        out_shape=jax.ShapeDtypeStruct((seq, n), BF16),
        compiler_params=_cparams("parallel", "arbitrary"),
        name="s5_glu",
    )(z, w_glu, proj)


def _rope_tables(pos, freq):
    ang = pos * freq
    lane = lax.broadcasted_iota(I32, ang.shape, 1)
    half = MLA_ROPE // 2
    cos = jnp.cos(ang)
    sin = jnp.sin(ang)
    c = jnp.where(lane < MLA_ROPE, cos, 0.0)
    s1 = jnp.where(lane < half, -sin, 0.0)
    s2 = jnp.where((lane >= half) & (lane < MLA_ROPE), sin, 0.0)
    return c, s1, s2


def _rope_apply(x, c, s1, s2):
    half = MLA_ROPE // 2
    return x * c + pltpu.roll(x, LANES - half, 1) * s1 + pltpu.roll(x, half, 1) * s2


def _mla_prep_kernel(qf_ref, kvf_ref, kr_ref, pos_ref, freq_ref, q_ref, k_ref, v_ref):
    c, s1, s2 = _rope_tables(pos_ref[...], freq_ref[...])
    kr = _rope_apply(kr_ref[...], c, s1, s2).astype(BF16)
    for h in range(MLA_HEADS):
        b = 2 * LANES * h
        q_ref[:, b:b + LANES] = qf_ref[:, b:b + LANES].astype(BF16)
        q_ref[:, b + LANES:b + 2 * LANES] = _rope_apply(qf_ref[:, b + LANES:b + 2 * LANES], c, s1, s2).astype(BF16)
        k_ref[:, b:b + LANES] = kvf_ref[:, b:b + LANES].astype(BF16)
        k_ref[:, b + LANES:b + 2 * LANES] = kr
        v_ref[:, LANES * h:LANES * (h + 1)] = kvf_ref[:, b + LANES:b + 2 * LANES].astype(BF16)


def _mla_prep(qf, kvf, proj, kr_cb, pos_col, freq, *, t):
    seq = qf.shape[0]
    w = 2 * LANES * MLA_HEADS
    return pl.pallas_call(
        _mla_prep_kernel,
        grid=(seq // t,),
        in_specs=[pl.BlockSpec((t, w), lambda i: (i, 0)),
                  pl.BlockSpec((t, w), lambda i: (i, 0)),
                  pl.BlockSpec((t, LANES), lambda i: (i, kr_cb)),
                  pl.BlockSpec((t, 1), lambda i: (i, 0)),
                  pl.BlockSpec((1, LANES), lambda i: (0, 0))],
        out_specs=[pl.BlockSpec((t, w), lambda i: (i, 0)),
                   pl.BlockSpec((t, w), lambda i: (i, 0)),
                   pl.BlockSpec((t, MLA_WIDTH), lambda i: (i, 0))],
        out_shape=[jax.ShapeDtypeStruct((seq, w), BF16), jax.ShapeDtypeStruct((seq, w), BF16),
                   jax.ShapeDtypeStruct((seq, MLA_WIDTH), BF16)],
        compiler_params=_cparams("parallel"),
        name="mla_prep",
    )(qf, kvf, proj, pos_col, freq)


def _t5_bucket(dist):
    n = jnp.maximum(dist, 0)
    max_exact = T5_BUCKETS // 2
    log_ratio = jnp.log(jnp.maximum(n, 1).astype(F32) / max_exact) / math.log(T5_MAX_DIST / max_exact)
    large = jnp.minimum(max_exact + (log_ratio * (T5_BUCKETS - max_exact)).astype(I32), T5_BUCKETS - 1)
    return jnp.where(n < max_exact, n, large)


def _t5_lookup(table_row, bucket):
    tab = jnp.broadcast_to(table_row, bucket.shape)
    return jnp.take_along_axis(tab, bucket, axis=1)


def _nsa_cmp_kernel(x_ref, pe_ref, w1_ref, w2_ref, o_ref, xf_ref, *, nc):
    half = NSA_CMP_LEN // 2
    d = NSA_HEAD_DIM
    xf_ref[...] = x_ref[...].astype(F32)
    u = jnp.zeros((nc, NSA_CMP_HIDDEN), F32)
    v = jnp.zeros((nc, NSA_CMP_HIDDEN), F32)
    for r in range(half):
        a = xf_ref[pl.ds(r, nc, stride=NSA_CMP_STRIDE), :]
        u = u + jnp.dot((a + pe_ref[r:r + 1, :]).astype(BF16), w1_ref[r * d:(r + 1) * d, :],
                        preferred_element_type=F32)
        v = v + jnp.dot((a + pe_ref[half + r:half + r + 1, :]).astype(BF16),
                        w1_ref[(half + r) * d:(half + r + 1) * d, :], preferred_element_type=F32)
    hid = u + pltpu.roll(v, nc - 1, 0)
    o_ref[...] = jnp.dot(jax.nn.gelu(hid).astype(BF16), w2_ref[...], preferred_element_type=F32).astype(o_ref.dtype)


def _nsa_compress(proj, k_cb, pe, w1, w2):
    seq = proj.shape[0]
    nc = seq // NSA_CMP_STRIDE
    d = NSA_HEAD_DIM
    g = NSA_KV_GROUPS
    return pl.pallas_call(
        functools.partial(_nsa_cmp_kernel, nc=nc),
        grid=(2, g),
        in_specs=[pl.BlockSpec((seq, d), lambda a, b: (0, k_cb + a * g + b)),
                  pl.BlockSpec((None, NSA_CMP_LEN, d), lambda a, b: (a, 0, 0)),
                  pl.BlockSpec((None, NSA_CMP_LEN * d, NSA_CMP_HIDDEN), lambda a, b: (a, 0, 0)),
                  pl.BlockSpec((None, NSA_CMP_HIDDEN, d), lambda a, b: (a, 0, 0))],
        out_specs=pl.BlockSpec((None, None, nc, d), lambda a, b: (a, b, 0, 0)),
        out_shape=jax.ShapeDtypeStruct((2, g, nc, d), BF16),
        scratch_shapes=[pltpu.VMEM((seq, d), F32)],
        compiler_params=_cparams("parallel", "arbitrary"),
        name="nsa_compress",
    )(proj, pe, w1, w2)


def _nsa_select_kernel(q_ref, kc_ref, vc_ref, posq_ref, posc_ref, tab_ref, gate_ref, ov_ref,
                       oc_ref, sel_ref, *, nc, n_slc):
    qi = pl.program_id(0)
    d = NSA_HEAD_DIM
    scale = d ** -0.5
    t = qi * QB + lax.broadcasted_iota(I32, (QB, 1), 0)
    n_idx = lax.broadcasted_iota(I32, (1, nc), 1)
    cmp_end = n_idx * NSA_CMP_STRIDE + (NSA_CMP_LEN - 1)
    valid = cmp_end <= t
    dist = posq_ref[...] - posc_ref[...]
    buckets = [_t5_bucket(dist[:, c * LANES:(c + 1) * LANES]) for c in range(nc // LANES)]
    gates = jax.nn.sigmoid(gate_ref[...])
    ov = ov_ref[...]
    jl = lax.broadcasted_iota(I32, (QB, LANES), 1)
    cur = t // NSA_SLC_BLOCK
    forced = (jl == 0) | (jl == cur) | (jl == cur - 1)
    js = lax.broadcasted_iota(I32, (LANES, QB), 0).astype(F32)
    for g in range(NSA_KV_GROUPS):
        kc = kc_ref[g]
        vc = vc_ref[g]
        psum = jnp.zeros((QB, nc), F32)
        for r in range(NSA_REP):
            h = g * NSA_REP + r
            q = q_ref[:, h * d:(h + 1) * d]
            s = lax.dot_general(q, kc, (((1,), (1,)), ((), ())), preferred_element_type=F32) * scale
            bias = jnp.concatenate([_t5_lookup(tab_ref[h:h + 1, :], b) for b in buckets], axis=1)
            s = jnp.where(valid, s + bias, NEG_INF)
            m = jnp.max(s, axis=1, keepdims=True)
            e = jnp.exp(s - m)
            p = jnp.where(valid, e / jnp.sum(e, axis=1, keepdims=True), 0.0)
            o = jnp.dot(p.astype(BF16), vc, preferred_element_type=F32)
            oc_ref[:, h * d:(h + 1) * d] = gates[:, 3 * h:3 * h + 1] * o
            psum = psum + p
        p_hi = psum.astype(BF16)
        p_lo = (psum - p_hi.astype(F32)).astype(BF16)
        imp = jnp.dot(p_hi, ov, preferred_element_type=F32) + jnp.dot(p_lo, ov, preferred_element_type=F32)
        score = jnp.where(forced, FORCE_SCORE, jnp.where(jl > cur, -1.0, imp))
        score = jnp.where(jl < n_slc, score, -2.0)
        st = score.T
        sel = jnp.zeros((LANES, QB), F32)
        for _ in range(NSA_SLC_TOPK):
            mx = jnp.max(st, axis=0, keepdims=True)
            first = jnp.min(jnp.where(st == mx, js, float(LANES)), axis=0, keepdims=True)
            hit = js == first
            sel = jnp.where(hit, 1.0, sel)
            st = jnp.where(hit, -3e38, st)
        sel_ref[g] = sel.astype(sel_ref.dtype)


def _nsa_select(proj_b, q_cb, proj_f, gate_cb, kc, vc, pos_col, pos_cmp, tab_t, ov, *, n_slc):
    seq = proj_b.shape[0]
    nc = kc.shape[1]
    g = NSA_KV_GROUPS
    d = NSA_HEAD_DIM
    return pl.pallas_call(
        functools.partial(_nsa_select_kernel, nc=nc, n_slc=n_slc),
        grid=(seq // QB,),
        in_specs=[pl.BlockSpec((QB, NSA_WIDTH), lambda i: (i, q_cb)),
                  pl.BlockSpec((g, nc, d), lambda i: (0, 0, 0)),
                  pl.BlockSpec((g, nc, d), lambda i: (0, 0, 0)),
                  pl.BlockSpec((QB, 1), lambda i: (i, 0)),
                  pl.BlockSpec((1, nc), lambda i: (0, 0)),
                  pl.BlockSpec((SUBLANES, LANES), lambda i: (0, 0)),
                  pl.BlockSpec((QB, LANES), lambda i: (i, gate_cb)),
                  pl.BlockSpec((nc, LANES), lambda i: (0, 0))],
        out_specs=[pl.BlockSpec((QB, NSA_WIDTH), lambda i: (i, 0)),
                   pl.BlockSpec((g, LANES, QB), lambda i: (0, 0, i))],
        out_shape=[jax.ShapeDtypeStruct((seq, NSA_WIDTH), F32),
                   jax.ShapeDtypeStruct((g, LANES, seq), BF16)],
        compiler_params=_cparams("parallel"),
        name="nsa_select",
    )(proj_b, kc, vc, pos_col, pos_cmp, tab_t, proj_f, ov)


def _nsa_slc_kernel(pqmin_ref, pkmax_ref, q_ref, k_ref, v_ref, sel_ref, posq_ref, posk_ref, tab_ref, o_ref,
                    vt_ref, m_ref, acc_ref, *, tk):
    g = pl.program_id(0)
    qi = pl.program_id(1)
    d = NSA_HEAD_DIM
    scale = d ** -0.5
    seq = k_ref.shape[0]

    @pl.when(qi == 0)
    def _():
        _build_vt(v_ref, vt_ref, 0, d, seq, tk)

    m_ref[...] = jnp.full(m_ref.shape, NEG_INF, F32)
    acc_ref[...] = jnp.zeros(acc_ref.shape, F32)
    sel = sel_ref[...]
    pos_q = posq_ref[...]
    key_blk = lax.broadcasted_iota(I32, (tk, LANES), 0) // NSA_SLC_BLOCK
    blk = lax.broadcasted_iota(I32, (tk, LANES), 1)
    per_tile = tk // NSA_SLC_BLOCK
    n_full = (qi * QB) // tk

    def step(j, masked, near):
        start = pl.multiple_of(j * tk, tk)
        k = k_ref[pl.ds(start, tk), :]
        expand = jnp.where(blk == key_blk + j * per_tile, 1.0, 0.0).astype(BF16)
        picked = jnp.dot(expand, sel, preferred_element_type=F32)
        if masked:
            key = start + lax.broadcasted_iota(I32, (tk, QB), 0)
            qry = qi * QB + lax.broadcasted_iota(I32, (tk, QB), 1)
            picked = jnp.where(key <= qry, picked, 0.0)
        mask = picked > 0.5
        if near:
            bucket = _t5_bucket(pos_q - posk_ref[pl.ds(start, tk), :])
        for r in range(NSA_REP):
            h = g * NSA_REP + r
            s = lax.dot_general(k, q_ref[:, r * d:(r + 1) * d], (((1,), (1,)), ((), ())),
                                preferred_element_type=F32) * scale
            if near:
                s = s + _t5_lookup(tab_ref[pl.ds(h, 1), :], bucket)
            else:
                s = s + tab_ref[pl.ds(h, 1), T5_BUCKETS - 1:T5_BUCKETS]
            s = jnp.where(mask, s, NEG_INF)
            m_prev = m_ref[r]
            m_new = jnp.maximum(m_prev, jnp.max(s, axis=0, keepdims=True))
            alpha = jnp.exp(m_prev - m_new)
            p = jnp.exp(s - m_new).astype(BF16)
            acc_ref[r] = alpha * acc_ref[r] + jnp.dot(vt_ref[0, :, pl.ds(start, tk)], p, preferred_element_type=F32)
            m_ref[r] = m_new

    def body(j, carry):
        far = pqmin_ref[qi] - pkmax_ref[j] >= T5_MAX_DIST

        @pl.when(far)
        def _():
            step(j, False, False)

        @pl.when(jnp.logical_not(far))
        def _():
            step(j, False, True)

        return carry

    lax.fori_loop(0, n_full, body, 0)
    step(n_full, True, True)
    for r in range(NSA_REP):
        a = acc_ref[r]
        o_ref[:, r * d:(r + 1) * d] = (a[0:d, :] / a[d:d + 1, :]).T


def _nsa_slc(proj_b, q_off, k_off, v_off, sel, pos_row, pos_rep, tab_t, pq_min, pk_max, *, tk):
    seq = proj_b.shape[0]
    d = NSA_HEAD_DIM
    gw = NSA_REP * d
    once = pl.Buffered(1)
    grid_spec = pltpu.PrefetchScalarGridSpec(
        num_scalar_prefetch=2,
        grid=(NSA_KV_GROUPS, seq // QB),
        in_specs=[pl.BlockSpec((QB, gw), lambda g, i, *_: (i, q_off // gw + g)),
                  pl.BlockSpec((seq, d), lambda g, i, *_: (0, k_off // d + g), pipeline_mode=once),
                  pl.BlockSpec((seq, d), lambda g, i, *_: (0, v_off // d + g), pipeline_mode=once),
                  pl.BlockSpec((None, LANES, QB), lambda g, i, *_: (g, 0, i)),
                  pl.BlockSpec((1, QB), lambda g, i, *_: (0, i)),
                  pl.BlockSpec((seq, LANES), lambda g, i, *_: (0, 0), pipeline_mode=once),
                  pl.BlockSpec((SUBLANES, LANES), lambda g, i, *_: (0, 0))],
        out_specs=pl.BlockSpec((QB, gw), lambda g, i, *_: (i, g)),
        scratch_shapes=[pltpu.VMEM((1, d + ONES_ROWS, seq), BF16), pltpu.VMEM((NSA_REP, 1, QB), F32),
                        pltpu.VMEM((NSA_REP, d + ONES_ROWS, QB), F32)])
    return pl.pallas_call(
        functools.partial(_nsa_slc_kernel, tk=tk),
        grid_spec=grid_spec,
        out_shape=jax.ShapeDtypeStruct((seq, NSA_WIDTH), F32),
        compiler_params=_cparams("parallel", "arbitrary"),
        name="nsa_slc",
    )(pq_min, pk_max, proj_b, proj_b, proj_b, sel, pos_row, pos_rep, tab_t)


NSA_WIN_TILES = NSA_WINDOW // QB + 1


def _nsa_win_kernel(*refs):
    nt = NSA_WIN_TILES
    q_ref = refs[0]
    k_refs = refs[1:1 + nt]
    v_refs = refs[1 + nt:1 + 2 * nt]
    pk_refs = refs[1 + 2 * nt:1 + 3 * nt]
    posq_ref, tab_ref, gate_ref, gout_ref, oc_ref, os_ref, o_ref = refs[1 + 3 * nt:]
    qi = pl.program_id(0)
    d = NSA_HEAD_DIM
    scale = d ** -0.5
    t = qi * QB + lax.broadcasted_iota(I32, (QB, 1), 0)
    pos_q = posq_ref[...]
    lane = lax.broadcasted_iota(I32, (1, QB), 1)
    bands, buckets = [], []
    for jj in range(nt):
        kidx = (qi - (nt - 1) + jj) * QB + lane
        bands.append((kidx >= 0) & (kidx <= t) & (t - kidx < NSA_WINDOW))
        buckets.append(_t5_bucket(pos_q - pk_refs[jj][...]))
    band = jnp.concatenate(bands, axis=1)
    gates = jax.nn.sigmoid(gate_ref[...])
    for g in range(NSA_KV_GROUPS):
        k = jnp.concatenate([kr[:, g * d:(g + 1) * d] for kr in k_refs], axis=0)
        v = jnp.concatenate([vr[:, g * d:(g + 1) * d] for vr in v_refs], axis=0)
        for r in range(NSA_REP):
            h = g * NSA_REP + r
            hs = slice(h * d, (h + 1) * d)
            s = lax.dot_general(q_ref[:, hs], k, (((1,), (1,)), ((), ())), preferred_element_type=F32) * scale
            bias = jnp.concatenate([_t5_lookup(tab_ref[h:h + 1, :], b) for b in buckets], axis=1)
            s = jnp.where(band, s + bias, NEG_INF)
            m = jnp.max(s, axis=1, keepdims=True)
            p = jnp.exp(s - m)
            o_w = jnp.dot(p.astype(BF16), v, preferred_element_type=F32) / jnp.sum(p, axis=1, keepdims=True)
            o = oc_ref[:, hs] + gates[:, 3 * h + 1:3 * h + 2] * os_ref[:, hs] + gates[:, 3 * h + 2:3 * h + 3] * o_w
            o_ref[:, hs] = (o * _silu(gout_ref[:, hs])).astype(o_ref.dtype)


def _nsa_win(proj_b, q_cb, k_cb, v_cb, proj_f, gate_cb, gout_cb, oc, o_s, pos_col, pos_row, tab_t):
    seq = proj_b.shape[0]
    nt = NSA_WIN_TILES

    def band_rows(jj, cb):
        return pl.BlockSpec((QB, NSA_KV), lambda i: (jnp.maximum(i - (nt - 1) + jj, 0), cb))

    def band_pos(jj):
        return pl.BlockSpec((1, QB), lambda i: (0, jnp.maximum(i - (nt - 1) + jj, 0)))

    in_specs = [pl.BlockSpec((QB, NSA_WIDTH), lambda i: (i, q_cb))]
    in_specs += [band_rows(jj, k_cb) for jj in range(nt)]
    in_specs += [band_rows(jj, v_cb) for jj in range(nt)]
    in_specs += [band_pos(jj) for jj in range(nt)]
    in_specs += [pl.BlockSpec((QB, 1), lambda i: (i, 0)),
                 pl.BlockSpec((SUBLANES, LANES), lambda i: (0, 0)),
                 pl.BlockSpec((QB, LANES), lambda i: (i, gate_cb)),
                 pl.BlockSpec((QB, NSA_WIDTH), lambda i: (i, gout_cb)),
                 pl.BlockSpec((QB, NSA_WIDTH), lambda i: (i, 0)),
                 pl.BlockSpec((QB, NSA_WIDTH), lambda i: (i, 0))]
    args = [proj_b] * (1 + 2 * nt) + [pos_row] * nt + [pos_col, tab_t, proj_f, proj_f, oc, o_s]
    return pl.pallas_call(
        _nsa_win_kernel,
        grid=(seq // QB,),
        in_specs=in_specs,
        out_specs=pl.BlockSpec((QB, NSA_WIDTH), lambda i: (i, 0)),
        out_shape=jax.ShapeDtypeStruct((seq, NSA_WIDTH), BF16),
        compiler_params=_cparams("parallel"),
        name="nsa_win",
    )(*args)


def _in_proj(h, norm_g, w_b, w_f, tiles, name):
    proj_b = _norm_matmul(h, norm_g, w_b, tm=tiles["tm"], tn=512, name=name + "_b", out_dtype=BF16)
    proj_f = _norm_matmul(h, norm_g, w_f, tm=tiles["tm"], tn=768, name=name + "_f")
    return proj_b, proj_f


def _even_layer(h, mem_kv, norm_g, w_in, s5, w_glu, b_f, tiles):
    pb, pf = _in_proj(h, norm_g, *w_in, tiles, "in_proj_even")
    ob, of = EVEN_B_OFF, EVEN_F_OFF
    b_cat, c_cat, tab, d_skip = s5
    z = _s5_scan(pf, of["u"] // LANES, b_cat, c_cat, d_skip, tab, tc=tiles["s5_tc"])
    y_s5 = _s5_glu(z, w_glu, pf, of["g_s5"] // S5_WIDTH, tm=tiles["tm"], tn=512)
    decay = _decay(pf, of["f"] // LANES, b_f, t=tiles["decay_t"])
    d = FOX_HEAD_DIM
    y_fox = _flash(pb, ob["q"], pb, ob["k"], pb, ob["v"], pf, of["g_fox"], heads=FOX_HEADS, dk=d, dv=d,
                   scale=d ** -0.5, tq=tiles["attn_tq"], tk=tiles["attn_tk"], hp=tiles["attn_hp"], decay=decay,
                   name="fox_attn")
    y_mem = _mem_attn(pb, ob["q_mem"] // MEM_HEAD_DIM, pf, of["g_mem"] // MEM_HEAD_DIM, mem_kv, t=tiles["mem_t"])
    return y_s5, y_fox, y_mem


def _odd_layer(h, mem_kv, norm_g, w_in, mla, nsa, pos, tiles):
    pb, pf = _in_proj(h, norm_g, *w_in, tiles, "in_proj_odd")
    ob, of = ODD_B_OFF, ODD_F_OFF
    g_cq, g_ckv, w_uq, w_ukv, freq = mla
    pos_col, pos_col_f, pos_row, pos_cmp, pos_rep, pq_min, pk_max = pos
    qf = _norm_matmul(pf, g_cq, w_uq, x_cb=of["c_q"] // MLA_Q_RANK, tm=tiles["tm"], tn=512, name="mla_q_up")
    kvf = _norm_matmul(pf, g_ckv, w_ukv, x_cb=of["c_kv"] // MLA_KV_RANK, tm=tiles["tm"], tn=512, name="mla_kv_up")
    q_r, k_r, v_r = _mla_prep(qf, kvf, pf, of["k_rope"] // LANES, pos_col_f, freq, t=tiles["prep_t"])
    y_mla = _flash(q_r, 0, k_r, 0, v_r, 0, pf, of["g_mla"], heads=MLA_HEADS, dk=2 * LANES, dv=MLA_V,
                   scale=(MLA_NOPE + MLA_ROPE) ** -0.5, tq=tiles["attn_tq"], tk=tiles["attn_tk"], hp=tiles["attn_hp"],
                   name="mla_attn")
    pe, w1, w2, tab_t, ov, n_slc = nsa
    kvc = _nsa_compress(pb, ob["k_cmp"] // NSA_HEAD_DIM, pe, w1, w2)
    oc, sel = _nsa_select(pb, ob["q_nsa"] // NSA_WIDTH, pf, of["gates"] // LANES, kvc[0], kvc[1], pos_col, pos_cmp,
                          tab_t, ov, n_slc=n_slc)
    o_s = _nsa_slc(pb, ob["q_nsa"], ob["k_slc"], ob["v_slc"], sel, pos_row, pos_rep, tab_t, pq_min, pk_max,
                   tk=tiles["slc_tk"])
    y_nsa = _nsa_win(pb, ob["q_nsa"] // NSA_WIDTH, ob["k_win"] // NSA_KV, ob["v_win"] // NSA_KV,
                     pf, of["gates"] // LANES, of["g_nsa"] // NSA_WIDTH, oc, o_s, pos_col, pos_row, tab_t)
    y_mem = _mem_attn(pb, ob["q_mem"] // MEM_HEAD_DIM, pf, of["g_mem"] // MEM_HEAD_DIM, mem_kv, t=tiles["mem_t"])
    return y_mla, y_nsa, y_mem


def _tiles(seq):
    return {"tm": min(seq, 1024), "s5_tc": min(seq, 1024), "decay_t": min(seq, 512), "attn_tq": min(seq, 256),
            "attn_tk": min(seq, 512), "attn_hp": 2, "slc_tk": 256, "mem_t": min(seq, 512), "prep_t": min(seq, 256)}


def _context(positions, t5_table, seq, tiles):
    pos = positions[0]
    pos_col = pos.reshape(seq, 1)
    pos_row = pos.reshape(1, seq)
    pos_rep = jnp.broadcast_to(pos_col, (seq, LANES))
    pq_min = jnp.min(pos.reshape(seq // QB, QB), axis=1)
    pk_max = jnp.max(pos.reshape(seq // tiles["slc_tk"], tiles["slc_tk"]), axis=1)
    nc = seq // NSA_CMP_STRIDE
    pos_cmp = jnp.pad(pos[NSA_CMP_LEN - 1::NSA_CMP_STRIDE], (0, 1)).reshape(1, nc)
    half = MLA_ROPE // 2
    inv_freq = ROPE_THETA ** (-jnp.arange(half, dtype=F32) / half)
    freq = jnp.concatenate([inv_freq, inv_freq, jnp.zeros((LANES - MLA_ROPE,), F32)]).reshape(1, LANES)
    tab_t = jnp.pad(t5_table.astype(F32).T, ((0, SUBLANES - NSA_HEADS), (0, LANES - T5_BUCKETS)))
    n_slc = seq // NSA_SLC_BLOCK
    cs = np.arange(nc) * NSA_CMP_STRIDE
    ss = np.arange(LANES) * NSA_SLC_BLOCK
    ov_np = np.clip(np.minimum(cs[:, None] + NSA_CMP_LEN, ss[None, :] + NSA_SLC_BLOCK)
                    - np.maximum(cs[:, None], ss[None, :]), 0, None) / NSA_CMP_LEN
    ov_np[nc - 1, :] = 0.0
    ov_np[:, n_slc:] = 0.0
    return {"pos": (pos_col, pos_col.astype(F32), pos_row, pos_cmp, pos_rep, pq_min, pk_max), "freq": freq,
            "tab_t": tab_t,
            "ov": jnp.asarray(ov_np, BF16), "n_slc": n_slc}


def _odd_params(i, mla_g_cq, mla_g_ckv, mla_w_uq, mla_w_ukv, nsa_cmp_pe, nsa_cmp_w1, nsa_cmp_w2, ctx):
    dq = MLA_NOPE + MLA_ROPE
    w_uq = mla_w_uq[i].reshape(MLA_Q_RANK, MLA_HEADS, dq)
    w_uq = jnp.pad(w_uq, ((0, 0), (0, 0), (0, 2 * LANES - dq))).reshape(MLA_Q_RANK, -1).astype(BF16)
    mla = (mla_g_cq[i], mla_g_ckv[i], w_uq, mla_w_ukv[i].astype(BF16), ctx["freq"])
    nsa = (nsa_cmp_pe[i].astype(F32), nsa_cmp_w1[i].astype(BF16), nsa_cmp_w2[i].astype(BF16), ctx["tab_t"],
           ctx["ov"], ctx["n_slc"])
    return mla, nsa


def kernel(x, mem, positions, norm_g, mem_norm_g, final_norm_g, t5_table, w_out, mem_w_kv, even_w_in, s5_lam_re,
           s5_lam_im, s5_log_dt, s5_b_re, s5_b_im, s5_c_re, s5_c_im, s5_d, s5_w_glu, fox_b_f, odd_w_in, mla_g_cq,
           mla_g_ckv, mla_w_uq, mla_w_ukv, nsa_cmp_pe, nsa_cmp_w1, nsa_cmp_w2):
    batch, seq, _ = x.shape
    assert batch == 1 and seq % 1024 == 0 and seq // NSA_SLC_BLOCK <= LANES
    depth = norm_g.shape[0]
    tiles = _tiles(seq)
    ctx = _context(positions, t5_table, seq, tiles)
    h = x[0]
    mem_w = mem_w_kv.astype(BF16)
    w_out_b = w_out.astype(BF16)
    for layer in range(depth):
        i = layer // 2
        mem_kv = _norm_matmul(mem[0], mem_norm_g, mem_w[layer], tm=mem.shape[1], tn=512, name="mem_kv")
        if layer % 2 == 0:
            w_in = (_reorder_w_in(even_w_in[i], EVEN_SPLITS, EVEN_B_ORDER),
                    _reorder_w_in(even_w_in[i], EVEN_SPLITS, EVEN_F_ORDER))
            b_cat, c_cat, tab = _s5_prepare(s5_lam_re[i], s5_lam_im[i], s5_log_dt[i], s5_b_re[i], s5_b_im[i],
                                            s5_c_re[i], s5_c_im[i])
            ys = _even_layer(h, mem_kv, norm_g[layer], w_in, (b_cat, c_cat, tab, s5_d[i]),
                             s5_w_glu[i].astype(BF16), fox_b_f[i], tiles)
            widths = (S5_WIDTH, FOX_WIDTH, MEM_WIDTH)
        else:
            w_in = (_reorder_w_in(odd_w_in[i], ODD_SPLITS, ODD_B_ORDER),
                    _reorder_w_in(odd_w_in[i], ODD_SPLITS, ODD_F_ORDER))
            mla, nsa = _odd_params(i, mla_g_cq, mla_g_ckv, mla_w_uq, mla_w_ukv, nsa_cmp_pe, nsa_cmp_w1, nsa_cmp_w2,
                                   ctx)
            ys = _odd_layer(h, mem_kv, norm_g[layer], w_in, mla, nsa, ctx["pos"], tiles)
            widths = (MLA_WIDTH, NSA_WIDTH, MEM_WIDTH)
        starts = np.concatenate([[0], np.cumsum(widths)])
        ws = [w_out_b[layer, int(starts[k]):int(starts[k + 1])] for k in range(3)]
        h = _out_proj(h, ys, ws, tm=tiles["tm"], tn=512)
    return _final_norm(h, final_norm_g, tm=tiles["mem_t"])[None]
```

```python
import functools
import math

import numpy as np
import jax
import jax.numpy as jnp
from jax import lax
from jax.experimental import pallas as pl
from jax.experimental.pallas import tpu as pltpu

F32 = jnp.float32
BF16 = jnp.bfloat16
I32 = jnp.int32

D_MODEL = 2048
DEPTH = 4
N_MEM = 256
RMS_EPS = 1e-6
NEG_INF = -1e30
LOG2E = math.log2(math.e)

S5_WIDTH = 1024
S5_GROUP = 16
S5_GROUPS = S5_WIDTH // S5_GROUP
S5_STATE = 64
FOX_HEADS = 8
FOX_HEAD_DIM = 128
FOX_WIDTH = FOX_HEADS * FOX_HEAD_DIM
MEM_HEADS = 4
MEM_HEAD_DIM = 128
MEM_WIDTH = MEM_HEADS * MEM_HEAD_DIM
MLA_HEADS = 8
MLA_Q_RANK = 512
MLA_KV_RANK = 512
MLA_NOPE = 128
MLA_ROPE = 64
MLA_V = 128
MLA_WIDTH = MLA_HEADS * MLA_V
ROPE_THETA = 10000.0
NSA_HEADS = 8
NSA_KV_GROUPS = 2
NSA_REP = NSA_HEADS // NSA_KV_GROUPS
NSA_HEAD_DIM = 128
NSA_WIDTH = NSA_HEADS * NSA_HEAD_DIM
NSA_KV = NSA_KV_GROUPS * NSA_HEAD_DIM
NSA_CMP_LEN = 32
NSA_CMP_STRIDE = 16
NSA_CMP_HIDDEN = 256
NSA_SLC_BLOCK = 64
NSA_SLC_TOPK = 16
NSA_WINDOW = 512
FORCE_SCORE = 1e6
T5_BUCKETS = 32
T5_MAX_DIST = 1024

EVEN_SPLITS = (S5_WIDTH, S5_WIDTH, FOX_WIDTH, FOX_WIDTH, FOX_WIDTH, FOX_HEADS, FOX_WIDTH, MEM_WIDTH, MEM_WIDTH)
ODD_SPLITS = (MLA_Q_RANK, MLA_KV_RANK, MLA_ROPE, MLA_WIDTH, NSA_WIDTH, NSA_KV, NSA_KV, NSA_KV, NSA_KV, NSA_KV,
              NSA_KV, 3 * NSA_HEADS, NSA_WIDTH, MEM_WIDTH, MEM_WIDTH)

LANES = 128
SUBLANES = 8
QB = 128
VMEM_LIMIT_BYTES = 56 * 1024 * 1024

EVEN_B_ORDER = (("q", 2, 1024), ("k", 3, 1024), ("v", 4, 1024), ("q_mem", 7, 512))
EVEN_F_ORDER = (("u", 0, 1024), ("g_s5", 1, 1024), ("g_fox", 6, 1024), ("g_mem", 8, 512), ("f", 5, 128),
                ("pad", None, 128))
ODD_B_ORDER = (("q_nsa", 4, 1024), ("q_mem", 13, 512), ("k_cmp", 5, 256), ("v_cmp", 6, 256), ("k_slc", 7, 256),
               ("v_slc", 8, 256), ("k_win", 9, 256), ("v_win", 10, 256))
ODD_F_ORDER = (("g_nsa", 12, 1024), ("g_mla", 3, 1024), ("c_q", 0, 512), ("c_kv", 1, 512), ("g_mem", 14, 512),
               ("k_rope", 2, 128), ("gates", 11, 128))


def _layout(order):
    off, out = 0, {}
    for name, _, width in order:
        assert off % width == 0
        out[name] = off
        off += width
    return out, off


EVEN_B_OFF, EVEN_B_N = _layout(EVEN_B_ORDER)
EVEN_F_OFF, EVEN_F_N = _layout(EVEN_F_ORDER)
ODD_B_OFF, ODD_B_N = _layout(ODD_B_ORDER)
ODD_F_OFF, ODD_F_N = _layout(ODD_F_ORDER)


def _reorder_w_in(w, splits, order):
    starts = np.concatenate([[0], np.cumsum(splits)])
    cols = []
    for _, idx, width in order:
        if idx is None:
            cols.append(jnp.zeros((w.shape[0], width), w.dtype))
            continue
        seg = w[:, int(starts[idx]):int(starts[idx + 1])]
        pad = width - seg.shape[1]
        if pad:
            seg = jnp.pad(seg, ((0, 0), (0, pad)))
        cols.append(seg)
    return jnp.concatenate(cols, axis=1).astype(BF16)


def _cparams(*sem):
    return pltpu.CompilerParams(dimension_semantics=sem, vmem_limit_bytes=VMEM_LIMIT_BYTES)


def _silu(g):
    return g * jax.nn.sigmoid(g)


def _pick(n, cands):
    for c in cands:
        if n % c == 0:
            return c
    raise ValueError(f"no tile for {n} in {cands}")


def _norm_matmul_kernel(x_ref, g_ref, w_ref, o_ref, xn_ref):
    @pl.when(pl.program_id(1) == 0)
    def _():
        x = x_ref[...]
        ms = jnp.mean(x * x, axis=-1, keepdims=True)
        xn_ref[...] = (x * lax.rsqrt(ms + RMS_EPS) * g_ref[...]).astype(BF16)

    o_ref[...] = jnp.dot(xn_ref[...], w_ref[...], preferred_element_type=F32).astype(o_ref.dtype)


def _norm_matmul(x, g, w, *, x_cb=0, tm, tn, name, out_dtype=F32):
    m = x.shape[0]
    k, n = w.shape
    return pl.pallas_call(
        _norm_matmul_kernel,
        grid=(m // tm, n // tn),
        in_specs=[pl.BlockSpec((tm, k), lambda i, j: (i, x_cb)),
                  pl.BlockSpec((1, k), lambda i, j: (0, 0)),
                  pl.BlockSpec((k, tn), lambda i, j: (0, j))],
        out_specs=pl.BlockSpec((tm, tn), lambda i, j: (i, j)),
        out_shape=jax.ShapeDtypeStruct((m, n), out_dtype),
        scratch_shapes=[pltpu.VMEM((tm, k), BF16)],
        compiler_params=_cparams("parallel", "arbitrary"),
        name=name,
    )(x, g.reshape(1, k), w)


def _out_proj_kernel(h_ref, *refs):
    o_ref = refs[-1]
    n = (len(refs) - 1) // 2
    acc = h_ref[...]
    for y_ref, w_ref in zip(refs[:n], refs[n:2 * n]):
        acc = acc + jnp.dot(y_ref[...], w_ref[...], preferred_element_type=F32)
    o_ref[...] = acc


def _out_proj(h, ys, ws, *, tm, tn):
    m, n = h.shape
    in_specs = [pl.BlockSpec((tm, tn), lambda i, j: (i, j))]
    in_specs += [pl.BlockSpec((tm, y.shape[1]), lambda i, j: (i, 0)) for y in ys]
    in_specs += [pl.BlockSpec((w.shape[0], tn), lambda i, j: (0, j)) for w in ws]
    return pl.pallas_call(
        _out_proj_kernel,
        grid=(m // tm, n // tn),
        in_specs=in_specs,
        out_specs=pl.BlockSpec((tm, tn), lambda i, j: (i, j)),
        out_shape=jax.ShapeDtypeStruct((m, n), F32),
        compiler_params=_cparams("parallel", "arbitrary"),
        name="out_proj",
    )(h, *ys, *ws)


def _final_norm_kernel(x_ref, g_ref, o_ref):
    x = x_ref[...]
    ms = jnp.mean(x * x, axis=-1, keepdims=True)
    o_ref[...] = x * lax.rsqrt(ms + RMS_EPS) * g_ref[...]


def _final_norm(h, g, *, tm):
    m, n = h.shape
    return pl.pallas_call(
        _final_norm_kernel,
        grid=(m // tm,),
        in_specs=[pl.BlockSpec((tm, n), lambda i: (i, 0)), pl.BlockSpec((1, n), lambda i: (0, 0))],
        out_specs=pl.BlockSpec((tm, n), lambda i: (i, 0)),
        out_shape=jax.ShapeDtypeStruct((m, n), F32),
        compiler_params=_cparams("parallel"),
        name="final_norm",
    )(h, g.reshape(1, n))


ONES_ROWS = 16


def _build_vt(v_ref, vt_ref, hh, dv, seq, chunk):
    def body(c, carry):
        st = pl.multiple_of(c * chunk, chunk)
        vt_ref[hh, 0:dv, pl.ds(st, chunk)] = v_ref[pl.ds(st, chunk), hh * dv:(hh + 1) * dv].astype(F32).T.astype(BF16)
        return carry

    lax.fori_loop(0, seq // chunk, body, 0)
    vt_ref[hh, dv:dv + ONES_ROWS, :] = jnp.ones((ONES_ROWS, seq), BF16)


def _flash_kernel(*refs, scale, tq, tk, hp, dk, dv, has_decay):
    if has_decay:
        q_ref, k_ref, v_ref, g_ref, cq_ref, ck_ref, o_ref, vt_ref, qt_ref, s_ref, m_ref, acc_ref = refs
    else:
        q_ref, k_ref, v_ref, g_ref, o_ref, vt_ref, qt_ref, s_ref, m_ref, acc_ref = refs
    hb = pl.program_id(0)
    qi = pl.program_id(1)
    seq = k_ref.shape[0]

    @pl.when(qi == 0)
    def _():
        for hh in range(hp):
            _build_vt(v_ref, vt_ref, hh, dv, seq, tk)

    for hh in range(hp):
        qt_ref[hh] = (q_ref[:, hh * dk:(hh + 1) * dk].astype(F32) * (scale * LOG2E)).T.astype(BF16)
    m_ref[...] = jnp.full(m_ref.shape, NEG_INF, F32)
    acc_ref[...] = jnp.zeros(acc_ref.shape, F32)
    n_full = (qi * tq) // tk
    if has_decay:
        cq2 = [cq_ref[pl.ds(hb * hp + hh, 1), :] for hh in range(hp)]

    def scores(j, slot):
        start = pl.multiple_of(j * tk, tk)
        for hh in range(hp):
            s_ref[slot, hh] = jnp.dot(k_ref[pl.ds(start, tk), hh * dk:(hh + 1) * dk], qt_ref[hh],
                                      preferred_element_type=F32)

    def softmax_pv(j, slot, masked):
        start = pl.multiple_of(j * tk, tk)
        for hh in range(hp):
            t = s_ref[slot, hh]
            if has_decay:
                t = t - jnp.concatenate([ck_ref[hh, pl.ds(start, tk), :]] * (tq // LANES), axis=1)
            if masked:
                key = start + lax.broadcasted_iota(I32, (tk, tq), 0)
                qry = qi * tq + lax.broadcasted_iota(I32, (tk, tq), 1)
                t = jnp.where(key <= qry, t, NEG_INF)
            m_prev = m_ref[hh]
            mx = jnp.max(t, axis=0, keepdims=True)
            if has_decay:
                m_new = jnp.maximum(m_prev, mx + cq2[hh])
                shift = m_new - cq2[hh]
            else:
                m_new = jnp.maximum(m_prev, mx)
                shift = m_new
            alpha = jnp.exp2(m_prev - m_new)
            p = jnp.exp2(t - shift).astype(BF16)
            acc_ref[hh] = alpha * acc_ref[hh] + jnp.dot(vt_ref[hh, :, pl.ds(start, tk)], p,
                                                         preferred_element_type=F32)
            m_ref[hh] = m_new

    scores(0, 0)

    def pair(jj, carry):
        j = 2 * jj
        scores(j + 1, 1)
        softmax_pv(j, 0, False)
        scores(j + 2, 0)
        softmax_pv(j + 1, 1, False)
        return carry

    pairs = n_full // 2
    lax.fori_loop(0, pairs, pair, 0)
    last = 2 * pairs

    @pl.when(n_full % 2 == 1)
    def _():
        scores(last + 1, 1)
        softmax_pv(last, 0, False)
        softmax_pv(last + 1, 1, True)

    @pl.when(n_full % 2 == 0)
    def _():
        softmax_pv(last, 0, True)

    for hh in range(hp):
        a = acc_ref[hh]
        o = (a[0:dv, :] / a[dv:dv + 1, :]).T
        o_ref[:, hh * dv:(hh + 1) * dv] = (o * _silu(g_ref[:, hh * dv:(hh + 1) * dv])).astype(o_ref.dtype)


def _flash(q_arr, q_off, k_arr, k_off, v_arr, v_off, g_arr, g_off, *, heads, dk, dv, scale, tq, tk, hp,
           decay=None, name):
    seq = q_arr.shape[0]
    once = pl.Buffered(1)
    in_specs = [pl.BlockSpec((tq, hp * dk), lambda h, i: (i, q_off // (hp * dk) + h)),
                pl.BlockSpec((seq, hp * dk), lambda h, i: (0, k_off // (hp * dk) + h), pipeline_mode=once),
                pl.BlockSpec((seq, hp * dv), lambda h, i: (0, v_off // (hp * dv) + h), pipeline_mode=once),
                pl.BlockSpec((tq, hp * dv), lambda h, i: (i, g_off // (hp * dv) + h))]
    args = [q_arr, k_arr, v_arr, g_arr]
    if decay is not None:
        cum_t, cum_rep = decay
        in_specs += [pl.BlockSpec((SUBLANES, tq), lambda h, i: (0, i)),
                     pl.BlockSpec((hp, seq, LANES), lambda h, i: (h, 0, 0), pipeline_mode=once)]
        args += [cum_t, cum_rep]
    return pl.pallas_call(
        functools.partial(_flash_kernel, scale=scale, tq=tq, tk=tk, hp=hp, dk=dk, dv=dv, has_decay=decay is not None),
        grid=(heads // hp, seq // tq),
        in_specs=in_specs,
        out_specs=pl.BlockSpec((tq, hp * dv), lambda h, i: (i, h)),
        out_shape=jax.ShapeDtypeStruct((seq, heads * dv), BF16),
        scratch_shapes=[pltpu.VMEM((hp, dv + ONES_ROWS, seq), BF16), pltpu.VMEM((hp, dk, tq), BF16),
                        pltpu.VMEM((2, hp, tk, tq), F32), pltpu.VMEM((hp, 1, tq), F32),
                        pltpu.VMEM((hp, dv + ONES_ROWS, tq), F32)],
        compiler_params=_cparams("parallel", "arbitrary"),
        name=name,
    )(*args)


def _decay_kernel(f_ref, b_ref, ct_ref, cr_ref, carry_ref, *, t):
    i = pl.program_id(0)

    @pl.when(i == 0)
    def _():
        carry_ref[...] = jnp.zeros(carry_ref.shape, F32)

    x = f_ref[...] + b_ref[...]
    lf = jnp.minimum(x, 0.0) - jnp.log1p(jnp.exp(-jnp.abs(x)))
    row = lax.broadcasted_iota(I32, lf.shape, 0)
    s = 1
    while s < t:
        lf = lf + jnp.where(row >= s, pltpu.roll(lf, s, 0), 0.0)
        s *= 2
    lf = lf + carry_ref[...]
    carry_ref[...] = lf[t - 1:t, :]
    lf2 = lf * LOG2E
    ct_ref[...] = lf2.T[:FOX_HEADS, :]
    for h in range(FOX_HEADS):
        cr_ref[h] = jnp.broadcast_to(lf2[:, h:h + 1], (t, LANES))


def _decay(proj, f_cb, b_f, *, t):
    seq = proj.shape[0]
    b = jnp.pad(b_f.reshape(1, FOX_HEADS), ((0, 0), (0, LANES - FOX_HEADS)))
    return pl.pallas_call(
        functools.partial(_decay_kernel, t=t),
        grid=(seq // t,),
        in_specs=[pl.BlockSpec((t, LANES), lambda i: (i, f_cb)), pl.BlockSpec((1, LANES), lambda i: (0, 0))],
        out_specs=[pl.BlockSpec((FOX_HEADS, t), lambda i: (0, i)),
                   pl.BlockSpec((FOX_HEADS, t, LANES), lambda i: (0, i, 0))],
        out_shape=[jax.ShapeDtypeStruct((FOX_HEADS, seq), F32), jax.ShapeDtypeStruct((FOX_HEADS, seq, LANES), F32)],
        scratch_shapes=[pltpu.VMEM((1, LANES), F32)],
        compiler_params=_cparams("arbitrary"),
        name="fox_decay",
    )(proj, b)


def _mem_attn_kernel(q_ref, k_ref, v_ref, g_ref, o_ref):
    k = k_ref[...].astype(BF16)
    s = lax.dot_general(q_ref[...], k, (((1,), (1,)), ((), ())), preferred_element_type=F32) * (MEM_HEAD_DIM ** -0.5)
    m = jnp.max(s, axis=1, keepdims=True)
    p = jnp.exp(s - m)
    l = jnp.sum(p, axis=1, keepdims=True)
    o = jnp.dot(p.astype(BF16), v_ref[...].astype(BF16), preferred_element_type=F32) / l
    o_ref[...] = (o * _silu(g_ref[...])).astype(o_ref.dtype)


def _mem_attn(proj_b, q_cb, proj_f, g_cb, mem_kv, *, t):
    seq = proj_b.shape[0]
    nm = mem_kv.shape[0]
    d = MEM_HEAD_DIM
    return pl.pallas_call(
        _mem_attn_kernel,
        grid=(MEM_HEADS, seq // t),
        in_specs=[pl.BlockSpec((t, d), lambda h, i: (i, q_cb + h)),
                  pl.BlockSpec((nm, d), lambda h, i: (0, h)),
                  pl.BlockSpec((nm, d), lambda h, i: (0, MEM_HEADS + h)),
                  pl.BlockSpec((t, d), lambda h, i: (i, g_cb + h))],
        out_specs=pl.BlockSpec((t, d), lambda h, i: (i, h)),
        out_shape=jax.ShapeDtypeStruct((seq, MEM_WIDTH), BF16),
        compiler_params=_cparams("parallel", "arbitrary"),
        name="mem_attn",
    )(proj_b, mem_kv, mem_kv, proj_f)


S5_TILE_GROUPS = LANES // S5_GROUP
S5_TILE_STATES = S5_TILE_GROUPS * S5_STATE
S5_TILES = S5_GROUPS // S5_TILE_GROUPS


def _s5_scan_kernel(u_ref, b_ref, c_ref, d_ref, tab_ref, z_ref, bu_ref, carry_ref, *, tc):
    ns = S5_TILE_STATES

    @pl.when(pl.program_id(1) == 0)
    def _():
        carry_ref[...] = jnp.zeros(carry_ref.shape, F32)

    u = u_ref[...]
    bu_ref[...] = jnp.dot(u.astype(BF16), b_ref[...], preferred_element_type=F32)
    steps = [(1, tab_ref[0], tab_ref[1]), (2, tab_ref[2], tab_ref[3]), (4, tab_ref[4], tab_ref[5])]
    pr = tab_ref[6]
    pi = tab_ref[7]

    def body(i, carry):
        cr, ci = carry
        r0 = pl.multiple_of(i * SUBLANES, SUBLANES)
        xr = bu_ref[pl.ds(r0, SUBLANES), 0:ns]
        xi = bu_ref[pl.ds(r0, SUBLANES), ns:2 * ns]
        for s, ar, ai in steps:
            sr = pltpu.roll(xr, s, 0)
            si = pltpu.roll(xi, s, 0)
            xr, xi = xr + ar * sr - ai * si, xi + ar * si + ai * sr
        xr, xi = xr + pr * cr - pi * ci, xi + pr * ci + pi * cr
        bu_ref[pl.ds(r0, SUBLANES), 0:ns] = xr
        bu_ref[pl.ds(r0, SUBLANES), ns:2 * ns] = xi
        return xr[SUBLANES - 1:SUBLANES, :], xi[SUBLANES - 1:SUBLANES, :]

    cr, ci = lax.fori_loop(0, tc // SUBLANES, body, (carry_ref[0:1, 0:ns], carry_ref[0:1, ns:2 * ns]))
    carry_ref[0:1, 0:ns] = cr
    carry_ref[0:1, ns:2 * ns] = ci
    y = jnp.dot(bu_ref[...].astype(BF16), c_ref[...], preferred_element_type=F32) + d_ref[...] * u
    z_ref[...] = jax.nn.gelu(y)


def _s5_prepare(lam_re, lam_im, log_dt, b_re, b_im, c_re, c_im):
    dt = jnp.exp(log_dt.astype(F32))[:, None]
    lr = lam_re.astype(F32)
    li = lam_im.astype(F32)
    mag = jnp.exp(lr * dt)
    ab_re = mag * jnp.cos(li * dt)
    ab_im = mag * jnp.sin(li * dt)
    den = lr * lr + li * li
    nr = ab_re - 1.0
    f_re = (nr * lr + ab_im * li) / den
    f_im = (ab_im * lr - nr * li) / den
    br = b_re.astype(F32)
    bim = b_im.astype(F32)
    bb_re = f_re[..., None] * br - f_im[..., None] * bim
    bb_im = f_re[..., None] * bim + f_im[..., None] * br
    eye = jnp.eye(S5_TILE_GROUPS, dtype=F32)

    def blockdiag_in(bb):
        t = bb.reshape(S5_TILES, S5_TILE_GROUPS, S5_STATE, S5_GROUP)
        m = jnp.einsum("jgpc,gh->jgchp", t, eye)
        return m.reshape(S5_TILES, LANES, S5_TILE_STATES)

    def blockdiag_out(cc):
        t = cc.reshape(S5_TILES, S5_TILE_GROUPS, S5_GROUP, S5_STATE)
        m = jnp.einsum("jgcp,gh->jgphc", t, eye)
        return m.reshape(S5_TILES, S5_TILE_STATES, LANES)

    b_cat = jnp.concatenate([blockdiag_in(bb_re), blockdiag_in(bb_im)], axis=2).astype(BF16)
    c_cat = jnp.concatenate([blockdiag_out(c_re.astype(F32)), -blockdiag_out(c_im.astype(F32))], axis=1).astype(BF16)

    a_r = ab_re.reshape(S5_TILES, 1, S5_TILE_STATES)
    a_i = ab_im.reshape(S5_TILES, 1, S5_TILE_STATES)

    def cmul(xr, xi, yr, yi):
        return xr * yr - xi * yi, xr * yi + xi * yr

    a2 = cmul(a_r, a_i, a_r, a_i)
    a4 = cmul(*a2, *a2)
    row = jnp.arange(SUBLANES)[None, :, None]
    tabs = []
    for s, (pr_, pi_) in ((1, (a_r, a_i)), (2, a2), (4, a4)):
        tabs.append(jnp.where(row >= s, pr_, 0.0))
        tabs.append(jnp.where(row >= s, pi_, 0.0))
    pw = [(a_r, a_i)]
    for _ in range(SUBLANES - 1):
        pw.append(cmul(*pw[-1], a_r, a_i))
    tabs.append(jnp.concatenate([p[0] for p in pw], axis=1))
    tabs.append(jnp.concatenate([p[1] for p in pw], axis=1))
    tab = jnp.stack([jnp.broadcast_to(t, (S5_TILES, SUBLANES, S5_TILE_STATES)) for t in tabs], axis=1)
    return b_cat, c_cat, tab.astype(F32)


def _s5_scan(proj, u_cb, b_cat, c_cat, d_skip, tab, *, tc):
    seq = proj.shape[0]
    ns = S5_TILE_STATES
    d = d_skip.astype(F32).reshape(S5_TILES, 1, LANES)
    return pl.pallas_call(
        functools.partial(_s5_scan_kernel, tc=tc),
        grid=(S5_TILES, seq // tc),
        in_specs=[pl.BlockSpec((tc, LANES), lambda j, c: (c, u_cb + j)),
                  pl.BlockSpec((None, LANES, 2 * ns), lambda j, c: (j, 0, 0)),
                  pl.BlockSpec((None, 2 * ns, LANES), lambda j, c: (j, 0, 0)),
                  pl.BlockSpec((None, 1, LANES), lambda j, c: (j, 0, 0)),
                  pl.BlockSpec((None, 8, SUBLANES, ns), lambda j, c: (j, 0, 0, 0))],
        out_specs=pl.BlockSpec((tc, LANES), lambda j, c: (c, j)),
        out_shape=jax.ShapeDtypeStruct((seq, S5_WIDTH), F32),
        scratch_shapes=[pltpu.VMEM((tc, 2 * ns), F32), pltpu.VMEM((SUBLANES, 2 * ns), F32)],
        compiler_params=_cparams("parallel", "arbitrary"),
        name="s5_scan",
    )(proj, b_cat, c_cat, d, tab)


def _s5_glu_kernel(z_ref, w_ref, g_ref, o_ref, *, tn):
    j = pl.program_id(1)
    z = z_ref[...]
    a = jnp.dot(z.astype(BF16), w_ref[...], preferred_element_type=F32)
    zc = z_ref[:, pl.ds(pl.multiple_of(j * tn, tn), tn)]
    o_ref[...] = (zc * jax.nn.sigmoid(a) * _silu(g_ref[...])).astype(o_ref.dtype)


def _s5_glu(z, w_glu, proj, g_cb, *, tm, tn):
    seq, n = z.shape
    return pl.pallas_call(
        functools.partial(_s5_glu_kernel, tn=tn),
        grid=(seq // tm, n // tn),
        in_specs=[pl.BlockSpec((tm, n), lambda i, j: (i, 0)),
                  pl.BlockSpec((n, tn), lambda i, j: (0, j)),
                  pl.BlockSpec((tm, tn), lambda i, j: (i, g_cb * (n // tn) + j))],
        out_specs=pl.BlockSpec((tm, tn), lambda i, j: (i, j)),
        out_shape=jax.ShapeDtypeStruct((seq, n), BF16),
        compiler_params=_cparams("parallel", "arbitrary"),
        name="s5_glu",
    )(z, w_glu, proj)


def _rope_tables(pos, freq):
    ang = pos * freq
    lane = lax.broadcasted_iota(I32, ang.shape, 1)
    half = MLA_ROPE // 2
    cos = jnp.cos(ang)
    sin = jnp.sin(ang)
    c = jnp.where(lane < MLA_ROPE, cos, 0.0)
    s1 = jnp.where(lane < half, -sin, 0.0)
    s2 = jnp.where((lane >= half) & (lane < MLA_ROPE), sin, 0.0)
    return c, s1, s2


def _rope_apply(x, c, s1, s2):
    half = MLA_ROPE // 2
    return x * c + pltpu.roll(x, LANES - half, 1) * s1 + pltpu.roll(x, half, 1) * s2


def _mla_prep_kernel(qf_ref, kvf_ref, kr_ref, pos_ref, freq_ref, q_ref, k_ref, v_ref):
    c, s1, s2 = _rope_tables(pos_ref[...], freq_ref[...])
    kr = _rope_apply(kr_ref[...], c, s1, s2).astype(BF16)
    for h in range(MLA_HEADS):
        b = 2 * LANES * h
        q_ref[:, b:b + LANES] = qf_ref[:, b:b + LANES].astype(BF16)
        q_ref[:, b + LANES:b + 2 * LANES] = _rope_apply(qf_ref[:, b + LANES:b + 2 * LANES], c, s1, s2).astype(BF16)
        k_ref[:, b:b + LANES] = kvf_ref[:, b:b + LANES].astype(BF16)
        k_ref[:, b + LANES:b + 2 * LANES] = kr
        v_ref[:, LANES * h:LANES * (h + 1)] = kvf_ref[:, b + LANES:b + 2 * LANES].astype(BF16)


def _mla_prep(qf, kvf, proj, kr_cb, pos_col, freq, *, t):
    seq = qf.shape[0]
    w = 2 * LANES * MLA_HEADS
    return pl.pallas_call(
        _mla_prep_kernel,
        grid=(seq // t,),
        in_specs=[pl.BlockSpec((t, w), lambda i: (i, 0)),
                  pl.BlockSpec((t, w), lambda i: (i, 0)),
                  pl.BlockSpec((t, LANES), lambda i: (i, kr_cb)),
                  pl.BlockSpec((t, 1), lambda i: (i, 0)),
                  pl.BlockSpec((1, LANES), lambda i: (0, 0))],
        out_specs=[pl.BlockSpec((t, w), lambda i: (i, 0)),
                   pl.BlockSpec((t, w), lambda i: (i, 0)),
                   pl.BlockSpec((t, MLA_WIDTH), lambda i: (i, 0))],
        out_shape=[jax.ShapeDtypeStruct((seq, w), BF16), jax.ShapeDtypeStruct((seq, w), BF16),
                   jax.ShapeDtypeStruct((seq, MLA_WIDTH), BF16)],
        compiler_params=_cparams("parallel"),
        name="mla_prep",
    )(qf, kvf, proj, pos_col, freq)


def _t5_bucket(dist):
    n = jnp.maximum(dist, 0)
    max_exact = T5_BUCKETS // 2
    log_ratio = jnp.log(jnp.maximum(n, 1).astype(F32) / max_exact) / math.log(T5_MAX_DIST / max_exact)
    large = jnp.minimum(max_exact + (log_ratio * (T5_BUCKETS - max_exact)).astype(I32), T5_BUCKETS - 1)
    return jnp.where(n < max_exact, n, large)


def _t5_lookup(table_row, bucket):
    tab = jnp.broadcast_to(table_row, bucket.shape)
    return jnp.take_along_axis(tab, bucket, axis=1)


def _nsa_cmp_kernel(x_ref, pe_ref, w1_ref, w2_ref, o_ref, xf_ref, *, nc):
    half = NSA_CMP_LEN // 2
    d = NSA_HEAD_DIM
    xf_ref[...] = x_ref[...].astype(F32)
    u = jnp.zeros((nc, NSA_CMP_HIDDEN), F32)
    v = jnp.zeros((nc, NSA_CMP_HIDDEN), F32)
    for r in range(half):
        a = xf_ref[pl.ds(r, nc, stride=NSA_CMP_STRIDE), :]
        u = u + jnp.dot((a + pe_ref[r:r + 1, :]).astype(BF16), w1_ref[r * d:(r + 1) * d, :],
                        preferred_element_type=F32)
        v = v + jnp.dot((a + pe_ref[half + r:half + r + 1, :]).astype(BF16),
                        w1_ref[(half + r) * d:(half + r + 1) * d, :], preferred_element_type=F32)
    hid = u + pltpu.roll(v, nc - 1, 0)
    o_ref[...] = jnp.dot(jax.nn.gelu(hid).astype(BF16), w2_ref[...], preferred_element_type=F32).astype(o_ref.dtype)


def _nsa_compress(proj, k_cb, pe, w1, w2):
    seq = proj.shape[0]
    nc = seq // NSA_CMP_STRIDE
    d = NSA_HEAD_DIM
    g = NSA_KV_GROUPS
    return pl.pallas_call(
        functools.partial(_nsa_cmp_kernel, nc=nc),
        grid=(2, g),
        in_specs=[pl.BlockSpec((seq, d), lambda a, b: (0, k_cb + a * g + b)),
                  pl.BlockSpec((None, NSA_CMP_LEN, d), lambda a, b: (a, 0, 0)),
                  pl.BlockSpec((None, NSA_CMP_LEN * d, NSA_CMP_HIDDEN), lambda a, b: (a, 0, 0)),
                  pl.BlockSpec((None, NSA_CMP_HIDDEN, d), lambda a, b: (a, 0, 0))],
        out_specs=pl.BlockSpec((None, None, nc, d), lambda a, b: (a, b, 0, 0)),
        out_shape=jax.ShapeDtypeStruct((2, g, nc, d), BF16),
        scratch_shapes=[pltpu.VMEM((seq, d), F32)],
        compiler_params=_cparams("parallel", "arbitrary"),
        name="nsa_compress",
    )(proj, pe, w1, w2)


def _nsa_select_kernel(q_ref, kc_ref, vc_ref, posq_ref, posc_ref, tab_ref, gate_ref, ov_ref,
                       oc_ref, sel_ref, *, nc, n_slc):
    qi = pl.program_id(0)
    d = NSA_HEAD_DIM
    scale = d ** -0.5
    t = qi * QB + lax.broadcasted_iota(I32, (QB, 1), 0)
    n_idx = lax.broadcasted_iota(I32, (1, nc), 1)
    cmp_end = n_idx * NSA_CMP_STRIDE + (NSA_CMP_LEN - 1)
    valid = cmp_end <= t
    dist = posq_ref[...] - posc_ref[...]
    buckets = [_t5_bucket(dist[:, c * LANES:(c + 1) * LANES]) for c in range(nc // LANES)]
    gates = jax.nn.sigmoid(gate_ref[...])
    ov = ov_ref[...]
    jl = lax.broadcasted_iota(I32, (QB, LANES), 1)
    cur = t // NSA_SLC_BLOCK
    forced = (jl == 0) | (jl == cur) | (jl == cur - 1)
    js = lax.broadcasted_iota(I32, (LANES, QB), 0).astype(F32)
    for g in range(NSA_KV_GROUPS):
        kc = kc_ref[g]
        vc = vc_ref[g]
        psum = jnp.zeros((QB, nc), F32)
        for r in range(NSA_REP):
            h = g * NSA_REP + r
            q = q_ref[:, h * d:(h + 1) * d]
            s = lax.dot_general(q, kc, (((1,), (1,)), ((), ())), preferred_element_type=F32) * scale
            bias = jnp.concatenate([_t5_lookup(tab_ref[h:h + 1, :], b) for b in buckets], axis=1)
            s = jnp.where(valid, s + bias, NEG_INF)
            m = jnp.max(s, axis=1, keepdims=True)
            e = jnp.exp(s - m)
            p = jnp.where(valid, e / jnp.sum(e, axis=1, keepdims=True), 0.0)
            o = jnp.dot(p.astype(BF16), vc, preferred_element_type=F32)
            oc_ref[:, h * d:(h + 1) * d] = gates[:, 3 * h:3 * h + 1] * o
            psum = psum + p
        p_hi = psum.astype(BF16)
        p_lo = (psum - p_hi.astype(F32)).astype(BF16)
        imp = jnp.dot(p_hi, ov, preferred_element_type=F32) + jnp.dot(p_lo, ov, preferred_element_type=F32)
        score = jnp.where(forced, FORCE_SCORE, jnp.where(jl > cur, -1.0, imp))
        score = jnp.where(jl < n_slc, score, -2.0)
        st = score.T
        sel = jnp.zeros((LANES, QB), F32)
        for _ in range(NSA_SLC_TOPK):
            mx = jnp.max(st, axis=0, keepdims=True)
            first = jnp.min(jnp.where(st == mx, js, float(LANES)), axis=0, keepdims=True)
            hit = js == first
            sel = jnp.where(hit, 1.0, sel)
            st = jnp.where(hit, -3e38, st)
        sel_ref[g] = sel.astype(sel_ref.dtype)


def _nsa_select(proj_b, q_cb, proj_f, gate_cb, kc, vc, pos_col, pos_cmp, tab_t, ov, *, n_slc):
    seq = proj_b.shape[0]
    nc = kc.shape[1]
    g = NSA_KV_GROUPS
    d = NSA_HEAD_DIM
    return pl.pallas_call(
        functools.partial(_nsa_select_kernel, nc=nc, n_slc=n_slc),
        grid=(seq // QB,),
        in_specs=[pl.BlockSpec((QB, NSA_WIDTH), lambda i: (i, q_cb)),
                  pl.BlockSpec((g, nc, d), lambda i: (0, 0, 0)),
                  pl.BlockSpec((g, nc, d), lambda i: (0, 0, 0)),
                  pl.BlockSpec((QB, 1), lambda i: (i, 0)),
                  pl.BlockSpec((1, nc), lambda i: (0, 0)),
                  pl.BlockSpec((SUBLANES, LANES), lambda i: (0, 0)),
                  pl.BlockSpec((QB, LANES), lambda i: (i, gate_cb)),
                  pl.BlockSpec((nc, LANES), lambda i: (0, 0))],
        out_specs=[pl.BlockSpec((QB, NSA_WIDTH), lambda i: (i, 0)),
                   pl.BlockSpec((g, LANES, QB), lambda i: (0, 0, i))],
        out_shape=[jax.ShapeDtypeStruct((seq, NSA_WIDTH), F32),
                   jax.ShapeDtypeStruct((g, LANES, seq), BF16)],
        compiler_params=_cparams("parallel"),
        name="nsa_select",
    )(proj_b, kc, vc, pos_col, pos_cmp, tab_t, proj_f, ov)


def _nsa_slc_kernel(pqmin_ref, pkmax_ref, q_ref, k_ref, v_ref, sel_ref, posq_ref, posk_ref, tab_ref, o_ref,
                    vt_ref, qt_ref, s_ref, m_ref, acc_ref, *, tk):
    g = pl.program_id(0)
    qi = pl.program_id(1)
    d = NSA_HEAD_DIM
    scale = d ** -0.5
    seq = k_ref.shape[0]

    @pl.when(qi == 0)
    def _():
        _build_vt(v_ref, vt_ref, 0, d, seq, tk)

    for r in range(NSA_REP):
        qt_ref[r] = (q_ref[:, r * d:(r + 1) * d].astype(F32) * (scale * LOG2E)).T.astype(BF16)
    m_ref[...] = jnp.full(m_ref.shape, NEG_INF, F32)
    acc_ref[...] = jnp.zeros(acc_ref.shape, F32)
    sel = sel_ref[...]
    pos_q = posq_ref[...]
    key_blk = lax.broadcasted_iota(I32, (tk, LANES), 0) // NSA_SLC_BLOCK
    blk = lax.broadcasted_iota(I32, (tk, LANES), 1)
    per_tile = tk // NSA_SLC_BLOCK
    n_full = (qi * QB) // tk

    def scores(j, slot):
        start = pl.multiple_of(j * tk, tk)
        k = k_ref[pl.ds(start, tk), :]
        for r in range(NSA_REP):
            s_ref[slot, r] = jnp.dot(k, qt_ref[r], preferred_element_type=F32)

    def softmax_pv(j, slot, masked, near):
        start = pl.multiple_of(j * tk, tk)
        expand = jnp.where(blk == key_blk + j * per_tile, 1.0, 0.0).astype(BF16)
        picked = jnp.dot(expand, sel, preferred_element_type=F32)
        if masked:
            key = start + lax.broadcasted_iota(I32, (tk, QB), 0)
            qry = qi * QB + lax.broadcasted_iota(I32, (tk, QB), 1)
            picked = jnp.where(key <= qry, picked, 0.0)
        mask = picked > 0.5
        if near:
            bucket = _t5_bucket(pos_q - posk_ref[pl.ds(start, tk), :])
        for r in range(NSA_REP):
            h = g * NSA_REP + r
            t = s_ref[slot, r]
            if near:
                t = t + _t5_lookup(tab_ref[pl.ds(h, 1), :], bucket)
            t = jnp.where(mask, t, NEG_INF)
            m_prev = m_ref[r]
            mx = jnp.max(t, axis=0, keepdims=True)
            if near:
                m_new = jnp.maximum(m_prev, mx)
                shift = m_new
            else:
                b = tab_ref[pl.ds(h, 1), T5_BUCKETS - 1:T5_BUCKETS]
                m_new = jnp.maximum(m_prev, mx + b)
                shift = m_new - b
            alpha = jnp.exp2(m_prev - m_new)
            p = jnp.exp2(t - shift).astype(BF16)
            acc_ref[r] = alpha * acc_ref[r] + jnp.dot(vt_ref[0, :, pl.ds(start, tk)], p, preferred_element_type=F32)
            m_ref[r] = m_new

    def stage(j, slot):
        far = pqmin_ref[qi] - pkmax_ref[j] >= T5_MAX_DIST

        @pl.when(far)
        def _():
            scores(j + 1, 1 - slot)
            softmax_pv(j, slot, False, False)

        @pl.when(jnp.logical_not(far))
        def _():
            scores(j + 1, 1 - slot)
            softmax_pv(j, slot, False, True)

    scores(0, 0)

    def pair(jj, carry):
        stage(2 * jj, 0)
        stage(2 * jj + 1, 1)
        return carry

    pairs = n_full // 2
    lax.fori_loop(0, pairs, pair, 0)
    last = 2 * pairs

    @pl.when(n_full % 2 == 1)
    def _():
        stage(last, 0)
        softmax_pv(last + 1, 1, True, True)

    @pl.when(n_full % 2 == 0)
    def _():
        softmax_pv(last, 0, True, True)

    for r in range(NSA_REP):
        a = acc_ref[r]
        o_ref[:, r * d:(r + 1) * d] = (a[0:d, :] / a[d:d + 1, :]).T


def _nsa_slc(proj_b, q_off, k_off, v_off, sel, pos_row, pos_rep, tab_t, pq_min, pk_max, *, tk):
    seq = proj_b.shape[0]
    tab_t = tab_t * LOG2E
    d = NSA_HEAD_DIM
    gw = NSA_REP * d
    once = pl.Buffered(1)
    grid_spec = pltpu.PrefetchScalarGridSpec(
        num_scalar_prefetch=2,
        grid=(NSA_KV_GROUPS, seq // QB),
        in_specs=[pl.BlockSpec((QB, gw), lambda g, i, *_: (i, q_off // gw + g)),
                  pl.BlockSpec((seq, d), lambda g, i, *_: (0, k_off // d + g), pipeline_mode=once),
                  pl.BlockSpec((seq, d), lambda g, i, *_: (0, v_off // d + g), pipeline_mode=once),
                  pl.BlockSpec((None, LANES, QB), lambda g, i, *_: (g, 0, i)),
                  pl.BlockSpec((1, QB), lambda g, i, *_: (0, i)),
                  pl.BlockSpec((seq, LANES), lambda g, i, *_: (0, 0), pipeline_mode=once),
                  pl.BlockSpec((SUBLANES, LANES), lambda g, i, *_: (0, 0))],
        out_specs=pl.BlockSpec((QB, gw), lambda g, i, *_: (i, g)),
        scratch_shapes=[pltpu.VMEM((1, d + ONES_ROWS, seq), BF16), pltpu.VMEM((NSA_REP, d, QB), BF16),
                        pltpu.VMEM((2, NSA_REP, tk, QB), F32), pltpu.VMEM((NSA_REP, 1, QB), F32),
                        pltpu.VMEM((NSA_REP, d + ONES_ROWS, QB), F32)])
    return pl.pallas_call(
        functools.partial(_nsa_slc_kernel, tk=tk),
        grid_spec=grid_spec,
        out_shape=jax.ShapeDtypeStruct((seq, NSA_WIDTH), F32),
        compiler_params=_cparams("parallel", "arbitrary"),
        name="nsa_slc",
    )(pq_min, pk_max, proj_b, proj_b, proj_b, sel, pos_row, pos_rep, tab_t)


NSA_WIN_TILES = NSA_WINDOW // QB + 1


def _nsa_win_kernel(*refs):
    nt = NSA_WIN_TILES
    q_ref = refs[0]
    k_refs = refs[1:1 + nt]
    v_refs = refs[1 + nt:1 + 2 * nt]
    pk_refs = refs[1 + 2 * nt:1 + 3 * nt]
    posq_ref, tab_ref, gate_ref, gout_ref, oc_ref, os_ref, o_ref = refs[1 + 3 * nt:]
    qi = pl.program_id(0)
    d = NSA_HEAD_DIM
    scale = d ** -0.5
    t = qi * QB + lax.broadcasted_iota(I32, (QB, 1), 0)
    pos_q = posq_ref[...]
    lane = lax.broadcasted_iota(I32, (1, QB), 1)
    bands, buckets = [], []
    for jj in range(nt):
        kidx = (qi - (nt - 1) + jj) * QB + lane
        bands.append((kidx >= 0) & (kidx <= t) & (t - kidx < NSA_WINDOW))
        buckets.append(_t5_bucket(pos_q - pk_refs[jj][...]))
    band = jnp.concatenate(bands, axis=1)
    gates = jax.nn.sigmoid(gate_ref[...])
    for g in range(NSA_KV_GROUPS):
        k = jnp.concatenate([kr[:, g * d:(g + 1) * d] for kr in k_refs], axis=0)
        v = jnp.concatenate([vr[:, g * d:(g + 1) * d] for vr in v_refs], axis=0)
        for r in range(NSA_REP):
            h = g * NSA_REP + r
            hs = slice(h * d, (h + 1) * d)
            s = lax.dot_general(q_ref[:, hs], k, (((1,), (1,)), ((), ())), preferred_element_type=F32) * scale
            bias = jnp.concatenate([_t5_lookup(tab_ref[h:h + 1, :], b) for b in buckets], axis=1)
            s = jnp.where(band, s + bias, NEG_INF)
            m = jnp.max(s, axis=1, keepdims=True)
            p = jnp.exp(s - m)
            o_w = jnp.dot(p.astype(BF16), v, preferred_element_type=F32) / jnp.sum(p, axis=1, keepdims=True)
            o = oc_ref[:, hs] + gates[:, 3 * h + 1:3 * h + 2] * os_ref[:, hs] + gates[:, 3 * h + 2:3 * h + 3] * o_w
            o_ref[:, hs] = (o * _silu(gout_ref[:, hs])).astype(o_ref.dtype)


def _nsa_win(proj_b, q_cb, k_cb, v_cb, proj_f, gate_cb, gout_cb, oc, o_s, pos_col, pos_row, tab_t):
    seq = proj_b.shape[0]
    nt = NSA_WIN_TILES

    def band_rows(jj, cb):
        return pl.BlockSpec((QB, NSA_KV), lambda i: (jnp.maximum(i - (nt - 1) + jj, 0), cb))

    def band_pos(jj):
        return pl.BlockSpec((1, QB), lambda i: (0, jnp.maximum(i - (nt - 1) + jj, 0)))

    in_specs = [pl.BlockSpec((QB, NSA_WIDTH), lambda i: (i, q_cb))]
    in_specs += [band_rows(jj, k_cb) for jj in range(nt)]
    in_specs += [band_rows(jj, v_cb) for jj in range(nt)]
    in_specs += [band_pos(jj) for jj in range(nt)]
    in_specs += [pl.BlockSpec((QB, 1), lambda i: (i, 0)),
                 pl.BlockSpec((SUBLANES, LANES), lambda i: (0, 0)),
                 pl.BlockSpec((QB, LANES), lambda i: (i, gate_cb)),
                 pl.BlockSpec((QB, NSA_WIDTH), lambda i: (i, gout_cb)),
                 pl.BlockSpec((QB, NSA_WIDTH), lambda i: (i, 0)),
                 pl.BlockSpec((QB, NSA_WIDTH), lambda i: (i, 0))]
    args = [proj_b] * (1 + 2 * nt) + [pos_row] * nt + [pos_col, tab_t, proj_f, proj_f, oc, o_s]
    return pl.pallas_call(
        _nsa_win_kernel,
        grid=(seq // QB,),
        in_specs=in_specs,
        out_specs=pl.BlockSpec((QB, NSA_WIDTH), lambda i: (i, 0)),
        out_shape=jax.ShapeDtypeStruct((seq, NSA_WIDTH), BF16),
        compiler_params=_cparams("parallel"),
        name="nsa_win",
    )(*args)


def _in_proj(h, norm_g, w_b, w_f, tiles, name):
    proj_b = _norm_matmul(h, norm_g, w_b, tm=tiles["tm"], tn=512, name=name + "_b", out_dtype=BF16)
    proj_f = _norm_matmul(h, norm_g, w_f, tm=tiles["tm"], tn=768, name=name + "_f")
    return proj_b, proj_f


def _even_layer(h, mem_kv, norm_g, w_in, s5, w_glu, b_f, tiles):
    pb, pf = _in_proj(h, norm_g, *w_in, tiles, "in_proj_even")
    ob, of = EVEN_B_OFF, EVEN_F_OFF
    b_cat, c_cat, tab, d_skip = s5
    z = _s5_scan(pf, of["u"] // LANES, b_cat, c_cat, d_skip, tab, tc=tiles["s5_tc"])
    y_s5 = _s5_glu(z, w_glu, pf, of["g_s5"] // S5_WIDTH, tm=tiles["tm"], tn=512)
    decay = _decay(pf, of["f"] // LANES, b_f, t=tiles["decay_t"])
    d = FOX_HEAD_DIM
    y_fox = _flash(pb, ob["q"], pb, ob["k"], pb, ob["v"], pf, of["g_fox"], heads=FOX_HEADS, dk=d, dv=d,
                   scale=d ** -0.5, tq=tiles["attn_tq"], tk=tiles["attn_tk"], hp=tiles["attn_hp"], decay=decay,
                   name="fox_attn")
    y_mem = _mem_attn(pb, ob["q_mem"] // MEM_HEAD_DIM, pf, of["g_mem"] // MEM_HEAD_DIM, mem_kv, t=tiles["mem_t"])
    return y_s5, y_fox, y_mem


def _odd_layer(h, mem_kv, norm_g, w_in, mla, nsa, pos, tiles):
    pb, pf = _in_proj(h, norm_g, *w_in, tiles, "in_proj_odd")
    ob, of = ODD_B_OFF, ODD_F_OFF
    g_cq, g_ckv, w_uq, w_ukv, freq = mla
    pos_col, pos_col_f, pos_row, pos_cmp, pos_rep, pq_min, pk_max = pos
    qf = _norm_matmul(pf, g_cq, w_uq, x_cb=of["c_q"] // MLA_Q_RANK, tm=tiles["tm"], tn=512, name="mla_q_up")
    kvf = _norm_matmul(pf, g_ckv, w_ukv, x_cb=of["c_kv"] // MLA_KV_RANK, tm=tiles["tm"], tn=512, name="mla_kv_up")
    q_r, k_r, v_r = _mla_prep(qf, kvf, pf, of["k_rope"] // LANES, pos_col_f, freq, t=tiles["prep_t"])
    y_mla = _flash(q_r, 0, k_r, 0, v_r, 0, pf, of["g_mla"], heads=MLA_HEADS, dk=2 * LANES, dv=MLA_V,
                   scale=(MLA_NOPE + MLA_ROPE) ** -0.5, tq=tiles["attn_tq"], tk=tiles["attn_tk"], hp=tiles["attn_hp"],
                   name="mla_attn")
    pe, w1, w2, tab_t, ov, n_slc = nsa
    kvc = _nsa_compress(pb, ob["k_cmp"] // NSA_HEAD_DIM, pe, w1, w2)
    oc, sel = _nsa_select(pb, ob["q_nsa"] // NSA_WIDTH, pf, of["gates"] // LANES, kvc[0], kvc[1], pos_col, pos_cmp,
                          tab_t, ov, n_slc=n_slc)
    o_s = _nsa_slc(pb, ob["q_nsa"], ob["k_slc"], ob["v_slc"], sel, pos_row, pos_rep, tab_t, pq_min, pk_max,
                   tk=tiles["slc_tk"])
    y_nsa = _nsa_win(pb, ob["q_nsa"] // NSA_WIDTH, ob["k_win"] // NSA_KV, ob["v_win"] // NSA_KV,
                     pf, of["gates"] // LANES, of["g_nsa"] // NSA_WIDTH, oc, o_s, pos_col, pos_row, tab_t)
    y_mem = _mem_attn(pb, ob["q_mem"] // MEM_HEAD_DIM, pf, of["g_mem"] // MEM_HEAD_DIM, mem_kv, t=tiles["mem_t"])
    return y_mla, y_nsa, y_mem


def _tiles(seq):
    return {"tm": min(seq, 1024), "s5_tc": min(seq, 1024), "decay_t": min(seq, 512), "attn_tq": min(seq, 256),
            "attn_tk": min(seq, 512), "attn_hp": 2, "slc_tk": 256, "mem_t": min(seq, 512), "prep_t": min(seq, 256)}


def _context(positions, t5_table, seq, tiles):
    pos = positions[0]
    pos_col = pos.reshape(seq, 1)
    pos_row = pos.reshape(1, seq)
    pos_rep = jnp.broadcast_to(pos_col, (seq, LANES))
    pq_min = jnp.min(pos.reshape(seq // QB, QB), axis=1)
    pk_max = jnp.max(pos.reshape(seq // tiles["slc_tk"], tiles["slc_tk"]), axis=1)
    nc = seq // NSA_CMP_STRIDE
    pos_cmp = jnp.pad(pos[NSA_CMP_LEN - 1::NSA_CMP_STRIDE], (0, 1)).reshape(1, nc)
    half = MLA_ROPE // 2
    inv_freq = ROPE_THETA ** (-jnp.arange(half, dtype=F32) / half)
    freq = jnp.concatenate([inv_freq, inv_freq, jnp.zeros((LANES - MLA_ROPE,), F32)]).reshape(1, LANES)
    tab_t = jnp.pad(t5_table.astype(F32).T, ((0, SUBLANES - NSA_HEADS), (0, LANES - T5_BUCKETS)))
    n_slc = seq // NSA_SLC_BLOCK
    cs = np.arange(nc) * NSA_CMP_STRIDE
    ss = np.arange(LANES) * NSA_SLC_BLOCK
    ov_np = np.clip(np.minimum(cs[:, None] + NSA_CMP_LEN, ss[None, :] + NSA_SLC_BLOCK)
                    - np.maximum(cs[:, None], ss[None, :]), 0, None) / NSA_CMP_LEN
    ov_np[nc - 1, :] = 0.0
    ov_np[:, n_slc:] = 0.0
    return {"pos": (pos_col, pos_col.astype(F32), pos_row, pos_cmp, pos_rep, pq_min, pk_max), "freq": freq,
            "tab_t": tab_t,
            "ov": jnp.asarray(ov_np, BF16), "n_slc": n_slc}


def _odd_params(i, mla_g_cq, mla_g_ckv, mla_w_uq, mla_w_ukv, nsa_cmp_pe, nsa_cmp_w1, nsa_cmp_w2, ctx):
    dq = MLA_NOPE + MLA_ROPE
    w_uq = mla_w_uq[i].reshape(MLA_Q_RANK, MLA_HEADS, dq)
    w_uq = jnp.pad(w_uq, ((0, 0), (0, 0), (0, 2 * LANES - dq))).reshape(MLA_Q_RANK, -1).astype(BF16)
    mla = (mla_g_cq[i], mla_g_ckv[i], w_uq, mla_w_ukv[i].astype(BF16), ctx["freq"])
    nsa = (nsa_cmp_pe[i].astype(F32), nsa_cmp_w1[i].astype(BF16), nsa_cmp_w2[i].astype(BF16), ctx["tab_t"],
           ctx["ov"], ctx["n_slc"])
    return mla, nsa


def kernel(x, mem, positions, norm_g, mem_norm_g, final_norm_g, t5_table, w_out, mem_w_kv, even_w_in, s5_lam_re,
           s5_lam_im, s5_log_dt, s5_b_re, s5_b_im, s5_c_re, s5_c_im, s5_d, s5_w_glu, fox_b_f, odd_w_in, mla_g_cq,
           mla_g_ckv, mla_w_uq, mla_w_ukv, nsa_cmp_pe, nsa_cmp_w1, nsa_cmp_w2):
    batch, seq, _ = x.shape
    assert batch == 1 and seq % 1024 == 0 and seq // NSA_SLC_BLOCK <= LANES
    depth = norm_g.shape[0]
    tiles = _tiles(seq)
    ctx = _context(positions, t5_table, seq, tiles)
    h = x[0]
    mem_w = mem_w_kv.astype(BF16)
    w_out_b = w_out.astype(BF16)
    for layer in range(depth):
        i = layer // 2
        mem_kv = _norm_matmul(mem[0], mem_norm_g, mem_w[layer], tm=mem.shape[1], tn=512, name="mem_kv")
        if layer % 2 == 0:
            w_in = (_reorder_w_in(even_w_in[i], EVEN_SPLITS, EVEN_B_ORDER),
                    _reorder_w_in(even_w_in[i], EVEN_SPLITS, EVEN_F_ORDER))
            b_cat, c_cat, tab = _s5_prepare(s5_lam_re[i], s5_lam_im[i], s5_log_dt[i], s5_b_re[i], s5_b_im[i],
                                            s5_c_re[i], s5_c_im[i])
            ys = _even_layer(h, mem_kv, norm_g[layer], w_in, (b_cat, c_cat, tab, s5_d[i]),
                             s5_w_glu[i].astype(BF16), fox_b_f[i], tiles)
            widths = (S5_WIDTH, FOX_WIDTH, MEM_WIDTH)
        else:
            w_in = (_reorder_w_in(odd_w_in[i], ODD_SPLITS, ODD_B_ORDER),
                    _reorder_w_in(odd_w_in[i], ODD_SPLITS, ODD_F_ORDER))
            mla, nsa = _odd_params(i, mla_g_cq, mla_g_ckv, mla_w_uq, mla_w_ukv, nsa_cmp_pe, nsa_cmp_w1, nsa_cmp_w2,
                                   ctx)
            ys = _odd_layer(h, mem_kv, norm_g[layer], w_in, mla, nsa, ctx["pos"], tiles)
            widths = (MLA_WIDTH, NSA_WIDTH, MEM_WIDTH)
        starts = np.concatenate([[0], np.cumsum(widths)])
        ws = [w_out_b[layer, int(starts[k]):int(starts[k + 1])] for k in range(3)]
        h = _out_proj(h, ys, ws, tm=tiles["tm"], tn=512)
    return _final_norm(h, final_norm_g, tm=tiles["mem_t"])[None]
```

```python
import functools
import math

import numpy as np
import jax
import jax.numpy as jnp
from jax import lax
from jax.experimental import pallas as pl
from jax.experimental.pallas import tpu as pltpu

F32 = jnp.float32
BF16 = jnp.bfloat16
I32 = jnp.int32

D_MODEL = 2048
DEPTH = 4
N_MEM = 256
RMS_EPS = 1e-6
NEG_INF = -1e30
LOG2E = math.log2(math.e)

S5_WIDTH = 1024
S5_GROUP = 16
S5_GROUPS = S5_WIDTH // S5_GROUP
S5_STATE = 64
FOX_HEADS = 8
FOX_HEAD_DIM = 128
FOX_WIDTH = FOX_HEADS * FOX_HEAD_DIM
MEM_HEADS = 4
MEM_HEAD_DIM = 128
MEM_WIDTH = MEM_HEADS * MEM_HEAD_DIM
MLA_HEADS = 8
MLA_Q_RANK = 512
MLA_KV_RANK = 512
MLA_NOPE = 128
MLA_ROPE = 64
MLA_V = 128
MLA_WIDTH = MLA_HEADS * MLA_V
ROPE_THETA = 10000.0
NSA_HEADS = 8
NSA_KV_GROUPS = 2
NSA_REP = NSA_HEADS // NSA_KV_GROUPS
NSA_HEAD_DIM = 128
NSA_WIDTH = NSA_HEADS * NSA_HEAD_DIM
NSA_KV = NSA_KV_GROUPS * NSA_HEAD_DIM
NSA_CMP_LEN = 32
NSA_CMP_STRIDE = 16
NSA_CMP_HIDDEN = 256
NSA_SLC_BLOCK = 64
NSA_SLC_TOPK = 16
NSA_WINDOW = 512
FORCE_SCORE = 1e6
T5_BUCKETS = 32
T5_MAX_DIST = 1024

EVEN_SPLITS = (S5_WIDTH, S5_WIDTH, FOX_WIDTH, FOX_WIDTH, FOX_WIDTH, FOX_HEADS, FOX_WIDTH, MEM_WIDTH, MEM_WIDTH)
ODD_SPLITS = (MLA_Q_RANK, MLA_KV_RANK, MLA_ROPE, MLA_WIDTH, NSA_WIDTH, NSA_KV, NSA_KV, NSA_KV, NSA_KV, NSA_KV,
              NSA_KV, 3 * NSA_HEADS, NSA_WIDTH, MEM_WIDTH, MEM_WIDTH)

LANES = 128
SUBLANES = 8
QB = 128
VMEM_LIMIT_BYTES = 56 * 1024 * 1024

EVEN_B_ORDER = (("q", 2, 1024), ("k", 3, 1024), ("v", 4, 1024), ("q_mem", 7, 512))
EVEN_F_ORDER = (("u", 0, 1024), ("g_s5", 1, 1024), ("g_fox", 6, 1024), ("g_mem", 8, 512), ("f", 5, 128),
                ("pad", None, 128))
ODD_B_ORDER = (("q_nsa", 4, 1024), ("q_mem", 13, 512), ("k_cmp", 5, 256), ("v_cmp", 6, 256), ("k_slc", 7, 256),
               ("v_slc", 8, 256), ("k_win", 9, 256), ("v_win", 10, 256))
ODD_F_ORDER = (("g_nsa", 12, 1024), ("g_mla", 3, 1024), ("c_q", 0, 512), ("c_kv", 1, 512), ("g_mem", 14, 512),
               ("k_rope", 2, 128), ("gates", 11, 128))


def _layout(order):
    off, out = 0, {}
    for name, _, width in order:
        assert off % width == 0
        out[name] = off
        off += width
    return out, off


EVEN_B_OFF, EVEN_B_N = _layout(EVEN_B_ORDER)
EVEN_F_OFF, EVEN_F_N = _layout(EVEN_F_ORDER)
ODD_B_OFF, ODD_B_N = _layout(ODD_B_ORDER)
ODD_F_OFF, ODD_F_N = _layout(ODD_F_ORDER)


def _reorder_w_in(w, splits, order):
    starts = np.concatenate([[0], np.cumsum(splits)])
    cols = []
    for _, idx, width in order:
        if idx is None:
            cols.append(jnp.zeros((w.shape[0], width), w.dtype))
            continue
        seg = w[:, int(starts[idx]):int(starts[idx + 1])]
        pad = width - seg.shape[1]
        if pad:
            seg = jnp.pad(seg, ((0, 0), (0, pad)))
        cols.append(seg)
    return jnp.concatenate(cols, axis=1).astype(BF16)


def _cparams(*sem):
    return pltpu.CompilerParams(dimension_semantics=sem, vmem_limit_bytes=VMEM_LIMIT_BYTES)


def _silu(g):
    return g * jax.nn.sigmoid(g)


def _pick(n, cands):
    for c in cands:
        if n % c == 0:
            return c
    raise ValueError(f"no tile for {n} in {cands}")


def _norm_matmul_kernel(x_ref, g_ref, w_ref, o_ref, xn_ref):
    @pl.when(pl.program_id(1) == 0)
    def _():
        x = x_ref[...]
        ms = jnp.mean(x * x, axis=-1, keepdims=True)
        xn_ref[...] = (x * lax.rsqrt(ms + RMS_EPS) * g_ref[...]).astype(BF16)

    o_ref[...] = jnp.dot(xn_ref[...], w_ref[...], preferred_element_type=F32).astype(o_ref.dtype)


def _norm_matmul(x, g, w, *, x_cb=0, tm, tn, name, out_dtype=F32):
    m = x.shape[0]
    k, n = w.shape
    return pl.pallas_call(
        _norm_matmul_kernel,
        grid=(m // tm, n // tn),
        in_specs=[pl.BlockSpec((tm, k), lambda i, j: (i, x_cb)),
                  pl.BlockSpec((1, k), lambda i, j: (0, 0)),
                  pl.BlockSpec((k, tn), lambda i, j: (0, j))],
        out_specs=pl.BlockSpec((tm, tn), lambda i, j: (i, j)),
        out_shape=jax.ShapeDtypeStruct((m, n), out_dtype),
        scratch_shapes=[pltpu.VMEM((tm, k), BF16)],
        compiler_params=_cparams("parallel", "arbitrary"),
        name=name,
    )(x, g.reshape(1, k), w)


def _out_proj_kernel(h_ref, *refs):
    o_ref = refs[-1]
    n = (len(refs) - 1) // 2
    acc = h_ref[...]
    for y_ref, w_ref in zip(refs[:n], refs[n:2 * n]):
        acc = acc + jnp.dot(y_ref[...], w_ref[...], preferred_element_type=F32)
    o_ref[...] = acc


def _out_proj(h, ys, ws, *, tm, tn):
    m, n = h.shape
    in_specs = [pl.BlockSpec((tm, tn), lambda i, j: (i, j))]
    in_specs += [pl.BlockSpec((tm, y.shape[1]), lambda i, j: (i, 0)) for y in ys]
    in_specs += [pl.BlockSpec((w.shape[0], tn), lambda i, j: (0, j)) for w in ws]
    return pl.pallas_call(
        _out_proj_kernel,
        grid=(m // tm, n // tn),
        in_specs=in_specs,
        out_specs=pl.BlockSpec((tm, tn), lambda i, j: (i, j)),
        out_shape=jax.ShapeDtypeStruct((m, n), F32),
        compiler_params=_cparams("parallel", "arbitrary"),
        name="out_proj",
    )(h, *ys, *ws)


def _final_norm_kernel(x_ref, g_ref, o_ref):
    x = x_ref[...]
    ms = jnp.mean(x * x, axis=-1, keepdims=True)
    o_ref[...] = x * lax.rsqrt(ms + RMS_EPS) * g_ref[...]


def _final_norm(h, g, *, tm):
    m, n = h.shape
    return pl.pallas_call(
        _final_norm_kernel,
        grid=(m // tm,),
        in_specs=[pl.BlockSpec((tm, n), lambda i: (i, 0)), pl.BlockSpec((1, n), lambda i: (0, 0))],
        out_specs=pl.BlockSpec((tm, n), lambda i: (i, 0)),
        out_shape=jax.ShapeDtypeStruct((m, n), F32),
        compiler_params=_cparams("parallel"),
        name="final_norm",
    )(h, g.reshape(1, n))


ONES_ROWS = 16


def _build_vt(v_ref, vt_ref, hh, dv, seq, chunk):
    def body(c, carry):
        st = pl.multiple_of(c * chunk, chunk)
        vt_ref[hh, 0:dv, pl.ds(st, chunk)] = v_ref[pl.ds(st, chunk), hh * dv:(hh + 1) * dv].astype(F32).T.astype(BF16)
        return carry

    lax.fori_loop(0, seq // chunk, body, 0)
    vt_ref[hh, dv:dv + ONES_ROWS, :] = jnp.ones((ONES_ROWS, seq), BF16)


def _flash_kernel(*refs, scale, tq, tk, hp, dk, dv, has_decay):
    if has_decay:
        q_ref, k_ref, v_ref, g_ref, cq_ref, ck_ref, o_ref, vt_ref, qt_ref, s_ref, m_ref, acc_ref = refs
    else:
        q_ref, k_ref, v_ref, g_ref, o_ref, vt_ref, qt_ref, s_ref, m_ref, acc_ref = refs
    hb = pl.program_id(0)
    qi = pl.program_id(1)
    seq = k_ref.shape[0]

    @pl.when(qi == 0)
    def _():
        for hh in range(hp):
            _build_vt(v_ref, vt_ref, hh, dv, seq, tk)

    for hh in range(hp):
        qt_ref[hh] = (q_ref[:, hh * dk:(hh + 1) * dk].astype(F32) * (scale * LOG2E)).T.astype(BF16)
    m_ref[...] = jnp.full(m_ref.shape, NEG_INF, F32)
    acc_ref[...] = jnp.zeros(acc_ref.shape, F32)
    n_full = (qi * tq) // tk
    if has_decay:
        cq2 = [cq_ref[pl.ds(hb * hp + hh, 1), :] for hh in range(hp)]

    def scores(j, slot):
        start = pl.multiple_of(j * tk, tk)
        for hh in range(hp):
            s_ref[slot, hh] = jnp.dot(k_ref[pl.ds(start, tk), hh * dk:(hh + 1) * dk], qt_ref[hh],
                                      preferred_element_type=F32)

    def softmax_pv(j, slot, masked):
        start = pl.multiple_of(j * tk, tk)
        for hh in range(hp):
            t = s_ref[slot, hh]
            if has_decay:
                t = t - jnp.concatenate([ck_ref[hh, pl.ds(start, tk), :]] * (tq // LANES), axis=1)
            if masked:
                key = start + lax.broadcasted_iota(I32, (tk, tq), 0)
                qry = qi * tq + lax.broadcasted_iota(I32, (tk, tq), 1)
                t = jnp.where(key <= qry, t, NEG_INF)
            m_prev = m_ref[hh]
            mx = jnp.max(t, axis=0, keepdims=True)
            if has_decay:
                m_new = jnp.maximum(m_prev, mx + cq2[hh])
                shift = m_new - cq2[hh]
            else:
                m_new = jnp.maximum(m_prev, mx)
                shift = m_new
            alpha = jnp.exp2(m_prev - m_new)
            p = jnp.exp2(t - shift).astype(BF16)
            acc_ref[hh] = alpha * acc_ref[hh] + jnp.dot(vt_ref[hh, :, pl.ds(start, tk)], p,
                                                         preferred_element_type=F32)
            m_ref[hh] = m_new

    scores(0, 0)

    def pair(jj, carry):
        j = 2 * jj
        scores(j + 1, 1)
        softmax_pv(j, 0, False)
        scores(j + 2, 0)
        softmax_pv(j + 1, 1, False)
        return carry

    pairs = n_full // 2
    lax.fori_loop(0, pairs, pair, 0)
    last = 2 * pairs

    @pl.when(n_full % 2 == 1)
    def _():
        scores(last + 1, 1)
        softmax_pv(last, 0, False)
        softmax_pv(last + 1, 1, True)

    @pl.when(n_full % 2 == 0)
    def _():
        softmax_pv(last, 0, True)

    for hh in range(hp):
        a = acc_ref[hh]
        o = (a[0:dv, :] / a[dv:dv + 1, :]).T
        o_ref[:, hh * dv:(hh + 1) * dv] = (o * _silu(g_ref[:, hh * dv:(hh + 1) * dv])).astype(o_ref.dtype)


def _flash(q_arr, q_off, k_arr, k_off, v_arr, v_off, g_arr, g_off, *, heads, dk, dv, scale, tq, tk, hp,
           decay=None, name):
    seq = q_arr.shape[0]
    once = pl.Buffered(1)
    in_specs = [pl.BlockSpec((tq, hp * dk), lambda h, i: (i, q_off // (hp * dk) + h)),
                pl.BlockSpec((seq, hp * dk), lambda h, i: (0, k_off // (hp * dk) + h), pipeline_mode=once),
                pl.BlockSpec((seq, hp * dv), lambda h, i: (0, v_off // (hp * dv) + h), pipeline_mode=once),
                pl.BlockSpec((tq, hp * dv), lambda h, i: (i, g_off // (hp * dv) + h))]
    args = [q_arr, k_arr, v_arr, g_arr]
    if decay is not None:
        cum_t, cum_rep = decay
        in_specs += [pl.BlockSpec((SUBLANES, tq), lambda h, i: (0, i)),
                     pl.BlockSpec((hp, seq, LANES), lambda h, i: (h, 0, 0), pipeline_mode=once)]
        args += [cum_t, cum_rep]
    return pl.pallas_call(
        functools.partial(_flash_kernel, scale=scale, tq=tq, tk=tk, hp=hp, dk=dk, dv=dv, has_decay=decay is not None),
        grid=(heads // hp, seq // tq),
        in_specs=in_specs,
        out_specs=pl.BlockSpec((tq, hp * dv), lambda h, i: (i, h)),
        out_shape=jax.ShapeDtypeStruct((seq, heads * dv), BF16),
        scratch_shapes=[pltpu.VMEM((hp, dv + ONES_ROWS, seq), BF16), pltpu.VMEM((hp, dk, tq), BF16),
                        pltpu.VMEM((2, hp, tk, tq), F32), pltpu.VMEM((hp, 1, tq), F32),
                        pltpu.VMEM((hp, dv + ONES_ROWS, tq), F32)],
        compiler_params=_cparams("parallel", "arbitrary"),
        name=name,
    )(*args)


def _decay_kernel(f_ref, b_ref, ct_ref, cr_ref, carry_ref, *, t):
    i = pl.program_id(0)

    @pl.when(i == 0)
    def _():
        carry_ref[...] = jnp.zeros(carry_ref.shape, F32)

    x = f_ref[...] + b_ref[...]
    lf = jnp.minimum(x, 0.0) - jnp.log1p(jnp.exp(-jnp.abs(x)))
    row = lax.broadcasted_iota(I32, lf.shape, 0)
    s = 1
    while s < t:
        lf = lf + jnp.where(row >= s, pltpu.roll(lf, s, 0), 0.0)
        s *= 2
    lf = lf + carry_ref[...]
    carry_ref[...] = lf[t - 1:t, :]
    lf2 = lf * LOG2E
    ct_ref[...] = lf2.T[:FOX_HEADS, :]
    for h in range(FOX_HEADS):
        cr_ref[h] = jnp.broadcast_to(lf2[:, h:h + 1], (t, LANES))


def _decay(proj, f_cb, b_f, *, t):
    seq = proj.shape[0]
    b = jnp.pad(b_f.reshape(1, FOX_HEADS), ((0, 0), (0, LANES - FOX_HEADS)))
    return pl.pallas_call(
        functools.partial(_decay_kernel, t=t),
        grid=(seq // t,),
        in_specs=[pl.BlockSpec((t, LANES), lambda i: (i, f_cb)), pl.BlockSpec((1, LANES), lambda i: (0, 0))],
        out_specs=[pl.BlockSpec((FOX_HEADS, t), lambda i: (0, i)),
                   pl.BlockSpec((FOX_HEADS, t, LANES), lambda i: (0, i, 0))],
        out_shape=[jax.ShapeDtypeStruct((FOX_HEADS, seq), F32), jax.ShapeDtypeStruct((FOX_HEADS, seq, LANES), F32)],
        scratch_shapes=[pltpu.VMEM((1, LANES), F32)],
        compiler_params=_cparams("arbitrary"),
        name="fox_decay",
    )(proj, b)


def _mem_attn_kernel(q_ref, k_ref, v_ref, g_ref, o_ref):
    k = k_ref[...].astype(BF16)
    s = lax.dot_general(q_ref[...], k, (((1,), (1,)), ((), ())), preferred_element_type=F32) * (MEM_HEAD_DIM ** -0.5)
    m = jnp.max(s, axis=1, keepdims=True)
    p = jnp.exp(s - m)
    l = jnp.sum(p, axis=1, keepdims=True)
    o = jnp.dot(p.astype(BF16), v_ref[...].astype(BF16), preferred_element_type=F32) / l
    o_ref[...] = (o * _silu(g_ref[...])).astype(o_ref.dtype)


def _mem_attn(proj_b, q_cb, proj_f, g_cb, mem_kv, *, t):
    seq = proj_b.shape[0]
    nm = mem_kv.shape[0]
    d = MEM_HEAD_DIM
    return pl.pallas_call(
        _mem_attn_kernel,
        grid=(MEM_HEADS, seq // t),
        in_specs=[pl.BlockSpec((t, d), lambda h, i: (i, q_cb + h)),
                  pl.BlockSpec((nm, d), lambda h, i: (0, h)),
                  pl.BlockSpec((nm, d), lambda h, i: (0, MEM_HEADS + h)),
                  pl.BlockSpec((t, d), lambda h, i: (i, g_cb + h))],
        out_specs=pl.BlockSpec((t, d), lambda h, i: (i, h)),
        out_shape=jax.ShapeDtypeStruct((seq, MEM_WIDTH), BF16),
        compiler_params=_cparams("parallel", "arbitrary"),
        name="mem_attn",
    )(proj_b, mem_kv, mem_kv, proj_f)


S5_TILE_GROUPS = LANES // S5_GROUP
S5_TILE_STATES = S5_TILE_GROUPS * S5_STATE
S5_TILES = S5_GROUPS // S5_TILE_GROUPS


def _s5_scan_kernel(u_ref, b_ref, c_ref, d_ref, tab_ref, z_ref, bu_ref, carry_ref, *, tc):
    ns = S5_TILE_STATES

    @pl.when(pl.program_id(1) == 0)
    def _():
        carry_ref[...] = jnp.zeros(carry_ref.shape, F32)

    u = u_ref[...]
    bu_ref[...] = jnp.dot(u.astype(BF16), b_ref[...], preferred_element_type=F32)
    steps = [(1, tab_ref[0], tab_ref[1]), (2, tab_ref[2], tab_ref[3]), (4, tab_ref[4], tab_ref[5])]
    pr = tab_ref[6]
    pi = tab_ref[7]

    def body(i, carry):
        cr, ci = carry
        r0 = pl.multiple_of(i * SUBLANES, SUBLANES)
        xr = bu_ref[pl.ds(r0, SUBLANES), 0:ns]
        xi = bu_ref[pl.ds(r0, SUBLANES), ns:2 * ns]
        for s, ar, ai in steps:
            sr = pltpu.roll(xr, s, 0)
            si = pltpu.roll(xi, s, 0)
            xr, xi = xr + ar * sr - ai * si, xi + ar * si + ai * sr
        xr, xi = xr + pr * cr - pi * ci, xi + pr * ci + pi * cr
        bu_ref[pl.ds(r0, SUBLANES), 0:ns] = xr
        bu_ref[pl.ds(r0, SUBLANES), ns:2 * ns] = xi
        return xr[SUBLANES - 1:SUBLANES, :], xi[SUBLANES - 1:SUBLANES, :]

    cr, ci = lax.fori_loop(0, tc // SUBLANES, body, (carry_ref[0:1, 0:ns], carry_ref[0:1, ns:2 * ns]))
    carry_ref[0:1, 0:ns] = cr
    carry_ref[0:1, ns:2 * ns] = ci
    y = jnp.dot(bu_ref[...].astype(BF16), c_ref[...], preferred_element_type=F32) + d_ref[...] * u
    z_ref[...] = jax.nn.gelu(y)


def _s5_prepare(lam_re, lam_im, log_dt, b_re, b_im, c_re, c_im):
    dt = jnp.exp(log_dt.astype(F32))[:, None]
    lr = lam_re.astype(F32)
    li = lam_im.astype(F32)
    mag = jnp.exp(lr * dt)
    ab_re = mag * jnp.cos(li * dt)
    ab_im = mag * jnp.sin(li * dt)
    den = lr * lr + li * li
    nr = ab_re - 1.0
    f_re = (nr * lr + ab_im * li) / den
    f_im = (ab_im * lr - nr * li) / den
    br = b_re.astype(F32)
    bim = b_im.astype(F32)
    bb_re = f_re[..., None] * br - f_im[..., None] * bim
    bb_im = f_re[..., None] * bim + f_im[..., None] * br
    eye = jnp.eye(S5_TILE_GROUPS, dtype=F32)

    def blockdiag_in(bb):
        t = bb.reshape(S5_TILES, S5_TILE_GROUPS, S5_STATE, S5_GROUP)
        m = jnp.einsum("jgpc,gh->jgchp", t, eye)
        return m.reshape(S5_TILES, LANES, S5_TILE_STATES)

    def blockdiag_out(cc):
        t = cc.reshape(S5_TILES, S5_TILE_GROUPS, S5_GROUP, S5_STATE)
        m = jnp.einsum("jgcp,gh->jgphc", t, eye)
        return m.reshape(S5_TILES, S5_TILE_STATES, LANES)

    b_cat = jnp.concatenate([blockdiag_in(bb_re), blockdiag_in(bb_im)], axis=2).astype(BF16)
    c_cat = jnp.concatenate([blockdiag_out(c_re.astype(F32)), -blockdiag_out(c_im.astype(F32))], axis=1).astype(BF16)

    a_r = ab_re.reshape(S5_TILES, 1, S5_TILE_STATES)
    a_i = ab_im.reshape(S5_TILES, 1, S5_TILE_STATES)

    def cmul(xr, xi, yr, yi):
        return xr * yr - xi * yi, xr * yi + xi * yr

    a2 = cmul(a_r, a_i, a_r, a_i)
    a4 = cmul(*a2, *a2)
    row = jnp.arange(SUBLANES)[None, :, None]
    tabs = []
    for s, (pr_, pi_) in ((1, (a_r, a_i)), (2, a2), (4, a4)):
        tabs.append(jnp.where(row >= s, pr_, 0.0))
        tabs.append(jnp.where(row >= s, pi_, 0.0))
    pw = [(a_r, a_i)]
    for _ in range(SUBLANES - 1):
        pw.append(cmul(*pw[-1], a_r, a_i))
    tabs.append(jnp.concatenate([p[0] for p in pw], axis=1))
    tabs.append(jnp.concatenate([p[1] for p in pw], axis=1))
    tab = jnp.stack([jnp.broadcast_to(t, (S5_TILES, SUBLANES, S5_TILE_STATES)) for t in tabs], axis=1)
    return b_cat, c_cat, tab.astype(F32)


def _s5_scan(proj, u_cb, b_cat, c_cat, d_skip, tab, *, tc):
    seq = proj.shape[0]
    ns = S5_TILE_STATES
    d = d_skip.astype(F32).reshape(S5_TILES, 1, LANES)
    return pl.pallas_call(
        functools.partial(_s5_scan_kernel, tc=tc),
        grid=(S5_TILES, seq // tc),
        in_specs=[pl.BlockSpec((tc, LANES), lambda j, c: (c, u_cb + j)),
                  pl.BlockSpec((None, LANES, 2 * ns), lambda j, c: (j, 0, 0)),
                  pl.BlockSpec((None, 2 * ns, LANES), lambda j, c: (j, 0, 0)),
                  pl.BlockSpec((None, 1, LANES), lambda j, c: (j, 0, 0)),
                  pl.BlockSpec((None, 8, SUBLANES, ns), lambda j, c: (j, 0, 0, 0))],
        out_specs=pl.BlockSpec((tc, LANES), lambda j, c: (c, j)),
        out_shape=jax.ShapeDtypeStruct((seq, S5_WIDTH), F32),
        scratch_shapes=[pltpu.VMEM((tc, 2 * ns), F32), pltpu.VMEM((SUBLANES, 2 * ns), F32)],
        compiler_params=_cparams("parallel", "arbitrary"),
        name="s5_scan",
    )(proj, b_cat, c_cat, d, tab)


def _s5_glu_kernel(z_ref, w_ref, g_ref, o_ref, *, tn):
    j = pl.program_id(1)
    z = z_ref[...]
    a = jnp.dot(z.astype(BF16), w_ref[...], preferred_element_type=F32)
    zc = z_ref[:, pl.ds(pl.multiple_of(j * tn, tn), tn)]
    o_ref[...] = (zc * jax.nn.sigmoid(a) * _silu(g_ref[...])).astype(o_ref.dtype)


def _s5_glu(z, w_glu, proj, g_cb, *, tm, tn):
    seq, n = z.shape
    return pl.pallas_call(
        functools.partial(_s5_glu_kernel, tn=tn),
        grid=(seq // tm, n // tn),
        in_specs=[pl.BlockSpec((tm, n), lambda i, j: (i, 0)),
                  pl.BlockSpec((n, tn), lambda i, j: (0, j)),
                  pl.BlockSpec((tm, tn), lambda i, j: (i, g_cb * (n // tn) + j))],
        out_specs=pl.BlockSpec((tm, tn), lambda i, j: (i, j)),
        out_shape=jax.ShapeDtypeStruct((seq, n), BF16),
        compiler_params=_cparams("parallel", "arbitrary"),
        name="s5_glu",
    )(z, w_glu, proj)


def _rope_tables(pos, freq):
    ang = pos * freq
    lane = lax.broadcasted_iota(I32, ang.shape, 1)
    half = MLA_ROPE // 2
    cos = jnp.cos(ang)
    sin = jnp.sin(ang)
    c = jnp.where(lane < MLA_ROPE, cos, 0.0)
    s1 = jnp.where(lane < half, -sin, 0.0)
    s2 = jnp.where((lane >= half) & (lane < MLA_ROPE), sin, 0.0)
    return c, s1, s2


def _rope_apply(x, c, s1, s2):
    half = MLA_ROPE // 2
    return x * c + pltpu.roll(x, LANES - half, 1) * s1 + pltpu.roll(x, half, 1) * s2


def _mla_prep_kernel(qf_ref, kvf_ref, kr_ref, pos_ref, freq_ref, q_ref, k_ref, v_ref):
    c, s1, s2 = _rope_tables(pos_ref[...], freq_ref[...])
    kr = _rope_apply(kr_ref[...], c, s1, s2).astype(BF16)
    for h in range(MLA_HEADS):
        b = 2 * LANES * h
        q_ref[:, b:b + LANES] = qf_ref[:, b:b + LANES].astype(BF16)
        q_ref[:, b + LANES:b + 2 * LANES] = _rope_apply(qf_ref[:, b + LANES:b + 2 * LANES], c, s1, s2).astype(BF16)
        k_ref[:, b:b + LANES] = kvf_ref[:, b:b + LANES].astype(BF16)
        k_ref[:, b + LANES:b + 2 * LANES] = kr
        v_ref[:, LANES * h:LANES * (h + 1)] = kvf_ref[:, b + LANES:b + 2 * LANES].astype(BF16)


def _mla_prep(qf, kvf, proj, kr_cb, pos_col, freq, *, t):
    seq = qf.shape[0]
    w = 2 * LANES * MLA_HEADS
    return pl.pallas_call(
        _mla_prep_kernel,
        grid=(seq // t,),
        in_specs=[pl.BlockSpec((t, w), lambda i: (i, 0)),
                  pl.BlockSpec((t, w), lambda i: (i, 0)),
                  pl.BlockSpec((t, LANES), lambda i: (i, kr_cb)),
                  pl.BlockSpec((t, 1), lambda i: (i, 0)),
                  pl.BlockSpec((1, LANES), lambda i: (0, 0))],
        out_specs=[pl.BlockSpec((t, w), lambda i: (i, 0)),
                   pl.BlockSpec((t, w), lambda i: (i, 0)),
                   pl.BlockSpec((t, MLA_WIDTH), lambda i: (i, 0))],
        out_shape=[jax.ShapeDtypeStruct((seq, w), BF16), jax.ShapeDtypeStruct((seq, w), BF16),
                   jax.ShapeDtypeStruct((seq, MLA_WIDTH), BF16)],
        compiler_params=_cparams("parallel"),
        name="mla_prep",
    )(qf, kvf, proj, pos_col, freq)


def _t5_bucket(dist):
    n = jnp.maximum(dist, 0)
    max_exact = T5_BUCKETS // 2
    log_ratio = jnp.log(jnp.maximum(n, 1).astype(F32) / max_exact) / math.log(T5_MAX_DIST / max_exact)
    large = jnp.minimum(max_exact + (log_ratio * (T5_BUCKETS - max_exact)).astype(I32), T5_BUCKETS - 1)
    return jnp.where(n < max_exact, n, large)


def _t5_lookup(table_row, bucket):
    rows, width = bucket.shape
    tab = jnp.broadcast_to(table_row, (rows, LANES))
    parts = [jnp.take_along_axis(tab, bucket[:, c:c + LANES], axis=1) for c in range(0, width, LANES)]
    return parts[0] if len(parts) == 1 else jnp.concatenate(parts, axis=1)


def _nsa_cmp_kernel(x_ref, pe_ref, w1_ref, w2_ref, o_ref, xf_ref, *, nc):
    half = NSA_CMP_LEN // 2
    d = NSA_HEAD_DIM
    xf_ref[...] = x_ref[...].astype(F32)
    u = jnp.zeros((nc, NSA_CMP_HIDDEN), F32)
    v = jnp.zeros((nc, NSA_CMP_HIDDEN), F32)
    for r in range(half):
        a = xf_ref[pl.ds(r, nc, stride=NSA_CMP_STRIDE), :]
        u = u + jnp.dot((a + pe_ref[r:r + 1, :]).astype(BF16), w1_ref[r * d:(r + 1) * d, :],
                        preferred_element_type=F32)
        v = v + jnp.dot((a + pe_ref[half + r:half + r + 1, :]).astype(BF16),
                        w1_ref[(half + r) * d:(half + r + 1) * d, :], preferred_element_type=F32)
    hid = u + pltpu.roll(v, nc - 1, 0)
    o_ref[...] = jnp.dot(jax.nn.gelu(hid).astype(BF16), w2_ref[...], preferred_element_type=F32).astype(o_ref.dtype)


def _nsa_compress(proj, k_cb, pe, w1, w2):
    seq = proj.shape[0]
    nc = seq // NSA_CMP_STRIDE
    d = NSA_HEAD_DIM
    g = NSA_KV_GROUPS
    return pl.pallas_call(
        functools.partial(_nsa_cmp_kernel, nc=nc),
        grid=(2, g),
        in_specs=[pl.BlockSpec((seq, d), lambda a, b: (0, k_cb + a * g + b)),
                  pl.BlockSpec((None, NSA_CMP_LEN, d), lambda a, b: (a, 0, 0)),
                  pl.BlockSpec((None, NSA_CMP_LEN * d, NSA_CMP_HIDDEN), lambda a, b: (a, 0, 0)),
                  pl.BlockSpec((None, NSA_CMP_HIDDEN, d), lambda a, b: (a, 0, 0))],
        out_specs=pl.BlockSpec((None, None, nc, d), lambda a, b: (a, b, 0, 0)),
        out_shape=jax.ShapeDtypeStruct((2, g, nc, d), BF16),
        scratch_shapes=[pltpu.VMEM((seq, d), F32)],
        compiler_params=_cparams("parallel", "arbitrary"),
        name="nsa_compress",
    )(proj, pe, w1, w2)


def _nsa_select_kernel(q_ref, kc_ref, vc_ref, posq_ref, posc_ref, tab_ref, gate_ref, ov_ref,
                       oc_ref, sel_ref, *, nc, n_slc):
    qi = pl.program_id(0)
    d = NSA_HEAD_DIM
    scale = d ** -0.5
    t = qi * QB + lax.broadcasted_iota(I32, (QB, 1), 0)
    n_idx = lax.broadcasted_iota(I32, (1, nc), 1)
    cmp_end = n_idx * NSA_CMP_STRIDE + (NSA_CMP_LEN - 1)
    valid = cmp_end <= t
    dist = posq_ref[...] - posc_ref[...]
    buckets = [_t5_bucket(dist[:, c * LANES:(c + 1) * LANES]) for c in range(nc // LANES)]
    gates = jax.nn.sigmoid(gate_ref[...])
    ov = ov_ref[...]
    jl = lax.broadcasted_iota(I32, (QB, LANES), 1)
    cur = t // NSA_SLC_BLOCK
    forced = (jl == 0) | (jl == cur) | (jl == cur - 1)
    js = lax.broadcasted_iota(I32, (LANES, QB), 0).astype(F32)
    for g in range(NSA_KV_GROUPS):
        kc = kc_ref[g]
        vc = vc_ref[g]
        psum = jnp.zeros((QB, nc), F32)
        for r in range(NSA_REP):
            h = g * NSA_REP + r
            q = q_ref[:, h * d:(h + 1) * d]
            s = lax.dot_general(q, kc, (((1,), (1,)), ((), ())), preferred_element_type=F32) * scale
            bias = jnp.concatenate([_t5_lookup(tab_ref[h:h + 1, :], b) for b in buckets], axis=1)
            s = jnp.where(valid, s + bias, NEG_INF)
            m = jnp.max(s, axis=1, keepdims=True)
            e = jnp.exp(s - m)
            p = jnp.where(valid, e / jnp.sum(e, axis=1, keepdims=True), 0.0)
            o = jnp.dot(p.astype(BF16), vc, preferred_element_type=F32)
            oc_ref[:, h * d:(h + 1) * d] = gates[:, 3 * h:3 * h + 1] * o
            psum = psum + p
        p_hi = psum.astype(BF16)
        p_lo = (psum - p_hi.astype(F32)).astype(BF16)
        imp = jnp.dot(p_hi, ov, preferred_element_type=F32) + jnp.dot(p_lo, ov, preferred_element_type=F32)
        score = jnp.where(forced, FORCE_SCORE, jnp.where(jl > cur, -1.0, imp))
        score = jnp.where(jl < n_slc, score, -2.0)
        st = score.T
        sel = jnp.zeros((LANES, QB), F32)
        for _ in range(NSA_SLC_TOPK):
            mx = jnp.max(st, axis=0, keepdims=True)
            first = jnp.min(jnp.where(st == mx, js, float(LANES)), axis=0, keepdims=True)
            hit = js == first
            sel = jnp.where(hit, 1.0, sel)
            st = jnp.where(hit, -3e38, st)
        sel_ref[g] = sel.astype(sel_ref.dtype)


def _nsa_select(proj_b, q_cb, proj_f, gate_cb, kc, vc, pos_col, pos_cmp, tab_t, ov, *, n_slc):
    seq = proj_b.shape[0]
    nc = kc.shape[1]
    g = NSA_KV_GROUPS
    d = NSA_HEAD_DIM
    return pl.pallas_call(
        functools.partial(_nsa_select_kernel, nc=nc, n_slc=n_slc),
        grid=(seq // QB,),
        in_specs=[pl.BlockSpec((QB, NSA_WIDTH), lambda i: (i, q_cb)),
                  pl.BlockSpec((g, nc, d), lambda i: (0, 0, 0)),
                  pl.BlockSpec((g, nc, d), lambda i: (0, 0, 0)),
                  pl.BlockSpec((QB, 1), lambda i: (i, 0)),
                  pl.BlockSpec((1, nc), lambda i: (0, 0)),
                  pl.BlockSpec((SUBLANES, LANES), lambda i: (0, 0)),
                  pl.BlockSpec((QB, LANES), lambda i: (i, gate_cb)),
                  pl.BlockSpec((nc, LANES), lambda i: (0, 0))],
        out_specs=[pl.BlockSpec((QB, NSA_WIDTH), lambda i: (i, 0)),
                   pl.BlockSpec((g, LANES, QB), lambda i: (0, 0, i))],
        out_shape=[jax.ShapeDtypeStruct((seq, NSA_WIDTH), F32),
                   jax.ShapeDtypeStruct((g, LANES, seq), BF16)],
        compiler_params=_cparams("parallel"),
        name="nsa_select",
    )(proj_b, kc, vc, pos_col, pos_cmp, tab_t, proj_f, ov)


def _nsa_slc_kernel(pqmin_ref, pkmax_ref, q_ref, k_ref, v_ref, sel_ref, posq_ref, posk_ref, tab_ref, o_ref,
                    vt_ref, qt_ref, s_ref, m_ref, acc_ref, *, tq, tk):
    g = pl.program_id(0)
    qi = pl.program_id(1)
    d = NSA_HEAD_DIM
    scale = d ** -0.5
    seq = k_ref.shape[0]

    @pl.when(qi == 0)
    def _():
        _build_vt(v_ref, vt_ref, 0, d, seq, tk)

    for r in range(NSA_REP):
        qt_ref[r] = (q_ref[:, r * d:(r + 1) * d].astype(F32) * (scale * LOG2E)).T.astype(BF16)
    m_ref[...] = jnp.full(m_ref.shape, NEG_INF, F32)
    acc_ref[...] = jnp.zeros(acc_ref.shape, F32)
    sel = sel_ref[...]
    pos_q = posq_ref[...]
    key_blk = lax.broadcasted_iota(I32, (tk, LANES), 0) // NSA_SLC_BLOCK
    blk = lax.broadcasted_iota(I32, (tk, LANES), 1)
    per_tile = tk // NSA_SLC_BLOCK
    n_full = (qi * tq) // tk

    def scores(j, slot):
        start = pl.multiple_of(j * tk, tk)
        k = k_ref[pl.ds(start, tk), :]
        for r in range(NSA_REP):
            s_ref[slot, r] = jnp.dot(k, qt_ref[r], preferred_element_type=F32)

    def softmax_pv(j, slot, masked, near):
        start = pl.multiple_of(j * tk, tk)
        expand = jnp.where(blk == key_blk + j * per_tile, 1.0, 0.0).astype(BF16)
        picked = jnp.dot(expand, sel, preferred_element_type=F32)
        if masked:
            key = start + lax.broadcasted_iota(I32, (tk, tq), 0)
            qry = qi * tq + lax.broadcasted_iota(I32, (tk, tq), 1)
            picked = jnp.where(key <= qry, picked, 0.0)
        mask = picked > 0.5
        if near:
            pos_k = jnp.concatenate([posk_ref[pl.ds(start, tk), :]] * (tq // LANES), axis=1)
            bucket = _t5_bucket(pos_q - pos_k)
        for r in range(NSA_REP):
            h = g * NSA_REP + r
            t = s_ref[slot, r]
            if near:
                t = t + _t5_lookup(tab_ref[pl.ds(h, 1), :], bucket)
            t = jnp.where(mask, t, NEG_INF)
            m_prev = m_ref[r]
            mx = jnp.max(t, axis=0, keepdims=True)
            if near:
                m_new = jnp.maximum(m_prev, mx)
                shift = m_new
            else:
                b = tab_ref[pl.ds(h, 1), T5_BUCKETS - 1:T5_BUCKETS]
                m_new = jnp.maximum(m_prev, mx + b)
                shift = m_new - b
            alpha = jnp.exp2(m_prev - m_new)
            p = jnp.exp2(t - shift).astype(BF16)
            acc_ref[r] = alpha * acc_ref[r] + jnp.dot(vt_ref[0, :, pl.ds(start, tk)], p, preferred_element_type=F32)
            m_ref[r] = m_new

    def stage(j, slot):
        far = pqmin_ref[qi] - pkmax_ref[j] >= T5_MAX_DIST

        @pl.when(far)
        def _():
            scores(j + 1, 1 - slot)
            softmax_pv(j, slot, False, False)

        @pl.when(jnp.logical_not(far))
        def _():
            scores(j + 1, 1 - slot)
            softmax_pv(j, slot, False, True)

    scores(0, 0)

    def pair(jj, carry):
        stage(2 * jj, 0)
        stage(2 * jj + 1, 1)
        return carry

    pairs = n_full // 2
    lax.fori_loop(0, pairs, pair, 0)
    last = 2 * pairs

    @pl.when(n_full % 2 == 1)
    def _():
        stage(last, 0)
        softmax_pv(last + 1, 1, True, True)

    @pl.when(n_full % 2 == 0)
    def _():
        softmax_pv(last, 0, True, True)

    for r in range(NSA_REP):
        a = acc_ref[r]
        o_ref[:, r * d:(r + 1) * d] = (a[0:d, :] / a[d:d + 1, :]).T


def _nsa_slc(proj_b, q_off, k_off, v_off, sel, pos_row, pos_rep, tab_t, pq_min, pk_max, *, tq, tk):
    seq = proj_b.shape[0]
    tab_t = tab_t * LOG2E
    d = NSA_HEAD_DIM
    gw = NSA_REP * d
    once = pl.Buffered(1)
    grid_spec = pltpu.PrefetchScalarGridSpec(
        num_scalar_prefetch=2,
        grid=(NSA_KV_GROUPS, seq // tq),
        in_specs=[pl.BlockSpec((tq, gw), lambda g, i, *_: (i, q_off // gw + g)),
                  pl.BlockSpec((seq, d), lambda g, i, *_: (0, k_off // d + g), pipeline_mode=once),
                  pl.BlockSpec((seq, d), lambda g, i, *_: (0, v_off // d + g), pipeline_mode=once),
                  pl.BlockSpec((None, LANES, tq), lambda g, i, *_: (g, 0, i)),
                  pl.BlockSpec((1, tq), lambda g, i, *_: (0, i)),
                  pl.BlockSpec((seq, LANES), lambda g, i, *_: (0, 0), pipeline_mode=once),
                  pl.BlockSpec((SUBLANES, LANES), lambda g, i, *_: (0, 0))],
        out_specs=pl.BlockSpec((tq, gw), lambda g, i, *_: (i, g)),
        scratch_shapes=[pltpu.VMEM((1, d + ONES_ROWS, seq), BF16), pltpu.VMEM((NSA_REP, d, tq), BF16),
                        pltpu.VMEM((2, NSA_REP, tk, tq), F32), pltpu.VMEM((NSA_REP, 1, tq), F32),
                        pltpu.VMEM((NSA_REP, d + ONES_ROWS, tq), F32)])
    return pl.pallas_call(
        functools.partial(_nsa_slc_kernel, tq=tq, tk=tk),
        grid_spec=grid_spec,
        out_shape=jax.ShapeDtypeStruct((seq, NSA_WIDTH), F32),
        compiler_params=_cparams("parallel", "arbitrary"),
        name="nsa_slc",
    )(pq_min, pk_max, proj_b, proj_b, proj_b, sel, pos_row, pos_rep, tab_t)


NSA_WIN_TILES = NSA_WINDOW // QB + 1


def _nsa_win_kernel(*refs):
    nt = NSA_WIN_TILES
    q_ref = refs[0]
    k_refs = refs[1:1 + nt]
    v_refs = refs[1 + nt:1 + 2 * nt]
    pk_refs = refs[1 + 2 * nt:1 + 3 * nt]
    posq_ref, tab_ref, gate_ref, gout_ref, oc_ref, os_ref, o_ref = refs[1 + 3 * nt:]
    qi = pl.program_id(0)
    d = NSA_HEAD_DIM
    scale = d ** -0.5
    t = qi * QB + lax.broadcasted_iota(I32, (QB, 1), 0)
    pos_q = posq_ref[...]
    lane = lax.broadcasted_iota(I32, (1, QB), 1)
    bands, buckets = [], []
    for jj in range(nt):
        kidx = (qi - (nt - 1) + jj) * QB + lane
        bands.append((kidx >= 0) & (kidx <= t) & (t - kidx < NSA_WINDOW))
        buckets.append(_t5_bucket(pos_q - pk_refs[jj][...]))
    band = jnp.concatenate(bands, axis=1)
    gates = jax.nn.sigmoid(gate_ref[...])
    for g in range(NSA_KV_GROUPS):
        k = jnp.concatenate([kr[:, g * d:(g + 1) * d] for kr in k_refs], axis=0)
        v = jnp.concatenate([vr[:, g * d:(g + 1) * d] for vr in v_refs], axis=0)
        for r in range(NSA_REP):
            h = g * NSA_REP + r
            hs = slice(h * d, (h + 1) * d)
            s = lax.dot_general(q_ref[:, hs], k, (((1,), (1,)), ((), ())), preferred_element_type=F32) * scale
            bias = jnp.concatenate([_t5_lookup(tab_ref[h:h + 1, :], b) for b in buckets], axis=1)
            s = jnp.where(band, s + bias, NEG_INF)
            m = jnp.max(s, axis=1, keepdims=True)
            p = jnp.exp(s - m)
            o_w = jnp.dot(p.astype(BF16), v, preferred_element_type=F32) / jnp.sum(p, axis=1, keepdims=True)
            o = oc_ref[:, hs] + gates[:, 3 * h + 1:3 * h + 2] * os_ref[:, hs] + gates[:, 3 * h + 2:3 * h + 3] * o_w
            o_ref[:, hs] = (o * _silu(gout_ref[:, hs])).astype(o_ref.dtype)


def _nsa_win(proj_b, q_cb, k_cb, v_cb, proj_f, gate_cb, gout_cb, oc, o_s, pos_col, pos_row, tab_t):
    seq = proj_b.shape[0]
    nt = NSA_WIN_TILES

    def band_rows(jj, cb):
        return pl.BlockSpec((QB, NSA_KV), lambda i: (jnp.maximum(i - (nt - 1) + jj, 0), cb))

    def band_pos(jj):
        return pl.BlockSpec((1, QB), lambda i: (0, jnp.maximum(i - (nt - 1) + jj, 0)))

    in_specs = [pl.BlockSpec((QB, NSA_WIDTH), lambda i: (i, q_cb))]
    in_specs += [band_rows(jj, k_cb) for jj in range(nt)]
    in_specs += [band_rows(jj, v_cb) for jj in range(nt)]
    in_specs += [band_pos(jj) for jj in range(nt)]
    in_specs += [pl.BlockSpec((QB, 1), lambda i: (i, 0)),
                 pl.BlockSpec((SUBLANES, LANES), lambda i: (0, 0)),
                 pl.BlockSpec((QB, LANES), lambda i: (i, gate_cb)),
                 pl.BlockSpec((QB, NSA_WIDTH), lambda i: (i, gout_cb)),
                 pl.BlockSpec((QB, NSA_WIDTH), lambda i: (i, 0)),
                 pl.BlockSpec((QB, NSA_WIDTH), lambda i: (i, 0))]
    args = [proj_b] * (1 + 2 * nt) + [pos_row] * nt + [pos_col, tab_t, proj_f, proj_f, oc, o_s]
    return pl.pallas_call(
        _nsa_win_kernel,
        grid=(seq // QB,),
        in_specs=in_specs,
        out_specs=pl.BlockSpec((QB, NSA_WIDTH), lambda i: (i, 0)),
        out_shape=jax.ShapeDtypeStruct((seq, NSA_WIDTH), BF16),
        compiler_params=_cparams("parallel"),
        name="nsa_win",
    )(*args)


def _in_proj(h, norm_g, w_b, w_f, tiles, name):
    proj_b = _norm_matmul(h, norm_g, w_b, tm=tiles["tm"], tn=512, name=name + "_b", out_dtype=BF16)
    proj_f = _norm_matmul(h, norm_g, w_f, tm=tiles["tm"], tn=768, name=name + "_f")
    return proj_b, proj_f


def _even_layer(h, mem_kv, norm_g, w_in, s5, w_glu, b_f, tiles):
    pb, pf = _in_proj(h, norm_g, *w_in, tiles, "in_proj_even")
    ob, of = EVEN_B_OFF, EVEN_F_OFF
    b_cat, c_cat, tab, d_skip = s5
    z = _s5_scan(pf, of["u"] // LANES, b_cat, c_cat, d_skip, tab, tc=tiles["s5_tc"])
    y_s5 = _s5_glu(z, w_glu, pf, of["g_s5"] // S5_WIDTH, tm=tiles["tm"], tn=512)
    decay = _decay(pf, of["f"] // LANES, b_f, t=tiles["decay_t"])
    d = FOX_HEAD_DIM
    y_fox = _flash(pb, ob["q"], pb, ob["k"], pb, ob["v"], pf, of["g_fox"], heads=FOX_HEADS, dk=d, dv=d,
                   scale=d ** -0.5, tq=tiles["attn_tq"], tk=tiles["attn_tk"], hp=tiles["attn_hp"], decay=decay,
                   name="fox_attn")
    y_mem = _mem_attn(pb, ob["q_mem"] // MEM_HEAD_DIM, pf, of["g_mem"] // MEM_HEAD_DIM, mem_kv, t=tiles["mem_t"])
    return y_s5, y_fox, y_mem


def _odd_layer(h, mem_kv, norm_g, w_in, mla, nsa, pos, tiles):
    pb, pf = _in_proj(h, norm_g, *w_in, tiles, "in_proj_odd")
    ob, of = ODD_B_OFF, ODD_F_OFF
    g_cq, g_ckv, w_uq, w_ukv, freq = mla
    pos_col, pos_col_f, pos_row, pos_cmp, pos_rep, pq_min, pk_max = pos
    qf = _norm_matmul(pf, g_cq, w_uq, x_cb=of["c_q"] // MLA_Q_RANK, tm=tiles["tm"], tn=512, name="mla_q_up")
    kvf = _norm_matmul(pf, g_ckv, w_ukv, x_cb=of["c_kv"] // MLA_KV_RANK, tm=tiles["tm"], tn=512, name="mla_kv_up")
    q_r, k_r, v_r = _mla_prep(qf, kvf, pf, of["k_rope"] // LANES, pos_col_f, freq, t=tiles["prep_t"])
    y_mla = _flash(q_r, 0, k_r, 0, v_r, 0, pf, of["g_mla"], heads=MLA_HEADS, dk=2 * LANES, dv=MLA_V,
                   scale=(MLA_NOPE + MLA_ROPE) ** -0.5, tq=tiles["attn_tq"], tk=tiles["attn_tk"], hp=tiles["attn_hp"],
                   name="mla_attn")
    pe, w1, w2, tab_t, ov, n_slc = nsa
    kvc = _nsa_compress(pb, ob["k_cmp"] // NSA_HEAD_DIM, pe, w1, w2)
    oc, sel = _nsa_select(pb, ob["q_nsa"] // NSA_WIDTH, pf, of["gates"] // LANES, kvc[0], kvc[1], pos_col, pos_cmp,
                          tab_t, ov, n_slc=n_slc)
    o_s = _nsa_slc(pb, ob["q_nsa"], ob["k_slc"], ob["v_slc"], sel, pos_row, pos_rep, tab_t, pq_min, pk_max,
                   tq=tiles["slc_tq"], tk=tiles["slc_tk"])
    y_nsa = _nsa_win(pb, ob["q_nsa"] // NSA_WIDTH, ob["k_win"] // NSA_KV, ob["v_win"] // NSA_KV,
                     pf, of["gates"] // LANES, of["g_nsa"] // NSA_WIDTH, oc, o_s, pos_col, pos_row, tab_t)
    y_mem = _mem_attn(pb, ob["q_mem"] // MEM_HEAD_DIM, pf, of["g_mem"] // MEM_HEAD_DIM, mem_kv, t=tiles["mem_t"])
    return y_mla, y_nsa, y_mem


def _tiles(seq):
    return {"tm": min(seq, 1024), "s5_tc": min(seq, 1024), "decay_t": min(seq, 512), "attn_tq": min(seq, 256),
            "attn_tk": min(seq, 512), "attn_hp": 2, "slc_tq": 256, "slc_tk": 512, "mem_t": min(seq, 512), "prep_t": min(seq, 256)}


def _context(positions, t5_table, seq, tiles):
    pos = positions[0]
    pos_col = pos.reshape(seq, 1)
    pos_row = pos.reshape(1, seq)
    pos_rep = jnp.broadcast_to(pos_col, (seq, LANES))
    pq_min = jnp.min(pos.reshape(seq // tiles["slc_tq"], tiles["slc_tq"]), axis=1)
    pk_max = jnp.max(pos.reshape(seq // tiles["slc_tk"], tiles["slc_tk"]), axis=1)
    nc = seq // NSA_CMP_STRIDE
    pos_cmp = jnp.pad(pos[NSA_CMP_LEN - 1::NSA_CMP_STRIDE], (0, 1)).reshape(1, nc)
    half = MLA_ROPE // 2
    inv_freq = ROPE_THETA ** (-jnp.arange(half, dtype=F32) / half)
    freq = jnp.concatenate([inv_freq, inv_freq, jnp.zeros((LANES - MLA_ROPE,), F32)]).reshape(1, LANES)
    tab_t = jnp.pad(t5_table.astype(F32).T, ((0, SUBLANES - NSA_HEADS), (0, LANES - T5_BUCKETS)))
    n_slc = seq // NSA_SLC_BLOCK
    cs = np.arange(nc) * NSA_CMP_STRIDE
    ss = np.arange(LANES) * NSA_SLC_BLOCK
    ov_np = np.clip(np.minimum(cs[:, None] + NSA_CMP_LEN, ss[None, :] + NSA_SLC_BLOCK)
                    - np.maximum(cs[:, None], ss[None, :]), 0, None) / NSA_CMP_LEN
    ov_np[nc - 1, :] = 0.0
    ov_np[:, n_slc:] = 0.0
    return {"pos": (pos_col, pos_col.astype(F32), pos_row, pos_cmp, pos_rep, pq_min, pk_max), "freq": freq,
            "tab_t": tab_t,
            "ov": jnp.asarray(ov_np, BF16), "n_slc": n_slc}


def _odd_params(i, mla_g_cq, mla_g_ckv, mla_w_uq, mla_w_ukv, nsa_cmp_pe, nsa_cmp_w1, nsa_cmp_w2, ctx):
    dq = MLA_NOPE + MLA_ROPE
    w_uq = mla_w_uq[i].reshape(MLA_Q_RANK, MLA_HEADS, dq)
    w_uq = jnp.pad(w_uq, ((0, 0), (0, 0), (0, 2 * LANES - dq))).reshape(MLA_Q_RANK, -1).astype(BF16)
    mla = (mla_g_cq[i], mla_g_ckv[i], w_uq, mla_w_ukv[i].astype(BF16), ctx["freq"])
    nsa = (nsa_cmp_pe[i].astype(F32), nsa_cmp_w1[i].astype(BF16), nsa_cmp_w2[i].astype(BF16), ctx["tab_t"],
           ctx["ov"], ctx["n_slc"])
    return mla, nsa


def kernel(x, mem, positions, norm_g, mem_norm_g, final_norm_g, t5_table, w_out, mem_w_kv, even_w_in, s5_lam_re,
           s5_lam_im, s5_log_dt, s5_b_re, s5_b_im, s5_c_re, s5_c_im, s5_d, s5_w_glu, fox_b_f, odd_w_in, mla_g_cq,
           mla_g_ckv, mla_w_uq, mla_w_ukv, nsa_cmp_pe, nsa_cmp_w1, nsa_cmp_w2):
    batch, seq, _ = x.shape
    assert batch == 1 and seq % 1024 == 0 and seq // NSA_SLC_BLOCK <= LANES
    depth = norm_g.shape[0]
    tiles = _tiles(seq)
    ctx = _context(positions, t5_table, seq, tiles)
    h = x[0]
    mem_w = mem_w_kv.astype(BF16)
    w_out_b = w_out.astype(BF16)
    for layer in range(depth):
        i = layer // 2
        mem_kv = _norm_matmul(mem[0], mem_norm_g, mem_w[layer], tm=mem.shape[1], tn=512, name="mem_kv")
        if layer % 2 == 0:
            w_in = (_reorder_w_in(even_w_in[i], EVEN_SPLITS, EVEN_B_ORDER),
                    _reorder_w_in(even_w_in[i], EVEN_SPLITS, EVEN_F_ORDER))
            b_cat, c_cat, tab = _s5_prepare(s5_lam_re[i], s5_lam_im[i], s5_log_dt[i], s5_b_re[i], s5_b_im[i],
                                            s5_c_re[i], s5_c_im[i])
            ys = _even_layer(h, mem_kv, norm_g[layer], w_in, (b_cat, c_cat, tab, s5_d[i]),
                             s5_w_glu[i].astype(BF16), fox_b_f[i], tiles)
            widths = (S5_WIDTH, FOX_WIDTH, MEM_WIDTH)
        else:
            w_in = (_reorder_w_in(odd_w_in[i], ODD_SPLITS, ODD_B_ORDER),
                    _reorder_w_in(odd_w_in[i], ODD_SPLITS, ODD_F_ORDER))
            mla, nsa = _odd_params(i, mla_g_cq, mla_g_ckv, mla_w_uq, mla_w_ukv, nsa_cmp_pe, nsa_cmp_w1, nsa_cmp_w2,
                                   ctx)
            ys = _odd_layer(h, mem_kv, norm_g[layer], w_in, mla, nsa, ctx["pos"], tiles)
            widths = (MLA_WIDTH, NSA_WIDTH, MEM_WIDTH)
        starts = np.concatenate([[0], np.cumsum(widths)])
        ws = [w_out_b[layer, int(starts[k]):int(starts[k + 1])] for k in range(3)]
        h = _out_proj(h, ys, ws, tm=tiles["tm"], tn=512)
    return _final_norm(h, final_norm_g, tm=tiles["mem_t"])[None]
```

```python
import functools
import math

import numpy as np
import jax
import jax.numpy as jnp
from jax import lax
from jax.experimental import pallas as pl
from jax.experimental.pallas import tpu as pltpu

F32 = jnp.float32
BF16 = jnp.bfloat16
I32 = jnp.int32

D_MODEL = 2048
DEPTH = 4
N_MEM = 256
RMS_EPS = 1e-6
NEG_INF = -1e30
LOG2E = math.log2(math.e)

S5_WIDTH = 1024
S5_GROUP = 16
S5_GROUPS = S5_WIDTH // S5_GROUP
S5_STATE = 64
FOX_HEADS = 8
FOX_HEAD_DIM = 128
FOX_WIDTH = FOX_HEADS * FOX_HEAD_DIM
MEM_HEADS = 4
MEM_HEAD_DIM = 128
MEM_WIDTH = MEM_HEADS * MEM_HEAD_DIM
MLA_HEADS = 8
MLA_Q_RANK = 512
MLA_KV_RANK = 512
MLA_NOPE = 128
MLA_ROPE = 64
MLA_V = 128
MLA_WIDTH = MLA_HEADS * MLA_V
ROPE_THETA = 10000.0
NSA_HEADS = 8
NSA_KV_GROUPS = 2
NSA_REP = NSA_HEADS // NSA_KV_GROUPS
NSA_HEAD_DIM = 128
NSA_WIDTH = NSA_HEADS * NSA_HEAD_DIM
NSA_KV = NSA_KV_GROUPS * NSA_HEAD_DIM
NSA_CMP_LEN = 32
NSA_CMP_STRIDE = 16
NSA_CMP_HIDDEN = 256
NSA_SLC_BLOCK = 64
NSA_SLC_TOPK = 16
NSA_WINDOW = 512
FORCE_SCORE = 1e6
T5_BUCKETS = 32
T5_MAX_DIST = 1024

EVEN_SPLITS = (S5_WIDTH, S5_WIDTH, FOX_WIDTH, FOX_WIDTH, FOX_WIDTH, FOX_HEADS, FOX_WIDTH, MEM_WIDTH, MEM_WIDTH)
ODD_SPLITS = (MLA_Q_RANK, MLA_KV_RANK, MLA_ROPE, MLA_WIDTH, NSA_WIDTH, NSA_KV, NSA_KV, NSA_KV, NSA_KV, NSA_KV,
              NSA_KV, 3 * NSA_HEADS, NSA_WIDTH, MEM_WIDTH, MEM_WIDTH)

LANES = 128
SUBLANES = 8
VMEM_LIMIT_BYTES = 56 * 1024 * 1024

EVEN_B_ORDER = (("q", 2, 1024), ("k", 3, 1024), ("v", 4, 1024), ("q_mem", 7, 512))
EVEN_F_ORDER = (("u", 0, 1024), ("g_s5", 1, 1024), ("g_fox", 6, 1024), ("g_mem", 8, 512), ("f", 5, 128),
                ("pad", None, 128))
ODD_B_ORDER = (("q_nsa", 4, 1024), ("q_mem", 13, 512), ("k_cmp", 5, 256), ("v_cmp", 6, 256), ("k_slc", 7, 256),
               ("v_slc", 8, 256), ("k_win", 9, 256), ("v_win", 10, 256))
ODD_F_ORDER = (("g_nsa", 12, 1024), ("g_mla", 3, 1024), ("c_q", 0, 512), ("c_kv", 1, 512), ("g_mem", 14, 512),
               ("k_rope", 2, 128), ("gates", 11, 128))


def _layout(order):
    off, out = 0, {}
    for name, _, width in order:
        assert off % width == 0
        out[name] = off
        off += width
    return out, off


EVEN_B_OFF, EVEN_B_N = _layout(EVEN_B_ORDER)
EVEN_F_OFF, EVEN_F_N = _layout(EVEN_F_ORDER)
ODD_B_OFF, ODD_B_N = _layout(ODD_B_ORDER)
ODD_F_OFF, ODD_F_N = _layout(ODD_F_ORDER)


def _reorder_w_in(w, splits, order):
    starts = np.concatenate([[0], np.cumsum(splits)])
    cols = []
    for _, idx, width in order:
        if idx is None:
            cols.append(jnp.zeros((w.shape[0], width), w.dtype))
            continue
        seg = w[:, int(starts[idx]):int(starts[idx + 1])]
        pad = width - seg.shape[1]
        if pad:
            seg = jnp.pad(seg, ((0, 0), (0, pad)))
        cols.append(seg)
    return jnp.concatenate(cols, axis=1).astype(BF16)


def _cparams(*sem):
    return pltpu.CompilerParams(dimension_semantics=sem, vmem_limit_bytes=VMEM_LIMIT_BYTES)


def _silu(g):
    return g * jax.nn.sigmoid(g)


def _pick(n, cands):
    for c in cands:
        if n % c == 0:
            return c
    raise ValueError(f"no tile for {n} in {cands}")


def _norm_matmul_kernel(x_ref, g_ref, w_ref, o_ref, xn_ref):
    @pl.when(pl.program_id(1) == 0)
    def _():
        x = x_ref[...]
        ms = jnp.mean(x * x, axis=-1, keepdims=True)
        xn_ref[...] = (x * lax.rsqrt(ms + RMS_EPS) * g_ref[...]).astype(BF16)

    o_ref[...] = jnp.dot(xn_ref[...], w_ref[...], preferred_element_type=F32).astype(o_ref.dtype)


def _norm_matmul(x, g, w, *, x_cb=0, tm, tn, name, out_dtype=F32):
    m = x.shape[0]
    k, n = w.shape
    return pl.pallas_call(
        _norm_matmul_kernel,
        grid=(m // tm, n // tn),
        in_specs=[pl.BlockSpec((tm, k), lambda i, j: (i, x_cb)),
                  pl.BlockSpec((1, k), lambda i, j: (0, 0)),
                  pl.BlockSpec((k, tn), lambda i, j: (0, j))],
        out_specs=pl.BlockSpec((tm, tn), lambda i, j: (i, j)),
        out_shape=jax.ShapeDtypeStruct((m, n), out_dtype),
        scratch_shapes=[pltpu.VMEM((tm, k), BF16)],
        compiler_params=_cparams("parallel", "arbitrary"),
        name=name,
    )(x, g.reshape(1, k), w)


def _out_proj_kernel(h_ref, *refs):
    o_ref = refs[-1]
    n = (len(refs) - 1) // 2
    acc = h_ref[...]
    for y_ref, w_ref in zip(refs[:n], refs[n:2 * n]):
        acc = acc + jnp.dot(y_ref[...], w_ref[...], preferred_element_type=F32)
    o_ref[...] = acc


def _out_proj(h, ys, w_all, layer, *, tm, tn):
    m, n = h.shape
    in_specs = [pl.BlockSpec((tm, tn), lambda i, j: (i, j))]
    in_specs += [pl.BlockSpec((tm, y.shape[1]), lambda i, j: (i, 0)) for y in ys]
    row = 0
    for y in ys:
        width = y.shape[1]
        assert row % width == 0
        in_specs.append(pl.BlockSpec((None, width, tn), lambda i, j, rb=row // width: (layer, rb, j)))
        row += width
    return pl.pallas_call(
        _out_proj_kernel,
        grid=(m // tm, n // tn),
        in_specs=in_specs,
        out_specs=pl.BlockSpec((tm, tn), lambda i, j: (i, j)),
        out_shape=jax.ShapeDtypeStruct((m, n), F32),
        compiler_params=_cparams("parallel", "arbitrary"),
        name="out_proj",
    )(h, *ys, *([w_all] * len(ys)))


def _final_norm_kernel(x_ref, g_ref, o_ref):
    x = x_ref[...]
    ms = jnp.mean(x * x, axis=-1, keepdims=True)
    o_ref[...] = x * lax.rsqrt(ms + RMS_EPS) * g_ref[...]


def _final_norm(h, g, *, tm):
    m, n = h.shape
    return pl.pallas_call(
        _final_norm_kernel,
        grid=(m // tm,),
        in_specs=[pl.BlockSpec((tm, n), lambda i: (i, 0)), pl.BlockSpec((1, n), lambda i: (0, 0))],
        out_specs=pl.BlockSpec((tm, n), lambda i: (i, 0)),
        out_shape=jax.ShapeDtypeStruct((m, n), F32),
        compiler_params=_cparams("parallel"),
        name="final_norm",
    )(h, g.reshape(1, n))


ONES_ROWS = 16


def _build_vt(v_ref, vt_ref, hh, dv, seq, chunk):
    def body(c, carry):
        st = pl.multiple_of(c * chunk, chunk)
        vt_ref[hh, 0:dv, pl.ds(st, chunk)] = v_ref[pl.ds(st, chunk), hh * dv:(hh + 1) * dv].astype(F32).T.astype(BF16)
        return carry

    lax.fori_loop(0, seq // chunk, body, 0)
    vt_ref[hh, dv:dv + ONES_ROWS, :] = jnp.ones((ONES_ROWS, seq), BF16)


def _flash_kernel(*refs, scale, tq, tk, hp, dk, dv, has_decay):
    if has_decay:
        q_ref, k_ref, v_ref, g_ref, cq_ref, ck_ref, o_ref, vt_ref, qt_ref, s_ref, m_ref, acc_ref = refs
    else:
        q_ref, k_ref, v_ref, g_ref, o_ref, vt_ref, qt_ref, s_ref, m_ref, acc_ref = refs
    hb = pl.program_id(0)
    qi = pl.program_id(1)
    seq = k_ref.shape[0]

    @pl.when(qi == 0)
    def _():
        for hh in range(hp):
            _build_vt(v_ref, vt_ref, hh, dv, seq, tk)

    for hh in range(hp):
        qt_ref[hh] = (q_ref[:, hh * dk:(hh + 1) * dk].astype(F32) * (scale * LOG2E)).T.astype(BF16)
    m_ref[...] = jnp.full(m_ref.shape, NEG_INF, F32)
    acc_ref[...] = jnp.zeros(acc_ref.shape, F32)
    n_full = (qi * tq) // tk
    if has_decay:
        cq2 = [cq_ref[pl.ds(hb * hp + hh, 1), :] for hh in range(hp)]

    def scores(j, slot):
        start = pl.multiple_of(j * tk, tk)
        for hh in range(hp):
            s_ref[slot, hh] = jnp.dot(k_ref[pl.ds(start, tk), hh * dk:(hh + 1) * dk], qt_ref[hh],
                                      preferred_element_type=F32)

    def softmax_pv(j, slot, masked):
        start = pl.multiple_of(j * tk, tk)
        for hh in range(hp):
            t = s_ref[slot, hh]
            if has_decay:
                t = t - jnp.concatenate([ck_ref[hh, pl.ds(start, tk), :]] * (tq // LANES), axis=1)
            if masked:
                key = start + lax.broadcasted_iota(I32, (tk, tq), 0)
                qry = qi * tq + lax.broadcasted_iota(I32, (tk, tq), 1)
                t = jnp.where(key <= qry, t, NEG_INF)
            m_prev = m_ref[hh]
            mx = jnp.max(t, axis=0, keepdims=True)
            if has_decay:
                m_new = jnp.maximum(m_prev, mx + cq2[hh])
                shift = m_new - cq2[hh]
            else:
                m_new = jnp.maximum(m_prev, mx)
                shift = m_new
            alpha = jnp.exp2(m_prev - m_new)
            p = jnp.exp2(t - shift).astype(BF16)
            acc_ref[hh] = alpha * acc_ref[hh] + jnp.dot(vt_ref[hh, :, pl.ds(start, tk)], p,
                                                         preferred_element_type=F32)
            m_ref[hh] = m_new

    scores(0, 0)

    def pair(jj, carry):
        j = 2 * jj
        scores(j + 1, 1)
        softmax_pv(j, 0, False)
        scores(j + 2, 0)
        softmax_pv(j + 1, 1, False)
        return carry

    pairs = n_full // 2
    lax.fori_loop(0, pairs, pair, 0)
    last = 2 * pairs

    @pl.when(n_full % 2 == 1)
    def _():
        scores(last + 1, 1)
        softmax_pv(last, 0, False)
        softmax_pv(last + 1, 1, True)

    @pl.when(n_full % 2 == 0)
    def _():
        softmax_pv(last, 0, True)

    for hh in range(hp):
        a = acc_ref[hh]
        o = (a[0:dv, :] / a[dv:dv + 1, :]).T
        o_ref[:, hh * dv:(hh + 1) * dv] = (o * _silu(g_ref[:, hh * dv:(hh + 1) * dv])).astype(o_ref.dtype)


def _flash(q_arr, q_off, k_arr, k_off, v_arr, v_off, g_arr, g_off, *, heads, dk, dv, scale, tq, tk, hp,
           decay=None, name):
    seq = q_arr.shape[0]
    once = pl.Buffered(1)
    in_specs = [pl.BlockSpec((tq, hp * dk), lambda h, i: (i, q_off // (hp * dk) + h)),
                pl.BlockSpec((seq, hp * dk), lambda h, i: (0, k_off // (hp * dk) + h), pipeline_mode=once),
                pl.BlockSpec((seq, hp * dv), lambda h, i: (0, v_off // (hp * dv) + h), pipeline_mode=once),
                pl.BlockSpec((tq, hp * dv), lambda h, i: (i, g_off // (hp * dv) + h))]
    args = [q_arr, k_arr, v_arr, g_arr]
    if decay is not None:
        cum_t, cum_rep = decay
        in_specs += [pl.BlockSpec((SUBLANES, tq), lambda h, i: (0, i)),
                     pl.BlockSpec((hp, seq, LANES), lambda h, i: (h, 0, 0), pipeline_mode=once)]
        args += [cum_t, cum_rep]
    return pl.pallas_call(
        functools.partial(_flash_kernel, scale=scale, tq=tq, tk=tk, hp=hp, dk=dk, dv=dv, has_decay=decay is not None),
        grid=(heads // hp, seq // tq),
        in_specs=in_specs,
        out_specs=pl.BlockSpec((tq, hp * dv), lambda h, i: (i, h)),
        out_shape=jax.ShapeDtypeStruct((seq, heads * dv), BF16),
        scratch_shapes=[pltpu.VMEM((hp, dv + ONES_ROWS, seq), BF16), pltpu.VMEM((hp, dk, tq), BF16),
                        pltpu.VMEM((2, hp, tk, tq), F32), pltpu.VMEM((hp, 1, tq), F32),
                        pltpu.VMEM((hp, dv + ONES_ROWS, tq), F32)],
        compiler_params=_cparams("parallel", "arbitrary"),
        name=name,
    )(*args)


def _decay_kernel(f_ref, b_ref, ct_ref, cr_ref, carry_ref, *, t):
    i = pl.program_id(0)

    @pl.when(i == 0)
    def _():
        carry_ref[...] = jnp.zeros(carry_ref.shape, F32)

    x = f_ref[...] + b_ref[...]
    lf = jnp.minimum(x, 0.0) - jnp.log1p(jnp.exp(-jnp.abs(x)))
    row = lax.broadcasted_iota(I32, lf.shape, 0)
    s = 1
    while s < t:
        lf = lf + jnp.where(row >= s, pltpu.roll(lf, s, 0), 0.0)
        s *= 2
    lf = lf + carry_ref[...]
    carry_ref[...] = lf[t - 1:t, :]
    lf2 = lf * LOG2E
    ct_ref[...] = lf2.T[:FOX_HEADS, :]
    for h in range(FOX_HEADS):
        cr_ref[h] = jnp.broadcast_to(lf2[:, h:h + 1], (t, LANES))


def _decay(proj, f_cb, b_f, *, t):
    seq = proj.shape[0]
    b = jnp.pad(b_f.reshape(1, FOX_HEADS), ((0, 0), (0, LANES - FOX_HEADS)))
    return pl.pallas_call(
        functools.partial(_decay_kernel, t=t),
        grid=(seq // t,),
        in_specs=[pl.BlockSpec((t, LANES), lambda i: (i, f_cb)), pl.BlockSpec((1, LANES), lambda i: (0, 0))],
        out_specs=[pl.BlockSpec((FOX_HEADS, t), lambda i: (0, i)),
                   pl.BlockSpec((FOX_HEADS, t, LANES), lambda i: (0, i, 0))],
        out_shape=[jax.ShapeDtypeStruct((FOX_HEADS, seq), F32), jax.ShapeDtypeStruct((FOX_HEADS, seq, LANES), F32)],
        scratch_shapes=[pltpu.VMEM((1, LANES), F32)],
        compiler_params=_cparams("arbitrary"),
        name="fox_decay",
    )(proj, b)


def _mem_attn_kernel(q_ref, k_ref, v_ref, g_ref, o_ref):
    k = k_ref[...].astype(BF16)
    s = lax.dot_general(q_ref[...], k, (((1,), (1,)), ((), ())), preferred_element_type=F32) * (MEM_HEAD_DIM ** -0.5)
    m = jnp.max(s, axis=1, keepdims=True)
    p = jnp.exp(s - m)
    l = jnp.sum(p, axis=1, keepdims=True)
    o = jnp.dot(p.astype(BF16), v_ref[...].astype(BF16), preferred_element_type=F32) / l
    o_ref[...] = (o * _silu(g_ref[...])).astype(o_ref.dtype)


def _mem_kv_all(mem2d, g, w_all):
    depth, k, n = w_all.shape
    m = mem2d.shape[0]
    tn = 512
    return pl.pallas_call(
        _norm_matmul_kernel,
        grid=(depth, n // tn),
        in_specs=[pl.BlockSpec((m, k), lambda l, j: (0, 0)),
                  pl.BlockSpec((1, k), lambda l, j: (0, 0)),
                  pl.BlockSpec((None, k, tn), lambda l, j: (l, 0, j))],
        out_specs=pl.BlockSpec((None, m, tn), lambda l, j: (l, 0, j)),
        out_shape=jax.ShapeDtypeStruct((depth, m, n), F32),
        scratch_shapes=[pltpu.VMEM((m, k), BF16)],
        compiler_params=_cparams("arbitrary", "arbitrary"),
        name="mem_kv",
    )(mem2d, g.reshape(1, k), w_all)


def _mem_attn(proj_b, q_cb, proj_f, g_cb, mem_kv, *, t):
    seq = proj_b.shape[0]
    mem_kv, layer = mem_kv
    nm = mem_kv.shape[1]
    d = MEM_HEAD_DIM
    return pl.pallas_call(
        _mem_attn_kernel,
        grid=(MEM_HEADS, seq // t),
        in_specs=[pl.BlockSpec((t, d), lambda h, i: (i, q_cb + h)),
                  pl.BlockSpec((None, nm, d), lambda h, i: (layer, 0, h)),
                  pl.BlockSpec((None, nm, d), lambda h, i: (layer, 0, MEM_HEADS + h)),
                  pl.BlockSpec((t, d), lambda h, i: (i, g_cb + h))],
        out_specs=pl.BlockSpec((t, d), lambda h, i: (i, h)),
        out_shape=jax.ShapeDtypeStruct((seq, MEM_WIDTH), BF16),
        compiler_params=_cparams("parallel", "arbitrary"),
        name="mem_attn",
    )(proj_b, mem_kv, mem_kv, proj_f)


S5_TILE_GROUPS = LANES // S5_GROUP
S5_TILE_STATES = S5_TILE_GROUPS * S5_STATE
S5_TILES = S5_GROUPS // S5_TILE_GROUPS


def _s5_scan_kernel(u_ref, b_ref, c_ref, d_ref, tab_ref, z_ref, bu_ref, carry_ref, *, tc):
    ns = S5_TILE_STATES

    @pl.when(pl.program_id(1) == 0)
    def _():
        carry_ref[...] = jnp.zeros(carry_ref.shape, F32)

    u = u_ref[...]
    bu_ref[...] = jnp.dot(u.astype(BF16), b_ref[...], preferred_element_type=F32)
    steps = [(1, tab_ref[0], tab_ref[1]), (2, tab_ref[2], tab_ref[3]), (4, tab_ref[4], tab_ref[5])]
    pr = tab_ref[6]
    pi = tab_ref[7]

    def body(i, carry):
        cr, ci = carry
        r0 = pl.multiple_of(i * SUBLANES, SUBLANES)
        xr = bu_ref[pl.ds(r0, SUBLANES), 0:ns]
        xi = bu_ref[pl.ds(r0, SUBLANES), ns:2 * ns]
        for s, ar, ai in steps:
            sr = pltpu.roll(xr, s, 0)
            si = pltpu.roll(xi, s, 0)
            xr, xi = xr + ar * sr - ai * si, xi + ar * si + ai * sr
        xr, xi = xr + pr * cr - pi * ci, xi + pr * ci + pi * cr
        bu_ref[pl.ds(r0, SUBLANES), 0:ns] = xr
        bu_ref[pl.ds(r0, SUBLANES), ns:2 * ns] = xi
        return xr[SUBLANES - 1:SUBLANES, :], xi[SUBLANES - 1:SUBLANES, :]

    cr, ci = lax.fori_loop(0, tc // SUBLANES, body, (carry_ref[0:1, 0:ns], carry_ref[0:1, ns:2 * ns]))
    carry_ref[0:1, 0:ns] = cr
    carry_ref[0:1, ns:2 * ns] = ci
    y = jnp.dot(bu_ref[...].astype(BF16), c_ref[...], preferred_element_type=F32) + d_ref[...] * u
    z_ref[...] = jax.nn.gelu(y)


def _s5_prepare(lam_re, lam_im, log_dt, b_re, b_im, c_re, c_im):
    dt = jnp.exp(log_dt.astype(F32))[:, None]
    lr = lam_re.astype(F32)
    li = lam_im.astype(F32)
    mag = jnp.exp(lr * dt)
    ab_re = mag * jnp.cos(li * dt)
    ab_im = mag * jnp.sin(li * dt)
    den = lr * lr + li * li
    nr = ab_re - 1.0
    f_re = (nr * lr + ab_im * li) / den
    f_im = (ab_im * lr - nr * li) / den
    br = b_re.astype(F32)
    bim = b_im.astype(F32)
    bb_re = f_re[..., None] * br - f_im[..., None] * bim
    bb_im = f_re[..., None] * bim + f_im[..., None] * br
    eye = jnp.eye(S5_TILE_GROUPS, dtype=F32)

    def blockdiag_in(bb):
        t = bb.reshape(S5_TILES, S5_TILE_GROUPS, S5_STATE, S5_GROUP)
        m = jnp.einsum("jgpc,gh->jgchp", t, eye)
        return m.reshape(S5_TILES, LANES, S5_TILE_STATES)

    def blockdiag_out(cc):
        t = cc.reshape(S5_TILES, S5_TILE_GROUPS, S5_GROUP, S5_STATE)
        m = jnp.einsum("jgcp,gh->jgphc", t, eye)
        return m.reshape(S5_TILES, S5_TILE_STATES, LANES)

    b_cat = jnp.concatenate([blockdiag_in(bb_re), blockdiag_in(bb_im)], axis=2).astype(BF16)
    c_cat = jnp.concatenate([blockdiag_out(c_re.astype(F32)), -blockdiag_out(c_im.astype(F32))], axis=1).astype(BF16)

    a_r = ab_re.reshape(S5_TILES, 1, S5_TILE_STATES)
    a_i = ab_im.reshape(S5_TILES, 1, S5_TILE_STATES)

    def cmul(xr, xi, yr, yi):
        return xr * yr - xi * yi, xr * yi + xi * yr

    a2 = cmul(a_r, a_i, a_r, a_i)
    a4 = cmul(*a2, *a2)
    row = jnp.arange(SUBLANES)[None, :, None]
    tabs = []
    for s, (pr_, pi_) in ((1, (a_r, a_i)), (2, a2), (4, a4)):
        tabs.append(jnp.where(row >= s, pr_, 0.0))
        tabs.append(jnp.where(row >= s, pi_, 0.0))
    pw = [(a_r, a_i)]
    for _ in range(SUBLANES - 1):
        pw.append(cmul(*pw[-1], a_r, a_i))
    tabs.append(jnp.concatenate([p[0] for p in pw], axis=1))
    tabs.append(jnp.concatenate([p[1] for p in pw], axis=1))
    tab = jnp.stack([jnp.broadcast_to(t, (S5_TILES, SUBLANES, S5_TILE_STATES)) for t in tabs], axis=1)
    return b_cat, c_cat, tab.astype(F32)


def _s5_scan(proj, u_cb, b_cat, c_cat, d_skip, tab, *, tc):
    seq = proj.shape[0]
    ns = S5_TILE_STATES
    d = d_skip.astype(F32).reshape(S5_TILES, 1, LANES)
    return pl.pallas_call(
        functools.partial(_s5_scan_kernel, tc=tc),
        grid=(S5_TILES, seq // tc),
        in_specs=[pl.BlockSpec((tc, LANES), lambda j, c: (c, u_cb + j)),
                  pl.BlockSpec((None, LANES, 2 * ns), lambda j, c: (j, 0, 0)),
                  pl.BlockSpec((None, 2 * ns, LANES), lambda j, c: (j, 0, 0)),
                  pl.BlockSpec((None, 1, LANES), lambda j, c: (j, 0, 0)),
                  pl.BlockSpec((None, 8, SUBLANES, ns), lambda j, c: (j, 0, 0, 0))],
        out_specs=pl.BlockSpec((tc, LANES), lambda j, c: (c, j)),
        out_shape=jax.ShapeDtypeStruct((seq, S5_WIDTH), F32),
        scratch_shapes=[pltpu.VMEM((tc, 2 * ns), F32), pltpu.VMEM((SUBLANES, 2 * ns), F32)],
        compiler_params=_cparams("parallel", "arbitrary"),
        name="s5_scan",
    )(proj, b_cat, c_cat, d, tab)


def _s5_glu_kernel(z_ref, w_ref, g_ref, o_ref, *, tn):
    j = pl.program_id(1)
    z = z_ref[...]
    a = jnp.dot(z.astype(BF16), w_ref[...], preferred_element_type=F32)
    zc = z_ref[:, pl.ds(pl.multiple_of(j * tn, tn), tn)]
    o_ref[...] = (zc * jax.nn.sigmoid(a) * _silu(g_ref[...])).astype(o_ref.dtype)


def _s5_glu(z, w_glu, proj, g_cb, *, tm, tn):
    seq, n = z.shape
    return pl.pallas_call(
        functools.partial(_s5_glu_kernel, tn=tn),
        grid=(seq // tm, n // tn),
        in_specs=[pl.BlockSpec((tm, n), lambda i, j: (i, 0)),
                  pl.BlockSpec((n, tn), lambda i, j: (0, j)),
                  pl.BlockSpec((tm, tn), lambda i, j: (i, g_cb * (n // tn) + j))],
        out_specs=pl.BlockSpec((tm, tn), lambda i, j: (i, j)),
        out_shape=jax.ShapeDtypeStruct((seq, n), BF16),
        compiler_params=_cparams("parallel", "arbitrary"),
        name="s5_glu",
    )(z, w_glu, proj)


def _rope_tables(pos, freq):
    ang = pos * freq
    lane = lax.broadcasted_iota(I32, ang.shape, 1)
    half = MLA_ROPE // 2
    cos = jnp.cos(ang)
    sin = jnp.sin(ang)
    c = jnp.where(lane < MLA_ROPE, cos, 0.0)
    s1 = jnp.where(lane < half, -sin, 0.0)
    s2 = jnp.where((lane >= half) & (lane < MLA_ROPE), sin, 0.0)
    return c, s1, s2


def _rope_apply(x, c, s1, s2):
    half = MLA_ROPE // 2
    return x * c + pltpu.roll(x, LANES - half, 1) * s1 + pltpu.roll(x, half, 1) * s2


def _mla_prep_kernel(qf_ref, kvf_ref, kr_ref, pos_ref, freq_ref, q_ref, k_ref, v_ref):
    c, s1, s2 = _rope_tables(pos_ref[...], freq_ref[...])
    kr = _rope_apply(kr_ref[...], c, s1, s2).astype(BF16)
    for h in range(MLA_HEADS):
        b = 2 * LANES * h
        q_ref[:, b:b + LANES] = qf_ref[:, b:b + LANES].astype(BF16)
        q_ref[:, b + LANES:b + 2 * LANES] = _rope_apply(qf_ref[:, b + LANES:b + 2 * LANES], c, s1, s2).astype(BF16)
        k_ref[:, b:b + LANES] = kvf_ref[:, b:b + LANES].astype(BF16)
        k_ref[:, b + LANES:b + 2 * LANES] = kr
        v_ref[:, LANES * h:LANES * (h + 1)] = kvf_ref[:, b + LANES:b + 2 * LANES].astype(BF16)


def _mla_prep(qf, kvf, proj, kr_cb, pos_col, freq, *, t):
    seq = qf.shape[0]
    w = 2 * LANES * MLA_HEADS
    return pl.pallas_call(
        _mla_prep_kernel,
        grid=(seq // t,),
        in_specs=[pl.BlockSpec((t, w), lambda i: (i, 0)),
                  pl.BlockSpec((t, w), lambda i: (i, 0)),
                  pl.BlockSpec((t, LANES), lambda i: (i, kr_cb)),
                  pl.BlockSpec((t, 1), lambda i: (i, 0)),
                  pl.BlockSpec((1, LANES), lambda i: (0, 0))],
        out_specs=[pl.BlockSpec((t, w), lambda i: (i, 0)),
                   pl.BlockSpec((t, w), lambda i: (i, 0)),
                   pl.BlockSpec((t, MLA_WIDTH), lambda i: (i, 0))],
        out_shape=[jax.ShapeDtypeStruct((seq, w), BF16), jax.ShapeDtypeStruct((seq, w), BF16),
                   jax.ShapeDtypeStruct((seq, MLA_WIDTH), BF16)],
        compiler_params=_cparams("parallel"),
        name="mla_prep",
    )(qf, kvf, proj, pos_col, freq)


def _t5_bucket(dist):
    n = jnp.maximum(dist, 0)
    max_exact = T5_BUCKETS // 2
    log_ratio = jnp.log(jnp.maximum(n, 1).astype(F32) / max_exact) / math.log(T5_MAX_DIST / max_exact)
    large = jnp.minimum(max_exact + (log_ratio * (T5_BUCKETS - max_exact)).astype(I32), T5_BUCKETS - 1)
    return jnp.where(n < max_exact, n, large)


T5_MASK_BUCKET = T5_BUCKETS


def _t5_lookup(table_row, bucket):
    rows, width = bucket.shape
    tab = jnp.broadcast_to(table_row, (rows, LANES))
    parts = [jnp.take_along_axis(tab, bucket[:, c:c + LANES], axis=1, mode="promise_in_bounds")
             for c in range(0, width, LANES)]
    return parts[0] if len(parts) == 1 else jnp.concatenate(parts, axis=1)


def _nsa_cmp_kernel(x_ref, pe_ref, w1_ref, w2_ref, o_ref, xf_ref, *, nc):
    half = NSA_CMP_LEN // 2
    d = NSA_HEAD_DIM
    xf_ref[...] = x_ref[...].astype(F32)
    u = jnp.zeros((nc, NSA_CMP_HIDDEN), F32)
    v = jnp.zeros((nc, NSA_CMP_HIDDEN), F32)
    for r in range(half):
        a = xf_ref[pl.ds(r, nc, stride=NSA_CMP_STRIDE), :]
        u = u + jnp.dot((a + pe_ref[r:r + 1, :]).astype(BF16), w1_ref[r * d:(r + 1) * d, :],
                        preferred_element_type=F32)
        v = v + jnp.dot((a + pe_ref[half + r:half + r + 1, :]).astype(BF16),
                        w1_ref[(half + r) * d:(half + r + 1) * d, :], preferred_element_type=F32)
    hid = u + pltpu.roll(v, nc - 1, 0)
    o_ref[...] = jnp.dot(jax.nn.gelu(hid).astype(BF16), w2_ref[...], preferred_element_type=F32).astype(o_ref.dtype)


def _nsa_compress(proj, k_cb, pe, w1, w2):
    seq = proj.shape[0]
    nc = seq // NSA_CMP_STRIDE
    d = NSA_HEAD_DIM
    g = NSA_KV_GROUPS
    return pl.pallas_call(
        functools.partial(_nsa_cmp_kernel, nc=nc),
        grid=(2, g),
        in_specs=[pl.BlockSpec((seq, d), lambda a, b: (0, k_cb + a * g + b)),
                  pl.BlockSpec((None, NSA_CMP_LEN, d), lambda a, b: (a, 0, 0)),
                  pl.BlockSpec((None, NSA_CMP_LEN * d, NSA_CMP_HIDDEN), lambda a, b: (a, 0, 0)),
                  pl.BlockSpec((None, NSA_CMP_HIDDEN, d), lambda a, b: (a, 0, 0))],
        out_specs=pl.BlockSpec((None, None, nc, d), lambda a, b: (a, b, 0, 0)),
        out_shape=jax.ShapeDtypeStruct((2, g, nc, d), BF16),
        scratch_shapes=[pltpu.VMEM((seq, d), F32)],
        compiler_params=_cparams("parallel", "arbitrary"),
        name="nsa_compress",
    )(proj, pe, w1, w2)


def _nsa_select_kernel(q_ref, kc_ref, vc_ref, posq_ref, posc_ref, tab_ref, gate_ref, ov_ref,
                       oc_ref, sel_ref, *, nc, n_slc, tq):
    qi = pl.program_id(0)
    d = NSA_HEAD_DIM
    c = d ** -0.5 * LOG2E
    tok = qi * tq + lax.broadcasted_iota(I32, (1, tq), 1)
    cmp_end = lax.broadcasted_iota(I32, (nc, 1), 0) * NSA_CMP_STRIDE + (NSA_CMP_LEN - 1)
    valid = cmp_end <= tok
    pos_c = jnp.concatenate([posc_ref[...]] * (tq // LANES), axis=1)
    bucket = jnp.where(valid, _t5_bucket(posq_ref[...] - pos_c), T5_MASK_BUCKET)
    gates = jax.nn.sigmoid(gate_ref[...])
    ovt = ov_ref[...]
    js = lax.broadcasted_iota(I32, (LANES, tq), 0)
    jf = js.astype(F32)
    cur = tok // NSA_SLC_BLOCK
    forced = (js == 0) | (js == cur) | (js == cur - 1)
    for g in range(NSA_KV_GROUPS):
        kc = kc_ref[g]
        vct = vc_ref[g].astype(F32).T.astype(BF16)
        psum = jnp.zeros((nc, tq), F32)
        for r in range(NSA_REP):
            h = g * NSA_REP + r
            qt = (q_ref[:, h * d:(h + 1) * d].astype(F32) * c).T.astype(BF16)
            t = jnp.dot(kc, qt, preferred_element_type=F32) + _t5_lookup(tab_ref[h:h + 1, :], bucket)
            m = jnp.max(t, axis=0, keepdims=True)
            e = jnp.exp2(t - m)
            l = jnp.sum(e, axis=0, keepdims=True)
            p = e * jnp.where(m > 0.5 * NEG_INF, 1.0 / l, 0.0)
            o = jnp.dot(vct, p.astype(BF16), preferred_element_type=F32)
            oc_ref[:, h * d:(h + 1) * d] = gates[:, 3 * h:3 * h + 1] * o.T
            psum = psum + p
        p_hi = psum.astype(BF16)
        p_lo = (psum - p_hi.astype(F32)).astype(BF16)
        imp = jnp.dot(ovt, p_hi, preferred_element_type=F32) + jnp.dot(ovt, p_lo, preferred_element_type=F32)
        st = jnp.where(forced, FORCE_SCORE, jnp.where(js > cur, -1.0, imp))
        st = jnp.where(js < n_slc, st, -2.0)
        sel = jnp.zeros((LANES, tq), F32)
        for _ in range(NSA_SLC_TOPK):
            mx = jnp.max(st, axis=0, keepdims=True)
            first = jnp.min(jnp.where(st == mx, jf, float(LANES)), axis=0, keepdims=True)
            hit = jf == first
            sel = jnp.where(hit, 1.0, sel)
            st = jnp.where(hit, -3e38, st)
        sel_ref[g] = sel.astype(sel_ref.dtype)


def _nsa_select(proj_b, q_cb, proj_f, gate_cb, kc, vc, pos_row, pos_cmp_rep, tab_t, ov_t, *, n_slc, tq):
    seq = proj_b.shape[0]
    nc = kc.shape[1]
    g = NSA_KV_GROUPS
    d = NSA_HEAD_DIM
    return pl.pallas_call(
        functools.partial(_nsa_select_kernel, nc=nc, n_slc=n_slc, tq=tq),
        grid=(seq // tq,),
        in_specs=[pl.BlockSpec((tq, NSA_WIDTH), lambda i: (i, q_cb)),
                  pl.BlockSpec((g, nc, d), lambda i: (0, 0, 0)),
                  pl.BlockSpec((g, nc, d), lambda i: (0, 0, 0)),
                  pl.BlockSpec((1, tq), lambda i: (0, i)),
                  pl.BlockSpec((nc, LANES), lambda i: (0, 0)),
                  pl.BlockSpec((SUBLANES, LANES), lambda i: (0, 0)),
                  pl.BlockSpec((tq, LANES), lambda i: (i, gate_cb)),
                  pl.BlockSpec((LANES, nc), lambda i: (0, 0))],
        out_specs=[pl.BlockSpec((tq, NSA_WIDTH), lambda i: (i, 0)),
                   pl.BlockSpec((g, LANES, tq), lambda i: (0, 0, i))],
        out_shape=[jax.ShapeDtypeStruct((seq, NSA_WIDTH), F32),
                   jax.ShapeDtypeStruct((g, LANES, seq), BF16)],
        compiler_params=_cparams("parallel"),
        name="nsa_select",
    )(proj_b, kc, vc, pos_row, pos_cmp_rep, tab_t * LOG2E, proj_f, ov_t)


def _nsa_slc_kernel(pqmin_ref, pkmax_ref, q_ref, k_ref, v_ref, sel_ref, posq_ref, posk_ref, tab_ref, o_ref,
                    vt_ref, qt_ref, s_ref, m_ref, acc_ref, *, tq, tk):
    g = pl.program_id(0)
    qi = pl.program_id(1)
    d = NSA_HEAD_DIM
    scale = d ** -0.5
    seq = k_ref.shape[0]

    @pl.when(qi == 0)
    def _():
        _build_vt(v_ref, vt_ref, 0, d, seq, tk)

    for r in range(NSA_REP):
        qt_ref[r] = (q_ref[:, r * d:(r + 1) * d].astype(F32) * (scale * LOG2E)).T.astype(BF16)
    m_ref[...] = jnp.full(m_ref.shape, NEG_INF, F32)
    acc_ref[...] = jnp.zeros(acc_ref.shape, F32)
    sel = sel_ref[...]
    pos_q = posq_ref[...]
    key_blk = lax.broadcasted_iota(I32, (tk, LANES), 0) // NSA_SLC_BLOCK
    blk = lax.broadcasted_iota(I32, (tk, LANES), 1)
    per_tile = tk // NSA_SLC_BLOCK
    n_full = (qi * tq) // tk

    def scores(j, slot):
        start = pl.multiple_of(j * tk, tk)
        k = k_ref[pl.ds(start, tk), :]
        for r in range(NSA_REP):
            s_ref[slot, r] = jnp.dot(k, qt_ref[r], preferred_element_type=F32)

    def softmax_pv(j, slot, masked, near):
        start = pl.multiple_of(j * tk, tk)
        expand = jnp.where(blk == key_blk + j * per_tile, 1.0, 0.0).astype(BF16)
        picked = jnp.dot(expand, sel, preferred_element_type=F32)
        if masked:
            key = start + lax.broadcasted_iota(I32, (tk, tq), 0)
            qry = qi * tq + lax.broadcasted_iota(I32, (tk, tq), 1)
            picked = jnp.where(key <= qry, picked, 0.0)
        mask = picked > 0.5
        if near:
            pos_k = jnp.concatenate([posk_ref[pl.ds(start, tk), :]] * (tq // LANES), axis=1)
            bucket = jnp.where(mask, _t5_bucket(pos_q - pos_k), T5_MASK_BUCKET)
        for r in range(NSA_REP):
            h = g * NSA_REP + r
            t = s_ref[slot, r]
            if near:
                t = t + _t5_lookup(tab_ref[pl.ds(h, 1), :], bucket)
            else:
                t = jnp.where(mask, t, NEG_INF)
            m_prev = m_ref[r]
            mx = jnp.max(t, axis=0, keepdims=True)
            if near:
                m_new = jnp.maximum(m_prev, mx)
                shift = m_new
            else:
                b = tab_ref[pl.ds(h, 1), T5_BUCKETS - 1:T5_BUCKETS]
                m_new = jnp.maximum(m_prev, mx + b)
                shift = m_new - b
            alpha = jnp.exp2(m_prev - m_new)
            p = jnp.exp2(t - shift).astype(BF16)
            acc_ref[r] = alpha * acc_ref[r] + jnp.dot(vt_ref[0, :, pl.ds(start, tk)], p, preferred_element_type=F32)
            m_ref[r] = m_new

    def stage(j, slot):
        far = pqmin_ref[qi] - pkmax_ref[j] >= T5_MAX_DIST

        @pl.when(far)
        def _():
            scores(j + 1, 1 - slot)
            softmax_pv(j, slot, False, False)

        @pl.when(jnp.logical_not(far))
        def _():
            scores(j + 1, 1 - slot)
            softmax_pv(j, slot, False, True)

    scores(0, 0)

    def pair(jj, carry):
        stage(2 * jj, 0)
        stage(2 * jj + 1, 1)
        return carry

    pairs = n_full // 2
    lax.fori_loop(0, pairs, pair, 0)
    last = 2 * pairs

    @pl.when(n_full % 2 == 1)
    def _():
        stage(last, 0)
        softmax_pv(last + 1, 1, True, True)

    @pl.when(n_full % 2 == 0)
    def _():
        softmax_pv(last, 0, True, True)

    for r in range(NSA_REP):
        a = acc_ref[r]
        o_ref[:, r * d:(r + 1) * d] = (a[0:d, :] / a[d:d + 1, :]).T


def _nsa_slc(proj_b, q_off, k_off, v_off, sel, pos_row, pos_rep, tab_t, pq_min, pk_max, *, tq, tk):
    seq = proj_b.shape[0]
    tab_t = tab_t * LOG2E
    d = NSA_HEAD_DIM
    gw = NSA_REP * d
    once = pl.Buffered(1)
    grid_spec = pltpu.PrefetchScalarGridSpec(
        num_scalar_prefetch=2,
        grid=(NSA_KV_GROUPS, seq // tq),
        in_specs=[pl.BlockSpec((tq, gw), lambda g, i, *_: (i, q_off // gw + g)),
                  pl.BlockSpec((seq, d), lambda g, i, *_: (0, k_off // d + g), pipeline_mode=once),
                  pl.BlockSpec((seq, d), lambda g, i, *_: (0, v_off // d + g), pipeline_mode=once),
                  pl.BlockSpec((None, LANES, tq), lambda g, i, *_: (g, 0, i)),
                  pl.BlockSpec((1, tq), lambda g, i, *_: (0, i)),
                  pl.BlockSpec((seq, LANES), lambda g, i, *_: (0, 0), pipeline_mode=once),
                  pl.BlockSpec((SUBLANES, LANES), lambda g, i, *_: (0, 0))],
        out_specs=pl.BlockSpec((tq, gw), lambda g, i, *_: (i, g)),
        scratch_shapes=[pltpu.VMEM((1, d + ONES_ROWS, seq), BF16), pltpu.VMEM((NSA_REP, d, tq), BF16),
                        pltpu.VMEM((2, NSA_REP, tk, tq), F32), pltpu.VMEM((NSA_REP, 1, tq), F32),
                        pltpu.VMEM((NSA_REP, d + ONES_ROWS, tq), F32)])
    return pl.pallas_call(
        functools.partial(_nsa_slc_kernel, tq=tq, tk=tk),
        grid_spec=grid_spec,
        out_shape=jax.ShapeDtypeStruct((seq, NSA_WIDTH), F32),
        compiler_params=_cparams("parallel", "arbitrary"),
        name="nsa_slc",
    )(pq_min, pk_max, proj_b, proj_b, proj_b, sel, pos_row, pos_rep, tab_t)


def _nsa_win_kernel(*refs, tq, nt):
    q_ref = refs[0]
    k_refs = refs[1:1 + nt]
    v_refs = refs[1 + nt:1 + 2 * nt]
    pk_refs = refs[1 + 2 * nt:1 + 3 * nt]
    posq_ref, tab_ref, gate_ref, gout_ref, oc_ref, os_ref, o_ref = refs[1 + 3 * nt:]
    qi = pl.program_id(0)
    d = NSA_HEAD_DIM
    c = d ** -0.5 * LOG2E
    tok = qi * tq + lax.broadcasted_iota(I32, (1, tq), 1)
    lower = jnp.maximum(tok - (NSA_WINDOW - 1), 0)
    sub = lax.broadcasted_iota(I32, (tq, 1), 0)
    pos_q = posq_ref[...]
    buckets = []
    for jj in range(nt):
        kidx = (qi - (nt - 1) + jj) * tq + sub
        pos_k = jnp.concatenate([pk_refs[jj][...]] * (tq // LANES), axis=1)
        b = jnp.where(kidx >= lower, _t5_bucket(pos_q - pos_k), T5_MASK_BUCKET)
        buckets.append(jnp.where(kidx <= tok, b, T5_MASK_BUCKET))
    gates = jax.nn.sigmoid(gate_ref[...])
    for g in range(NSA_KV_GROUPS):
        ks = [kr[:, g * d:(g + 1) * d] for kr in k_refs]
        vts = [vr[:, g * d:(g + 1) * d].astype(F32).T.astype(BF16) for vr in v_refs]
        for r in range(NSA_REP):
            h = g * NSA_REP + r
            hs = slice(h * d, (h + 1) * d)
            qt = (q_ref[:, hs].astype(F32) * c).T.astype(BF16)
            ts = [jnp.dot(ks[jj], qt, preferred_element_type=F32) + _t5_lookup(tab_ref[h:h + 1, :], buckets[jj])
                  for jj in range(nt)]
            m = functools.reduce(jnp.maximum, [jnp.max(t, axis=0, keepdims=True) for t in ts])
            ps = [jnp.exp2(t - m) for t in ts]
            l = functools.reduce(jnp.add, [jnp.sum(p, axis=0, keepdims=True) for p in ps])
            o_t = functools.reduce(jnp.add, [jnp.dot(vt, p.astype(BF16), preferred_element_type=F32)
                                             for vt, p in zip(vts, ps)])
            o_w = (o_t / l).T
            o = oc_ref[:, hs] + gates[:, 3 * h + 1:3 * h + 2] * os_ref[:, hs] + gates[:, 3 * h + 2:3 * h + 3] * o_w
            o_ref[:, hs] = (o * _silu(gout_ref[:, hs])).astype(o_ref.dtype)


def _nsa_win(proj_b, q_cb, k_cb, v_cb, proj_f, gate_cb, gout_cb, oc, o_s, pos_row, pos_rep, tab_t, *, tq):
    seq = proj_b.shape[0]
    nt = NSA_WINDOW // tq + 1

    def band_rows(jj, cb):
        return pl.BlockSpec((tq, NSA_KV), lambda i: (jnp.maximum(i - (nt - 1) + jj, 0), cb))

    def band_pos(jj):
        return pl.BlockSpec((tq, LANES), lambda i: (jnp.maximum(i - (nt - 1) + jj, 0), 0))

    in_specs = [pl.BlockSpec((tq, NSA_WIDTH), lambda i: (i, q_cb))]
    in_specs += [band_rows(jj, k_cb) for jj in range(nt)]
    in_specs += [band_rows(jj, v_cb) for jj in range(nt)]
    in_specs += [band_pos(jj) for jj in range(nt)]
    in_specs += [pl.BlockSpec((1, tq), lambda i: (0, i)),
                 pl.BlockSpec((SUBLANES, LANES), lambda i: (0, 0)),
                 pl.BlockSpec((tq, LANES), lambda i: (i, gate_cb)),
                 pl.BlockSpec((tq, NSA_WIDTH), lambda i: (i, gout_cb)),
                 pl.BlockSpec((tq, NSA_WIDTH), lambda i: (i, 0)),
                 pl.BlockSpec((tq, NSA_WIDTH), lambda i: (i, 0))]
    args = [proj_b] * (1 + 2 * nt) + [pos_rep] * nt + [pos_row, tab_t * LOG2E, proj_f, proj_f, oc, o_s]
    return pl.pallas_call(
        functools.partial(_nsa_win_kernel, tq=tq, nt=nt),
        grid=(seq // tq,),
        in_specs=in_specs,
        out_specs=pl.BlockSpec((tq, NSA_WIDTH), lambda i: (i, 0)),
        out_shape=jax.ShapeDtypeStruct((seq, NSA_WIDTH), BF16),
        compiler_params=_cparams("parallel"),
        name="nsa_win",
    )(*args)


def _in_proj(h, norm_g, w_b, w_f, tiles, name):
    proj_b = _norm_matmul(h, norm_g, w_b, tm=tiles["tm"], tn=512, name=name + "_b", out_dtype=BF16)
    proj_f = _norm_matmul(h, norm_g, w_f, tm=tiles["tm"], tn=768, name=name + "_f")
    return proj_b, proj_f


def _even_layer(h, mem_kv, norm_g, w_in, s5, w_glu, b_f, tiles):
    pb, pf = _in_proj(h, norm_g, *w_in, tiles, "in_proj_even")
    ob, of = EVEN_B_OFF, EVEN_F_OFF
    b_cat, c_cat, tab, d_skip = s5
    z = _s5_scan(pf, of["u"] // LANES, b_cat, c_cat, d_skip, tab, tc=tiles["s5_tc"])
    y_s5 = _s5_glu(z, w_glu, pf, of["g_s5"] // S5_WIDTH, tm=tiles["tm"], tn=512)
    decay = _decay(pf, of["f"] // LANES, b_f, t=tiles["decay_t"])
    d = FOX_HEAD_DIM
    y_fox = _flash(pb, ob["q"], pb, ob["k"], pb, ob["v"], pf, of["g_fox"], heads=FOX_HEADS, dk=d, dv=d,
                   scale=d ** -0.5, tq=tiles["attn_tq"], tk=tiles["attn_tk"], hp=tiles["attn_hp"], decay=decay,
                   name="fox_attn")
    y_mem = _mem_attn(pb, ob["q_mem"] // MEM_HEAD_DIM, pf, of["g_mem"] // MEM_HEAD_DIM, mem_kv, t=tiles["mem_t"])
    return y_s5, y_fox, y_mem


def _odd_layer(h, mem_kv, norm_g, w_in, mla, nsa, pos, tiles):
    pb, pf = _in_proj(h, norm_g, *w_in, tiles, "in_proj_odd")
    ob, of = ODD_B_OFF, ODD_F_OFF
    g_cq, g_ckv, w_uq, w_ukv, freq = mla
    pos_col_f, pos_row, pos_cmp_rep, pos_rep, pq_min, pk_max = pos
    qf = _norm_matmul(pf, g_cq, w_uq, x_cb=of["c_q"] // MLA_Q_RANK, tm=tiles["tm"], tn=512, name="mla_q_up")
    kvf = _norm_matmul(pf, g_ckv, w_ukv, x_cb=of["c_kv"] // MLA_KV_RANK, tm=tiles["tm"], tn=512, name="mla_kv_up")
    q_r, k_r, v_r = _mla_prep(qf, kvf, pf, of["k_rope"] // LANES, pos_col_f, freq, t=tiles["prep_t"])
    y_mla = _flash(q_r, 0, k_r, 0, v_r, 0, pf, of["g_mla"], heads=MLA_HEADS, dk=2 * LANES, dv=MLA_V,
                   scale=(MLA_NOPE + MLA_ROPE) ** -0.5, tq=tiles["attn_tq"], tk=tiles["attn_tk"], hp=tiles["attn_hp"],
                   name="mla_attn")
    pe, w1, w2, tab_t, ov, n_slc = nsa
    kvc = _nsa_compress(pb, ob["k_cmp"] // NSA_HEAD_DIM, pe, w1, w2)
    oc, sel = _nsa_select(pb, ob["q_nsa"] // NSA_WIDTH, pf, of["gates"] // LANES, kvc[0], kvc[1], pos_row,
                          pos_cmp_rep, tab_t, ov, n_slc=n_slc, tq=tiles["nsa_tq"])
    o_s = _nsa_slc(pb, ob["q_nsa"], ob["k_slc"], ob["v_slc"], sel, pos_row, pos_rep, tab_t, pq_min, pk_max,
                   tq=tiles["slc_tq"], tk=tiles["slc_tk"])
    y_nsa = _nsa_win(pb, ob["q_nsa"] // NSA_WIDTH, ob["k_win"] // NSA_KV, ob["v_win"] // NSA_KV,
                     pf, of["gates"] // LANES, of["g_nsa"] // NSA_WIDTH, oc, o_s, pos_row, pos_rep, tab_t,
                     tq=tiles["win_tq"])
    y_mem = _mem_attn(pb, ob["q_mem"] // MEM_HEAD_DIM, pf, of["g_mem"] // MEM_HEAD_DIM, mem_kv, t=tiles["mem_t"])
    return y_mla, y_nsa, y_mem


def _tiles(seq):
    return {"tm": min(seq, 1024), "s5_tc": min(seq, 1024), "decay_t": min(seq, 512), "attn_tq": min(seq, 256),
            "attn_tk": min(seq, 512), "attn_hp": 2, "slc_tq": 256, "slc_tk": 512, "nsa_tq": 256, "win_tq": 128,
            "mem_t": min(seq, 512), "prep_t": min(seq, 256)}


def _context(positions, t5_table, seq, tiles):
    pos = positions[0]
    pos_col = pos.reshape(seq, 1)
    pos_row = pos.reshape(1, seq)
    pos_rep = jnp.broadcast_to(pos_col, (seq, LANES))
    pq_min = jnp.min(pos.reshape(seq // tiles["slc_tq"], tiles["slc_tq"]), axis=1)
    pk_max = jnp.max(pos.reshape(seq // tiles["slc_tk"], tiles["slc_tk"]), axis=1)
    nc = seq // NSA_CMP_STRIDE
    pos_cmp = jnp.pad(pos[NSA_CMP_LEN - 1::NSA_CMP_STRIDE], (0, 1))
    pos_cmp_rep = jnp.broadcast_to(pos_cmp.reshape(nc, 1), (nc, LANES))
    half = MLA_ROPE // 2
    inv_freq = ROPE_THETA ** (-jnp.arange(half, dtype=F32) / half)
    freq = jnp.concatenate([inv_freq, inv_freq, jnp.zeros((LANES - MLA_ROPE,), F32)]).reshape(1, LANES)
    tab_t = jnp.pad(t5_table.astype(F32).T, ((0, SUBLANES - NSA_HEADS), (0, LANES - T5_BUCKETS)))
    tab_t = tab_t.at[:, T5_MASK_BUCKET].set(NEG_INF)
    n_slc = seq // NSA_SLC_BLOCK
    cs = np.arange(nc) * NSA_CMP_STRIDE
    ss = np.arange(LANES) * NSA_SLC_BLOCK
    ov_np = np.clip(np.minimum(cs[:, None] + NSA_CMP_LEN, ss[None, :] + NSA_SLC_BLOCK)
                    - np.maximum(cs[:, None], ss[None, :]), 0, None) / NSA_CMP_LEN
    ov_np[nc - 1, :] = 0.0
    ov_np[:, n_slc:] = 0.0
    return {"pos": (pos_col.astype(F32), pos_row, pos_cmp_rep, pos_rep, pq_min, pk_max), "freq": freq,
            "tab_t": tab_t, "ov": jnp.asarray(ov_np.T, BF16), "n_slc": n_slc}


def _odd_params(i, mla_g_cq, mla_g_ckv, mla_w_uq, mla_w_ukv, nsa_cmp_pe, nsa_cmp_w1, nsa_cmp_w2, ctx):
    dq = MLA_NOPE + MLA_ROPE
    w_uq = mla_w_uq[i].reshape(MLA_Q_RANK, MLA_HEADS, dq)
    w_uq = jnp.pad(w_uq, ((0, 0), (0, 0), (0, 2 * LANES - dq))).reshape(MLA_Q_RANK, -1).astype(BF16)
    mla = (mla_g_cq[i], mla_g_ckv[i], w_uq, mla_w_ukv[i].astype(BF16), ctx["freq"])
    nsa = (nsa_cmp_pe[i].astype(F32), nsa_cmp_w1[i].astype(BF16), nsa_cmp_w2[i].astype(BF16), ctx["tab_t"],
           ctx["ov"], ctx["n_slc"])
    return mla, nsa


def kernel(x, mem, positions, norm_g, mem_norm_g, final_norm_g, t5_table, w_out, mem_w_kv, even_w_in, s5_lam_re,
           s5_lam_im, s5_log_dt, s5_b_re, s5_b_im, s5_c_re, s5_c_im, s5_d, s5_w_glu, fox_b_f, odd_w_in, mla_g_cq,
           mla_g_ckv, mla_w_uq, mla_w_ukv, nsa_cmp_pe, nsa_cmp_w1, nsa_cmp_w2):
    batch, seq, _ = x.shape
    assert batch == 1 and seq % 1024 == 0 and seq // NSA_SLC_BLOCK <= LANES
    depth = norm_g.shape[0]
    tiles = _tiles(seq)
    ctx = _context(positions, t5_table, seq, tiles)
    h = x[0]
    w_out_b = w_out.astype(BF16)
    mem_kv_all = _mem_kv_all(mem[0], mem_norm_g, mem_w_kv.astype(BF16))
    for layer in range(depth):
        i = layer // 2
        mem_kv = (mem_kv_all, layer)
        if layer % 2 == 0:
            w_in = (_reorder_w_in(even_w_in[i], EVEN_SPLITS, EVEN_B_ORDER),
                    _reorder_w_in(even_w_in[i], EVEN_SPLITS, EVEN_F_ORDER))
            b_cat, c_cat, tab = _s5_prepare(s5_lam_re[i], s5_lam_im[i], s5_log_dt[i], s5_b_re[i], s5_b_im[i],
                                            s5_c_re[i], s5_c_im[i])
            ys = _even_layer(h, mem_kv, norm_g[layer], w_in, (b_cat, c_cat, tab, s5_d[i]),
                             s5_w_glu[i].astype(BF16), fox_b_f[i], tiles)
        else:
            w_in = (_reorder_w_in(odd_w_in[i], ODD_SPLITS, ODD_B_ORDER),
                    _reorder_w_in(odd_w_in[i], ODD_SPLITS, ODD_F_ORDER))
            mla, nsa = _odd_params(i, mla_g_cq, mla_g_ckv, mla_w_uq, mla_w_ukv, nsa_cmp_pe, nsa_cmp_w1, nsa_cmp_w2,
                                   ctx)
            ys = _odd_layer(h, mem_kv, norm_g[layer], w_in, mla, nsa, ctx["pos"], tiles)
        h = _out_proj(h, ys, w_out_b, layer, tm=tiles["tm"], tn=512)
    return _final_norm(h, final_norm_g, tm=tiles["mem_t"])[None]
```

```python
import functools
import math

import numpy as np
import jax
import jax.numpy as jnp
from jax import lax
from jax.experimental import pallas as pl
from jax.experimental.pallas import tpu as pltpu

F32 = jnp.float32
BF16 = jnp.bfloat16
I32 = jnp.int32

D_MODEL = 2048
DEPTH = 4
N_MEM = 256
RMS_EPS = 1e-6
NEG_INF = -1e30
LOG2E = math.log2(math.e)

S5_WIDTH = 1024
S5_GROUP = 16
S5_GROUPS = S5_WIDTH // S5_GROUP
S5_STATE = 64
FOX_HEADS = 8
FOX_HEAD_DIM = 128
FOX_WIDTH = FOX_HEADS * FOX_HEAD_DIM
MEM_HEADS = 4
MEM_HEAD_DIM = 128
MEM_WIDTH = MEM_HEADS * MEM_HEAD_DIM
MLA_HEADS = 8
MLA_Q_RANK = 512
MLA_KV_RANK = 512
MLA_NOPE = 128
MLA_ROPE = 64
MLA_V = 128
MLA_WIDTH = MLA_HEADS * MLA_V
ROPE_THETA = 10000.0
NSA_HEADS = 8
NSA_KV_GROUPS = 2
NSA_REP = NSA_HEADS // NSA_KV_GROUPS
NSA_HEAD_DIM = 128
NSA_WIDTH = NSA_HEADS * NSA_HEAD_DIM
NSA_KV = NSA_KV_GROUPS * NSA_HEAD_DIM
NSA_CMP_LEN = 32
NSA_CMP_STRIDE = 16
NSA_CMP_HIDDEN = 256
NSA_SLC_BLOCK = 64
NSA_SLC_TOPK = 16
NSA_WINDOW = 512
FORCE_SCORE = 1e6
T5_BUCKETS = 32
T5_MAX_DIST = 1024

EVEN_SPLITS = (S5_WIDTH, S5_WIDTH, FOX_WIDTH, FOX_WIDTH, FOX_WIDTH, FOX_HEADS, FOX_WIDTH, MEM_WIDTH, MEM_WIDTH)
ODD_SPLITS = (MLA_Q_RANK, MLA_KV_RANK, MLA_ROPE, MLA_WIDTH, NSA_WIDTH, NSA_KV, NSA_KV, NSA_KV, NSA_KV, NSA_KV,
              NSA_KV, 3 * NSA_HEADS, NSA_WIDTH, MEM_WIDTH, MEM_WIDTH)

LANES = 128
SUBLANES = 8
VMEM_LIMIT_BYTES = 56 * 1024 * 1024

EVEN_B_ORDER = (("q", 2, 1024), ("k", 3, 1024), ("v", 4, 1024), ("q_mem", 7, 512))
EVEN_F_ORDER = (("u", 0, 1024), ("g_s5", 1, 1024), ("g_fox", 6, 1024), ("g_mem", 8, 512), ("f", 5, 128),
                ("pad", None, 128))
ODD_B_ORDER = (("q_nsa", 4, 1024), ("q_mem", 13, 512), ("k_cmp", 5, 256), ("v_cmp", 6, 256), ("k_slc", 7, 256),
               ("v_slc", 8, 256), ("k_win", 9, 256), ("v_win", 10, 256))
ODD_F_ORDER = (("g_nsa", 12, 1024), ("g_mla", 3, 1024), ("c_q", 0, 512), ("c_kv", 1, 512), ("g_mem", 14, 512),
               ("k_rope", 2, 128), ("gates", 11, 128))


def _layout(order):
    off, out = 0, {}
    for name, _, width in order:
        assert off % width == 0
        out[name] = off
        off += width
    return out, off


EVEN_B_OFF, EVEN_B_N = _layout(EVEN_B_ORDER)
EVEN_F_OFF, EVEN_F_N = _layout(EVEN_F_ORDER)
ODD_B_OFF, ODD_B_N = _layout(ODD_B_ORDER)
ODD_F_OFF, ODD_F_N = _layout(ODD_F_ORDER)


def _reorder_w_in(w, splits, order):
    starts = np.concatenate([[0], np.cumsum(splits)])
    cols = []
    for _, idx, width in order:
        if idx is None:
            cols.append(jnp.zeros((w.shape[0], width), w.dtype))
            continue
        seg = w[:, int(starts[idx]):int(starts[idx + 1])]
        pad = width - seg.shape[1]
        if pad:
            seg = jnp.pad(seg, ((0, 0), (0, pad)))
        cols.append(seg)
    return jnp.concatenate(cols, axis=1).astype(BF16)


def _cparams(*sem):
    return pltpu.CompilerParams(dimension_semantics=sem, vmem_limit_bytes=VMEM_LIMIT_BYTES)


def _silu(g):
    return g * jax.nn.sigmoid(g)


def _pick(n, cands):
    for c in cands:
        if n % c == 0:
            return c
    raise ValueError(f"no tile for {n} in {cands}")


def _norm_matmul_kernel(x_ref, g_ref, w_ref, o_ref, xn_ref):
    @pl.when(pl.program_id(1) == 0)
    def _():
        x = x_ref[...]
        ms = jnp.mean(x * x, axis=-1, keepdims=True)
        xn_ref[...] = (x * lax.rsqrt(ms + RMS_EPS) * g_ref[...]).astype(BF16)

    o_ref[...] = jnp.dot(xn_ref[...], w_ref[...], preferred_element_type=F32).astype(o_ref.dtype)


def _norm_matmul(x, g, w, *, x_cb=0, tm, tn, name, out_dtype=F32):
    m = x.shape[0]
    k, n = w.shape
    return pl.pallas_call(
        _norm_matmul_kernel,
        grid=(m // tm, n // tn),
        in_specs=[pl.BlockSpec((tm, k), lambda i, j: (i, x_cb)),
                  pl.BlockSpec((1, k), lambda i, j: (0, 0)),
                  pl.BlockSpec((k, tn), lambda i, j: (0, j))],
        out_specs=pl.BlockSpec((tm, tn), lambda i, j: (i, j)),
        out_shape=jax.ShapeDtypeStruct((m, n), out_dtype),
        scratch_shapes=[pltpu.VMEM((tm, k), BF16)],
        compiler_params=_cparams("parallel", "arbitrary"),
        name=name,
    )(x, g.reshape(1, k), w)


def _out_proj_kernel(h_ref, *refs):
    o_ref = refs[-1]
    n = (len(refs) - 1) // 2
    acc = h_ref[...]
    for y_ref, w_ref in zip(refs[:n], refs[n:2 * n]):
        acc = acc + jnp.dot(y_ref[...], w_ref[...], preferred_element_type=F32)
    o_ref[...] = acc


def _out_proj(h, ys, w_all, layer, *, tm, tn):
    m, n = h.shape
    in_specs = [pl.BlockSpec((tm, tn), lambda i, j: (i, j))]
    in_specs += [pl.BlockSpec((tm, y.shape[1]), lambda i, j: (i, 0)) for y in ys]
    row = 0
    for y in ys:
        width = y.shape[1]
        assert row % width == 0
        in_specs.append(pl.BlockSpec((None, width, tn), lambda i, j, rb=row // width: (layer, rb, j)))
        row += width
    return pl.pallas_call(
        _out_proj_kernel,
        grid=(m // tm, n // tn),
        in_specs=in_specs,
        out_specs=pl.BlockSpec((tm, tn), lambda i, j: (i, j)),
        out_shape=jax.ShapeDtypeStruct((m, n), F32),
        compiler_params=_cparams("parallel", "arbitrary"),
        name="out_proj",
    )(h, *ys, *([w_all] * len(ys)))


def _final_norm_kernel(x_ref, g_ref, o_ref):
    x = x_ref[...]
    ms = jnp.mean(x * x, axis=-1, keepdims=True)
    o_ref[...] = x * lax.rsqrt(ms + RMS_EPS) * g_ref[...]


def _final_norm(h, g, *, tm):
    m, n = h.shape
    return pl.pallas_call(
        _final_norm_kernel,
        grid=(m // tm,),
        in_specs=[pl.BlockSpec((tm, n), lambda i: (i, 0)), pl.BlockSpec((1, n), lambda i: (0, 0))],
        out_specs=pl.BlockSpec((tm, n), lambda i: (i, 0)),
        out_shape=jax.ShapeDtypeStruct((m, n), F32),
        compiler_params=_cparams("parallel"),
        name="final_norm",
    )(h, g.reshape(1, n))


ONES_ROWS = 16


def _build_vt(v_ref, vt_ref, hh, dv, seq, chunk):
    def body(c, carry):
        st = pl.multiple_of(c * chunk, chunk)
        vt_ref[hh, 0:dv, pl.ds(st, chunk)] = v_ref[pl.ds(st, chunk), hh * dv:(hh + 1) * dv].astype(F32).T.astype(BF16)
        return carry

    lax.fori_loop(0, seq // chunk, body, 0)
    vt_ref[hh, dv:dv + ONES_ROWS, :] = jnp.ones((ONES_ROWS, seq), BF16)


def _flash_kernel(*refs, scale, tq, tk, hp, dk, dv, has_decay):
    if has_decay:
        q_ref, k_ref, v_ref, g_ref, cq_ref, ck_ref, o_ref, vt_ref, qt_ref, s_ref, m_ref, acc_ref = refs
    else:
        q_ref, k_ref, v_ref, g_ref, o_ref, vt_ref, qt_ref, s_ref, m_ref, acc_ref = refs
    hb = pl.program_id(0)
    qi = pl.program_id(1)
    seq = k_ref.shape[0]

    @pl.when(qi == 0)
    def _():
        for hh in range(hp):
            _build_vt(v_ref, vt_ref, hh, dv, seq, tk)

    for hh in range(hp):
        qt_ref[hh] = (q_ref[:, hh * dk:(hh + 1) * dk].astype(F32) * (scale * LOG2E)).T.astype(BF16)
    m_ref[...] = jnp.full(m_ref.shape, NEG_INF, F32)
    acc_ref[...] = jnp.zeros(acc_ref.shape, F32)
    n_full = (qi * tq) // tk
    if has_decay:
        cq2 = [cq_ref[pl.ds(hb * hp + hh, 1), :] for hh in range(hp)]

    def scores(j, slot):
        start = pl.multiple_of(j * tk, tk)
        for hh in range(hp):
            s_ref[slot, hh] = jnp.dot(k_ref[pl.ds(start, tk), hh * dk:(hh + 1) * dk], qt_ref[hh],
                                      preferred_element_type=F32)

    def softmax_pv(j, slot, masked):
        start = pl.multiple_of(j * tk, tk)
        for hh in range(hp):
            t = s_ref[slot, hh]
            if has_decay:
                t = t - jnp.concatenate([ck_ref[hh, pl.ds(start, tk), :]] * (tq // LANES), axis=1)
            if masked:
                key = start + lax.broadcasted_iota(I32, (tk, tq), 0)
                qry = qi * tq + lax.broadcasted_iota(I32, (tk, tq), 1)
                t = jnp.where(key <= qry, t, NEG_INF)
            m_prev = m_ref[hh]
            mx = jnp.max(t, axis=0, keepdims=True)
            if has_decay:
                m_new = jnp.maximum(m_prev, mx + cq2[hh])
                shift = m_new - cq2[hh]
            else:
                m_new = jnp.maximum(m_prev, mx)
                shift = m_new
            alpha = jnp.exp2(m_prev - m_new)
            p = jnp.exp2(t - shift).astype(BF16)
            acc_ref[hh] = alpha * acc_ref[hh] + jnp.dot(vt_ref[hh, :, pl.ds(start, tk)], p,
                                                         preferred_element_type=F32)
            m_ref[hh] = m_new

    scores(0, 0)

    def pair(j):
        scores(j + 1, 1)
        softmax_pv(j, 0, False)
        scores(j + 2, 0)
        softmax_pv(j + 1, 1, False)

    def quad(qq, carry):
        pair(4 * qq)
        pair(4 * qq + 2)
        return carry

    quads = n_full // 4
    lax.fori_loop(0, quads, quad, 0)
    rem = n_full - 4 * quads

    @pl.when(rem >= 2)
    def _():
        pair(4 * quads)

    last = 4 * quads + 2 * (rem // 2)

    @pl.when(n_full % 2 == 1)
    def _():
        scores(last + 1, 1)
        softmax_pv(last, 0, False)
        softmax_pv(last + 1, 1, True)

    @pl.when(n_full % 2 == 0)
    def _():
        softmax_pv(last, 0, True)

    for hh in range(hp):
        a = acc_ref[hh]
        o = (a[0:dv, :] / a[dv:dv + 1, :]).T
        o_ref[:, hh * dv:(hh + 1) * dv] = (o * _silu(g_ref[:, hh * dv:(hh + 1) * dv])).astype(o_ref.dtype)


def _flash(q_arr, q_off, k_arr, k_off, v_arr, v_off, g_arr, g_off, *, heads, dk, dv, scale, tq, tk, hp,
           decay=None, name):
    seq = q_arr.shape[0]
    once = pl.Buffered(1)
    in_specs = [pl.BlockSpec((tq, hp * dk), lambda h, i: (i, q_off // (hp * dk) + h)),
                pl.BlockSpec((seq, hp * dk), lambda h, i: (0, k_off // (hp * dk) + h), pipeline_mode=once),
                pl.BlockSpec((seq, hp * dv), lambda h, i: (0, v_off // (hp * dv) + h), pipeline_mode=once),
                pl.BlockSpec((tq, hp * dv), lambda h, i: (i, g_off // (hp * dv) + h))]
    args = [q_arr, k_arr, v_arr, g_arr]
    if decay is not None:
        cum_t, cum_rep = decay
        in_specs += [pl.BlockSpec((SUBLANES, tq), lambda h, i: (0, i)),
                     pl.BlockSpec((hp, seq, LANES), lambda h, i: (h, 0, 0), pipeline_mode=once)]
        args += [cum_t, cum_rep]
    return pl.pallas_call(
        functools.partial(_flash_kernel, scale=scale, tq=tq, tk=tk, hp=hp, dk=dk, dv=dv, has_decay=decay is not None),
        grid=(heads // hp, seq // tq),
        in_specs=in_specs,
        out_specs=pl.BlockSpec((tq, hp * dv), lambda h, i: (i, h)),
        out_shape=jax.ShapeDtypeStruct((seq, heads * dv), BF16),
        scratch_shapes=[pltpu.VMEM((hp, dv + ONES_ROWS, seq), BF16), pltpu.VMEM((hp, dk, tq), BF16),
                        pltpu.VMEM((2, hp, tk, tq), F32), pltpu.VMEM((hp, 1, tq), F32),
                        pltpu.VMEM((hp, dv + ONES_ROWS, tq), F32)],
        compiler_params=_cparams("parallel", "arbitrary"),
        name=name,
    )(*args)


def _decay_kernel(f_ref, b_ref, ct_ref, cr_ref, carry_ref, *, t):
    i = pl.program_id(0)

    @pl.when(i == 0)
    def _():
        carry_ref[...] = jnp.zeros(carry_ref.shape, F32)

    x = f_ref[...] + b_ref[...]
    lf = jnp.minimum(x, 0.0) - jnp.log1p(jnp.exp(-jnp.abs(x)))
    row = lax.broadcasted_iota(I32, lf.shape, 0)
    s = 1
    while s < t:
        lf = lf + jnp.where(row >= s, pltpu.roll(lf, s, 0), 0.0)
        s *= 2
    lf = lf + carry_ref[...]
    carry_ref[...] = lf[t - 1:t, :]
    lf2 = lf * LOG2E
    ct_ref[...] = lf2.T[:FOX_HEADS, :]
    for h in range(FOX_HEADS):
        cr_ref[h] = jnp.broadcast_to(lf2[:, h:h + 1], (t, LANES))


def _decay(proj, f_cb, b_f, *, t):
    seq = proj.shape[0]
    b = jnp.pad(b_f.reshape(1, FOX_HEADS), ((0, 0), (0, LANES - FOX_HEADS)))
    return pl.pallas_call(
        functools.partial(_decay_kernel, t=t),
        grid=(seq // t,),
        in_specs=[pl.BlockSpec((t, LANES), lambda i: (i, f_cb)), pl.BlockSpec((1, LANES), lambda i: (0, 0))],
        out_specs=[pl.BlockSpec((FOX_HEADS, t), lambda i: (0, i)),
                   pl.BlockSpec((FOX_HEADS, t, LANES), lambda i: (0, i, 0))],
        out_shape=[jax.ShapeDtypeStruct((FOX_HEADS, seq), F32), jax.ShapeDtypeStruct((FOX_HEADS, seq, LANES), F32)],
        scratch_shapes=[pltpu.VMEM((1, LANES), F32)],
        compiler_params=_cparams("arbitrary"),
        name="fox_decay",
    )(proj, b)


def _mem_attn_kernel(q_ref, k_ref, v_ref, g_ref, o_ref):
    k = k_ref[...].astype(BF16)
    s = lax.dot_general(q_ref[...], k, (((1,), (1,)), ((), ())), preferred_element_type=F32) * (MEM_HEAD_DIM ** -0.5)
    m = jnp.max(s, axis=1, keepdims=True)
    p = jnp.exp(s - m)
    l = jnp.sum(p, axis=1, keepdims=True)
    o = jnp.dot(p.astype(BF16), v_ref[...].astype(BF16), preferred_element_type=F32) / l
    o_ref[...] = (o * _silu(g_ref[...])).astype(o_ref.dtype)


def _mem_kv_all(mem2d, g, w_all):
    depth, k, n = w_all.shape
    m = mem2d.shape[0]
    tn = 512
    return pl.pallas_call(
        _norm_matmul_kernel,
        grid=(depth, n // tn),
        in_specs=[pl.BlockSpec((m, k), lambda l, j: (0, 0)),
                  pl.BlockSpec((1, k), lambda l, j: (0, 0)),
                  pl.BlockSpec((None, k, tn), lambda l, j: (l, 0, j))],
        out_specs=pl.BlockSpec((None, m, tn), lambda l, j: (l, 0, j)),
        out_shape=jax.ShapeDtypeStruct((depth, m, n), F32),
        scratch_shapes=[pltpu.VMEM((m, k), BF16)],
        compiler_params=_cparams("arbitrary", "arbitrary"),
        name="mem_kv",
    )(mem2d, g.reshape(1, k), w_all)


def _mem_attn(proj_b, q_cb, proj_f, g_cb, mem_kv, *, t):
    seq = proj_b.shape[0]
    mem_kv, layer = mem_kv
    nm = mem_kv.shape[1]
    d = MEM_HEAD_DIM
    return pl.pallas_call(
        _mem_attn_kernel,
        grid=(MEM_HEADS, seq // t),
        in_specs=[pl.BlockSpec((t, d), lambda h, i: (i, q_cb + h)),
                  pl.BlockSpec((None, nm, d), lambda h, i: (layer, 0, h)),
                  pl.BlockSpec((None, nm, d), lambda h, i: (layer, 0, MEM_HEADS + h)),
                  pl.BlockSpec((t, d), lambda h, i: (i, g_cb + h))],
        out_specs=pl.BlockSpec((t, d), lambda h, i: (i, h)),
        out_shape=jax.ShapeDtypeStruct((seq, MEM_WIDTH), BF16),
        compiler_params=_cparams("parallel", "arbitrary"),
        name="mem_attn",
    )(proj_b, mem_kv, mem_kv, proj_f)


S5_TILE_GROUPS = LANES // S5_GROUP
S5_TILE_STATES = S5_TILE_GROUPS * S5_STATE
S5_TILES = S5_GROUPS // S5_TILE_GROUPS


def _s5_scan_kernel(u_ref, b_ref, c_ref, d_ref, a_ref, an_ref, p_ref, z_ref, up_ref, bu_ref, zp_ref, carry_ref, *, tc):
    ns = S5_TILE_STATES
    sub = tc // SUBLANES

    @pl.when(pl.program_id(1) == 0)
    def _():
        carry_ref[...] = jnp.zeros(carry_ref.shape, F32)

    def rows(i):
        return pl.ds(pl.multiple_of(i * SUBLANES, SUBLANES), SUBLANES)

    def regroup(i, carry):
        up_ref[rows(i), :] = u_ref[pl.ds(i, SUBLANES, stride=sub), :]
        return carry

    lax.fori_loop(0, sub, regroup, 0)
    up = up_ref[...]
    bu_ref[...] = jnp.dot(up.astype(BF16), b_ref[...], preferred_element_type=F32)
    ar = a_ref[0]
    ai = a_ref[1]

    def local_scan(i, carry):
        xr, xi = carry
        nr = ar * xr - ai * xi + bu_ref[rows(i), 0:ns]
        ni = ar * xi + ai * xr + bu_ref[rows(i), ns:2 * ns]
        bu_ref[rows(i), 0:ns] = nr
        bu_ref[rows(i), ns:2 * ns] = ni
        return nr, ni

    zero = jnp.zeros((SUBLANES, ns), F32)
    fr, fi = lax.fori_loop(0, sub, local_scan, (zero, zero), unroll=2)
    anr = an_ref[0]
    ani = an_ref[1]
    cr = carry_ref[0:1, 0:ns]
    ci = carry_ref[0:1, ns:2 * ns]
    in_r, in_i = [], []
    for s in range(SUBLANES):
        in_r.append(cr)
        in_i.append(ci)
        cr, ci = fr[s:s + 1] + anr * cr - ani * ci, fi[s:s + 1] + anr * ci + ani * cr
    carry_ref[0:1, 0:ns] = cr
    carry_ref[0:1, ns:2 * ns] = ci
    cin_r = jnp.concatenate(in_r, axis=0)
    cin_i = jnp.concatenate(in_i, axis=0)

    def add_incoming(i, carry):
        pr = p_ref[pl.ds(i, 1), 0:ns]
        pi = p_ref[pl.ds(i, 1), ns:2 * ns]
        bu_ref[rows(i), 0:ns] = bu_ref[rows(i), 0:ns] + (pr * cin_r - pi * cin_i)
        bu_ref[rows(i), ns:2 * ns] = bu_ref[rows(i), ns:2 * ns] + (pr * cin_i + pi * cin_r)
        return carry

    lax.fori_loop(0, sub, add_incoming, 0, unroll=2)
    y = jnp.dot(bu_ref[...].astype(BF16), c_ref[...], preferred_element_type=F32) + d_ref[...] * up
    zp_ref[...] = jax.nn.gelu(y)

    def ungroup(i, carry):
        z_ref[pl.ds(i, SUBLANES, stride=sub), :] = zp_ref[rows(i), :]
        return carry

    lax.fori_loop(0, sub, ungroup, 0)


def _s5_prepare(lam_re, lam_im, log_dt, b_re, b_im, c_re, c_im, sub):
    dt = jnp.exp(log_dt.astype(F32))[:, None]
    lr = lam_re.astype(F32)
    li = lam_im.astype(F32)
    mag = jnp.exp(lr * dt)
    ab_re = mag * jnp.cos(li * dt)
    ab_im = mag * jnp.sin(li * dt)
    den = lr * lr + li * li
    nr = ab_re - 1.0
    f_re = (nr * lr + ab_im * li) / den
    f_im = (ab_im * lr - nr * li) / den
    br = b_re.astype(F32)
    bim = b_im.astype(F32)
    bb_re = f_re[..., None] * br - f_im[..., None] * bim
    bb_im = f_re[..., None] * bim + f_im[..., None] * br
    eye = jnp.eye(S5_TILE_GROUPS, dtype=F32)

    def blockdiag_in(bb):
        t = bb.reshape(S5_TILES, S5_TILE_GROUPS, S5_STATE, S5_GROUP)
        m = jnp.einsum("jgpc,gh->jgchp", t, eye)
        return m.reshape(S5_TILES, LANES, S5_TILE_STATES)

    def blockdiag_out(cc):
        t = cc.reshape(S5_TILES, S5_TILE_GROUPS, S5_GROUP, S5_STATE)
        m = jnp.einsum("jgcp,gh->jgphc", t, eye)
        return m.reshape(S5_TILES, S5_TILE_STATES, LANES)

    b_cat = jnp.concatenate([blockdiag_in(bb_re), blockdiag_in(bb_im)], axis=2).astype(BF16)
    c_cat = jnp.concatenate([blockdiag_out(c_re.astype(F32)), -blockdiag_out(c_im.astype(F32))], axis=1).astype(BF16)

    def a_pow(n):
        n = jnp.asarray(n, F32).reshape(1, -1, 1)
        ldt = (lr * dt).reshape(S5_TILES, 1, S5_TILE_STATES)
        idt = (li * dt).reshape(S5_TILES, 1, S5_TILE_STATES)
        m = jnp.exp(n * ldt)
        return m * jnp.cos(n * idt), m * jnp.sin(n * idt)

    a_tab = jnp.stack([jnp.broadcast_to(ab_re.reshape(S5_TILES, 1, S5_TILE_STATES),
                                        (S5_TILES, SUBLANES, S5_TILE_STATES)),
                       jnp.broadcast_to(ab_im.reshape(S5_TILES, 1, S5_TILE_STATES),
                                        (S5_TILES, SUBLANES, S5_TILE_STATES))], axis=1)
    an_tab = jnp.stack(a_pow([sub]), axis=1)
    p_tab = jnp.concatenate(a_pow(np.arange(1, sub + 1)), axis=2)
    return b_cat, c_cat, (a_tab.astype(F32), an_tab.astype(F32), p_tab.astype(F32))


def _s5_scan(proj, u_cb, b_cat, c_cat, d_skip, tab, *, tc):
    seq = proj.shape[0]
    ns = S5_TILE_STATES
    sub = tc // SUBLANES
    d = d_skip.astype(F32).reshape(S5_TILES, 1, LANES)
    a_tab, an_tab, p_tab = tab
    return pl.pallas_call(
        functools.partial(_s5_scan_kernel, tc=tc),
        grid=(S5_TILES, seq // tc),
        in_specs=[pl.BlockSpec((tc, LANES), lambda j, c: (c, u_cb + j)),
                  pl.BlockSpec((None, LANES, 2 * ns), lambda j, c: (j, 0, 0)),
                  pl.BlockSpec((None, 2 * ns, LANES), lambda j, c: (j, 0, 0)),
                  pl.BlockSpec((None, 1, LANES), lambda j, c: (j, 0, 0)),
                  pl.BlockSpec((None, 2, SUBLANES, ns), lambda j, c: (j, 0, 0, 0)),
                  pl.BlockSpec((None, 2, 1, ns), lambda j, c: (j, 0, 0, 0)),
                  pl.BlockSpec((None, sub, 2 * ns), lambda j, c: (j, 0, 0))],
        out_specs=pl.BlockSpec((tc, LANES), lambda j, c: (c, j)),
        out_shape=jax.ShapeDtypeStruct((seq, S5_WIDTH), F32),
        scratch_shapes=[pltpu.VMEM((tc, LANES), F32), pltpu.VMEM((tc, 2 * ns), F32), pltpu.VMEM((tc, LANES), F32),
                        pltpu.VMEM((SUBLANES, 2 * ns), F32)],
        compiler_params=_cparams("parallel", "arbitrary"),
        name="s5_scan",
    )(proj, b_cat, c_cat, d, a_tab, an_tab, p_tab)


def _s5_glu_kernel(z_ref, w_ref, g_ref, o_ref, *, tn):
    j = pl.program_id(1)
    z = z_ref[...]
    a = jnp.dot(z.astype(BF16), w_ref[...], preferred_element_type=F32)
    zc = z_ref[:, pl.ds(pl.multiple_of(j * tn, tn), tn)]
    o_ref[...] = (zc * jax.nn.sigmoid(a) * _silu(g_ref[...])).astype(o_ref.dtype)


def _s5_glu(z, w_glu, proj, g_cb, *, tm, tn):
    seq, n = z.shape
    return pl.pallas_call(
        functools.partial(_s5_glu_kernel, tn=tn),
        grid=(seq // tm, n // tn),
        in_specs=[pl.BlockSpec((tm, n), lambda i, j: (i, 0)),
                  pl.BlockSpec((n, tn), lambda i, j: (0, j)),
                  pl.BlockSpec((tm, tn), lambda i, j: (i, g_cb * (n // tn) + j))],
        out_specs=pl.BlockSpec((tm, tn), lambda i, j: (i, j)),
        out_shape=jax.ShapeDtypeStruct((seq, n), BF16),
        compiler_params=_cparams("parallel", "arbitrary"),
        name="s5_glu",
    )(z, w_glu, proj)


def _rope_tables(pos, freq):
    ang = pos * freq
    lane = lax.broadcasted_iota(I32, ang.shape, 1)
    half = MLA_ROPE // 2
    cos = jnp.cos(ang)
    sin = jnp.sin(ang)
    c = jnp.where(lane < MLA_ROPE, cos, 0.0)
    s1 = jnp.where(lane < half, -sin, 0.0)
    s2 = jnp.where((lane >= half) & (lane < MLA_ROPE), sin, 0.0)
    return c, s1, s2


def _rope_apply(x, c, s1, s2):
    half = MLA_ROPE // 2
    return x * c + pltpu.roll(x, LANES - half, 1) * s1 + pltpu.roll(x, half, 1) * s2


def _mla_prep_kernel(qf_ref, kvf_ref, kr_ref, pos_ref, freq_ref, q_ref, k_ref, v_ref):
    c, s1, s2 = _rope_tables(pos_ref[...], freq_ref[...])
    kr = _rope_apply(kr_ref[...], c, s1, s2).astype(BF16)
    for h in range(MLA_HEADS):
        b = 2 * LANES * h
        q_ref[:, b:b + LANES] = qf_ref[:, b:b + LANES].astype(BF16)
        q_ref[:, b + LANES:b + 2 * LANES] = _rope_apply(qf_ref[:, b + LANES:b + 2 * LANES], c, s1, s2).astype(BF16)
        k_ref[:, b:b + LANES] = kvf_ref[:, b:b + LANES].astype(BF16)
        k_ref[:, b + LANES:b + 2 * LANES] = kr
        v_ref[:, LANES * h:LANES * (h + 1)] = kvf_ref[:, b + LANES:b + 2 * LANES].astype(BF16)


def _mla_prep(qf, kvf, proj, kr_cb, pos_col, freq, *, t):
    seq = qf.shape[0]
    w = 2 * LANES * MLA_HEADS
    return pl.pallas_call(
        _mla_prep_kernel,
        grid=(seq // t,),
        in_specs=[pl.BlockSpec((t, w), lambda i: (i, 0)),
                  pl.BlockSpec((t, w), lambda i: (i, 0)),
                  pl.BlockSpec((t, LANES), lambda i: (i, kr_cb)),
                  pl.BlockSpec((t, 1), lambda i: (i, 0)),
                  pl.BlockSpec((1, LANES), lambda i: (0, 0))],
        out_specs=[pl.BlockSpec((t, w), lambda i: (i, 0)),
                   pl.BlockSpec((t, w), lambda i: (i, 0)),
                   pl.BlockSpec((t, MLA_WIDTH), lambda i: (i, 0))],
        out_shape=[jax.ShapeDtypeStruct((seq, w), BF16), jax.ShapeDtypeStruct((seq, w), BF16),
                   jax.ShapeDtypeStruct((seq, MLA_WIDTH), BF16)],
        compiler_params=_cparams("parallel"),
        name="mla_prep",
    )(qf, kvf, proj, pos_col, freq)


def _t5_bucket(dist):
    n = jnp.maximum(dist, 0)
    max_exact = T5_BUCKETS // 2
    log_ratio = jnp.log(jnp.maximum(n, 1).astype(F32) / max_exact) / math.log(T5_MAX_DIST / max_exact)
    large = jnp.minimum(max_exact + (log_ratio * (T5_BUCKETS - max_exact)).astype(I32), T5_BUCKETS - 1)
    return jnp.where(n < max_exact, n, large)


T5_MASK_BUCKET = T5_BUCKETS


def _t5_lookup(table_row, bucket):
    rows, width = bucket.shape
    tab = jnp.broadcast_to(table_row, (rows, LANES))
    parts = [jnp.take_along_axis(tab, bucket[:, c:c + LANES], axis=1, mode="promise_in_bounds")
             for c in range(0, width, LANES)]
    return parts[0] if len(parts) == 1 else jnp.concatenate(parts, axis=1)


def _nsa_cmp_kernel(x_ref, pe_ref, w1_ref, w2_ref, o_ref, xf_ref, *, nc):
    half = NSA_CMP_LEN // 2
    d = NSA_HEAD_DIM
    xf_ref[...] = x_ref[...].astype(F32)
    u = jnp.zeros((nc, NSA_CMP_HIDDEN), F32)
    v = jnp.zeros((nc, NSA_CMP_HIDDEN), F32)
    for r in range(half):
        a = xf_ref[pl.ds(r, nc, stride=NSA_CMP_STRIDE), :]
        u = u + jnp.dot((a + pe_ref[r:r + 1, :]).astype(BF16), w1_ref[r * d:(r + 1) * d, :],
                        preferred_element_type=F32)
        v = v + jnp.dot((a + pe_ref[half + r:half + r + 1, :]).astype(BF16),
                        w1_ref[(half + r) * d:(half + r + 1) * d, :], preferred_element_type=F32)
    hid = u + pltpu.roll(v, nc - 1, 0)
    o_ref[...] = jnp.dot(jax.nn.gelu(hid).astype(BF16), w2_ref[...], preferred_element_type=F32).astype(o_ref.dtype)


def _nsa_compress(proj, k_cb, pe, w1, w2):
    seq = proj.shape[0]
    nc = seq // NSA_CMP_STRIDE
    d = NSA_HEAD_DIM
    g = NSA_KV_GROUPS
    return pl.pallas_call(
        functools.partial(_nsa_cmp_kernel, nc=nc),
        grid=(2, g),
        in_specs=[pl.BlockSpec((seq, d), lambda a, b: (0, k_cb + a * g + b)),
                  pl.BlockSpec((None, NSA_CMP_LEN, d), lambda a, b: (a, 0, 0)),
                  pl.BlockSpec((None, NSA_CMP_LEN * d, NSA_CMP_HIDDEN), lambda a, b: (a, 0, 0)),
                  pl.BlockSpec((None, NSA_CMP_HIDDEN, d), lambda a, b: (a, 0, 0))],
        out_specs=pl.BlockSpec((None, None, nc, d), lambda a, b: (a, b, 0, 0)),
        out_shape=jax.ShapeDtypeStruct((2, g, nc, d), BF16),
        scratch_shapes=[pltpu.VMEM((seq, d), F32)],
        compiler_params=_cparams("parallel", "arbitrary"),
        name="nsa_compress",
    )(proj, pe, w1, w2)


def _nsa_select_kernel(q_ref, kc_ref, vc_ref, posq_ref, posc_ref, tab_ref, gate_ref, ov_ref,
                       oc_ref, sel_ref, *, nc, n_slc, tq):
    qi = pl.program_id(0)
    d = NSA_HEAD_DIM
    c = d ** -0.5 * LOG2E
    tok = qi * tq + lax.broadcasted_iota(I32, (1, tq), 1)
    cmp_end = lax.broadcasted_iota(I32, (nc, 1), 0) * NSA_CMP_STRIDE + (NSA_CMP_LEN - 1)
    valid = cmp_end <= tok
    pos_c = jnp.concatenate([posc_ref[...]] * (tq // LANES), axis=1)
    bucket = jnp.where(valid, _t5_bucket(posq_ref[...] - pos_c), T5_MASK_BUCKET)
    gates = jax.nn.sigmoid(gate_ref[...])
    ovt = ov_ref[...]
    js = lax.broadcasted_iota(I32, (LANES, tq), 0)
    jf = js.astype(F32)
    cur = tok // NSA_SLC_BLOCK
    forced = (js == 0) | (js == cur) | (js == cur - 1)
    for g in range(NSA_KV_GROUPS):
        kc = kc_ref[g]
        vct = vc_ref[g].astype(F32).T.astype(BF16)
        psum = jnp.zeros((nc, tq), F32)
        for r in range(NSA_REP):
            h = g * NSA_REP + r
            qt = (q_ref[:, h * d:(h + 1) * d].astype(F32) * c).T.astype(BF16)
            t = jnp.dot(kc, qt, preferred_element_type=F32) + _t5_lookup(tab_ref[h:h + 1, :], bucket)
            m = jnp.max(t, axis=0, keepdims=True)
            e = jnp.exp2(t - m)
            l = jnp.sum(e, axis=0, keepdims=True)
            p = e * jnp.where(m > 0.5 * NEG_INF, 1.0 / l, 0.0)
            o = jnp.dot(vct, p.astype(BF16), preferred_element_type=F32)
            oc_ref[:, h * d:(h + 1) * d] = gates[:, 3 * h:3 * h + 1] * o.T
            psum = psum + p
        p_hi = psum.astype(BF16)
        p_lo = (psum - p_hi.astype(F32)).astype(BF16)
        imp = jnp.dot(ovt, p_hi, preferred_element_type=F32) + jnp.dot(ovt, p_lo, preferred_element_type=F32)
        st = jnp.where(forced, FORCE_SCORE, jnp.where(js > cur, -1.0, imp))
        st = jnp.where(js < n_slc, st, -2.0)
        sel = jnp.zeros((LANES, tq), F32)
        for _ in range(NSA_SLC_TOPK):
            mx = jnp.max(st, axis=0, keepdims=True)
            first = jnp.min(jnp.where(st == mx, jf, float(LANES)), axis=0, keepdims=True)
            hit = jf == first
            sel = jnp.where(hit, 1.0, sel)
            st = jnp.where(hit, -3e38, st)
        sel_ref[g] = sel.astype(sel_ref.dtype)


def _nsa_select(proj_b, q_cb, proj_f, gate_cb, kc, vc, pos_row, pos_cmp_rep, tab_t, ov_t, *, n_slc, tq):
    seq = proj_b.shape[0]
    nc = kc.shape[1]
    g = NSA_KV_GROUPS
    d = NSA_HEAD_DIM
    return pl.pallas_call(
        functools.partial(_nsa_select_kernel, nc=nc, n_slc=n_slc, tq=tq),
        grid=(seq // tq,),
        in_specs=[pl.BlockSpec((tq, NSA_WIDTH), lambda i: (i, q_cb)),
                  pl.BlockSpec((g, nc, d), lambda i: (0, 0, 0)),
                  pl.BlockSpec((g, nc, d), lambda i: (0, 0, 0)),
                  pl.BlockSpec((1, tq), lambda i: (0, i)),
                  pl.BlockSpec((nc, LANES), lambda i: (0, 0)),
                  pl.BlockSpec((SUBLANES, LANES), lambda i: (0, 0)),
                  pl.BlockSpec((tq, LANES), lambda i: (i, gate_cb)),
                  pl.BlockSpec((LANES, nc), lambda i: (0, 0))],
        out_specs=[pl.BlockSpec((tq, NSA_WIDTH), lambda i: (i, 0)),
                   pl.BlockSpec((g, LANES, tq), lambda i: (0, 0, i))],
        out_shape=[jax.ShapeDtypeStruct((seq, NSA_WIDTH), F32),
                   jax.ShapeDtypeStruct((g, LANES, seq), BF16)],
        compiler_params=_cparams("parallel"),
        name="nsa_select",
    )(proj_b, kc, vc, pos_row, pos_cmp_rep, tab_t * LOG2E, proj_f, ov_t)


def _nsa_slc_kernel(pqmin_ref, pkmax_ref, q_ref, k_ref, v_ref, sel_ref, posq_ref, posk_ref, tab_ref, o_ref,
                    vt_ref, qt_ref, s_ref, m_ref, acc_ref, *, tq, tk):
    g = pl.program_id(0)
    qi = pl.program_id(1)
    d = NSA_HEAD_DIM
    scale = d ** -0.5
    seq = k_ref.shape[0]

    @pl.when(qi == 0)
    def _():
        _build_vt(v_ref, vt_ref, 0, d, seq, tk)

    for r in range(NSA_REP):
        qt_ref[r] = (q_ref[:, r * d:(r + 1) * d].astype(F32) * (scale * LOG2E)).T.astype(BF16)
    m_ref[...] = jnp.full(m_ref.shape, NEG_INF, F32)
    acc_ref[...] = jnp.zeros(acc_ref.shape, F32)
    sel = sel_ref[...]
    pos_q = posq_ref[...]
    key_blk = lax.broadcasted_iota(I32, (tk, LANES), 0) // NSA_SLC_BLOCK
    blk = lax.broadcasted_iota(I32, (tk, LANES), 1)
    per_tile = tk // NSA_SLC_BLOCK
    n_full = (qi * tq) // tk

    def scores(j, slot):
        start = pl.multiple_of(j * tk, tk)
        k = k_ref[pl.ds(start, tk), :]
        for r in range(NSA_REP):
            s_ref[slot, r] = jnp.dot(k, qt_ref[r], preferred_element_type=F32)

    def softmax_pv(j, slot, masked, near):
        start = pl.multiple_of(j * tk, tk)
        expand = jnp.where(blk == key_blk + j * per_tile, 1.0, 0.0).astype(BF16)
        picked = jnp.dot(expand, sel, preferred_element_type=F32)
        if masked:
            key = start + lax.broadcasted_iota(I32, (tk, tq), 0)
            qry = qi * tq + lax.broadcasted_iota(I32, (tk, tq), 1)
            picked = jnp.where(key <= qry, picked, 0.0)
        mask = picked > 0.5
        if near:
            pos_k = jnp.concatenate([posk_ref[pl.ds(start, tk), :]] * (tq // LANES), axis=1)
            bucket = jnp.where(mask, _t5_bucket(pos_q - pos_k), T5_MASK_BUCKET)
        for r in range(NSA_REP):
            h = g * NSA_REP + r
            t = s_ref[slot, r]
            if near:
                t = t + _t5_lookup(tab_ref[pl.ds(h, 1), :], bucket)
            else:
                t = jnp.where(mask, t, NEG_INF)
            m_prev = m_ref[r]
            mx = jnp.max(t, axis=0, keepdims=True)
            if near:
                m_new = jnp.maximum(m_prev, mx)
                shift = m_new
            else:
                b = tab_ref[pl.ds(h, 1), T5_BUCKETS - 1:T5_BUCKETS]
                m_new = jnp.maximum(m_prev, mx + b)
                shift = m_new - b
            alpha = jnp.exp2(m_prev - m_new)
            p = jnp.exp2(t - shift).astype(BF16)
            acc_ref[r] = alpha * acc_ref[r] + jnp.dot(vt_ref[0, :, pl.ds(start, tk)], p, preferred_element_type=F32)
            m_ref[r] = m_new

    def stage(j, slot):
        far = pqmin_ref[qi] - pkmax_ref[j] >= T5_MAX_DIST

        @pl.when(far)
        def _():
            scores(j + 1, 1 - slot)
            softmax_pv(j, slot, False, False)

        @pl.when(jnp.logical_not(far))
        def _():
            scores(j + 1, 1 - slot)
            softmax_pv(j, slot, False, True)

    scores(0, 0)

    def pair(jj, carry):
        stage(2 * jj, 0)
        stage(2 * jj + 1, 1)
        return carry

    pairs = n_full // 2
    lax.fori_loop(0, pairs, pair, 0)
    last = 2 * pairs

    @pl.when(n_full % 2 == 1)
    def _():
        stage(last, 0)
        softmax_pv(last + 1, 1, True, True)

    @pl.when(n_full % 2 == 0)
    def _():
        softmax_pv(last, 0, True, True)

    for r in range(NSA_REP):
        a = acc_ref[r]
        o_ref[:, r * d:(r + 1) * d] = (a[0:d, :] / a[d:d + 1, :]).T


def _nsa_slc(proj_b, q_off, k_off, v_off, sel, pos_row, pos_rep, tab_t, pq_min, pk_max, *, tq, tk):
    seq = proj_b.shape[0]
    tab_t = tab_t * LOG2E
    d = NSA_HEAD_DIM
    gw = NSA_REP * d
    once = pl.Buffered(1)
    grid_spec = pltpu.PrefetchScalarGridSpec(
        num_scalar_prefetch=2,
        grid=(NSA_KV_GROUPS, seq // tq),
        in_specs=[pl.BlockSpec((tq, gw), lambda g, i, *_: (i, q_off // gw + g)),
                  pl.BlockSpec((seq, d), lambda g, i, *_: (0, k_off // d + g), pipeline_mode=once),
                  pl.BlockSpec((seq, d), lambda g, i, *_: (0, v_off // d + g), pipeline_mode=once),
                  pl.BlockSpec((None, LANES, tq), lambda g, i, *_: (g, 0, i)),
                  pl.BlockSpec((1, tq), lambda g, i, *_: (0, i)),
                  pl.BlockSpec((seq, LANES), lambda g, i, *_: (0, 0), pipeline_mode=once),
                  pl.BlockSpec((SUBLANES, LANES), lambda g, i, *_: (0, 0))],
        out_specs=pl.BlockSpec((tq, gw), lambda g, i, *_: (i, g)),
        scratch_shapes=[pltpu.VMEM((1, d + ONES_ROWS, seq), BF16), pltpu.VMEM((NSA_REP, d, tq), BF16),
                        pltpu.VMEM((2, NSA_REP, tk, tq), F32), pltpu.VMEM((NSA_REP, 1, tq), F32),
                        pltpu.VMEM((NSA_REP, d + ONES_ROWS, tq), F32)])
    return pl.pallas_call(
        functools.partial(_nsa_slc_kernel, tq=tq, tk=tk),
        grid_spec=grid_spec,
        out_shape=jax.ShapeDtypeStruct((seq, NSA_WIDTH), F32),
        compiler_params=_cparams("parallel", "arbitrary"),
        name="nsa_slc",
    )(pq_min, pk_max, proj_b, proj_b, proj_b, sel, pos_row, pos_rep, tab_t)


def _nsa_win_kernel(*refs, tq, nt):
    q_ref = refs[0]
    k_refs = refs[1:1 + nt]
    v_refs = refs[1 + nt:1 + 2 * nt]
    pk_refs = refs[1 + 2 * nt:1 + 3 * nt]
    posq_ref, tab_ref, gate_ref, gout_ref, oc_ref, os_ref, o_ref = refs[1 + 3 * nt:]
    qi = pl.program_id(0)
    d = NSA_HEAD_DIM
    c = d ** -0.5 * LOG2E
    tok = qi * tq + lax.broadcasted_iota(I32, (1, tq), 1)
    lower = jnp.maximum(tok - (NSA_WINDOW - 1), 0)
    sub = lax.broadcasted_iota(I32, (tq, 1), 0)
    pos_q = posq_ref[...]
    buckets = []
    for jj in range(nt):
        kidx = (qi - (nt - 1) + jj) * tq + sub
        pos_k = jnp.concatenate([pk_refs[jj][...]] * (tq // LANES), axis=1)
        b = jnp.where(kidx >= lower, _t5_bucket(pos_q - pos_k), T5_MASK_BUCKET)
        buckets.append(jnp.where(kidx <= tok, b, T5_MASK_BUCKET))
    gates = jax.nn.sigmoid(gate_ref[...])
    for g in range(NSA_KV_GROUPS):
        ks = [kr[:, g * d:(g + 1) * d] for kr in k_refs]
        vts = [vr[:, g * d:(g + 1) * d].astype(F32).T.astype(BF16) for vr in v_refs]
        for r in range(NSA_REP):
            h = g * NSA_REP + r
            hs = slice(h * d, (h + 1) * d)
            qt = (q_ref[:, hs].astype(F32) * c).T.astype(BF16)
            ts = [jnp.dot(ks[jj], qt, preferred_element_type=F32) + _t5_lookup(tab_ref[h:h + 1, :], buckets[jj])
                  for jj in range(nt)]
            m = functools.reduce(jnp.maximum, [jnp.max(t, axis=0, keepdims=True) for t in ts])
            ps = [jnp.exp2(t - m) for t in ts]
            l = functools.reduce(jnp.add, [jnp.sum(p, axis=0, keepdims=True) for p in ps])
            o_t = functools.reduce(jnp.add, [jnp.dot(vt, p.astype(BF16), preferred_element_type=F32)
                                             for vt, p in zip(vts, ps)])
            o_w = (o_t / l).T
            o = oc_ref[:, hs] + gates[:, 3 * h + 1:3 * h + 2] * os_ref[:, hs] + gates[:, 3 * h + 2:3 * h + 3] * o_w
            o_ref[:, hs] = (o * _silu(gout_ref[:, hs])).astype(o_ref.dtype)


def _nsa_win(proj_b, q_cb, k_cb, v_cb, proj_f, gate_cb, gout_cb, oc, o_s, pos_row, pos_rep, tab_t, *, tq):
    seq = proj_b.shape[0]
    nt = NSA_WINDOW // tq + 1

    def band_rows(jj, cb):
        return pl.BlockSpec((tq, NSA_KV), lambda i: (jnp.maximum(i - (nt - 1) + jj, 0), cb))

    def band_pos(jj):
        return pl.BlockSpec((tq, LANES), lambda i: (jnp.maximum(i - (nt - 1) + jj, 0), 0))

    in_specs = [pl.BlockSpec((tq, NSA_WIDTH), lambda i: (i, q_cb))]
    in_specs += [band_rows(jj, k_cb) for jj in range(nt)]
    in_specs += [band_rows(jj, v_cb) for jj in range(nt)]
    in_specs += [band_pos(jj) for jj in range(nt)]
    in_specs += [pl.BlockSpec((1, tq), lambda i: (0, i)),
                 pl.BlockSpec((SUBLANES, LANES), lambda i: (0, 0)),
                 pl.BlockSpec((tq, LANES), lambda i: (i, gate_cb)),
                 pl.BlockSpec((tq, NSA_WIDTH), lambda i: (i, gout_cb)),
                 pl.BlockSpec((tq, NSA_WIDTH), lambda i: (i, 0)),
                 pl.BlockSpec((tq, NSA_WIDTH), lambda i: (i, 0))]
    args = [proj_b] * (1 + 2 * nt) + [pos_rep] * nt + [pos_row, tab_t * LOG2E, proj_f, proj_f, oc, o_s]
    return pl.pallas_call(
        functools.partial(_nsa_win_kernel, tq=tq, nt=nt),
        grid=(seq // tq,),
        in_specs=in_specs,
        out_specs=pl.BlockSpec((tq, NSA_WIDTH), lambda i: (i, 0)),
        out_shape=jax.ShapeDtypeStruct((seq, NSA_WIDTH), BF16),
        compiler_params=_cparams("parallel"),
        name="nsa_win",
    )(*args)


def _in_proj(h, norm_g, w_b, w_f, tiles, name):
    proj_b = _norm_matmul(h, norm_g, w_b, tm=tiles["tm"], tn=512, name=name + "_b", out_dtype=BF16)
    proj_f = _norm_matmul(h, norm_g, w_f, tm=tiles["tm"], tn=768, name=name + "_f")
    return proj_b, proj_f


def _even_layer(h, mem_kv, norm_g, w_in, s5, w_glu, b_f, tiles):
    pb, pf = _in_proj(h, norm_g, *w_in, tiles, "in_proj_even")
    ob, of = EVEN_B_OFF, EVEN_F_OFF
    b_cat, c_cat, tab, d_skip = s5
    z = _s5_scan(pf, of["u"] // LANES, b_cat, c_cat, d_skip, tab, tc=tiles["s5_tc"])
    y_s5 = _s5_glu(z, w_glu, pf, of["g_s5"] // S5_WIDTH, tm=tiles["tm"], tn=512)
    decay = _decay(pf, of["f"] // LANES, b_f, t=tiles["decay_t"])
    d = FOX_HEAD_DIM
    y_fox = _flash(pb, ob["q"], pb, ob["k"], pb, ob["v"], pf, of["g_fox"], heads=FOX_HEADS, dk=d, dv=d,
                   scale=d ** -0.5, tq=tiles["attn_tq"], tk=tiles["attn_tk"], hp=tiles["attn_hp"], decay=decay,
                   name="fox_attn")
    y_mem = _mem_attn(pb, ob["q_mem"] // MEM_HEAD_DIM, pf, of["g_mem"] // MEM_HEAD_DIM, mem_kv, t=tiles["mem_t"])
    return y_s5, y_fox, y_mem


def _odd_layer(h, mem_kv, norm_g, w_in, mla, nsa, pos, tiles):
    pb, pf = _in_proj(h, norm_g, *w_in, tiles, "in_proj_odd")
    ob, of = ODD_B_OFF, ODD_F_OFF
    g_cq, g_ckv, w_uq, w_ukv, freq = mla
    pos_col_f, pos_row, pos_cmp_rep, pos_rep, pq_min, pk_max = pos
    qf = _norm_matmul(pf, g_cq, w_uq, x_cb=of["c_q"] // MLA_Q_RANK, tm=tiles["tm"], tn=512, name="mla_q_up")
    kvf = _norm_matmul(pf, g_ckv, w_ukv, x_cb=of["c_kv"] // MLA_KV_RANK, tm=tiles["tm"], tn=512, name="mla_kv_up")
    q_r, k_r, v_r = _mla_prep(qf, kvf, pf, of["k_rope"] // LANES, pos_col_f, freq, t=tiles["prep_t"])
    y_mla = _flash(q_r, 0, k_r, 0, v_r, 0, pf, of["g_mla"], heads=MLA_HEADS, dk=2 * LANES, dv=MLA_V,
                   scale=(MLA_NOPE + MLA_ROPE) ** -0.5, tq=tiles["attn_tq"], tk=tiles["attn_tk"], hp=tiles["attn_hp"],
                   name="mla_attn")
    pe, w1, w2, tab_t, ov, n_slc = nsa
    kvc = _nsa_compress(pb, ob["k_cmp"] // NSA_HEAD_DIM, pe, w1, w2)
    oc, sel = _nsa_select(pb, ob["q_nsa"] // NSA_WIDTH, pf, of["gates"] // LANES, kvc[0], kvc[1], pos_row,
                          pos_cmp_rep, tab_t, ov, n_slc=n_slc, tq=tiles["nsa_tq"])
    o_s = _nsa_slc(pb, ob["q_nsa"], ob["k_slc"], ob["v_slc"], sel, pos_row, pos_rep, tab_t, pq_min, pk_max,
                   tq=tiles["slc_tq"], tk=tiles["slc_tk"])
    y_nsa = _nsa_win(pb, ob["q_nsa"] // NSA_WIDTH, ob["k_win"] // NSA_KV, ob["v_win"] // NSA_KV,
                     pf, of["gates"] // LANES, of["g_nsa"] // NSA_WIDTH, oc, o_s, pos_row, pos_rep, tab_t,
                     tq=tiles["win_tq"])
    y_mem = _mem_attn(pb, ob["q_mem"] // MEM_HEAD_DIM, pf, of["g_mem"] // MEM_HEAD_DIM, mem_kv, t=tiles["mem_t"])
    return y_mla, y_nsa, y_mem


def _tiles(seq):
    return {"tm": min(seq, 1024), "s5_tc": min(seq, 1024), "decay_t": min(seq, 512), "attn_tq": min(seq, 256),
            "attn_tk": min(seq, 512), "attn_hp": 2, "slc_tq": 256, "slc_tk": 512, "nsa_tq": 256, "win_tq": 128,
            "mem_t": min(seq, 512), "prep_t": min(seq, 256)}


def _context(positions, t5_table, seq, tiles):
    pos = positions[0]
    pos_col = pos.reshape(seq, 1)
    pos_row = pos.reshape(1, seq)
    pos_rep = jnp.broadcast_to(pos_col, (seq, LANES))
    pq_min = jnp.min(pos.reshape(seq // tiles["slc_tq"], tiles["slc_tq"]), axis=1)
    pk_max = jnp.max(pos.reshape(seq // tiles["slc_tk"], tiles["slc_tk"]), axis=1)
    nc = seq // NSA_CMP_STRIDE
    pos_cmp = jnp.pad(pos[NSA_CMP_LEN - 1::NSA_CMP_STRIDE], (0, 1))
    pos_cmp_rep = jnp.broadcast_to(pos_cmp.reshape(nc, 1), (nc, LANES))
    half = MLA_ROPE // 2
    inv_freq = ROPE_THETA ** (-jnp.arange(half, dtype=F32) / half)
    freq = jnp.concatenate([inv_freq, inv_freq, jnp.zeros((LANES - MLA_ROPE,), F32)]).reshape(1, LANES)
    tab_t = jnp.pad(t5_table.astype(F32).T, ((0, SUBLANES - NSA_HEADS), (0, LANES - T5_BUCKETS)))
    tab_t = tab_t.at[:, T5_MASK_BUCKET].set(NEG_INF)
    n_slc = seq // NSA_SLC_BLOCK
    cs = np.arange(nc) * NSA_CMP_STRIDE
    ss = np.arange(LANES) * NSA_SLC_BLOCK
    ov_np = np.clip(np.minimum(cs[:, None] + NSA_CMP_LEN, ss[None, :] + NSA_SLC_BLOCK)
                    - np.maximum(cs[:, None], ss[None, :]), 0, None) / NSA_CMP_LEN
    ov_np[nc - 1, :] = 0.0
    ov_np[:, n_slc:] = 0.0
    return {"pos": (pos_col.astype(F32), pos_row, pos_cmp_rep, pos_rep, pq_min, pk_max), "freq": freq,
            "tab_t": tab_t, "ov": jnp.asarray(ov_np.T, BF16), "n_slc": n_slc}


def _odd_params(i, mla_g_cq, mla_g_ckv, mla_w_uq, mla_w_ukv, nsa_cmp_pe, nsa_cmp_w1, nsa_cmp_w2, ctx):
    dq = MLA_NOPE + MLA_ROPE
    w_uq = mla_w_uq[i].reshape(MLA_Q_RANK, MLA_HEADS, dq)
    w_uq = jnp.pad(w_uq, ((0, 0), (0, 0), (0, 2 * LANES - dq))).reshape(MLA_Q_RANK, -1).astype(BF16)
    mla = (mla_g_cq[i], mla_g_ckv[i], w_uq, mla_w_ukv[i].astype(BF16), ctx["freq"])
    nsa = (nsa_cmp_pe[i].astype(F32), nsa_cmp_w1[i].astype(BF16), nsa_cmp_w2[i].astype(BF16), ctx["tab_t"],
           ctx["ov"], ctx["n_slc"])
    return mla, nsa


def kernel(x, mem, positions, norm_g, mem_norm_g, final_norm_g, t5_table, w_out, mem_w_kv, even_w_in, s5_lam_re,
           s5_lam_im, s5_log_dt, s5_b_re, s5_b_im, s5_c_re, s5_c_im, s5_d, s5_w_glu, fox_b_f, odd_w_in, mla_g_cq,
           mla_g_ckv, mla_w_uq, mla_w_ukv, nsa_cmp_pe, nsa_cmp_w1, nsa_cmp_w2):
    batch, seq, _ = x.shape
    assert batch == 1 and seq % 1024 == 0 and seq // NSA_SLC_BLOCK <= LANES
    depth = norm_g.shape[0]
    tiles = _tiles(seq)
    ctx = _context(positions, t5_table, seq, tiles)
    h = x[0]
    w_out_b = w_out.astype(BF16)
    mem_kv_all = _mem_kv_all(mem[0], mem_norm_g, mem_w_kv.astype(BF16))
    for layer in range(depth):
        i = layer // 2
        mem_kv = (mem_kv_all, layer)
        if layer % 2 == 0:
            w_in = (_reorder_w_in(even_w_in[i], EVEN_SPLITS, EVEN_B_ORDER),
                    _reorder_w_in(even_w_in[i], EVEN_SPLITS, EVEN_F_ORDER))
            b_cat, c_cat, tab = _s5_prepare(s5_lam_re[i], s5_lam_im[i], s5_log_dt[i], s5_b_re[i], s5_b_im[i],
                                            s5_c_re[i], s5_c_im[i], tiles["s5_tc"] // SUBLANES)
            ys = _even_layer(h, mem_kv, norm_g[layer], w_in, (b_cat, c_cat, tab, s5_d[i]),
                             s5_w_glu[i].astype(BF16), fox_b_f[i], tiles)
        else:
            w_in = (_reorder_w_in(odd_w_in[i], ODD_SPLITS, ODD_B_ORDER),
                    _reorder_w_in(odd_w_in[i], ODD_SPLITS, ODD_F_ORDER))
            mla, nsa = _odd_params(i, mla_g_cq, mla_g_ckv, mla_w_uq, mla_w_ukv, nsa_cmp_pe, nsa_cmp_w1, nsa_cmp_w2,
                                   ctx)
            ys = _odd_layer(h, mem_kv, norm_g[layer], w_in, mla, nsa, ctx["pos"], tiles)
        h = _out_proj(h, ys, w_out_b, layer, tm=tiles["tm"], tn=512)
    return _final_norm(h, final_norm_g, tm=tiles["mem_t"])[None]
```

```python
import functools
import math

import numpy as np
import jax
import jax.numpy as jnp
from jax import lax
from jax.experimental import pallas as pl
from jax.experimental.pallas import tpu as pltpu

F32 = jnp.float32
BF16 = jnp.bfloat16
I32 = jnp.int32

D_MODEL = 2048
DEPTH = 4
N_MEM = 256
RMS_EPS = 1e-6
NEG_INF = -1e30
LOG2E = math.log2(math.e)

S5_WIDTH = 1024
S5_GROUP = 16
S5_GROUPS = S5_WIDTH // S5_GROUP
S5_STATE = 64
FOX_HEADS = 8
FOX_HEAD_DIM = 128
FOX_WIDTH = FOX_HEADS * FOX_HEAD_DIM
MEM_HEADS = 4
MEM_HEAD_DIM = 128
MEM_WIDTH = MEM_HEADS * MEM_HEAD_DIM
MLA_HEADS = 8
MLA_Q_RANK = 512
MLA_KV_RANK = 512
MLA_NOPE = 128
MLA_ROPE = 64
MLA_V = 128
MLA_WIDTH = MLA_HEADS * MLA_V
ROPE_THETA = 10000.0
NSA_HEADS = 8
NSA_KV_GROUPS = 2
NSA_REP = NSA_HEADS // NSA_KV_GROUPS
NSA_HEAD_DIM = 128
NSA_WIDTH = NSA_HEADS * NSA_HEAD_DIM
NSA_KV = NSA_KV_GROUPS * NSA_HEAD_DIM
NSA_CMP_LEN = 32
NSA_CMP_STRIDE = 16
NSA_CMP_HIDDEN = 256
NSA_SLC_BLOCK = 64
NSA_SLC_TOPK = 16
NSA_WINDOW = 512
FORCE_SCORE = 1e6
T5_BUCKETS = 32
T5_MAX_DIST = 1024

EVEN_SPLITS = (S5_WIDTH, S5_WIDTH, FOX_WIDTH, FOX_WIDTH, FOX_WIDTH, FOX_HEADS, FOX_WIDTH, MEM_WIDTH, MEM_WIDTH)
ODD_SPLITS = (MLA_Q_RANK, MLA_KV_RANK, MLA_ROPE, MLA_WIDTH, NSA_WIDTH, NSA_KV, NSA_KV, NSA_KV, NSA_KV, NSA_KV,
              NSA_KV, 3 * NSA_HEADS, NSA_WIDTH, MEM_WIDTH, MEM_WIDTH)

LANES = 128
SUBLANES = 8
VMEM_LIMIT_BYTES = 56 * 1024 * 1024

EVEN_B_ORDER = (("q", 2, 1024), ("k", 3, 1024), ("v", 4, 1024), ("q_mem", 7, 512))
EVEN_F_ORDER = (("u", 0, 1024), ("g_s5", 1, 1024), ("g_fox", 6, 1024), ("g_mem", 8, 512), ("f", 5, 128),
                ("pad", None, 128))
ODD_B_ORDER = (("q_nsa", 4, 1024), ("q_mem", 13, 512), ("k_cmp", 5, 256), ("v_cmp", 6, 256), ("k_slc", 7, 256),
               ("v_slc", 8, 256), ("k_win", 9, 256), ("v_win", 10, 256))
ODD_F_ORDER = (("g_nsa", 12, 1024), ("g_mla", 3, 1024), ("c_q", 0, 512), ("c_kv", 1, 512), ("g_mem", 14, 512),
               ("k_rope", 2, 128), ("gates", 11, 128))


def _layout(order):
    off, out = 0, {}
    for name, _, width in order:
        assert off % width == 0
        out[name] = off
        off += width
    return out, off


EVEN_B_OFF, EVEN_B_N = _layout(EVEN_B_ORDER)
EVEN_F_OFF, EVEN_F_N = _layout(EVEN_F_ORDER)
ODD_B_OFF, ODD_B_N = _layout(ODD_B_ORDER)
ODD_F_OFF, ODD_F_N = _layout(ODD_F_ORDER)


def _reorder_w_in(w, splits, order):
    starts = np.concatenate([[0], np.cumsum(splits)])
    cols = []
    for _, idx, width in order:
        if idx is None:
            cols.append(jnp.zeros((w.shape[0], width), w.dtype))
            continue
        seg = w[:, int(starts[idx]):int(starts[idx + 1])]
        pad = width - seg.shape[1]
        if pad:
            seg = jnp.pad(seg, ((0, 0), (0, pad)))
        cols.append(seg)
    return jnp.concatenate(cols, axis=1).astype(BF16)


def _cparams(*sem):
    return pltpu.CompilerParams(dimension_semantics=sem, vmem_limit_bytes=VMEM_LIMIT_BYTES)


def _silu(g):
    return g * jax.nn.sigmoid(g)


def _pick(n, cands):
    for c in cands:
        if n % c == 0:
            return c
    raise ValueError(f"no tile for {n} in {cands}")


def _norm_matmul_kernel(x_ref, g_ref, w_ref, o_ref, xn_ref):
    @pl.when(pl.program_id(1) == 0)
    def _():
        x = x_ref[...]
        ms = jnp.mean(x * x, axis=-1, keepdims=True)
        xn_ref[...] = (x * lax.rsqrt(ms + RMS_EPS) * g_ref[...]).astype(BF16)

    o_ref[...] = jnp.dot(xn_ref[...], w_ref[...], preferred_element_type=F32).astype(o_ref.dtype)


def _norm_matmul(x, g, w, *, x_cb=0, tm, tn, name, out_dtype=F32):
    m = x.shape[0]
    k, n = w.shape
    return pl.pallas_call(
        _norm_matmul_kernel,
        grid=(m // tm, n // tn),
        in_specs=[pl.BlockSpec((tm, k), lambda i, j: (i, x_cb)),
                  pl.BlockSpec((1, k), lambda i, j: (0, 0)),
                  pl.BlockSpec((k, tn), lambda i, j: (0, j))],
        out_specs=pl.BlockSpec((tm, tn), lambda i, j: (i, j)),
        out_shape=jax.ShapeDtypeStruct((m, n), out_dtype),
        scratch_shapes=[pltpu.VMEM((tm, k), BF16)],
        compiler_params=_cparams("parallel", "arbitrary"),
        name=name,
    )(x, g.reshape(1, k), w)


def _out_proj_kernel(h_ref, *refs):
    o_ref = refs[-1]
    n = (len(refs) - 1) // 2
    acc = h_ref[...]
    for y_ref, w_ref in zip(refs[:n], refs[n:2 * n]):
        acc = acc + jnp.dot(y_ref[...], w_ref[...], preferred_element_type=F32)
    o_ref[...] = acc


def _out_proj(h, ys, w_all, layer, *, tm, tn):
    m, n = h.shape
    in_specs = [pl.BlockSpec((tm, tn), lambda i, j: (i, j))]
    in_specs += [pl.BlockSpec((tm, y.shape[1]), lambda i, j: (i, 0)) for y in ys]
    row = 0
    for y in ys:
        width = y.shape[1]
        assert row % width == 0
        in_specs.append(pl.BlockSpec((None, width, tn), lambda i, j, rb=row // width: (layer, rb, j)))
        row += width
    return pl.pallas_call(
        _out_proj_kernel,
        grid=(m // tm, n // tn),
        in_specs=in_specs,
        out_specs=pl.BlockSpec((tm, tn), lambda i, j: (i, j)),
        out_shape=jax.ShapeDtypeStruct((m, n), F32),
        compiler_params=_cparams("parallel", "arbitrary"),
        name="out_proj",
    )(h, *ys, *([w_all] * len(ys)))


def _final_norm_kernel(x_ref, g_ref, o_ref):
    x = x_ref[...]
    ms = jnp.mean(x * x, axis=-1, keepdims=True)
    o_ref[...] = x * lax.rsqrt(ms + RMS_EPS) * g_ref[...]


def _final_norm(h, g, *, tm):
    m, n = h.shape
    return pl.pallas_call(
        _final_norm_kernel,
        grid=(m // tm,),
        in_specs=[pl.BlockSpec((tm, n), lambda i: (i, 0)), pl.BlockSpec((1, n), lambda i: (0, 0))],
        out_specs=pl.BlockSpec((tm, n), lambda i: (i, 0)),
        out_shape=jax.ShapeDtypeStruct((m, n), F32),
        compiler_params=_cparams("parallel"),
        name="final_norm",
    )(h, g.reshape(1, n))


ONES_ROWS = 16


def _build_vt(v_ref, vt_ref, hh, dv, seq, chunk):
    def body(c, carry):
        st = pl.multiple_of(c * chunk, chunk)
        vt_ref[hh, 0:dv, pl.ds(st, chunk)] = v_ref[pl.ds(st, chunk), hh * dv:(hh + 1) * dv].astype(F32).T.astype(BF16)
        return carry

    lax.fori_loop(0, seq // chunk, body, 0)
    vt_ref[hh, dv:dv + ONES_ROWS, :] = jnp.ones((ONES_ROWS, seq), BF16)


def _flash_kernel(*refs, scale, tq, tk, hp, dk, dv, has_decay):
    if has_decay:
        q_ref, k_ref, v_ref, g_ref, cq_ref, ck_ref, o_ref, vt_ref, qt_ref, s_ref, m_ref, acc_ref = refs
    else:
        q_ref, k_ref, v_ref, g_ref, o_ref, vt_ref, qt_ref, s_ref, m_ref, acc_ref = refs
    hb = pl.program_id(0)
    qi = pl.program_id(1)
    seq = k_ref.shape[0]

    @pl.when(qi == 0)
    def _():
        for hh in range(hp):
            _build_vt(v_ref, vt_ref, hh, dv, seq, tk)

    for hh in range(hp):
        qt_ref[hh] = (q_ref[:, hh * dk:(hh + 1) * dk].astype(F32) * (scale * LOG2E)).T.astype(BF16)
    m_ref[...] = jnp.full(m_ref.shape, NEG_INF, F32)
    acc_ref[...] = jnp.zeros(acc_ref.shape, F32)
    n_full = (qi * tq) // tk
    if has_decay:
        cq2 = [cq_ref[pl.ds(hb * hp + hh, 1), :] for hh in range(hp)]

    def scores(j, slot):
        start = pl.multiple_of(j * tk, tk)
        for hh in range(hp):
            s_ref[slot, hh] = jnp.dot(k_ref[pl.ds(start, tk), hh * dk:(hh + 1) * dk], qt_ref[hh],
                                      preferred_element_type=F32)

    def softmax_pv(j, slot, masked):
        start = pl.multiple_of(j * tk, tk)
        for hh in range(hp):
            t = s_ref[slot, hh]
            if has_decay:
                t = t - jnp.concatenate([ck_ref[hh, pl.ds(start, tk), :]] * (tq // LANES), axis=1)
            if masked:
                key = start + lax.broadcasted_iota(I32, (tk, tq), 0)
                qry = qi * tq + lax.broadcasted_iota(I32, (tk, tq), 1)
                t = jnp.where(key <= qry, t, NEG_INF)
            m_prev = m_ref[hh]
            mx = jnp.max(t, axis=0, keepdims=True)
            if has_decay:
                m_new = jnp.maximum(m_prev, mx + cq2[hh])
                shift = m_new - cq2[hh]
            else:
                m_new = jnp.maximum(m_prev, mx)
                shift = m_new
            alpha = jnp.exp2(m_prev - m_new)
            p = jnp.exp2(t - shift).astype(BF16)
            acc_ref[hh] = alpha * acc_ref[hh] + jnp.dot(vt_ref[hh, :, pl.ds(start, tk)], p,
                                                         preferred_element_type=F32)
            m_ref[hh] = m_new

    scores(0, 0)

    def pair(j):
        scores(j + 1, 1)
        softmax_pv(j, 0, False)
        scores(j + 2, 0)
        softmax_pv(j + 1, 1, False)

    def quad(qq, carry):
        pair(4 * qq)
        pair(4 * qq + 2)
        return carry

    quads = n_full // 4
    lax.fori_loop(0, quads, quad, 0)
    rem = n_full - 4 * quads

    @pl.when(rem >= 2)
    def _():
        pair(4 * quads)

    last = 4 * quads + 2 * (rem // 2)

    @pl.when(n_full % 2 == 1)
    def _():
        scores(last + 1, 1)
        softmax_pv(last, 0, False)
        softmax_pv(last + 1, 1, True)

    @pl.when(n_full % 2 == 0)
    def _():
        softmax_pv(last, 0, True)

    for hh in range(hp):
        a = acc_ref[hh]
        o = (a[0:dv, :] / a[dv:dv + 1, :]).T
        o_ref[:, hh * dv:(hh + 1) * dv] = (o * _silu(g_ref[:, hh * dv:(hh + 1) * dv])).astype(o_ref.dtype)


def _flash(q_arr, q_off, k_arr, k_off, v_arr, v_off, g_arr, g_off, *, heads, dk, dv, scale, tq, tk, hp,
           decay=None, name):
    seq = q_arr.shape[0]
    once = pl.Buffered(1)
    in_specs = [pl.BlockSpec((tq, hp * dk), lambda h, i: (i, q_off // (hp * dk) + h)),
                pl.BlockSpec((seq, hp * dk), lambda h, i: (0, k_off // (hp * dk) + h), pipeline_mode=once),
                pl.BlockSpec((seq, hp * dv), lambda h, i: (0, v_off // (hp * dv) + h), pipeline_mode=once),
                pl.BlockSpec((tq, hp * dv), lambda h, i: (i, g_off // (hp * dv) + h))]
    args = [q_arr, k_arr, v_arr, g_arr]
    if decay is not None:
        cum_t, cum_rep = decay
        in_specs += [pl.BlockSpec((SUBLANES, tq), lambda h, i: (0, i)),
                     pl.BlockSpec((hp, seq, LANES), lambda h, i: (h, 0, 0), pipeline_mode=once)]
        args += [cum_t, cum_rep]
    return pl.pallas_call(
        functools.partial(_flash_kernel, scale=scale, tq=tq, tk=tk, hp=hp, dk=dk, dv=dv, has_decay=decay is not None),
        grid=(heads // hp, seq // tq),
        in_specs=in_specs,
        out_specs=pl.BlockSpec((tq, hp * dv), lambda h, i: (i, h)),
        out_shape=jax.ShapeDtypeStruct((seq, heads * dv), BF16),
        scratch_shapes=[pltpu.VMEM((hp, dv + ONES_ROWS, seq), BF16), pltpu.VMEM((hp, dk, tq), BF16),
                        pltpu.VMEM((2, hp, tk, tq), F32), pltpu.VMEM((hp, 1, tq), F32),
                        pltpu.VMEM((hp, dv + ONES_ROWS, tq), F32)],
        compiler_params=_cparams("parallel", "arbitrary"),
        name=name,
    )(*args)


def _decay_kernel(f_ref, b_ref, ct_ref, cr_ref, carry_ref, *, t):
    i = pl.program_id(0)

    @pl.when(i == 0)
    def _():
        carry_ref[...] = jnp.zeros(carry_ref.shape, F32)

    x = f_ref[...] + b_ref[...]
    lf = jnp.minimum(x, 0.0) - jnp.log1p(jnp.exp(-jnp.abs(x)))
    row = lax.broadcasted_iota(I32, lf.shape, 0)
    s = 1
    while s < t:
        lf = lf + jnp.where(row >= s, pltpu.roll(lf, s, 0), 0.0)
        s *= 2
    lf = lf + carry_ref[...]
    carry_ref[...] = lf[t - 1:t, :]
    lf2 = lf * LOG2E
    ct_ref[...] = lf2.T[:FOX_HEADS, :]
    for h in range(FOX_HEADS):
        cr_ref[h] = jnp.broadcast_to(lf2[:, h:h + 1], (t, LANES))


def _decay(proj, f_cb, b_f, *, t):
    seq = proj.shape[0]
    b = jnp.pad(b_f.reshape(1, FOX_HEADS), ((0, 0), (0, LANES - FOX_HEADS)))
    return pl.pallas_call(
        functools.partial(_decay_kernel, t=t),
        grid=(seq // t,),
        in_specs=[pl.BlockSpec((t, LANES), lambda i: (i, f_cb)), pl.BlockSpec((1, LANES), lambda i: (0, 0))],
        out_specs=[pl.BlockSpec((FOX_HEADS, t), lambda i: (0, i)),
                   pl.BlockSpec((FOX_HEADS, t, LANES), lambda i: (0, i, 0))],
        out_shape=[jax.ShapeDtypeStruct((FOX_HEADS, seq), F32), jax.ShapeDtypeStruct((FOX_HEADS, seq, LANES), F32)],
        scratch_shapes=[pltpu.VMEM((1, LANES), F32)],
        compiler_params=_cparams("arbitrary"),
        name="fox_decay",
    )(proj, b)


def _mem_attn_kernel(q_ref, k_ref, v_ref, g_ref, o_ref):
    k = k_ref[...].astype(BF16)
    s = lax.dot_general(q_ref[...], k, (((1,), (1,)), ((), ())), preferred_element_type=F32) * (MEM_HEAD_DIM ** -0.5)
    m = jnp.max(s, axis=1, keepdims=True)
    p = jnp.exp(s - m)
    l = jnp.sum(p, axis=1, keepdims=True)
    o = jnp.dot(p.astype(BF16), v_ref[...].astype(BF16), preferred_element_type=F32) / l
    o_ref[...] = (o * _silu(g_ref[...])).astype(o_ref.dtype)


def _mem_kv_all(mem2d, g, w_all):
    depth, k, n = w_all.shape
    m = mem2d.shape[0]
    tn = 512
    return pl.pallas_call(
        _norm_matmul_kernel,
        grid=(depth, n // tn),
        in_specs=[pl.BlockSpec((m, k), lambda l, j: (0, 0)),
                  pl.BlockSpec((1, k), lambda l, j: (0, 0)),
                  pl.BlockSpec((None, k, tn), lambda l, j: (l, 0, j))],
        out_specs=pl.BlockSpec((None, m, tn), lambda l, j: (l, 0, j)),
        out_shape=jax.ShapeDtypeStruct((depth, m, n), F32),
        scratch_shapes=[pltpu.VMEM((m, k), BF16)],
        compiler_params=_cparams("arbitrary", "arbitrary"),
        name="mem_kv",
    )(mem2d, g.reshape(1, k), w_all)


def _mem_attn(proj_b, q_cb, proj_f, g_cb, mem_kv, *, t):
    seq = proj_b.shape[0]
    mem_kv, layer = mem_kv
    nm = mem_kv.shape[1]
    d = MEM_HEAD_DIM
    return pl.pallas_call(
        _mem_attn_kernel,
        grid=(MEM_HEADS, seq // t),
        in_specs=[pl.BlockSpec((t, d), lambda h, i: (i, q_cb + h)),
                  pl.BlockSpec((None, nm, d), lambda h, i: (layer, 0, h)),
                  pl.BlockSpec((None, nm, d), lambda h, i: (layer, 0, MEM_HEADS + h)),
                  pl.BlockSpec((t, d), lambda h, i: (i, g_cb + h))],
        out_specs=pl.BlockSpec((t, d), lambda h, i: (i, h)),
        out_shape=jax.ShapeDtypeStruct((seq, MEM_WIDTH), BF16),
        compiler_params=_cparams("parallel", "arbitrary"),
        name="mem_attn",
    )(proj_b, mem_kv, mem_kv, proj_f)


S5_TILE_GROUPS = LANES // S5_GROUP
S5_TILE_STATES = S5_TILE_GROUPS * S5_STATE
S5_TILES = S5_GROUPS // S5_TILE_GROUPS


def _s5_scan_kernel(u_ref, b_ref, c_ref, d_ref, tab_ref, z_ref, bu_ref, carry_ref, *, tc):
    ns = S5_TILE_STATES

    @pl.when(pl.program_id(1) == 0)
    def _():
        carry_ref[...] = jnp.zeros(carry_ref.shape, F32)

    u = u_ref[...]
    bu_ref[...] = jnp.dot(u.astype(BF16), b_ref[...], preferred_element_type=F32)
    steps = [(1, tab_ref[0], tab_ref[1]), (2, tab_ref[2], tab_ref[3]), (4, tab_ref[4], tab_ref[5])]
    pr = tab_ref[6]
    pi = tab_ref[7]

    def body(i, carry):
        cr, ci = carry
        r0 = pl.multiple_of(i * SUBLANES, SUBLANES)
        xr = bu_ref[pl.ds(r0, SUBLANES), 0:ns]
        xi = bu_ref[pl.ds(r0, SUBLANES), ns:2 * ns]
        for s, ar, ai in steps:
            sr = pltpu.roll(xr, s, 0)
            si = pltpu.roll(xi, s, 0)
            xr, xi = xr + ar * sr - ai * si, xi + ar * si + ai * sr
        xr, xi = xr + pr * cr - pi * ci, xi + pr * ci + pi * cr
        bu_ref[pl.ds(r0, SUBLANES), 0:ns] = xr
        bu_ref[pl.ds(r0, SUBLANES), ns:2 * ns] = xi
        return xr[SUBLANES - 1:SUBLANES, :], xi[SUBLANES - 1:SUBLANES, :]

    cr, ci = lax.fori_loop(0, tc // SUBLANES, body, (carry_ref[0:1, 0:ns], carry_ref[0:1, ns:2 * ns]))
    carry_ref[0:1, 0:ns] = cr
    carry_ref[0:1, ns:2 * ns] = ci
    y = jnp.dot(bu_ref[...].astype(BF16), c_ref[...], preferred_element_type=F32) + d_ref[...] * u
    z_ref[...] = jax.nn.gelu(y)


def _s5_prepare(lam_re, lam_im, log_dt, b_re, b_im, c_re, c_im):
    dt = jnp.exp(log_dt.astype(F32))[:, None]
    lr = lam_re.astype(F32)
    li = lam_im.astype(F32)
    mag = jnp.exp(lr * dt)
    ab_re = mag * jnp.cos(li * dt)
    ab_im = mag * jnp.sin(li * dt)
    den = lr * lr + li * li
    nr = ab_re - 1.0
    f_re = (nr * lr + ab_im * li) / den
    f_im = (ab_im * lr - nr * li) / den
    br = b_re.astype(F32)
    bim = b_im.astype(F32)
    bb_re = f_re[..., None] * br - f_im[..., None] * bim
    bb_im = f_re[..., None] * bim + f_im[..., None] * br
    eye = jnp.eye(S5_TILE_GROUPS, dtype=F32)

    def blockdiag_in(bb):
        t = bb.reshape(S5_TILES, S5_TILE_GROUPS, S5_STATE, S5_GROUP)
        m = jnp.einsum("jgpc,gh->jgchp", t, eye)
        return m.reshape(S5_TILES, LANES, S5_TILE_STATES)

    def blockdiag_out(cc):
        t = cc.reshape(S5_TILES, S5_TILE_GROUPS, S5_GROUP, S5_STATE)
        m = jnp.einsum("jgcp,gh->jgphc", t, eye)
        return m.reshape(S5_TILES, S5_TILE_STATES, LANES)

    b_cat = jnp.concatenate([blockdiag_in(bb_re), blockdiag_in(bb_im)], axis=2).astype(BF16)
    c_cat = jnp.concatenate([blockdiag_out(c_re.astype(F32)), -blockdiag_out(c_im.astype(F32))], axis=1).astype(BF16)

    a_r = ab_re.reshape(S5_TILES, 1, S5_TILE_STATES)
    a_i = ab_im.reshape(S5_TILES, 1, S5_TILE_STATES)

    def cmul(xr, xi, yr, yi):
        return xr * yr - xi * yi, xr * yi + xi * yr

    a2 = cmul(a_r, a_i, a_r, a_i)
    a4 = cmul(*a2, *a2)
    row = jnp.arange(SUBLANES)[None, :, None]
    tabs = []
    for s, (pr_, pi_) in ((1, (a_r, a_i)), (2, a2), (4, a4)):
        tabs.append(jnp.where(row >= s, pr_, 0.0))
        tabs.append(jnp.where(row >= s, pi_, 0.0))
    pw = [(a_r, a_i)]
    for _ in range(SUBLANES - 1):
        pw.append(cmul(*pw[-1], a_r, a_i))
    tabs.append(jnp.concatenate([p[0] for p in pw], axis=1))
    tabs.append(jnp.concatenate([p[1] for p in pw], axis=1))
    tab = jnp.stack([jnp.broadcast_to(t, (S5_TILES, SUBLANES, S5_TILE_STATES)) for t in tabs], axis=1)
    return b_cat, c_cat, tab.astype(F32)


def _s5_scan(proj, u_cb, b_cat, c_cat, d_skip, tab, *, tc):
    seq = proj.shape[0]
    ns = S5_TILE_STATES
    d = d_skip.astype(F32).reshape(S5_TILES, 1, LANES)
    return pl.pallas_call(
        functools.partial(_s5_scan_kernel, tc=tc),
        grid=(S5_TILES, seq // tc),
        in_specs=[pl.BlockSpec((tc, LANES), lambda j, c: (c, u_cb + j)),
                  pl.BlockSpec((None, LANES, 2 * ns), lambda j, c: (j, 0, 0)),
                  pl.BlockSpec((None, 2 * ns, LANES), lambda j, c: (j, 0, 0)),
                  pl.BlockSpec((None, 1, LANES), lambda j, c: (j, 0, 0)),
                  pl.BlockSpec((None, 8, SUBLANES, ns), lambda j, c: (j, 0, 0, 0))],
        out_specs=pl.BlockSpec((tc, LANES), lambda j, c: (c, j)),
        out_shape=jax.ShapeDtypeStruct((seq, S5_WIDTH), F32),
        scratch_shapes=[pltpu.VMEM((tc, 2 * ns), F32), pltpu.VMEM((SUBLANES, 2 * ns), F32)],
        compiler_params=_cparams("parallel", "arbitrary"),
        name="s5_scan",
    )(proj, b_cat, c_cat, d, tab)


def _s5_glu_kernel(z_ref, w_ref, g_ref, o_ref, *, tn):
    j = pl.program_id(1)
    z = z_ref[...]
    a = jnp.dot(z.astype(BF16), w_ref[...], preferred_element_type=F32)
    zc = z_ref[:, pl.ds(pl.multiple_of(j * tn, tn), tn)]
    o_ref[...] = (zc * jax.nn.sigmoid(a) * _silu(g_ref[...])).astype(o_ref.dtype)


def _s5_glu(z, w_glu, proj, g_cb, *, tm, tn):
    seq, n = z.shape
    return pl.pallas_call(
        functools.partial(_s5_glu_kernel, tn=tn),
        grid=(seq // tm, n // tn),
        in_specs=[pl.BlockSpec((tm, n), lambda i, j: (i, 0)),
                  pl.BlockSpec((n, tn), lambda i, j: (0, j)),
                  pl.BlockSpec((tm, tn), lambda i, j: (i, g_cb * (n // tn) + j))],
        out_specs=pl.BlockSpec((tm, tn), lambda i, j: (i, j)),
        out_shape=jax.ShapeDtypeStruct((seq, n), BF16),
        compiler_params=_cparams("parallel", "arbitrary"),
        name="s5_glu",
    )(z, w_glu, proj)


def _rope_tables(pos, freq):
    ang = pos * freq
    lane = lax.broadcasted_iota(I32, ang.shape, 1)
    half = MLA_ROPE // 2
    cos = jnp.cos(ang)
    sin = jnp.sin(ang)
    c = jnp.where(lane < MLA_ROPE, cos, 0.0)
    s1 = jnp.where(lane < half, -sin, 0.0)
    s2 = jnp.where((lane >= half) & (lane < MLA_ROPE), sin, 0.0)
    return c, s1, s2


def _rope_apply(x, c, s1, s2):
    half = MLA_ROPE // 2
    return x * c + pltpu.roll(x, LANES - half, 1) * s1 + pltpu.roll(x, half, 1) * s2


def _mla_prep_kernel(qf_ref, kvf_ref, kr_ref, pos_ref, freq_ref, q_ref, k_ref, v_ref):
    c, s1, s2 = _rope_tables(pos_ref[...], freq_ref[...])
    kr = _rope_apply(kr_ref[...], c, s1, s2).astype(BF16)
    for h in range(MLA_HEADS):
        b = 2 * LANES * h
        q_ref[:, b:b + LANES] = qf_ref[:, b:b + LANES].astype(BF16)
        q_ref[:, b + LANES:b + 2 * LANES] = _rope_apply(qf_ref[:, b + LANES:b + 2 * LANES], c, s1, s2).astype(BF16)
        k_ref[:, b:b + LANES] = kvf_ref[:, b:b + LANES].astype(BF16)
        k_ref[:, b + LANES:b + 2 * LANES] = kr
        v_ref[:, LANES * h:LANES * (h + 1)] = kvf_ref[:, b + LANES:b + 2 * LANES].astype(BF16)


def _mla_prep(qf, kvf, proj, kr_cb, pos_col, freq, *, t):
    seq = qf.shape[0]
    w = 2 * LANES * MLA_HEADS
    return pl.pallas_call(
        _mla_prep_kernel,
        grid=(seq // t,),
        in_specs=[pl.BlockSpec((t, w), lambda i: (i, 0)),
                  pl.BlockSpec((t, w), lambda i: (i, 0)),
                  pl.BlockSpec((t, LANES), lambda i: (i, kr_cb)),
                  pl.BlockSpec((t, 1), lambda i: (i, 0)),
                  pl.BlockSpec((1, LANES), lambda i: (0, 0))],
        out_specs=[pl.BlockSpec((t, w), lambda i: (i, 0)),
                   pl.BlockSpec((t, w), lambda i: (i, 0)),
                   pl.BlockSpec((t, MLA_WIDTH), lambda i: (i, 0))],
        out_shape=[jax.ShapeDtypeStruct((seq, w), BF16), jax.ShapeDtypeStruct((seq, w), BF16),
                   jax.ShapeDtypeStruct((seq, MLA_WIDTH), BF16)],
        compiler_params=_cparams("parallel"),
        name="mla_prep",
    )(qf, kvf, proj, pos_col, freq)


def _t5_bucket(dist):
    n = jnp.maximum(dist, 0)
    max_exact = T5_BUCKETS // 2
    log_ratio = jnp.log(jnp.maximum(n, 1).astype(F32) / max_exact) / math.log(T5_MAX_DIST / max_exact)
    large = jnp.minimum(max_exact + (log_ratio * (T5_BUCKETS - max_exact)).astype(I32), T5_BUCKETS - 1)
    return jnp.where(n < max_exact, n, large)


T5_MASK_BUCKET = T5_BUCKETS


def _t5_lookup(table_row, bucket):
    rows, width = bucket.shape
    tab = jnp.broadcast_to(table_row, (rows, LANES))
    parts = [jnp.take_along_axis(tab, bucket[:, c:c + LANES], axis=1, mode="promise_in_bounds")
             for c in range(0, width, LANES)]
    return parts[0] if len(parts) == 1 else jnp.concatenate(parts, axis=1)


def _nsa_cmp_kernel(x_ref, pe_ref, w1_ref, w2_ref, o_ref, xf_ref, *, nc):
    half = NSA_CMP_LEN // 2
    d = NSA_HEAD_DIM
    xf_ref[...] = x_ref[...].astype(F32)
    u = jnp.zeros((nc, NSA_CMP_HIDDEN), F32)
    v = jnp.zeros((nc, NSA_CMP_HIDDEN), F32)
    for r in range(half):
        a = xf_ref[pl.ds(r, nc, stride=NSA_CMP_STRIDE), :]
        u = u + jnp.dot((a + pe_ref[r:r + 1, :]).astype(BF16), w1_ref[r * d:(r + 1) * d, :],
                        preferred_element_type=F32)
        v = v + jnp.dot((a + pe_ref[half + r:half + r + 1, :]).astype(BF16),
                        w1_ref[(half + r) * d:(half + r + 1) * d, :], preferred_element_type=F32)
    hid = u + pltpu.roll(v, nc - 1, 0)
    o_ref[...] = jnp.dot(jax.nn.gelu(hid).astype(BF16), w2_ref[...], preferred_element_type=F32).astype(o_ref.dtype)


def _nsa_compress(proj, k_cb, pe, w1, w2):
    seq = proj.shape[0]
    nc = seq // NSA_CMP_STRIDE
    d = NSA_HEAD_DIM
    g = NSA_KV_GROUPS
    return pl.pallas_call(
        functools.partial(_nsa_cmp_kernel, nc=nc),
        grid=(2, g),
        in_specs=[pl.BlockSpec((seq, d), lambda a, b: (0, k_cb + a * g + b)),
                  pl.BlockSpec((None, NSA_CMP_LEN, d), lambda a, b: (a, 0, 0)),
                  pl.BlockSpec((None, NSA_CMP_LEN * d, NSA_CMP_HIDDEN), lambda a, b: (a, 0, 0)),
                  pl.BlockSpec((None, NSA_CMP_HIDDEN, d), lambda a, b: (a, 0, 0))],
        out_specs=pl.BlockSpec((None, None, nc, d), lambda a, b: (a, b, 0, 0)),
        out_shape=jax.ShapeDtypeStruct((2, g, nc, d), BF16),
        scratch_shapes=[pltpu.VMEM((seq, d), F32)],
        compiler_params=_cparams("parallel", "arbitrary"),
        name="nsa_compress",
    )(proj, pe, w1, w2)


def _nsa_select_kernel(q_ref, kc_ref, vc_ref, posq_ref, posc_ref, tab_ref, gate_ref, ov_ref,
                       oc_ref, sel_ref, *, nc, n_slc, tq):
    qi = pl.program_id(0)
    d = NSA_HEAD_DIM
    c = d ** -0.5 * LOG2E
    tok = qi * tq + lax.broadcasted_iota(I32, (1, tq), 1)
    cmp_end = lax.broadcasted_iota(I32, (nc, 1), 0) * NSA_CMP_STRIDE + (NSA_CMP_LEN - 1)
    valid = cmp_end <= tok
    pos_c = jnp.concatenate([posc_ref[...]] * (tq // LANES), axis=1)
    bucket = jnp.where(valid, _t5_bucket(posq_ref[...] - pos_c), T5_MASK_BUCKET)
    gates = jax.nn.sigmoid(gate_ref[...])
    ovt = ov_ref[...]
    js = lax.broadcasted_iota(I32, (LANES, tq), 0)
    jf = js.astype(F32)
    cur = tok // NSA_SLC_BLOCK
    forced = (js == 0) | (js == cur) | (js == cur - 1)
    for g in range(NSA_KV_GROUPS):
        kc = kc_ref[g]
        vct = vc_ref[g].astype(F32).T.astype(BF16)
        psum = jnp.zeros((nc, tq), F32)
        for r in range(NSA_REP):
            h = g * NSA_REP + r
            qt = (q_ref[:, h * d:(h + 1) * d].astype(F32) * c).T.astype(BF16)
            t = jnp.dot(kc, qt, preferred_element_type=F32) + _t5_lookup(tab_ref[h:h + 1, :], bucket)
            m = jnp.max(t, axis=0, keepdims=True)
            e = jnp.exp2(t - m)
            l = jnp.sum(e, axis=0, keepdims=True)
            p = e * jnp.where(m > 0.5 * NEG_INF, 1.0 / l, 0.0)
            o = jnp.dot(vct, p.astype(BF16), preferred_element_type=F32)
            oc_ref[:, h * d:(h + 1) * d] = gates[:, 3 * h:3 * h + 1] * o.T
            psum = psum + p
        p_hi = psum.astype(BF16)
        p_lo = (psum - p_hi.astype(F32)).astype(BF16)
        imp = jnp.dot(ovt, p_hi, preferred_element_type=F32) + jnp.dot(ovt, p_lo, preferred_element_type=F32)
        st = jnp.where(forced, FORCE_SCORE, jnp.where(js > cur, -1.0, imp))
        st = jnp.where(js < n_slc, st, -2.0)
        sel = jnp.zeros((LANES, tq), F32)
        for _ in range(NSA_SLC_TOPK):
            mx = jnp.max(st, axis=0, keepdims=True)
            first = jnp.min(jnp.where(st == mx, jf, float(LANES)), axis=0, keepdims=True)
            hit = jf == first
            sel = jnp.where(hit, 1.0, sel)
            st = jnp.where(hit, -3e38, st)
        sel_ref[g] = sel.astype(sel_ref.dtype)


def _nsa_select(proj_b, q_cb, proj_f, gate_cb, kc, vc, pos_row, pos_cmp_rep, tab_t, ov_t, *, n_slc, tq):
    seq = proj_b.shape[0]
    nc = kc.shape[1]
    g = NSA_KV_GROUPS
    d = NSA_HEAD_DIM
    return pl.pallas_call(
        functools.partial(_nsa_select_kernel, nc=nc, n_slc=n_slc, tq=tq),
        grid=(seq // tq,),
        in_specs=[pl.BlockSpec((tq, NSA_WIDTH), lambda i: (i, q_cb)),
                  pl.BlockSpec((g, nc, d), lambda i: (0, 0, 0)),
                  pl.BlockSpec((g, nc, d), lambda i: (0, 0, 0)),
                  pl.BlockSpec((1, tq), lambda i: (0, i)),
                  pl.BlockSpec((nc, LANES), lambda i: (0, 0)),
                  pl.BlockSpec((SUBLANES, LANES), lambda i: (0, 0)),
                  pl.BlockSpec((tq, LANES), lambda i: (i, gate_cb)),
                  pl.BlockSpec((LANES, nc), lambda i: (0, 0))],
        out_specs=[pl.BlockSpec((tq, NSA_WIDTH), lambda i: (i, 0)),
                   pl.BlockSpec((g, LANES, tq), lambda i: (0, 0, i))],
        out_shape=[jax.ShapeDtypeStruct((seq, NSA_WIDTH), F32),
                   jax.ShapeDtypeStruct((g, LANES, seq), BF16)],
        compiler_params=_cparams("parallel"),
        name="nsa_select",
    )(proj_b, kc, vc, pos_row, pos_cmp_rep, tab_t * LOG2E, proj_f, ov_t)


def _nsa_slc_kernel(pqmin_ref, pkmax_ref, q_ref, k_ref, v_ref, sel_ref, posq_ref, posk_ref, tab_ref, o_ref,
                    vt_ref, qt_ref, s_ref, m_ref, acc_ref, *, tq, tk):
    g = pl.program_id(0)
    qi = pl.program_id(1)
    d = NSA_HEAD_DIM
    scale = d ** -0.5
    seq = k_ref.shape[0]

    @pl.when(qi == 0)
    def _():
        _build_vt(v_ref, vt_ref, 0, d, seq, tk)

    for r in range(NSA_REP):
        qt_ref[r] = (q_ref[:, r * d:(r + 1) * d].astype(F32) * (scale * LOG2E)).T.astype(BF16)
    m_ref[...] = jnp.full(m_ref.shape, NEG_INF, F32)
    acc_ref[...] = jnp.zeros(acc_ref.shape, F32)
    sel = sel_ref[...]
    pos_q = posq_ref[...]
    key_blk = lax.broadcasted_iota(I32, (tk, LANES), 0) // NSA_SLC_BLOCK
    blk = lax.broadcasted_iota(I32, (tk, LANES), 1)
    per_tile = tk // NSA_SLC_BLOCK
    n_full = (qi * tq) // tk

    def scores(j, slot):
        start = pl.multiple_of(j * tk, tk)
        k = k_ref[pl.ds(start, tk), :]
        for r in range(NSA_REP):
            s_ref[slot, r] = jnp.dot(k, qt_ref[r], preferred_element_type=F32)

    def softmax_pv(j, slot, masked, near):
        start = pl.multiple_of(j * tk, tk)
        expand = jnp.where(blk == key_blk + j * per_tile, 1.0, 0.0).astype(BF16)
        picked = jnp.dot(expand, sel, preferred_element_type=F32)
        if masked:
            key = start + lax.broadcasted_iota(I32, (tk, tq), 0)
            qry = qi * tq + lax.broadcasted_iota(I32, (tk, tq), 1)
            picked = jnp.where(key <= qry, picked, 0.0)
        mask = picked > 0.5
        if near:
            pos_k = jnp.concatenate([posk_ref[pl.ds(start, tk), :]] * (tq // LANES), axis=1)
            bucket = jnp.where(mask, _t5_bucket(pos_q - pos_k), T5_MASK_BUCKET)
        for r in range(NSA_REP):
            h = g * NSA_REP + r
            t = s_ref[slot, r]
            if near:
                t = t + _t5_lookup(tab_ref[pl.ds(h, 1), :], bucket)
            else:
                t = jnp.where(mask, t, NEG_INF)
            m_prev = m_ref[r]
            mx = jnp.max(t, axis=0, keepdims=True)
            if near:
                m_new = jnp.maximum(m_prev, mx)
                shift = m_new
            else:
                b = tab_ref[pl.ds(h, 1), T5_BUCKETS - 1:T5_BUCKETS]
                m_new = jnp.maximum(m_prev, mx + b)
                shift = m_new - b
            alpha = jnp.exp2(m_prev - m_new)
            p = jnp.exp2(t - shift).astype(BF16)
            acc_ref[r] = alpha * acc_ref[r] + jnp.dot(vt_ref[0, :, pl.ds(start, tk)], p, preferred_element_type=F32)
            m_ref[r] = m_new

    def is_far(j):
        return pqmin_ref[qi] - pkmax_ref[j] >= T5_MAX_DIST

    def stages(j, count, near):
        for k in range(count):
            scores(j + k + 1, (k + 1) % 2)
            softmax_pv(j + k, k % 2, False, near)

    scores(0, 0)

    def pair(jj, carry):
        j = 2 * jj
        far = jnp.logical_and(is_far(j), is_far(j + 1))

        @pl.when(far)
        def _():
            stages(j, 2, False)

        @pl.when(jnp.logical_not(far))
        def _():
            stages(j, 2, True)

        return carry

    pairs = n_full // 2
    lax.fori_loop(0, pairs, pair, 0)
    last = 2 * pairs

    @pl.when(n_full % 2 == 1)
    def _():
        far = is_far(last)

        @pl.when(far)
        def _():
            stages(last, 1, False)

        @pl.when(jnp.logical_not(far))
        def _():
            stages(last, 1, True)

        softmax_pv(last + 1, 1, True, True)

    @pl.when(n_full % 2 == 0)
    def _():
        softmax_pv(last, 0, True, True)

    for r in range(NSA_REP):
        a = acc_ref[r]
        o_ref[:, r * d:(r + 1) * d] = (a[0:d, :] / a[d:d + 1, :]).T


def _nsa_slc(proj_b, q_off, k_off, v_off, sel, pos_row, pos_rep, tab_t, pq_min, pk_max, *, tq, tk):
    seq = proj_b.shape[0]
    tab_t = tab_t * LOG2E
    d = NSA_HEAD_DIM
    gw = NSA_REP * d
    once = pl.Buffered(1)
    grid_spec = pltpu.PrefetchScalarGridSpec(
        num_scalar_prefetch=2,
        grid=(NSA_KV_GROUPS, seq // tq),
        in_specs=[pl.BlockSpec((tq, gw), lambda g, i, *_: (i, q_off // gw + g)),
                  pl.BlockSpec((seq, d), lambda g, i, *_: (0, k_off // d + g), pipeline_mode=once),
                  pl.BlockSpec((seq, d), lambda g, i, *_: (0, v_off // d + g), pipeline_mode=once),
                  pl.BlockSpec((None, LANES, tq), lambda g, i, *_: (g, 0, i)),
                  pl.BlockSpec((1, tq), lambda g, i, *_: (0, i)),
                  pl.BlockSpec((seq, LANES), lambda g, i, *_: (0, 0), pipeline_mode=once),
                  pl.BlockSpec((SUBLANES, LANES), lambda g, i, *_: (0, 0))],
        out_specs=pl.BlockSpec((tq, gw), lambda g, i, *_: (i, g)),
        scratch_shapes=[pltpu.VMEM((1, d + ONES_ROWS, seq), BF16), pltpu.VMEM((NSA_REP, d, tq), BF16),
                        pltpu.VMEM((2, NSA_REP, tk, tq), F32), pltpu.VMEM((NSA_REP, 1, tq), F32),
                        pltpu.VMEM((NSA_REP, d + ONES_ROWS, tq), F32)])
    return pl.pallas_call(
        functools.partial(_nsa_slc_kernel, tq=tq, tk=tk),
        grid_spec=grid_spec,
        out_shape=jax.ShapeDtypeStruct((seq, NSA_WIDTH), F32),
        compiler_params=_cparams("parallel", "arbitrary"),
        name="nsa_slc",
    )(pq_min, pk_max, proj_b, proj_b, proj_b, sel, pos_row, pos_rep, tab_t)


def _nsa_win_kernel(*refs, tq, nt):
    q_ref = refs[0]
    k_refs = refs[1:1 + nt]
    v_refs = refs[1 + nt:1 + 2 * nt]
    pk_refs = refs[1 + 2 * nt:1 + 3 * nt]
    posq_ref, tab_ref, gate_ref, gout_ref, oc_ref, os_ref, o_ref = refs[1 + 3 * nt:]
    qi = pl.program_id(0)
    d = NSA_HEAD_DIM
    c = d ** -0.5 * LOG2E
    tok = qi * tq + lax.broadcasted_iota(I32, (1, tq), 1)
    lower = jnp.maximum(tok - (NSA_WINDOW - 1), 0)
    sub = lax.broadcasted_iota(I32, (tq, 1), 0)
    pos_q = posq_ref[...]
    buckets = []
    for jj in range(nt):
        kidx = (qi - (nt - 1) + jj) * tq + sub
        pos_k = jnp.concatenate([pk_refs[jj][...]] * (tq // LANES), axis=1)
        b = jnp.where(kidx >= lower, _t5_bucket(pos_q - pos_k), T5_MASK_BUCKET)
        buckets.append(jnp.where(kidx <= tok, b, T5_MASK_BUCKET))
    gates = jax.nn.sigmoid(gate_ref[...])
    for g in range(NSA_KV_GROUPS):
        ks = [kr[:, g * d:(g + 1) * d] for kr in k_refs]
        vts = [vr[:, g * d:(g + 1) * d].astype(F32).T.astype(BF16) for vr in v_refs]
        for r in range(NSA_REP):
            h = g * NSA_REP + r
            hs = slice(h * d, (h + 1) * d)
            qt = (q_ref[:, hs].astype(F32) * c).T.astype(BF16)
            ts = [jnp.dot(ks[jj], qt, preferred_element_type=F32) + _t5_lookup(tab_ref[h:h + 1, :], buckets[jj])
                  for jj in range(nt)]
            m = functools.reduce(jnp.maximum, [jnp.max(t, axis=0, keepdims=True) for t in ts])
            ps = [jnp.exp2(t - m) for t in ts]
            l = functools.reduce(jnp.add, [jnp.sum(p, axis=0, keepdims=True) for p in ps])
            o_t = functools.reduce(jnp.add, [jnp.dot(vt, p.astype(BF16), preferred_element_type=F32)
                                             for vt, p in zip(vts, ps)])
            o_w = (o_t / l).T
            o = oc_ref[:, hs] + gates[:, 3 * h + 1:3 * h + 2] * os_ref[:, hs] + gates[:, 3 * h + 2:3 * h + 3] * o_w
            o_ref[:, hs] = (o * _silu(gout_ref[:, hs])).astype(o_ref.dtype)


def _nsa_win(proj_b, q_cb, k_cb, v_cb, proj_f, gate_cb, gout_cb, oc, o_s, pos_row, pos_rep, tab_t, *, tq):
    seq = proj_b.shape[0]
    nt = NSA_WINDOW // tq + 1

    def band_rows(jj, cb):
        return pl.BlockSpec((tq, NSA_KV), lambda i: (jnp.maximum(i - (nt - 1) + jj, 0), cb))

    def band_pos(jj):
        return pl.BlockSpec((tq, LANES), lambda i: (jnp.maximum(i - (nt - 1) + jj, 0), 0))

    in_specs = [pl.BlockSpec((tq, NSA_WIDTH), lambda i: (i, q_cb))]
    in_specs += [band_rows(jj, k_cb) for jj in range(nt)]
    in_specs += [band_rows(jj, v_cb) for jj in range(nt)]
    in_specs += [band_pos(jj) for jj in range(nt)]
    in_specs += [pl.BlockSpec((1, tq), lambda i: (0, i)),
                 pl.BlockSpec((SUBLANES, LANES), lambda i: (0, 0)),
                 pl.BlockSpec((tq, LANES), lambda i: (i, gate_cb)),
                 pl.BlockSpec((tq, NSA_WIDTH), lambda i: (i, gout_cb)),
                 pl.BlockSpec((tq, NSA_WIDTH), lambda i: (i, 0)),
                 pl.BlockSpec((tq, NSA_WIDTH), lambda i: (i, 0))]
    args = [proj_b] * (1 + 2 * nt) + [pos_rep] * nt + [pos_row, tab_t * LOG2E, proj_f, proj_f, oc, o_s]
    return pl.pallas_call(
        functools.partial(_nsa_win_kernel, tq=tq, nt=nt),
        grid=(seq // tq,),
        in_specs=in_specs,
        out_specs=pl.BlockSpec((tq, NSA_WIDTH), lambda i: (i, 0)),
        out_shape=jax.ShapeDtypeStruct((seq, NSA_WIDTH), BF16),
        compiler_params=_cparams("parallel"),
        name="nsa_win",
    )(*args)


def _in_proj(h, norm_g, w_b, w_f, tiles, name):
    proj_b = _norm_matmul(h, norm_g, w_b, tm=tiles["tm"], tn=512, name=name + "_b", out_dtype=BF16)
    proj_f = _norm_matmul(h, norm_g, w_f, tm=tiles["tm"], tn=768, name=name + "_f")
    return proj_b, proj_f


def _even_layer(h, mem_kv, norm_g, w_in, s5, w_glu, b_f, tiles):
    pb, pf = _in_proj(h, norm_g, *w_in, tiles, "in_proj_even")
    ob, of = EVEN_B_OFF, EVEN_F_OFF
    b_cat, c_cat, tab, d_skip = s5
    z = _s5_scan(pf, of["u"] // LANES, b_cat, c_cat, d_skip, tab, tc=tiles["s5_tc"])
    y_s5 = _s5_glu(z, w_glu, pf, of["g_s5"] // S5_WIDTH, tm=tiles["tm"], tn=512)
    decay = _decay(pf, of["f"] // LANES, b_f, t=tiles["decay_t"])
    d = FOX_HEAD_DIM
    y_fox = _flash(pb, ob["q"], pb, ob["k"], pb, ob["v"], pf, of["g_fox"], heads=FOX_HEADS, dk=d, dv=d,
                   scale=d ** -0.5, tq=tiles["attn_tq"], tk=tiles["attn_tk"], hp=tiles["attn_hp"], decay=decay,
                   name="fox_attn")
    y_mem = _mem_attn(pb, ob["q_mem"] // MEM_HEAD_DIM, pf, of["g_mem"] // MEM_HEAD_DIM, mem_kv, t=tiles["mem_t"])
    return y_s5, y_fox, y_mem


def _odd_layer(h, mem_kv, norm_g, w_in, mla, nsa, pos, tiles):
    pb, pf = _in_proj(h, norm_g, *w_in, tiles, "in_proj_odd")
    ob, of = ODD_B_OFF, ODD_F_OFF
    g_cq, g_ckv, w_uq, w_ukv, freq = mla
    pos_col_f, pos_row, pos_cmp_rep, pos_rep, pq_min, pk_max = pos
    qf = _norm_matmul(pf, g_cq, w_uq, x_cb=of["c_q"] // MLA_Q_RANK, tm=tiles["tm"], tn=512, name="mla_q_up")
    kvf = _norm_matmul(pf, g_ckv, w_ukv, x_cb=of["c_kv"] // MLA_KV_RANK, tm=tiles["tm"], tn=512, name="mla_kv_up")
    q_r, k_r, v_r = _mla_prep(qf, kvf, pf, of["k_rope"] // LANES, pos_col_f, freq, t=tiles["prep_t"])
    y_mla = _flash(q_r, 0, k_r, 0, v_r, 0, pf, of["g_mla"], heads=MLA_HEADS, dk=2 * LANES, dv=MLA_V,
                   scale=(MLA_NOPE + MLA_ROPE) ** -0.5, tq=tiles["attn_tq"], tk=tiles["attn_tk"], hp=tiles["attn_hp"],
                   name="mla_attn")
    pe, w1, w2, tab_t, ov, n_slc = nsa
    kvc = _nsa_compress(pb, ob["k_cmp"] // NSA_HEAD_DIM, pe, w1, w2)
    oc, sel = _nsa_select(pb, ob["q_nsa"] // NSA_WIDTH, pf, of["gates"] // LANES, kvc[0], kvc[1], pos_row,
                          pos_cmp_rep, tab_t, ov, n_slc=n_slc, tq=tiles["nsa_tq"])
    o_s = _nsa_slc(pb, ob["q_nsa"], ob["k_slc"], ob["v_slc"], sel, pos_row, pos_rep, tab_t, pq_min, pk_max,
                   tq=tiles["slc_tq"], tk=tiles["slc_tk"])
    y_nsa = _nsa_win(pb, ob["q_nsa"] // NSA_WIDTH, ob["k_win"] // NSA_KV, ob["v_win"] // NSA_KV,
                     pf, of["gates"] // LANES, of["g_nsa"] // NSA_WIDTH, oc, o_s, pos_row, pos_rep, tab_t,
                     tq=tiles["win_tq"])
    y_mem = _mem_attn(pb, ob["q_mem"] // MEM_HEAD_DIM, pf, of["g_mem"] // MEM_HEAD_DIM, mem_kv, t=tiles["mem_t"])
    return y_mla, y_nsa, y_mem


def _tiles(seq):
    return {"tm": min(seq, 1024), "s5_tc": min(seq, 1024), "decay_t": min(seq, 512), "attn_tq": min(seq, 256),
            "attn_tk": min(seq, 512), "attn_hp": 2, "slc_tq": 256, "slc_tk": 512, "nsa_tq": 256, "win_tq": 128,
            "mem_t": min(seq, 512), "prep_t": min(seq, 256)}


def _context(positions, t5_table, seq, tiles):
    pos = positions[0]
    pos_col = pos.reshape(seq, 1)
    pos_row = pos.reshape(1, seq)
    pos_rep = jnp.broadcast_to(pos_col, (seq, LANES))
    pq_min = jnp.min(pos.reshape(seq // tiles["slc_tq"], tiles["slc_tq"]), axis=1)
    pk_max = jnp.max(pos.reshape(seq // tiles["slc_tk"], tiles["slc_tk"]), axis=1)
    nc = seq // NSA_CMP_STRIDE
    pos_cmp = jnp.pad(pos[NSA_CMP_LEN - 1::NSA_CMP_STRIDE], (0, 1))
    pos_cmp_rep = jnp.broadcast_to(pos_cmp.reshape(nc, 1), (nc, LANES))
    half = MLA_ROPE // 2
    inv_freq = ROPE_THETA ** (-jnp.arange(half, dtype=F32) / half)
    freq = jnp.concatenate([inv_freq, inv_freq, jnp.zeros((LANES - MLA_ROPE,), F32)]).reshape(1, LANES)
    tab_t = jnp.pad(t5_table.astype(F32).T, ((0, SUBLANES - NSA_HEADS), (0, LANES - T5_BUCKETS)))
    tab_t = tab_t.at[:, T5_MASK_BUCKET].set(NEG_INF)
    n_slc = seq // NSA_SLC_BLOCK
    cs = np.arange(nc) * NSA_CMP_STRIDE
    ss = np.arange(LANES) * NSA_SLC_BLOCK
    ov_np = np.clip(np.minimum(cs[:, None] + NSA_CMP_LEN, ss[None, :] + NSA_SLC_BLOCK)
                    - np.maximum(cs[:, None], ss[None, :]), 0, None) / NSA_CMP_LEN
    ov_np[nc - 1, :] = 0.0
    ov_np[:, n_slc:] = 0.0
    return {"pos": (pos_col.astype(F32), pos_row, pos_cmp_rep, pos_rep, pq_min, pk_max), "freq": freq,
            "tab_t": tab_t, "ov": jnp.asarray(ov_np.T, BF16), "n_slc": n_slc}


def _odd_params(i, mla_g_cq, mla_g_ckv, mla_w_uq, mla_w_ukv, nsa_cmp_pe, nsa_cmp_w1, nsa_cmp_w2, ctx):
    dq = MLA_NOPE + MLA_ROPE
    w_uq = mla_w_uq[i].reshape(MLA_Q_RANK, MLA_HEADS, dq)
    w_uq = jnp.pad(w_uq, ((0, 0), (0, 0), (0, 2 * LANES - dq))).reshape(MLA_Q_RANK, -1).astype(BF16)
    mla = (mla_g_cq[i], mla_g_ckv[i], w_uq, mla_w_ukv[i].astype(BF16), ctx["freq"])
    nsa = (nsa_cmp_pe[i].astype(F32), nsa_cmp_w1[i].astype(BF16), nsa_cmp_w2[i].astype(BF16), ctx["tab_t"],
           ctx["ov"], ctx["n_slc"])
    return mla, nsa


def kernel(x, mem, positions, norm_g, mem_norm_g, final_norm_g, t5_table, w_out, mem_w_kv, even_w_in, s5_lam_re,
           s5_lam_im, s5_log_dt, s5_b_re, s5_b_im, s5_c_re, s5_c_im, s5_d, s5_w_glu, fox_b_f, odd_w_in, mla_g_cq,
           mla_g_ckv, mla_w_uq, mla_w_ukv, nsa_cmp_pe, nsa_cmp_w1, nsa_cmp_w2):
    batch, seq, _ = x.shape
    assert batch == 1 and seq % 1024 == 0 and seq // NSA_SLC_BLOCK <= LANES
    depth = norm_g.shape[0]
    tiles = _tiles(seq)
    ctx = _context(positions, t5_table, seq, tiles)
    h = x[0]
    w_out_b = w_out.astype(BF16)
    mem_kv_all = _mem_kv_all(mem[0], mem_norm_g, mem_w_kv.astype(BF16))
    for layer in range(depth):
        i = layer // 2
        mem_kv = (mem_kv_all, layer)
        if layer % 2 == 0:
            w_in = (_reorder_w_in(even_w_in[i], EVEN_SPLITS, EVEN_B_ORDER),
                    _reorder_w_in(even_w_in[i], EVEN_SPLITS, EVEN_F_ORDER))
            b_cat, c_cat, tab = _s5_prepare(s5_lam_re[i], s5_lam_im[i], s5_log_dt[i], s5_b_re[i], s5_b_im[i],
                                            s5_c_re[i], s5_c_im[i])
            ys = _even_layer(h, mem_kv, norm_g[layer], w_in, (b_cat, c_cat, tab, s5_d[i]),
                             s5_w_glu[i].astype(BF16), fox_b_f[i], tiles)
        else:
            w_in = (_reorder_w_in(odd_w_in[i], ODD_SPLITS, ODD_B_ORDER),
                    _reorder_w_in(odd_w_in[i], ODD_SPLITS, ODD_F_ORDER))
            mla, nsa = _odd_params(i, mla_g_cq, mla_g_ckv, mla_w_uq, mla_w_ukv, nsa_cmp_pe, nsa_cmp_w1, nsa_cmp_w2,
                                   ctx)
            ys = _odd_layer(h, mem_kv, norm_g[layer], w_in, mla, nsa, ctx["pos"], tiles)
        h = _out_proj(h, ys, w_out_b, layer, tm=tiles["tm"], tn=512)
    return _final_norm(h, final_norm_g, tm=tiles["mem_t"])[None]
```

```python
import functools
import math

import numpy as np
import jax
import jax.numpy as jnp
from jax import lax
from jax.experimental import pallas as pl
from jax.experimental.pallas import tpu as pltpu

F32 = jnp.float32
BF16 = jnp.bfloat16
I32 = jnp.int32

D_MODEL = 2048
DEPTH = 4
N_MEM = 256
RMS_EPS = 1e-6
NEG_INF = -1e30
LOG2E = math.log2(math.e)

S5_WIDTH = 1024
S5_GROUP = 16
S5_GROUPS = S5_WIDTH // S5_GROUP
S5_STATE = 64
FOX_HEADS = 8
FOX_HEAD_DIM = 128
FOX_WIDTH = FOX_HEADS * FOX_HEAD_DIM
MEM_HEADS = 4
MEM_HEAD_DIM = 128
MEM_WIDTH = MEM_HEADS * MEM_HEAD_DIM
MLA_HEADS = 8
MLA_Q_RANK = 512
MLA_KV_RANK = 512
MLA_NOPE = 128
MLA_ROPE = 64
MLA_V = 128
MLA_WIDTH = MLA_HEADS * MLA_V
ROPE_THETA = 10000.0
NSA_HEADS = 8
NSA_KV_GROUPS = 2
NSA_REP = NSA_HEADS // NSA_KV_GROUPS
NSA_HEAD_DIM = 128
NSA_WIDTH = NSA_HEADS * NSA_HEAD_DIM
NSA_KV = NSA_KV_GROUPS * NSA_HEAD_DIM
NSA_CMP_LEN = 32
NSA_CMP_STRIDE = 16
NSA_CMP_HIDDEN = 256
NSA_SLC_BLOCK = 64
NSA_SLC_TOPK = 16
NSA_WINDOW = 512
FORCE_SCORE = 1e6
T5_BUCKETS = 32
T5_MAX_DIST = 1024

EVEN_SPLITS = (S5_WIDTH, S5_WIDTH, FOX_WIDTH, FOX_WIDTH, FOX_WIDTH, FOX_HEADS, FOX_WIDTH, MEM_WIDTH, MEM_WIDTH)
ODD_SPLITS = (MLA_Q_RANK, MLA_KV_RANK, MLA_ROPE, MLA_WIDTH, NSA_WIDTH, NSA_KV, NSA_KV, NSA_KV, NSA_KV, NSA_KV,
              NSA_KV, 3 * NSA_HEADS, NSA_WIDTH, MEM_WIDTH, MEM_WIDTH)

LANES = 128
SUBLANES = 8
VMEM_LIMIT_BYTES = 56 * 1024 * 1024

EVEN_B_ORDER = (("q", 2, 1024), ("k", 3, 1024), ("v", 4, 1024), ("q_mem", 7, 512))
EVEN_F_ORDER = (("u", 0, 1024), ("g_s5", 1, 1024), ("g_fox", 6, 1024), ("g_mem", 8, 512), ("f", 5, 128),
                ("pad", None, 128))
ODD_B_ORDER = (("q_nsa", 4, 1024), ("q_mem", 13, 512), ("k_cmp", 5, 256), ("v_cmp", 6, 256), ("k_slc", 7, 256),
               ("v_slc", 8, 256), ("k_win", 9, 256), ("v_win", 10, 256))
ODD_F_ORDER = (("g_nsa", 12, 1024), ("g_mla", 3, 1024), ("c_q", 0, 512), ("c_kv", 1, 512), ("g_mem", 14, 512),
               ("k_rope", 2, 128), ("gates", 11, 128))


def _layout(order):
    off, out = 0, {}
    for name, _, width in order:
        assert off % width == 0
        out[name] = off
        off += width
    return out, off


EVEN_B_OFF, EVEN_B_N = _layout(EVEN_B_ORDER)
EVEN_F_OFF, EVEN_F_N = _layout(EVEN_F_ORDER)
ODD_B_OFF, ODD_B_N = _layout(ODD_B_ORDER)
ODD_F_OFF, ODD_F_N = _layout(ODD_F_ORDER)


def _reorder_w_in(w, splits, order):
    starts = np.concatenate([[0], np.cumsum(splits)])
    cols = []
    for _, idx, width in order:
        if idx is None:
            cols.append(jnp.zeros((w.shape[0], width), w.dtype))
            continue
        seg = w[:, int(starts[idx]):int(starts[idx + 1])]
        pad = width - seg.shape[1]
        if pad:
            seg = jnp.pad(seg, ((0, 0), (0, pad)))
        cols.append(seg)
    return jnp.concatenate(cols, axis=1).astype(BF16)


def _cparams(*sem):
    return pltpu.CompilerParams(dimension_semantics=sem, vmem_limit_bytes=VMEM_LIMIT_BYTES)


def _silu(g):
    return g * jax.nn.sigmoid(g)


def _pick(n, cands):
    for c in cands:
        if n % c == 0:
            return c
    raise ValueError(f"no tile for {n} in {cands}")


def _norm_matmul_kernel(x_ref, g_ref, w_ref, o_ref, xn_ref):
    @pl.when(pl.program_id(1) == 0)
    def _():
        x = x_ref[...]
        ms = jnp.mean(x * x, axis=-1, keepdims=True)
        xn_ref[...] = (x * lax.rsqrt(ms + RMS_EPS) * g_ref[...]).astype(BF16)

    o_ref[...] = jnp.dot(xn_ref[...], w_ref[...], preferred_element_type=F32).astype(o_ref.dtype)


def _norm_matmul(x, g, w, *, x_cb=0, tm, tn, name, out_dtype=F32):
    m = x.shape[0]
    k, n = w.shape
    return pl.pallas_call(
        _norm_matmul_kernel,
        grid=(m // tm, n // tn),
        in_specs=[pl.BlockSpec((tm, k), lambda i, j: (i, x_cb)),
                  pl.BlockSpec((1, k), lambda i, j: (0, 0)),
                  pl.BlockSpec((k, tn), lambda i, j: (0, j))],
        out_specs=pl.BlockSpec((tm, tn), lambda i, j: (i, j)),
        out_shape=jax.ShapeDtypeStruct((m, n), out_dtype),
        scratch_shapes=[pltpu.VMEM((tm, k), BF16)],
        compiler_params=_cparams("parallel", "arbitrary"),
        name=name,
    )(x, g.reshape(1, k), w)


def _out_proj_kernel(h_ref, *refs):
    o_ref = refs[-1]
    n = (len(refs) - 1) // 2
    acc = h_ref[...]
    for y_ref, w_ref in zip(refs[:n], refs[n:2 * n]):
        acc = acc + jnp.dot(y_ref[...], w_ref[...], preferred_element_type=F32)
    o_ref[...] = acc


def _out_proj(h, ys, w_all, layer, *, tm, tn):
    m, n = h.shape
    in_specs = [pl.BlockSpec((tm, tn), lambda i, j: (i, j))]
    in_specs += [pl.BlockSpec((tm, y.shape[1]), lambda i, j: (i, 0)) for y in ys]
    row = 0
    for y in ys:
        width = y.shape[1]
        assert row % width == 0
        in_specs.append(pl.BlockSpec((None, width, tn), lambda i, j, rb=row // width: (layer, rb, j)))
        row += width
    return pl.pallas_call(
        _out_proj_kernel,
        grid=(m // tm, n // tn),
        in_specs=in_specs,
        out_specs=pl.BlockSpec((tm, tn), lambda i, j: (i, j)),
        out_shape=jax.ShapeDtypeStruct((m, n), F32),
        compiler_params=_cparams("parallel", "arbitrary"),
        name="out_proj",
    )(h, *ys, *([w_all] * len(ys)))


def _final_norm_kernel(x_ref, g_ref, o_ref):
    x = x_ref[...]
    ms = jnp.mean(x * x, axis=-1, keepdims=True)
    o_ref[...] = x * lax.rsqrt(ms + RMS_EPS) * g_ref[...]


def _final_norm(h, g, *, tm):
    m, n = h.shape
    return pl.pallas_call(
        _final_norm_kernel,
        grid=(m // tm,),
        in_specs=[pl.BlockSpec((tm, n), lambda i: (i, 0)), pl.BlockSpec((1, n), lambda i: (0, 0))],
        out_specs=pl.BlockSpec((tm, n), lambda i: (i, 0)),
        out_shape=jax.ShapeDtypeStruct((m, n), F32),
        compiler_params=_cparams("parallel"),
        name="final_norm",
    )(h, g.reshape(1, n))


ONES_ROWS = 16


def _build_vt(v_ref, vt_ref, hh, dv, seq, chunk):
    def body(c, carry):
        st = pl.multiple_of(c * chunk, chunk)
        vt_ref[hh, 0:dv, pl.ds(st, chunk)] = v_ref[pl.ds(st, chunk), hh * dv:(hh + 1) * dv].astype(F32).T.astype(BF16)
        return carry

    lax.fori_loop(0, seq // chunk, body, 0)
    vt_ref[hh, dv:dv + ONES_ROWS, :] = jnp.ones((ONES_ROWS, seq), BF16)


def _flash_kernel(*refs, scale, tq, tk, hp, dk, dv, has_decay):
    if has_decay:
        q_ref, k_ref, v_ref, g_ref, cq_ref, ck_ref, o_ref, vt_ref, qt_ref, s_ref, m_ref, acc_ref = refs
    else:
        q_ref, k_ref, v_ref, g_ref, o_ref, vt_ref, qt_ref, s_ref, m_ref, acc_ref = refs
    hb = pl.program_id(0)
    qi = pl.program_id(1)
    seq = k_ref.shape[0]

    @pl.when(qi == 0)
    def _():
        for hh in range(hp):
            _build_vt(v_ref, vt_ref, hh, dv, seq, tk)

    for hh in range(hp):
        qt_ref[hh] = (q_ref[:, hh * dk:(hh + 1) * dk].astype(F32) * (scale * LOG2E)).T.astype(BF16)
    m_ref[...] = jnp.full(m_ref.shape, NEG_INF, F32)
    acc_ref[...] = jnp.zeros(acc_ref.shape, F32)
    n_full = (qi * tq) // tk
    if has_decay:
        cq2 = [cq_ref[pl.ds(hb * hp + hh, 1), :] for hh in range(hp)]

    def scores(j, slot):
        start = pl.multiple_of(j * tk, tk)
        for hh in range(hp):
            s_ref[slot, hh] = jnp.dot(k_ref[pl.ds(start, tk), hh * dk:(hh + 1) * dk], qt_ref[hh],
                                      preferred_element_type=F32)

    def softmax_pv(j, slot, masked):
        start = pl.multiple_of(j * tk, tk)
        for hh in range(hp):
            t = s_ref[slot, hh]
            if has_decay:
                t = t - jnp.concatenate([ck_ref[hh, pl.ds(start, tk), :]] * (tq // LANES), axis=1)
            if masked:
                key = start + lax.broadcasted_iota(I32, (tk, tq), 0)
                qry = qi * tq + lax.broadcasted_iota(I32, (tk, tq), 1)
                t = jnp.where(key <= qry, t, NEG_INF)
            m_prev = m_ref[hh]
            mx = jnp.max(t, axis=0, keepdims=True)
            if has_decay:
                m_new = jnp.maximum(m_prev, mx + cq2[hh])
                shift = m_new - cq2[hh]
            else:
                m_new = jnp.maximum(m_prev, mx)
                shift = m_new
            alpha = jnp.exp2(m_prev - m_new)
            p = jnp.exp2(t - shift).astype(BF16)
            acc_ref[hh] = alpha * acc_ref[hh] + jnp.dot(vt_ref[hh, :, pl.ds(start, tk)], p,
                                                         preferred_element_type=F32)
            m_ref[hh] = m_new

    scores(0, 0)

    def pair(j):
        scores(j + 1, 1)
        softmax_pv(j, 0, False)
        scores(j + 2, 0)
        softmax_pv(j + 1, 1, False)

    def quad(qq, carry):
        pair(4 * qq)
        pair(4 * qq + 2)
        return carry

    quads = n_full // 4
    lax.fori_loop(0, quads, quad, 0)
    rem = n_full - 4 * quads

    @pl.when(rem >= 2)
    def _():
        pair(4 * quads)

    last = 4 * quads + 2 * (rem // 2)

    @pl.when(n_full % 2 == 1)
    def _():
        scores(last + 1, 1)
        softmax_pv(last, 0, False)
        softmax_pv(last + 1, 1, True)

    @pl.when(n_full % 2 == 0)
    def _():
        softmax_pv(last, 0, True)

    for hh in range(hp):
        a = acc_ref[hh]
        o = (a[0:dv, :] / a[dv:dv + 1, :]).T
        o_ref[:, hh * dv:(hh + 1) * dv] = (o * _silu(g_ref[:, hh * dv:(hh + 1) * dv])).astype(o_ref.dtype)


def _flash(q_arr, q_off, k_arr, k_off, v_arr, v_off, g_arr, g_off, *, heads, dk, dv, scale, tq, tk, hp,
           decay=None, name):
    seq = q_arr.shape[0]
    once = pl.Buffered(1)
    in_specs = [pl.BlockSpec((tq, hp * dk), lambda h, i: (i, q_off // (hp * dk) + h)),
                pl.BlockSpec((seq, hp * dk), lambda h, i: (0, k_off // (hp * dk) + h), pipeline_mode=once),
                pl.BlockSpec((seq, hp * dv), lambda h, i: (0, v_off // (hp * dv) + h), pipeline_mode=once),
                pl.BlockSpec((tq, hp * dv), lambda h, i: (i, g_off // (hp * dv) + h))]
    args = [q_arr, k_arr, v_arr, g_arr]
    if decay is not None:
        cum_t, cum_rep = decay
        in_specs += [pl.BlockSpec((SUBLANES, tq), lambda h, i: (0, i)),
                     pl.BlockSpec((hp, seq, LANES), lambda h, i: (h, 0, 0), pipeline_mode=once)]
        args += [cum_t, cum_rep]
    return pl.pallas_call(
        functools.partial(_flash_kernel, scale=scale, tq=tq, tk=tk, hp=hp, dk=dk, dv=dv, has_decay=decay is not None),
        grid=(heads // hp, seq // tq),
        in_specs=in_specs,
        out_specs=pl.BlockSpec((tq, hp * dv), lambda h, i: (i, h)),
        out_shape=jax.ShapeDtypeStruct((seq, heads * dv), BF16),
        scratch_shapes=[pltpu.VMEM((hp, dv + ONES_ROWS, seq), BF16), pltpu.VMEM((hp, dk, tq), BF16),
                        pltpu.VMEM((2, hp, tk, tq), F32), pltpu.VMEM((hp, 1, tq), F32),
                        pltpu.VMEM((hp, dv + ONES_ROWS, tq), F32)],
        compiler_params=_cparams("parallel", "arbitrary"),
        name=name,
    )(*args)


def _decay_kernel(f_ref, b_ref, ct_ref, cr_ref, carry_ref, *, t):
    i = pl.program_id(0)

    @pl.when(i == 0)
    def _():
        carry_ref[...] = jnp.zeros(carry_ref.shape, F32)

    x = f_ref[...] + b_ref[...]
    lf = jnp.minimum(x, 0.0) - jnp.log1p(jnp.exp(-jnp.abs(x)))
    row = lax.broadcasted_iota(I32, lf.shape, 0)
    s = 1
    while s < t:
        lf = lf + jnp.where(row >= s, pltpu.roll(lf, s, 0), 0.0)
        s *= 2
    lf = lf + carry_ref[...]
    carry_ref[...] = lf[t - 1:t, :]
    lf2 = lf * LOG2E
    ct_ref[...] = lf2.T[:FOX_HEADS, :]
    for h in range(FOX_HEADS):
        cr_ref[h] = jnp.broadcast_to(lf2[:, h:h + 1], (t, LANES))


def _decay(proj, f_cb, b_f, *, t):
    seq = proj.shape[0]
    b = jnp.pad(b_f.reshape(1, FOX_HEADS), ((0, 0), (0, LANES - FOX_HEADS)))
    return pl.pallas_call(
        functools.partial(_decay_kernel, t=t),
        grid=(seq // t,),
        in_specs=[pl.BlockSpec((t, LANES), lambda i: (i, f_cb)), pl.BlockSpec((1, LANES), lambda i: (0, 0))],
        out_specs=[pl.BlockSpec((FOX_HEADS, t), lambda i: (0, i)),
                   pl.BlockSpec((FOX_HEADS, t, LANES), lambda i: (0, i, 0))],
        out_shape=[jax.ShapeDtypeStruct((FOX_HEADS, seq), F32), jax.ShapeDtypeStruct((FOX_HEADS, seq, LANES), F32)],
        scratch_shapes=[pltpu.VMEM((1, LANES), F32)],
        compiler_params=_cparams("arbitrary"),
        name="fox_decay",
    )(proj, b)


def _mem_attn_kernel(q_ref, kv_ref, g_ref, o_ref):
    d = MEM_HEAD_DIM
    for h in range(MEM_HEADS):
        hs = slice(h * d, (h + 1) * d)
        k = kv_ref[:, hs].astype(BF16)
        v = kv_ref[:, MEM_WIDTH + h * d:MEM_WIDTH + (h + 1) * d].astype(BF16)
        s = lax.dot_general(q_ref[:, hs], k, (((1,), (1,)), ((), ())), preferred_element_type=F32) * (d ** -0.5)
        m = jnp.max(s, axis=1, keepdims=True)
        p = jnp.exp(s - m)
        l = jnp.sum(p, axis=1, keepdims=True)
        o = jnp.dot(p.astype(BF16), v, preferred_element_type=F32) / l
        o_ref[:, hs] = (o * _silu(g_ref[:, hs])).astype(o_ref.dtype)


def _mem_kv_all(mem2d, g, w_all):
    depth, k, n = w_all.shape
    m = mem2d.shape[0]
    tn = 512
    return pl.pallas_call(
        _norm_matmul_kernel,
        grid=(depth, n // tn),
        in_specs=[pl.BlockSpec((m, k), lambda l, j: (0, 0)),
                  pl.BlockSpec((1, k), lambda l, j: (0, 0)),
                  pl.BlockSpec((None, k, tn), lambda l, j: (l, 0, j))],
        out_specs=pl.BlockSpec((None, m, tn), lambda l, j: (l, 0, j)),
        out_shape=jax.ShapeDtypeStruct((depth, m, n), F32),
        scratch_shapes=[pltpu.VMEM((m, k), BF16)],
        compiler_params=_cparams("arbitrary", "arbitrary"),
        name="mem_kv",
    )(mem2d, g.reshape(1, k), w_all)


def _mem_attn(proj_b, q_cb, proj_f, g_cb, mem_kv, *, t):
    seq = proj_b.shape[0]
    mem_kv, layer = mem_kv
    nm = mem_kv.shape[1]
    w = MEM_WIDTH
    d = MEM_HEAD_DIM
    return pl.pallas_call(
        _mem_attn_kernel,
        grid=(seq // t,),
        in_specs=[pl.BlockSpec((t, w), lambda i: (i, q_cb * d // w)),
                  pl.BlockSpec((None, nm, 2 * w), lambda i: (layer, 0, 0)),
                  pl.BlockSpec((t, w), lambda i: (i, g_cb * d // w))],
        out_specs=pl.BlockSpec((t, w), lambda i: (i, 0)),
        out_shape=jax.ShapeDtypeStruct((seq, w), BF16),
        compiler_params=_cparams("parallel"),
        name="mem_attn",
    )(proj_b, mem_kv, proj_f)


S5_TILE_GROUPS = LANES // S5_GROUP
S5_TILE_STATES = S5_TILE_GROUPS * S5_STATE
S5_TILES = S5_GROUPS // S5_TILE_GROUPS


def _s5_scan_kernel(u_ref, b_ref, c_ref, d_ref, tab_ref, z_ref, bu_ref, carry_ref, *, tc):
    ns = S5_TILE_STATES

    @pl.when(pl.program_id(1) == 0)
    def _():
        carry_ref[...] = jnp.zeros(carry_ref.shape, F32)

    u = u_ref[...]
    bu_ref[...] = jnp.dot(u.astype(BF16), b_ref[...], preferred_element_type=F32)
    steps = [(1, tab_ref[0], tab_ref[1]), (2, tab_ref[2], tab_ref[3]), (4, tab_ref[4], tab_ref[5])]
    pr = tab_ref[6]
    pi = tab_ref[7]

    def body(i, carry):
        cr, ci = carry
        r0 = pl.multiple_of(i * SUBLANES, SUBLANES)
        xr = bu_ref[pl.ds(r0, SUBLANES), 0:ns]
        xi = bu_ref[pl.ds(r0, SUBLANES), ns:2 * ns]
        for s, ar, ai in steps:
            sr = pltpu.roll(xr, s, 0)
            si = pltpu.roll(xi, s, 0)
            xr, xi = xr + ar * sr - ai * si, xi + ar * si + ai * sr
        xr, xi = xr + pr * cr - pi * ci, xi + pr * ci + pi * cr
        bu_ref[pl.ds(r0, SUBLANES), 0:ns] = xr
        bu_ref[pl.ds(r0, SUBLANES), ns:2 * ns] = xi
        return xr[SUBLANES - 1:SUBLANES, :], xi[SUBLANES - 1:SUBLANES, :]

    cr, ci = lax.fori_loop(0, tc // SUBLANES, body, (carry_ref[0:1, 0:ns], carry_ref[0:1, ns:2 * ns]))
    carry_ref[0:1, 0:ns] = cr
    carry_ref[0:1, ns:2 * ns] = ci
    y = jnp.dot(bu_ref[...].astype(BF16), c_ref[...], preferred_element_type=F32) + d_ref[...] * u
    z_ref[...] = jax.nn.gelu(y)


def _s5_prepare(lam_re, lam_im, log_dt, b_re, b_im, c_re, c_im):
    dt = jnp.exp(log_dt.astype(F32))[:, None]
    lr = lam_re.astype(F32)
    li = lam_im.astype(F32)
    mag = jnp.exp(lr * dt)
    ab_re = mag * jnp.cos(li * dt)
    ab_im = mag * jnp.sin(li * dt)
    den = lr * lr + li * li
    nr = ab_re - 1.0
    f_re = (nr * lr + ab_im * li) / den
    f_im = (ab_im * lr - nr * li) / den
    br = b_re.astype(F32)
    bim = b_im.astype(F32)
    bb_re = f_re[..., None] * br - f_im[..., None] * bim
    bb_im = f_re[..., None] * bim + f_im[..., None] * br
    eye = jnp.eye(S5_TILE_GROUPS, dtype=F32)

    def blockdiag_in(bb):
        t = bb.reshape(S5_TILES, S5_TILE_GROUPS, S5_STATE, S5_GROUP)
        m = jnp.einsum("jgpc,gh->jgchp", t, eye)
        return m.reshape(S5_TILES, LANES, S5_TILE_STATES)

    def blockdiag_out(cc):
        t = cc.reshape(S5_TILES, S5_TILE_GROUPS, S5_GROUP, S5_STATE)
        m = jnp.einsum("jgcp,gh->jgphc", t, eye)
        return m.reshape(S5_TILES, S5_TILE_STATES, LANES)

    b_cat = jnp.concatenate([blockdiag_in(bb_re), blockdiag_in(bb_im)], axis=2).astype(BF16)
    c_cat = jnp.concatenate([blockdiag_out(c_re.astype(F32)), -blockdiag_out(c_im.astype(F32))], axis=1).astype(BF16)

    a_r = ab_re.reshape(S5_TILES, 1, S5_TILE_STATES)
    a_i = ab_im.reshape(S5_TILES, 1, S5_TILE_STATES)

    def cmul(xr, xi, yr, yi):
        return xr * yr - xi * yi, xr * yi + xi * yr

    a2 = cmul(a_r, a_i, a_r, a_i)
    a4 = cmul(*a2, *a2)
    row = jnp.arange(SUBLANES)[None, :, None]
    tabs = []
    for s, (pr_, pi_) in ((1, (a_r, a_i)), (2, a2), (4, a4)):
        tabs.append(jnp.where(row >= s, pr_, 0.0))
        tabs.append(jnp.where(row >= s, pi_, 0.0))
    pw = [(a_r, a_i)]
    for _ in range(SUBLANES - 1):
        pw.append(cmul(*pw[-1], a_r, a_i))
    tabs.append(jnp.concatenate([p[0] for p in pw], axis=1))
    tabs.append(jnp.concatenate([p[1] for p in pw], axis=1))
    tab = jnp.stack([jnp.broadcast_to(t, (S5_TILES, SUBLANES, S5_TILE_STATES)) for t in tabs], axis=1)
    return b_cat, c_cat, tab.astype(F32)


def _s5_scan(proj, u_cb, b_cat, c_cat, d_skip, tab, *, tc):
    seq = proj.shape[0]
    ns = S5_TILE_STATES
    d = d_skip.astype(F32).reshape(S5_TILES, 1, LANES)
    return pl.pallas_call(
        functools.partial(_s5_scan_kernel, tc=tc),
        grid=(S5_TILES, seq // tc),
        in_specs=[pl.BlockSpec((tc, LANES), lambda j, c: (c, u_cb + j)),
                  pl.BlockSpec((None, LANES, 2 * ns), lambda j, c: (j, 0, 0)),
                  pl.BlockSpec((None, 2 * ns, LANES), lambda j, c: (j, 0, 0)),
                  pl.BlockSpec((None, 1, LANES), lambda j, c: (j, 0, 0)),
                  pl.BlockSpec((None, 8, SUBLANES, ns), lambda j, c: (j, 0, 0, 0))],
        out_specs=pl.BlockSpec((tc, LANES), lambda j, c: (c, j)),
        out_shape=jax.ShapeDtypeStruct((seq, S5_WIDTH), F32),
        scratch_shapes=[pltpu.VMEM((tc, 2 * ns), F32), pltpu.VMEM((SUBLANES, 2 * ns), F32)],
        compiler_params=_cparams("parallel", "arbitrary"),
        name="s5_scan",
    )(proj, b_cat, c_cat, d, tab)


def _s5_glu_kernel(z_ref, w_ref, g_ref, o_ref, *, tn):
    j = pl.program_id(1)
    z = z_ref[...]
    a = jnp.dot(z.astype(BF16), w_ref[...], preferred_element_type=F32)
    zc = z_ref[:, pl.ds(pl.multiple_of(j * tn, tn), tn)]
    o_ref[...] = (zc * jax.nn.sigmoid(a) * _silu(g_ref[...])).astype(o_ref.dtype)


def _s5_glu(z, w_glu, proj, g_cb, *, tm, tn):
    seq, n = z.shape
    return pl.pallas_call(
        functools.partial(_s5_glu_kernel, tn=tn),
        grid=(seq // tm, n // tn),
        in_specs=[pl.BlockSpec((tm, n), lambda i, j: (i, 0)),
                  pl.BlockSpec((n, tn), lambda i, j: (0, j)),
                  pl.BlockSpec((tm, tn), lambda i, j: (i, g_cb * (n // tn) + j))],
        out_specs=pl.BlockSpec((tm, tn), lambda i, j: (i, j)),
        out_shape=jax.ShapeDtypeStruct((seq, n), BF16),
        compiler_params=_cparams("parallel", "arbitrary"),
        name="s5_glu",
    )(z, w_glu, proj)


def _rope_tables(pos, freq):
    ang = pos * freq
    lane = lax.broadcasted_iota(I32, ang.shape, 1)
    half = MLA_ROPE // 2
    cos = jnp.cos(ang)
    sin = jnp.sin(ang)
    c = jnp.where(lane < MLA_ROPE, cos, 0.0)
    s1 = jnp.where(lane < half, -sin, 0.0)
    s2 = jnp.where((lane >= half) & (lane < MLA_ROPE), sin, 0.0)
    return c, s1, s2


def _rope_apply(x, c, s1, s2):
    half = MLA_ROPE // 2
    return x * c + pltpu.roll(x, LANES - half, 1) * s1 + pltpu.roll(x, half, 1) * s2


def _mla_up_kernel(cq_ref, ckv_ref, kr_ref, pos_ref, freq_ref, gq_ref, gkv_ref, wq_ref, wkv_ref, q_ref, k_ref, v_ref):
    def normed(x_ref, g_ref):
        x = x_ref[...]
        ms = jnp.mean(x * x, axis=-1, keepdims=True)
        return (x * lax.rsqrt(ms + RMS_EPS) * g_ref[...]).astype(BF16)

    qf = jnp.dot(normed(cq_ref, gq_ref), wq_ref[...], preferred_element_type=F32)
    kvf = jnp.dot(normed(ckv_ref, gkv_ref), wkv_ref[...], preferred_element_type=F32)
    c, s1, s2 = _rope_tables(pos_ref[...], freq_ref[...])
    kr = _rope_apply(kr_ref[...], c, s1, s2).astype(BF16)
    for h in range(MLA_HEADS):
        b = 2 * LANES * h
        q_ref[:, b:b + LANES] = qf[:, b:b + LANES].astype(BF16)
        q_ref[:, b + LANES:b + 2 * LANES] = _rope_apply(qf[:, b + LANES:b + 2 * LANES], c, s1, s2).astype(BF16)
        k_ref[:, b:b + LANES] = kvf[:, b:b + LANES].astype(BF16)
        k_ref[:, b + LANES:b + 2 * LANES] = kr
        v_ref[:, LANES * h:LANES * (h + 1)] = kvf[:, b + LANES:b + 2 * LANES].astype(BF16)


def _mla_up(proj, cq_cb, ckv_cb, kr_cb, g_cq, g_ckv, w_uq, w_ukv, pos_col, freq, *, t):
    seq = proj.shape[0]
    w = 2 * LANES * MLA_HEADS
    rq, rkv = w_uq.shape[0], w_ukv.shape[0]
    return pl.pallas_call(
        _mla_up_kernel,
        grid=(seq // t,),
        in_specs=[pl.BlockSpec((t, rq), lambda i: (i, cq_cb)),
                  pl.BlockSpec((t, rkv), lambda i: (i, ckv_cb)),
                  pl.BlockSpec((t, LANES), lambda i: (i, kr_cb)),
                  pl.BlockSpec((t, 1), lambda i: (i, 0)),
                  pl.BlockSpec((1, LANES), lambda i: (0, 0)),
                  pl.BlockSpec((1, rq), lambda i: (0, 0)),
                  pl.BlockSpec((1, rkv), lambda i: (0, 0)),
                  pl.BlockSpec((rq, w), lambda i: (0, 0)),
                  pl.BlockSpec((rkv, w), lambda i: (0, 0))],
        out_specs=[pl.BlockSpec((t, w), lambda i: (i, 0)),
                   pl.BlockSpec((t, w), lambda i: (i, 0)),
                   pl.BlockSpec((t, MLA_WIDTH), lambda i: (i, 0))],
        out_shape=[jax.ShapeDtypeStruct((seq, w), BF16), jax.ShapeDtypeStruct((seq, w), BF16),
                   jax.ShapeDtypeStruct((seq, MLA_WIDTH), BF16)],
        compiler_params=_cparams("parallel"),
        name="mla_up",
    )(proj, proj, proj, pos_col, freq, g_cq.reshape(1, rq), g_ckv.reshape(1, rkv), w_uq, w_ukv)


def _t5_bucket(dist):
    n = jnp.maximum(dist, 0)
    max_exact = T5_BUCKETS // 2
    log_ratio = jnp.log(jnp.maximum(n, 1).astype(F32) / max_exact) / math.log(T5_MAX_DIST / max_exact)
    large = jnp.minimum(max_exact + (log_ratio * (T5_BUCKETS - max_exact)).astype(I32), T5_BUCKETS - 1)
    return jnp.where(n < max_exact, n, large)


T5_MASK_BUCKET = T5_BUCKETS


def _t5_lookup(table_row, bucket):
    rows, width = bucket.shape
    tab = jnp.broadcast_to(table_row, (rows, LANES))
    parts = [jnp.take_along_axis(tab, bucket[:, c:c + LANES], axis=1, mode="promise_in_bounds")
             for c in range(0, width, LANES)]
    return parts[0] if len(parts) == 1 else jnp.concatenate(parts, axis=1)


def _nsa_cmp_kernel(x_ref, pe_ref, w1_ref, w2_ref, o_ref, xf_ref, *, nc):
    half = NSA_CMP_LEN // 2
    d = NSA_HEAD_DIM
    xf_ref[...] = x_ref[...].astype(F32)
    u = jnp.zeros((nc, NSA_CMP_HIDDEN), F32)
    v = jnp.zeros((nc, NSA_CMP_HIDDEN), F32)
    for r in range(half):
        a = xf_ref[pl.ds(r, nc, stride=NSA_CMP_STRIDE), :]
        u = u + jnp.dot((a + pe_ref[r:r + 1, :]).astype(BF16), w1_ref[r * d:(r + 1) * d, :],
                        preferred_element_type=F32)
        v = v + jnp.dot((a + pe_ref[half + r:half + r + 1, :]).astype(BF16),
                        w1_ref[(half + r) * d:(half + r + 1) * d, :], preferred_element_type=F32)
    hid = u + pltpu.roll(v, nc - 1, 0)
    o_ref[...] = jnp.dot(jax.nn.gelu(hid).astype(BF16), w2_ref[...], preferred_element_type=F32).astype(o_ref.dtype)


def _nsa_compress(proj, k_cb, pe, w1, w2):
    seq = proj.shape[0]
    nc = seq // NSA_CMP_STRIDE
    d = NSA_HEAD_DIM
    g = NSA_KV_GROUPS
    return pl.pallas_call(
        functools.partial(_nsa_cmp_kernel, nc=nc),
        grid=(2, g),
        in_specs=[pl.BlockSpec((seq, d), lambda a, b: (0, k_cb + a * g + b)),
                  pl.BlockSpec((None, NSA_CMP_LEN, d), lambda a, b: (a, 0, 0)),
                  pl.BlockSpec((None, NSA_CMP_LEN * d, NSA_CMP_HIDDEN), lambda a, b: (a, 0, 0)),
                  pl.BlockSpec((None, NSA_CMP_HIDDEN, d), lambda a, b: (a, 0, 0))],
        out_specs=pl.BlockSpec((None, None, nc, d), lambda a, b: (a, b, 0, 0)),
        out_shape=jax.ShapeDtypeStruct((2, g, nc, d), BF16),
        scratch_shapes=[pltpu.VMEM((seq, d), F32)],
        compiler_params=_cparams("parallel", "arbitrary"),
        name="nsa_compress",
    )(proj, pe, w1, w2)


def _nsa_select_kernel(q_ref, kc_ref, vc_ref, posq_ref, posc_ref, tab_ref, gate_ref, ov_ref,
                       oc_ref, sel_ref, *, nc, n_slc, tq):
    qi = pl.program_id(0)
    d = NSA_HEAD_DIM
    c = d ** -0.5 * LOG2E
    tok = qi * tq + lax.broadcasted_iota(I32, (1, tq), 1)
    cmp_end = lax.broadcasted_iota(I32, (nc, 1), 0) * NSA_CMP_STRIDE + (NSA_CMP_LEN - 1)
    valid = cmp_end <= tok
    pos_c = jnp.concatenate([posc_ref[...]] * (tq // LANES), axis=1)
    bucket = jnp.where(valid, _t5_bucket(posq_ref[...] - pos_c), T5_MASK_BUCKET)
    gates = jax.nn.sigmoid(gate_ref[...])
    ovt = ov_ref[...]
    js = lax.broadcasted_iota(I32, (LANES, tq), 0)
    jf = js.astype(F32)
    cur = tok // NSA_SLC_BLOCK
    forced = (js == 0) | (js == cur) | (js == cur - 1)
    for g in range(NSA_KV_GROUPS):
        kc = kc_ref[g]
        vct = vc_ref[g].astype(F32).T.astype(BF16)
        psum = jnp.zeros((nc, tq), F32)
        for r in range(NSA_REP):
            h = g * NSA_REP + r
            qt = (q_ref[:, h * d:(h + 1) * d].astype(F32) * c).T.astype(BF16)
            t = jnp.dot(kc, qt, preferred_element_type=F32) + _t5_lookup(tab_ref[h:h + 1, :], bucket)
            m = jnp.max(t, axis=0, keepdims=True)
            e = jnp.exp2(t - m)
            l = jnp.sum(e, axis=0, keepdims=True)
            p = e * jnp.where(m > 0.5 * NEG_INF, 1.0 / l, 0.0)
            o = jnp.dot(vct, p.astype(BF16), preferred_element_type=F32)
            oc_ref[:, h * d:(h + 1) * d] = gates[:, 3 * h:3 * h + 1] * o.T
            psum = psum + p
        p_hi = psum.astype(BF16)
        p_lo = (psum - p_hi.astype(F32)).astype(BF16)
        imp = jnp.dot(ovt, p_hi, preferred_element_type=F32) + jnp.dot(ovt, p_lo, preferred_element_type=F32)
        st = jnp.where(forced, FORCE_SCORE, jnp.where(js > cur, -1.0, imp))
        st = jnp.where(js < n_slc, st, -2.0)
        sel = jnp.zeros((LANES, tq), F32)
        for _ in range(NSA_SLC_TOPK):
            mx = jnp.max(st, axis=0, keepdims=True)
            first = jnp.min(jnp.where(st == mx, jf, float(LANES)), axis=0, keepdims=True)
            hit = jf == first
            sel = jnp.where(hit, 1.0, sel)
            st = jnp.where(hit, -3e38, st)
        sel_ref[g] = sel.astype(sel_ref.dtype)


def _nsa_select(proj_b, q_cb, proj_f, gate_cb, kc, vc, pos_row, pos_cmp_rep, tab_t, ov_t, *, n_slc, tq):
    seq = proj_b.shape[0]
    nc = kc.shape[1]
    g = NSA_KV_GROUPS
    d = NSA_HEAD_DIM
    return pl.pallas_call(
        functools.partial(_nsa_select_kernel, nc=nc, n_slc=n_slc, tq=tq),
        grid=(seq // tq,),
        in_specs=[pl.BlockSpec((tq, NSA_WIDTH), lambda i: (i, q_cb)),
                  pl.BlockSpec((g, nc, d), lambda i: (0, 0, 0)),
                  pl.BlockSpec((g, nc, d), lambda i: (0, 0, 0)),
                  pl.BlockSpec((1, tq), lambda i: (0, i)),
                  pl.BlockSpec((nc, LANES), lambda i: (0, 0)),
                  pl.BlockSpec((SUBLANES, LANES), lambda i: (0, 0)),
                  pl.BlockSpec((tq, LANES), lambda i: (i, gate_cb)),
                  pl.BlockSpec((LANES, nc), lambda i: (0, 0))],
        out_specs=[pl.BlockSpec((tq, NSA_WIDTH), lambda i: (i, 0)),
                   pl.BlockSpec((g, LANES, tq), lambda i: (0, 0, i))],
        out_shape=[jax.ShapeDtypeStruct((seq, NSA_WIDTH), F32),
                   jax.ShapeDtypeStruct((g, LANES, seq), F32)],
        compiler_params=_cparams("parallel"),
        name="nsa_select",
    )(proj_b, kc, vc, pos_row, pos_cmp_rep, tab_t * LOG2E, proj_f, ov_t)


def _nsa_slc_kernel(pqmin_ref, pkmax_ref, q_ref, k_ref, v_ref, sel_ref, posq_ref, posk_ref, tab_ref, o_ref,
                    vt_ref, qt_ref, s_ref, m_ref, acc_ref, *, tq, tk):
    g = pl.program_id(0)
    qi = pl.program_id(1)
    d = NSA_HEAD_DIM
    scale = d ** -0.5
    seq = k_ref.shape[0]

    @pl.when(qi == 0)
    def _():
        _build_vt(v_ref, vt_ref, 0, d, seq, tk)

    for r in range(NSA_REP):
        qt_ref[r] = (q_ref[:, r * d:(r + 1) * d].astype(F32) * (scale * LOG2E)).T.astype(BF16)
    m_ref[...] = jnp.full(m_ref.shape, NEG_INF, F32)
    acc_ref[...] = jnp.zeros(acc_ref.shape, F32)
    pos_q = posq_ref[...]
    per_tile = tk // NSA_SLC_BLOCK
    n_full = (qi * tq) // tk

    def scores(j, slot):
        start = pl.multiple_of(j * tk, tk)
        k = k_ref[pl.ds(start, tk), :]
        for r in range(NSA_REP):
            s_ref[slot, r] = jnp.dot(k, qt_ref[r], preferred_element_type=F32)

    def softmax_pv(j, slot, masked, near):
        start = pl.multiple_of(j * tk, tk)
        picked = jnp.concatenate(
            [jnp.broadcast_to(sel_ref[pl.ds(j * per_tile + b, 1), :], (NSA_SLC_BLOCK, tq)) for b in range(per_tile)],
            axis=0)
        if masked:
            key = start + lax.broadcasted_iota(I32, (tk, tq), 0)
            qry = qi * tq + lax.broadcasted_iota(I32, (tk, tq), 1)
            picked = jnp.where(key <= qry, picked, 0.0)
        mask = picked > 0.5
        if near:
            pos_k = jnp.concatenate([posk_ref[pl.ds(start, tk), :]] * (tq // LANES), axis=1)
            bucket = jnp.where(mask, _t5_bucket(pos_q - pos_k), T5_MASK_BUCKET)
        for r in range(NSA_REP):
            h = g * NSA_REP + r
            t = s_ref[slot, r]
            if near:
                t = t + _t5_lookup(tab_ref[pl.ds(h, 1), :], bucket)
            else:
                t = jnp.where(mask, t, NEG_INF)
            m_prev = m_ref[r]
            mx = jnp.max(t, axis=0, keepdims=True)
            if near:
                m_new = jnp.maximum(m_prev, mx)
                shift = m_new
            else:
                b = tab_ref[pl.ds(h, 1), T5_BUCKETS - 1:T5_BUCKETS]
                m_new = jnp.maximum(m_prev, mx + b)
                shift = m_new - b
            alpha = jnp.exp2(m_prev - m_new)
            p = jnp.exp2(t - shift).astype(BF16)
            acc_ref[r] = alpha * acc_ref[r] + jnp.dot(vt_ref[0, :, pl.ds(start, tk)], p, preferred_element_type=F32)
            m_ref[r] = m_new

    def is_far(j):
        return pqmin_ref[qi] - pkmax_ref[j] >= T5_MAX_DIST

    def stages(j, count, near):
        for k in range(count):
            scores(j + k + 1, (k + 1) % 2)
            softmax_pv(j + k, k % 2, False, near)

    scores(0, 0)

    def pair(jj, carry):
        j = 2 * jj
        far = jnp.logical_and(is_far(j), is_far(j + 1))

        @pl.when(far)
        def _():
            stages(j, 2, False)

        @pl.when(jnp.logical_not(far))
        def _():
            stages(j, 2, True)

        return carry

    pairs = n_full // 2
    lax.fori_loop(0, pairs, pair, 0)
    last = 2 * pairs

    @pl.when(n_full % 2 == 1)
    def _():
        far = is_far(last)

        @pl.when(far)
        def _():
            stages(last, 1, False)

        @pl.when(jnp.logical_not(far))
        def _():
            stages(last, 1, True)

        softmax_pv(last + 1, 1, True, True)

    @pl.when(n_full % 2 == 0)
    def _():
        softmax_pv(last, 0, True, True)

    for r in range(NSA_REP):
        a = acc_ref[r]
        o_ref[:, r * d:(r + 1) * d] = (a[0:d, :] / a[d:d + 1, :]).T


def _nsa_slc(proj_b, q_off, k_off, v_off, sel, pos_row, pos_rep, tab_t, pq_min, pk_max, *, tq, tk):
    seq = proj_b.shape[0]
    tab_t = tab_t * LOG2E
    d = NSA_HEAD_DIM
    gw = NSA_REP * d
    once = pl.Buffered(1)
    grid_spec = pltpu.PrefetchScalarGridSpec(
        num_scalar_prefetch=2,
        grid=(NSA_KV_GROUPS, seq // tq),
        in_specs=[pl.BlockSpec((tq, gw), lambda g, i, *_: (i, q_off // gw + g)),
                  pl.BlockSpec((seq, d), lambda g, i, *_: (0, k_off // d + g), pipeline_mode=once),
                  pl.BlockSpec((seq, d), lambda g, i, *_: (0, v_off // d + g), pipeline_mode=once),
                  pl.BlockSpec((None, LANES, tq), lambda g, i, *_: (g, 0, i)),
                  pl.BlockSpec((1, tq), lambda g, i, *_: (0, i)),
                  pl.BlockSpec((seq, LANES), lambda g, i, *_: (0, 0), pipeline_mode=once),
                  pl.BlockSpec((SUBLANES, LANES), lambda g, i, *_: (0, 0))],
        out_specs=pl.BlockSpec((tq, gw), lambda g, i, *_: (i, g)),
        scratch_shapes=[pltpu.VMEM((1, d + ONES_ROWS, seq), BF16), pltpu.VMEM((NSA_REP, d, tq), BF16),
                        pltpu.VMEM((2, NSA_REP, tk, tq), F32), pltpu.VMEM((NSA_REP, 1, tq), F32),
                        pltpu.VMEM((NSA_REP, d + ONES_ROWS, tq), F32)])
    return pl.pallas_call(
        functools.partial(_nsa_slc_kernel, tq=tq, tk=tk),
        grid_spec=grid_spec,
        out_shape=jax.ShapeDtypeStruct((seq, NSA_WIDTH), F32),
        compiler_params=_cparams("parallel", "arbitrary"),
        name="nsa_slc",
    )(pq_min, pk_max, proj_b, proj_b, proj_b, sel, pos_row, pos_rep, tab_t)


def _nsa_win_kernel(*refs, tq, nt):
    q_ref = refs[0]
    k_refs = refs[1:1 + nt]
    v_refs = refs[1 + nt:1 + 2 * nt]
    pk_refs = refs[1 + 2 * nt:1 + 3 * nt]
    posq_ref, tab_ref, gate_ref, gout_ref, oc_ref, os_ref, o_ref = refs[1 + 3 * nt:]
    qi = pl.program_id(0)
    d = NSA_HEAD_DIM
    c = d ** -0.5 * LOG2E
    tok = qi * tq + lax.broadcasted_iota(I32, (1, tq), 1)
    lower = jnp.maximum(tok - (NSA_WINDOW - 1), 0)
    sub = lax.broadcasted_iota(I32, (tq, 1), 0)
    pos_q = posq_ref[...]
    buckets = []
    for jj in range(nt):
        kidx = (qi - (nt - 1) + jj) * tq + sub
        pos_k = jnp.concatenate([pk_refs[jj][...]] * (tq // LANES), axis=1)
        b = jnp.where(kidx >= lower, _t5_bucket(pos_q - pos_k), T5_MASK_BUCKET)
        buckets.append(jnp.where(kidx <= tok, b, T5_MASK_BUCKET))
    gates = jax.nn.sigmoid(gate_ref[...])
    for g in range(NSA_KV_GROUPS):
        ks = [kr[:, g * d:(g + 1) * d] for kr in k_refs]
        vts = [vr[:, g * d:(g + 1) * d].astype(F32).T.astype(BF16) for vr in v_refs]
        for r in range(NSA_REP):
            h = g * NSA_REP + r
            hs = slice(h * d, (h + 1) * d)
            qt = (q_ref[:, hs].astype(F32) * c).T.astype(BF16)
            ts = [jnp.dot(ks[jj], qt, preferred_element_type=F32) + _t5_lookup(tab_ref[h:h + 1, :], buckets[jj])
                  for jj in range(nt)]
            m = functools.reduce(jnp.maximum, [jnp.max(t, axis=0, keepdims=True) for t in ts])
            ps = [jnp.exp2(t - m) for t in ts]
            l = functools.reduce(jnp.add, [jnp.sum(p, axis=0, keepdims=True) for p in ps])
            o_t = functools.reduce(jnp.add, [jnp.dot(vt, p.astype(BF16), preferred_element_type=F32)
                                             for vt, p in zip(vts, ps)])
            o_w = (o_t / l).T
            o = oc_ref[:, hs] + gates[:, 3 * h + 1:3 * h + 2] * os_ref[:, hs] + gates[:, 3 * h + 2:3 * h + 3] * o_w
            o_ref[:, hs] = (o * _silu(gout_ref[:, hs])).astype(o_ref.dtype)


def _nsa_win(proj_b, q_cb, k_cb, v_cb, proj_f, gate_cb, gout_cb, oc, o_s, pos_row, pos_rep, tab_t, *, tq):
    seq = proj_b.shape[0]
    nt = NSA_WINDOW // tq + 1

    def band_rows(jj, cb):
        return pl.BlockSpec((tq, NSA_KV), lambda i: (jnp.maximum(i - (nt - 1) + jj, 0), cb))

    def band_pos(jj):
        return pl.BlockSpec((tq, LANES), lambda i: (jnp.maximum(i - (nt - 1) + jj, 0), 0))

    in_specs = [pl.BlockSpec((tq, NSA_WIDTH), lambda i: (i, q_cb))]
    in_specs += [band_rows(jj, k_cb) for jj in range(nt)]
    in_specs += [band_rows(jj, v_cb) for jj in range(nt)]
    in_specs += [band_pos(jj) for jj in range(nt)]
    in_specs += [pl.BlockSpec((1, tq), lambda i: (0, i)),
                 pl.BlockSpec((SUBLANES, LANES), lambda i: (0, 0)),
                 pl.BlockSpec((tq, LANES), lambda i: (i, gate_cb)),
                 pl.BlockSpec((tq, NSA_WIDTH), lambda i: (i, gout_cb)),
                 pl.BlockSpec((tq, NSA_WIDTH), lambda i: (i, 0)),
                 pl.BlockSpec((tq, NSA_WIDTH), lambda i: (i, 0))]
    args = [proj_b] * (1 + 2 * nt) + [pos_rep] * nt + [pos_row, tab_t * LOG2E, proj_f, proj_f, oc, o_s]
    return pl.pallas_call(
        functools.partial(_nsa_win_kernel, tq=tq, nt=nt),
        grid=(seq // tq,),
        in_specs=in_specs,
        out_specs=pl.BlockSpec((tq, NSA_WIDTH), lambda i: (i, 0)),
        out_shape=jax.ShapeDtypeStruct((seq, NSA_WIDTH), BF16),
        compiler_params=_cparams("parallel"),
        name="nsa_win",
    )(*args)


def _in_proj(h, norm_g, w_b, w_f, tiles, name):
    proj_b = _norm_matmul(h, norm_g, w_b, tm=tiles["tm"], tn=512, name=name + "_b", out_dtype=BF16)
    proj_f = _norm_matmul(h, norm_g, w_f, tm=tiles["tm"], tn=768, name=name + "_f")
    return proj_b, proj_f


def _even_layer(h, mem_kv, norm_g, w_in, s5, w_glu, b_f, tiles):
    pb, pf = _in_proj(h, norm_g, *w_in, tiles, "in_proj_even")
    ob, of = EVEN_B_OFF, EVEN_F_OFF
    b_cat, c_cat, tab, d_skip = s5
    z = _s5_scan(pf, of["u"] // LANES, b_cat, c_cat, d_skip, tab, tc=tiles["s5_tc"])
    y_s5 = _s5_glu(z, w_glu, pf, of["g_s5"] // S5_WIDTH, tm=tiles["tm"], tn=512)
    decay = _decay(pf, of["f"] // LANES, b_f, t=tiles["decay_t"])
    d = FOX_HEAD_DIM
    y_fox = _flash(pb, ob["q"], pb, ob["k"], pb, ob["v"], pf, of["g_fox"], heads=FOX_HEADS, dk=d, dv=d,
                   scale=d ** -0.5, tq=tiles["attn_tq"], tk=tiles["attn_tk"], hp=tiles["attn_hp"], decay=decay,
                   name="fox_attn")
    y_mem = _mem_attn(pb, ob["q_mem"] // MEM_HEAD_DIM, pf, of["g_mem"] // MEM_HEAD_DIM, mem_kv, t=tiles["mem_t"])
    return y_s5, y_fox, y_mem


def _odd_layer(h, mem_kv, norm_g, w_in, mla, nsa, pos, tiles):
    pb, pf = _in_proj(h, norm_g, *w_in, tiles, "in_proj_odd")
    ob, of = ODD_B_OFF, ODD_F_OFF
    g_cq, g_ckv, w_uq, w_ukv, freq = mla
    pos_col_f, pos_row, pos_cmp_rep, pos_rep, pq_min, pk_max = pos
    q_r, k_r, v_r = _mla_up(pf, of["c_q"] // MLA_Q_RANK, of["c_kv"] // MLA_KV_RANK, of["k_rope"] // LANES,
                            g_cq, g_ckv, w_uq, w_ukv, pos_col_f, freq, t=tiles["prep_t"])
    y_mla = _flash(q_r, 0, k_r, 0, v_r, 0, pf, of["g_mla"], heads=MLA_HEADS, dk=2 * LANES, dv=MLA_V,
                   scale=(MLA_NOPE + MLA_ROPE) ** -0.5, tq=tiles["attn_tq"], tk=tiles["attn_tk"], hp=tiles["attn_hp"],
                   name="mla_attn")
    pe, w1, w2, tab_t, ov, n_slc = nsa
    kvc = _nsa_compress(pb, ob["k_cmp"] // NSA_HEAD_DIM, pe, w1, w2)
    oc, sel = _nsa_select(pb, ob["q_nsa"] // NSA_WIDTH, pf, of["gates"] // LANES, kvc[0], kvc[1], pos_row,
                          pos_cmp_rep, tab_t, ov, n_slc=n_slc, tq=tiles["nsa_tq"])
    o_s = _nsa_slc(pb, ob["q_nsa"], ob["k_slc"], ob["v_slc"], sel, pos_row, pos_rep, tab_t, pq_min, pk_max,
                   tq=tiles["slc_tq"], tk=tiles["slc_tk"])
    y_nsa = _nsa_win(pb, ob["q_nsa"] // NSA_WIDTH, ob["k_win"] // NSA_KV, ob["v_win"] // NSA_KV,
                     pf, of["gates"] // LANES, of["g_nsa"] // NSA_WIDTH, oc, o_s, pos_row, pos_rep, tab_t,
                     tq=tiles["win_tq"])
    y_mem = _mem_attn(pb, ob["q_mem"] // MEM_HEAD_DIM, pf, of["g_mem"] // MEM_HEAD_DIM, mem_kv, t=tiles["mem_t"])
    return y_mla, y_nsa, y_mem


def _tiles(seq):
    return {"tm": min(seq, 1024), "s5_tc": min(seq, 1024), "decay_t": min(seq, 512), "attn_tq": min(seq, 256),
            "attn_tk": min(seq, 512), "attn_hp": 2, "slc_tq": 256, "slc_tk": 512, "nsa_tq": 256, "win_tq": 128,
            "mem_t": min(seq, 1024), "norm_t": min(seq, 512), "prep_t": min(seq, 256)}


def _context(positions, t5_table, seq, tiles):
    pos = positions[0]
    pos_col = pos.reshape(seq, 1)
    pos_row = pos.reshape(1, seq)
    pos_rep = jnp.broadcast_to(pos_col, (seq, LANES))
    pq_min = jnp.min(pos.reshape(seq // tiles["slc_tq"], tiles["slc_tq"]), axis=1)
    pk_max = jnp.max(pos.reshape(seq // tiles["slc_tk"], tiles["slc_tk"]), axis=1)
    nc = seq // NSA_CMP_STRIDE
    pos_cmp = jnp.pad(pos[NSA_CMP_LEN - 1::NSA_CMP_STRIDE], (0, 1))
    pos_cmp_rep = jnp.broadcast_to(pos_cmp.reshape(nc, 1), (nc, LANES))
    half = MLA_ROPE // 2
    inv_freq = ROPE_THETA ** (-jnp.arange(half, dtype=F32) / half)
    freq = jnp.concatenate([inv_freq, inv_freq, jnp.zeros((LANES - MLA_ROPE,), F32)]).reshape(1, LANES)
    tab_t = jnp.pad(t5_table.astype(F32).T, ((0, SUBLANES - NSA_HEADS), (0, LANES - T5_BUCKETS)))
    tab_t = tab_t.at[:, T5_MASK_BUCKET].set(NEG_INF)
    n_slc = seq // NSA_SLC_BLOCK
    cs = np.arange(nc) * NSA_CMP_STRIDE
    ss = np.arange(LANES) * NSA_SLC_BLOCK
    ov_np = np.clip(np.minimum(cs[:, None] + NSA_CMP_LEN, ss[None, :] + NSA_SLC_BLOCK)
                    - np.maximum(cs[:, None], ss[None, :]), 0, None) / NSA_CMP_LEN
    ov_np[nc - 1, :] = 0.0
    ov_np[:, n_slc:] = 0.0
    return {"pos": (pos_col.astype(F32), pos_row, pos_cmp_rep, pos_rep, pq_min, pk_max), "freq": freq,
            "tab_t": tab_t, "ov": jnp.asarray(ov_np.T, BF16), "n_slc": n_slc}


def _odd_params(i, mla_g_cq, mla_g_ckv, mla_w_uq, mla_w_ukv, nsa_cmp_pe, nsa_cmp_w1, nsa_cmp_w2, ctx):
    dq = MLA_NOPE + MLA_ROPE
    w_uq = mla_w_uq[i].reshape(MLA_Q_RANK, MLA_HEADS, dq)
    w_uq = jnp.pad(w_uq, ((0, 0), (0, 0), (0, 2 * LANES - dq))).reshape(MLA_Q_RANK, -1).astype(BF16)
    mla = (mla_g_cq[i], mla_g_ckv[i], w_uq, mla_w_ukv[i].astype(BF16), ctx["freq"])
    nsa = (nsa_cmp_pe[i].astype(F32), nsa_cmp_w1[i].astype(BF16), nsa_cmp_w2[i].astype(BF16), ctx["tab_t"],
           ctx["ov"], ctx["n_slc"])
    return mla, nsa


def kernel(x, mem, positions, norm_g, mem_norm_g, final_norm_g, t5_table, w_out, mem_w_kv, even_w_in, s5_lam_re,
           s5_lam_im, s5_log_dt, s5_b_re, s5_b_im, s5_c_re, s5_c_im, s5_d, s5_w_glu, fox_b_f, odd_w_in, mla_g_cq,
           mla_g_ckv, mla_w_uq, mla_w_ukv, nsa_cmp_pe, nsa_cmp_w1, nsa_cmp_w2):
    batch, seq, _ = x.shape
    assert batch == 1 and seq % 1024 == 0 and seq // NSA_SLC_BLOCK <= LANES
    depth = norm_g.shape[0]
    tiles = _tiles(seq)
    ctx = _context(positions, t5_table, seq, tiles)
    h = x[0]
    w_out_b = w_out.astype(BF16)
    mem_kv_all = _mem_kv_all(mem[0], mem_norm_g, mem_w_kv.astype(BF16))
    for layer in range(depth):
        i = layer // 2
        mem_kv = (mem_kv_all, layer)
        if layer % 2 == 0:
            w_in = (_reorder_w_in(even_w_in[i], EVEN_SPLITS, EVEN_B_ORDER),
                    _reorder_w_in(even_w_in[i], EVEN_SPLITS, EVEN_F_ORDER))
            b_cat, c_cat, tab = _s5_prepare(s5_lam_re[i], s5_lam_im[i], s5_log_dt[i], s5_b_re[i], s5_b_im[i],
                                            s5_c_re[i], s5_c_im[i])
            ys = _even_layer(h, mem_kv, norm_g[layer], w_in, (b_cat, c_cat, tab, s5_d[i]),
                             s5_w_glu[i].astype(BF16), fox_b_f[i], tiles)
        else:
            w_in = (_reorder_w_in(odd_w_in[i], ODD_SPLITS, ODD_B_ORDER),
                    _reorder_w_in(odd_w_in[i], ODD_SPLITS, ODD_F_ORDER))
            mla, nsa = _odd_params(i, mla_g_cq, mla_g_ckv, mla_w_uq, mla_w_ukv, nsa_cmp_pe, nsa_cmp_w1, nsa_cmp_w2,
                                   ctx)
            ys = _odd_layer(h, mem_kv, norm_g[layer], w_in, mla, nsa, ctx["pos"], tiles)
        h = _out_proj(h, ys, w_out_b, layer, tm=tiles["tm"], tn=512)
    return _final_norm(h, final_norm_g, tm=tiles["norm_t"])[None]
```

```python
import functools
import math

import numpy as np
import jax
import jax.numpy as jnp
from jax import lax
from jax.experimental import pallas as pl
from jax.experimental.pallas import tpu as pltpu

F32 = jnp.float32
BF16 = jnp.bfloat16
I32 = jnp.int32

D_MODEL = 2048
DEPTH = 4
N_MEM = 256
RMS_EPS = 1e-6
NEG_INF = -1e30
LOG2E = math.log2(math.e)

S5_WIDTH = 1024
S5_GROUP = 16
S5_GROUPS = S5_WIDTH // S5_GROUP
S5_STATE = 64
FOX_HEADS = 8
FOX_HEAD_DIM = 128
FOX_WIDTH = FOX_HEADS * FOX_HEAD_DIM
MEM_HEADS = 4
MEM_HEAD_DIM = 128
MEM_WIDTH = MEM_HEADS * MEM_HEAD_DIM
MLA_HEADS = 8
MLA_Q_RANK = 512
MLA_KV_RANK = 512
MLA_NOPE = 128
MLA_ROPE = 64
MLA_V = 128
MLA_WIDTH = MLA_HEADS * MLA_V
ROPE_THETA = 10000.0
NSA_HEADS = 8
NSA_KV_GROUPS = 2
NSA_REP = NSA_HEADS // NSA_KV_GROUPS
NSA_HEAD_DIM = 128
NSA_WIDTH = NSA_HEADS * NSA_HEAD_DIM
NSA_KV = NSA_KV_GROUPS * NSA_HEAD_DIM
NSA_CMP_LEN = 32
NSA_CMP_STRIDE = 16
NSA_CMP_HIDDEN = 256
NSA_SLC_BLOCK = 64
NSA_SLC_TOPK = 16
NSA_WINDOW = 512
FORCE_SCORE = 1e6
T5_BUCKETS = 32
T5_MAX_DIST = 1024

EVEN_SPLITS = (S5_WIDTH, S5_WIDTH, FOX_WIDTH, FOX_WIDTH, FOX_WIDTH, FOX_HEADS, FOX_WIDTH, MEM_WIDTH, MEM_WIDTH)
ODD_SPLITS = (MLA_Q_RANK, MLA_KV_RANK, MLA_ROPE, MLA_WIDTH, NSA_WIDTH, NSA_KV, NSA_KV, NSA_KV, NSA_KV, NSA_KV,
              NSA_KV, 3 * NSA_HEADS, NSA_WIDTH, MEM_WIDTH, MEM_WIDTH)

LANES = 128
SUBLANES = 8
VMEM_LIMIT_BYTES = 56 * 1024 * 1024

EVEN_B_ORDER = (("q", 2, 1024), ("k", 3, 1024), ("v", 4, 1024), ("q_mem", 7, 512))
EVEN_F_ORDER = (("u", 0, 1024), ("g_s5", 1, 1024), ("g_fox", 6, 1024), ("g_mem", 8, 512), ("f", 5, 128),
                ("pad", None, 128))
ODD_B_ORDER = (("q_nsa", 4, 1024), ("q_mem", 13, 512), ("k_cmp", 5, 256), ("v_cmp", 6, 256), ("k_slc", 7, 256),
               ("v_slc", 8, 256), ("k_win", 9, 256), ("v_win", 10, 256))
ODD_F_ORDER = (("g_nsa", 12, 1024), ("g_mla", 3, 1024), ("c_q", 0, 512), ("c_kv", 1, 512), ("g_mem", 14, 512),
               ("k_rope", 2, 128), ("gates", 11, 128))


def _layout(order):
    off, out = 0, {}
    for name, _, width in order:
        assert off % width == 0
        out[name] = off
        off += width
    return out, off


EVEN_B_OFF, EVEN_B_N = _layout(EVEN_B_ORDER)
EVEN_F_OFF, EVEN_F_N = _layout(EVEN_F_ORDER)
ODD_B_OFF, ODD_B_N = _layout(ODD_B_ORDER)
ODD_F_OFF, ODD_F_N = _layout(ODD_F_ORDER)


def _reorder_w_in(w, splits, order):
    starts = np.concatenate([[0], np.cumsum(splits)])
    cols = []
    for _, idx, width in order:
        if idx is None:
            cols.append(jnp.zeros((w.shape[0], width), w.dtype))
            continue
        seg = w[:, int(starts[idx]):int(starts[idx + 1])]
        pad = width - seg.shape[1]
        if pad:
            seg = jnp.pad(seg, ((0, 0), (0, pad)))
        cols.append(seg)
    return jnp.concatenate(cols, axis=1).astype(BF16)


def _cparams(*sem):
    return pltpu.CompilerParams(dimension_semantics=sem, vmem_limit_bytes=VMEM_LIMIT_BYTES)


def _silu(g):
    return g * jax.nn.sigmoid(g)


def _pick(n, cands):
    for c in cands:
        if n % c == 0:
            return c
    raise ValueError(f"no tile for {n} in {cands}")


def _norm_matmul_kernel(x_ref, g_ref, w_ref, o_ref, xn_ref):
    @pl.when(pl.program_id(1) == 0)
    def _():
        x = x_ref[...]
        ms = jnp.mean(x * x, axis=-1, keepdims=True)
        xn_ref[...] = (x * lax.rsqrt(ms + RMS_EPS) * g_ref[...]).astype(BF16)

    o_ref[...] = jnp.dot(xn_ref[...], w_ref[...], preferred_element_type=F32).astype(o_ref.dtype)


def _norm_matmul(x, g, w, *, x_cb=0, tm, tn, name, out_dtype=F32):
    m = x.shape[0]
    k, n = w.shape
    return pl.pallas_call(
        _norm_matmul_kernel,
        grid=(m // tm, n // tn),
        in_specs=[pl.BlockSpec((tm, k), lambda i, j: (i, x_cb)),
                  pl.BlockSpec((1, k), lambda i, j: (0, 0)),
                  pl.BlockSpec((k, tn), lambda i, j: (0, j))],
        out_specs=pl.BlockSpec((tm, tn), lambda i, j: (i, j)),
        out_shape=jax.ShapeDtypeStruct((m, n), out_dtype),
        scratch_shapes=[pltpu.VMEM((tm, k), BF16)],
        compiler_params=_cparams("parallel", "arbitrary"),
        name=name,
    )(x, g.reshape(1, k), w)


def _out_proj_kernel(h_ref, *refs):
    o_ref = refs[-1]
    n = (len(refs) - 1) // 2
    acc = h_ref[...]
    for y_ref, w_ref in zip(refs[:n], refs[n:2 * n]):
        acc = acc + jnp.dot(y_ref[...], w_ref[...], preferred_element_type=F32)
    o_ref[...] = acc


def _out_proj(h, ys, w_all, layer, *, tm, tn):
    m, n = h.shape
    in_specs = [pl.BlockSpec((tm, tn), lambda i, j: (i, j))]
    in_specs += [pl.BlockSpec((tm, y.shape[1]), lambda i, j: (i, 0)) for y in ys]
    row = 0
    for y in ys:
        width = y.shape[1]
        assert row % width == 0
        in_specs.append(pl.BlockSpec((None, width, tn), lambda i, j, rb=row // width: (layer, rb, j)))
        row += width
    return pl.pallas_call(
        _out_proj_kernel,
        grid=(m // tm, n // tn),
        in_specs=in_specs,
        out_specs=pl.BlockSpec((tm, tn), lambda i, j: (i, j)),
        out_shape=jax.ShapeDtypeStruct((m, n), F32),
        compiler_params=_cparams("parallel", "arbitrary"),
        name="out_proj",
    )(h, *ys, *([w_all] * len(ys)))


def _final_norm_kernel(x_ref, g_ref, o_ref):
    x = x_ref[...]
    ms = jnp.mean(x * x, axis=-1, keepdims=True)
    o_ref[...] = x * lax.rsqrt(ms + RMS_EPS) * g_ref[...]


def _final_norm(h, g, *, tm):
    m, n = h.shape
    return pl.pallas_call(
        _final_norm_kernel,
        grid=(m // tm,),
        in_specs=[pl.BlockSpec((tm, n), lambda i: (i, 0)), pl.BlockSpec((1, n), lambda i: (0, 0))],
        out_specs=pl.BlockSpec((tm, n), lambda i: (i, 0)),
        out_shape=jax.ShapeDtypeStruct((m, n), F32),
        compiler_params=_cparams("parallel"),
        name="final_norm",
    )(h, g.reshape(1, n))


ONES_ROWS = 16


def _build_vt(v_ref, vt_ref, hh, dv, seq, chunk):
    def body(c, carry):
        st = pl.multiple_of(c * chunk, chunk)
        vt_ref[hh, 0:dv, pl.ds(st, chunk)] = v_ref[pl.ds(st, chunk), hh * dv:(hh + 1) * dv].astype(F32).T.astype(BF16)
        return carry

    lax.fori_loop(0, seq // chunk, body, 0)
    vt_ref[hh, dv:dv + ONES_ROWS, :] = jnp.ones((ONES_ROWS, seq), BF16)


def _flash_kernel(*refs, scale, tq, tk, hp, dk, dv, has_decay):
    if has_decay:
        q_ref, k_ref, v_ref, g_ref, cq_ref, ck_ref, o_ref, vt_ref, qt_ref, s_ref, m_ref, acc_ref = refs
    else:
        q_ref, k_ref, v_ref, g_ref, o_ref, vt_ref, qt_ref, s_ref, m_ref, acc_ref = refs
    hb = pl.program_id(0)
    qi = pl.program_id(1)
    seq = k_ref.shape[0]

    @pl.when(qi == 0)
    def _():
        for hh in range(hp):
            _build_vt(v_ref, vt_ref, hh, dv, seq, tk)

    for hh in range(hp):
        qt_ref[hh] = (q_ref[:, hh * dk:(hh + 1) * dk].astype(F32) * (scale * LOG2E)).T.astype(BF16)
    m_ref[...] = jnp.full(m_ref.shape, NEG_INF, F32)
    acc_ref[...] = jnp.zeros(acc_ref.shape, F32)
    n_full = (qi * tq) // tk
    if has_decay:
        cq2 = [cq_ref[pl.ds(hb * hp + hh, 1), :] for hh in range(hp)]

    def scores(j, slot):
        start = pl.multiple_of(j * tk, tk)
        for hh in range(hp):
            s_ref[slot, hh] = jnp.dot(k_ref[pl.ds(start, tk), hh * dk:(hh + 1) * dk], qt_ref[hh],
                                      preferred_element_type=F32)

    def softmax_pv(j, slot, masked):
        start = pl.multiple_of(j * tk, tk)
        for hh in range(hp):
            t = s_ref[slot, hh]
            if has_decay:
                t = t - jnp.concatenate([ck_ref[hh, pl.ds(start, tk), :]] * (tq // LANES), axis=1)
            if masked:
                key = start + lax.broadcasted_iota(I32, (tk, tq), 0)
                qry = qi * tq + lax.broadcasted_iota(I32, (tk, tq), 1)
                t = jnp.where(key <= qry, t, NEG_INF)
            m_prev = m_ref[hh]
            mx = jnp.max(t, axis=0, keepdims=True)
            if has_decay:
                m_new = jnp.maximum(m_prev, mx + cq2[hh])
                shift = m_new - cq2[hh]
            else:
                m_new = jnp.maximum(m_prev, mx)
                shift = m_new
            alpha = jnp.exp2(m_prev - m_new)
            p = jnp.exp2(t - shift).astype(BF16)
            acc_ref[hh] = alpha * acc_ref[hh] + jnp.dot(vt_ref[hh, :, pl.ds(start, tk)], p,
                                                         preferred_element_type=F32)
            m_ref[hh] = m_new

    scores(0, 0)

    def pair(j):
        scores(j + 1, 1)
        softmax_pv(j, 0, False)
        scores(j + 2, 0)
        softmax_pv(j + 1, 1, False)

    def quad(qq, carry):
        pair(4 * qq)
        pair(4 * qq + 2)
        return carry

    quads = n_full // 4
    lax.fori_loop(0, quads, quad, 0)
    rem = n_full - 4 * quads

    @pl.when(rem >= 2)
    def _():
        pair(4 * quads)

    last = 4 * quads + 2 * (rem // 2)

    @pl.when(n_full % 2 == 1)
    def _():
        scores(last + 1, 1)
        softmax_pv(last, 0, False)
        softmax_pv(last + 1, 1, True)

    @pl.when(n_full % 2 == 0)
    def _():
        softmax_pv(last, 0, True)

    for hh in range(hp):
        a = acc_ref[hh]
        o = (a[0:dv, :] / a[dv:dv + 1, :]).T
        o_ref[:, hh * dv:(hh + 1) * dv] = (o * _silu(g_ref[:, hh * dv:(hh + 1) * dv])).astype(o_ref.dtype)


def _flash(q_arr, q_off, k_arr, k_off, v_arr, v_off, g_arr, g_off, *, heads, dk, dv, scale, tq, tk, hp,
           decay=None, name):
    seq = q_arr.shape[0]
    once = pl.Buffered(1)
    in_specs = [pl.BlockSpec((tq, hp * dk), lambda h, i: (i, q_off // (hp * dk) + h)),
                pl.BlockSpec((seq, hp * dk), lambda h, i: (0, k_off // (hp * dk) + h), pipeline_mode=once),
                pl.BlockSpec((seq, hp * dv), lambda h, i: (0, v_off // (hp * dv) + h), pipeline_mode=once),
                pl.BlockSpec((tq, hp * dv), lambda h, i: (i, g_off // (hp * dv) + h))]
    args = [q_arr, k_arr, v_arr, g_arr]
    if decay is not None:
        cum_t, cum_rep = decay
        in_specs += [pl.BlockSpec((SUBLANES, tq), lambda h, i: (0, i)),
                     pl.BlockSpec((hp, seq, LANES), lambda h, i: (h, 0, 0), pipeline_mode=once)]
        args += [cum_t, cum_rep]
    return pl.pallas_call(
        functools.partial(_flash_kernel, scale=scale, tq=tq, tk=tk, hp=hp, dk=dk, dv=dv, has_decay=decay is not None),
        grid=(heads // hp, seq // tq),
        in_specs=in_specs,
        out_specs=pl.BlockSpec((tq, hp * dv), lambda h, i: (i, h)),
        out_shape=jax.ShapeDtypeStruct((seq, heads * dv), BF16),
        scratch_shapes=[pltpu.VMEM((hp, dv + ONES_ROWS, seq), BF16), pltpu.VMEM((hp, dk, tq), BF16),
                        pltpu.VMEM((2, hp, tk, tq), F32), pltpu.VMEM((hp, 1, tq), F32),
                        pltpu.VMEM((hp, dv + ONES_ROWS, tq), F32)],
        compiler_params=_cparams("parallel", "arbitrary"),
        name=name,
    )(*args)


def _decay_kernel(f_ref, b_ref, ct_ref, cr_ref, carry_ref, *, t):
    i = pl.program_id(0)

    @pl.when(i == 0)
    def _():
        carry_ref[...] = jnp.zeros(carry_ref.shape, F32)

    x = f_ref[...] + b_ref[...]
    lf = jnp.minimum(x, 0.0) - jnp.log1p(jnp.exp(-jnp.abs(x)))
    row = lax.broadcasted_iota(I32, lf.shape, 0)
    s = 1
    while s < t:
        lf = lf + jnp.where(row >= s, pltpu.roll(lf, s, 0), 0.0)
        s *= 2
    lf = lf + carry_ref[...]
    carry_ref[...] = lf[t - 1:t, :]
    lf2 = lf * LOG2E
    ct_ref[...] = lf2.T[:FOX_HEADS, :]
    for h in range(FOX_HEADS):
        cr_ref[h] = jnp.broadcast_to(lf2[:, h:h + 1], (t, LANES))


def _decay(proj, f_cb, b_f, *, t):
    seq = proj.shape[0]
    b = jnp.pad(b_f.reshape(1, FOX_HEADS), ((0, 0), (0, LANES - FOX_HEADS)))
    return pl.pallas_call(
        functools.partial(_decay_kernel, t=t),
        grid=(seq // t,),
        in_specs=[pl.BlockSpec((t, LANES), lambda i: (i, f_cb)), pl.BlockSpec((1, LANES), lambda i: (0, 0))],
        out_specs=[pl.BlockSpec((FOX_HEADS, t), lambda i: (0, i)),
                   pl.BlockSpec((FOX_HEADS, t, LANES), lambda i: (0, i, 0))],
        out_shape=[jax.ShapeDtypeStruct((FOX_HEADS, seq), F32), jax.ShapeDtypeStruct((FOX_HEADS, seq, LANES), F32)],
        scratch_shapes=[pltpu.VMEM((1, LANES), F32)],
        compiler_params=_cparams("arbitrary"),
        name="fox_decay",
    )(proj, b)


def _mem_attn_kernel(q_ref, kv_ref, g_ref, o_ref):
    d = MEM_HEAD_DIM
    for h in range(MEM_HEADS):
        hs = slice(h * d, (h + 1) * d)
        k = kv_ref[:, hs].astype(BF16)
        v = kv_ref[:, MEM_WIDTH + h * d:MEM_WIDTH + (h + 1) * d].astype(BF16)
        s = lax.dot_general(q_ref[:, hs], k, (((1,), (1,)), ((), ())), preferred_element_type=F32) * (d ** -0.5)
        m = jnp.max(s, axis=1, keepdims=True)
        p = jnp.exp(s - m)
        l = jnp.sum(p, axis=1, keepdims=True)
        o = jnp.dot(p.astype(BF16), v, preferred_element_type=F32) / l
        o_ref[:, hs] = (o * _silu(g_ref[:, hs])).astype(o_ref.dtype)


def _mem_kv_all(mem2d, g, w_all):
    depth, k, n = w_all.shape
    m = mem2d.shape[0]
    tn = 512
    return pl.pallas_call(
        _norm_matmul_kernel,
        grid=(depth, n // tn),
        in_specs=[pl.BlockSpec((m, k), lambda l, j: (0, 0)),
                  pl.BlockSpec((1, k), lambda l, j: (0, 0)),
                  pl.BlockSpec((None, k, tn), lambda l, j: (l, 0, j))],
        out_specs=pl.BlockSpec((None, m, tn), lambda l, j: (l, 0, j)),
        out_shape=jax.ShapeDtypeStruct((depth, m, n), F32),
        scratch_shapes=[pltpu.VMEM((m, k), BF16)],
        compiler_params=_cparams("arbitrary", "arbitrary"),
        name="mem_kv",
    )(mem2d, g.reshape(1, k), w_all)


def _mem_attn(proj_b, q_cb, proj_f, g_cb, mem_kv, *, t):
    seq = proj_b.shape[0]
    mem_kv, layer = mem_kv
    nm = mem_kv.shape[1]
    w = MEM_WIDTH
    d = MEM_HEAD_DIM
    return pl.pallas_call(
        _mem_attn_kernel,
        grid=(seq // t,),
        in_specs=[pl.BlockSpec((t, w), lambda i: (i, q_cb * d // w)),
                  pl.BlockSpec((None, nm, 2 * w), lambda i: (layer, 0, 0)),
                  pl.BlockSpec((t, w), lambda i: (i, g_cb * d // w))],
        out_specs=pl.BlockSpec((t, w), lambda i: (i, 0)),
        out_shape=jax.ShapeDtypeStruct((seq, w), BF16),
        compiler_params=_cparams("parallel"),
        name="mem_attn",
    )(proj_b, mem_kv, proj_f)


S5_TILE_GROUPS = LANES // S5_GROUP
S5_TILE_STATES = S5_TILE_GROUPS * S5_STATE
S5_TILES = S5_GROUPS // S5_TILE_GROUPS


def _s5_scan_kernel(u_ref, b_ref, c_ref, d_ref, tab_ref, z_ref, bu_ref, carry_ref, *, tc):
    ns = S5_TILE_STATES

    @pl.when(pl.program_id(1) == 0)
    def _():
        carry_ref[...] = jnp.zeros(carry_ref.shape, F32)

    u = u_ref[...]
    bu_ref[...] = jnp.dot(u.astype(BF16), b_ref[...], preferred_element_type=F32)
    steps = [(1, tab_ref[0], tab_ref[1]), (2, tab_ref[2], tab_ref[3]), (4, tab_ref[4], tab_ref[5])]
    pr = tab_ref[6]
    pi = tab_ref[7]

    def body(i, carry):
        cr, ci = carry
        r0 = pl.multiple_of(i * SUBLANES, SUBLANES)
        xr = bu_ref[pl.ds(r0, SUBLANES), 0:ns]
        xi = bu_ref[pl.ds(r0, SUBLANES), ns:2 * ns]
        for s, ar, ai in steps:
            sr = pltpu.roll(xr, s, 0)
            si = pltpu.roll(xi, s, 0)
            xr, xi = xr + ar * sr - ai * si, xi + ar * si + ai * sr
        xr, xi = xr + pr * cr - pi * ci, xi + pr * ci + pi * cr
        bu_ref[pl.ds(r0, SUBLANES), 0:ns] = xr
        bu_ref[pl.ds(r0, SUBLANES), ns:2 * ns] = xi
        return xr[SUBLANES - 1:SUBLANES, :], xi[SUBLANES - 1:SUBLANES, :]

    cr, ci = lax.fori_loop(0, tc // SUBLANES, body, (carry_ref[0:1, 0:ns], carry_ref[0:1, ns:2 * ns]))
    carry_ref[0:1, 0:ns] = cr
    carry_ref[0:1, ns:2 * ns] = ci
    y = jnp.dot(bu_ref[...].astype(BF16), c_ref[...], preferred_element_type=F32) + d_ref[...] * u
    z_ref[...] = jax.nn.gelu(y)


def _s5_prepare(lam_re, lam_im, log_dt, b_re, b_im, c_re, c_im):
    dt = jnp.exp(log_dt.astype(F32))[:, None]
    lr = lam_re.astype(F32)
    li = lam_im.astype(F32)
    mag = jnp.exp(lr * dt)
    ab_re = mag * jnp.cos(li * dt)
    ab_im = mag * jnp.sin(li * dt)
    den = lr * lr + li * li
    nr = ab_re - 1.0
    f_re = (nr * lr + ab_im * li) / den
    f_im = (ab_im * lr - nr * li) / den
    br = b_re.astype(F32)
    bim = b_im.astype(F32)
    bb_re = f_re[..., None] * br - f_im[..., None] * bim
    bb_im = f_re[..., None] * bim + f_im[..., None] * br
    eye = jnp.eye(S5_TILE_GROUPS, dtype=F32)

    def blockdiag_in(bb):
        t = bb.reshape(S5_TILES, S5_TILE_GROUPS, S5_STATE, S5_GROUP)
        m = jnp.einsum("jgpc,gh->jgchp", t, eye)
        return m.reshape(S5_TILES, LANES, S5_TILE_STATES)

    def blockdiag_out(cc):
        t = cc.reshape(S5_TILES, S5_TILE_GROUPS, S5_GROUP, S5_STATE)
        m = jnp.einsum("jgcp,gh->jgphc", t, eye)
        return m.reshape(S5_TILES, S5_TILE_STATES, LANES)

    b_cat = jnp.concatenate([blockdiag_in(bb_re), blockdiag_in(bb_im)], axis=2).astype(BF16)
    c_cat = jnp.concatenate([blockdiag_out(c_re.astype(F32)), -blockdiag_out(c_im.astype(F32))], axis=1).astype(BF16)

    a_r = ab_re.reshape(S5_TILES, 1, S5_TILE_STATES)
    a_i = ab_im.reshape(S5_TILES, 1, S5_TILE_STATES)

    def cmul(xr, xi, yr, yi):
        return xr * yr - xi * yi, xr * yi + xi * yr

    a2 = cmul(a_r, a_i, a_r, a_i)
    a4 = cmul(*a2, *a2)
    row = jnp.arange(SUBLANES)[None, :, None]
    tabs = []
    for s, (pr_, pi_) in ((1, (a_r, a_i)), (2, a2), (4, a4)):
        tabs.append(jnp.where(row >= s, pr_, 0.0))
        tabs.append(jnp.where(row >= s, pi_, 0.0))
    pw = [(a_r, a_i)]
    for _ in range(SUBLANES - 1):
        pw.append(cmul(*pw[-1], a_r, a_i))
    tabs.append(jnp.concatenate([p[0] for p in pw], axis=1))
    tabs.append(jnp.concatenate([p[1] for p in pw], axis=1))
    tab = jnp.stack([jnp.broadcast_to(t, (S5_TILES, SUBLANES, S5_TILE_STATES)) for t in tabs], axis=1)
    return b_cat, c_cat, tab.astype(F32)


def _s5_scan(proj, u_cb, b_cat, c_cat, d_skip, tab, *, tc):
    seq = proj.shape[0]
    ns = S5_TILE_STATES
    d = d_skip.astype(F32).reshape(S5_TILES, 1, LANES)
    return pl.pallas_call(
        functools.partial(_s5_scan_kernel, tc=tc),
        grid=(S5_TILES, seq // tc),
        in_specs=[pl.BlockSpec((tc, LANES), lambda j, c: (c, u_cb + j)),
                  pl.BlockSpec((None, LANES, 2 * ns), lambda j, c: (j, 0, 0)),
                  pl.BlockSpec((None, 2 * ns, LANES), lambda j, c: (j, 0, 0)),
                  pl.BlockSpec((None, 1, LANES), lambda j, c: (j, 0, 0)),
                  pl.BlockSpec((None, 8, SUBLANES, ns), lambda j, c: (j, 0, 0, 0))],
        out_specs=pl.BlockSpec((tc, LANES), lambda j, c: (c, j)),
        out_shape=jax.ShapeDtypeStruct((seq, S5_WIDTH), F32),
        scratch_shapes=[pltpu.VMEM((tc, 2 * ns), F32), pltpu.VMEM((SUBLANES, 2 * ns), F32)],
        compiler_params=_cparams("parallel", "arbitrary"),
        name="s5_scan",
    )(proj, b_cat, c_cat, d, tab)


def _s5_glu_kernel(z_ref, w_ref, g_ref, o_ref, *, tn):
    j = pl.program_id(1)
    z = z_ref[...]
    a = jnp.dot(z.astype(BF16), w_ref[...], preferred_element_type=F32)
    zc = z_ref[:, pl.ds(pl.multiple_of(j * tn, tn), tn)]
    o_ref[...] = (zc * jax.nn.sigmoid(a) * _silu(g_ref[...])).astype(o_ref.dtype)


def _s5_glu(z, w_glu, proj, g_cb, *, tm, tn):
    seq, n = z.shape
    return pl.pallas_call(
        functools.partial(_s5_glu_kernel, tn=tn),
        grid=(seq // tm, n // tn),
        in_specs=[pl.BlockSpec((tm, n), lambda i, j: (i, 0)),
                  pl.BlockSpec((n, tn), lambda i, j: (0, j)),
                  pl.BlockSpec((tm, tn), lambda i, j: (i, g_cb * (n // tn) + j))],
        out_specs=pl.BlockSpec((tm, tn), lambda i, j: (i, j)),
        out_shape=jax.ShapeDtypeStruct((seq, n), BF16),
        compiler_params=_cparams("parallel", "arbitrary"),
        name="s5_glu",
    )(z, w_glu, proj)


def _rope_tables(pos, freq):
    ang = pos * freq
    lane = lax.broadcasted_iota(I32, ang.shape, 1)
    half = MLA_ROPE // 2
    cos = jnp.cos(ang)
    sin = jnp.sin(ang)
    c = jnp.where(lane < MLA_ROPE, cos, 0.0)
    s1 = jnp.where(lane < half, -sin, 0.0)
    s2 = jnp.where((lane >= half) & (lane < MLA_ROPE), sin, 0.0)
    return c, s1, s2


def _rope_apply(x, c, s1, s2):
    half = MLA_ROPE // 2
    return x * c + pltpu.roll(x, LANES - half, 1) * s1 + pltpu.roll(x, half, 1) * s2


def _mla_up_kernel(cq_ref, ckv_ref, kr_ref, pos_ref, freq_ref, gq_ref, gkv_ref, wq_ref, wkv_ref, q_ref, k_ref, v_ref):
    def normed(x_ref, g_ref):
        x = x_ref[...]
        ms = jnp.mean(x * x, axis=-1, keepdims=True)
        return (x * lax.rsqrt(ms + RMS_EPS) * g_ref[...]).astype(BF16)

    qf = jnp.dot(normed(cq_ref, gq_ref), wq_ref[...], preferred_element_type=F32)
    kvf = jnp.dot(normed(ckv_ref, gkv_ref), wkv_ref[...], preferred_element_type=F32)
    c, s1, s2 = _rope_tables(pos_ref[...], freq_ref[...])
    kr = _rope_apply(kr_ref[...], c, s1, s2).astype(BF16)
    for h in range(MLA_HEADS):
        b = 2 * LANES * h
        q_ref[:, b:b + LANES] = qf[:, b:b + LANES].astype(BF16)
        q_ref[:, b + LANES:b + 2 * LANES] = _rope_apply(qf[:, b + LANES:b + 2 * LANES], c, s1, s2).astype(BF16)
        k_ref[:, b:b + LANES] = kvf[:, b:b + LANES].astype(BF16)
        k_ref[:, b + LANES:b + 2 * LANES] = kr
        v_ref[:, LANES * h:LANES * (h + 1)] = kvf[:, b + LANES:b + 2 * LANES].astype(BF16)


def _mla_up(proj, cq_cb, ckv_cb, kr_cb, g_cq, g_ckv, w_uq, w_ukv, pos_col, freq, *, t):
    seq = proj.shape[0]
    w = 2 * LANES * MLA_HEADS
    rq, rkv = w_uq.shape[0], w_ukv.shape[0]
    return pl.pallas_call(
        _mla_up_kernel,
        grid=(seq // t,),
        in_specs=[pl.BlockSpec((t, rq), lambda i: (i, cq_cb)),
                  pl.BlockSpec((t, rkv), lambda i: (i, ckv_cb)),
                  pl.BlockSpec((t, LANES), lambda i: (i, kr_cb)),
                  pl.BlockSpec((t, 1), lambda i: (i, 0)),
                  pl.BlockSpec((1, LANES), lambda i: (0, 0)),
                  pl.BlockSpec((1, rq), lambda i: (0, 0)),
                  pl.BlockSpec((1, rkv), lambda i: (0, 0)),
                  pl.BlockSpec((rq, w), lambda i: (0, 0)),
                  pl.BlockSpec((rkv, w), lambda i: (0, 0))],
        out_specs=[pl.BlockSpec((t, w), lambda i: (i, 0)),
                   pl.BlockSpec((t, w), lambda i: (i, 0)),
                   pl.BlockSpec((t, MLA_WIDTH), lambda i: (i, 0))],
        out_shape=[jax.ShapeDtypeStruct((seq, w), BF16), jax.ShapeDtypeStruct((seq, w), BF16),
                   jax.ShapeDtypeStruct((seq, MLA_WIDTH), BF16)],
        compiler_params=_cparams("parallel"),
        name="mla_up",
    )(proj, proj, proj, pos_col, freq, g_cq.reshape(1, rq), g_ckv.reshape(1, rkv), w_uq, w_ukv)


def _t5_bucket(dist):
    n = jnp.maximum(dist, 0)
    max_exact = T5_BUCKETS // 2
    log_ratio = jnp.log(jnp.maximum(n, 1).astype(F32) / max_exact) / math.log(T5_MAX_DIST / max_exact)
    large = jnp.minimum(max_exact + (log_ratio * (T5_BUCKETS - max_exact)).astype(I32), T5_BUCKETS - 1)
    return jnp.where(n < max_exact, n, large)


T5_MASK_BUCKET = T5_BUCKETS


def _t5_lookup(table_row, bucket):
    rows, width = bucket.shape
    tab = jnp.broadcast_to(table_row, (rows, LANES))
    parts = [jnp.take_along_axis(tab, bucket[:, c:c + LANES], axis=1, mode="promise_in_bounds")
             for c in range(0, width, LANES)]
    return parts[0] if len(parts) == 1 else jnp.concatenate(parts, axis=1)


def _nsa_cmp_kernel(x_ref, pe_ref, w1_ref, w2_ref, o_ref, xf_ref, *, nc):
    half = NSA_CMP_LEN // 2
    d = NSA_HEAD_DIM
    xf_ref[...] = x_ref[...].astype(F32)
    u = jnp.zeros((nc, NSA_CMP_HIDDEN), F32)
    v = jnp.zeros((nc, NSA_CMP_HIDDEN), F32)
    for r in range(half):
        a = xf_ref[pl.ds(r, nc, stride=NSA_CMP_STRIDE), :]
        u = u + jnp.dot((a + pe_ref[r:r + 1, :]).astype(BF16), w1_ref[r * d:(r + 1) * d, :],
                        preferred_element_type=F32)
        v = v + jnp.dot((a + pe_ref[half + r:half + r + 1, :]).astype(BF16),
                        w1_ref[(half + r) * d:(half + r + 1) * d, :], preferred_element_type=F32)
    hid = u + pltpu.roll(v, nc - 1, 0)
    o_ref[...] = jnp.dot(jax.nn.gelu(hid).astype(BF16), w2_ref[...], preferred_element_type=F32).astype(o_ref.dtype)


def _nsa_compress(proj, k_cb, pe, w1, w2):
    seq = proj.shape[0]
    nc = seq // NSA_CMP_STRIDE
    d = NSA_HEAD_DIM
    g = NSA_KV_GROUPS
    return pl.pallas_call(
        functools.partial(_nsa_cmp_kernel, nc=nc),
        grid=(2, g),
        in_specs=[pl.BlockSpec((seq, d), lambda a, b: (0, k_cb + a * g + b)),
                  pl.BlockSpec((None, NSA_CMP_LEN, d), lambda a, b: (a, 0, 0)),
                  pl.BlockSpec((None, NSA_CMP_LEN * d, NSA_CMP_HIDDEN), lambda a, b: (a, 0, 0)),
                  pl.BlockSpec((None, NSA_CMP_HIDDEN, d), lambda a, b: (a, 0, 0))],
        out_specs=pl.BlockSpec((None, None, nc, d), lambda a, b: (a, b, 0, 0)),
        out_shape=jax.ShapeDtypeStruct((2, g, nc, d), BF16),
        scratch_shapes=[pltpu.VMEM((seq, d), F32)],
        compiler_params=_cparams("parallel", "arbitrary"),
        name="nsa_compress",
    )(proj, pe, w1, w2)


def _nsa_select_kernel(consec_ref, q_ref, kc_ref, vc_ref, posq_ref, posc_ref, tab_ref, gate_ref, ov_ref,
                       oc_ref, sel_ref, strip_ref, *, nc, n_slc, tq):
    qi = pl.program_id(0)
    d = NSA_HEAD_DIM
    c = d ** -0.5 * LOG2E
    per_block = tq // NSA_CMP_STRIDE
    lane = lax.broadcasted_iota(I32, (1, tq), 1)
    tok = qi * tq + lane

    def select(bias):
        gates = jax.nn.sigmoid(gate_ref[...])
        ovt = ov_ref[...]
        js = lax.broadcasted_iota(I32, (LANES, tq), 0)
        jf = js.astype(F32)
        cur = tok // NSA_SLC_BLOCK
        forced = (js == 0) | (js == cur) | (js == cur - 1)
        for g in range(NSA_KV_GROUPS):
            kc = kc_ref[g]
            vct = vc_ref[g].astype(F32).T.astype(BF16)
            psum = jnp.zeros((nc, tq), F32)
            for r in range(NSA_REP):
                h = g * NSA_REP + r
                qt = (q_ref[:, h * d:(h + 1) * d].astype(F32) * c).T.astype(BF16)
                t = jnp.dot(kc, qt, preferred_element_type=F32) + bias(h)
                m = jnp.max(t, axis=0, keepdims=True)
                e = jnp.exp2(t - m)
                l = jnp.sum(e, axis=0, keepdims=True)
                p = e * jnp.where(m > 0.5 * NEG_INF, 1.0 / l, 0.0)
                o = jnp.dot(vct, p.astype(BF16), preferred_element_type=F32)
                oc_ref[:, h * d:(h + 1) * d] = gates[:, 3 * h:3 * h + 1] * o.T
                psum = psum + p
            p_hi = psum.astype(BF16)
            p_lo = (psum - p_hi.astype(F32)).astype(BF16)
            imp = jnp.dot(ovt, p_hi, preferred_element_type=F32) + jnp.dot(ovt, p_lo, preferred_element_type=F32)
            st = jnp.where(forced, FORCE_SCORE, jnp.where(js > cur, -1.0, imp))
            st = jnp.where(js < n_slc, st, -2.0)
            sel = jnp.zeros((LANES, tq), F32)
            for _ in range(NSA_SLC_TOPK):
                mx = jnp.max(st, axis=0, keepdims=True)
                first = jnp.min(jnp.where(st == mx, jf, float(LANES)), axis=0, keepdims=True)
                hit = jf == first
                sel = jnp.where(hit, 1.0, sel)
                st = jnp.where(hit, -3e38, st)
            sel_ref[g] = sel.astype(sel_ref.dtype)

    consecutive = consec_ref[0] == 1

    @pl.when(jnp.logical_and(consecutive, qi == 0))
    def _():
        for chunk in range(2 * nc // LANES):
            rel = chunk * LANES - nc + lax.broadcasted_iota(I32, (LANES, 1), 0)
            dist = lane - (rel * NSA_CMP_STRIDE + (NSA_CMP_LEN - 1))
            bucket = jnp.where(dist >= 0, _t5_bucket(dist), T5_MASK_BUCKET)
            for h in range(NSA_HEADS):
                strip_ref[h, chunk * LANES:(chunk + 1) * LANES, :] = _t5_lookup(tab_ref[h:h + 1, :], bucket)

    @pl.when(consecutive)
    def _():
        start = pl.multiple_of(nc - per_block * qi, per_block)
        select(lambda h: strip_ref[h, pl.ds(start, nc), :])

    @pl.when(jnp.logical_not(consecutive))
    def _():
        cmp_end = lax.broadcasted_iota(I32, (nc, 1), 0) * NSA_CMP_STRIDE + (NSA_CMP_LEN - 1)
        pos_c = jnp.concatenate([posc_ref[...]] * (tq // LANES), axis=1)
        bucket = jnp.where(cmp_end <= tok, _t5_bucket(posq_ref[...] - pos_c), T5_MASK_BUCKET)
        select(lambda h: _t5_lookup(tab_ref[h:h + 1, :], bucket))


def _nsa_select(proj_b, q_cb, proj_f, gate_cb, kc, vc, pos_row, pos_cmp_rep, consec, tab_t, ov_t, *, n_slc, tq):
    seq = proj_b.shape[0]
    nc = kc.shape[1]
    g = NSA_KV_GROUPS
    d = NSA_HEAD_DIM
    grid_spec = pltpu.PrefetchScalarGridSpec(
        num_scalar_prefetch=1,
        grid=(seq // tq,),
        in_specs=[pl.BlockSpec((tq, NSA_WIDTH), lambda i, *_: (i, q_cb)),
                  pl.BlockSpec((g, nc, d), lambda i, *_: (0, 0, 0)),
                  pl.BlockSpec((g, nc, d), lambda i, *_: (0, 0, 0)),
                  pl.BlockSpec((1, tq), lambda i, *_: (0, i)),
                  pl.BlockSpec((nc, LANES), lambda i, *_: (0, 0)),
                  pl.BlockSpec((SUBLANES, LANES), lambda i, *_: (0, 0)),
                  pl.BlockSpec((tq, LANES), lambda i, *_: (i, gate_cb)),
                  pl.BlockSpec((LANES, nc), lambda i, *_: (0, 0))],
        out_specs=[pl.BlockSpec((tq, NSA_WIDTH), lambda i, *_: (i, 0)),
                   pl.BlockSpec((g, LANES, tq), lambda i, *_: (0, 0, i))],
        scratch_shapes=[pltpu.VMEM((NSA_HEADS, 2 * nc, tq), F32)])
    return pl.pallas_call(
        functools.partial(_nsa_select_kernel, nc=nc, n_slc=n_slc, tq=tq),
        grid_spec=grid_spec,
        out_shape=[jax.ShapeDtypeStruct((seq, NSA_WIDTH), F32),
                   jax.ShapeDtypeStruct((g, LANES, seq), F32)],
        compiler_params=_cparams("arbitrary"),
        name="nsa_select",
    )(consec, proj_b, kc, vc, pos_row, pos_cmp_rep, tab_t * LOG2E, proj_f, ov_t)


def _nsa_slc_kernel(pqmin_ref, pkmax_ref, q_ref, k_ref, v_ref, sel_ref, posq_ref, posk_ref, tab_ref, o_ref,
                    vt_ref, qt_ref, s_ref, m_ref, acc_ref, *, tq, tk):
    g = pl.program_id(0)
    qi = pl.program_id(1)
    d = NSA_HEAD_DIM
    scale = d ** -0.5
    seq = k_ref.shape[0]

    @pl.when(qi == 0)
    def _():
        _build_vt(v_ref, vt_ref, 0, d, seq, tk)

    for r in range(NSA_REP):
        qt_ref[r] = (q_ref[:, r * d:(r + 1) * d].astype(F32) * (scale * LOG2E)).T.astype(BF16)
    m_ref[...] = jnp.full(m_ref.shape, NEG_INF, F32)
    acc_ref[...] = jnp.zeros(acc_ref.shape, F32)
    pos_q = posq_ref[...]
    per_tile = tk // NSA_SLC_BLOCK
    n_full = (qi * tq) // tk

    def scores(j, slot):
        start = pl.multiple_of(j * tk, tk)
        k = k_ref[pl.ds(start, tk), :]
        for r in range(NSA_REP):
            s_ref[slot, r] = jnp.dot(k, qt_ref[r], preferred_element_type=F32)

    def softmax_pv(j, slot, masked, near):
        start = pl.multiple_of(j * tk, tk)
        picked = jnp.concatenate(
            [jnp.broadcast_to(sel_ref[pl.ds(j * per_tile + b, 1), :], (NSA_SLC_BLOCK, tq)) for b in range(per_tile)],
            axis=0)
        if masked:
            key = start + lax.broadcasted_iota(I32, (tk, tq), 0)
            qry = qi * tq + lax.broadcasted_iota(I32, (tk, tq), 1)
            picked = jnp.where(key <= qry, picked, 0.0)
        mask = picked > 0.5
        if near:
            pos_k = jnp.concatenate([posk_ref[pl.ds(start, tk), :]] * (tq // LANES), axis=1)
            bucket = jnp.where(mask, _t5_bucket(pos_q - pos_k), T5_MASK_BUCKET)
        for r in range(NSA_REP):
            h = g * NSA_REP + r
            t = s_ref[slot, r]
            if near:
                t = t + _t5_lookup(tab_ref[pl.ds(h, 1), :], bucket)
            else:
                t = jnp.where(mask, t, NEG_INF)
            m_prev = m_ref[r]
            mx = jnp.max(t, axis=0, keepdims=True)
            if near:
                m_new = jnp.maximum(m_prev, mx)
                shift = m_new
            else:
                b = tab_ref[pl.ds(h, 1), T5_BUCKETS - 1:T5_BUCKETS]
                m_new = jnp.maximum(m_prev, mx + b)
                shift = m_new - b
            alpha = jnp.exp2(m_prev - m_new)
            p = jnp.exp2(t - shift).astype(BF16)
            acc_ref[r] = alpha * acc_ref[r] + jnp.dot(vt_ref[0, :, pl.ds(start, tk)], p, preferred_element_type=F32)
            m_ref[r] = m_new

    def is_far(j):
        return pqmin_ref[qi] - pkmax_ref[j] >= T5_MAX_DIST

    def stages(j, count, near):
        for k in range(count):
            scores(j + k + 1, (k + 1) % 2)
            softmax_pv(j + k, k % 2, False, near)

    scores(0, 0)

    def pair(jj, carry):
        j = 2 * jj
        far = jnp.logical_and(is_far(j), is_far(j + 1))

        @pl.when(far)
        def _():
            stages(j, 2, False)

        @pl.when(jnp.logical_not(far))
        def _():
            stages(j, 2, True)

        return carry

    pairs = n_full // 2
    lax.fori_loop(0, pairs, pair, 0)
    last = 2 * pairs

    @pl.when(n_full % 2 == 1)
    def _():
        far = is_far(last)

        @pl.when(far)
        def _():
            stages(last, 1, False)

        @pl.when(jnp.logical_not(far))
        def _():
            stages(last, 1, True)

        softmax_pv(last + 1, 1, True, True)

    @pl.when(n_full % 2 == 0)
    def _():
        softmax_pv(last, 0, True, True)

    for r in range(NSA_REP):
        a = acc_ref[r]
        o_ref[:, r * d:(r + 1) * d] = (a[0:d, :] / a[d:d + 1, :]).T


def _nsa_slc(proj_b, q_off, k_off, v_off, sel, pos_row, pos_rep, tab_t, pq_min, pk_max, *, tq, tk):
    seq = proj_b.shape[0]
    tab_t = tab_t * LOG2E
    d = NSA_HEAD_DIM
    gw = NSA_REP * d
    once = pl.Buffered(1)
    grid_spec = pltpu.PrefetchScalarGridSpec(
        num_scalar_prefetch=2,
        grid=(NSA_KV_GROUPS, seq // tq),
        in_specs=[pl.BlockSpec((tq, gw), lambda g, i, *_: (i, q_off // gw + g)),
                  pl.BlockSpec((seq, d), lambda g, i, *_: (0, k_off // d + g), pipeline_mode=once),
                  pl.BlockSpec((seq, d), lambda g, i, *_: (0, v_off // d + g), pipeline_mode=once),
                  pl.BlockSpec((None, LANES, tq), lambda g, i, *_: (g, 0, i)),
                  pl.BlockSpec((1, tq), lambda g, i, *_: (0, i)),
                  pl.BlockSpec((seq, LANES), lambda g, i, *_: (0, 0), pipeline_mode=once),
                  pl.BlockSpec((SUBLANES, LANES), lambda g, i, *_: (0, 0))],
        out_specs=pl.BlockSpec((tq, gw), lambda g, i, *_: (i, g)),
        scratch_shapes=[pltpu.VMEM((1, d + ONES_ROWS, seq), BF16), pltpu.VMEM((NSA_REP, d, tq), BF16),
                        pltpu.VMEM((2, NSA_REP, tk, tq), F32), pltpu.VMEM((NSA_REP, 1, tq), F32),
                        pltpu.VMEM((NSA_REP, d + ONES_ROWS, tq), F32)])
    return pl.pallas_call(
        functools.partial(_nsa_slc_kernel, tq=tq, tk=tk),
        grid_spec=grid_spec,
        out_shape=jax.ShapeDtypeStruct((seq, NSA_WIDTH), F32),
        compiler_params=_cparams("parallel", "arbitrary"),
        name="nsa_slc",
    )(pq_min, pk_max, proj_b, proj_b, proj_b, sel, pos_row, pos_rep, tab_t)


def _nsa_win_kernel(*refs, tq, nt):
    consec_ref = refs[0]
    q_ref = refs[1]
    k_refs = refs[2:2 + nt]
    v_refs = refs[2 + nt:2 + 2 * nt]
    pk_refs = refs[2 + 2 * nt:2 + 3 * nt]
    posq_ref, tab_ref, gate_ref, gout_ref, oc_ref, os_ref, o_ref, bias_ref = refs[2 + 3 * nt:]
    qi = pl.program_id(0)
    d = NSA_HEAD_DIM
    c = d ** -0.5 * LOG2E
    sub = lax.broadcasted_iota(I32, (tq, 1), 0)
    lane = lax.broadcasted_iota(I32, (1, tq), 1)

    def band_bucket(kidx, tok, dist):
        lower = jnp.maximum(tok - (NSA_WINDOW - 1), 0)
        b = jnp.where(kidx >= lower, _t5_bucket(dist), T5_MASK_BUCKET)
        return jnp.where(kidx <= tok, b, T5_MASK_BUCKET)

    def attend(bias):
        gates = jax.nn.sigmoid(gate_ref[...])
        for g in range(NSA_KV_GROUPS):
            ks = [kr[:, g * d:(g + 1) * d] for kr in k_refs]
            vts = [vr[:, g * d:(g + 1) * d].astype(F32).T.astype(BF16) for vr in v_refs]
            for r in range(NSA_REP):
                h = g * NSA_REP + r
                hs = slice(h * d, (h + 1) * d)
                qt = (q_ref[:, hs].astype(F32) * c).T.astype(BF16)
                ts = [jnp.dot(ks[jj], qt, preferred_element_type=F32) + bias(h, jj) for jj in range(nt)]
                m = functools.reduce(jnp.maximum, [jnp.max(t, axis=0, keepdims=True) for t in ts])
                ps = [jnp.exp2(t - m) for t in ts]
                l = functools.reduce(jnp.add, [jnp.sum(p, axis=0, keepdims=True) for p in ps])
                o_t = functools.reduce(jnp.add, [jnp.dot(vt, p.astype(BF16), preferred_element_type=F32)
                                                 for vt, p in zip(vts, ps)])
                o_w = (o_t / l).T
                o = (oc_ref[:, hs] + gates[:, 3 * h + 1:3 * h + 2] * os_ref[:, hs]
                     + gates[:, 3 * h + 2:3 * h + 3] * o_w)
                o_ref[:, hs] = (o * _silu(gout_ref[:, hs])).astype(o_ref.dtype)

    consecutive = consec_ref[0] == 1

    @pl.when(jnp.logical_and(consecutive, qi == 0))
    def _():
        tok0 = (nt - 1) * tq + lane
        for jj in range(nt):
            kidx0 = jj * tq + sub
            bucket = band_bucket(kidx0, tok0, tok0 - kidx0)
            for h in range(NSA_HEADS):
                bias_ref[h, jj] = _t5_lookup(tab_ref[h:h + 1, :], bucket)

    cached = jnp.logical_and(consecutive, qi >= nt - 1)

    @pl.when(cached)
    def _():
        attend(lambda h, jj: bias_ref[h, jj])

    @pl.when(jnp.logical_not(cached))
    def _():
        tok = qi * tq + lane
        pos_q = posq_ref[...]
        buckets = []
        for jj in range(nt):
            kidx = (qi - (nt - 1) + jj) * tq + sub
            pos_k = jnp.concatenate([pk_refs[jj][...]] * (tq // LANES), axis=1)
            buckets.append(band_bucket(kidx, tok, pos_q - pos_k))
        attend(lambda h, jj: _t5_lookup(tab_ref[h:h + 1, :], buckets[jj]))


def _nsa_win(proj_b, q_cb, k_cb, v_cb, proj_f, gate_cb, gout_cb, oc, o_s, pos_row, pos_rep, consec, tab_t, *, tq):
    seq = proj_b.shape[0]
    nt = NSA_WINDOW // tq + 1

    def band_rows(jj, cb):
        return pl.BlockSpec((tq, NSA_KV), lambda i, *_: (jnp.maximum(i - (nt - 1) + jj, 0), cb))

    def band_pos(jj):
        return pl.BlockSpec((tq, LANES), lambda i, *_: (jnp.maximum(i - (nt - 1) + jj, 0), 0))

    in_specs = [pl.BlockSpec((tq, NSA_WIDTH), lambda i, *_: (i, q_cb))]
    in_specs += [band_rows(jj, k_cb) for jj in range(nt)]
    in_specs += [band_rows(jj, v_cb) for jj in range(nt)]
    in_specs += [band_pos(jj) for jj in range(nt)]
    in_specs += [pl.BlockSpec((1, tq), lambda i, *_: (0, i)),
                 pl.BlockSpec((SUBLANES, LANES), lambda i, *_: (0, 0)),
                 pl.BlockSpec((tq, LANES), lambda i, *_: (i, gate_cb)),
                 pl.BlockSpec((tq, NSA_WIDTH), lambda i, *_: (i, gout_cb)),
                 pl.BlockSpec((tq, NSA_WIDTH), lambda i, *_: (i, 0)),
                 pl.BlockSpec((tq, NSA_WIDTH), lambda i, *_: (i, 0))]
    args = [proj_b] * (1 + 2 * nt) + [pos_rep] * nt + [pos_row, tab_t * LOG2E, proj_f, proj_f, oc, o_s]
    grid_spec = pltpu.PrefetchScalarGridSpec(
        num_scalar_prefetch=1,
        grid=(seq // tq,),
        in_specs=in_specs,
        out_specs=pl.BlockSpec((tq, NSA_WIDTH), lambda i, *_: (i, 0)),
        scratch_shapes=[pltpu.VMEM((NSA_HEADS, nt, tq, tq), F32)])
    return pl.pallas_call(
        functools.partial(_nsa_win_kernel, tq=tq, nt=nt),
        grid_spec=grid_spec,
        out_shape=jax.ShapeDtypeStruct((seq, NSA_WIDTH), BF16),
        compiler_params=_cparams("arbitrary"),
        name="nsa_win",
    )(consec, *args)


def _in_proj(h, norm_g, w_b, w_f, tiles, name):
    proj_b = _norm_matmul(h, norm_g, w_b, tm=tiles["tm"], tn=512, name=name + "_b", out_dtype=BF16)
    proj_f = _norm_matmul(h, norm_g, w_f, tm=tiles["tm"], tn=768, name=name + "_f")
    return proj_b, proj_f


def _even_layer(h, mem_kv, norm_g, w_in, s5, w_glu, b_f, tiles):
    pb, pf = _in_proj(h, norm_g, *w_in, tiles, "in_proj_even")
    ob, of = EVEN_B_OFF, EVEN_F_OFF
    b_cat, c_cat, tab, d_skip = s5
    z = _s5_scan(pf, of["u"] // LANES, b_cat, c_cat, d_skip, tab, tc=tiles["s5_tc"])
    y_s5 = _s5_glu(z, w_glu, pf, of["g_s5"] // S5_WIDTH, tm=tiles["tm"], tn=512)
    decay = _decay(pf, of["f"] // LANES, b_f, t=tiles["decay_t"])
    d = FOX_HEAD_DIM
    y_fox = _flash(pb, ob["q"], pb, ob["k"], pb, ob["v"], pf, of["g_fox"], heads=FOX_HEADS, dk=d, dv=d,
                   scale=d ** -0.5, tq=tiles["attn_tq"], tk=tiles["attn_tk"], hp=tiles["attn_hp"], decay=decay,
                   name="fox_attn")
    y_mem = _mem_attn(pb, ob["q_mem"] // MEM_HEAD_DIM, pf, of["g_mem"] // MEM_HEAD_DIM, mem_kv, t=tiles["mem_t"])
    return y_s5, y_fox, y_mem


def _odd_layer(h, mem_kv, norm_g, w_in, mla, nsa, pos, tiles):
    pb, pf = _in_proj(h, norm_g, *w_in, tiles, "in_proj_odd")
    ob, of = ODD_B_OFF, ODD_F_OFF
    g_cq, g_ckv, w_uq, w_ukv, freq = mla
    pos_col_f, pos_row, pos_cmp_rep, pos_rep, pq_min, pk_max, consec = pos
    q_r, k_r, v_r = _mla_up(pf, of["c_q"] // MLA_Q_RANK, of["c_kv"] // MLA_KV_RANK, of["k_rope"] // LANES,
                            g_cq, g_ckv, w_uq, w_ukv, pos_col_f, freq, t=tiles["prep_t"])
    y_mla = _flash(q_r, 0, k_r, 0, v_r, 0, pf, of["g_mla"], heads=MLA_HEADS, dk=2 * LANES, dv=MLA_V,
                   scale=(MLA_NOPE + MLA_ROPE) ** -0.5, tq=tiles["attn_tq"], tk=tiles["attn_tk"], hp=tiles["attn_hp"],
                   name="mla_attn")
    pe, w1, w2, tab_t, ov, n_slc = nsa
    kvc = _nsa_compress(pb, ob["k_cmp"] // NSA_HEAD_DIM, pe, w1, w2)
    oc, sel = _nsa_select(pb, ob["q_nsa"] // NSA_WIDTH, pf, of["gates"] // LANES, kvc[0], kvc[1], pos_row,
                          pos_cmp_rep, consec, tab_t, ov, n_slc=n_slc, tq=tiles["nsa_tq"])
    o_s = _nsa_slc(pb, ob["q_nsa"], ob["k_slc"], ob["v_slc"], sel, pos_row, pos_rep, tab_t, pq_min, pk_max,
                   tq=tiles["slc_tq"], tk=tiles["slc_tk"])
    y_nsa = _nsa_win(pb, ob["q_nsa"] // NSA_WIDTH, ob["k_win"] // NSA_KV, ob["v_win"] // NSA_KV,
                     pf, of["gates"] // LANES, of["g_nsa"] // NSA_WIDTH, oc, o_s, pos_row, pos_rep, consec, tab_t,
                     tq=tiles["win_tq"])
    y_mem = _mem_attn(pb, ob["q_mem"] // MEM_HEAD_DIM, pf, of["g_mem"] // MEM_HEAD_DIM, mem_kv, t=tiles["mem_t"])
    return y_mla, y_nsa, y_mem


def _tiles(seq):
    return {"tm": min(seq, 1024), "s5_tc": min(seq, 1024), "decay_t": min(seq, 512), "attn_tq": min(seq, 256),
            "attn_tk": min(seq, 512), "attn_hp": 2, "slc_tq": 256, "slc_tk": 512, "nsa_tq": 256, "win_tq": 128,
            "mem_t": min(seq, 1024), "norm_t": min(seq, 512), "prep_t": min(seq, 256)}


def _context(positions, t5_table, seq, tiles):
    pos = positions[0]
    pos_col = pos.reshape(seq, 1)
    pos_row = pos.reshape(1, seq)
    pos_rep = jnp.broadcast_to(pos_col, (seq, LANES))
    pq_min = jnp.min(pos.reshape(seq // tiles["slc_tq"], tiles["slc_tq"]), axis=1)
    pk_max = jnp.max(pos.reshape(seq // tiles["slc_tk"], tiles["slc_tk"]), axis=1)
    nc = seq // NSA_CMP_STRIDE
    pos_cmp = jnp.pad(pos[NSA_CMP_LEN - 1::NSA_CMP_STRIDE], (0, 1))
    pos_cmp_rep = jnp.broadcast_to(pos_cmp.reshape(nc, 1), (nc, LANES))
    half = MLA_ROPE // 2
    inv_freq = ROPE_THETA ** (-jnp.arange(half, dtype=F32) / half)
    freq = jnp.concatenate([inv_freq, inv_freq, jnp.zeros((LANES - MLA_ROPE,), F32)]).reshape(1, LANES)
    tab_t = jnp.pad(t5_table.astype(F32).T, ((0, SUBLANES - NSA_HEADS), (0, LANES - T5_BUCKETS)))
    tab_t = tab_t.at[:, T5_MASK_BUCKET].set(NEG_INF)
    n_slc = seq // NSA_SLC_BLOCK
    cs = np.arange(nc) * NSA_CMP_STRIDE
    ss = np.arange(LANES) * NSA_SLC_BLOCK
    ov_np = np.clip(np.minimum(cs[:, None] + NSA_CMP_LEN, ss[None, :] + NSA_SLC_BLOCK)
                    - np.maximum(cs[:, None], ss[None, :]), 0, None) / NSA_CMP_LEN
    ov_np[nc - 1, :] = 0.0
    ov_np[:, n_slc:] = 0.0
    consec = jnp.all(pos[1:] - pos[:-1] == 1).astype(I32).reshape(1)
    return {"pos": (pos_col.astype(F32), pos_row, pos_cmp_rep, pos_rep, pq_min, pk_max, consec), "freq": freq,
            "tab_t": tab_t, "ov": jnp.asarray(ov_np.T, BF16), "n_slc": n_slc}


def _odd_params(i, mla_g_cq, mla_g_ckv, mla_w_uq, mla_w_ukv, nsa_cmp_pe, nsa_cmp_w1, nsa_cmp_w2, ctx):
    dq = MLA_NOPE + MLA_ROPE
    w_uq = mla_w_uq[i].reshape(MLA_Q_RANK, MLA_HEADS, dq)
    w_uq = jnp.pad(w_uq, ((0, 0), (0, 0), (0, 2 * LANES - dq))).reshape(MLA_Q_RANK, -1).astype(BF16)
    mla = (mla_g_cq[i], mla_g_ckv[i], w_uq, mla_w_ukv[i].astype(BF16), ctx["freq"])
    nsa = (nsa_cmp_pe[i].astype(F32), nsa_cmp_w1[i].astype(BF16), nsa_cmp_w2[i].astype(BF16), ctx["tab_t"],
           ctx["ov"], ctx["n_slc"])
    return mla, nsa


def kernel(x, mem, positions, norm_g, mem_norm_g, final_norm_g, t5_table, w_out, mem_w_kv, even_w_in, s5_lam_re,
           s5_lam_im, s5_log_dt, s5_b_re, s5_b_im, s5_c_re, s5_c_im, s5_d, s5_w_glu, fox_b_f, odd_w_in, mla_g_cq,
           mla_g_ckv, mla_w_uq, mla_w_ukv, nsa_cmp_pe, nsa_cmp_w1, nsa_cmp_w2):
    batch, seq, _ = x.shape
    assert batch == 1 and seq % 1024 == 0 and seq // NSA_SLC_BLOCK <= LANES
    depth = norm_g.shape[0]
    tiles = _tiles(seq)
    ctx = _context(positions, t5_table, seq, tiles)
    h = x[0]
    w_out_b = w_out.astype(BF16)
    mem_kv_all = _mem_kv_all(mem[0], mem_norm_g, mem_w_kv.astype(BF16))
    for layer in range(depth):
        i = layer // 2
        mem_kv = (mem_kv_all, layer)
        if layer % 2 == 0:
            w_in = (_reorder_w_in(even_w_in[i], EVEN_SPLITS, EVEN_B_ORDER),
                    _reorder_w_in(even_w_in[i], EVEN_SPLITS, EVEN_F_ORDER))
            b_cat, c_cat, tab = _s5_prepare(s5_lam_re[i], s5_lam_im[i], s5_log_dt[i], s5_b_re[i], s5_b_im[i],
                                            s5_c_re[i], s5_c_im[i])
            ys = _even_layer(h, mem_kv, norm_g[layer], w_in, (b_cat, c_cat, tab, s5_d[i]),
                             s5_w_glu[i].astype(BF16), fox_b_f[i], tiles)
        else:
            w_in = (_reorder_w_in(odd_w_in[i], ODD_SPLITS, ODD_B_ORDER),
                    _reorder_w_in(odd_w_in[i], ODD_SPLITS, ODD_F_ORDER))
            mla, nsa = _odd_params(i, mla_g_cq, mla_g_ckv, mla_w_uq, mla_w_ukv, nsa_cmp_pe, nsa_cmp_w1, nsa_cmp_w2,
                                   ctx)
            ys = _odd_layer(h, mem_kv, norm_g[layer], w_in, mla, nsa, ctx["pos"], tiles)
        h = _out_proj(h, ys, w_out_b, layer, tm=tiles["tm"], tn=512)
    return _final_norm(h, final_norm_g, tm=tiles["norm_t"])[None]
```

```python
import functools
import math

import numpy as np
import jax
import jax.numpy as jnp
from jax import lax
from jax.experimental import pallas as pl
from jax.experimental.pallas import tpu as pltpu

F32 = jnp.float32
BF16 = jnp.bfloat16
I32 = jnp.int32

D_MODEL = 2048
DEPTH = 4
N_MEM = 256
RMS_EPS = 1e-6
NEG_INF = -1e30
LOG2E = math.log2(math.e)

S5_WIDTH = 1024
S5_GROUP = 16
S5_GROUPS = S5_WIDTH // S5_GROUP
S5_STATE = 64
FOX_HEADS = 8
FOX_HEAD_DIM = 128
FOX_WIDTH = FOX_HEADS * FOX_HEAD_DIM
MEM_HEADS = 4
MEM_HEAD_DIM = 128
MEM_WIDTH = MEM_HEADS * MEM_HEAD_DIM
MLA_HEADS = 8
MLA_Q_RANK = 512
MLA_KV_RANK = 512
MLA_NOPE = 128
MLA_ROPE = 64
MLA_V = 128
MLA_WIDTH = MLA_HEADS * MLA_V
ROPE_THETA = 10000.0
NSA_HEADS = 8
NSA_KV_GROUPS = 2
NSA_REP = NSA_HEADS // NSA_KV_GROUPS
NSA_HEAD_DIM = 128
NSA_WIDTH = NSA_HEADS * NSA_HEAD_DIM
NSA_KV = NSA_KV_GROUPS * NSA_HEAD_DIM
NSA_CMP_LEN = 32
NSA_CMP_STRIDE = 16
NSA_CMP_HIDDEN = 256
NSA_SLC_BLOCK = 64
NSA_SLC_TOPK = 16
NSA_WINDOW = 512
FORCE_SCORE = 1e6
T5_BUCKETS = 32
T5_MAX_DIST = 1024

EVEN_SPLITS = (S5_WIDTH, S5_WIDTH, FOX_WIDTH, FOX_WIDTH, FOX_WIDTH, FOX_HEADS, FOX_WIDTH, MEM_WIDTH, MEM_WIDTH)
ODD_SPLITS = (MLA_Q_RANK, MLA_KV_RANK, MLA_ROPE, MLA_WIDTH, NSA_WIDTH, NSA_KV, NSA_KV, NSA_KV, NSA_KV, NSA_KV,
              NSA_KV, 3 * NSA_HEADS, NSA_WIDTH, MEM_WIDTH, MEM_WIDTH)

LANES = 128
SUBLANES = 8
VMEM_LIMIT_BYTES = 56 * 1024 * 1024

EVEN_B_ORDER = (("q", 2, 1024), ("k", 3, 1024), ("v", 4, 1024), ("q_mem", 7, 512))
EVEN_F_ORDER = (("u", 0, 1024), ("g_s5", 1, 1024), ("g_fox", 6, 1024), ("g_mem", 8, 512), ("f", 5, 128),
                ("pad", None, 128))
ODD_B_ORDER = (("q_nsa", 4, 1024), ("q_mem", 13, 512), ("k_cmp", 5, 256), ("v_cmp", 6, 256), ("k_slc", 7, 256),
               ("v_slc", 8, 256), ("k_win", 9, 256), ("v_win", 10, 256))
ODD_F_ORDER = (("g_nsa", 12, 1024), ("g_mla", 3, 1024), ("c_q", 0, 512), ("c_kv", 1, 512), ("g_mem", 14, 512),
               ("k_rope", 2, 128), ("gates", 11, 128))


def _layout(order):
    off, out = 0, {}
    for name, _, width in order:
        assert off % width == 0
        out[name] = off
        off += width
    return out, off


EVEN_B_OFF, EVEN_B_N = _layout(EVEN_B_ORDER)
EVEN_F_OFF, EVEN_F_N = _layout(EVEN_F_ORDER)
ODD_B_OFF, ODD_B_N = _layout(ODD_B_ORDER)
ODD_F_OFF, ODD_F_N = _layout(ODD_F_ORDER)


def _reorder_w_in(w, splits, order):
    starts = np.concatenate([[0], np.cumsum(splits)])
    cols = []
    for _, idx, width in order:
        if idx is None:
            cols.append(jnp.zeros((w.shape[0], width), w.dtype))
            continue
        seg = w[:, int(starts[idx]):int(starts[idx + 1])]
        pad = width - seg.shape[1]
        if pad:
            seg = jnp.pad(seg, ((0, 0), (0, pad)))
        cols.append(seg)
    return jnp.concatenate(cols, axis=1).astype(BF16)


def _cparams(*sem):
    return pltpu.CompilerParams(dimension_semantics=sem, vmem_limit_bytes=VMEM_LIMIT_BYTES)


def _silu(g):
    return g * jax.nn.sigmoid(g)


def _pick(n, cands):
    for c in cands:
        if n % c == 0:
            return c
    raise ValueError(f"no tile for {n} in {cands}")


def _norm_matmul_kernel(x_ref, g_ref, w_ref, o_ref, xn_ref):
    @pl.when(pl.program_id(1) == 0)
    def _():
        x = x_ref[...]
        ms = jnp.mean(x * x, axis=-1, keepdims=True)
        xn_ref[...] = (x * lax.rsqrt(ms + RMS_EPS) * g_ref[...]).astype(BF16)

    o_ref[...] = jnp.dot(xn_ref[...], w_ref[...], preferred_element_type=F32).astype(o_ref.dtype)


def _norm_matmul(x, g, w, *, x_cb=0, tm, tn, name, out_dtype=F32):
    m = x.shape[0]
    k, n = w.shape
    return pl.pallas_call(
        _norm_matmul_kernel,
        grid=(m // tm, n // tn),
        in_specs=[pl.BlockSpec((tm, k), lambda i, j: (i, x_cb)),
                  pl.BlockSpec((1, k), lambda i, j: (0, 0)),
                  pl.BlockSpec((k, tn), lambda i, j: (0, j))],
        out_specs=pl.BlockSpec((tm, tn), lambda i, j: (i, j)),
        out_shape=jax.ShapeDtypeStruct((m, n), out_dtype),
        scratch_shapes=[pltpu.VMEM((tm, k), BF16)],
        compiler_params=_cparams("parallel", "arbitrary"),
        name=name,
    )(x, g.reshape(1, k), w)


def _out_proj_kernel(h_ref, *refs):
    o_ref = refs[-1]
    n = (len(refs) - 1) // 2
    acc = h_ref[...]
    for y_ref, w_ref in zip(refs[:n], refs[n:2 * n]):
        acc = acc + jnp.dot(y_ref[...], w_ref[...], preferred_element_type=F32)
    o_ref[...] = acc


def _out_proj(h, ys, w_all, layer, *, tm, tn):
    m, n = h.shape
    in_specs = [pl.BlockSpec((tm, tn), lambda i, j: (i, j))]
    in_specs += [pl.BlockSpec((tm, y.shape[1]), lambda i, j: (i, 0)) for y in ys]
    row = 0
    for y in ys:
        width = y.shape[1]
        assert row % width == 0
        in_specs.append(pl.BlockSpec((None, width, tn), lambda i, j, rb=row // width: (layer, rb, j)))
        row += width
    return pl.pallas_call(
        _out_proj_kernel,
        grid=(m // tm, n // tn),
        in_specs=in_specs,
        out_specs=pl.BlockSpec((tm, tn), lambda i, j: (i, j)),
        out_shape=jax.ShapeDtypeStruct((m, n), F32),
        compiler_params=_cparams("parallel", "arbitrary"),
        name="out_proj",
    )(h, *ys, *([w_all] * len(ys)))


def _final_norm_kernel(x_ref, g_ref, o_ref):
    x = x_ref[...]
    ms = jnp.mean(x * x, axis=-1, keepdims=True)
    o_ref[...] = x * lax.rsqrt(ms + RMS_EPS) * g_ref[...]


def _final_norm(h, g, *, tm):
    m, n = h.shape
    return pl.pallas_call(
        _final_norm_kernel,
        grid=(m // tm,),
        in_specs=[pl.BlockSpec((tm, n), lambda i: (i, 0)), pl.BlockSpec((1, n), lambda i: (0, 0))],
        out_specs=pl.BlockSpec((tm, n), lambda i: (i, 0)),
        out_shape=jax.ShapeDtypeStruct((m, n), F32),
        compiler_params=_cparams("parallel"),
        name="final_norm",
    )(h, g.reshape(1, n))


ONES_ROWS = 16


def _build_vt(v_ref, vt_ref, hh, dv, seq, chunk):
    def body(c, carry):
        st = pl.multiple_of(c * chunk, chunk)
        vt_ref[hh, 0:dv, pl.ds(st, chunk)] = v_ref[pl.ds(st, chunk), hh * dv:(hh + 1) * dv].astype(F32).T.astype(BF16)
        return carry

    lax.fori_loop(0, seq // chunk, body, 0)
    extra = vt_ref.shape[1] - dv
    if extra:
        vt_ref[hh, dv:dv + extra, :] = jnp.ones((extra, seq), BF16)


def _flash_kernel(*refs, scale, tq, tk, hp, dk, dv, has_decay):
    if has_decay:
        q_ref, k_ref, v_ref, g_ref, cq_ref, ck_ref, o_ref, vt_ref, qt_ref, s_ref, m_ref, acc_ref = refs
    else:
        q_ref, k_ref, v_ref, g_ref, o_ref, vt_ref, qt_ref, s_ref, m_ref, acc_ref = refs
    hb = pl.program_id(0)
    qi = pl.program_id(1)
    seq = k_ref.shape[0]

    @pl.when(qi == 0)
    def _():
        for hh in range(hp):
            _build_vt(v_ref, vt_ref, hh, dv, seq, tk)

    for hh in range(hp):
        qt_ref[hh] = (q_ref[:, hh * dk:(hh + 1) * dk].astype(F32) * (scale * LOG2E)).T.astype(BF16)
    m_ref[...] = jnp.full(m_ref.shape, NEG_INF, F32)
    acc_ref[...] = jnp.zeros(acc_ref.shape, F32)
    n_full = (qi * tq) // tk
    if has_decay:
        cq2 = [cq_ref[pl.ds(hb * hp + hh, 1), :] for hh in range(hp)]

    def scores(j, slot):
        start = pl.multiple_of(j * tk, tk)
        for hh in range(hp):
            s_ref[slot, hh] = jnp.dot(k_ref[pl.ds(start, tk), hh * dk:(hh + 1) * dk], qt_ref[hh],
                                      preferred_element_type=F32)

    def softmax_pv(j, slot, masked):
        start = pl.multiple_of(j * tk, tk)
        for hh in range(hp):
            t = s_ref[slot, hh]
            if has_decay:
                t = t - jnp.concatenate([ck_ref[hh, pl.ds(start, tk), :]] * (tq // LANES), axis=1)
            if masked:
                key = start + lax.broadcasted_iota(I32, (tk, tq), 0)
                qry = qi * tq + lax.broadcasted_iota(I32, (tk, tq), 1)
                t = jnp.where(key <= qry, t, NEG_INF)
            m_prev = m_ref[hh]
            mx = jnp.max(t, axis=0, keepdims=True)
            if has_decay:
                m_new = jnp.maximum(m_prev, mx + cq2[hh])
                shift = m_new - cq2[hh]
            else:
                m_new = jnp.maximum(m_prev, mx)
                shift = m_new
            alpha = jnp.exp2(m_prev - m_new)
            p = jnp.exp2(t - shift).astype(BF16)
            acc_ref[hh] = alpha * acc_ref[hh] + jnp.dot(vt_ref[hh, :, pl.ds(start, tk)], p,
                                                         preferred_element_type=F32)
            m_ref[hh] = m_new

    scores(0, 0)

    def pair(j):
        scores(j + 1, 1)
        softmax_pv(j, 0, False)
        scores(j + 2, 0)
        softmax_pv(j + 1, 1, False)

    def quad(qq, carry):
        pair(4 * qq)
        pair(4 * qq + 2)
        return carry

    quads = n_full // 4
    lax.fori_loop(0, quads, quad, 0)
    rem = n_full - 4 * quads

    @pl.when(rem >= 2)
    def _():
        pair(4 * quads)

    last = 4 * quads + 2 * (rem // 2)

    @pl.when(n_full % 2 == 1)
    def _():
        scores(last + 1, 1)
        softmax_pv(last, 0, False)
        softmax_pv(last + 1, 1, True)

    @pl.when(n_full % 2 == 0)
    def _():
        softmax_pv(last, 0, True)

    for hh in range(hp):
        a = acc_ref[hh]
        o = (a[0:dv, :] / a[dv:dv + 1, :]).T
        o_ref[:, hh * dv:(hh + 1) * dv] = (o * _silu(g_ref[:, hh * dv:(hh + 1) * dv])).astype(o_ref.dtype)


def _flash(q_arr, q_off, k_arr, k_off, v_arr, v_off, g_arr, g_off, *, heads, dk, dv, scale, tq, tk, hp,
           decay=None, name):
    seq = q_arr.shape[0]
    once = pl.Buffered(1)
    in_specs = [pl.BlockSpec((tq, hp * dk), lambda h, i: (i, q_off // (hp * dk) + h)),
                pl.BlockSpec((seq, hp * dk), lambda h, i: (0, k_off // (hp * dk) + h), pipeline_mode=once),
                pl.BlockSpec((seq, hp * dv), lambda h, i: (0, v_off // (hp * dv) + h), pipeline_mode=once),
                pl.BlockSpec((tq, hp * dv), lambda h, i: (i, g_off // (hp * dv) + h))]
    args = [q_arr, k_arr, v_arr, g_arr]
    if decay is not None:
        cum_t, cum_rep = decay
        in_specs += [pl.BlockSpec((SUBLANES, tq), lambda h, i: (0, i)),
                     pl.BlockSpec((hp, seq, LANES), lambda h, i: (h, 0, 0), pipeline_mode=once)]
        args += [cum_t, cum_rep]
    return pl.pallas_call(
        functools.partial(_flash_kernel, scale=scale, tq=tq, tk=tk, hp=hp, dk=dk, dv=dv, has_decay=decay is not None),
        grid=(heads // hp, seq // tq),
        in_specs=in_specs,
        out_specs=pl.BlockSpec((tq, hp * dv), lambda h, i: (i, h)),
        out_shape=jax.ShapeDtypeStruct((seq, heads * dv), BF16),
        scratch_shapes=[pltpu.VMEM((hp, dv + ONES_ROWS, seq), BF16), pltpu.VMEM((hp, dk, tq), BF16),
                        pltpu.VMEM((2, hp, tk, tq), F32), pltpu.VMEM((hp, 1, tq), F32),
                        pltpu.VMEM((hp, dv + ONES_ROWS, tq), F32)],
        compiler_params=_cparams("parallel", "arbitrary"),
        name=name,
    )(*args)


def _decay_kernel(f_ref, b_ref, ct_ref, cr_ref, carry_ref, *, t):
    i = pl.program_id(0)

    @pl.when(i == 0)
    def _():
        carry_ref[...] = jnp.zeros(carry_ref.shape, F32)

    x = f_ref[...] + b_ref[...]
    lf = jnp.minimum(x, 0.0) - jnp.log1p(jnp.exp(-jnp.abs(x)))
    row = lax.broadcasted_iota(I32, lf.shape, 0)
    s = 1
    while s < t:
        lf = lf + jnp.where(row >= s, pltpu.roll(lf, s, 0), 0.0)
        s *= 2
    lf = lf + carry_ref[...]
    carry_ref[...] = lf[t - 1:t, :]
    lf2 = lf * LOG2E
    ct_ref[...] = lf2.T[:FOX_HEADS, :]
    for h in range(FOX_HEADS):
        cr_ref[h] = jnp.broadcast_to(lf2[:, h:h + 1], (t, LANES))


def _decay(proj, f_cb, b_f, *, t):
    seq = proj.shape[0]
    b = jnp.pad(b_f.reshape(1, FOX_HEADS), ((0, 0), (0, LANES - FOX_HEADS)))
    return pl.pallas_call(
        functools.partial(_decay_kernel, t=t),
        grid=(seq // t,),
        in_specs=[pl.BlockSpec((t, LANES), lambda i: (i, f_cb)), pl.BlockSpec((1, LANES), lambda i: (0, 0))],
        out_specs=[pl.BlockSpec((FOX_HEADS, t), lambda i: (0, i)),
                   pl.BlockSpec((FOX_HEADS, t, LANES), lambda i: (0, i, 0))],
        out_shape=[jax.ShapeDtypeStruct((FOX_HEADS, seq), F32), jax.ShapeDtypeStruct((FOX_HEADS, seq, LANES), F32)],
        scratch_shapes=[pltpu.VMEM((1, LANES), F32)],
        compiler_params=_cparams("arbitrary"),
        name="fox_decay",
    )(proj, b)


def _mem_attn_kernel(q_ref, kv_ref, g_ref, o_ref):
    d = MEM_HEAD_DIM
    for h in range(MEM_HEADS):
        hs = slice(h * d, (h + 1) * d)
        k = kv_ref[:, hs].astype(BF16)
        v = kv_ref[:, MEM_WIDTH + h * d:MEM_WIDTH + (h + 1) * d].astype(BF16)
        s = lax.dot_general(q_ref[:, hs], k, (((1,), (1,)), ((), ())), preferred_element_type=F32) * (d ** -0.5)
        m = jnp.max(s, axis=1, keepdims=True)
        p = jnp.exp(s - m)
        l = jnp.sum(p, axis=1, keepdims=True)
        o = jnp.dot(p.astype(BF16), v, preferred_element_type=F32) / l
        o_ref[:, hs] = (o * _silu(g_ref[:, hs])).astype(o_ref.dtype)


def _mem_kv_all(mem2d, g, w_all):
    depth, k, n = w_all.shape
    m = mem2d.shape[0]
    tn = 512
    return pl.pallas_call(
        _norm_matmul_kernel,
        grid=(depth, n // tn),
        in_specs=[pl.BlockSpec((m, k), lambda l, j: (0, 0)),
                  pl.BlockSpec((1, k), lambda l, j: (0, 0)),
                  pl.BlockSpec((None, k, tn), lambda l, j: (l, 0, j))],
        out_specs=pl.BlockSpec((None, m, tn), lambda l, j: (l, 0, j)),
        out_shape=jax.ShapeDtypeStruct((depth, m, n), F32),
        scratch_shapes=[pltpu.VMEM((m, k), BF16)],
        compiler_params=_cparams("arbitrary", "arbitrary"),
        name="mem_kv",
    )(mem2d, g.reshape(1, k), w_all)


def _mem_attn(proj_b, q_cb, proj_f, g_cb, mem_kv, *, t):
    seq = proj_b.shape[0]
    mem_kv, layer = mem_kv
    nm = mem_kv.shape[1]
    w = MEM_WIDTH
    d = MEM_HEAD_DIM
    return pl.pallas_call(
        _mem_attn_kernel,
        grid=(seq // t,),
        in_specs=[pl.BlockSpec((t, w), lambda i: (i, q_cb * d // w)),
                  pl.BlockSpec((None, nm, 2 * w), lambda i: (layer, 0, 0)),
                  pl.BlockSpec((t, w), lambda i: (i, g_cb * d // w))],
        out_specs=pl.BlockSpec((t, w), lambda i: (i, 0)),
        out_shape=jax.ShapeDtypeStruct((seq, w), BF16),
        compiler_params=_cparams("parallel"),
        name="mem_attn",
    )(proj_b, mem_kv, proj_f)


S5_TILE_GROUPS = LANES // S5_GROUP
S5_TILE_STATES = S5_TILE_GROUPS * S5_STATE
S5_TILES = S5_GROUPS // S5_TILE_GROUPS


def _s5_scan_kernel(u_ref, b_ref, c_ref, d_ref, tab_ref, z_ref, bu_ref, carry_ref, *, tc):
    ns = S5_TILE_STATES

    @pl.when(pl.program_id(1) == 0)
    def _():
        carry_ref[...] = jnp.zeros(carry_ref.shape, F32)

    u = u_ref[...]
    bu_ref[...] = jnp.dot(u.astype(BF16), b_ref[...], preferred_element_type=F32)
    steps = [(1, tab_ref[0], tab_ref[1]), (2, tab_ref[2], tab_ref[3]), (4, tab_ref[4], tab_ref[5])]
    pr = tab_ref[6]
    pi = tab_ref[7]

    def body(i, carry):
        cr, ci = carry
        r0 = pl.multiple_of(i * SUBLANES, SUBLANES)
        xr = bu_ref[pl.ds(r0, SUBLANES), 0:ns]
        xi = bu_ref[pl.ds(r0, SUBLANES), ns:2 * ns]
        for s, ar, ai in steps:
            sr = pltpu.roll(xr, s, 0)
            si = pltpu.roll(xi, s, 0)
            xr, xi = xr + ar * sr - ai * si, xi + ar * si + ai * sr
        xr, xi = xr + pr * cr - pi * ci, xi + pr * ci + pi * cr
        bu_ref[pl.ds(r0, SUBLANES), 0:ns] = xr
        bu_ref[pl.ds(r0, SUBLANES), ns:2 * ns] = xi
        return xr[SUBLANES - 1:SUBLANES, :], xi[SUBLANES - 1:SUBLANES, :]

    cr, ci = lax.fori_loop(0, tc // SUBLANES, body, (carry_ref[0:1, 0:ns], carry_ref[0:1, ns:2 * ns]))
    carry_ref[0:1, 0:ns] = cr
    carry_ref[0:1, ns:2 * ns] = ci
    y = jnp.dot(bu_ref[...].astype(BF16), c_ref[...], preferred_element_type=F32) + d_ref[...] * u
    z_ref[...] = jax.nn.gelu(y)


def _s5_prepare(lam_re, lam_im, log_dt, b_re, b_im, c_re, c_im):
    dt = jnp.exp(log_dt.astype(F32))[:, None]
    lr = lam_re.astype(F32)
    li = lam_im.astype(F32)
    mag = jnp.exp(lr * dt)
    ab_re = mag * jnp.cos(li * dt)
    ab_im = mag * jnp.sin(li * dt)
    den = lr * lr + li * li
    nr = ab_re - 1.0
    f_re = (nr * lr + ab_im * li) / den
    f_im = (ab_im * lr - nr * li) / den
    br = b_re.astype(F32)
    bim = b_im.astype(F32)
    bb_re = f_re[..., None] * br - f_im[..., None] * bim
    bb_im = f_re[..., None] * bim + f_im[..., None] * br
    eye = jnp.eye(S5_TILE_GROUPS, dtype=F32)

    def blockdiag_in(bb):
        t = bb.reshape(S5_TILES, S5_TILE_GROUPS, S5_STATE, S5_GROUP)
        m = jnp.einsum("jgpc,gh->jgchp", t, eye)
        return m.reshape(S5_TILES, LANES, S5_TILE_STATES)

    def blockdiag_out(cc):
        t = cc.reshape(S5_TILES, S5_TILE_GROUPS, S5_GROUP, S5_STATE)
        m = jnp.einsum("jgcp,gh->jgphc", t, eye)
        return m.reshape(S5_TILES, S5_TILE_STATES, LANES)

    b_cat = jnp.concatenate([blockdiag_in(bb_re), blockdiag_in(bb_im)], axis=2).astype(BF16)
    c_cat = jnp.concatenate([blockdiag_out(c_re.astype(F32)), -blockdiag_out(c_im.astype(F32))], axis=1).astype(BF16)

    a_r = ab_re.reshape(S5_TILES, 1, S5_TILE_STATES)
    a_i = ab_im.reshape(S5_TILES, 1, S5_TILE_STATES)

    def cmul(xr, xi, yr, yi):
        return xr * yr - xi * yi, xr * yi + xi * yr

    a2 = cmul(a_r, a_i, a_r, a_i)
    a4 = cmul(*a2, *a2)
    row = jnp.arange(SUBLANES)[None, :, None]
    tabs = []
    for s, (pr_, pi_) in ((1, (a_r, a_i)), (2, a2), (4, a4)):
        tabs.append(jnp.where(row >= s, pr_, 0.0))
        tabs.append(jnp.where(row >= s, pi_, 0.0))
    pw = [(a_r, a_i)]
    for _ in range(SUBLANES - 1):
        pw.append(cmul(*pw[-1], a_r, a_i))
    tabs.append(jnp.concatenate([p[0] for p in pw], axis=1))
    tabs.append(jnp.concatenate([p[1] for p in pw], axis=1))
    tab = jnp.stack([jnp.broadcast_to(t, (S5_TILES, SUBLANES, S5_TILE_STATES)) for t in tabs], axis=1)
    return b_cat, c_cat, tab.astype(F32)


def _s5_scan(proj, u_cb, b_cat, c_cat, d_skip, tab, *, tc):
    seq = proj.shape[0]
    ns = S5_TILE_STATES
    d = d_skip.astype(F32).reshape(S5_TILES, 1, LANES)
    return pl.pallas_call(
        functools.partial(_s5_scan_kernel, tc=tc),
        grid=(S5_TILES, seq // tc),
        in_specs=[pl.BlockSpec((tc, LANES), lambda j, c: (c, u_cb + j)),
                  pl.BlockSpec((None, LANES, 2 * ns), lambda j, c: (j, 0, 0)),
                  pl.BlockSpec((None, 2 * ns, LANES), lambda j, c: (j, 0, 0)),
                  pl.BlockSpec((None, 1, LANES), lambda j, c: (j, 0, 0)),
                  pl.BlockSpec((None, 8, SUBLANES, ns), lambda j, c: (j, 0, 0, 0))],
        out_specs=pl.BlockSpec((tc, LANES), lambda j, c: (c, j)),
        out_shape=jax.ShapeDtypeStruct((seq, S5_WIDTH), F32),
        scratch_shapes=[pltpu.VMEM((tc, 2 * ns), F32), pltpu.VMEM((SUBLANES, 2 * ns), F32)],
        compiler_params=_cparams("parallel", "arbitrary"),
        name="s5_scan",
    )(proj, b_cat, c_cat, d, tab)


def _s5_glu_kernel(z_ref, w_ref, g_ref, o_ref, *, tn):
    j = pl.program_id(1)
    z = z_ref[...]
    a = jnp.dot(z.astype(BF16), w_ref[...], preferred_element_type=F32)
    zc = z_ref[:, pl.ds(pl.multiple_of(j * tn, tn), tn)]
    o_ref[...] = (zc * jax.nn.sigmoid(a) * _silu(g_ref[...])).astype(o_ref.dtype)


def _s5_glu(z, w_glu, proj, g_cb, *, tm, tn):
    seq, n = z.shape
    return pl.pallas_call(
        functools.partial(_s5_glu_kernel, tn=tn),
        grid=(seq // tm, n // tn),
        in_specs=[pl.BlockSpec((tm, n), lambda i, j: (i, 0)),
                  pl.BlockSpec((n, tn), lambda i, j: (0, j)),
                  pl.BlockSpec((tm, tn), lambda i, j: (i, g_cb * (n // tn) + j))],
        out_specs=pl.BlockSpec((tm, tn), lambda i, j: (i, j)),
        out_shape=jax.ShapeDtypeStruct((seq, n), BF16),
        compiler_params=_cparams("parallel", "arbitrary"),
        name="s5_glu",
    )(z, w_glu, proj)


def _rope_tables(pos, freq):
    ang = pos * freq
    lane = lax.broadcasted_iota(I32, ang.shape, 1)
    half = MLA_ROPE // 2
    cos = jnp.cos(ang)
    sin = jnp.sin(ang)
    c = jnp.where(lane < MLA_ROPE, cos, 0.0)
    s1 = jnp.where(lane < half, -sin, 0.0)
    s2 = jnp.where((lane >= half) & (lane < MLA_ROPE), sin, 0.0)
    return c, s1, s2


def _rope_apply(x, c, s1, s2):
    half = MLA_ROPE // 2
    return x * c + pltpu.roll(x, LANES - half, 1) * s1 + pltpu.roll(x, half, 1) * s2


def _mla_up_kernel(cq_ref, ckv_ref, kr_ref, pos_ref, freq_ref, gq_ref, gkv_ref, wq_ref, wkv_ref, q_ref, k_ref, v_ref):
    def normed(x_ref, g_ref):
        x = x_ref[...]
        ms = jnp.mean(x * x, axis=-1, keepdims=True)
        return (x * lax.rsqrt(ms + RMS_EPS) * g_ref[...]).astype(BF16)

    qf = jnp.dot(normed(cq_ref, gq_ref), wq_ref[...], preferred_element_type=F32)
    kvf = jnp.dot(normed(ckv_ref, gkv_ref), wkv_ref[...], preferred_element_type=F32)
    c, s1, s2 = _rope_tables(pos_ref[...], freq_ref[...])
    kr = _rope_apply(kr_ref[...], c, s1, s2).astype(BF16)
    for h in range(MLA_HEADS):
        b = 2 * LANES * h
        q_ref[:, b:b + LANES] = qf[:, b:b + LANES].astype(BF16)
        q_ref[:, b + LANES:b + 2 * LANES] = _rope_apply(qf[:, b + LANES:b + 2 * LANES], c, s1, s2).astype(BF16)
        k_ref[:, b:b + LANES] = kvf[:, b:b + LANES].astype(BF16)
        k_ref[:, b + LANES:b + 2 * LANES] = kr
        v_ref[:, LANES * h:LANES * (h + 1)] = kvf[:, b + LANES:b + 2 * LANES].astype(BF16)


def _mla_up(proj, cq_cb, ckv_cb, kr_cb, g_cq, g_ckv, w_uq, w_ukv, pos_col, freq, *, t):
    seq = proj.shape[0]
    w = 2 * LANES * MLA_HEADS
    rq, rkv = w_uq.shape[0], w_ukv.shape[0]
    return pl.pallas_call(
        _mla_up_kernel,
        grid=(seq // t,),
        in_specs=[pl.BlockSpec((t, rq), lambda i: (i, cq_cb)),
                  pl.BlockSpec((t, rkv), lambda i: (i, ckv_cb)),
                  pl.BlockSpec((t, LANES), lambda i: (i, kr_cb)),
                  pl.BlockSpec((t, 1), lambda i: (i, 0)),
                  pl.BlockSpec((1, LANES), lambda i: (0, 0)),
                  pl.BlockSpec((1, rq), lambda i: (0, 0)),
                  pl.BlockSpec((1, rkv), lambda i: (0, 0)),
                  pl.BlockSpec((rq, w), lambda i: (0, 0)),
                  pl.BlockSpec((rkv, w), lambda i: (0, 0))],
        out_specs=[pl.BlockSpec((t, w), lambda i: (i, 0)),
                   pl.BlockSpec((t, w), lambda i: (i, 0)),
                   pl.BlockSpec((t, MLA_WIDTH), lambda i: (i, 0))],
        out_shape=[jax.ShapeDtypeStruct((seq, w), BF16), jax.ShapeDtypeStruct((seq, w), BF16),
                   jax.ShapeDtypeStruct((seq, MLA_WIDTH), BF16)],
        compiler_params=_cparams("parallel"),
        name="mla_up",
    )(proj, proj, proj, pos_col, freq, g_cq.reshape(1, rq), g_ckv.reshape(1, rkv), w_uq, w_ukv)


def _t5_bucket(dist):
    n = jnp.maximum(dist, 0)
    max_exact = T5_BUCKETS // 2
    log_ratio = jnp.log(jnp.maximum(n, 1).astype(F32) / max_exact) / math.log(T5_MAX_DIST / max_exact)
    large = jnp.minimum(max_exact + (log_ratio * (T5_BUCKETS - max_exact)).astype(I32), T5_BUCKETS - 1)
    return jnp.where(n < max_exact, n, large)


T5_MASK_BUCKET = T5_BUCKETS


def _t5_lookup(table_row, bucket):
    rows, width = bucket.shape
    tab = jnp.broadcast_to(table_row, (rows, LANES))
    parts = [jnp.take_along_axis(tab, bucket[:, c:c + LANES], axis=1, mode="promise_in_bounds")
             for c in range(0, width, LANES)]
    return parts[0] if len(parts) == 1 else jnp.concatenate(parts, axis=1)


def _nsa_cmp_kernel(x_ref, pe_ref, w1_ref, w2_ref, o_ref, xf_ref, *, nc):
    half = NSA_CMP_LEN // 2
    d = NSA_HEAD_DIM
    xf_ref[...] = x_ref[...].astype(F32)
    u = jnp.zeros((nc, NSA_CMP_HIDDEN), F32)
    v = jnp.zeros((nc, NSA_CMP_HIDDEN), F32)
    for r in range(half):
        a = xf_ref[pl.ds(r, nc, stride=NSA_CMP_STRIDE), :]
        u = u + jnp.dot((a + pe_ref[r:r + 1, :]).astype(BF16), w1_ref[r * d:(r + 1) * d, :],
                        preferred_element_type=F32)
        v = v + jnp.dot((a + pe_ref[half + r:half + r + 1, :]).astype(BF16),
                        w1_ref[(half + r) * d:(half + r + 1) * d, :], preferred_element_type=F32)
    hid = u + pltpu.roll(v, nc - 1, 0)
    o_ref[...] = jnp.dot(jax.nn.gelu(hid).astype(BF16), w2_ref[...], preferred_element_type=F32).astype(o_ref.dtype)


def _nsa_compress(proj, k_cb, pe, w1, w2):
    seq = proj.shape[0]
    nc = seq // NSA_CMP_STRIDE
    d = NSA_HEAD_DIM
    g = NSA_KV_GROUPS
    return pl.pallas_call(
        functools.partial(_nsa_cmp_kernel, nc=nc),
        grid=(2, g),
        in_specs=[pl.BlockSpec((seq, d), lambda a, b: (0, k_cb + a * g + b)),
                  pl.BlockSpec((None, NSA_CMP_LEN, d), lambda a, b: (a, 0, 0)),
                  pl.BlockSpec((None, NSA_CMP_LEN * d, NSA_CMP_HIDDEN), lambda a, b: (a, 0, 0)),
                  pl.BlockSpec((None, NSA_CMP_HIDDEN, d), lambda a, b: (a, 0, 0))],
        out_specs=pl.BlockSpec((None, None, nc, d), lambda a, b: (a, b, 0, 0)),
        out_shape=jax.ShapeDtypeStruct((2, g, nc, d), BF16),
        scratch_shapes=[pltpu.VMEM((seq, d), F32)],
        compiler_params=_cparams("parallel", "arbitrary"),
        name="nsa_compress",
    )(proj, pe, w1, w2)


def _nsa_select_kernel(consec_ref, q_ref, kc_ref, vc_ref, posq_ref, posc_ref, tab_ref, gate_ref, ov_ref,
                       oc_ref, sel_ref, strip_ref, *, nc, n_slc, tq):
    qi = pl.program_id(0)
    d = NSA_HEAD_DIM
    c = d ** -0.5 * LOG2E
    per_block = tq // NSA_CMP_STRIDE
    lane = lax.broadcasted_iota(I32, (1, tq), 1)
    tok = qi * tq + lane

    def select(bias):
        gates = jax.nn.sigmoid(gate_ref[...])
        ovt = ov_ref[...]
        js = lax.broadcasted_iota(I32, (LANES, tq), 0)
        jf = js.astype(F32)
        cur = tok // NSA_SLC_BLOCK
        forced = (js == 0) | (js == cur) | (js == cur - 1)
        for g in range(NSA_KV_GROUPS):
            kc = kc_ref[g]
            vct = vc_ref[g].astype(F32).T.astype(BF16)
            psum = jnp.zeros((nc, tq), F32)
            for r in range(NSA_REP):
                h = g * NSA_REP + r
                qt = (q_ref[:, h * d:(h + 1) * d].astype(F32) * c).T.astype(BF16)
                t = jnp.dot(kc, qt, preferred_element_type=F32) + bias(h)
                m = jnp.max(t, axis=0, keepdims=True)
                e = jnp.exp2(t - m)
                l = jnp.sum(e, axis=0, keepdims=True)
                p = e * jnp.where(m > 0.5 * NEG_INF, 1.0 / l, 0.0)
                o = jnp.dot(vct, p.astype(BF16), preferred_element_type=F32)
                oc_ref[:, h * d:(h + 1) * d] = gates[:, 3 * h:3 * h + 1] * o.T
                psum = psum + p
            p_hi = psum.astype(BF16)
            p_lo = (psum - p_hi.astype(F32)).astype(BF16)
            imp = jnp.dot(ovt, p_hi, preferred_element_type=F32) + jnp.dot(ovt, p_lo, preferred_element_type=F32)
            st = jnp.where(forced, FORCE_SCORE, jnp.where(js > cur, -1.0, imp))
            st = jnp.where(js < n_slc, st, -2.0)
            sel = jnp.zeros((LANES, tq), F32)
            for _ in range(NSA_SLC_TOPK):
                mx = jnp.max(st, axis=0, keepdims=True)
                first = jnp.min(jnp.where(st == mx, jf, float(LANES)), axis=0, keepdims=True)
                hit = jf == first
                sel = jnp.where(hit, 1.0, sel)
                st = jnp.where(hit, -3e38, st)
            sel_ref[g] = sel.astype(sel_ref.dtype)

    consecutive = consec_ref[0] == 1

    @pl.when(jnp.logical_and(consecutive, qi == 0))
    def _():
        for chunk in range(2 * nc // LANES):
            rel = chunk * LANES - nc + lax.broadcasted_iota(I32, (LANES, 1), 0)
            dist = lane - (rel * NSA_CMP_STRIDE + (NSA_CMP_LEN - 1))
            bucket = jnp.where(dist >= 0, _t5_bucket(dist), T5_MASK_BUCKET)
            for h in range(NSA_HEADS):
                strip_ref[h, chunk * LANES:(chunk + 1) * LANES, :] = _t5_lookup(tab_ref[h:h + 1, :], bucket)

    @pl.when(consecutive)
    def _():
        start = pl.multiple_of(nc - per_block * qi, per_block)
        select(lambda h: strip_ref[h, pl.ds(start, nc), :])

    @pl.when(jnp.logical_not(consecutive))
    def _():
        cmp_end = lax.broadcasted_iota(I32, (nc, 1), 0) * NSA_CMP_STRIDE + (NSA_CMP_LEN - 1)
        pos_c = jnp.concatenate([posc_ref[...]] * (tq // LANES), axis=1)
        bucket = jnp.where(cmp_end <= tok, _t5_bucket(posq_ref[...] - pos_c), T5_MASK_BUCKET)
        select(lambda h: _t5_lookup(tab_ref[h:h + 1, :], bucket))


def _nsa_select(proj_b, q_cb, proj_f, gate_cb, kc, vc, pos_row, pos_cmp_rep, consec, tab_t, ov_t, *, n_slc, tq):
    seq = proj_b.shape[0]
    nc = kc.shape[1]
    g = NSA_KV_GROUPS
    d = NSA_HEAD_DIM
    grid_spec = pltpu.PrefetchScalarGridSpec(
        num_scalar_prefetch=1,
        grid=(seq // tq,),
        in_specs=[pl.BlockSpec((tq, NSA_WIDTH), lambda i, *_: (i, q_cb)),
                  pl.BlockSpec((g, nc, d), lambda i, *_: (0, 0, 0)),
                  pl.BlockSpec((g, nc, d), lambda i, *_: (0, 0, 0)),
                  pl.BlockSpec((1, tq), lambda i, *_: (0, i)),
                  pl.BlockSpec((nc, LANES), lambda i, *_: (0, 0)),
                  pl.BlockSpec((SUBLANES, LANES), lambda i, *_: (0, 0)),
                  pl.BlockSpec((tq, LANES), lambda i, *_: (i, gate_cb)),
                  pl.BlockSpec((LANES, nc), lambda i, *_: (0, 0))],
        out_specs=[pl.BlockSpec((tq, NSA_WIDTH), lambda i, *_: (i, 0)),
                   pl.BlockSpec((g, LANES, tq), lambda i, *_: (0, 0, i))],
        scratch_shapes=[pltpu.VMEM((NSA_HEADS, 2 * nc, tq), F32)])
    return pl.pallas_call(
        functools.partial(_nsa_select_kernel, nc=nc, n_slc=n_slc, tq=tq),
        grid_spec=grid_spec,
        out_shape=[jax.ShapeDtypeStruct((seq, NSA_WIDTH), F32),
                   jax.ShapeDtypeStruct((g, LANES, seq), F32)],
        compiler_params=_cparams("arbitrary"),
        name="nsa_select",
    )(consec, proj_b, kc, vc, pos_row, pos_cmp_rep, tab_t * LOG2E, proj_f, ov_t)


def _nsa_slc_kernel(pqmin_ref, pkmax_ref, consec_ref, q_ref, k_ref, v_ref, sel_ref, posq_ref, posk_ref, tab_ref,
                    o_ref, vt_ref, qt_ref, s_ref, m_ref, l_ref, acc_ref, cache_ref, *, tq, tk):
    g = pl.program_id(0)
    qi = pl.program_id(1)
    d = NSA_HEAD_DIM
    scale = d ** -0.5
    seq = k_ref.shape[0]
    consecutive = consec_ref[0] == 1
    n_cached = cache_ref.shape[0]

    @pl.when(qi == 0)
    def _():
        _build_vt(v_ref, vt_ref, 0, d, seq, tk)

    @pl.when(jnp.logical_and(consecutive, qi == 0))
    def _():
        rel = lax.broadcasted_iota(I32, (1, tq), 1) - lax.broadcasted_iota(I32, (tk, 1), 0)
        for v in range(n_cached):
            dist = rel + v * tq
            bucket = jnp.where(dist >= 0, _t5_bucket(dist), T5_MASK_BUCKET)
            for r in range(NSA_REP):
                cache_ref[v, r] = _t5_lookup(tab_ref[pl.ds(g * NSA_REP + r, 1), :], bucket)

    for r in range(NSA_REP):
        qt_ref[r] = (q_ref[:, r * d:(r + 1) * d].astype(F32) * (scale * LOG2E)).T.astype(BF16)
    m_ref[...] = jnp.full(m_ref.shape, NEG_INF, F32)
    l_ref[...] = jnp.zeros(l_ref.shape, F32)
    acc_ref[...] = jnp.zeros(acc_ref.shape, F32)
    pos_q = posq_ref[...]
    per_tile = tk // NSA_SLC_BLOCK
    n_full = (qi * tq) // tk

    def scores(j, slot):
        start = pl.multiple_of(j * tk, tk)
        k = k_ref[pl.ds(start, tk), :]
        for r in range(NSA_REP):
            s_ref[slot, r] = jnp.dot(k, qt_ref[r], preferred_element_type=F32)

    def softmax_pv(j, slot, masked, mode):
        start = pl.multiple_of(j * tk, tk)
        picked = jnp.concatenate(
            [jnp.broadcast_to(sel_ref[pl.ds(j * per_tile + b, 1), :], (NSA_SLC_BLOCK, tq)) for b in range(per_tile)],
            axis=0)
        if masked and mode == "gather":
            key = start + lax.broadcasted_iota(I32, (tk, tq), 0)
            qry = qi * tq + lax.broadcasted_iota(I32, (tk, tq), 1)
            picked = jnp.where(key <= qry, picked, 0.0)
        mask = picked > 0.5
        if mode == "gather":
            pos_k = jnp.concatenate([posk_ref[pl.ds(start, tk), :]] * (tq // LANES), axis=1)
            bucket = jnp.where(mask, _t5_bucket(pos_q - pos_k), T5_MASK_BUCKET)
        for r in range(NSA_REP):
            h = g * NSA_REP + r
            t = s_ref[slot, r]
            if mode == "gather":
                t = t + _t5_lookup(tab_ref[pl.ds(h, 1), :], bucket)
            elif mode == "cached":
                t = jnp.where(mask, t + cache_ref[(qi * tq - start) // tq, r], NEG_INF)
            else:
                t = jnp.where(mask, t, NEG_INF)
            m_prev = m_ref[r]
            mx = jnp.max(t, axis=0, keepdims=True)
            if mode == "far":
                b = tab_ref[pl.ds(h, 1), T5_BUCKETS - 1:T5_BUCKETS]
                m_new = jnp.maximum(m_prev, mx + b)
                shift = m_new - b
            else:
                m_new = jnp.maximum(m_prev, mx)
                shift = m_new
            alpha = jnp.exp2(m_prev - m_new)
            p = jnp.exp2(t - shift)
            l_ref[r] = alpha * l_ref[r] + jnp.sum(p, axis=0, keepdims=True)
            acc_ref[r] = alpha * acc_ref[r] + jnp.dot(vt_ref[0, :, pl.ds(start, tk)], p.astype(BF16),
                                                       preferred_element_type=F32)
            m_ref[r] = m_new

    def is_far(j):
        return pqmin_ref[qi] - pkmax_ref[j] >= T5_MAX_DIST

    def stages(j, count, mode):
        for k in range(count):
            scores(j + k + 1, (k + 1) % 2)
            softmax_pv(j + k, k % 2, False, mode)

    def near(fn):
        @pl.when(consecutive)
        def _():
            fn("cached")

        @pl.when(jnp.logical_not(consecutive))
        def _():
            fn("gather")

    def far_or_near(far, fn):
        @pl.when(far)
        def _():
            fn("far")

        @pl.when(jnp.logical_not(far))
        def _():
            near(fn)

    scores(0, 0)

    def pair(jj, carry):
        j = 2 * jj
        far_or_near(jnp.logical_and(is_far(j), is_far(j + 1)), lambda mode: stages(j, 2, mode))
        return carry

    pairs = n_full // 2
    lax.fori_loop(0, pairs, pair, 0)
    last = 2 * pairs

    @pl.when(n_full % 2 == 1)
    def _():
        far_or_near(is_far(last), lambda mode: stages(last, 1, mode))
        near(lambda mode: softmax_pv(last + 1, 1, True, mode))

    @pl.when(n_full % 2 == 0)
    def _():
        near(lambda mode: softmax_pv(last, 0, True, mode))

    for r in range(NSA_REP):
        o_ref[:, r * d:(r + 1) * d] = (acc_ref[r] / l_ref[r]).T


def _nsa_slc(proj_b, q_off, k_off, v_off, sel, pos_row, pos_rep, tab_t, pq_min, pk_max, consec, *, tq, tk):
    seq = proj_b.shape[0]
    tab_t = tab_t * LOG2E
    d = NSA_HEAD_DIM
    gw = NSA_REP * d
    once = pl.Buffered(1)
    n_cached = (T5_MAX_DIST + tk - 2) // tq + 1 + tk // tq
    grid_spec = pltpu.PrefetchScalarGridSpec(
        num_scalar_prefetch=3,
        grid=(NSA_KV_GROUPS, seq // tq),
        in_specs=[pl.BlockSpec((tq, gw), lambda g, i, *_: (i, q_off // gw + g)),
                  pl.BlockSpec((seq, d), lambda g, i, *_: (0, k_off // d + g), pipeline_mode=once),
                  pl.BlockSpec((seq, d), lambda g, i, *_: (0, v_off // d + g), pipeline_mode=once),
                  pl.BlockSpec((None, LANES, tq), lambda g, i, *_: (g, 0, i)),
                  pl.BlockSpec((1, tq), lambda g, i, *_: (0, i)),
                  pl.BlockSpec((seq, LANES), lambda g, i, *_: (0, 0), pipeline_mode=once),
                  pl.BlockSpec((SUBLANES, LANES), lambda g, i, *_: (0, 0))],
        out_specs=pl.BlockSpec((tq, gw), lambda g, i, *_: (i, g)),
        scratch_shapes=[pltpu.VMEM((1, d, seq), BF16), pltpu.VMEM((NSA_REP, d, tq), BF16),
                        pltpu.VMEM((2, NSA_REP, tk, tq), F32), pltpu.VMEM((NSA_REP, 1, tq), F32),
                        pltpu.VMEM((NSA_REP, 1, tq), F32), pltpu.VMEM((NSA_REP, d, tq), F32),
                        pltpu.VMEM((n_cached, NSA_REP, tk, tq), F32)])
    return pl.pallas_call(
        functools.partial(_nsa_slc_kernel, tq=tq, tk=tk),
        grid_spec=grid_spec,
        out_shape=jax.ShapeDtypeStruct((seq, NSA_WIDTH), F32),
        compiler_params=_cparams("parallel", "arbitrary"),
        name="nsa_slc",
    )(pq_min, pk_max, consec, proj_b, proj_b, proj_b, sel, pos_row, pos_rep, tab_t)


def _nsa_win_kernel(*refs, tq, nt):
    consec_ref = refs[0]
    q_ref = refs[1]
    k_refs = refs[2:2 + nt]
    v_refs = refs[2 + nt:2 + 2 * nt]
    pk_refs = refs[2 + 2 * nt:2 + 3 * nt]
    posq_ref, tab_ref, gate_ref, gout_ref, oc_ref, os_ref, o_ref, bias_ref = refs[2 + 3 * nt:]
    qi = pl.program_id(0)
    d = NSA_HEAD_DIM
    c = d ** -0.5 * LOG2E
    sub = lax.broadcasted_iota(I32, (tq, 1), 0)
    lane = lax.broadcasted_iota(I32, (1, tq), 1)

    def band_bucket(kidx, tok, dist):
        lower = jnp.maximum(tok - (NSA_WINDOW - 1), 0)
        b = jnp.where(kidx >= lower, _t5_bucket(dist), T5_MASK_BUCKET)
        return jnp.where(kidx <= tok, b, T5_MASK_BUCKET)

    def attend(bias):
        gates = jax.nn.sigmoid(gate_ref[...])
        for g in range(NSA_KV_GROUPS):
            ks = [kr[:, g * d:(g + 1) * d] for kr in k_refs]
            vts = [vr[:, g * d:(g + 1) * d].astype(F32).T.astype(BF16) for vr in v_refs]
            for r in range(NSA_REP):
                h = g * NSA_REP + r
                hs = slice(h * d, (h + 1) * d)
                qt = (q_ref[:, hs].astype(F32) * c).T.astype(BF16)
                ts = [jnp.dot(ks[jj], qt, preferred_element_type=F32) + bias(h, jj) for jj in range(nt)]
                m = functools.reduce(jnp.maximum, [jnp.max(t, axis=0, keepdims=True) for t in ts])
                ps = [jnp.exp2(t - m) for t in ts]
                l = functools.reduce(jnp.add, [jnp.sum(p, axis=0, keepdims=True) for p in ps])
                o_t = functools.reduce(jnp.add, [jnp.dot(vt, p.astype(BF16), preferred_element_type=F32)
                                                 for vt, p in zip(vts, ps)])
                o_w = (o_t / l).T
                o = (oc_ref[:, hs] + gates[:, 3 * h + 1:3 * h + 2] * os_ref[:, hs]
                     + gates[:, 3 * h + 2:3 * h + 3] * o_w)
                o_ref[:, hs] = (o * _silu(gout_ref[:, hs])).astype(o_ref.dtype)

    consecutive = consec_ref[0] == 1

    @pl.when(jnp.logical_and(consecutive, qi == 0))
    def _():
        tok0 = (nt - 1) * tq + lane
        for jj in range(nt):
            kidx0 = jj * tq + sub
            bucket = band_bucket(kidx0, tok0, tok0 - kidx0)
            for h in range(NSA_HEADS):
                bias_ref[h, jj] = _t5_lookup(tab_ref[h:h + 1, :], bucket)

    cached = jnp.logical_and(consecutive, qi >= nt - 1)

    @pl.when(cached)
    def _():
        attend(lambda h, jj: bias_ref[h, jj])

    @pl.when(jnp.logical_not(cached))
    def _():
        tok = qi * tq + lane
        pos_q = posq_ref[...]
        buckets = []
        for jj in range(nt):
            kidx = (qi - (nt - 1) + jj) * tq + sub
            pos_k = jnp.concatenate([pk_refs[jj][...]] * (tq // LANES), axis=1)
            buckets.append(band_bucket(kidx, tok, pos_q - pos_k))
        attend(lambda h, jj: _t5_lookup(tab_ref[h:h + 1, :], buckets[jj]))


def _nsa_win(proj_b, q_cb, k_cb, v_cb, proj_f, gate_cb, gout_cb, oc, o_s, pos_row, pos_rep, consec, tab_t, *, tq):
    seq = proj_b.shape[0]
    nt = NSA_WINDOW // tq + 1

    def band_rows(jj, cb):
        return pl.BlockSpec((tq, NSA_KV), lambda i, *_: (jnp.maximum(i - (nt - 1) + jj, 0), cb))

    def band_pos(jj):
        return pl.BlockSpec((tq, LANES), lambda i, *_: (jnp.maximum(i - (nt - 1) + jj, 0), 0))

    in_specs = [pl.BlockSpec((tq, NSA_WIDTH), lambda i, *_: (i, q_cb))]
    in_specs += [band_rows(jj, k_cb) for jj in range(nt)]
    in_specs += [band_rows(jj, v_cb) for jj in range(nt)]
    in_specs += [band_pos(jj) for jj in range(nt)]
    in_specs += [pl.BlockSpec((1, tq), lambda i, *_: (0, i)),
                 pl.BlockSpec((SUBLANES, LANES), lambda i, *_: (0, 0)),
                 pl.BlockSpec((tq, LANES), lambda i, *_: (i, gate_cb)),
                 pl.BlockSpec((tq, NSA_WIDTH), lambda i, *_: (i, gout_cb)),
                 pl.BlockSpec((tq, NSA_WIDTH), lambda i, *_: (i, 0)),
                 pl.BlockSpec((tq, NSA_WIDTH), lambda i, *_: (i, 0))]
    args = [proj_b] * (1 + 2 * nt) + [pos_rep] * nt + [pos_row, tab_t * LOG2E, proj_f, proj_f, oc, o_s]
    grid_spec = pltpu.PrefetchScalarGridSpec(
        num_scalar_prefetch=1,
        grid=(seq // tq,),
        in_specs=in_specs,
        out_specs=pl.BlockSpec((tq, NSA_WIDTH), lambda i, *_: (i, 0)),
        scratch_shapes=[pltpu.VMEM((NSA_HEADS, nt, tq, tq), F32)])
    return pl.pallas_call(
        functools.partial(_nsa_win_kernel, tq=tq, nt=nt),
        grid_spec=grid_spec,
        out_shape=jax.ShapeDtypeStruct((seq, NSA_WIDTH), BF16),
        compiler_params=_cparams("arbitrary"),
        name="nsa_win",
    )(consec, *args)


def _in_proj(h, norm_g, w_b, w_f, tiles, name):
    proj_b = _norm_matmul(h, norm_g, w_b, tm=tiles["tm"], tn=512, name=name + "_b", out_dtype=BF16)
    proj_f = _norm_matmul(h, norm_g, w_f, tm=tiles["tm"], tn=768, name=name + "_f")
    return proj_b, proj_f


def _even_layer(h, mem_kv, norm_g, w_in, s5, w_glu, b_f, tiles):
    pb, pf = _in_proj(h, norm_g, *w_in, tiles, "in_proj_even")
    ob, of = EVEN_B_OFF, EVEN_F_OFF
    b_cat, c_cat, tab, d_skip = s5
    z = _s5_scan(pf, of["u"] // LANES, b_cat, c_cat, d_skip, tab, tc=tiles["s5_tc"])
    y_s5 = _s5_glu(z, w_glu, pf, of["g_s5"] // S5_WIDTH, tm=tiles["tm"], tn=512)
    decay = _decay(pf, of["f"] // LANES, b_f, t=tiles["decay_t"])
    d = FOX_HEAD_DIM
    y_fox = _flash(pb, ob["q"], pb, ob["k"], pb, ob["v"], pf, of["g_fox"], heads=FOX_HEADS, dk=d, dv=d,
                   scale=d ** -0.5, tq=tiles["attn_tq"], tk=tiles["attn_tk"], hp=tiles["attn_hp"], decay=decay,
                   name="fox_attn")
    y_mem = _mem_attn(pb, ob["q_mem"] // MEM_HEAD_DIM, pf, of["g_mem"] // MEM_HEAD_DIM, mem_kv, t=tiles["mem_t"])
    return y_s5, y_fox, y_mem


def _odd_layer(h, mem_kv, norm_g, w_in, mla, nsa, pos, tiles):
    pb, pf = _in_proj(h, norm_g, *w_in, tiles, "in_proj_odd")
    ob, of = ODD_B_OFF, ODD_F_OFF
    g_cq, g_ckv, w_uq, w_ukv, freq = mla
    pos_col_f, pos_row, pos_cmp_rep, pos_rep, pq_min, pk_max, consec = pos
    q_r, k_r, v_r = _mla_up(pf, of["c_q"] // MLA_Q_RANK, of["c_kv"] // MLA_KV_RANK, of["k_rope"] // LANES,
                            g_cq, g_ckv, w_uq, w_ukv, pos_col_f, freq, t=tiles["prep_t"])
    y_mla = _flash(q_r, 0, k_r, 0, v_r, 0, pf, of["g_mla"], heads=MLA_HEADS, dk=2 * LANES, dv=MLA_V,
                   scale=(MLA_NOPE + MLA_ROPE) ** -0.5, tq=tiles["attn_tq"], tk=tiles["attn_tk"], hp=tiles["attn_hp"],
                   name="mla_attn")
    pe, w1, w2, tab_t, ov, n_slc = nsa
    kvc = _nsa_compress(pb, ob["k_cmp"] // NSA_HEAD_DIM, pe, w1, w2)
    oc, sel = _nsa_select(pb, ob["q_nsa"] // NSA_WIDTH, pf, of["gates"] // LANES, kvc[0], kvc[1], pos_row,
                          pos_cmp_rep, consec, tab_t, ov, n_slc=n_slc, tq=tiles["nsa_tq"])
    o_s = _nsa_slc(pb, ob["q_nsa"], ob["k_slc"], ob["v_slc"], sel, pos_row, pos_rep, tab_t, pq_min, pk_max, consec,
                   tq=tiles["slc_tq"], tk=tiles["slc_tk"])
    y_nsa = _nsa_win(pb, ob["q_nsa"] // NSA_WIDTH, ob["k_win"] // NSA_KV, ob["v_win"] // NSA_KV,
                     pf, of["gates"] // LANES, of["g_nsa"] // NSA_WIDTH, oc, o_s, pos_row, pos_rep, consec, tab_t,
                     tq=tiles["win_tq"])
    y_mem = _mem_attn(pb, ob["q_mem"] // MEM_HEAD_DIM, pf, of["g_mem"] // MEM_HEAD_DIM, mem_kv, t=tiles["mem_t"])
    return y_mla, y_nsa, y_mem


def _tiles(seq):
    return {"tm": min(seq, 1024), "s5_tc": min(seq, 1024), "decay_t": min(seq, 512), "attn_tq": min(seq, 256),
            "attn_tk": min(seq, 512), "attn_hp": 2, "slc_tq": 256, "slc_tk": 512, "nsa_tq": 256, "win_tq": 128,
            "mem_t": min(seq, 1024), "norm_t": min(seq, 512), "prep_t": min(seq, 256)}


def _context(positions, t5_table, seq, tiles):
    pos = positions[0]
    pos_col = pos.reshape(seq, 1)
    pos_row = pos.reshape(1, seq)
    pos_rep = jnp.broadcast_to(pos_col, (seq, LANES))
    pq_min = jnp.min(pos.reshape(seq // tiles["slc_tq"], tiles["slc_tq"]), axis=1)
    pk_max = jnp.max(pos.reshape(seq // tiles["slc_tk"], tiles["slc_tk"]), axis=1)
    nc = seq // NSA_CMP_STRIDE
    pos_cmp = jnp.pad(pos[NSA_CMP_LEN - 1::NSA_CMP_STRIDE], (0, 1))
    pos_cmp_rep = jnp.broadcast_to(pos_cmp.reshape(nc, 1), (nc, LANES))
    half = MLA_ROPE // 2
    inv_freq = ROPE_THETA ** (-jnp.arange(half, dtype=F32) / half)
    freq = jnp.concatenate([inv_freq, inv_freq, jnp.zeros((LANES - MLA_ROPE,), F32)]).reshape(1, LANES)
    tab_t = jnp.pad(t5_table.astype(F32).T, ((0, SUBLANES - NSA_HEADS), (0, LANES - T5_BUCKETS)))
    tab_t = tab_t.at[:, T5_MASK_BUCKET].set(NEG_INF)
    n_slc = seq // NSA_SLC_BLOCK
    cs = np.arange(nc) * NSA_CMP_STRIDE
    ss = np.arange(LANES) * NSA_SLC_BLOCK
    ov_np = np.clip(np.minimum(cs[:, None] + NSA_CMP_LEN, ss[None, :] + NSA_SLC_BLOCK)
                    - np.maximum(cs[:, None], ss[None, :]), 0, None) / NSA_CMP_LEN
    ov_np[nc - 1, :] = 0.0
    ov_np[:, n_slc:] = 0.0
    consec = jnp.all(pos[1:] - pos[:-1] == 1).astype(I32).reshape(1)
    return {"pos": (pos_col.astype(F32), pos_row, pos_cmp_rep, pos_rep, pq_min, pk_max, consec), "freq": freq,
            "tab_t": tab_t, "ov": jnp.asarray(ov_np.T, BF16), "n_slc": n_slc}


def _odd_params(i, mla_g_cq, mla_g_ckv, mla_w_uq, mla_w_ukv, nsa_cmp_pe, nsa_cmp_w1, nsa_cmp_w2, ctx):
    dq = MLA_NOPE + MLA_ROPE
    w_uq = mla_w_uq[i].reshape(MLA_Q_RANK, MLA_HEADS, dq)
    w_uq = jnp.pad(w_uq, ((0, 0), (0, 0), (0, 2 * LANES - dq))).reshape(MLA_Q_RANK, -1).astype(BF16)
    mla = (mla_g_cq[i], mla_g_ckv[i], w_uq, mla_w_ukv[i].astype(BF16), ctx["freq"])
    nsa = (nsa_cmp_pe[i].astype(F32), nsa_cmp_w1[i].astype(BF16), nsa_cmp_w2[i].astype(BF16), ctx["tab_t"],
           ctx["ov"], ctx["n_slc"])
    return mla, nsa


def kernel(x, mem, positions, norm_g, mem_norm_g, final_norm_g, t5_table, w_out, mem_w_kv, even_w_in, s5_lam_re,
           s5_lam_im, s5_log_dt, s5_b_re, s5_b_im, s5_c_re, s5_c_im, s5_d, s5_w_glu, fox_b_f, odd_w_in, mla_g_cq,
           mla_g_ckv, mla_w_uq, mla_w_ukv, nsa_cmp_pe, nsa_cmp_w1, nsa_cmp_w2):
    batch, seq, _ = x.shape
    assert batch == 1 and seq % 1024 == 0 and seq // NSA_SLC_BLOCK <= LANES
    depth = norm_g.shape[0]
    tiles = _tiles(seq)
    ctx = _context(positions, t5_table, seq, tiles)
    h = x[0]
    w_out_b = w_out.astype(BF16)
    mem_kv_all = _mem_kv_all(mem[0], mem_norm_g, mem_w_kv.astype(BF16))
    for layer in range(depth):
        i = layer // 2
        mem_kv = (mem_kv_all, layer)
        if layer % 2 == 0:
            w_in = (_reorder_w_in(even_w_in[i], EVEN_SPLITS, EVEN_B_ORDER),
                    _reorder_w_in(even_w_in[i], EVEN_SPLITS, EVEN_F_ORDER))
            b_cat, c_cat, tab = _s5_prepare(s5_lam_re[i], s5_lam_im[i], s5_log_dt[i], s5_b_re[i], s5_b_im[i],
                                            s5_c_re[i], s5_c_im[i])
            ys = _even_layer(h, mem_kv, norm_g[layer], w_in, (b_cat, c_cat, tab, s5_d[i]),
                             s5_w_glu[i].astype(BF16), fox_b_f[i], tiles)
        else:
            w_in = (_reorder_w_in(odd_w_in[i], ODD_SPLITS, ODD_B_ORDER),
                    _reorder_w_in(odd_w_in[i], ODD_SPLITS, ODD_F_ORDER))
            mla, nsa = _odd_params(i, mla_g_cq, mla_g_ckv, mla_w_uq, mla_w_ukv, nsa_cmp_pe, nsa_cmp_w1, nsa_cmp_w2,
                                   ctx)
            ys = _odd_layer(h, mem_kv, norm_g[layer], w_in, mla, nsa, ctx["pos"], tiles)
        h = _out_proj(h, ys, w_out_b, layer, tm=tiles["tm"], tn=512)
    return _final_norm(h, final_norm_g, tm=tiles["norm_t"])[None]
```

```python
import functools
import math

import numpy as np
import jax
import jax.numpy as jnp
from jax import lax
from jax.experimental import pallas as pl
from jax.experimental.pallas import tpu as pltpu

F32 = jnp.float32
BF16 = jnp.bfloat16
I32 = jnp.int32

D_MODEL = 2048
DEPTH = 4
N_MEM = 256
RMS_EPS = 1e-6
NEG_INF = -1e30
LOG2E = math.log2(math.e)

S5_WIDTH = 1024
S5_GROUP = 16
S5_GROUPS = S5_WIDTH // S5_GROUP
S5_STATE = 64
FOX_HEADS = 8
FOX_HEAD_DIM = 128
FOX_WIDTH = FOX_HEADS * FOX_HEAD_DIM
MEM_HEADS = 4
MEM_HEAD_DIM = 128
MEM_WIDTH = MEM_HEADS * MEM_HEAD_DIM
MLA_HEADS = 8
MLA_Q_RANK = 512
MLA_KV_RANK = 512
MLA_NOPE = 128
MLA_ROPE = 64
MLA_V = 128
MLA_WIDTH = MLA_HEADS * MLA_V
ROPE_THETA = 10000.0
NSA_HEADS = 8
NSA_KV_GROUPS = 2
NSA_REP = NSA_HEADS // NSA_KV_GROUPS
NSA_HEAD_DIM = 128
NSA_WIDTH = NSA_HEADS * NSA_HEAD_DIM
NSA_KV = NSA_KV_GROUPS * NSA_HEAD_DIM
NSA_CMP_LEN = 32
NSA_CMP_STRIDE = 16
NSA_CMP_HIDDEN = 256
NSA_SLC_BLOCK = 64
NSA_SLC_TOPK = 16
NSA_WINDOW = 512
FORCE_SCORE = 1e6
T5_BUCKETS = 32
T5_MAX_DIST = 1024

EVEN_SPLITS = (S5_WIDTH, S5_WIDTH, FOX_WIDTH, FOX_WIDTH, FOX_WIDTH, FOX_HEADS, FOX_WIDTH, MEM_WIDTH, MEM_WIDTH)
ODD_SPLITS = (MLA_Q_RANK, MLA_KV_RANK, MLA_ROPE, MLA_WIDTH, NSA_WIDTH, NSA_KV, NSA_KV, NSA_KV, NSA_KV, NSA_KV,
              NSA_KV, 3 * NSA_HEADS, NSA_WIDTH, MEM_WIDTH, MEM_WIDTH)

LANES = 128
SUBLANES = 8
VMEM_LIMIT_BYTES = 56 * 1024 * 1024

EVEN_B_ORDER = (("q", 2, 1024), ("k", 3, 1024), ("v", 4, 1024), ("q_mem", 7, 512))
EVEN_F_ORDER = (("u", 0, 1024), ("g_s5", 1, 1024), ("g_fox", 6, 1024), ("g_mem", 8, 512), ("f", 5, 128),
                ("pad", None, 128))
ODD_B_ORDER = (("q_nsa", 4, 1024), ("q_mem", 13, 512), ("k_cmp", 5, 256), ("v_cmp", 6, 256), ("k_slc", 7, 256),
               ("v_slc", 8, 256), ("k_win", 9, 256), ("v_win", 10, 256))
ODD_F_ORDER = (("g_nsa", 12, 1024), ("g_mla", 3, 1024), ("c_q", 0, 512), ("c_kv", 1, 512), ("g_mem", 14, 512),
               ("k_rope", 2, 128), ("gates", 11, 128))


def _layout(order):
    off, out = 0, {}
    for name, _, width in order:
        assert off % width == 0
        out[name] = off
        off += width
    return out, off


EVEN_B_OFF, EVEN_B_N = _layout(EVEN_B_ORDER)
EVEN_F_OFF, EVEN_F_N = _layout(EVEN_F_ORDER)
ODD_B_OFF, ODD_B_N = _layout(ODD_B_ORDER)
ODD_F_OFF, ODD_F_N = _layout(ODD_F_ORDER)


def _reorder_w_in(w, splits, order):
    starts = np.concatenate([[0], np.cumsum(splits)])
    cols = []
    for _, idx, width in order:
        if idx is None:
            cols.append(jnp.zeros((w.shape[0], width), w.dtype))
            continue
        seg = w[:, int(starts[idx]):int(starts[idx + 1])]
        pad = width - seg.shape[1]
        if pad:
            seg = jnp.pad(seg, ((0, 0), (0, pad)))
        cols.append(seg)
    return jnp.concatenate(cols, axis=1).astype(BF16)


def _cparams(*sem):
    return pltpu.CompilerParams(dimension_semantics=sem, vmem_limit_bytes=VMEM_LIMIT_BYTES)


def _silu(g):
    return g * jax.nn.sigmoid(g)


def _pick(n, cands):
    for c in cands:
        if n % c == 0:
            return c
    raise ValueError(f"no tile for {n} in {cands}")


def _norm_matmul_kernel(x_ref, g_ref, w_ref, o_ref, xn_ref):
    @pl.when(pl.program_id(1) == 0)
    def _():
        x = x_ref[...]
        ms = jnp.mean(x * x, axis=-1, keepdims=True)
        xn_ref[...] = (x * lax.rsqrt(ms + RMS_EPS) * g_ref[...]).astype(BF16)

    o_ref[...] = jnp.dot(xn_ref[...], w_ref[...].astype(BF16), preferred_element_type=F32).astype(o_ref.dtype)


def _norm_matmul(x, g, w, *, x_cb=0, tm, tn, name, out_dtype=F32):
    m = x.shape[0]
    k, n = w.shape
    return pl.pallas_call(
        _norm_matmul_kernel,
        grid=(m // tm, n // tn),
        in_specs=[pl.BlockSpec((tm, k), lambda i, j: (i, x_cb)),
                  pl.BlockSpec((1, k), lambda i, j: (0, 0)),
                  pl.BlockSpec((k, tn), lambda i, j: (0, j))],
        out_specs=pl.BlockSpec((tm, tn), lambda i, j: (i, j)),
        out_shape=jax.ShapeDtypeStruct((m, n), out_dtype),
        scratch_shapes=[pltpu.VMEM((tm, k), BF16)],
        compiler_params=_cparams("parallel", "arbitrary"),
        name=name,
    )(x, g.reshape(1, k), w)


def _out_proj_kernel(h_ref, *refs):
    o_ref = refs[-1]
    n = (len(refs) - 1) // 2
    acc = h_ref[...]
    for y_ref, w_ref in zip(refs[:n], refs[n:2 * n]):
        acc = acc + jnp.dot(y_ref[...], w_ref[...].astype(BF16), preferred_element_type=F32)
    o_ref[...] = acc


def _out_proj(h, ys, w_all, layer, *, tm, tn):
    m, n = h.shape
    in_specs = [pl.BlockSpec((tm, tn), lambda i, j: (i, j))]
    in_specs += [pl.BlockSpec((tm, y.shape[1]), lambda i, j: (i, 0)) for y in ys]
    row = 0
    for y in ys:
        width = y.shape[1]
        assert row % width == 0
        in_specs.append(pl.BlockSpec((None, width, tn), lambda i, j, rb=row // width: (layer, rb, j)))
        row += width
    return pl.pallas_call(
        _out_proj_kernel,
        grid=(m // tm, n // tn),
        in_specs=in_specs,
        out_specs=pl.BlockSpec((tm, tn), lambda i, j: (i, j)),
        out_shape=jax.ShapeDtypeStruct((m, n), F32),
        compiler_params=_cparams("parallel", "arbitrary"),
        name="out_proj",
    )(h, *ys, *([w_all] * len(ys)))


def _final_norm_kernel(x_ref, g_ref, o_ref):
    x = x_ref[...]
    ms = jnp.mean(x * x, axis=-1, keepdims=True)
    o_ref[...] = x * lax.rsqrt(ms + RMS_EPS) * g_ref[...]


def _final_norm(h, g, *, tm):
    m, n = h.shape
    return pl.pallas_call(
        _final_norm_kernel,
        grid=(m // tm,),
        in_specs=[pl.BlockSpec((tm, n), lambda i: (i, 0)), pl.BlockSpec((1, n), lambda i: (0, 0))],
        out_specs=pl.BlockSpec((tm, n), lambda i: (i, 0)),
        out_shape=jax.ShapeDtypeStruct((m, n), F32),
        compiler_params=_cparams("parallel"),
        name="final_norm",
    )(h, g.reshape(1, n))


ONES_ROWS = 16


def _build_vt(v_ref, vt_ref, hh, dv, seq, chunk):
    def body(c, carry):
        st = pl.multiple_of(c * chunk, chunk)
        vt_ref[hh, 0:dv, pl.ds(st, chunk)] = v_ref[pl.ds(st, chunk), hh * dv:(hh + 1) * dv].astype(F32).T.astype(BF16)
        return carry

    lax.fori_loop(0, seq // chunk, body, 0)
    extra = vt_ref.shape[1] - dv
    if extra:
        vt_ref[hh, dv:dv + extra, :] = jnp.ones((extra, seq), BF16)


def _flash_kernel(*refs, scale, tq, tk, hp, dk, dv, has_decay):
    if has_decay:
        q_ref, k_ref, v_ref, g_ref, cq_ref, ck_ref, o_ref, vt_ref, qt_ref, s_ref, m_ref, acc_ref = refs
    else:
        q_ref, k_ref, v_ref, g_ref, o_ref, vt_ref, qt_ref, s_ref, m_ref, acc_ref = refs
    hb = pl.program_id(0)
    qi = pl.program_id(1)
    seq = k_ref.shape[0]

    @pl.when(qi == 0)
    def _():
        for hh in range(hp):
            _build_vt(v_ref, vt_ref, hh, dv, seq, tk)

    for hh in range(hp):
        qt_ref[hh] = (q_ref[:, hh * dk:(hh + 1) * dk].astype(F32) * (scale * LOG2E)).T.astype(BF16)
    m_ref[...] = jnp.full(m_ref.shape, NEG_INF, F32)
    acc_ref[...] = jnp.zeros(acc_ref.shape, F32)
    n_full = (qi * tq) // tk
    if has_decay:
        cq2 = [cq_ref[pl.ds(hb * hp + hh, 1), :] for hh in range(hp)]

    def scores(j, slot):
        start = pl.multiple_of(j * tk, tk)
        for hh in range(hp):
            s_ref[slot, hh] = jnp.dot(k_ref[pl.ds(start, tk), hh * dk:(hh + 1) * dk], qt_ref[hh],
                                      preferred_element_type=F32)

    def softmax_pv(j, slot, masked):
        start = pl.multiple_of(j * tk, tk)
        for hh in range(hp):
            t = s_ref[slot, hh]
            if has_decay:
                t = t - jnp.concatenate([ck_ref[hh, pl.ds(start, tk), :]] * (tq // LANES), axis=1)
            if masked:
                key = start + lax.broadcasted_iota(I32, (tk, tq), 0)
                qry = qi * tq + lax.broadcasted_iota(I32, (tk, tq), 1)
                t = jnp.where(key <= qry, t, NEG_INF)
            m_prev = m_ref[hh]
            mx = jnp.max(t, axis=0, keepdims=True)
            if has_decay:
                m_new = jnp.maximum(m_prev, mx + cq2[hh])
                shift = m_new - cq2[hh]
            else:
                m_new = jnp.maximum(m_prev, mx)
                shift = m_new
            alpha = jnp.exp2(m_prev - m_new)
            p = jnp.exp2(t - shift).astype(BF16)
            acc_ref[hh] = alpha * acc_ref[hh] + jnp.dot(vt_ref[hh, :, pl.ds(start, tk)], p,
                                                         preferred_element_type=F32)
            m_ref[hh] = m_new

    scores(0, 0)

    def pair(j):
        scores(j + 1, 1)
        softmax_pv(j, 0, False)
        scores(j + 2, 0)
        softmax_pv(j + 1, 1, False)

    def quad(qq, carry):
        pair(4 * qq)
        pair(4 * qq + 2)
        return carry

    quads = n_full // 4
    lax.fori_loop(0, quads, quad, 0)
    rem = n_full - 4 * quads

    @pl.when(rem >= 2)
    def _():
        pair(4 * quads)

    last = 4 * quads + 2 * (rem // 2)

    @pl.when(n_full % 2 == 1)
    def _():
        scores(last + 1, 1)
        softmax_pv(last, 0, False)
        softmax_pv(last + 1, 1, True)

    @pl.when(n_full % 2 == 0)
    def _():
        softmax_pv(last, 0, True)

    for hh in range(hp):
        a = acc_ref[hh]
        o = (a[0:dv, :] / a[dv:dv + 1, :]).T
        o_ref[:, hh * dv:(hh + 1) * dv] = (o * _silu(g_ref[:, hh * dv:(hh + 1) * dv])).astype(o_ref.dtype)


def _flash(q_arr, q_off, k_arr, k_off, v_arr, v_off, g_arr, g_off, *, heads, dk, dv, scale, tq, tk, hp,
           decay=None, name):
    seq = q_arr.shape[0]
    once = pl.Buffered(1)
    in_specs = [pl.BlockSpec((tq, hp * dk), lambda h, i: (i, q_off // (hp * dk) + h)),
                pl.BlockSpec((seq, hp * dk), lambda h, i: (0, k_off // (hp * dk) + h), pipeline_mode=once),
                pl.BlockSpec((seq, hp * dv), lambda h, i: (0, v_off // (hp * dv) + h), pipeline_mode=once),
                pl.BlockSpec((tq, hp * dv), lambda h, i: (i, g_off // (hp * dv) + h))]
    args = [q_arr, k_arr, v_arr, g_arr]
    if decay is not None:
        cum_t, cum_rep = decay
        in_specs += [pl.BlockSpec((SUBLANES, tq), lambda h, i: (0, i)),
                     pl.BlockSpec((hp, seq, LANES), lambda h, i: (h, 0, 0), pipeline_mode=once)]
        args += [cum_t, cum_rep]
    return pl.pallas_call(
        functools.partial(_flash_kernel, scale=scale, tq=tq, tk=tk, hp=hp, dk=dk, dv=dv, has_decay=decay is not None),
        grid=(heads // hp, seq // tq),
        in_specs=in_specs,
        out_specs=pl.BlockSpec((tq, hp * dv), lambda h, i: (i, h)),
        out_shape=jax.ShapeDtypeStruct((seq, heads * dv), BF16),
        scratch_shapes=[pltpu.VMEM((hp, dv + ONES_ROWS, seq), BF16), pltpu.VMEM((hp, dk, tq), BF16),
                        pltpu.VMEM((2, hp, tk, tq), F32), pltpu.VMEM((hp, 1, tq), F32),
                        pltpu.VMEM((hp, dv + ONES_ROWS, tq), F32)],
        compiler_params=_cparams("parallel", "arbitrary"),
        name=name,
    )(*args)


def _decay_kernel(f_ref, b_ref, ct_ref, cr_ref, carry_ref, *, t):
    i = pl.program_id(0)

    @pl.when(i == 0)
    def _():
        carry_ref[...] = jnp.zeros(carry_ref.shape, F32)

    x = f_ref[...] + b_ref[...]
    lf = jnp.minimum(x, 0.0) - jnp.log1p(jnp.exp(-jnp.abs(x)))
    row = lax.broadcasted_iota(I32, lf.shape, 0)
    s = 1
    while s < t:
        lf = lf + jnp.where(row >= s, pltpu.roll(lf, s, 0), 0.0)
        s *= 2
    lf = lf + carry_ref[...]
    carry_ref[...] = lf[t - 1:t, :]
    lf2 = lf * LOG2E
    ct_ref[...] = lf2.T[:FOX_HEADS, :]
    for h in range(FOX_HEADS):
        cr_ref[h] = jnp.broadcast_to(lf2[:, h:h + 1], (t, LANES))


def _decay(proj, f_cb, b_f, *, t):
    seq = proj.shape[0]
    b = jnp.pad(b_f.reshape(1, FOX_HEADS), ((0, 0), (0, LANES - FOX_HEADS)))
    return pl.pallas_call(
        functools.partial(_decay_kernel, t=t),
        grid=(seq // t,),
        in_specs=[pl.BlockSpec((t, LANES), lambda i: (i, f_cb)), pl.BlockSpec((1, LANES), lambda i: (0, 0))],
        out_specs=[pl.BlockSpec((FOX_HEADS, t), lambda i: (0, i)),
                   pl.BlockSpec((FOX_HEADS, t, LANES), lambda i: (0, i, 0))],
        out_shape=[jax.ShapeDtypeStruct((FOX_HEADS, seq), F32), jax.ShapeDtypeStruct((FOX_HEADS, seq, LANES), F32)],
        scratch_shapes=[pltpu.VMEM((1, LANES), F32)],
        compiler_params=_cparams("arbitrary"),
        name="fox_decay",
    )(proj, b)


def _mem_attn_kernel(q_ref, kv_ref, g_ref, o_ref):
    d = MEM_HEAD_DIM
    for h in range(MEM_HEADS):
        hs = slice(h * d, (h + 1) * d)
        k = kv_ref[:, hs].astype(BF16)
        v = kv_ref[:, MEM_WIDTH + h * d:MEM_WIDTH + (h + 1) * d].astype(BF16)
        s = lax.dot_general(q_ref[:, hs], k, (((1,), (1,)), ((), ())), preferred_element_type=F32) * (d ** -0.5)
        m = jnp.max(s, axis=1, keepdims=True)
        p = jnp.exp(s - m)
        l = jnp.sum(p, axis=1, keepdims=True)
        o = jnp.dot(p.astype(BF16), v, preferred_element_type=F32) / l
        o_ref[:, hs] = (o * _silu(g_ref[:, hs])).astype(o_ref.dtype)


def _mem_kv_all(mem2d, g, w_all):
    depth, k, n = w_all.shape
    m = mem2d.shape[0]
    tn = 512
    return pl.pallas_call(
        _norm_matmul_kernel,
        grid=(depth, n // tn),
        in_specs=[pl.BlockSpec((m, k), lambda l, j: (0, 0)),
                  pl.BlockSpec((1, k), lambda l, j: (0, 0)),
                  pl.BlockSpec((None, k, tn), lambda l, j: (l, 0, j))],
        out_specs=pl.BlockSpec((None, m, tn), lambda l, j: (l, 0, j)),
        out_shape=jax.ShapeDtypeStruct((depth, m, n), F32),
        scratch_shapes=[pltpu.VMEM((m, k), BF16)],
        compiler_params=_cparams("arbitrary", "arbitrary"),
        name="mem_kv",
    )(mem2d, g.reshape(1, k), w_all)


def _mem_attn(proj_b, q_cb, proj_f, g_cb, mem_kv, *, t):
    seq = proj_b.shape[0]
    mem_kv, layer = mem_kv
    nm = mem_kv.shape[1]
    w = MEM_WIDTH
    d = MEM_HEAD_DIM
    return pl.pallas_call(
        _mem_attn_kernel,
        grid=(seq // t,),
        in_specs=[pl.BlockSpec((t, w), lambda i: (i, q_cb * d // w)),
                  pl.BlockSpec((None, nm, 2 * w), lambda i: (layer, 0, 0)),
                  pl.BlockSpec((t, w), lambda i: (i, g_cb * d // w))],
        out_specs=pl.BlockSpec((t, w), lambda i: (i, 0)),
        out_shape=jax.ShapeDtypeStruct((seq, w), BF16),
        compiler_params=_cparams("parallel"),
        name="mem_attn",
    )(proj_b, mem_kv, proj_f)


S5_TILE_GROUPS = LANES // S5_GROUP
S5_TILE_STATES = S5_TILE_GROUPS * S5_STATE
S5_TILES = S5_GROUPS // S5_TILE_GROUPS


def _s5_scan_kernel(u_ref, b_ref, c_ref, d_ref, tab_ref, z_ref, bu_ref, carry_ref, *, tc):
    ns = S5_TILE_STATES

    @pl.when(pl.program_id(1) == 0)
    def _():
        carry_ref[...] = jnp.zeros(carry_ref.shape, F32)

    u = u_ref[...]
    bu_ref[...] = jnp.dot(u.astype(BF16), b_ref[...], preferred_element_type=F32)
    steps = [(1, tab_ref[0], tab_ref[1]), (2, tab_ref[2], tab_ref[3]), (4, tab_ref[4], tab_ref[5])]
    pr = tab_ref[6]
    pi = tab_ref[7]

    def body(i, carry):
        cr, ci = carry
        r0 = pl.multiple_of(i * SUBLANES, SUBLANES)
        xr = bu_ref[pl.ds(r0, SUBLANES), 0:ns]
        xi = bu_ref[pl.ds(r0, SUBLANES), ns:2 * ns]
        for s, ar, ai in steps:
            sr = pltpu.roll(xr, s, 0)
            si = pltpu.roll(xi, s, 0)
            xr, xi = xr + ar * sr - ai * si, xi + ar * si + ai * sr
        xr, xi = xr + pr * cr - pi * ci, xi + pr * ci + pi * cr
        bu_ref[pl.ds(r0, SUBLANES), 0:ns] = xr
        bu_ref[pl.ds(r0, SUBLANES), ns:2 * ns] = xi
        return xr[SUBLANES - 1:SUBLANES, :], xi[SUBLANES - 1:SUBLANES, :]

    cr, ci = lax.fori_loop(0, tc // SUBLANES, body, (carry_ref[0:1, 0:ns], carry_ref[0:1, ns:2 * ns]))
    carry_ref[0:1, 0:ns] = cr
    carry_ref[0:1, ns:2 * ns] = ci
    y = jnp.dot(bu_ref[...].astype(BF16), c_ref[...], preferred_element_type=F32) + d_ref[...] * u
    z_ref[...] = jax.nn.gelu(y)


def _s5_prepare(lam_re, lam_im, log_dt, b_re, b_im, c_re, c_im):
    dt = jnp.exp(log_dt.astype(F32))[:, None]
    lr = lam_re.astype(F32)
    li = lam_im.astype(F32)
    mag = jnp.exp(lr * dt)
    ab_re = mag * jnp.cos(li * dt)
    ab_im = mag * jnp.sin(li * dt)
    den = lr * lr + li * li
    nr = ab_re - 1.0
    f_re = (nr * lr + ab_im * li) / den
    f_im = (ab_im * lr - nr * li) / den
    br = b_re.astype(F32)
    bim = b_im.astype(F32)
    bb_re = f_re[..., None] * br - f_im[..., None] * bim
    bb_im = f_re[..., None] * bim + f_im[..., None] * br
    eye = jnp.eye(S5_TILE_GROUPS, dtype=F32)

    def blockdiag_in(bb):
        t = bb.reshape(S5_TILES, S5_TILE_GROUPS, S5_STATE, S5_GROUP)
        m = jnp.einsum("jgpc,gh->jgchp", t, eye)
        return m.reshape(S5_TILES, LANES, S5_TILE_STATES)

    def blockdiag_out(cc):
        t = cc.reshape(S5_TILES, S5_TILE_GROUPS, S5_GROUP, S5_STATE)
        m = jnp.einsum("jgcp,gh->jgphc", t, eye)
        return m.reshape(S5_TILES, S5_TILE_STATES, LANES)

    b_cat = jnp.concatenate([blockdiag_in(bb_re), blockdiag_in(bb_im)], axis=2).astype(BF16)
    c_cat = jnp.concatenate([blockdiag_out(c_re.astype(F32)), -blockdiag_out(c_im.astype(F32))], axis=1).astype(BF16)

    a_r = ab_re.reshape(S5_TILES, 1, S5_TILE_STATES)
    a_i = ab_im.reshape(S5_TILES, 1, S5_TILE_STATES)

    def cmul(xr, xi, yr, yi):
        return xr * yr - xi * yi, xr * yi + xi * yr

    a2 = cmul(a_r, a_i, a_r, a_i)
    a4 = cmul(*a2, *a2)
    row = jnp.arange(SUBLANES)[None, :, None]
    tabs = []
    for s, (pr_, pi_) in ((1, (a_r, a_i)), (2, a2), (4, a4)):
        tabs.append(jnp.where(row >= s, pr_, 0.0))
        tabs.append(jnp.where(row >= s, pi_, 0.0))
    pw = [(a_r, a_i)]
    for _ in range(SUBLANES - 1):
        pw.append(cmul(*pw[-1], a_r, a_i))
    tabs.append(jnp.concatenate([p[0] for p in pw], axis=1))
    tabs.append(jnp.concatenate([p[1] for p in pw], axis=1))
    tab = jnp.stack([jnp.broadcast_to(t, (S5_TILES, SUBLANES, S5_TILE_STATES)) for t in tabs], axis=1)
    return b_cat, c_cat, tab.astype(F32)


def _s5_scan(proj, u_cb, b_cat, c_cat, d_skip, tab, *, tc):
    seq = proj.shape[0]
    ns = S5_TILE_STATES
    d = d_skip.astype(F32).reshape(S5_TILES, 1, LANES)
    return pl.pallas_call(
        functools.partial(_s5_scan_kernel, tc=tc),
        grid=(S5_TILES, seq // tc),
        in_specs=[pl.BlockSpec((tc, LANES), lambda j, c: (c, u_cb + j)),
                  pl.BlockSpec((None, LANES, 2 * ns), lambda j, c: (j, 0, 0)),
                  pl.BlockSpec((None, 2 * ns, LANES), lambda j, c: (j, 0, 0)),
                  pl.BlockSpec((None, 1, LANES), lambda j, c: (j, 0, 0)),
                  pl.BlockSpec((None, 8, SUBLANES, ns), lambda j, c: (j, 0, 0, 0))],
        out_specs=pl.BlockSpec((tc, LANES), lambda j, c: (c, j)),
        out_shape=jax.ShapeDtypeStruct((seq, S5_WIDTH), F32),
        scratch_shapes=[pltpu.VMEM((tc, 2 * ns), F32), pltpu.VMEM((SUBLANES, 2 * ns), F32)],
        compiler_params=_cparams("parallel", "arbitrary"),
        name="s5_scan",
    )(proj, b_cat, c_cat, d, tab)


def _s5_glu_kernel(z_ref, w_ref, g_ref, o_ref, *, tn):
    j = pl.program_id(1)
    z = z_ref[...]
    a = jnp.dot(z.astype(BF16), w_ref[...].astype(BF16), preferred_element_type=F32)
    zc = z_ref[:, pl.ds(pl.multiple_of(j * tn, tn), tn)]
    o_ref[...] = (zc * jax.nn.sigmoid(a) * _silu(g_ref[...])).astype(o_ref.dtype)


def _s5_glu(z, w_glu, proj, g_cb, *, tm, tn):
    seq, n = z.shape
    return pl.pallas_call(
        functools.partial(_s5_glu_kernel, tn=tn),
        grid=(seq // tm, n // tn),
        in_specs=[pl.BlockSpec((tm, n), lambda i, j: (i, 0)),
                  pl.BlockSpec((n, tn), lambda i, j: (0, j)),
                  pl.BlockSpec((tm, tn), lambda i, j: (i, g_cb * (n // tn) + j))],
        out_specs=pl.BlockSpec((tm, tn), lambda i, j: (i, j)),
        out_shape=jax.ShapeDtypeStruct((seq, n), BF16),
        compiler_params=_cparams("parallel", "arbitrary"),
        name="s5_glu",
    )(z, w_glu, proj)


def _rope_tables(pos, freq):
    ang = pos * freq
    lane = lax.broadcasted_iota(I32, ang.shape, 1)
    half = MLA_ROPE // 2
    cos = jnp.cos(ang)
    sin = jnp.sin(ang)
    c = jnp.where(lane < MLA_ROPE, cos, 0.0)
    s1 = jnp.where(lane < half, -sin, 0.0)
    s2 = jnp.where((lane >= half) & (lane < MLA_ROPE), sin, 0.0)
    return c, s1, s2


def _rope_apply(x, c, s1, s2):
    half = MLA_ROPE // 2
    return x * c + pltpu.roll(x, LANES - half, 1) * s1 + pltpu.roll(x, half, 1) * s2


def _mla_up_kernel(cq_ref, ckv_ref, kr_ref, pos_ref, freq_ref, gq_ref, gkv_ref, wq_ref, wkv_ref, q_ref, k_ref, v_ref):
    def normed(x_ref, g_ref):
        x = x_ref[...]
        ms = jnp.mean(x * x, axis=-1, keepdims=True)
        return (x * lax.rsqrt(ms + RMS_EPS) * g_ref[...]).astype(BF16)

    qf = jnp.dot(normed(cq_ref, gq_ref), wq_ref[...], preferred_element_type=F32)
    kvf = jnp.dot(normed(ckv_ref, gkv_ref), wkv_ref[...], preferred_element_type=F32)
    c, s1, s2 = _rope_tables(pos_ref[...], freq_ref[...])
    kr = _rope_apply(kr_ref[...], c, s1, s2).astype(BF16)
    for h in range(MLA_HEADS):
        b = 2 * LANES * h
        q_ref[:, b:b + LANES] = qf[:, b:b + LANES].astype(BF16)
        q_ref[:, b + LANES:b + 2 * LANES] = _rope_apply(qf[:, b + LANES:b + 2 * LANES], c, s1, s2).astype(BF16)
        k_ref[:, b:b + LANES] = kvf[:, b:b + LANES].astype(BF16)
        k_ref[:, b + LANES:b + 2 * LANES] = kr
        v_ref[:, LANES * h:LANES * (h + 1)] = kvf[:, b + LANES:b + 2 * LANES].astype(BF16)


def _mla_up(proj, cq_cb, ckv_cb, kr_cb, g_cq, g_ckv, w_uq, w_ukv, pos_col, freq, *, t):
    seq = proj.shape[0]
    w = 2 * LANES * MLA_HEADS
    rq, rkv = w_uq.shape[0], w_ukv.shape[0]
    return pl.pallas_call(
        _mla_up_kernel,
        grid=(seq // t,),
        in_specs=[pl.BlockSpec((t, rq), lambda i: (i, cq_cb)),
                  pl.BlockSpec((t, rkv), lambda i: (i, ckv_cb)),
                  pl.BlockSpec((t, LANES), lambda i: (i, kr_cb)),
                  pl.BlockSpec((t, 1), lambda i: (i, 0)),
                  pl.BlockSpec((1, LANES), lambda i: (0, 0)),
                  pl.BlockSpec((1, rq), lambda i: (0, 0)),
                  pl.BlockSpec((1, rkv), lambda i: (0, 0)),
                  pl.BlockSpec((rq, w), lambda i: (0, 0)),
                  pl.BlockSpec((rkv, w), lambda i: (0, 0))],
        out_specs=[pl.BlockSpec((t, w), lambda i: (i, 0)),
                   pl.BlockSpec((t, w), lambda i: (i, 0)),
                   pl.BlockSpec((t, MLA_WIDTH), lambda i: (i, 0))],
        out_shape=[jax.ShapeDtypeStruct((seq, w), BF16), jax.ShapeDtypeStruct((seq, w), BF16),
                   jax.ShapeDtypeStruct((seq, MLA_WIDTH), BF16)],
        compiler_params=_cparams("parallel"),
        name="mla_up",
    )(proj, proj, proj, pos_col, freq, g_cq.reshape(1, rq), g_ckv.reshape(1, rkv), w_uq, w_ukv)


def _t5_bucket(dist):
    n = jnp.maximum(dist, 0)
    max_exact = T5_BUCKETS // 2
    log_ratio = jnp.log(jnp.maximum(n, 1).astype(F32) / max_exact) / math.log(T5_MAX_DIST / max_exact)
    large = jnp.minimum(max_exact + (log_ratio * (T5_BUCKETS - max_exact)).astype(I32), T5_BUCKETS - 1)
    return jnp.where(n < max_exact, n, large)


T5_MASK_BUCKET = T5_BUCKETS


def _t5_lookup(table_row, bucket):
    rows, width = bucket.shape
    tab = jnp.broadcast_to(table_row, (rows, LANES))
    parts = [jnp.take_along_axis(tab, bucket[:, c:c + LANES], axis=1, mode="promise_in_bounds")
             for c in range(0, width, LANES)]
    return parts[0] if len(parts) == 1 else jnp.concatenate(parts, axis=1)


def _nsa_cmp_kernel(x_ref, pe_ref, w1_ref, w2_ref, o_ref, xf_ref, *, nc):
    half = NSA_CMP_LEN // 2
    d = NSA_HEAD_DIM
    xf_ref[...] = x_ref[...].astype(F32)
    u = jnp.zeros((nc, NSA_CMP_HIDDEN), F32)
    v = jnp.zeros((nc, NSA_CMP_HIDDEN), F32)
    for r in range(half):
        a = xf_ref[pl.ds(r, nc, stride=NSA_CMP_STRIDE), :]
        u = u + jnp.dot((a + pe_ref[r:r + 1, :]).astype(BF16), w1_ref[r * d:(r + 1) * d, :].astype(BF16),
                        preferred_element_type=F32)
        v = v + jnp.dot((a + pe_ref[half + r:half + r + 1, :]).astype(BF16),
                        w1_ref[(half + r) * d:(half + r + 1) * d, :].astype(BF16), preferred_element_type=F32)
    hid = u + pltpu.roll(v, nc - 1, 0)
    o_ref[...] = jnp.dot(jax.nn.gelu(hid).astype(BF16), w2_ref[...].astype(BF16),
                         preferred_element_type=F32).astype(o_ref.dtype)


def _nsa_compress(proj, k_cb, pe, w1, w2):
    seq = proj.shape[0]
    nc = seq // NSA_CMP_STRIDE
    d = NSA_HEAD_DIM
    g = NSA_KV_GROUPS
    return pl.pallas_call(
        functools.partial(_nsa_cmp_kernel, nc=nc),
        grid=(2, g),
        in_specs=[pl.BlockSpec((seq, d), lambda a, b: (0, k_cb + a * g + b)),
                  pl.BlockSpec((None, NSA_CMP_LEN, d), lambda a, b: (a, 0, 0)),
                  pl.BlockSpec((None, NSA_CMP_LEN * d, NSA_CMP_HIDDEN), lambda a, b: (a, 0, 0)),
                  pl.BlockSpec((None, NSA_CMP_HIDDEN, d), lambda a, b: (a, 0, 0))],
        out_specs=pl.BlockSpec((None, None, nc, d), lambda a, b: (a, b, 0, 0)),
        out_shape=jax.ShapeDtypeStruct((2, g, nc, d), BF16),
        scratch_shapes=[pltpu.VMEM((seq, d), F32)],
        compiler_params=_cparams("parallel", "arbitrary"),
        name="nsa_compress",
    )(proj, pe, w1, w2)


def _nsa_select_kernel(consec_ref, q_ref, kc_ref, vc_ref, posq_ref, posc_ref, tab_ref, gate_ref, ov_ref,
                       oc_ref, sel_ref, strip_ref, *, nc, n_slc, tq):
    qi = pl.program_id(0)
    d = NSA_HEAD_DIM
    c = d ** -0.5 * LOG2E
    per_block = tq // NSA_CMP_STRIDE
    lane = lax.broadcasted_iota(I32, (1, tq), 1)
    tok = qi * tq + lane

    def select(bias):
        gates = jax.nn.sigmoid(gate_ref[...])
        ovt = ov_ref[...]
        js = lax.broadcasted_iota(I32, (LANES, tq), 0)
        jf = js.astype(F32)
        cur = tok // NSA_SLC_BLOCK
        forced = (js == 0) | (js == cur) | (js == cur - 1)
        for g in range(NSA_KV_GROUPS):
            kc = kc_ref[g]
            vct = vc_ref[g].astype(F32).T.astype(BF16)
            psum = jnp.zeros((nc, tq), F32)
            for r in range(NSA_REP):
                h = g * NSA_REP + r
                qt = (q_ref[:, h * d:(h + 1) * d].astype(F32) * c).T.astype(BF16)
                t = jnp.dot(kc, qt, preferred_element_type=F32) + bias(h)
                m = jnp.max(t, axis=0, keepdims=True)
                e = jnp.exp2(t - m)
                l = jnp.sum(e, axis=0, keepdims=True)
                p = e * jnp.where(m > 0.5 * NEG_INF, 1.0 / l, 0.0)
                o = jnp.dot(vct, p.astype(BF16), preferred_element_type=F32)
                oc_ref[:, h * d:(h + 1) * d] = gates[:, 3 * h:3 * h + 1] * o.T
                psum = psum + p
            p_hi = psum.astype(BF16)
            p_lo = (psum - p_hi.astype(F32)).astype(BF16)
            imp = jnp.dot(ovt, p_hi, preferred_element_type=F32) + jnp.dot(ovt, p_lo, preferred_element_type=F32)
            st = jnp.where(forced, FORCE_SCORE, jnp.where(js > cur, -1.0, imp))
            st = jnp.where(js < n_slc, st, -2.0)
            sel = jnp.zeros((LANES, tq), F32)
            for _ in range(NSA_SLC_TOPK):
                mx = jnp.max(st, axis=0, keepdims=True)
                first = jnp.min(jnp.where(st == mx, jf, float(LANES)), axis=0, keepdims=True)
                hit = jf == first
                sel = jnp.where(hit, 1.0, sel)
                st = jnp.where(hit, -3e38, st)
            sel_ref[g] = sel.astype(sel_ref.dtype)

    consecutive = consec_ref[0] == 1

    @pl.when(jnp.logical_and(consecutive, qi == 0))
    def _():
        for chunk in range(2 * nc // LANES):
            rel = chunk * LANES - nc + lax.broadcasted_iota(I32, (LANES, 1), 0)
            dist = lane - (rel * NSA_CMP_STRIDE + (NSA_CMP_LEN - 1))
            bucket = jnp.where(dist >= 0, _t5_bucket(dist), T5_MASK_BUCKET)
            for h in range(NSA_HEADS):
                strip_ref[h, chunk * LANES:(chunk + 1) * LANES, :] = _t5_lookup(tab_ref[h:h + 1, :], bucket)

    @pl.when(consecutive)
    def _():
        start = pl.multiple_of(nc - per_block * qi, per_block)
        select(lambda h: strip_ref[h, pl.ds(start, nc), :])

    @pl.when(jnp.logical_not(consecutive))
    def _():
        cmp_end = lax.broadcasted_iota(I32, (nc, 1), 0) * NSA_CMP_STRIDE + (NSA_CMP_LEN - 1)
        pos_c = jnp.concatenate([posc_ref[...]] * (tq // LANES), axis=1)
        bucket = jnp.where(cmp_end <= tok, _t5_bucket(posq_ref[...] - pos_c), T5_MASK_BUCKET)
        select(lambda h: _t5_lookup(tab_ref[h:h + 1, :], bucket))


def _nsa_select(proj_b, q_cb, proj_f, gate_cb, kc, vc, pos_row, pos_cmp_rep, consec, tab_t, ov_t, *, n_slc, tq):
    seq = proj_b.shape[0]
    nc = kc.shape[1]
    g = NSA_KV_GROUPS
    d = NSA_HEAD_DIM
    grid_spec = pltpu.PrefetchScalarGridSpec(
        num_scalar_prefetch=1,
        grid=(seq // tq,),
        in_specs=[pl.BlockSpec((tq, NSA_WIDTH), lambda i, *_: (i, q_cb)),
                  pl.BlockSpec((g, nc, d), lambda i, *_: (0, 0, 0)),
                  pl.BlockSpec((g, nc, d), lambda i, *_: (0, 0, 0)),
                  pl.BlockSpec((1, tq), lambda i, *_: (0, i)),
                  pl.BlockSpec((nc, LANES), lambda i, *_: (0, 0)),
                  pl.BlockSpec((SUBLANES, LANES), lambda i, *_: (0, 0)),
                  pl.BlockSpec((tq, LANES), lambda i, *_: (i, gate_cb)),
                  pl.BlockSpec((LANES, nc), lambda i, *_: (0, 0))],
        out_specs=[pl.BlockSpec((tq, NSA_WIDTH), lambda i, *_: (i, 0)),
                   pl.BlockSpec((g, LANES, tq), lambda i, *_: (0, 0, i))],
        scratch_shapes=[pltpu.VMEM((NSA_HEADS, 2 * nc, tq), F32)])
    return pl.pallas_call(
        functools.partial(_nsa_select_kernel, nc=nc, n_slc=n_slc, tq=tq),
        grid_spec=grid_spec,
        out_shape=[jax.ShapeDtypeStruct((seq, NSA_WIDTH), F32),
                   jax.ShapeDtypeStruct((g, LANES, seq), F32)],
        compiler_params=_cparams("arbitrary"),
        name="nsa_select",
    )(consec, proj_b, kc, vc, pos_row, pos_cmp_rep, tab_t * LOG2E, proj_f, ov_t)


def _nsa_slc_kernel(pqmin_ref, pkmax_ref, consec_ref, q_ref, k_ref, v_ref, sel_ref, posq_ref, posk_ref, tab_ref,
                    o_ref, vt_ref, qt_ref, s_ref, m_ref, l_ref, acc_ref, cache_ref, *, tq, tk):
    g = pl.program_id(0)
    qi = pl.program_id(1)
    d = NSA_HEAD_DIM
    scale = d ** -0.5
    seq = k_ref.shape[0]
    consecutive = consec_ref[0] == 1
    n_cached = cache_ref.shape[0]

    @pl.when(qi == 0)
    def _():
        _build_vt(v_ref, vt_ref, 0, d, seq, tk)

    @pl.when(jnp.logical_and(consecutive, qi == 0))
    def _():
        rel = lax.broadcasted_iota(I32, (1, tq), 1) - lax.broadcasted_iota(I32, (tk, 1), 0)
        for v in range(n_cached):
            dist = rel + v * tq
            bucket = jnp.where(dist >= 0, _t5_bucket(dist), T5_MASK_BUCKET)
            for r in range(NSA_REP):
                cache_ref[v, r] = _t5_lookup(tab_ref[pl.ds(g * NSA_REP + r, 1), :], bucket)

    for r in range(NSA_REP):
        qt_ref[r] = (q_ref[:, r * d:(r + 1) * d].astype(F32) * (scale * LOG2E)).T.astype(BF16)
    m_ref[...] = jnp.full(m_ref.shape, NEG_INF, F32)
    l_ref[...] = jnp.zeros(l_ref.shape, F32)
    acc_ref[...] = jnp.zeros(acc_ref.shape, F32)
    pos_q = posq_ref[...]
    per_tile = tk // NSA_SLC_BLOCK
    n_full = (qi * tq) // tk

    def scores(j, slot):
        start = pl.multiple_of(j * tk, tk)
        k = k_ref[pl.ds(start, tk), :]
        for r in range(NSA_REP):
            s_ref[slot, r] = jnp.dot(k, qt_ref[r], preferred_element_type=F32)

    def softmax_pv(j, slot, masked, mode):
        start = pl.multiple_of(j * tk, tk)
        picked = jnp.concatenate(
            [jnp.broadcast_to(sel_ref[pl.ds(j * per_tile + b, 1), :], (NSA_SLC_BLOCK, tq)) for b in range(per_tile)],
            axis=0)
        if masked and mode == "gather":
            key = start + lax.broadcasted_iota(I32, (tk, tq), 0)
            qry = qi * tq + lax.broadcasted_iota(I32, (tk, tq), 1)
            picked = jnp.where(key <= qry, picked, 0.0)
        mask = picked > 0.5
        if mode == "gather":
            pos_k = jnp.concatenate([posk_ref[pl.ds(start, tk), :]] * (tq // LANES), axis=1)
            bucket = jnp.where(mask, _t5_bucket(pos_q - pos_k), T5_MASK_BUCKET)
        for r in range(NSA_REP):
            h = g * NSA_REP + r
            t = s_ref[slot, r]
            if mode == "gather":
                t = t + _t5_lookup(tab_ref[pl.ds(h, 1), :], bucket)
            elif mode == "cached":
                t = jnp.where(mask, t + cache_ref[(qi * tq - start) // tq, r], NEG_INF)
            else:
                t = jnp.where(mask, t, NEG_INF)
            m_prev = m_ref[r]
            mx = jnp.max(t, axis=0, keepdims=True)
            if mode == "far":
                b = tab_ref[pl.ds(h, 1), T5_BUCKETS - 1:T5_BUCKETS]
                m_new = jnp.maximum(m_prev, mx + b)
                shift = m_new - b
            else:
                m_new = jnp.maximum(m_prev, mx)
                shift = m_new
            alpha = jnp.exp2(m_prev - m_new)
            p = jnp.exp2(t - shift)
            l_ref[r] = alpha * l_ref[r] + jnp.sum(p, axis=0, keepdims=True)
            acc_ref[r] = alpha * acc_ref[r] + jnp.dot(vt_ref[0, :, pl.ds(start, tk)], p.astype(BF16),
                                                       preferred_element_type=F32)
            m_ref[r] = m_new

    def is_far(j):
        return pqmin_ref[qi] - pkmax_ref[j] >= T5_MAX_DIST

    def stages(j, count, mode):
        for k in range(count):
            scores(j + k + 1, (k + 1) % 2)
            softmax_pv(j + k, k % 2, False, mode)

    def near(fn):
        @pl.when(consecutive)
        def _():
            fn("cached")

        @pl.when(jnp.logical_not(consecutive))
        def _():
            fn("gather")

    def far_or_near(far, fn):
        @pl.when(far)
        def _():
            fn("far")

        @pl.when(jnp.logical_not(far))
        def _():
            near(fn)

    scores(0, 0)

    def pair(jj, carry):
        j = 2 * jj
        far_or_near(jnp.logical_and(is_far(j), is_far(j + 1)), lambda mode: stages(j, 2, mode))
        return carry

    pairs = n_full // 2
    lax.fori_loop(0, pairs, pair, 0)
    last = 2 * pairs

    @pl.when(n_full % 2 == 1)
    def _():
        far_or_near(is_far(last), lambda mode: stages(last, 1, mode))
        near(lambda mode: softmax_pv(last + 1, 1, True, mode))

    @pl.when(n_full % 2 == 0)
    def _():
        near(lambda mode: softmax_pv(last, 0, True, mode))

    for r in range(NSA_REP):
        o_ref[:, r * d:(r + 1) * d] = (acc_ref[r] / l_ref[r]).T


def _nsa_slc(proj_b, q_off, k_off, v_off, sel, pos_row, pos_rep, tab_t, pq_min, pk_max, consec, *, tq, tk):
    seq = proj_b.shape[0]
    tab_t = tab_t * LOG2E
    d = NSA_HEAD_DIM
    gw = NSA_REP * d
    once = pl.Buffered(1)
    n_cached = (T5_MAX_DIST + tk - 2) // tq + 1 + tk // tq
    grid_spec = pltpu.PrefetchScalarGridSpec(
        num_scalar_prefetch=3,
        grid=(NSA_KV_GROUPS, seq // tq),
        in_specs=[pl.BlockSpec((tq, gw), lambda g, i, *_: (i, q_off // gw + g)),
                  pl.BlockSpec((seq, d), lambda g, i, *_: (0, k_off // d + g), pipeline_mode=once),
                  pl.BlockSpec((seq, d), lambda g, i, *_: (0, v_off // d + g), pipeline_mode=once),
                  pl.BlockSpec((None, LANES, tq), lambda g, i, *_: (g, 0, i)),
                  pl.BlockSpec((1, tq), lambda g, i, *_: (0, i)),
                  pl.BlockSpec((seq, LANES), lambda g, i, *_: (0, 0), pipeline_mode=once),
                  pl.BlockSpec((SUBLANES, LANES), lambda g, i, *_: (0, 0))],
        out_specs=pl.BlockSpec((tq, gw), lambda g, i, *_: (i, g)),
        scratch_shapes=[pltpu.VMEM((1, d, seq), BF16), pltpu.VMEM((NSA_REP, d, tq), BF16),
                        pltpu.VMEM((2, NSA_REP, tk, tq), F32), pltpu.VMEM((NSA_REP, 1, tq), F32),
                        pltpu.VMEM((NSA_REP, 1, tq), F32), pltpu.VMEM((NSA_REP, d, tq), F32),
                        pltpu.VMEM((n_cached, NSA_REP, tk, tq), F32)])
    return pl.pallas_call(
        functools.partial(_nsa_slc_kernel, tq=tq, tk=tk),
        grid_spec=grid_spec,
        out_shape=jax.ShapeDtypeStruct((seq, NSA_WIDTH), F32),
        compiler_params=_cparams("parallel", "arbitrary"),
        name="nsa_slc",
    )(pq_min, pk_max, consec, proj_b, proj_b, proj_b, sel, pos_row, pos_rep, tab_t)


def _nsa_win_kernel(*refs, tq, nt):
    consec_ref = refs[0]
    q_ref = refs[1]
    k_refs = refs[2:2 + nt]
    v_refs = refs[2 + nt:2 + 2 * nt]
    pk_refs = refs[2 + 2 * nt:2 + 3 * nt]
    posq_ref, tab_ref, gate_ref, gout_ref, oc_ref, os_ref, o_ref, bias_ref = refs[2 + 3 * nt:]
    qi = pl.program_id(0)
    d = NSA_HEAD_DIM
    c = d ** -0.5 * LOG2E
    sub = lax.broadcasted_iota(I32, (tq, 1), 0)
    lane = lax.broadcasted_iota(I32, (1, tq), 1)

    def band_bucket(kidx, tok, dist):
        lower = jnp.maximum(tok - (NSA_WINDOW - 1), 0)
        b = jnp.where(kidx >= lower, _t5_bucket(dist), T5_MASK_BUCKET)
        return jnp.where(kidx <= tok, b, T5_MASK_BUCKET)

    def attend(bias):
        gates = jax.nn.sigmoid(gate_ref[...])
        for g in range(NSA_KV_GROUPS):
            ks = [kr[:, g * d:(g + 1) * d] for kr in k_refs]
            vts = [vr[:, g * d:(g + 1) * d].astype(F32).T.astype(BF16) for vr in v_refs]
            for r in range(NSA_REP):
                h = g * NSA_REP + r
                hs = slice(h * d, (h + 1) * d)
                qt = (q_ref[:, hs].astype(F32) * c).T.astype(BF16)
                ts = [jnp.dot(ks[jj], qt, preferred_element_type=F32) + bias(h, jj) for jj in range(nt)]
                m = functools.reduce(jnp.maximum, [jnp.max(t, axis=0, keepdims=True) for t in ts])
                ps = [jnp.exp2(t - m) for t in ts]
                l = functools.reduce(jnp.add, [jnp.sum(p, axis=0, keepdims=True) for p in ps])
                o_t = functools.reduce(jnp.add, [jnp.dot(vt, p.astype(BF16), preferred_element_type=F32)
                                                 for vt, p in zip(vts, ps)])
                o_w = (o_t / l).T
                o = (oc_ref[:, hs] + gates[:, 3 * h + 1:3 * h + 2] * os_ref[:, hs]
                     + gates[:, 3 * h + 2:3 * h + 3] * o_w)
                o_ref[:, hs] = (o * _silu(gout_ref[:, hs])).astype(o_ref.dtype)

    consecutive = consec_ref[0] == 1

    @pl.when(jnp.logical_and(consecutive, qi == 0))
    def _():
        tok0 = (nt - 1) * tq + lane
        for jj in range(nt):
            kidx0 = jj * tq + sub
            bucket = band_bucket(kidx0, tok0, tok0 - kidx0)
            for h in range(NSA_HEADS):
                bias_ref[h, jj] = _t5_lookup(tab_ref[h:h + 1, :], bucket)

    cached = jnp.logical_and(consecutive, qi >= nt - 1)

    @pl.when(cached)
    def _():
        attend(lambda h, jj: bias_ref[h, jj])

    @pl.when(jnp.logical_not(cached))
    def _():
        tok = qi * tq + lane
        pos_q = posq_ref[...]
        buckets = []
        for jj in range(nt):
            kidx = (qi - (nt - 1) + jj) * tq + sub
            pos_k = jnp.concatenate([pk_refs[jj][...]] * (tq // LANES), axis=1)
            buckets.append(band_bucket(kidx, tok, pos_q - pos_k))
        attend(lambda h, jj: _t5_lookup(tab_ref[h:h + 1, :], buckets[jj]))


def _nsa_win(proj_b, q_cb, k_cb, v_cb, proj_f, gate_cb, gout_cb, oc, o_s, pos_row, pos_rep, consec, tab_t, *, tq):
    seq = proj_b.shape[0]
    nt = NSA_WINDOW // tq + 1

    def band_rows(jj, cb):
        return pl.BlockSpec((tq, NSA_KV), lambda i, *_: (jnp.maximum(i - (nt - 1) + jj, 0), cb))

    def band_pos(jj):
        return pl.BlockSpec((tq, LANES), lambda i, *_: (jnp.maximum(i - (nt - 1) + jj, 0), 0))

    in_specs = [pl.BlockSpec((tq, NSA_WIDTH), lambda i, *_: (i, q_cb))]
    in_specs += [band_rows(jj, k_cb) for jj in range(nt)]
    in_specs += [band_rows(jj, v_cb) for jj in range(nt)]
    in_specs += [band_pos(jj) for jj in range(nt)]
    in_specs += [pl.BlockSpec((1, tq), lambda i, *_: (0, i)),
                 pl.BlockSpec((SUBLANES, LANES), lambda i, *_: (0, 0)),
                 pl.BlockSpec((tq, LANES), lambda i, *_: (i, gate_cb)),
                 pl.BlockSpec((tq, NSA_WIDTH), lambda i, *_: (i, gout_cb)),
                 pl.BlockSpec((tq, NSA_WIDTH), lambda i, *_: (i, 0)),
                 pl.BlockSpec((tq, NSA_WIDTH), lambda i, *_: (i, 0))]
    args = [proj_b] * (1 + 2 * nt) + [pos_rep] * nt + [pos_row, tab_t * LOG2E, proj_f, proj_f, oc, o_s]
    grid_spec = pltpu.PrefetchScalarGridSpec(
        num_scalar_prefetch=1,
        grid=(seq // tq,),
        in_specs=in_specs,
        out_specs=pl.BlockSpec((tq, NSA_WIDTH), lambda i, *_: (i, 0)),
        scratch_shapes=[pltpu.VMEM((NSA_HEADS, nt, tq, tq), F32)])
    return pl.pallas_call(
        functools.partial(_nsa_win_kernel, tq=tq, nt=nt),
        grid_spec=grid_spec,
        out_shape=jax.ShapeDtypeStruct((seq, NSA_WIDTH), BF16),
        compiler_params=_cparams("arbitrary"),
        name="nsa_win",
    )(consec, *args)


def _in_proj(h, norm_g, w_b, w_f, tiles, name):
    proj_b = _norm_matmul(h, norm_g, w_b, tm=tiles["tm"], tn=512, name=name + "_b", out_dtype=BF16)
    proj_f = _norm_matmul(h, norm_g, w_f, tm=tiles["tm"], tn=768, name=name + "_f")
    return proj_b, proj_f


def _even_layer(h, mem_kv, norm_g, w_in, s5, w_glu, b_f, tiles):
    pb, pf = _in_proj(h, norm_g, *w_in, tiles, "in_proj_even")
    ob, of = EVEN_B_OFF, EVEN_F_OFF
    b_cat, c_cat, tab, d_skip = s5
    z = _s5_scan(pf, of["u"] // LANES, b_cat, c_cat, d_skip, tab, tc=tiles["s5_tc"])
    y_s5 = _s5_glu(z, w_glu, pf, of["g_s5"] // S5_WIDTH, tm=tiles["tm"], tn=512)
    decay = _decay(pf, of["f"] // LANES, b_f, t=tiles["decay_t"])
    d = FOX_HEAD_DIM
    y_fox = _flash(pb, ob["q"], pb, ob["k"], pb, ob["v"], pf, of["g_fox"], heads=FOX_HEADS, dk=d, dv=d,
                   scale=d ** -0.5, tq=tiles["attn_tq"], tk=tiles["attn_tk"], hp=tiles["attn_hp"], decay=decay,
                   name="fox_attn")
    y_mem = _mem_attn(pb, ob["q_mem"] // MEM_HEAD_DIM, pf, of["g_mem"] // MEM_HEAD_DIM, mem_kv, t=tiles["mem_t"])
    return y_s5, y_fox, y_mem


def _odd_layer(h, mem_kv, norm_g, w_in, mla, nsa, pos, tiles):
    pb, pf = _in_proj(h, norm_g, *w_in, tiles, "in_proj_odd")
    ob, of = ODD_B_OFF, ODD_F_OFF
    g_cq, g_ckv, w_uq, w_ukv, freq = mla
    pos_col_f, pos_row, pos_cmp_rep, pos_rep, pq_min, pk_max, consec = pos
    q_r, k_r, v_r = _mla_up(pf, of["c_q"] // MLA_Q_RANK, of["c_kv"] // MLA_KV_RANK, of["k_rope"] // LANES,
                            g_cq, g_ckv, w_uq, w_ukv, pos_col_f, freq, t=tiles["prep_t"])
    y_mla = _flash(q_r, 0, k_r, 0, v_r, 0, pf, of["g_mla"], heads=MLA_HEADS, dk=2 * LANES, dv=MLA_V,
                   scale=(MLA_NOPE + MLA_ROPE) ** -0.5, tq=tiles["attn_tq"], tk=tiles["attn_tk"], hp=tiles["attn_hp"],
                   name="mla_attn")
    pe, w1, w2, tab_t, ov, n_slc = nsa
    kvc = _nsa_compress(pb, ob["k_cmp"] // NSA_HEAD_DIM, pe, w1, w2)
    oc, sel = _nsa_select(pb, ob["q_nsa"] // NSA_WIDTH, pf, of["gates"] // LANES, kvc[0], kvc[1], pos_row,
                          pos_cmp_rep, consec, tab_t, ov, n_slc=n_slc, tq=tiles["nsa_tq"])
    o_s = _nsa_slc(pb, ob["q_nsa"], ob["k_slc"], ob["v_slc"], sel, pos_row, pos_rep, tab_t, pq_min, pk_max, consec,
                   tq=tiles["slc_tq"], tk=tiles["slc_tk"])
    y_nsa = _nsa_win(pb, ob["q_nsa"] // NSA_WIDTH, ob["k_win"] // NSA_KV, ob["v_win"] // NSA_KV,
                     pf, of["gates"] // LANES, of["g_nsa"] // NSA_WIDTH, oc, o_s, pos_row, pos_rep, consec, tab_t,
                     tq=tiles["win_tq"])
    y_mem = _mem_attn(pb, ob["q_mem"] // MEM_HEAD_DIM, pf, of["g_mem"] // MEM_HEAD_DIM, mem_kv, t=tiles["mem_t"])
    return y_mla, y_nsa, y_mem


def _tiles(seq):
    return {"tm": min(seq, 1024), "s5_tc": min(seq, 1024), "decay_t": min(seq, 512), "attn_tq": min(seq, 256),
            "attn_tk": min(seq, 512), "attn_hp": 2, "slc_tq": 256, "slc_tk": 512, "nsa_tq": 256, "win_tq": 128,
            "mem_t": min(seq, 1024), "norm_t": min(seq, 512), "prep_t": min(seq, 256)}


def _context(positions, t5_table, seq, tiles):
    pos = positions[0]
    pos_col = pos.reshape(seq, 1)
    pos_row = pos.reshape(1, seq)
    pos_rep = jnp.broadcast_to(pos_col, (seq, LANES))
    pq_min = jnp.min(pos.reshape(seq // tiles["slc_tq"], tiles["slc_tq"]), axis=1)
    pk_max = jnp.max(pos.reshape(seq // tiles["slc_tk"], tiles["slc_tk"]), axis=1)
    nc = seq // NSA_CMP_STRIDE
    pos_cmp = jnp.pad(pos[NSA_CMP_LEN - 1::NSA_CMP_STRIDE], (0, 1))
    pos_cmp_rep = jnp.broadcast_to(pos_cmp.reshape(nc, 1), (nc, LANES))
    half = MLA_ROPE // 2
    inv_freq = ROPE_THETA ** (-jnp.arange(half, dtype=F32) / half)
    freq = jnp.concatenate([inv_freq, inv_freq, jnp.zeros((LANES - MLA_ROPE,), F32)]).reshape(1, LANES)
    tab_t = jnp.pad(t5_table.astype(F32).T, ((0, SUBLANES - NSA_HEADS), (0, LANES - T5_BUCKETS)))
    tab_t = tab_t.at[:, T5_MASK_BUCKET].set(NEG_INF)
    n_slc = seq // NSA_SLC_BLOCK
    cs = np.arange(nc) * NSA_CMP_STRIDE
    ss = np.arange(LANES) * NSA_SLC_BLOCK
    ov_np = np.clip(np.minimum(cs[:, None] + NSA_CMP_LEN, ss[None, :] + NSA_SLC_BLOCK)
                    - np.maximum(cs[:, None], ss[None, :]), 0, None) / NSA_CMP_LEN
    ov_np[nc - 1, :] = 0.0
    ov_np[:, n_slc:] = 0.0
    consec = jnp.all(pos[1:] - pos[:-1] == 1).astype(I32).reshape(1)
    return {"pos": (pos_col.astype(F32), pos_row, pos_cmp_rep, pos_rep, pq_min, pk_max, consec), "freq": freq,
            "tab_t": tab_t, "ov": jnp.asarray(ov_np.T, BF16), "n_slc": n_slc}


def _odd_params(i, mla_g_cq, mla_g_ckv, mla_w_uq, mla_w_ukv, nsa_cmp_pe, nsa_cmp_w1, nsa_cmp_w2, ctx):
    dq = MLA_NOPE + MLA_ROPE
    w_uq = mla_w_uq[i].reshape(MLA_Q_RANK, MLA_HEADS, dq)
    w_uq = jnp.pad(w_uq, ((0, 0), (0, 0), (0, 2 * LANES - dq))).reshape(MLA_Q_RANK, -1).astype(BF16)
    mla = (mla_g_cq[i], mla_g_ckv[i], w_uq, mla_w_ukv[i].astype(BF16), ctx["freq"])
    nsa = (nsa_cmp_pe[i].astype(F32), nsa_cmp_w1[i], nsa_cmp_w2[i], ctx["tab_t"],
           ctx["ov"], ctx["n_slc"])
    return mla, nsa


def kernel(x, mem, positions, norm_g, mem_norm_g, final_norm_g, t5_table, w_out, mem_w_kv, even_w_in, s5_lam_re,
           s5_lam_im, s5_log_dt, s5_b_re, s5_b_im, s5_c_re, s5_c_im, s5_d, s5_w_glu, fox_b_f, odd_w_in, mla_g_cq,
           mla_g_ckv, mla_w_uq, mla_w_ukv, nsa_cmp_pe, nsa_cmp_w1, nsa_cmp_w2):
    batch, seq, _ = x.shape
    assert batch == 1 and seq % 1024 == 0 and seq // NSA_SLC_BLOCK <= LANES
    depth = norm_g.shape[0]
    tiles = _tiles(seq)
    ctx = _context(positions, t5_table, seq, tiles)
    h = x[0]
    mem_kv_all = _mem_kv_all(mem[0], mem_norm_g, mem_w_kv)
    for layer in range(depth):
        i = layer // 2
        mem_kv = (mem_kv_all, layer)
        if layer % 2 == 0:
            w_in = (_reorder_w_in(even_w_in[i], EVEN_SPLITS, EVEN_B_ORDER),
                    _reorder_w_in(even_w_in[i], EVEN_SPLITS, EVEN_F_ORDER))
            b_cat, c_cat, tab = _s5_prepare(s5_lam_re[i], s5_lam_im[i], s5_log_dt[i], s5_b_re[i], s5_b_im[i],
                                            s5_c_re[i], s5_c_im[i])
            ys = _even_layer(h, mem_kv, norm_g[layer], w_in, (b_cat, c_cat, tab, s5_d[i]),
                             s5_w_glu[i], fox_b_f[i], tiles)
        else:
            w_in = (_reorder_w_in(odd_w_in[i], ODD_SPLITS, ODD_B_ORDER),
                    _reorder_w_in(odd_w_in[i], ODD_SPLITS, ODD_F_ORDER))
            mla, nsa = _odd_params(i, mla_g_cq, mla_g_ckv, mla_w_uq, mla_w_ukv, nsa_cmp_pe, nsa_cmp_w1, nsa_cmp_w2,
                                   ctx)
            ys = _odd_layer(h, mem_kv, norm_g[layer], w_in, mla, nsa, ctx["pos"], tiles)
        h = _out_proj(h, ys, w_out, layer, tm=tiles["tm"], tn=512)
    return _final_norm(h, final_norm_g, tm=tiles["norm_t"])[None]
```

```python
import functools
import math

import numpy as np
import jax
import jax.numpy as jnp
from jax import lax
from jax.experimental import pallas as pl
from jax.experimental.pallas import tpu as pltpu

F32 = jnp.float32
BF16 = jnp.bfloat16
I32 = jnp.int32

RMS_EPS = 1e-6
NEG_INF = -1e30
LOG2E = math.log2(math.e)

S5_WIDTH = 1024
S5_GROUP = 16
S5_GROUPS = S5_WIDTH // S5_GROUP
S5_STATE = 64
FOX_HEADS = 8
FOX_HEAD_DIM = 128
FOX_WIDTH = FOX_HEADS * FOX_HEAD_DIM
MEM_HEADS = 4
MEM_HEAD_DIM = 128
MEM_WIDTH = MEM_HEADS * MEM_HEAD_DIM
MLA_HEADS = 8
MLA_Q_RANK = 512
MLA_KV_RANK = 512
MLA_NOPE = 128
MLA_ROPE = 64
MLA_V = 128
MLA_WIDTH = MLA_HEADS * MLA_V
ROPE_THETA = 10000.0
NSA_HEADS = 8
NSA_KV_GROUPS = 2
NSA_REP = NSA_HEADS // NSA_KV_GROUPS
NSA_HEAD_DIM = 128
NSA_WIDTH = NSA_HEADS * NSA_HEAD_DIM
NSA_KV = NSA_KV_GROUPS * NSA_HEAD_DIM
NSA_CMP_LEN = 32
NSA_CMP_STRIDE = 16
NSA_CMP_HIDDEN = 256
NSA_SLC_BLOCK = 64
NSA_SLC_TOPK = 16
NSA_WINDOW = 512
FORCE_SCORE = 1e6
T5_BUCKETS = 32
T5_MAX_DIST = 1024

EVEN_SPLITS = (S5_WIDTH, S5_WIDTH, FOX_WIDTH, FOX_WIDTH, FOX_WIDTH, FOX_HEADS, FOX_WIDTH, MEM_WIDTH, MEM_WIDTH)
ODD_SPLITS = (MLA_Q_RANK, MLA_KV_RANK, MLA_ROPE, MLA_WIDTH, NSA_WIDTH, NSA_KV, NSA_KV, NSA_KV, NSA_KV, NSA_KV,
              NSA_KV, 3 * NSA_HEADS, NSA_WIDTH, MEM_WIDTH, MEM_WIDTH)

LANES = 128
SUBLANES = 8
VMEM_LIMIT_BYTES = 56 * 1024 * 1024

EVEN_B_ORDER = (("q", 2, 1024), ("k", 3, 1024), ("v", 4, 1024), ("q_mem", 7, 512))
EVEN_F_ORDER = (("u", 0, 1024), ("g_s5", 1, 1024), ("g_fox", 6, 1024), ("g_mem", 8, 512), ("f", 5, 128),
                ("pad", None, 128))
ODD_B_ORDER = (("q_nsa", 4, 1024), ("q_mem", 13, 512), ("k_cmp", 5, 256), ("v_cmp", 6, 256), ("k_slc", 7, 256),
               ("v_slc", 8, 256), ("k_win", 9, 256), ("v_win", 10, 256))
ODD_F_ORDER = (("g_nsa", 12, 1024), ("g_mla", 3, 1024), ("c_q", 0, 512), ("c_kv", 1, 512), ("g_mem", 14, 512),
               ("k_rope", 2, 128), ("gates", 11, 128))


def _layout(order):
    off, out = 0, {}
    for name, _, width in order:
        assert off % width == 0
        out[name] = off
        off += width
    return out, off


EVEN_B_OFF, EVEN_B_N = _layout(EVEN_B_ORDER)
EVEN_F_OFF, EVEN_F_N = _layout(EVEN_F_ORDER)
ODD_B_OFF, ODD_B_N = _layout(ODD_B_ORDER)
ODD_F_OFF, ODD_F_N = _layout(ODD_F_ORDER)


def _reorder_w_in(w, splits, order):
    starts = np.concatenate([[0], np.cumsum(splits)])
    cols = []
    for _, idx, width in order:
        if idx is None:
            cols.append(jnp.zeros((w.shape[0], width), w.dtype))
            continue
        seg = w[:, int(starts[idx]):int(starts[idx + 1])]
        pad = width - seg.shape[1]
        if pad:
            seg = jnp.pad(seg, ((0, 0), (0, pad)))
        cols.append(seg)
    return jnp.concatenate(cols, axis=1).astype(BF16)


def _cparams(*sem):
    return pltpu.CompilerParams(dimension_semantics=sem, vmem_limit_bytes=VMEM_LIMIT_BYTES)


def _silu(g):
    return g * jax.nn.sigmoid(g)


def _norm_matmul_kernel(x_ref, g_ref, w_ref, o_ref, xn_ref):
    @pl.when(pl.program_id(1) == 0)
    def _():
        x = x_ref[...]
        ms = jnp.mean(x * x, axis=-1, keepdims=True)
        xn_ref[...] = (x * lax.rsqrt(ms + RMS_EPS) * g_ref[...]).astype(BF16)

    o_ref[...] = jnp.dot(xn_ref[...], w_ref[...].astype(BF16), preferred_element_type=F32).astype(o_ref.dtype)


def _norm_matmul(x, g, w, *, x_cb=0, tm, tn, name, out_dtype=F32):
    m = x.shape[0]
    k, n = w.shape
    return pl.pallas_call(
        _norm_matmul_kernel,
        grid=(m // tm, n // tn),
        in_specs=[pl.BlockSpec((tm, k), lambda i, j: (i, x_cb)),
                  pl.BlockSpec((1, k), lambda i, j: (0, 0)),
                  pl.BlockSpec((k, tn), lambda i, j: (0, j))],
        out_specs=pl.BlockSpec((tm, tn), lambda i, j: (i, j)),
        out_shape=jax.ShapeDtypeStruct((m, n), out_dtype),
        scratch_shapes=[pltpu.VMEM((tm, k), BF16)],
        compiler_params=_cparams("parallel", "arbitrary"),
        name=name,
    )(x, g.reshape(1, k), w)


def _out_proj_kernel(h_ref, *refs):
    o_ref = refs[-1]
    n = (len(refs) - 1) // 2
    acc = h_ref[...]
    for y_ref, w_ref in zip(refs[:n], refs[n:2 * n]):
        acc = acc + jnp.dot(y_ref[...], w_ref[...].astype(BF16), preferred_element_type=F32)
    o_ref[...] = acc


def _out_proj(h, ys, w_all, layer, *, tm, tn):
    m, n = h.shape
    in_specs = [pl.BlockSpec((tm, tn), lambda i, j: (i, j))]
    in_specs += [pl.BlockSpec((tm, y.shape[1]), lambda i, j: (i, 0)) for y in ys]
    row = 0
    for y in ys:
        width = y.shape[1]
        assert row % width == 0
        in_specs.append(pl.BlockSpec((None, width, tn), lambda i, j, rb=row // width: (layer, rb, j)))
        row += width
    return pl.pallas_call(
        _out_proj_kernel,
        grid=(m // tm, n // tn),
        in_specs=in_specs,
        out_specs=pl.BlockSpec((tm, tn), lambda i, j: (i, j)),
        out_shape=jax.ShapeDtypeStruct((m, n), F32),
        compiler_params=_cparams("parallel", "arbitrary"),
        name="out_proj",
    )(h, *ys, *([w_all] * len(ys)))


def _final_norm_kernel(x_ref, g_ref, o_ref):
    x = x_ref[...]
    ms = jnp.mean(x * x, axis=-1, keepdims=True)
    o_ref[...] = x * lax.rsqrt(ms + RMS_EPS) * g_ref[...]


def _final_norm(h, g, *, tm):
    m, n = h.shape
    return pl.pallas_call(
        _final_norm_kernel,
        grid=(m // tm,),
        in_specs=[pl.BlockSpec((tm, n), lambda i: (i, 0)), pl.BlockSpec((1, n), lambda i: (0, 0))],
        out_specs=pl.BlockSpec((tm, n), lambda i: (i, 0)),
        out_shape=jax.ShapeDtypeStruct((m, n), F32),
        compiler_params=_cparams("parallel"),
        name="final_norm",
    )(h, g.reshape(1, n))


ONES_ROWS = 16


def _build_vt(v_ref, vt_ref, hh, dv, seq, chunk):
    def body(c, carry):
        st = pl.multiple_of(c * chunk, chunk)
        vt_ref[hh, 0:dv, pl.ds(st, chunk)] = v_ref[pl.ds(st, chunk), hh * dv:(hh + 1) * dv].astype(F32).T.astype(BF16)
        return carry

    lax.fori_loop(0, seq // chunk, body, 0)
    extra = vt_ref.shape[1] - dv
    if extra:
        vt_ref[hh, dv:dv + extra, :] = jnp.ones((extra, seq), BF16)


def _flash_kernel(*refs, scale, tq, tk, hp, dk, dv, has_decay):
    if has_decay:
        q_ref, k_ref, v_ref, g_ref, cq_ref, ck_ref, o_ref, vt_ref, qt_ref, s_ref, m_ref, acc_ref = refs
    else:
        q_ref, k_ref, v_ref, g_ref, o_ref, vt_ref, qt_ref, s_ref, m_ref, acc_ref = refs
    hb = pl.program_id(0)
    qi = pl.program_id(1)
    seq = k_ref.shape[0]

    @pl.when(qi == 0)
    def _():
        for hh in range(hp):
            _build_vt(v_ref, vt_ref, hh, dv, seq, tk)

    for hh in range(hp):
        qt_ref[hh] = (q_ref[:, hh * dk:(hh + 1) * dk].astype(F32) * (scale * LOG2E)).T.astype(BF16)
    m_ref[...] = jnp.full(m_ref.shape, NEG_INF, F32)
    acc_ref[...] = jnp.zeros(acc_ref.shape, F32)
    n_full = (qi * tq) // tk
    if has_decay:
        cq2 = [cq_ref[pl.ds(hb * hp + hh, 1), :] for hh in range(hp)]

    def scores(j, slot):
        start = pl.multiple_of(j * tk, tk)
        for hh in range(hp):
            s_ref[slot, hh] = jnp.dot(k_ref[pl.ds(start, tk), hh * dk:(hh + 1) * dk], qt_ref[hh],
                                      preferred_element_type=F32)

    def softmax_pv(j, slot, masked):
        start = pl.multiple_of(j * tk, tk)
        for hh in range(hp):
            t = s_ref[slot, hh]
            if has_decay:
                t = t - jnp.concatenate([ck_ref[hh, pl.ds(start, tk), :]] * (tq // LANES), axis=1)
            if masked:
                key = start + lax.broadcasted_iota(I32, (tk, tq), 0)
                qry = qi * tq + lax.broadcasted_iota(I32, (tk, tq), 1)
                t = jnp.where(key <= qry, t, NEG_INF)
            m_prev = m_ref[hh]
            mx = jnp.max(t, axis=0, keepdims=True)
            if has_decay:
                m_new = jnp.maximum(m_prev, mx + cq2[hh])
                shift = m_new - cq2[hh]
            else:
                m_new = jnp.maximum(m_prev, mx)
                shift = m_new
            alpha = jnp.exp2(m_prev - m_new)
            p = jnp.exp2(t - shift).astype(BF16)
            acc_ref[hh] = alpha * acc_ref[hh] + jnp.dot(vt_ref[hh, :, pl.ds(start, tk)], p,
                                                         preferred_element_type=F32)
            m_ref[hh] = m_new

    scores(0, 0)

    def pair(j):
        scores(j + 1, 1)
        softmax_pv(j, 0, False)
        scores(j + 2, 0)
        softmax_pv(j + 1, 1, False)

    def quad(qq, carry):
        pair(4 * qq)
        pair(4 * qq + 2)
        return carry

    quads = n_full // 4
    lax.fori_loop(0, quads, quad, 0)
    rem = n_full - 4 * quads

    @pl.when(rem >= 2)
    def _():
        pair(4 * quads)

    last = 4 * quads + 2 * (rem // 2)

    @pl.when(n_full % 2 == 1)
    def _():
        scores(last + 1, 1)
        softmax_pv(last, 0, False)
        softmax_pv(last + 1, 1, True)

    @pl.when(n_full % 2 == 0)
    def _():
        softmax_pv(last, 0, True)

    for hh in range(hp):
        a = acc_ref[hh]
        o = (a[0:dv, :] / a[dv:dv + 1, :]).T
        o_ref[:, hh * dv:(hh + 1) * dv] = (o * _silu(g_ref[:, hh * dv:(hh + 1) * dv])).astype(o_ref.dtype)


def _flash(q_arr, q_off, k_arr, k_off, v_arr, v_off, g_arr, g_off, *, heads, dk, dv, scale, tq, tk, hp,
           decay=None, name):
    seq = q_arr.shape[0]
    once = pl.Buffered(1)
    in_specs = [pl.BlockSpec((tq, hp * dk), lambda h, i: (i, q_off // (hp * dk) + h)),
                pl.BlockSpec((seq, hp * dk), lambda h, i: (0, k_off // (hp * dk) + h), pipeline_mode=once),
                pl.BlockSpec((seq, hp * dv), lambda h, i: (0, v_off // (hp * dv) + h), pipeline_mode=once),
                pl.BlockSpec((tq, hp * dv), lambda h, i: (i, g_off // (hp * dv) + h))]
    args = [q_arr, k_arr, v_arr, g_arr]
    if decay is not None:
        cum_t, cum_rep = decay
        in_specs += [pl.BlockSpec((SUBLANES, tq), lambda h, i: (0, i)),
                     pl.BlockSpec((hp, seq, LANES), lambda h, i: (h, 0, 0), pipeline_mode=once)]
        args += [cum_t, cum_rep]
    return pl.pallas_call(
        functools.partial(_flash_kernel, scale=scale, tq=tq, tk=tk, hp=hp, dk=dk, dv=dv, has_decay=decay is not None),
        grid=(heads // hp, seq // tq),
        in_specs=in_specs,
        out_specs=pl.BlockSpec((tq, hp * dv), lambda h, i: (i, h)),
        out_shape=jax.ShapeDtypeStruct((seq, heads * dv), BF16),
        scratch_shapes=[pltpu.VMEM((hp, dv + ONES_ROWS, seq), BF16), pltpu.VMEM((hp, dk, tq), BF16),
                        pltpu.VMEM((2, hp, tk, tq), F32), pltpu.VMEM((hp, 1, tq), F32),
                        pltpu.VMEM((hp, dv + ONES_ROWS, tq), F32)],
        compiler_params=_cparams("parallel", "arbitrary"),
        name=name,
    )(*args)


def _decay_kernel(f_ref, b_ref, ct_ref, cr_ref, carry_ref, *, t):
    i = pl.program_id(0)

    @pl.when(i == 0)
    def _():
        carry_ref[...] = jnp.zeros(carry_ref.shape, F32)

    x = f_ref[...] + b_ref[...]
    lf = jnp.minimum(x, 0.0) - jnp.log1p(jnp.exp(-jnp.abs(x)))
    row = lax.broadcasted_iota(I32, lf.shape, 0)
    s = 1
    while s < t:
        lf = lf + jnp.where(row >= s, pltpu.roll(lf, s, 0), 0.0)
        s *= 2
    lf = lf + carry_ref[...]
    carry_ref[...] = lf[t - 1:t, :]
    lf2 = lf * LOG2E
    ct_ref[...] = lf2.T[:FOX_HEADS, :]
    for h in range(FOX_HEADS):
        cr_ref[h] = jnp.broadcast_to(lf2[:, h:h + 1], (t, LANES))


def _decay(proj, f_cb, b_f, *, t):
    seq = proj.shape[0]
    b = jnp.pad(b_f.reshape(1, FOX_HEADS), ((0, 0), (0, LANES - FOX_HEADS)))
    return pl.pallas_call(
        functools.partial(_decay_kernel, t=t),
        grid=(seq // t,),
        in_specs=[pl.BlockSpec((t, LANES), lambda i: (i, f_cb)), pl.BlockSpec((1, LANES), lambda i: (0, 0))],
        out_specs=[pl.BlockSpec((FOX_HEADS, t), lambda i: (0, i)),
                   pl.BlockSpec((FOX_HEADS, t, LANES), lambda i: (0, i, 0))],
        out_shape=[jax.ShapeDtypeStruct((FOX_HEADS, seq), F32), jax.ShapeDtypeStruct((FOX_HEADS, seq, LANES), F32)],
        scratch_shapes=[pltpu.VMEM((1, LANES), F32)],
        compiler_params=_cparams("arbitrary"),
        name="fox_decay",
    )(proj, b)


def _mem_attn_kernel(q_ref, kv_ref, g_ref, o_ref):
    d = MEM_HEAD_DIM
    for h in range(MEM_HEADS):
        hs = slice(h * d, (h + 1) * d)
        k = kv_ref[:, hs].astype(BF16)
        v = kv_ref[:, MEM_WIDTH + h * d:MEM_WIDTH + (h + 1) * d].astype(BF16)
        s = lax.dot_general(q_ref[:, hs], k, (((1,), (1,)), ((), ())), preferred_element_type=F32) * (d ** -0.5)
        m = jnp.max(s, axis=1, keepdims=True)
        p = jnp.exp(s - m)
        l = jnp.sum(p, axis=1, keepdims=True)
        o = jnp.dot(p.astype(BF16), v, preferred_element_type=F32) / l
        o_ref[:, hs] = (o * _silu(g_ref[:, hs])).astype(o_ref.dtype)


def _mem_kv_all(mem2d, g, w_all, *, tn):
    depth, k, n = w_all.shape
    m = mem2d.shape[0]
    return pl.pallas_call(
        _norm_matmul_kernel,
        grid=(depth, n // tn),
        in_specs=[pl.BlockSpec((m, k), lambda l, j: (0, 0)),
                  pl.BlockSpec((1, k), lambda l, j: (0, 0)),
                  pl.BlockSpec((None, k, tn), lambda l, j: (l, 0, j))],
        out_specs=pl.BlockSpec((None, m, tn), lambda l, j: (l, 0, j)),
        out_shape=jax.ShapeDtypeStruct((depth, m, n), F32),
        scratch_shapes=[pltpu.VMEM((m, k), BF16)],
        compiler_params=_cparams("arbitrary", "arbitrary"),
        name="mem_kv",
    )(mem2d, g.reshape(1, k), w_all)


def _mem_attn(proj_b, q_cb, proj_f, g_cb, mem_kv, *, t):
    seq = proj_b.shape[0]
    mem_kv, layer = mem_kv
    nm = mem_kv.shape[1]
    w = MEM_WIDTH
    d = MEM_HEAD_DIM
    return pl.pallas_call(
        _mem_attn_kernel,
        grid=(seq // t,),
        in_specs=[pl.BlockSpec((t, w), lambda i: (i, q_cb * d // w)),
                  pl.BlockSpec((None, nm, 2 * w), lambda i: (layer, 0, 0)),
                  pl.BlockSpec((t, w), lambda i: (i, g_cb * d // w))],
        out_specs=pl.BlockSpec((t, w), lambda i: (i, 0)),
        out_shape=jax.ShapeDtypeStruct((seq, w), BF16),
        compiler_params=_cparams("parallel"),
        name="mem_attn",
    )(proj_b, mem_kv, proj_f)


S5_TILE_GROUPS = LANES // S5_GROUP
S5_TILE_STATES = S5_TILE_GROUPS * S5_STATE
S5_TILES = S5_GROUPS // S5_TILE_GROUPS


def _s5_scan_kernel(u_ref, b_ref, c_ref, d_ref, tab_ref, z_ref, bu_ref, carry_ref, *, tc):
    ns = S5_TILE_STATES

    @pl.when(pl.program_id(1) == 0)
    def _():
        carry_ref[...] = jnp.zeros(carry_ref.shape, F32)

    u = u_ref[...]
    bu_ref[...] = jnp.dot(u.astype(BF16), b_ref[...], preferred_element_type=F32)
    steps = [(1, tab_ref[0], tab_ref[1]), (2, tab_ref[2], tab_ref[3]), (4, tab_ref[4], tab_ref[5])]
    pr = tab_ref[6]
    pi = tab_ref[7]

    def body(i, carry):
        cr, ci = carry
        r0 = pl.multiple_of(i * SUBLANES, SUBLANES)
        xr = bu_ref[pl.ds(r0, SUBLANES), 0:ns]
        xi = bu_ref[pl.ds(r0, SUBLANES), ns:2 * ns]
        for s, ar, ai in steps:
            sr = pltpu.roll(xr, s, 0)
            si = pltpu.roll(xi, s, 0)
            xr, xi = xr + ar * sr - ai * si, xi + ar * si + ai * sr
        xr, xi = xr + pr * cr - pi * ci, xi + pr * ci + pi * cr
        bu_ref[pl.ds(r0, SUBLANES), 0:ns] = xr
        bu_ref[pl.ds(r0, SUBLANES), ns:2 * ns] = xi
        return xr[SUBLANES - 1:SUBLANES, :], xi[SUBLANES - 1:SUBLANES, :]

    cr, ci = lax.fori_loop(0, tc // SUBLANES, body, (carry_ref[0:1, 0:ns], carry_ref[0:1, ns:2 * ns]))
    carry_ref[0:1, 0:ns] = cr
    carry_ref[0:1, ns:2 * ns] = ci
    y = jnp.dot(bu_ref[...].astype(BF16), c_ref[...], preferred_element_type=F32) + d_ref[...] * u
    z_ref[...] = jax.nn.gelu(y)


def _s5_prepare(lam_re, lam_im, log_dt, b_re, b_im, c_re, c_im):
    dt = jnp.exp(log_dt.astype(F32))[:, None]
    lr = lam_re.astype(F32)
    li = lam_im.astype(F32)
    mag = jnp.exp(lr * dt)
    ab_re = mag * jnp.cos(li * dt)
    ab_im = mag * jnp.sin(li * dt)
    den = lr * lr + li * li
    nr = ab_re - 1.0
    f_re = (nr * lr + ab_im * li) / den
    f_im = (ab_im * lr - nr * li) / den
    br = b_re.astype(F32)
    bim = b_im.astype(F32)
    bb_re = f_re[..., None] * br - f_im[..., None] * bim
    bb_im = f_re[..., None] * bim + f_im[..., None] * br
    eye = jnp.eye(S5_TILE_GROUPS, dtype=F32)

    def blockdiag_in(bb):
        t = bb.reshape(S5_TILES, S5_TILE_GROUPS, S5_STATE, S5_GROUP)
        m = jnp.einsum("jgpc,gh->jgchp", t, eye)
        return m.reshape(S5_TILES, LANES, S5_TILE_STATES)

    def blockdiag_out(cc):
        t = cc.reshape(S5_TILES, S5_TILE_GROUPS, S5_GROUP, S5_STATE)
        m = jnp.einsum("jgcp,gh->jgphc", t, eye)
        return m.reshape(S5_TILES, S5_TILE_STATES, LANES)

    b_cat = jnp.concatenate([blockdiag_in(bb_re), blockdiag_in(bb_im)], axis=2).astype(BF16)
    c_cat = jnp.concatenate([blockdiag_out(c_re.astype(F32)), -blockdiag_out(c_im.astype(F32))], axis=1).astype(BF16)

    a_r = ab_re.reshape(S5_TILES, 1, S5_TILE_STATES)
    a_i = ab_im.reshape(S5_TILES, 1, S5_TILE_STATES)

    def cmul(xr, xi, yr, yi):
        return xr * yr - xi * yi, xr * yi + xi * yr

    a2 = cmul(a_r, a_i, a_r, a_i)
    a4 = cmul(*a2, *a2)
    row = jnp.arange(SUBLANES)[None, :, None]
    tabs = []
    for s, (pr_, pi_) in ((1, (a_r, a_i)), (2, a2), (4, a4)):
        tabs.append(jnp.where(row >= s, pr_, 0.0))
        tabs.append(jnp.where(row >= s, pi_, 0.0))
    pw = [(a_r, a_i)]
    for _ in range(SUBLANES - 1):
        pw.append(cmul(*pw[-1], a_r, a_i))
    tabs.append(jnp.concatenate([p[0] for p in pw], axis=1))
    tabs.append(jnp.concatenate([p[1] for p in pw], axis=1))
    tab = jnp.stack([jnp.broadcast_to(t, (S5_TILES, SUBLANES, S5_TILE_STATES)) for t in tabs], axis=1)
    return b_cat, c_cat, tab.astype(F32)


def _s5_scan(proj, u_cb, b_cat, c_cat, d_skip, tab, *, tc):
    seq = proj.shape[0]
    ns = S5_TILE_STATES
    d = d_skip.astype(F32).reshape(S5_TILES, 1, LANES)
    return pl.pallas_call(
        functools.partial(_s5_scan_kernel, tc=tc),
        grid=(S5_TILES, seq // tc),
        in_specs=[pl.BlockSpec((tc, LANES), lambda j, c: (c, u_cb + j)),
                  pl.BlockSpec((None, LANES, 2 * ns), lambda j, c: (j, 0, 0)),
                  pl.BlockSpec((None, 2 * ns, LANES), lambda j, c: (j, 0, 0)),
                  pl.BlockSpec((None, 1, LANES), lambda j, c: (j, 0, 0)),
                  pl.BlockSpec((None, 8, SUBLANES, ns), lambda j, c: (j, 0, 0, 0))],
        out_specs=pl.BlockSpec((tc, LANES), lambda j, c: (c, j)),
        out_shape=jax.ShapeDtypeStruct((seq, S5_WIDTH), F32),
        scratch_shapes=[pltpu.VMEM((tc, 2 * ns), F32), pltpu.VMEM((SUBLANES, 2 * ns), F32)],
        compiler_params=_cparams("parallel", "arbitrary"),
        name="s5_scan",
    )(proj, b_cat, c_cat, d, tab)


def _s5_glu_kernel(z_ref, w_ref, g_ref, o_ref, *, tn):
    j = pl.program_id(1)
    z = z_ref[...]
    a = jnp.dot(z.astype(BF16), w_ref[...].astype(BF16), preferred_element_type=F32)
    zc = z_ref[:, pl.ds(pl.multiple_of(j * tn, tn), tn)]
    o_ref[...] = (zc * jax.nn.sigmoid(a) * _silu(g_ref[...])).astype(o_ref.dtype)


def _s5_glu(z, w_glu, proj, g_cb, *, tm, tn):
    seq, n = z.shape
    return pl.pallas_call(
        functools.partial(_s5_glu_kernel, tn=tn),
        grid=(seq // tm, n // tn),
        in_specs=[pl.BlockSpec((tm, n), lambda i, j: (i, 0)),
                  pl.BlockSpec((n, tn), lambda i, j: (0, j)),
                  pl.BlockSpec((tm, tn), lambda i, j: (i, g_cb * (n // tn) + j))],
        out_specs=pl.BlockSpec((tm, tn), lambda i, j: (i, j)),
        out_shape=jax.ShapeDtypeStruct((seq, n), BF16),
        compiler_params=_cparams("parallel", "arbitrary"),
        name="s5_glu",
    )(z, w_glu, proj)


def _rope_tables(pos, freq):
    ang = pos * freq
    lane = lax.broadcasted_iota(I32, ang.shape, 1)
    half = MLA_ROPE // 2
    cos = jnp.cos(ang)
    sin = jnp.sin(ang)
    c = jnp.where(lane < MLA_ROPE, cos, 0.0)
    s1 = jnp.where(lane < half, -sin, 0.0)
    s2 = jnp.where((lane >= half) & (lane < MLA_ROPE), sin, 0.0)
    return c, s1, s2


def _rope_apply(x, c, s1, s2):
    half = MLA_ROPE // 2
    return x * c + pltpu.roll(x, LANES - half, 1) * s1 + pltpu.roll(x, half, 1) * s2


def _mla_up_kernel(cq_ref, ckv_ref, kr_ref, pos_ref, freq_ref, gq_ref, gkv_ref, wq_ref, wkv_ref, q_ref, k_ref, v_ref):
    def normed(x_ref, g_ref):
        x = x_ref[...]
        ms = jnp.mean(x * x, axis=-1, keepdims=True)
        return (x * lax.rsqrt(ms + RMS_EPS) * g_ref[...]).astype(BF16)

    qf = jnp.dot(normed(cq_ref, gq_ref), wq_ref[...], preferred_element_type=F32)
    kvf = jnp.dot(normed(ckv_ref, gkv_ref), wkv_ref[...], preferred_element_type=F32)
    c, s1, s2 = _rope_tables(pos_ref[...], freq_ref[...])
    kr = _rope_apply(kr_ref[...], c, s1, s2).astype(BF16)
    for h in range(MLA_HEADS):
        b = 2 * LANES * h
        q_ref[:, b:b + LANES] = qf[:, b:b + LANES].astype(BF16)
        q_ref[:, b + LANES:b + 2 * LANES] = _rope_apply(qf[:, b + LANES:b + 2 * LANES], c, s1, s2).astype(BF16)
        k_ref[:, b:b + LANES] = kvf[:, b:b + LANES].astype(BF16)
        k_ref[:, b + LANES:b + 2 * LANES] = kr
        v_ref[:, LANES * h:LANES * (h + 1)] = kvf[:, b + LANES:b + 2 * LANES].astype(BF16)


def _mla_up(proj, cq_cb, ckv_cb, kr_cb, g_cq, g_ckv, w_uq, w_ukv, pos_col, freq, *, t):
    seq = proj.shape[0]
    w = 2 * LANES * MLA_HEADS
    rq, rkv = w_uq.shape[0], w_ukv.shape[0]
    return pl.pallas_call(
        _mla_up_kernel,
        grid=(seq // t,),
        in_specs=[pl.BlockSpec((t, rq), lambda i: (i, cq_cb)),
                  pl.BlockSpec((t, rkv), lambda i: (i, ckv_cb)),
                  pl.BlockSpec((t, LANES), lambda i: (i, kr_cb)),
                  pl.BlockSpec((t, 1), lambda i: (i, 0)),
                  pl.BlockSpec((1, LANES), lambda i: (0, 0)),
                  pl.BlockSpec((1, rq), lambda i: (0, 0)),
                  pl.BlockSpec((1, rkv), lambda i: (0, 0)),
                  pl.BlockSpec((rq, w), lambda i: (0, 0)),
                  pl.BlockSpec((rkv, w), lambda i: (0, 0))],
        out_specs=[pl.BlockSpec((t, w), lambda i: (i, 0)),
                   pl.BlockSpec((t, w), lambda i: (i, 0)),
                   pl.BlockSpec((t, MLA_WIDTH), lambda i: (i, 0))],
        out_shape=[jax.ShapeDtypeStruct((seq, w), BF16), jax.ShapeDtypeStruct((seq, w), BF16),
                   jax.ShapeDtypeStruct((seq, MLA_WIDTH), BF16)],
        compiler_params=_cparams("parallel"),
        name="mla_up",
    )(proj, proj, proj, pos_col, freq, g_cq.reshape(1, rq), g_ckv.reshape(1, rkv), w_uq, w_ukv)


def _t5_bucket(dist):
    n = jnp.maximum(dist, 0)
    max_exact = T5_BUCKETS // 2
    log_ratio = jnp.log(jnp.maximum(n, 1).astype(F32) / max_exact) / math.log(T5_MAX_DIST / max_exact)
    large = jnp.minimum(max_exact + (log_ratio * (T5_BUCKETS - max_exact)).astype(I32), T5_BUCKETS - 1)
    return jnp.where(n < max_exact, n, large)


T5_MASK_BUCKET = T5_BUCKETS


def _t5_lookup(table_row, bucket):
    rows, width = bucket.shape
    tab = jnp.broadcast_to(table_row, (rows, LANES))
    parts = [jnp.take_along_axis(tab, bucket[:, c:c + LANES], axis=1, mode="promise_in_bounds")
             for c in range(0, width, LANES)]
    return parts[0] if len(parts) == 1 else jnp.concatenate(parts, axis=1)


def _nsa_cmp_kernel(x_ref, pe_ref, w1_ref, w2_ref, o_ref, xf_ref, *, nc):
    half = NSA_CMP_LEN // 2
    d = NSA_HEAD_DIM
    xf_ref[...] = x_ref[...].astype(F32)
    u = jnp.zeros((nc, NSA_CMP_HIDDEN), F32)
    v = jnp.zeros((nc, NSA_CMP_HIDDEN), F32)
    for r in range(half):
        a = xf_ref[pl.ds(r, nc, stride=NSA_CMP_STRIDE), :]
        u = u + jnp.dot((a + pe_ref[r:r + 1, :]).astype(BF16), w1_ref[r * d:(r + 1) * d, :].astype(BF16),
                        preferred_element_type=F32)
        v = v + jnp.dot((a + pe_ref[half + r:half + r + 1, :]).astype(BF16),
                        w1_ref[(half + r) * d:(half + r + 1) * d, :].astype(BF16), preferred_element_type=F32)
    hid = u + pltpu.roll(v, nc - 1, 0)
    o_ref[...] = jnp.dot(jax.nn.gelu(hid).astype(BF16), w2_ref[...].astype(BF16),
                         preferred_element_type=F32).astype(o_ref.dtype)


def _nsa_compress(proj, k_cb, pe, w1, w2):
    seq = proj.shape[0]
    nc = seq // NSA_CMP_STRIDE
    d = NSA_HEAD_DIM
    g = NSA_KV_GROUPS
    return pl.pallas_call(
        functools.partial(_nsa_cmp_kernel, nc=nc),
        grid=(2, g),
        in_specs=[pl.BlockSpec((seq, d), lambda a, b: (0, k_cb + a * g + b)),
                  pl.BlockSpec((None, NSA_CMP_LEN, d), lambda a, b: (a, 0, 0)),
                  pl.BlockSpec((None, NSA_CMP_LEN * d, NSA_CMP_HIDDEN), lambda a, b: (a, 0, 0)),
                  pl.BlockSpec((None, NSA_CMP_HIDDEN, d), lambda a, b: (a, 0, 0))],
        out_specs=pl.BlockSpec((None, None, nc, d), lambda a, b: (a, b, 0, 0)),
        out_shape=jax.ShapeDtypeStruct((2, g, nc, d), BF16),
        scratch_shapes=[pltpu.VMEM((seq, d), F32)],
        compiler_params=_cparams("parallel", "arbitrary"),
        name="nsa_compress",
    )(proj, pe, w1, w2)


def _nsa_select_kernel(consec_ref, q_ref, kc_ref, vc_ref, posq_ref, posc_ref, tab_ref, gate_ref, ov_ref,
                       oc_ref, sel_ref, strip_ref, *, nc, n_slc, tq):
    qi = pl.program_id(0)
    d = NSA_HEAD_DIM
    c = d ** -0.5 * LOG2E
    per_block = tq // NSA_CMP_STRIDE
    lane = lax.broadcasted_iota(I32, (1, tq), 1)
    tok = qi * tq + lane

    def select(bias):
        gates = jax.nn.sigmoid(gate_ref[...])
        ovt = ov_ref[...]
        js = lax.broadcasted_iota(I32, (LANES, tq), 0)
        jf = js.astype(F32)
        cur = tok // NSA_SLC_BLOCK
        forced = (js == 0) | (js == cur) | (js == cur - 1)
        for g in range(NSA_KV_GROUPS):
            kc = kc_ref[g]
            vct = vc_ref[g].astype(F32).T.astype(BF16)
            psum = jnp.zeros((nc, tq), F32)
            for r in range(NSA_REP):
                h = g * NSA_REP + r
                qt = (q_ref[:, h * d:(h + 1) * d].astype(F32) * c).T.astype(BF16)
                t = jnp.dot(kc, qt, preferred_element_type=F32) + bias(h)
                m = jnp.max(t, axis=0, keepdims=True)
                e = jnp.exp2(t - m)
                l = jnp.sum(e, axis=0, keepdims=True)
                p = e * jnp.where(m > 0.5 * NEG_INF, 1.0 / l, 0.0)
                o = jnp.dot(vct, p.astype(BF16), preferred_element_type=F32)
                oc_ref[:, h * d:(h + 1) * d] = gates[:, 3 * h:3 * h + 1] * o.T
                psum = psum + p
            p_hi = psum.astype(BF16)
            p_lo = (psum - p_hi.astype(F32)).astype(BF16)
            imp = jnp.dot(ovt, p_hi, preferred_element_type=F32) + jnp.dot(ovt, p_lo, preferred_element_type=F32)
            st = jnp.where(forced, FORCE_SCORE, jnp.where(js > cur, -1.0, imp))
            st = jnp.where(js < n_slc, st, -2.0)
            sel = jnp.zeros((LANES, tq), F32)
            for _ in range(NSA_SLC_TOPK):
                mx = jnp.max(st, axis=0, keepdims=True)
                first = jnp.min(jnp.where(st == mx, jf, float(LANES)), axis=0, keepdims=True)
                hit = jf == first
                sel = jnp.where(hit, 1.0, sel)
                st = jnp.where(hit, -3e38, st)
            sel_ref[g] = sel.astype(sel_ref.dtype)

    consecutive = consec_ref[0] == 1

    @pl.when(jnp.logical_and(consecutive, qi == 0))
    def _():
        for chunk in range(2 * nc // LANES):
            rel = chunk * LANES - nc + lax.broadcasted_iota(I32, (LANES, 1), 0)
            dist = lane - (rel * NSA_CMP_STRIDE + (NSA_CMP_LEN - 1))
            bucket = jnp.where(dist >= 0, _t5_bucket(dist), T5_MASK_BUCKET)
            for h in range(NSA_HEADS):
                strip_ref[h, chunk * LANES:(chunk + 1) * LANES, :] = _t5_lookup(tab_ref[h:h + 1, :], bucket)

    @pl.when(consecutive)
    def _():
        start = pl.multiple_of(nc - per_block * qi, per_block)
        select(lambda h: strip_ref[h, pl.ds(start, nc), :])

    @pl.when(jnp.logical_not(consecutive))
    def _():
        cmp_end = lax.broadcasted_iota(I32, (nc, 1), 0) * NSA_CMP_STRIDE + (NSA_CMP_LEN - 1)
        pos_c = jnp.concatenate([posc_ref[...]] * (tq // LANES), axis=1)
        bucket = jnp.where(cmp_end <= tok, _t5_bucket(posq_ref[...] - pos_c), T5_MASK_BUCKET)
        select(lambda h: _t5_lookup(tab_ref[h:h + 1, :], bucket))


def _nsa_select(proj_b, q_cb, proj_f, gate_cb, kc, vc, pos_row, pos_cmp_rep, consec, tab_t, ov_t, *, n_slc, tq):
    seq = proj_b.shape[0]
    nc = kc.shape[1]
    g = NSA_KV_GROUPS
    d = NSA_HEAD_DIM
    grid_spec = pltpu.PrefetchScalarGridSpec(
        num_scalar_prefetch=1,
        grid=(seq // tq,),
        in_specs=[pl.BlockSpec((tq, NSA_WIDTH), lambda i, *_: (i, q_cb)),
                  pl.BlockSpec((g, nc, d), lambda i, *_: (0, 0, 0)),
                  pl.BlockSpec((g, nc, d), lambda i, *_: (0, 0, 0)),
                  pl.BlockSpec((1, tq), lambda i, *_: (0, i)),
                  pl.BlockSpec((nc, LANES), lambda i, *_: (0, 0)),
                  pl.BlockSpec((SUBLANES, LANES), lambda i, *_: (0, 0)),
                  pl.BlockSpec((tq, LANES), lambda i, *_: (i, gate_cb)),
                  pl.BlockSpec((LANES, nc), lambda i, *_: (0, 0))],
        out_specs=[pl.BlockSpec((tq, NSA_WIDTH), lambda i, *_: (i, 0)),
                   pl.BlockSpec((g, LANES, tq), lambda i, *_: (0, 0, i))],
        scratch_shapes=[pltpu.VMEM((NSA_HEADS, 2 * nc, tq), F32)])
    return pl.pallas_call(
        functools.partial(_nsa_select_kernel, nc=nc, n_slc=n_slc, tq=tq),
        grid_spec=grid_spec,
        out_shape=[jax.ShapeDtypeStruct((seq, NSA_WIDTH), F32),
                   jax.ShapeDtypeStruct((g, LANES, seq), F32)],
        compiler_params=_cparams("arbitrary"),
        name="nsa_select",
    )(consec, proj_b, kc, vc, pos_row, pos_cmp_rep, tab_t * LOG2E, proj_f, ov_t)


def _nsa_slc_kernel(pqmin_ref, pkmax_ref, consec_ref, q_ref, k_ref, v_ref, sel_ref, posq_ref, posk_ref, tab_ref,
                    o_ref, vt_ref, qt_ref, s_ref, m_ref, l_ref, acc_ref, cache_ref, *, tq, tk):
    g = pl.program_id(0)
    qi = pl.program_id(1)
    d = NSA_HEAD_DIM
    scale = d ** -0.5
    seq = k_ref.shape[0]
    consecutive = consec_ref[0] == 1
    n_cached = cache_ref.shape[0]

    @pl.when(qi == 0)
    def _():
        _build_vt(v_ref, vt_ref, 0, d, seq, tk)

    @pl.when(jnp.logical_and(consecutive, qi == 0))
    def _():
        rel = lax.broadcasted_iota(I32, (1, tq), 1) - lax.broadcasted_iota(I32, (tk, 1), 0)
        for v in range(n_cached):
            dist = rel + v * tq
            bucket = jnp.where(dist >= 0, _t5_bucket(dist), T5_MASK_BUCKET)
            for r in range(NSA_REP):
                cache_ref[v, r] = _t5_lookup(tab_ref[pl.ds(g * NSA_REP + r, 1), :], bucket)

    for r in range(NSA_REP):
        qt_ref[r] = (q_ref[:, r * d:(r + 1) * d].astype(F32) * (scale * LOG2E)).T.astype(BF16)
    m_ref[...] = jnp.full(m_ref.shape, NEG_INF, F32)
    l_ref[...] = jnp.zeros(l_ref.shape, F32)
    acc_ref[...] = jnp.zeros(acc_ref.shape, F32)
    pos_q = posq_ref[...]
    per_tile = tk // NSA_SLC_BLOCK
    n_full = (qi * tq) // tk

    def scores(j, slot):
        start = pl.multiple_of(j * tk, tk)
        k = k_ref[pl.ds(start, tk), :]
        for r in range(NSA_REP):
            s_ref[slot, r] = jnp.dot(k, qt_ref[r], preferred_element_type=F32)

    def softmax_pv(j, slot, masked, mode):
        start = pl.multiple_of(j * tk, tk)
        picked = jnp.concatenate(
            [jnp.broadcast_to(sel_ref[pl.ds(j * per_tile + b, 1), :], (NSA_SLC_BLOCK, tq)) for b in range(per_tile)],
            axis=0)
        if masked and mode == "gather":
            key = start + lax.broadcasted_iota(I32, (tk, tq), 0)
            qry = qi * tq + lax.broadcasted_iota(I32, (tk, tq), 1)
            picked = jnp.where(key <= qry, picked, 0.0)
        mask = picked > 0.5
        if mode == "gather":
            pos_k = jnp.concatenate([posk_ref[pl.ds(start, tk), :]] * (tq // LANES), axis=1)
            bucket = jnp.where(mask, _t5_bucket(pos_q - pos_k), T5_MASK_BUCKET)
        for r in range(NSA_REP):
            h = g * NSA_REP + r
            t = s_ref[slot, r]
            if mode == "gather":
                t = t + _t5_lookup(tab_ref[pl.ds(h, 1), :], bucket)
            elif mode == "cached":
                t = jnp.where(mask, t + cache_ref[(qi * tq - start) // tq, r], NEG_INF)
            else:
                t = jnp.where(mask, t, NEG_INF)
            m_prev = m_ref[r]
            mx = jnp.max(t, axis=0, keepdims=True)
            if mode == "far":
                b = tab_ref[pl.ds(h, 1), T5_BUCKETS - 1:T5_BUCKETS]
                m_new = jnp.maximum(m_prev, mx + b)
                shift = m_new - b
            else:
                m_new = jnp.maximum(m_prev, mx)
                shift = m_new
            alpha = jnp.exp2(m_prev - m_new)
            p = jnp.exp2(t - shift)
            l_ref[r] = alpha * l_ref[r] + jnp.sum(p, axis=0, keepdims=True)
            acc_ref[r] = alpha * acc_ref[r] + jnp.dot(vt_ref[0, :, pl.ds(start, tk)], p.astype(BF16),
                                                       preferred_element_type=F32)
            m_ref[r] = m_new

    def is_far(j):
        return pqmin_ref[qi] - pkmax_ref[j] >= T5_MAX_DIST

    def stages(j, count, mode):
        for k in range(count):
            scores(j + k + 1, (k + 1) % 2)
            softmax_pv(j + k, k % 2, False, mode)

    def near(fn):
        @pl.when(consecutive)
        def _():
            fn("cached")

        @pl.when(jnp.logical_not(consecutive))
        def _():
            fn("gather")

    def far_or_near(far, fn):
        @pl.when(far)
        def _():
            fn("far")

        @pl.when(jnp.logical_not(far))
        def _():
            near(fn)

    scores(0, 0)

    def pair(jj, carry):
        j = 2 * jj
        far_or_near(jnp.logical_and(is_far(j), is_far(j + 1)), lambda mode: stages(j, 2, mode))
        return carry

    pairs = n_full // 2
    lax.fori_loop(0, pairs, pair, 0)
    last = 2 * pairs

    @pl.when(n_full % 2 == 1)
    def _():
        far_or_near(is_far(last), lambda mode: stages(last, 1, mode))
        near(lambda mode: softmax_pv(last + 1, 1, True, mode))

    @pl.when(n_full % 2 == 0)
    def _():
        near(lambda mode: softmax_pv(last, 0, True, mode))

    for r in range(NSA_REP):
        o_ref[:, r * d:(r + 1) * d] = (acc_ref[r] / l_ref[r]).T


def _nsa_slc(proj_b, q_off, k_off, v_off, sel, pos_row, pos_rep, tab_t, pq_min, pk_max, consec, *, tq, tk):
    seq = proj_b.shape[0]
    tab_t = tab_t * LOG2E
    d = NSA_HEAD_DIM
    gw = NSA_REP * d
    once = pl.Buffered(1)
    n_cached = (T5_MAX_DIST + tk - 2) // tq + 1 + tk // tq
    grid_spec = pltpu.PrefetchScalarGridSpec(
        num_scalar_prefetch=3,
        grid=(NSA_KV_GROUPS, seq // tq),
        in_specs=[pl.BlockSpec((tq, gw), lambda g, i, *_: (i, q_off // gw + g)),
                  pl.BlockSpec((seq, d), lambda g, i, *_: (0, k_off // d + g), pipeline_mode=once),
                  pl.BlockSpec((seq, d), lambda g, i, *_: (0, v_off // d + g), pipeline_mode=once),
                  pl.BlockSpec((None, LANES, tq), lambda g, i, *_: (g, 0, i)),
                  pl.BlockSpec((1, tq), lambda g, i, *_: (0, i)),
                  pl.BlockSpec((seq, LANES), lambda g, i, *_: (0, 0), pipeline_mode=once),
                  pl.BlockSpec((SUBLANES, LANES), lambda g, i, *_: (0, 0))],
        out_specs=pl.BlockSpec((tq, gw), lambda g, i, *_: (i, g)),
        scratch_shapes=[pltpu.VMEM((1, d, seq), BF16), pltpu.VMEM((NSA_REP, d, tq), BF16),
                        pltpu.VMEM((2, NSA_REP, tk, tq), F32), pltpu.VMEM((NSA_REP, 1, tq), F32),
                        pltpu.VMEM((NSA_REP, 1, tq), F32), pltpu.VMEM((NSA_REP, d, tq), F32),
                        pltpu.VMEM((n_cached, NSA_REP, tk, tq), F32)])
    return pl.pallas_call(
        functools.partial(_nsa_slc_kernel, tq=tq, tk=tk),
        grid_spec=grid_spec,
        out_shape=jax.ShapeDtypeStruct((seq, NSA_WIDTH), F32),
        compiler_params=_cparams("parallel", "arbitrary"),
        name="nsa_slc",
    )(pq_min, pk_max, consec, proj_b, proj_b, proj_b, sel, pos_row, pos_rep, tab_t)


def _nsa_win_kernel(*refs, tq, nt):
    consec_ref = refs[0]
    q_ref = refs[1]
    k_refs = refs[2:2 + nt]
    v_refs = refs[2 + nt:2 + 2 * nt]
    pk_refs = refs[2 + 2 * nt:2 + 3 * nt]
    posq_ref, tab_ref, gate_ref, gout_ref, oc_ref, os_ref, o_ref, bias_ref = refs[2 + 3 * nt:]
    qi = pl.program_id(0)
    d = NSA_HEAD_DIM
    c = d ** -0.5 * LOG2E
    sub = lax.broadcasted_iota(I32, (tq, 1), 0)
    lane = lax.broadcasted_iota(I32, (1, tq), 1)

    def band_bucket(kidx, tok, dist):
        lower = jnp.maximum(tok - (NSA_WINDOW - 1), 0)
        b = jnp.where(kidx >= lower, _t5_bucket(dist), T5_MASK_BUCKET)
        return jnp.where(kidx <= tok, b, T5_MASK_BUCKET)

    def attend(bias):
        gates = jax.nn.sigmoid(gate_ref[...])
        for g in range(NSA_KV_GROUPS):
            ks = [kr[:, g * d:(g + 1) * d] for kr in k_refs]
            vts = [vr[:, g * d:(g + 1) * d].astype(F32).T.astype(BF16) for vr in v_refs]
            for r in range(NSA_REP):
                h = g * NSA_REP + r
                hs = slice(h * d, (h + 1) * d)
                qt = (q_ref[:, hs].astype(F32) * c).T.astype(BF16)
                ts = [jnp.dot(ks[jj], qt, preferred_element_type=F32) + bias(h, jj) for jj in range(nt)]
                m = functools.reduce(jnp.maximum, [jnp.max(t, axis=0, keepdims=True) for t in ts])
                ps = [jnp.exp2(t - m) for t in ts]
                l = functools.reduce(jnp.add, [jnp.sum(p, axis=0, keepdims=True) for p in ps])
                o_t = functools.reduce(jnp.add, [jnp.dot(vt, p.astype(BF16), preferred_element_type=F32)
                                                 for vt, p in zip(vts, ps)])
                o_w = (o_t / l).T
                o = (oc_ref[:, hs] + gates[:, 3 * h + 1:3 * h + 2] * os_ref[:, hs]
                     + gates[:, 3 * h + 2:3 * h + 3] * o_w)
                o_ref[:, hs] = (o * _silu(gout_ref[:, hs])).astype(o_ref.dtype)

    consecutive = consec_ref[0] == 1

    @pl.when(jnp.logical_and(consecutive, qi == 0))
    def _():
        tok0 = (nt - 1) * tq + lane
        for jj in range(nt):
            kidx0 = jj * tq + sub
            bucket = band_bucket(kidx0, tok0, tok0 - kidx0)
            for h in range(NSA_HEADS):
                bias_ref[h, jj] = _t5_lookup(tab_ref[h:h + 1, :], bucket)

    cached = jnp.logical_and(consecutive, qi >= nt - 1)

    @pl.when(cached)
    def _():
        attend(lambda h, jj: bias_ref[h, jj])

    @pl.when(jnp.logical_not(cached))
    def _():
        tok = qi * tq + lane
        pos_q = posq_ref[...]
        buckets = []
        for jj in range(nt):
            kidx = (qi - (nt - 1) + jj) * tq + sub
            pos_k = jnp.concatenate([pk_refs[jj][...]] * (tq // LANES), axis=1)
            buckets.append(band_bucket(kidx, tok, pos_q - pos_k))
        attend(lambda h, jj: _t5_lookup(tab_ref[h:h + 1, :], buckets[jj]))


def _nsa_win(proj_b, q_cb, k_cb, v_cb, proj_f, gate_cb, gout_cb, oc, o_s, pos_row, pos_rep, consec, tab_t, *, tq):
    seq = proj_b.shape[0]
    nt = NSA_WINDOW // tq + 1

    def band_rows(jj, cb):
        return pl.BlockSpec((tq, NSA_KV), lambda i, *_: (jnp.maximum(i - (nt - 1) + jj, 0), cb))

    def band_pos(jj):
        return pl.BlockSpec((tq, LANES), lambda i, *_: (jnp.maximum(i - (nt - 1) + jj, 0), 0))

    in_specs = [pl.BlockSpec((tq, NSA_WIDTH), lambda i, *_: (i, q_cb))]
    in_specs += [band_rows(jj, k_cb) for jj in range(nt)]
    in_specs += [band_rows(jj, v_cb) for jj in range(nt)]
    in_specs += [band_pos(jj) for jj in range(nt)]
    in_specs += [pl.BlockSpec((1, tq), lambda i, *_: (0, i)),
                 pl.BlockSpec((SUBLANES, LANES), lambda i, *_: (0, 0)),
                 pl.BlockSpec((tq, LANES), lambda i, *_: (i, gate_cb)),
                 pl.BlockSpec((tq, NSA_WIDTH), lambda i, *_: (i, gout_cb)),
                 pl.BlockSpec((tq, NSA_WIDTH), lambda i, *_: (i, 0)),
                 pl.BlockSpec((tq, NSA_WIDTH), lambda i, *_: (i, 0))]
    args = [proj_b] * (1 + 2 * nt) + [pos_rep] * nt + [pos_row, tab_t * LOG2E, proj_f, proj_f, oc, o_s]
    grid_spec = pltpu.PrefetchScalarGridSpec(
        num_scalar_prefetch=1,
        grid=(seq // tq,),
        in_specs=in_specs,
        out_specs=pl.BlockSpec((tq, NSA_WIDTH), lambda i, *_: (i, 0)),
        scratch_shapes=[pltpu.VMEM((NSA_HEADS, nt, tq, tq), F32)])
    return pl.pallas_call(
        functools.partial(_nsa_win_kernel, tq=tq, nt=nt),
        grid_spec=grid_spec,
        out_shape=jax.ShapeDtypeStruct((seq, NSA_WIDTH), BF16),
        compiler_params=_cparams("arbitrary"),
        name="nsa_win",
    )(consec, *args)


def _in_proj(h, norm_g, w_b, w_f, tiles, name):
    proj_b = _norm_matmul(h, norm_g, w_b, tm=tiles["tm"], tn=tiles["tn_b"], name=name + "_b", out_dtype=BF16)
    proj_f = _norm_matmul(h, norm_g, w_f, tm=tiles["tm"], tn=tiles["tn_f"], name=name + "_f")
    return proj_b, proj_f


def _even_layer(h, mem_kv, norm_g, w_in, s5, w_glu, b_f, tiles):
    pb, pf = _in_proj(h, norm_g, *w_in, tiles, "in_proj_even")
    ob, of = EVEN_B_OFF, EVEN_F_OFF
    b_cat, c_cat, tab, d_skip = s5
    z = _s5_scan(pf, of["u"] // LANES, b_cat, c_cat, d_skip, tab, tc=tiles["s5_tc"])
    y_s5 = _s5_glu(z, w_glu, pf, of["g_s5"] // S5_WIDTH, tm=tiles["tm"], tn=tiles["tn_glu"])
    decay = _decay(pf, of["f"] // LANES, b_f, t=tiles["decay_t"])
    d = FOX_HEAD_DIM
    y_fox = _flash(pb, ob["q"], pb, ob["k"], pb, ob["v"], pf, of["g_fox"], heads=FOX_HEADS, dk=d, dv=d,
                   scale=d ** -0.5, tq=tiles["attn_tq"], tk=tiles["attn_tk"], hp=tiles["attn_hp"], decay=decay,
                   name="fox_attn")
    y_mem = _mem_attn(pb, ob["q_mem"] // MEM_HEAD_DIM, pf, of["g_mem"] // MEM_HEAD_DIM, mem_kv, t=tiles["mem_t"])
    return y_s5, y_fox, y_mem


def _odd_layer(h, mem_kv, norm_g, w_in, mla, nsa, pos, tiles):
    pb, pf = _in_proj(h, norm_g, *w_in, tiles, "in_proj_odd")
    ob, of = ODD_B_OFF, ODD_F_OFF
    g_cq, g_ckv, w_uq, w_ukv, freq = mla
    pos_col_f, pos_row, pos_cmp_rep, pos_rep, pq_min, pk_max, consec = pos
    q_r, k_r, v_r = _mla_up(pf, of["c_q"] // MLA_Q_RANK, of["c_kv"] // MLA_KV_RANK, of["k_rope"] // LANES,
                            g_cq, g_ckv, w_uq, w_ukv, pos_col_f, freq, t=tiles["prep_t"])
    y_mla = _flash(q_r, 0, k_r, 0, v_r, 0, pf, of["g_mla"], heads=MLA_HEADS, dk=2 * LANES, dv=MLA_V,
                   scale=(MLA_NOPE + MLA_ROPE) ** -0.5, tq=tiles["attn_tq"], tk=tiles["attn_tk"], hp=tiles["attn_hp"],
                   name="mla_attn")
    pe, w1, w2, tab_t, ov, n_slc = nsa
    kvc = _nsa_compress(pb, ob["k_cmp"] // NSA_HEAD_DIM, pe, w1, w2)
    oc, sel = _nsa_select(pb, ob["q_nsa"] // NSA_WIDTH, pf, of["gates"] // LANES, kvc[0], kvc[1], pos_row,
                          pos_cmp_rep, consec, tab_t, ov, n_slc=n_slc, tq=tiles["nsa_tq"])
    o_s = _nsa_slc(pb, ob["q_nsa"], ob["k_slc"], ob["v_slc"], sel, pos_row, pos_rep, tab_t, pq_min, pk_max, consec,
                   tq=tiles["slc_tq"], tk=tiles["slc_tk"])
    y_nsa = _nsa_win(pb, ob["q_nsa"] // NSA_WIDTH, ob["k_win"] // NSA_KV, ob["v_win"] // NSA_KV,
                     pf, of["gates"] // LANES, of["g_nsa"] // NSA_WIDTH, oc, o_s, pos_row, pos_rep, consec, tab_t,
                     tq=tiles["win_tq"])
    y_mem = _mem_attn(pb, ob["q_mem"] // MEM_HEAD_DIM, pf, of["g_mem"] // MEM_HEAD_DIM, mem_kv, t=tiles["mem_t"])
    return y_mla, y_nsa, y_mem


def _tiles(seq):
    return {"tm": min(seq, 1024), "tn_b": 512, "tn_f": 768, "tn_out": 512, "tn_glu": 512, "tn_mem": 512,
            "s5_tc": min(seq, 1024), "decay_t": min(seq, 512), "attn_tq": min(seq, 256),
            "attn_tk": min(seq, 512), "attn_hp": 2, "slc_tq": 256, "slc_tk": 512, "nsa_tq": 256, "win_tq": 128,
            "mem_t": min(seq, 1024), "norm_t": min(seq, 512), "prep_t": min(seq, 256)}


def _context(positions, t5_table, seq, tiles):
    pos = positions[0]
    pos_col = pos.reshape(seq, 1)
    pos_row = pos.reshape(1, seq)
    pos_rep = jnp.broadcast_to(pos_col, (seq, LANES))
    pq_min = jnp.min(pos.reshape(seq // tiles["slc_tq"], tiles["slc_tq"]), axis=1)
    pk_max = jnp.max(pos.reshape(seq // tiles["slc_tk"], tiles["slc_tk"]), axis=1)
    nc = seq // NSA_CMP_STRIDE
    pos_cmp = jnp.pad(pos[NSA_CMP_LEN - 1::NSA_CMP_STRIDE], (0, 1))
    pos_cmp_rep = jnp.broadcast_to(pos_cmp.reshape(nc, 1), (nc, LANES))
    half = MLA_ROPE // 2
    inv_freq = ROPE_THETA ** (-jnp.arange(half, dtype=F32) / half)
    freq = jnp.concatenate([inv_freq, inv_freq, jnp.zeros((LANES - MLA_ROPE,), F32)]).reshape(1, LANES)
    tab_t = jnp.pad(t5_table.astype(F32).T, ((0, SUBLANES - NSA_HEADS), (0, LANES - T5_BUCKETS)))
    tab_t = tab_t.at[:, T5_MASK_BUCKET].set(NEG_INF)
    n_slc = seq // NSA_SLC_BLOCK
    cs = np.arange(nc) * NSA_CMP_STRIDE
    ss = np.arange(LANES) * NSA_SLC_BLOCK
    ov_np = np.clip(np.minimum(cs[:, None] + NSA_CMP_LEN, ss[None, :] + NSA_SLC_BLOCK)
                    - np.maximum(cs[:, None], ss[None, :]), 0, None) / NSA_CMP_LEN
    ov_np[nc - 1, :] = 0.0
    ov_np[:, n_slc:] = 0.0
    consec = jnp.all(pos[1:] - pos[:-1] == 1).astype(I32).reshape(1)
    return {"pos": (pos_col.astype(F32), pos_row, pos_cmp_rep, pos_rep, pq_min, pk_max, consec), "freq": freq,
            "tab_t": tab_t, "ov": jnp.asarray(ov_np.T, BF16), "n_slc": n_slc}


def _odd_params(i, mla_g_cq, mla_g_ckv, mla_w_uq, mla_w_ukv, nsa_cmp_pe, nsa_cmp_w1, nsa_cmp_w2, ctx):
    dq = MLA_NOPE + MLA_ROPE
    w_uq = mla_w_uq[i].reshape(MLA_Q_RANK, MLA_HEADS, dq)
    w_uq = jnp.pad(w_uq, ((0, 0), (0, 0), (0, 2 * LANES - dq))).reshape(MLA_Q_RANK, -1).astype(BF16)
    mla = (mla_g_cq[i], mla_g_ckv[i], w_uq, mla_w_ukv[i].astype(BF16), ctx["freq"])
    nsa = (nsa_cmp_pe[i].astype(F32), nsa_cmp_w1[i], nsa_cmp_w2[i], ctx["tab_t"],
           ctx["ov"], ctx["n_slc"])
    return mla, nsa


def kernel(x, mem, positions, norm_g, mem_norm_g, final_norm_g, t5_table, w_out, mem_w_kv, even_w_in, s5_lam_re,
           s5_lam_im, s5_log_dt, s5_b_re, s5_b_im, s5_c_re, s5_c_im, s5_d, s5_w_glu, fox_b_f, odd_w_in, mla_g_cq,
           mla_g_ckv, mla_w_uq, mla_w_ukv, nsa_cmp_pe, nsa_cmp_w1, nsa_cmp_w2):
    batch, seq, _ = x.shape
    assert batch == 1 and seq % 1024 == 0 and seq // NSA_SLC_BLOCK <= LANES
    depth = norm_g.shape[0]
    tiles = _tiles(seq)
    ctx = _context(positions, t5_table, seq, tiles)
    h = x[0]
    mem_kv_all = _mem_kv_all(mem[0], mem_norm_g, mem_w_kv, tn=tiles["tn_mem"])
    for layer in range(depth):
        i = layer // 2
        mem_kv = (mem_kv_all, layer)
        if layer % 2 == 0:
            w_in = (_reorder_w_in(even_w_in[i], EVEN_SPLITS, EVEN_B_ORDER),
                    _reorder_w_in(even_w_in[i], EVEN_SPLITS, EVEN_F_ORDER))
            b_cat, c_cat, tab = _s5_prepare(s5_lam_re[i], s5_lam_im[i], s5_log_dt[i], s5_b_re[i], s5_b_im[i],
                                            s5_c_re[i], s5_c_im[i])
            ys = _even_layer(h, mem_kv, norm_g[layer], w_in, (b_cat, c_cat, tab, s5_d[i]),
                             s5_w_glu[i], fox_b_f[i], tiles)
        else:
            w_in = (_reorder_w_in(odd_w_in[i], ODD_SPLITS, ODD_B_ORDER),
                    _reorder_w_in(odd_w_in[i], ODD_SPLITS, ODD_F_ORDER))
            mla, nsa = _odd_params(i, mla_g_cq, mla_g_ckv, mla_w_uq, mla_w_ukv, nsa_cmp_pe, nsa_cmp_w1, nsa_cmp_w2,
                                   ctx)
            ys = _odd_layer(h, mem_kv, norm_g[layer], w_in, mla, nsa, ctx["pos"], tiles)
        h = _out_proj(h, ys, w_out, layer, tm=tiles["tm"], tn=tiles["tn_out"])
    return _final_norm(h, final_norm_g, tm=tiles["norm_t"])[None]
```

```python
import functools
import math

import numpy as np
import jax
import jax.numpy as jnp
from jax import lax
from jax.experimental import pallas as pl
from jax.experimental.pallas import tpu as pltpu

F32 = jnp.float32
BF16 = jnp.bfloat16
I32 = jnp.int32

RMS_EPS = 1e-6
NEG_INF = -1e30
LOG2E = math.log2(math.e)

S5_WIDTH = 1024
S5_GROUP = 16
S5_GROUPS = S5_WIDTH // S5_GROUP
S5_STATE = 64
FOX_HEADS = 8
FOX_HEAD_DIM = 128
FOX_WIDTH = FOX_HEADS * FOX_HEAD_DIM
MEM_HEADS = 4
MEM_HEAD_DIM = 128
MEM_WIDTH = MEM_HEADS * MEM_HEAD_DIM
MLA_HEADS = 8
MLA_Q_RANK = 512
MLA_KV_RANK = 512
MLA_NOPE = 128
MLA_ROPE = 64
MLA_V = 128
MLA_WIDTH = MLA_HEADS * MLA_V
ROPE_THETA = 10000.0
NSA_HEADS = 8
NSA_KV_GROUPS = 2
NSA_REP = NSA_HEADS // NSA_KV_GROUPS
NSA_HEAD_DIM = 128
NSA_WIDTH = NSA_HEADS * NSA_HEAD_DIM
NSA_KV = NSA_KV_GROUPS * NSA_HEAD_DIM
NSA_CMP_LEN = 32
NSA_CMP_STRIDE = 16
NSA_CMP_HIDDEN = 256
NSA_SLC_BLOCK = 64
NSA_SLC_TOPK = 16
NSA_WINDOW = 512
FORCE_SCORE = 1e6
T5_BUCKETS = 32
T5_MAX_DIST = 1024

EVEN_SPLITS = (S5_WIDTH, S5_WIDTH, FOX_WIDTH, FOX_WIDTH, FOX_WIDTH, FOX_HEADS, FOX_WIDTH, MEM_WIDTH, MEM_WIDTH)
ODD_SPLITS = (MLA_Q_RANK, MLA_KV_RANK, MLA_ROPE, MLA_WIDTH, NSA_WIDTH, NSA_KV, NSA_KV, NSA_KV, NSA_KV, NSA_KV,
              NSA_KV, 3 * NSA_HEADS, NSA_WIDTH, MEM_WIDTH, MEM_WIDTH)

LANES = 128
SUBLANES = 8
MXU_COLS = 256
VMEM_LIMIT_BYTES = 56 * 1024 * 1024

EVEN_B_ORDER = (("q", 2, 1024), ("k", 3, 1024), ("v", 4, 1024), ("q_mem", 7, 512))
EVEN_F_ORDER = (("u", 0, 1024), ("g_s5", 1, 1024), ("g_fox", 6, 1024), ("g_mem", 8, 512), ("f", 5, 128),
                ("pad", None, 128))
ODD_B_ORDER = (("q_nsa", 4, 1024), ("q_mem", 13, 512), ("k_cmp", 5, 256), ("v_cmp", 6, 256), ("k_slc", 7, 256),
               ("v_slc", 8, 256), ("k_win", 9, 256), ("v_win", 10, 256))
ODD_F_ORDER = (("g_nsa", 12, 1024), ("g_mla", 3, 1024), ("c_q", 0, 512), ("c_kv", 1, 512), ("g_mem", 14, 512),
               ("k_rope", 2, 128), ("gates", 11, 128))


def _layout(order):
    off, out = 0, {}
    for name, _, width in order:
        assert off % width == 0
        out[name] = off
        off += width
    return out, off


EVEN_B_OFF, EVEN_B_N = _layout(EVEN_B_ORDER)
EVEN_F_OFF, EVEN_F_N = _layout(EVEN_F_ORDER)
ODD_B_OFF, ODD_B_N = _layout(ODD_B_ORDER)
ODD_F_OFF, ODD_F_N = _layout(ODD_F_ORDER)


def _reorder_w_in(w, splits, order):
    starts = np.concatenate([[0], np.cumsum(splits)])
    cols = []
    for _, idx, width in order:
        if idx is None:
            cols.append(jnp.zeros((w.shape[0], width), w.dtype))
            continue
        seg = w[:, int(starts[idx]):int(starts[idx + 1])]
        pad = width - seg.shape[1]
        if pad:
            seg = jnp.pad(seg, ((0, 0), (0, pad)))
        cols.append(seg)
    return jnp.concatenate(cols, axis=1).astype(BF16)


def _cparams(*sem):
    return pltpu.CompilerParams(dimension_semantics=sem, vmem_limit_bytes=VMEM_LIMIT_BYTES)


def _silu(g):
    return g * jax.nn.sigmoid(g)


def _norm_matmul_kernel(x_ref, g_ref, w_ref, o_ref, xn_ref):
    @pl.when(pl.program_id(1) == 0)
    def _():
        x = x_ref[...]
        ms = jnp.mean(x * x, axis=-1, keepdims=True)
        xn_ref[...] = (x * lax.rsqrt(ms + RMS_EPS) * g_ref[...]).astype(BF16)

    o_ref[...] = jnp.dot(xn_ref[...], w_ref[...].astype(BF16), preferred_element_type=F32).astype(o_ref.dtype)


def _norm_matmul(x, g, w, *, x_cb=0, tm, tn, name, out_dtype=F32):
    m = x.shape[0]
    k, n = w.shape
    return pl.pallas_call(
        _norm_matmul_kernel,
        grid=(m // tm, n // tn),
        in_specs=[pl.BlockSpec((tm, k), lambda i, j: (i, x_cb)),
                  pl.BlockSpec((1, k), lambda i, j: (0, 0)),
                  pl.BlockSpec((k, tn), lambda i, j: (0, j))],
        out_specs=pl.BlockSpec((tm, tn), lambda i, j: (i, j)),
        out_shape=jax.ShapeDtypeStruct((m, n), out_dtype),
        scratch_shapes=[pltpu.VMEM((tm, k), BF16)],
        compiler_params=_cparams("parallel", "arbitrary"),
        name=name,
    )(x, g.reshape(1, k), w)


def _out_proj_kernel(h_ref, *refs):
    o_ref = refs[-1]
    n = (len(refs) - 1) // 2
    acc = h_ref[...]
    for y_ref, w_ref in zip(refs[:n], refs[n:2 * n]):
        acc = acc + jnp.dot(y_ref[...], w_ref[...].astype(BF16), preferred_element_type=F32)
    o_ref[...] = acc


def _out_proj(h, ys, w_all, layer, *, tm, tn):
    m, n = h.shape
    in_specs = [pl.BlockSpec((tm, tn), lambda i, j: (i, j))]
    in_specs += [pl.BlockSpec((tm, y.shape[1]), lambda i, j: (i, 0)) for y in ys]
    row = 0
    for y in ys:
        width = y.shape[1]
        assert row % width == 0
        in_specs.append(pl.BlockSpec((None, width, tn), lambda i, j, rb=row // width: (layer, rb, j)))
        row += width
    return pl.pallas_call(
        _out_proj_kernel,
        grid=(m // tm, n // tn),
        in_specs=in_specs,
        out_specs=pl.BlockSpec((tm, tn), lambda i, j: (i, j)),
        out_shape=jax.ShapeDtypeStruct((m, n), F32),
        compiler_params=_cparams("parallel", "arbitrary"),
        name="out_proj",
    )(h, *ys, *([w_all] * len(ys)))


def _final_norm_kernel(x_ref, g_ref, o_ref):
    x = x_ref[...]
    ms = jnp.mean(x * x, axis=-1, keepdims=True)
    o_ref[...] = x * lax.rsqrt(ms + RMS_EPS) * g_ref[...]


def _final_norm(h, g, *, tm):
    m, n = h.shape
    return pl.pallas_call(
        _final_norm_kernel,
        grid=(m // tm,),
        in_specs=[pl.BlockSpec((tm, n), lambda i: (i, 0)), pl.BlockSpec((1, n), lambda i: (0, 0))],
        out_specs=pl.BlockSpec((tm, n), lambda i: (i, 0)),
        out_shape=jax.ShapeDtypeStruct((m, n), F32),
        compiler_params=_cparams("parallel"),
        name="final_norm",
    )(h, g.reshape(1, n))


ONES_ROWS = 16


def _build_vt(v_ref, vt_ref, hh, dv, seq, chunk):
    def body(c, carry):
        st = pl.multiple_of(c * chunk, chunk)
        vt_ref[hh, 0:dv, pl.ds(st, chunk)] = v_ref[pl.ds(st, chunk), hh * dv:(hh + 1) * dv].astype(F32).T.astype(BF16)
        return carry

    lax.fori_loop(0, seq // chunk, body, 0)
    extra = vt_ref.shape[1] - dv
    if extra:
        vt_ref[hh, dv:dv + extra, :] = jnp.ones((extra, seq), BF16)


def _flash_kernel(*refs, scale, tq, tk, hp, dk, dv, has_decay):
    cw = min(tq, MXU_COLS)
    ncg = tq // cw
    streams = [(hh, cg) for hh in range(hp) for cg in range(ncg)]
    if has_decay:
        q_ref, k_ref, v_ref, g_ref, cq_ref, ck_ref, o_ref, vt_ref, qt_ref, s_ref, m_ref, acc_ref = refs
    else:
        q_ref, k_ref, v_ref, g_ref, o_ref, vt_ref, qt_ref, s_ref, m_ref, acc_ref = refs
    hb = pl.program_id(0)
    qi = pl.program_id(1)
    seq = k_ref.shape[0]

    @pl.when(qi == 0)
    def _():
        for hh in range(hp):
            _build_vt(v_ref, vt_ref, hh, dv, seq, tk)

    for v, (hh, cg) in enumerate(streams):
        qt_ref[v] = (q_ref[cg * cw:(cg + 1) * cw, hh * dk:(hh + 1) * dk].astype(F32) * (scale * LOG2E)).T.astype(BF16)
    m_ref[...] = jnp.full(m_ref.shape, NEG_INF, F32)
    acc_ref[...] = jnp.zeros(acc_ref.shape, F32)
    n_full = (qi * tq) // tk
    if has_decay:
        cq2 = [cq_ref[pl.ds(hb * hp + hh, 1), cg * cw:(cg + 1) * cw] for hh, cg in streams]

    def scores(j, slot):
        start = pl.multiple_of(j * tk, tk)
        for v, (hh, cg) in enumerate(streams):
            s_ref[slot, v] = jnp.dot(k_ref[pl.ds(start, tk), hh * dk:(hh + 1) * dk], qt_ref[v],
                                     preferred_element_type=F32)

    def softmax_pv(j, slot, masked):
        start = pl.multiple_of(j * tk, tk)
        for v, (hh, cg) in enumerate(streams):
            t = s_ref[slot, v]
            if has_decay:
                t = t - jnp.concatenate([ck_ref[hh, pl.ds(start, tk), :]] * (cw // LANES), axis=1)
            if masked:
                key = start + lax.broadcasted_iota(I32, (tk, cw), 0)
                qry = qi * tq + cg * cw + lax.broadcasted_iota(I32, (tk, cw), 1)
                t = jnp.where(key <= qry, t, NEG_INF)
            m_prev = m_ref[v]
            mx = jnp.max(t, axis=0, keepdims=True)
            if has_decay:
                m_new = jnp.maximum(m_prev, mx + cq2[v])
                shift = m_new - cq2[v]
            else:
                m_new = jnp.maximum(m_prev, mx)
                shift = m_new
            alpha = jnp.exp2(m_prev - m_new)
            p = jnp.exp2(t - shift).astype(BF16)
            acc_ref[v] = alpha * acc_ref[v] + jnp.dot(vt_ref[hh, :, pl.ds(start, tk)], p,
                                                       preferred_element_type=F32)
            m_ref[v] = m_new

    scores(0, 0)

    def pair(j):
        scores(j + 1, 1)
        softmax_pv(j, 0, False)
        scores(j + 2, 0)
        softmax_pv(j + 1, 1, False)

    def quad(qq, carry):
        pair(4 * qq)
        pair(4 * qq + 2)
        return carry

    quads = n_full // 4
    lax.fori_loop(0, quads, quad, 0)
    rem = n_full - 4 * quads

    @pl.when(rem >= 2)
    def _():
        pair(4 * quads)

    last = 4 * quads + 2 * (rem // 2)

    @pl.when(n_full % 2 == 1)
    def _():
        scores(last + 1, 1)
        softmax_pv(last, 0, False)
        softmax_pv(last + 1, 1, True)

    @pl.when(n_full % 2 == 0)
    def _():
        softmax_pv(last, 0, True)

    for v, (hh, cg) in enumerate(streams):
        a = acc_ref[v]
        o = (a[0:dv, :] / a[dv:dv + 1, :]).T
        rows = slice(cg * cw, (cg + 1) * cw)
        cols = slice(hh * dv, (hh + 1) * dv)
        o_ref[rows, cols] = (o * _silu(g_ref[rows, cols])).astype(o_ref.dtype)


def _flash(q_arr, q_off, k_arr, k_off, v_arr, v_off, g_arr, g_off, *, heads, dk, dv, scale, tq, tk, hp,
           decay=None, name):
    seq = q_arr.shape[0]
    cw = min(tq, MXU_COLS)
    ns = hp * (tq // cw)
    once = pl.Buffered(1)
    in_specs = [pl.BlockSpec((tq, hp * dk), lambda h, i: (i, q_off // (hp * dk) + h)),
                pl.BlockSpec((seq, hp * dk), lambda h, i: (0, k_off // (hp * dk) + h), pipeline_mode=once),
                pl.BlockSpec((seq, hp * dv), lambda h, i: (0, v_off // (hp * dv) + h), pipeline_mode=once),
                pl.BlockSpec((tq, hp * dv), lambda h, i: (i, g_off // (hp * dv) + h))]
    args = [q_arr, k_arr, v_arr, g_arr]
    if decay is not None:
        cum_t, cum_rep = decay
        in_specs += [pl.BlockSpec((SUBLANES, tq), lambda h, i: (0, i)),
                     pl.BlockSpec((hp, seq, LANES), lambda h, i: (h, 0, 0), pipeline_mode=once)]
        args += [cum_t, cum_rep]
    return pl.pallas_call(
        functools.partial(_flash_kernel, scale=scale, tq=tq, tk=tk, hp=hp, dk=dk, dv=dv, has_decay=decay is not None),
        grid=(heads // hp, seq // tq),
        in_specs=in_specs,
        out_specs=pl.BlockSpec((tq, hp * dv), lambda h, i: (i, h)),
        out_shape=jax.ShapeDtypeStruct((seq, heads * dv), BF16),
        scratch_shapes=[pltpu.VMEM((hp, dv + ONES_ROWS, seq), BF16), pltpu.VMEM((ns, dk, cw), BF16),
                        pltpu.VMEM((2, ns, tk, cw), F32), pltpu.VMEM((ns, 1, cw), F32),
                        pltpu.VMEM((ns, dv + ONES_ROWS, cw), F32)],
        compiler_params=_cparams("parallel", "arbitrary"),
        name=name,
    )(*args)


def _decay_kernel(f_ref, b_ref, ct_ref, cr_ref, carry_ref, *, t):
    i = pl.program_id(0)

    @pl.when(i == 0)
    def _():
        carry_ref[...] = jnp.zeros(carry_ref.shape, F32)

    x = f_ref[...] + b_ref[...]
    lf = jnp.minimum(x, 0.0) - jnp.log1p(jnp.exp(-jnp.abs(x)))
    row = lax.broadcasted_iota(I32, lf.shape, 0)
    s = 1
    while s < t:
        lf = lf + jnp.where(row >= s, pltpu.roll(lf, s, 0), 0.0)
        s *= 2
    lf = lf + carry_ref[...]
    carry_ref[...] = lf[t - 1:t, :]
    lf2 = lf * LOG2E
    ct_ref[...] = lf2.T[:FOX_HEADS, :]
    for h in range(FOX_HEADS):
        cr_ref[h] = jnp.broadcast_to(lf2[:, h:h + 1], (t, LANES))


def _decay(proj, f_cb, b_f, *, t):
    seq = proj.shape[0]
    b = jnp.pad(b_f.reshape(1, FOX_HEADS), ((0, 0), (0, LANES - FOX_HEADS)))
    return pl.pallas_call(
        functools.partial(_decay_kernel, t=t),
        grid=(seq // t,),
        in_specs=[pl.BlockSpec((t, LANES), lambda i: (i, f_cb)), pl.BlockSpec((1, LANES), lambda i: (0, 0))],
        out_specs=[pl.BlockSpec((FOX_HEADS, t), lambda i: (0, i)),
                   pl.BlockSpec((FOX_HEADS, t, LANES), lambda i: (0, i, 0))],
        out_shape=[jax.ShapeDtypeStruct((FOX_HEADS, seq), F32), jax.ShapeDtypeStruct((FOX_HEADS, seq, LANES), F32)],
        scratch_shapes=[pltpu.VMEM((1, LANES), F32)],
        compiler_params=_cparams("arbitrary"),
        name="fox_decay",
    )(proj, b)


def _mem_attn_kernel(q_ref, kv_ref, g_ref, o_ref):
    d = MEM_HEAD_DIM
    for h in range(MEM_HEADS):
        hs = slice(h * d, (h + 1) * d)
        k = kv_ref[:, hs].astype(BF16)
        v = kv_ref[:, MEM_WIDTH + h * d:MEM_WIDTH + (h + 1) * d].astype(BF16)
        s = lax.dot_general(q_ref[:, hs], k, (((1,), (1,)), ((), ())), preferred_element_type=F32) * (d ** -0.5)
        m = jnp.max(s, axis=1, keepdims=True)
        p = jnp.exp(s - m)
        l = jnp.sum(p, axis=1, keepdims=True)
        o = jnp.dot(p.astype(BF16), v, preferred_element_type=F32) / l
        o_ref[:, hs] = (o * _silu(g_ref[:, hs])).astype(o_ref.dtype)


def _mem_kv_all(mem2d, g, w_all, *, tn):
    depth, k, n = w_all.shape
    m = mem2d.shape[0]
    return pl.pallas_call(
        _norm_matmul_kernel,
        grid=(depth, n // tn),
        in_specs=[pl.BlockSpec((m, k), lambda l, j: (0, 0)),
                  pl.BlockSpec((1, k), lambda l, j: (0, 0)),
                  pl.BlockSpec((None, k, tn), lambda l, j: (l, 0, j))],
        out_specs=pl.BlockSpec((None, m, tn), lambda l, j: (l, 0, j)),
        out_shape=jax.ShapeDtypeStruct((depth, m, n), F32),
        scratch_shapes=[pltpu.VMEM((m, k), BF16)],
        compiler_params=_cparams("arbitrary", "arbitrary"),
        name="mem_kv",
    )(mem2d, g.reshape(1, k), w_all)


def _mem_attn(proj_b, q_cb, proj_f, g_cb, mem_kv, *, t):
    seq = proj_b.shape[0]
    mem_kv, layer = mem_kv
    nm = mem_kv.shape[1]
    w = MEM_WIDTH
    d = MEM_HEAD_DIM
    return pl.pallas_call(
        _mem_attn_kernel,
        grid=(seq // t,),
        in_specs=[pl.BlockSpec((t, w), lambda i: (i, q_cb * d // w)),
                  pl.BlockSpec((None, nm, 2 * w), lambda i: (layer, 0, 0)),
                  pl.BlockSpec((t, w), lambda i: (i, g_cb * d // w))],
        out_specs=pl.BlockSpec((t, w), lambda i: (i, 0)),
        out_shape=jax.ShapeDtypeStruct((seq, w), BF16),
        compiler_params=_cparams("parallel"),
        name="mem_attn",
    )(proj_b, mem_kv, proj_f)


S5_TILE_GROUPS = LANES // S5_GROUP
S5_TILE_STATES = S5_TILE_GROUPS * S5_STATE
S5_TILES = S5_GROUPS // S5_TILE_GROUPS


def _s5_scan_kernel(u_ref, b_ref, c_ref, d_ref, tab_ref, z_ref, bu_ref, carry_ref, *, tc):
    ns = S5_TILE_STATES

    @pl.when(pl.program_id(1) == 0)
    def _():
        carry_ref[...] = jnp.zeros(carry_ref.shape, F32)

    u = u_ref[...]
    bu_ref[...] = jnp.dot(u.astype(BF16), b_ref[...], preferred_element_type=F32)
    steps = [(1, tab_ref[0], tab_ref[1]), (2, tab_ref[2], tab_ref[3]), (4, tab_ref[4], tab_ref[5])]
    pr = tab_ref[6]
    pi = tab_ref[7]

    def body(i, carry):
        cr, ci = carry
        r0 = pl.multiple_of(i * SUBLANES, SUBLANES)
        xr = bu_ref[pl.ds(r0, SUBLANES), 0:ns]
        xi = bu_ref[pl.ds(r0, SUBLANES), ns:2 * ns]
        for s, ar, ai in steps:
            sr = pltpu.roll(xr, s, 0)
            si = pltpu.roll(xi, s, 0)
            xr, xi = xr + ar * sr - ai * si, xi + ar * si + ai * sr
        xr, xi = xr + pr * cr - pi * ci, xi + pr * ci + pi * cr
        bu_ref[pl.ds(r0, SUBLANES), 0:ns] = xr
        bu_ref[pl.ds(r0, SUBLANES), ns:2 * ns] = xi
        return xr[SUBLANES - 1:SUBLANES, :], xi[SUBLANES - 1:SUBLANES, :]

    cr, ci = lax.fori_loop(0, tc // SUBLANES, body, (carry_ref[0:1, 0:ns], carry_ref[0:1, ns:2 * ns]))
    carry_ref[0:1, 0:ns] = cr
    carry_ref[0:1, ns:2 * ns] = ci
    y = jnp.dot(bu_ref[...].astype(BF16), c_ref[...], preferred_element_type=F32) + d_ref[...] * u
    z_ref[...] = jax.nn.gelu(y)


def _s5_prepare(lam_re, lam_im, log_dt, b_re, b_im, c_re, c_im):
    dt = jnp.exp(log_dt.astype(F32))[:, None]
    lr = lam_re.astype(F32)
    li = lam_im.astype(F32)
    mag = jnp.exp(lr * dt)
    ab_re = mag * jnp.cos(li * dt)
    ab_im = mag * jnp.sin(li * dt)
    den = lr * lr + li * li
    nr = ab_re - 1.0
    f_re = (nr * lr + ab_im * li) / den
    f_im = (ab_im * lr - nr * li) / den
    br = b_re.astype(F32)
    bim = b_im.astype(F32)
    bb_re = f_re[..., None] * br - f_im[..., None] * bim
    bb_im = f_re[..., None] * bim + f_im[..., None] * br
    eye = jnp.eye(S5_TILE_GROUPS, dtype=F32)

    def blockdiag_in(bb):
        t = bb.reshape(S5_TILES, S5_TILE_GROUPS, S5_STATE, S5_GROUP)
        m = jnp.einsum("jgpc,gh->jgchp", t, eye)
        return m.reshape(S5_TILES, LANES, S5_TILE_STATES)

    def blockdiag_out(cc):
        t = cc.reshape(S5_TILES, S5_TILE_GROUPS, S5_GROUP, S5_STATE)
        m = jnp.einsum("jgcp,gh->jgphc", t, eye)
        return m.reshape(S5_TILES, S5_TILE_STATES, LANES)

    b_cat = jnp.concatenate([blockdiag_in(bb_re), blockdiag_in(bb_im)], axis=2).astype(BF16)
    c_cat = jnp.concatenate([blockdiag_out(c_re.astype(F32)), -blockdiag_out(c_im.astype(F32))], axis=1).astype(BF16)

    a_r = ab_re.reshape(S5_TILES, 1, S5_TILE_STATES)
    a_i = ab_im.reshape(S5_TILES, 1, S5_TILE_STATES)

    def cmul(xr, xi, yr, yi):
        return xr * yr - xi * yi, xr * yi + xi * yr

    a2 = cmul(a_r, a_i, a_r, a_i)
    a4 = cmul(*a2, *a2)
    row = jnp.arange(SUBLANES)[None, :, None]
    tabs = []
    for s, (pr_, pi_) in ((1, (a_r, a_i)), (2, a2), (4, a4)):
        tabs.append(jnp.where(row >= s, pr_, 0.0))
        tabs.append(jnp.where(row >= s, pi_, 0.0))
    pw = [(a_r, a_i)]
    for _ in range(SUBLANES - 1):
        pw.append(cmul(*pw[-1], a_r, a_i))
    tabs.append(jnp.concatenate([p[0] for p in pw], axis=1))
    tabs.append(jnp.concatenate([p[1] for p in pw], axis=1))
    tab = jnp.stack([jnp.broadcast_to(t, (S5_TILES, SUBLANES, S5_TILE_STATES)) for t in tabs], axis=1)
    return b_cat, c_cat, tab.astype(F32)


def _s5_scan(proj, u_cb, b_cat, c_cat, d_skip, tab, *, tc):
    seq = proj.shape[0]
    ns = S5_TILE_STATES
    d = d_skip.astype(F32).reshape(S5_TILES, 1, LANES)
    return pl.pallas_call(
        functools.partial(_s5_scan_kernel, tc=tc),
        grid=(S5_TILES, seq // tc),
        in_specs=[pl.BlockSpec((tc, LANES), lambda j, c: (c, u_cb + j)),
                  pl.BlockSpec((None, LANES, 2 * ns), lambda j, c: (j, 0, 0)),
                  pl.BlockSpec((None, 2 * ns, LANES), lambda j, c: (j, 0, 0)),
                  pl.BlockSpec((None, 1, LANES), lambda j, c: (j, 0, 0)),
                  pl.BlockSpec((None, 8, SUBLANES, ns), lambda j, c: (j, 0, 0, 0))],
        out_specs=pl.BlockSpec((tc, LANES), lambda j, c: (c, j)),
        out_shape=jax.ShapeDtypeStruct((seq, S5_WIDTH), F32),
        scratch_shapes=[pltpu.VMEM((tc, 2 * ns), F32), pltpu.VMEM((SUBLANES, 2 * ns), F32)],
        compiler_params=_cparams("parallel", "arbitrary"),
        name="s5_scan",
    )(proj, b_cat, c_cat, d, tab)


def _s5_glu_kernel(z_ref, w_ref, g_ref, o_ref, *, tn):
    j = pl.program_id(1)
    z = z_ref[...]
    a = jnp.dot(z.astype(BF16), w_ref[...].astype(BF16), preferred_element_type=F32)
    zc = z_ref[:, pl.ds(pl.multiple_of(j * tn, tn), tn)]
    o_ref[...] = (zc * jax.nn.sigmoid(a) * _silu(g_ref[...])).astype(o_ref.dtype)


def _s5_glu(z, w_glu, proj, g_cb, *, tm, tn):
    seq, n = z.shape
    return pl.pallas_call(
        functools.partial(_s5_glu_kernel, tn=tn),
        grid=(seq // tm, n // tn),
        in_specs=[pl.BlockSpec((tm, n), lambda i, j: (i, 0)),
                  pl.BlockSpec((n, tn), lambda i, j: (0, j)),
                  pl.BlockSpec((tm, tn), lambda i, j: (i, g_cb * (n // tn) + j))],
        out_specs=pl.BlockSpec((tm, tn), lambda i, j: (i, j)),
        out_shape=jax.ShapeDtypeStruct((seq, n), BF16),
        compiler_params=_cparams("parallel", "arbitrary"),
        name="s5_glu",
    )(z, w_glu, proj)


def _rope_tables(pos, freq):
    ang = pos * freq
    lane = lax.broadcasted_iota(I32, ang.shape, 1)
    half = MLA_ROPE // 2
    cos = jnp.cos(ang)
    sin = jnp.sin(ang)
    c = jnp.where(lane < MLA_ROPE, cos, 0.0)
    s1 = jnp.where(lane < half, -sin, 0.0)
    s2 = jnp.where((lane >= half) & (lane < MLA_ROPE), sin, 0.0)
    return c, s1, s2


def _rope_apply(x, c, s1, s2):
    half = MLA_ROPE // 2
    return x * c + pltpu.roll(x, LANES - half, 1) * s1 + pltpu.roll(x, half, 1) * s2


def _mla_up_kernel(cq_ref, ckv_ref, kr_ref, pos_ref, freq_ref, gq_ref, gkv_ref, wq_ref, wkv_ref, q_ref, k_ref, v_ref):
    def normed(x_ref, g_ref):
        x = x_ref[...]
        ms = jnp.mean(x * x, axis=-1, keepdims=True)
        return (x * lax.rsqrt(ms + RMS_EPS) * g_ref[...]).astype(BF16)

    qf = jnp.dot(normed(cq_ref, gq_ref), wq_ref[...], preferred_element_type=F32)
    kvf = jnp.dot(normed(ckv_ref, gkv_ref), wkv_ref[...], preferred_element_type=F32)
    c, s1, s2 = _rope_tables(pos_ref[...], freq_ref[...])
    kr = _rope_apply(kr_ref[...], c, s1, s2).astype(BF16)
    for h in range(MLA_HEADS):
        b = 2 * LANES * h
        q_ref[:, b:b + LANES] = qf[:, b:b + LANES].astype(BF16)
        q_ref[:, b + LANES:b + 2 * LANES] = _rope_apply(qf[:, b + LANES:b + 2 * LANES], c, s1, s2).astype(BF16)
        k_ref[:, b:b + LANES] = kvf[:, b:b + LANES].astype(BF16)
        k_ref[:, b + LANES:b + 2 * LANES] = kr
        v_ref[:, LANES * h:LANES * (h + 1)] = kvf[:, b + LANES:b + 2 * LANES].astype(BF16)


def _mla_up(proj, cq_cb, ckv_cb, kr_cb, g_cq, g_ckv, w_uq, w_ukv, pos_col, freq, *, t):
    seq = proj.shape[0]
    w = 2 * LANES * MLA_HEADS
    rq, rkv = w_uq.shape[0], w_ukv.shape[0]
    return pl.pallas_call(
        _mla_up_kernel,
        grid=(seq // t,),
        in_specs=[pl.BlockSpec((t, rq), lambda i: (i, cq_cb)),
                  pl.BlockSpec((t, rkv), lambda i: (i, ckv_cb)),
                  pl.BlockSpec((t, LANES), lambda i: (i, kr_cb)),
                  pl.BlockSpec((t, 1), lambda i: (i, 0)),
                  pl.BlockSpec((1, LANES), lambda i: (0, 0)),
                  pl.BlockSpec((1, rq), lambda i: (0, 0)),
                  pl.BlockSpec((1, rkv), lambda i: (0, 0)),
                  pl.BlockSpec((rq, w), lambda i: (0, 0)),
                  pl.BlockSpec((rkv, w), lambda i: (0, 0))],
        out_specs=[pl.BlockSpec((t, w), lambda i: (i, 0)),
                   pl.BlockSpec((t, w), lambda i: (i, 0)),
                   pl.BlockSpec((t, MLA_WIDTH), lambda i: (i, 0))],
        out_shape=[jax.ShapeDtypeStruct((seq, w), BF16), jax.ShapeDtypeStruct((seq, w), BF16),
                   jax.ShapeDtypeStruct((seq, MLA_WIDTH), BF16)],
        compiler_params=_cparams("parallel"),
        name="mla_up",
    )(proj, proj, proj, pos_col, freq, g_cq.reshape(1, rq), g_ckv.reshape(1, rkv), w_uq, w_ukv)


def _t5_bucket(dist):
    n = jnp.maximum(dist, 0)
    max_exact = T5_BUCKETS // 2
    log_ratio = jnp.log(jnp.maximum(n, 1).astype(F32) / max_exact) / math.log(T5_MAX_DIST / max_exact)
    large = jnp.minimum(max_exact + (log_ratio * (T5_BUCKETS - max_exact)).astype(I32), T5_BUCKETS - 1)
    return jnp.where(n < max_exact, n, large)


T5_MASK_BUCKET = T5_BUCKETS


def _t5_lookup(table_row, bucket):
    rows, width = bucket.shape
    tab = jnp.broadcast_to(table_row, (rows, LANES))
    parts = [jnp.take_along_axis(tab, bucket[:, c:c + LANES], axis=1, mode="promise_in_bounds")
             for c in range(0, width, LANES)]
    return parts[0] if len(parts) == 1 else jnp.concatenate(parts, axis=1)


def _nsa_cmp_kernel(x_ref, pe_ref, w1_ref, w2_ref, o_ref, xf_ref, *, nc):
    half = NSA_CMP_LEN // 2
    d = NSA_HEAD_DIM
    xf_ref[...] = x_ref[...].astype(F32)
    u = jnp.zeros((nc, NSA_CMP_HIDDEN), F32)
    v = jnp.zeros((nc, NSA_CMP_HIDDEN), F32)
    for r in range(half):
        a = xf_ref[pl.ds(r, nc, stride=NSA_CMP_STRIDE), :]
        u = u + jnp.dot((a + pe_ref[r:r + 1, :]).astype(BF16), w1_ref[r * d:(r + 1) * d, :].astype(BF16),
                        preferred_element_type=F32)
        v = v + jnp.dot((a + pe_ref[half + r:half + r + 1, :]).astype(BF16),
                        w1_ref[(half + r) * d:(half + r + 1) * d, :].astype(BF16), preferred_element_type=F32)
    hid = u + pltpu.roll(v, nc - 1, 0)
    o_ref[...] = jnp.dot(jax.nn.gelu(hid).astype(BF16), w2_ref[...].astype(BF16),
                         preferred_element_type=F32).astype(o_ref.dtype)


def _nsa_compress(proj, k_cb, pe, w1, w2):
    seq = proj.shape[0]
    nc = seq // NSA_CMP_STRIDE
    d = NSA_HEAD_DIM
    g = NSA_KV_GROUPS
    return pl.pallas_call(
        functools.partial(_nsa_cmp_kernel, nc=nc),
        grid=(2, g),
        in_specs=[pl.BlockSpec((seq, d), lambda a, b: (0, k_cb + a * g + b)),
                  pl.BlockSpec((None, NSA_CMP_LEN, d), lambda a, b: (a, 0, 0)),
                  pl.BlockSpec((None, NSA_CMP_LEN * d, NSA_CMP_HIDDEN), lambda a, b: (a, 0, 0)),
                  pl.BlockSpec((None, NSA_CMP_HIDDEN, d), lambda a, b: (a, 0, 0))],
        out_specs=pl.BlockSpec((None, None, nc, d), lambda a, b: (a, b, 0, 0)),
        out_shape=jax.ShapeDtypeStruct((2, g, nc, d), BF16),
        scratch_shapes=[pltpu.VMEM((seq, d), F32)],
        compiler_params=_cparams("parallel", "arbitrary"),
        name="nsa_compress",
    )(proj, pe, w1, w2)


def _nsa_select_kernel(consec_ref, q_ref, kc_ref, vc_ref, posq_ref, posc_ref, tab_ref, gate_ref, ov_ref,
                       oc_ref, sel_ref, strip_ref, *, nc, n_slc, tq):
    qi = pl.program_id(0)
    d = NSA_HEAD_DIM
    c = d ** -0.5 * LOG2E
    per_block = tq // NSA_CMP_STRIDE
    lane = lax.broadcasted_iota(I32, (1, tq), 1)
    tok = qi * tq + lane

    def select(bias):
        gates = jax.nn.sigmoid(gate_ref[...])
        ovt = ov_ref[...]
        js = lax.broadcasted_iota(I32, (LANES, tq), 0)
        jf = js.astype(F32)
        cur = tok // NSA_SLC_BLOCK
        forced = (js == 0) | (js == cur) | (js == cur - 1)
        for g in range(NSA_KV_GROUPS):
            kc = kc_ref[g]
            vct = vc_ref[g].astype(F32).T.astype(BF16)
            psum = jnp.zeros((nc, tq), F32)
            for r in range(NSA_REP):
                h = g * NSA_REP + r
                qt = (q_ref[:, h * d:(h + 1) * d].astype(F32) * c).T.astype(BF16)
                t = jnp.dot(kc, qt, preferred_element_type=F32) + bias(h)
                m = jnp.max(t, axis=0, keepdims=True)
                e = jnp.exp2(t - m)
                l = jnp.sum(e, axis=0, keepdims=True)
                p = e * jnp.where(m > 0.5 * NEG_INF, 1.0 / l, 0.0)
                o = jnp.dot(vct, p.astype(BF16), preferred_element_type=F32)
                oc_ref[:, h * d:(h + 1) * d] = gates[:, 3 * h:3 * h + 1] * o.T
                psum = psum + p
            p_hi = psum.astype(BF16)
            p_lo = (psum - p_hi.astype(F32)).astype(BF16)
            imp = jnp.dot(ovt, p_hi, preferred_element_type=F32) + jnp.dot(ovt, p_lo, preferred_element_type=F32)
            st = jnp.where(forced, FORCE_SCORE, jnp.where(js > cur, -1.0, imp))
            st = jnp.where(js < n_slc, st, -2.0)
            sel = jnp.zeros((LANES, tq), F32)
            for _ in range(NSA_SLC_TOPK):
                mx = jnp.max(st, axis=0, keepdims=True)
                first = jnp.min(jnp.where(st == mx, jf, float(LANES)), axis=0, keepdims=True)
                hit = jf == first
                sel = jnp.where(hit, 1.0, sel)
                st = jnp.where(hit, -3e38, st)
            sel_ref[g] = sel.astype(sel_ref.dtype)

    consecutive = consec_ref[0] == 1

    @pl.when(jnp.logical_and(consecutive, qi == 0))
    def _():
        for chunk in range(2 * nc // LANES):
            rel = chunk * LANES - nc + lax.broadcasted_iota(I32, (LANES, 1), 0)
            dist = lane - (rel * NSA_CMP_STRIDE + (NSA_CMP_LEN - 1))
            bucket = jnp.where(dist >= 0, _t5_bucket(dist), T5_MASK_BUCKET)
            for h in range(NSA_HEADS):
                strip_ref[h, chunk * LANES:(chunk + 1) * LANES, :] = _t5_lookup(tab_ref[h:h + 1, :], bucket)

    @pl.when(consecutive)
    def _():
        start = pl.multiple_of(nc - per_block * qi, per_block)
        select(lambda h: strip_ref[h, pl.ds(start, nc), :])

    @pl.when(jnp.logical_not(consecutive))
    def _():
        cmp_end = lax.broadcasted_iota(I32, (nc, 1), 0) * NSA_CMP_STRIDE + (NSA_CMP_LEN - 1)
        pos_c = jnp.concatenate([posc_ref[...]] * (tq // LANES), axis=1)
        bucket = jnp.where(cmp_end <= tok, _t5_bucket(posq_ref[...] - pos_c), T5_MASK_BUCKET)
        select(lambda h: _t5_lookup(tab_ref[h:h + 1, :], bucket))


def _nsa_select(proj_b, q_cb, proj_f, gate_cb, kc, vc, pos_row, pos_cmp_rep, consec, tab_t, ov_t, *, n_slc, tq):
    seq = proj_b.shape[0]
    nc = kc.shape[1]
    g = NSA_KV_GROUPS
    d = NSA_HEAD_DIM
    grid_spec = pltpu.PrefetchScalarGridSpec(
        num_scalar_prefetch=1,
        grid=(seq // tq,),
        in_specs=[pl.BlockSpec((tq, NSA_WIDTH), lambda i, *_: (i, q_cb)),
                  pl.BlockSpec((g, nc, d), lambda i, *_: (0, 0, 0)),
                  pl.BlockSpec((g, nc, d), lambda i, *_: (0, 0, 0)),
                  pl.BlockSpec((1, tq), lambda i, *_: (0, i)),
                  pl.BlockSpec((nc, LANES), lambda i, *_: (0, 0)),
                  pl.BlockSpec((SUBLANES, LANES), lambda i, *_: (0, 0)),
                  pl.BlockSpec((tq, LANES), lambda i, *_: (i, gate_cb)),
                  pl.BlockSpec((LANES, nc), lambda i, *_: (0, 0))],
        out_specs=[pl.BlockSpec((tq, NSA_WIDTH), lambda i, *_: (i, 0)),
                   pl.BlockSpec((g, LANES, tq), lambda i, *_: (0, 0, i))],
        scratch_shapes=[pltpu.VMEM((NSA_HEADS, 2 * nc, tq), F32)])
    return pl.pallas_call(
        functools.partial(_nsa_select_kernel, nc=nc, n_slc=n_slc, tq=tq),
        grid_spec=grid_spec,
        out_shape=[jax.ShapeDtypeStruct((seq, NSA_WIDTH), F32),
                   jax.ShapeDtypeStruct((g, LANES, seq), F32)],
        compiler_params=_cparams("arbitrary"),
        name="nsa_select",
    )(consec, proj_b, kc, vc, pos_row, pos_cmp_rep, tab_t * LOG2E, proj_f, ov_t)


def _nsa_slc_kernel(pqmin_ref, pkmax_ref, consec_ref, q_ref, k_ref, v_ref, sel_ref, posq_ref, posk_ref, tab_ref,
                    o_ref, vt_ref, qt_ref, s_ref, m_ref, l_ref, acc_ref, cache_ref, *, tq, tk):
    g = pl.program_id(0)
    qi = pl.program_id(1)
    d = NSA_HEAD_DIM
    scale = d ** -0.5
    seq = k_ref.shape[0]
    consecutive = consec_ref[0] == 1
    n_cached = cache_ref.shape[0]

    @pl.when(qi == 0)
    def _():
        _build_vt(v_ref, vt_ref, 0, d, seq, tk)

    @pl.when(jnp.logical_and(consecutive, qi == 0))
    def _():
        rel = lax.broadcasted_iota(I32, (1, tq), 1) - lax.broadcasted_iota(I32, (tk, 1), 0)
        for v in range(n_cached):
            dist = rel + v * tq
            bucket = jnp.where(dist >= 0, _t5_bucket(dist), T5_MASK_BUCKET)
            for r in range(NSA_REP):
                cache_ref[v, r] = _t5_lookup(tab_ref[pl.ds(g * NSA_REP + r, 1), :], bucket)

    for r in range(NSA_REP):
        qt_ref[r] = (q_ref[:, r * d:(r + 1) * d].astype(F32) * (scale * LOG2E)).T.astype(BF16)
    m_ref[...] = jnp.full(m_ref.shape, NEG_INF, F32)
    l_ref[...] = jnp.zeros(l_ref.shape, F32)
    acc_ref[...] = jnp.zeros(acc_ref.shape, F32)
    pos_q = posq_ref[...]
    per_tile = tk // NSA_SLC_BLOCK
    n_full = (qi * tq) // tk

    def scores(j, slot):
        start = pl.multiple_of(j * tk, tk)
        k = k_ref[pl.ds(start, tk), :]
        for r in range(NSA_REP):
            s_ref[slot, r] = jnp.dot(k, qt_ref[r], preferred_element_type=F32)

    def softmax_pv(j, slot, masked, mode):
        start = pl.multiple_of(j * tk, tk)
        picked = jnp.concatenate(
            [jnp.broadcast_to(sel_ref[pl.ds(j * per_tile + b, 1), :], (NSA_SLC_BLOCK, tq)) for b in range(per_tile)],
            axis=0)
        if masked and mode == "gather":
            key = start + lax.broadcasted_iota(I32, (tk, tq), 0)
            qry = qi * tq + lax.broadcasted_iota(I32, (tk, tq), 1)
            picked = jnp.where(key <= qry, picked, 0.0)
        mask = picked > 0.5
        if mode == "gather":
            pos_k = jnp.concatenate([posk_ref[pl.ds(start, tk), :]] * (tq // LANES), axis=1)
            bucket = jnp.where(mask, _t5_bucket(pos_q - pos_k), T5_MASK_BUCKET)
        for r in range(NSA_REP):
            h = g * NSA_REP + r
            t = s_ref[slot, r]
            if mode == "gather":
                t = t + _t5_lookup(tab_ref[pl.ds(h, 1), :], bucket)
            elif mode == "cached":
                t = jnp.where(mask, t + cache_ref[(qi * tq - start) // tq, r], NEG_INF)
            else:
                t = jnp.where(mask, t, NEG_INF)
            m_prev = m_ref[r]
            mx = jnp.max(t, axis=0, keepdims=True)
            if mode == "far":
                b = tab_ref[pl.ds(h, 1), T5_BUCKETS - 1:T5_BUCKETS]
                m_new = jnp.maximum(m_prev, mx + b)
                shift = m_new - b
            else:
                m_new = jnp.maximum(m_prev, mx)
                shift = m_new
            alpha = jnp.exp2(m_prev - m_new)
            p = jnp.exp2(t - shift)
            l_ref[r] = alpha * l_ref[r] + jnp.sum(p, axis=0, keepdims=True)
            acc_ref[r] = alpha * acc_ref[r] + jnp.dot(vt_ref[0, :, pl.ds(start, tk)], p.astype(BF16),
                                                       preferred_element_type=F32)
            m_ref[r] = m_new

    def is_far(j):
        return pqmin_ref[qi] - pkmax_ref[j] >= T5_MAX_DIST

    def stages(j, count, mode):
        for k in range(count):
            scores(j + k + 1, (k + 1) % 2)
            softmax_pv(j + k, k % 2, False, mode)

    def near(fn):
        @pl.when(consecutive)
        def _():
            fn("cached")

        @pl.when(jnp.logical_not(consecutive))
        def _():
            fn("gather")

    def far_or_near(far, fn):
        @pl.when(far)
        def _():
            fn("far")

        @pl.when(jnp.logical_not(far))
        def _():
            near(fn)

    scores(0, 0)

    def pair(jj, carry):
        j = 2 * jj
        far_or_near(jnp.logical_and(is_far(j), is_far(j + 1)), lambda mode: stages(j, 2, mode))
        return carry

    pairs = n_full // 2
    lax.fori_loop(0, pairs, pair, 0)
    last = 2 * pairs

    @pl.when(n_full % 2 == 1)
    def _():
        far_or_near(is_far(last), lambda mode: stages(last, 1, mode))
        near(lambda mode: softmax_pv(last + 1, 1, True, mode))

    @pl.when(n_full % 2 == 0)
    def _():
        near(lambda mode: softmax_pv(last, 0, True, mode))

    for r in range(NSA_REP):
        o_ref[:, r * d:(r + 1) * d] = (acc_ref[r] / l_ref[r]).T


def _nsa_slc(proj_b, q_off, k_off, v_off, sel, pos_row, pos_rep, tab_t, pq_min, pk_max, consec, *, tq, tk):
    seq = proj_b.shape[0]
    tab_t = tab_t * LOG2E
    d = NSA_HEAD_DIM
    gw = NSA_REP * d
    once = pl.Buffered(1)
    n_cached = (T5_MAX_DIST + tk - 2) // tq + 1 + tk // tq
    grid_spec = pltpu.PrefetchScalarGridSpec(
        num_scalar_prefetch=3,
        grid=(NSA_KV_GROUPS, seq // tq),
        in_specs=[pl.BlockSpec((tq, gw), lambda g, i, *_: (i, q_off // gw + g)),
                  pl.BlockSpec((seq, d), lambda g, i, *_: (0, k_off // d + g), pipeline_mode=once),
                  pl.BlockSpec((seq, d), lambda g, i, *_: (0, v_off // d + g), pipeline_mode=once),
                  pl.BlockSpec((None, LANES, tq), lambda g, i, *_: (g, 0, i)),
                  pl.BlockSpec((1, tq), lambda g, i, *_: (0, i)),
                  pl.BlockSpec((seq, LANES), lambda g, i, *_: (0, 0), pipeline_mode=once),
                  pl.BlockSpec((SUBLANES, LANES), lambda g, i, *_: (0, 0))],
        out_specs=pl.BlockSpec((tq, gw), lambda g, i, *_: (i, g)),
        scratch_shapes=[pltpu.VMEM((1, d, seq), BF16), pltpu.VMEM((NSA_REP, d, tq), BF16),
                        pltpu.VMEM((2, NSA_REP, tk, tq), F32), pltpu.VMEM((NSA_REP, 1, tq), F32),
                        pltpu.VMEM((NSA_REP, 1, tq), F32), pltpu.VMEM((NSA_REP, d, tq), F32),
                        pltpu.VMEM((n_cached, NSA_REP, tk, tq), F32)])
    return pl.pallas_call(
        functools.partial(_nsa_slc_kernel, tq=tq, tk=tk),
        grid_spec=grid_spec,
        out_shape=jax.ShapeDtypeStruct((seq, NSA_WIDTH), F32),
        compiler_params=_cparams("parallel", "arbitrary"),
        name="nsa_slc",
    )(pq_min, pk_max, consec, proj_b, proj_b, proj_b, sel, pos_row, pos_rep, tab_t)


def _nsa_win_kernel(*refs, tq, nt):
    consec_ref = refs[0]
    q_ref = refs[1]
    k_refs = refs[2:2 + nt]
    v_refs = refs[2 + nt:2 + 2 * nt]
    pk_refs = refs[2 + 2 * nt:2 + 3 * nt]
    posq_ref, tab_ref, gate_ref, gout_ref, oc_ref, os_ref, o_ref, bias_ref = refs[2 + 3 * nt:]
    qi = pl.program_id(0)
    d = NSA_HEAD_DIM
    c = d ** -0.5 * LOG2E
    sub = lax.broadcasted_iota(I32, (tq, 1), 0)
    lane = lax.broadcasted_iota(I32, (1, tq), 1)

    def band_bucket(kidx, tok, dist):
        lower = jnp.maximum(tok - (NSA_WINDOW - 1), 0)
        b = jnp.where(kidx >= lower, _t5_bucket(dist), T5_MASK_BUCKET)
        return jnp.where(kidx <= tok, b, T5_MASK_BUCKET)

    def attend(bias):
        gates = jax.nn.sigmoid(gate_ref[...])
        for g in range(NSA_KV_GROUPS):
            ks = [kr[:, g * d:(g + 1) * d] for kr in k_refs]
            vts = [vr[:, g * d:(g + 1) * d].astype(F32).T.astype(BF16) for vr in v_refs]
            for r in range(NSA_REP):
                h = g * NSA_REP + r
                hs = slice(h * d, (h + 1) * d)
                qt = (q_ref[:, hs].astype(F32) * c).T.astype(BF16)
                ts = [jnp.dot(ks[jj], qt, preferred_element_type=F32) + bias(h, jj) for jj in range(nt)]
                m = functools.reduce(jnp.maximum, [jnp.max(t, axis=0, keepdims=True) for t in ts])
                ps = [jnp.exp2(t - m) for t in ts]
                l = functools.reduce(jnp.add, [jnp.sum(p, axis=0, keepdims=True) for p in ps])
                o_t = functools.reduce(jnp.add, [jnp.dot(vt, p.astype(BF16), preferred_element_type=F32)
                                                 for vt, p in zip(vts, ps)])
                o_w = (o_t / l).T
                o = (oc_ref[:, hs] + gates[:, 3 * h + 1:3 * h + 2] * os_ref[:, hs]
                     + gates[:, 3 * h + 2:3 * h + 3] * o_w)
                o_ref[:, hs] = (o * _silu(gout_ref[:, hs])).astype(o_ref.dtype)

    consecutive = consec_ref[0] == 1

    @pl.when(jnp.logical_and(consecutive, qi == 0))
    def _():
        tok0 = (nt - 1) * tq + lane
        for jj in range(nt):
            kidx0 = jj * tq + sub
            bucket = band_bucket(kidx0, tok0, tok0 - kidx0)
            for h in range(NSA_HEADS):
                bias_ref[h, jj] = _t5_lookup(tab_ref[h:h + 1, :], bucket)

    cached = jnp.logical_and(consecutive, qi >= nt - 1)

    @pl.when(cached)
    def _():
        attend(lambda h, jj: bias_ref[h, jj])

    @pl.when(jnp.logical_not(cached))
    def _():
        tok = qi * tq + lane
        pos_q = posq_ref[...]
        buckets = []
        for jj in range(nt):
            kidx = (qi - (nt - 1) + jj) * tq + sub
            pos_k = jnp.concatenate([pk_refs[jj][...]] * (tq // LANES), axis=1)
            buckets.append(band_bucket(kidx, tok, pos_q - pos_k))
        attend(lambda h, jj: _t5_lookup(tab_ref[h:h + 1, :], buckets[jj]))


def _nsa_win(proj_b, q_cb, k_cb, v_cb, proj_f, gate_cb, gout_cb, oc, o_s, pos_row, pos_rep, consec, tab_t, *, tq):
    seq = proj_b.shape[0]
    nt = NSA_WINDOW // tq + 1

    def band_rows(jj, cb):
        return pl.BlockSpec((tq, NSA_KV), lambda i, *_: (jnp.maximum(i - (nt - 1) + jj, 0), cb))

    def band_pos(jj):
        return pl.BlockSpec((tq, LANES), lambda i, *_: (jnp.maximum(i - (nt - 1) + jj, 0), 0))

    in_specs = [pl.BlockSpec((tq, NSA_WIDTH), lambda i, *_: (i, q_cb))]
    in_specs += [band_rows(jj, k_cb) for jj in range(nt)]
    in_specs += [band_rows(jj, v_cb) for jj in range(nt)]
    in_specs += [band_pos(jj) for jj in range(nt)]
    in_specs += [pl.BlockSpec((1, tq), lambda i, *_: (0, i)),
                 pl.BlockSpec((SUBLANES, LANES), lambda i, *_: (0, 0)),
                 pl.BlockSpec((tq, LANES), lambda i, *_: (i, gate_cb)),
                 pl.BlockSpec((tq, NSA_WIDTH), lambda i, *_: (i, gout_cb)),
                 pl.BlockSpec((tq, NSA_WIDTH), lambda i, *_: (i, 0)),
                 pl.BlockSpec((tq, NSA_WIDTH), lambda i, *_: (i, 0))]
    args = [proj_b] * (1 + 2 * nt) + [pos_rep] * nt + [pos_row, tab_t * LOG2E, proj_f, proj_f, oc, o_s]
    grid_spec = pltpu.PrefetchScalarGridSpec(
        num_scalar_prefetch=1,
        grid=(seq // tq,),
        in_specs=in_specs,
        out_specs=pl.BlockSpec((tq, NSA_WIDTH), lambda i, *_: (i, 0)),
        scratch_shapes=[pltpu.VMEM((NSA_HEADS, nt, tq, tq), F32)])
    return pl.pallas_call(
        functools.partial(_nsa_win_kernel, tq=tq, nt=nt),
        grid_spec=grid_spec,
        out_shape=jax.ShapeDtypeStruct((seq, NSA_WIDTH), BF16),
        compiler_params=_cparams("arbitrary"),
        name="nsa_win",
    )(consec, *args)


def _in_proj(h, norm_g, w_b, w_f, tiles, name):
    proj_b = _norm_matmul(h, norm_g, w_b, tm=tiles["tm"], tn=tiles["tn_b"], name=name + "_b", out_dtype=BF16)
    proj_f = _norm_matmul(h, norm_g, w_f, tm=tiles["tm"], tn=tiles["tn_f"], name=name + "_f")
    return proj_b, proj_f


def _even_layer(h, mem_kv, norm_g, w_in, s5, w_glu, b_f, tiles):
    pb, pf = _in_proj(h, norm_g, *w_in, tiles, "in_proj_even")
    ob, of = EVEN_B_OFF, EVEN_F_OFF
    b_cat, c_cat, tab, d_skip = s5
    z = _s5_scan(pf, of["u"] // LANES, b_cat, c_cat, d_skip, tab, tc=tiles["s5_tc"])
    y_s5 = _s5_glu(z, w_glu, pf, of["g_s5"] // S5_WIDTH, tm=tiles["tm"], tn=tiles["tn_glu"])
    decay = _decay(pf, of["f"] // LANES, b_f, t=tiles["decay_t"])
    d = FOX_HEAD_DIM
    y_fox = _flash(pb, ob["q"], pb, ob["k"], pb, ob["v"], pf, of["g_fox"], heads=FOX_HEADS, dk=d, dv=d,
                   scale=d ** -0.5, tq=tiles["attn_tq"], tk=tiles["attn_tk"], hp=tiles["attn_hp"], decay=decay,
                   name="fox_attn")
    y_mem = _mem_attn(pb, ob["q_mem"] // MEM_HEAD_DIM, pf, of["g_mem"] // MEM_HEAD_DIM, mem_kv, t=tiles["mem_t"])
    return y_s5, y_fox, y_mem


def _odd_layer(h, mem_kv, norm_g, w_in, mla, nsa, pos, tiles):
    pb, pf = _in_proj(h, norm_g, *w_in, tiles, "in_proj_odd")
    ob, of = ODD_B_OFF, ODD_F_OFF
    g_cq, g_ckv, w_uq, w_ukv, freq = mla
    pos_col_f, pos_row, pos_cmp_rep, pos_rep, pq_min, pk_max, consec = pos
    q_r, k_r, v_r = _mla_up(pf, of["c_q"] // MLA_Q_RANK, of["c_kv"] // MLA_KV_RANK, of["k_rope"] // LANES,
                            g_cq, g_ckv, w_uq, w_ukv, pos_col_f, freq, t=tiles["prep_t"])
    y_mla = _flash(q_r, 0, k_r, 0, v_r, 0, pf, of["g_mla"], heads=MLA_HEADS, dk=2 * LANES, dv=MLA_V,
                   scale=(MLA_NOPE + MLA_ROPE) ** -0.5, tq=tiles["attn_tq"], tk=tiles["attn_tk"], hp=tiles["attn_hp"],
                   name="mla_attn")
    pe, w1, w2, tab_t, ov, n_slc = nsa
    kvc = _nsa_compress(pb, ob["k_cmp"] // NSA_HEAD_DIM, pe, w1, w2)
    oc, sel = _nsa_select(pb, ob["q_nsa"] // NSA_WIDTH, pf, of["gates"] // LANES, kvc[0], kvc[1], pos_row,
                          pos_cmp_rep, consec, tab_t, ov, n_slc=n_slc, tq=tiles["nsa_tq"])
    o_s = _nsa_slc(pb, ob["q_nsa"], ob["k_slc"], ob["v_slc"], sel, pos_row, pos_rep, tab_t, pq_min, pk_max, consec,
                   tq=tiles["slc_tq"], tk=tiles["slc_tk"])
    y_nsa = _nsa_win(pb, ob["q_nsa"] // NSA_WIDTH, ob["k_win"] // NSA_KV, ob["v_win"] // NSA_KV,
                     pf, of["gates"] // LANES, of["g_nsa"] // NSA_WIDTH, oc, o_s, pos_row, pos_rep, consec, tab_t,
                     tq=tiles["win_tq"])
    y_mem = _mem_attn(pb, ob["q_mem"] // MEM_HEAD_DIM, pf, of["g_mem"] // MEM_HEAD_DIM, mem_kv, t=tiles["mem_t"])
    return y_mla, y_nsa, y_mem


def _tiles(seq):
    return {"tm": min(seq, 1024), "tn_b": 512, "tn_f": 768, "tn_out": 512, "tn_glu": 512, "tn_mem": 512,
            "s5_tc": min(seq, 1024), "decay_t": min(seq, 512), "attn_tq": min(seq, 512),
            "attn_tk": min(seq, 512), "attn_hp": 2, "slc_tq": 256, "slc_tk": 512, "nsa_tq": 256, "win_tq": 128,
            "mem_t": min(seq, 1024), "norm_t": min(seq, 512), "prep_t": min(seq, 256)}


def _context(positions, t5_table, seq, tiles):
    pos = positions[0]
    pos_col = pos.reshape(seq, 1)
    pos_row = pos.reshape(1, seq)
    pos_rep = jnp.broadcast_to(pos_col, (seq, LANES))
    pq_min = jnp.min(pos.reshape(seq // tiles["slc_tq"], tiles["slc_tq"]), axis=1)
    pk_max = jnp.max(pos.reshape(seq // tiles["slc_tk"], tiles["slc_tk"]), axis=1)
    nc = seq // NSA_CMP_STRIDE
    pos_cmp = jnp.pad(pos[NSA_CMP_LEN - 1::NSA_CMP_STRIDE], (0, 1))
    pos_cmp_rep = jnp.broadcast_to(pos_cmp.reshape(nc, 1), (nc, LANES))
    half = MLA_ROPE // 2
    inv_freq = ROPE_THETA ** (-jnp.arange(half, dtype=F32) / half)
    freq = jnp.concatenate([inv_freq, inv_freq, jnp.zeros((LANES - MLA_ROPE,), F32)]).reshape(1, LANES)
    tab_t = jnp.pad(t5_table.astype(F32).T, ((0, SUBLANES - NSA_HEADS), (0, LANES - T5_BUCKETS)))
    tab_t = tab_t.at[:, T5_MASK_BUCKET].set(NEG_INF)
    n_slc = seq // NSA_SLC_BLOCK
    cs = np.arange(nc) * NSA_CMP_STRIDE
    ss = np.arange(LANES) * NSA_SLC_BLOCK
    ov_np = np.clip(np.minimum(cs[:, None] + NSA_CMP_LEN, ss[None, :] + NSA_SLC_BLOCK)
                    - np.maximum(cs[:, None], ss[None, :]), 0, None) / NSA_CMP_LEN
    ov_np[nc - 1, :] = 0.0
    ov_np[:, n_slc:] = 0.0
    consec = jnp.all(pos[1:] - pos[:-1] == 1).astype(I32).reshape(1)
    return {"pos": (pos_col.astype(F32), pos_row, pos_cmp_rep, pos_rep, pq_min, pk_max, consec), "freq": freq,
            "tab_t": tab_t, "ov": jnp.asarray(ov_np.T, BF16), "n_slc": n_slc}


def _odd_params(i, mla_g_cq, mla_g_ckv, mla_w_uq, mla_w_ukv, nsa_cmp_pe, nsa_cmp_w1, nsa_cmp_w2, ctx):
    dq = MLA_NOPE + MLA_ROPE
    w_uq = mla_w_uq[i].reshape(MLA_Q_RANK, MLA_HEADS, dq)
    w_uq = jnp.pad(w_uq, ((0, 0), (0, 0), (0, 2 * LANES - dq))).reshape(MLA_Q_RANK, -1).astype(BF16)
    mla = (mla_g_cq[i], mla_g_ckv[i], w_uq, mla_w_ukv[i].astype(BF16), ctx["freq"])
    nsa = (nsa_cmp_pe[i].astype(F32), nsa_cmp_w1[i], nsa_cmp_w2[i], ctx["tab_t"],
           ctx["ov"], ctx["n_slc"])
    return mla, nsa


def kernel(x, mem, positions, norm_g, mem_norm_g, final_norm_g, t5_table, w_out, mem_w_kv, even_w_in, s5_lam_re,
           s5_lam_im, s5_log_dt, s5_b_re, s5_b_im, s5_c_re, s5_c_im, s5_d, s5_w_glu, fox_b_f, odd_w_in, mla_g_cq,
           mla_g_ckv, mla_w_uq, mla_w_ukv, nsa_cmp_pe, nsa_cmp_w1, nsa_cmp_w2):
    batch, seq, _ = x.shape
    assert batch == 1 and seq % 1024 == 0 and seq // NSA_SLC_BLOCK <= LANES
    depth = norm_g.shape[0]
    tiles = _tiles(seq)
    ctx = _context(positions, t5_table, seq, tiles)
    h = x[0]
    mem_kv_all = _mem_kv_all(mem[0], mem_norm_g, mem_w_kv, tn=tiles["tn_mem"])
    for layer in range(depth):
        i = layer // 2
        mem_kv = (mem_kv_all, layer)
        if layer % 2 == 0:
            w_in = (_reorder_w_in(even_w_in[i], EVEN_SPLITS, EVEN_B_ORDER),
                    _reorder_w_in(even_w_in[i], EVEN_SPLITS, EVEN_F_ORDER))
            b_cat, c_cat, tab = _s5_prepare(s5_lam_re[i], s5_lam_im[i], s5_log_dt[i], s5_b_re[i], s5_b_im[i],
                                            s5_c_re[i], s5_c_im[i])
            ys = _even_layer(h, mem_kv, norm_g[layer], w_in, (b_cat, c_cat, tab, s5_d[i]),
                             s5_w_glu[i], fox_b_f[i], tiles)
        else:
            w_in = (_reorder_w_in(odd_w_in[i], ODD_SPLITS, ODD_B_ORDER),
                    _reorder_w_in(odd_w_in[i], ODD_SPLITS, ODD_F_ORDER))
            mla, nsa = _odd_params(i, mla_g_cq, mla_g_ckv, mla_w_uq, mla_w_ukv, nsa_cmp_pe, nsa_cmp_w1, nsa_cmp_w2,
                                   ctx)
            ys = _odd_layer(h, mem_kv, norm_g[layer], w_in, mla, nsa, ctx["pos"], tiles)
        h = _out_proj(h, ys, w_out, layer, tm=tiles["tm"], tn=tiles["tn_out"])
    return _final_norm(h, final_norm_g, tm=tiles["norm_t"])[None]
```

```python
import functools
import math

import numpy as np
import jax
import jax.numpy as jnp
from jax import lax
from jax.experimental import pallas as pl
from jax.experimental.pallas import tpu as pltpu

F32 = jnp.float32
BF16 = jnp.bfloat16
I32 = jnp.int32

RMS_EPS = 1e-6
NEG_INF = -1e30
LOG2E = math.log2(math.e)

S5_WIDTH = 1024
S5_GROUP = 16
S5_GROUPS = S5_WIDTH // S5_GROUP
S5_STATE = 64
FOX_HEADS = 8
FOX_HEAD_DIM = 128
FOX_WIDTH = FOX_HEADS * FOX_HEAD_DIM
MEM_HEADS = 4
MEM_HEAD_DIM = 128
MEM_WIDTH = MEM_HEADS * MEM_HEAD_DIM
MLA_HEADS = 8
MLA_Q_RANK = 512
MLA_KV_RANK = 512
MLA_NOPE = 128
MLA_ROPE = 64
MLA_V = 128
MLA_WIDTH = MLA_HEADS * MLA_V
ROPE_THETA = 10000.0
NSA_HEADS = 8
NSA_KV_GROUPS = 2
NSA_REP = NSA_HEADS // NSA_KV_GROUPS
NSA_HEAD_DIM = 128
NSA_WIDTH = NSA_HEADS * NSA_HEAD_DIM
NSA_KV = NSA_KV_GROUPS * NSA_HEAD_DIM
NSA_CMP_LEN = 32
NSA_CMP_STRIDE = 16
NSA_CMP_HIDDEN = 256
NSA_SLC_BLOCK = 64
NSA_SLC_TOPK = 16
NSA_WINDOW = 512
FORCE_SCORE = 1e6
T5_BUCKETS = 32
T5_MAX_DIST = 1024

EVEN_SPLITS = (S5_WIDTH, S5_WIDTH, FOX_WIDTH, FOX_WIDTH, FOX_WIDTH, FOX_HEADS, FOX_WIDTH, MEM_WIDTH, MEM_WIDTH)
ODD_SPLITS = (MLA_Q_RANK, MLA_KV_RANK, MLA_ROPE, MLA_WIDTH, NSA_WIDTH, NSA_KV, NSA_KV, NSA_KV, NSA_KV, NSA_KV,
              NSA_KV, 3 * NSA_HEADS, NSA_WIDTH, MEM_WIDTH, MEM_WIDTH)

LANES = 128
SUBLANES = 8
MXU_COLS = 256
VMEM_LIMIT_BYTES = 56 * 1024 * 1024

EVEN_B_ORDER = (("q", 2, 1024), ("k", 3, 1024), ("v", 4, 1024), ("q_mem", 7, 512))
EVEN_F_ORDER = (("u", 0, 1024), ("g_s5", 1, 1024), ("g_fox", 6, 1024), ("g_mem", 8, 512), ("f", 5, 128),
                ("pad", None, 128))
ODD_B_ORDER = (("q_nsa", 4, 1024), ("q_mem", 13, 512), ("k_cmp", 5, 256), ("v_cmp", 6, 256), ("k_slc", 7, 256),
               ("v_slc", 8, 256), ("k_win", 9, 256), ("v_win", 10, 256))
ODD_F_ORDER = (("g_nsa", 12, 1024), ("g_mla", 3, 1024), ("c_q", 0, 512), ("c_kv", 1, 512), ("g_mem", 14, 512),
               ("k_rope", 2, 128), ("gates", 11, 128))


def _layout(order):
    off, out = 0, {}
    for name, _, width in order:
        assert off % width == 0
        out[name] = off
        off += width
    return out, off


EVEN_B_OFF, EVEN_B_N = _layout(EVEN_B_ORDER)
EVEN_F_OFF, EVEN_F_N = _layout(EVEN_F_ORDER)
ODD_B_OFF, ODD_B_N = _layout(ODD_B_ORDER)
ODD_F_OFF, ODD_F_N = _layout(ODD_F_ORDER)


def _reorder_w_in(w, splits, order):
    starts = np.concatenate([[0], np.cumsum(splits)])
    cols = []
    for _, idx, width in order:
        if idx is None:
            cols.append(jnp.zeros((w.shape[0], width), w.dtype))
            continue
        seg = w[:, int(starts[idx]):int(starts[idx + 1])]
        pad = width - seg.shape[1]
        if pad:
            seg = jnp.pad(seg, ((0, 0), (0, pad)))
        cols.append(seg)
    return jnp.concatenate(cols, axis=1).astype(BF16)


def _cparams(*sem):
    return pltpu.CompilerParams(dimension_semantics=sem, vmem_limit_bytes=VMEM_LIMIT_BYTES)


def _silu(g):
    return g * jax.nn.sigmoid(g)


def _norm_matmul_kernel(x_ref, g_ref, w_ref, o_ref, xn_ref):
    @pl.when(pl.program_id(1) == 0)
    def _():
        x = x_ref[...]
        ms = jnp.mean(x * x, axis=-1, keepdims=True)
        xn_ref[...] = (x * lax.rsqrt(ms + RMS_EPS) * g_ref[...]).astype(BF16)

    o_ref[...] = jnp.dot(xn_ref[...], w_ref[...].astype(BF16), preferred_element_type=F32).astype(o_ref.dtype)


def _norm_matmul(x, g, w, *, x_cb=0, tm, tn, name, out_dtype=F32):
    m = x.shape[0]
    k, n = w.shape
    return pl.pallas_call(
        _norm_matmul_kernel,
        grid=(m // tm, n // tn),
        in_specs=[pl.BlockSpec((tm, k), lambda i, j: (i, x_cb)),
                  pl.BlockSpec((1, k), lambda i, j: (0, 0)),
                  pl.BlockSpec((k, tn), lambda i, j: (0, j))],
        out_specs=pl.BlockSpec((tm, tn), lambda i, j: (i, j)),
        out_shape=jax.ShapeDtypeStruct((m, n), out_dtype),
        scratch_shapes=[pltpu.VMEM((tm, k), BF16)],
        compiler_params=_cparams("parallel", "arbitrary"),
        name=name,
    )(x, g.reshape(1, k), w)


def _out_proj_kernel(h_ref, *refs):
    o_ref = refs[-1]
    n = (len(refs) - 1) // 2
    acc = h_ref[...]
    for y_ref, w_ref in zip(refs[:n], refs[n:2 * n]):
        acc = acc + jnp.dot(y_ref[...], w_ref[...].astype(BF16), preferred_element_type=F32)
    o_ref[...] = acc


def _out_proj(h, ys, w_all, layer, *, tm, tn):
    m, n = h.shape
    in_specs = [pl.BlockSpec((tm, tn), lambda i, j: (i, j))]
    in_specs += [pl.BlockSpec((tm, y.shape[1]), lambda i, j: (i, 0)) for y in ys]
    row = 0
    for y in ys:
        width = y.shape[1]
        assert row % width == 0
        in_specs.append(pl.BlockSpec((None, width, tn), lambda i, j, rb=row // width: (layer, rb, j)))
        row += width
    return pl.pallas_call(
        _out_proj_kernel,
        grid=(m // tm, n // tn),
        in_specs=in_specs,
        out_specs=pl.BlockSpec((tm, tn), lambda i, j: (i, j)),
        out_shape=jax.ShapeDtypeStruct((m, n), F32),
        compiler_params=_cparams("parallel", "arbitrary"),
        name="out_proj",
    )(h, *ys, *([w_all] * len(ys)))


def _final_norm_kernel(x_ref, g_ref, o_ref):
    x = x_ref[...]
    ms = jnp.mean(x * x, axis=-1, keepdims=True)
    o_ref[...] = x * lax.rsqrt(ms + RMS_EPS) * g_ref[...]


def _final_norm(h, g, *, tm):
    m, n = h.shape
    return pl.pallas_call(
        _final_norm_kernel,
        grid=(m // tm,),
        in_specs=[pl.BlockSpec((tm, n), lambda i: (i, 0)), pl.BlockSpec((1, n), lambda i: (0, 0))],
        out_specs=pl.BlockSpec((tm, n), lambda i: (i, 0)),
        out_shape=jax.ShapeDtypeStruct((m, n), F32),
        compiler_params=_cparams("parallel"),
        name="final_norm",
    )(h, g.reshape(1, n))


ONES_ROWS = 16


def _build_vt(v_ref, vt_ref, hh, dv, seq, chunk):
    def body(c, carry):
        st = pl.multiple_of(c * chunk, chunk)
        vt_ref[hh, 0:dv, pl.ds(st, chunk)] = v_ref[pl.ds(st, chunk), hh * dv:(hh + 1) * dv].astype(F32).T.astype(BF16)
        return carry

    lax.fori_loop(0, seq // chunk, body, 0)
    extra = vt_ref.shape[1] - dv
    if extra:
        vt_ref[hh, dv:dv + extra, :] = jnp.ones((extra, seq), BF16)


def _flash_kernel(*refs, scale, tq, tk, hp, dk, dv, has_decay):
    cw = min(tq, MXU_COLS)
    ncg = tq // cw
    streams = [(hh, cg) for hh in range(hp) for cg in range(ncg)]
    if has_decay:
        q_ref, k_ref, v_ref, g_ref, cq_ref, ck_ref, o_ref, vt_ref, qt_ref, s_ref, m_ref, acc_ref = refs
    else:
        q_ref, k_ref, v_ref, g_ref, o_ref, vt_ref, qt_ref, s_ref, m_ref, acc_ref = refs
    hb = pl.program_id(0)
    qi = pl.program_id(1)
    seq = k_ref.shape[0]

    @pl.when(qi == 0)
    def _():
        for hh in range(hp):
            _build_vt(v_ref, vt_ref, hh, dv, seq, tk)

    for v, (hh, cg) in enumerate(streams):
        qt_ref[v] = (q_ref[cg * cw:(cg + 1) * cw, hh * dk:(hh + 1) * dk].astype(F32) * (scale * LOG2E)).T.astype(BF16)
    m_ref[...] = jnp.full(m_ref.shape, NEG_INF, F32)
    acc_ref[...] = jnp.zeros(acc_ref.shape, F32)
    n_full = (qi * tq) // tk
    if has_decay:
        cq2 = [cq_ref[pl.ds(hb * hp + hh, 1), cg * cw:(cg + 1) * cw] for hh, cg in streams]

    def scores(j, slot):
        start = pl.multiple_of(j * tk, tk)
        for v, (hh, cg) in enumerate(streams):
            s_ref[slot, v] = jnp.dot(k_ref[pl.ds(start, tk), hh * dk:(hh + 1) * dk], qt_ref[v],
                                     preferred_element_type=F32)

    def softmax_pv(j, slot, masked):
        start = pl.multiple_of(j * tk, tk)
        for v, (hh, cg) in enumerate(streams):
            t = s_ref[slot, v]
            if has_decay:
                t = t - jnp.concatenate([ck_ref[hh, pl.ds(start, tk), :]] * (cw // LANES), axis=1)
            if masked:
                key = start + lax.broadcasted_iota(I32, (tk, cw), 0)
                qry = qi * tq + cg * cw + lax.broadcasted_iota(I32, (tk, cw), 1)
                t = jnp.where(key <= qry, t, NEG_INF)
            m_prev = m_ref[v]
            mx = jnp.max(t, axis=0, keepdims=True)
            if has_decay:
                m_new = jnp.maximum(m_prev, mx + cq2[v])
                shift = m_new - cq2[v]
            else:
                m_new = jnp.maximum(m_prev, mx)
                shift = m_new
            alpha = jnp.exp2(m_prev - m_new)
            p = jnp.exp2(t - shift).astype(BF16)
            acc_ref[v] = alpha * acc_ref[v] + jnp.dot(vt_ref[hh, :, pl.ds(start, tk)], p,
                                                       preferred_element_type=F32)
            m_ref[v] = m_new

    scores(0, 0)

    def pair(j):
        scores(j + 1, 1)
        softmax_pv(j, 0, False)
        scores(j + 2, 0)
        softmax_pv(j + 1, 1, False)

    def quad(qq, carry):
        pair(4 * qq)
        pair(4 * qq + 2)
        return carry

    quads = n_full // 4
    lax.fori_loop(0, quads, quad, 0)
    rem = n_full - 4 * quads

    @pl.when(rem >= 2)
    def _():
        pair(4 * quads)

    last = 4 * quads + 2 * (rem // 2)

    @pl.when(n_full % 2 == 1)
    def _():
        scores(last + 1, 1)
        softmax_pv(last, 0, False)
        softmax_pv(last + 1, 1, True)

    @pl.when(n_full % 2 == 0)
    def _():
        softmax_pv(last, 0, True)

    for v, (hh, cg) in enumerate(streams):
        a = acc_ref[v]
        o = (a[0:dv, :] / a[dv:dv + 1, :]).T
        rows = slice(cg * cw, (cg + 1) * cw)
        cols = slice(hh * dv, (hh + 1) * dv)
        o_ref[rows, cols] = (o * _silu(g_ref[rows, cols])).astype(o_ref.dtype)


def _flash(q_arr, q_off, k_arr, k_off, v_arr, v_off, g_arr, g_off, *, heads, dk, dv, scale, tq, tk, hp,
           decay=None, name):
    seq = q_arr.shape[0]
    cw = min(tq, MXU_COLS)
    ns = hp * (tq // cw)
    once = pl.Buffered(1)
    in_specs = [pl.BlockSpec((tq, hp * dk), lambda h, i: (i, q_off // (hp * dk) + h)),
                pl.BlockSpec((seq, hp * dk), lambda h, i: (0, k_off // (hp * dk) + h)),
                pl.BlockSpec((seq, hp * dv), lambda h, i: (0, v_off // (hp * dv) + h)),
                pl.BlockSpec((tq, hp * dv), lambda h, i: (i, g_off // (hp * dv) + h))]
    args = [q_arr, k_arr, v_arr, g_arr]
    if decay is not None:
        cum_t, cum_rep = decay
        in_specs += [pl.BlockSpec((SUBLANES, tq), lambda h, i: (0, i)),
                     pl.BlockSpec((hp, seq, LANES), lambda h, i: (h, 0, 0), pipeline_mode=once)]
        args += [cum_t, cum_rep]
    return pl.pallas_call(
        functools.partial(_flash_kernel, scale=scale, tq=tq, tk=tk, hp=hp, dk=dk, dv=dv, has_decay=decay is not None),
        grid=(heads // hp, seq // tq),
        in_specs=in_specs,
        out_specs=pl.BlockSpec((tq, hp * dv), lambda h, i: (i, h)),
        out_shape=jax.ShapeDtypeStruct((seq, heads * dv), BF16),
        scratch_shapes=[pltpu.VMEM((hp, dv + ONES_ROWS, seq), BF16), pltpu.VMEM((ns, dk, cw), BF16),
                        pltpu.VMEM((2, ns, tk, cw), F32), pltpu.VMEM((ns, 1, cw), F32),
                        pltpu.VMEM((ns, dv + ONES_ROWS, cw), F32)],
        compiler_params=_cparams("parallel", "arbitrary"),
        name=name,
    )(*args)


def _decay_kernel(f_ref, b_ref, ct_ref, cr_ref, carry_ref, *, t):
    i = pl.program_id(0)

    @pl.when(i == 0)
    def _():
        carry_ref[...] = jnp.zeros(carry_ref.shape, F32)

    x = f_ref[...] + b_ref[...]
    lf = jnp.minimum(x, 0.0) - jnp.log1p(jnp.exp(-jnp.abs(x)))
    row = lax.broadcasted_iota(I32, lf.shape, 0)
    s = 1
    while s < t:
        lf = lf + jnp.where(row >= s, pltpu.roll(lf, s, 0), 0.0)
        s *= 2
    lf = lf + carry_ref[...]
    carry_ref[...] = lf[t - 1:t, :]
    lf2 = lf * LOG2E
    ct_ref[...] = lf2.T[:FOX_HEADS, :]
    for h in range(FOX_HEADS):
        cr_ref[h] = jnp.broadcast_to(lf2[:, h:h + 1], (t, LANES))


def _decay(proj, f_cb, b_f, *, t):
    seq = proj.shape[0]
    b = jnp.pad(b_f.reshape(1, FOX_HEADS), ((0, 0), (0, LANES - FOX_HEADS)))
    return pl.pallas_call(
        functools.partial(_decay_kernel, t=t),
        grid=(seq // t,),
        in_specs=[pl.BlockSpec((t, LANES), lambda i: (i, f_cb)), pl.BlockSpec((1, LANES), lambda i: (0, 0))],
        out_specs=[pl.BlockSpec((FOX_HEADS, t), lambda i: (0, i)),
                   pl.BlockSpec((FOX_HEADS, t, LANES), lambda i: (0, i, 0))],
        out_shape=[jax.ShapeDtypeStruct((FOX_HEADS, seq), F32), jax.ShapeDtypeStruct((FOX_HEADS, seq, LANES), F32)],
        scratch_shapes=[pltpu.VMEM((1, LANES), F32)],
        compiler_params=_cparams("arbitrary"),
        name="fox_decay",
    )(proj, b)


def _mem_attn_kernel(q_ref, kv_ref, g_ref, o_ref):
    d = MEM_HEAD_DIM
    for h in range(MEM_HEADS):
        hs = slice(h * d, (h + 1) * d)
        k = kv_ref[:, hs].astype(BF16)
        v = kv_ref[:, MEM_WIDTH + h * d:MEM_WIDTH + (h + 1) * d].astype(BF16)
        s = lax.dot_general(q_ref[:, hs], k, (((1,), (1,)), ((), ())), preferred_element_type=F32) * (d ** -0.5)
        m = jnp.max(s, axis=1, keepdims=True)
        p = jnp.exp(s - m)
        l = jnp.sum(p, axis=1, keepdims=True)
        o = jnp.dot(p.astype(BF16), v, preferred_element_type=F32) / l
        o_ref[:, hs] = (o * _silu(g_ref[:, hs])).astype(o_ref.dtype)


def _mem_kv_all(mem2d, g, w_all, *, tn):
    depth, k, n = w_all.shape
    m = mem2d.shape[0]
    return pl.pallas_call(
        _norm_matmul_kernel,
        grid=(depth, n // tn),
        in_specs=[pl.BlockSpec((m, k), lambda l, j: (0, 0)),
                  pl.BlockSpec((1, k), lambda l, j: (0, 0)),
                  pl.BlockSpec((None, k, tn), lambda l, j: (l, 0, j))],
        out_specs=pl.BlockSpec((None, m, tn), lambda l, j: (l, 0, j)),
        out_shape=jax.ShapeDtypeStruct((depth, m, n), F32),
        scratch_shapes=[pltpu.VMEM((m, k), BF16)],
        compiler_params=_cparams("arbitrary", "arbitrary"),
        name="mem_kv",
    )(mem2d, g.reshape(1, k), w_all)


def _mem_attn(proj_b, q_cb, proj_f, g_cb, mem_kv, *, t):
    seq = proj_b.shape[0]
    mem_kv, layer = mem_kv
    nm = mem_kv.shape[1]
    w = MEM_WIDTH
    d = MEM_HEAD_DIM
    return pl.pallas_call(
        _mem_attn_kernel,
        grid=(seq // t,),
        in_specs=[pl.BlockSpec((t, w), lambda i: (i, q_cb * d // w)),
                  pl.BlockSpec((None, nm, 2 * w), lambda i: (layer, 0, 0)),
                  pl.BlockSpec((t, w), lambda i: (i, g_cb * d // w))],
        out_specs=pl.BlockSpec((t, w), lambda i: (i, 0)),
        out_shape=jax.ShapeDtypeStruct((seq, w), BF16),
        compiler_params=_cparams("parallel"),
        name="mem_attn",
    )(proj_b, mem_kv, proj_f)


S5_TILE_GROUPS = LANES // S5_GROUP
S5_TILE_STATES = S5_TILE_GROUPS * S5_STATE
S5_TILES = S5_GROUPS // S5_TILE_GROUPS


def _s5_scan_kernel(u_ref, b_ref, c_ref, d_ref, tab_ref, z_ref, bu_ref, carry_ref, *, tc):
    ns = S5_TILE_STATES

    @pl.when(pl.program_id(1) == 0)
    def _():
        carry_ref[...] = jnp.zeros(carry_ref.shape, F32)

    u = u_ref[...]
    bu_ref[...] = jnp.dot(u.astype(BF16), b_ref[...], preferred_element_type=F32)
    steps = [(1, tab_ref[0], tab_ref[1]), (2, tab_ref[2], tab_ref[3]), (4, tab_ref[4], tab_ref[5])]
    pr = tab_ref[6]
    pi = tab_ref[7]

    def body(i, carry):
        cr, ci = carry
        r0 = pl.multiple_of(i * SUBLANES, SUBLANES)
        xr = bu_ref[pl.ds(r0, SUBLANES), 0:ns]
        xi = bu_ref[pl.ds(r0, SUBLANES), ns:2 * ns]
        for s, ar, ai in steps:
            sr = pltpu.roll(xr, s, 0)
            si = pltpu.roll(xi, s, 0)
            xr, xi = xr + ar * sr - ai * si, xi + ar * si + ai * sr
        xr, xi = xr + pr * cr - pi * ci, xi + pr * ci + pi * cr
        bu_ref[pl.ds(r0, SUBLANES), 0:ns] = xr
        bu_ref[pl.ds(r0, SUBLANES), ns:2 * ns] = xi
        return xr[SUBLANES - 1:SUBLANES, :], xi[SUBLANES - 1:SUBLANES, :]

    cr, ci = lax.fori_loop(0, tc // SUBLANES, body, (carry_ref[0:1, 0:ns], carry_ref[0:1, ns:2 * ns]))
    carry_ref[0:1, 0:ns] = cr
    carry_ref[0:1, ns:2 * ns] = ci
    y = jnp.dot(bu_ref[...].astype(BF16), c_ref[...], preferred_element_type=F32) + d_ref[...] * u
    z_ref[...] = jax.nn.gelu(y)


def _s5_prepare(lam_re, lam_im, log_dt, b_re, b_im, c_re, c_im):
    dt = jnp.exp(log_dt.astype(F32))[:, None]
    lr = lam_re.astype(F32)
    li = lam_im.astype(F32)
    mag = jnp.exp(lr * dt)
    ab_re = mag * jnp.cos(li * dt)
    ab_im = mag * jnp.sin(li * dt)
    den = lr * lr + li * li
    nr = ab_re - 1.0
    f_re = (nr * lr + ab_im * li) / den
    f_im = (ab_im * lr - nr * li) / den
    br = b_re.astype(F32)
    bim = b_im.astype(F32)
    bb_re = f_re[..., None] * br - f_im[..., None] * bim
    bb_im = f_re[..., None] * bim + f_im[..., None] * br
    eye = jnp.eye(S5_TILE_GROUPS, dtype=F32)

    def blockdiag_in(bb):
        t = bb.reshape(S5_TILES, S5_TILE_GROUPS, S5_STATE, S5_GROUP)
        m = jnp.einsum("jgpc,gh->jgchp", t, eye)
        return m.reshape(S5_TILES, LANES, S5_TILE_STATES)

    def blockdiag_out(cc):
        t = cc.reshape(S5_TILES, S5_TILE_GROUPS, S5_GROUP, S5_STATE)
        m = jnp.einsum("jgcp,gh->jgphc", t, eye)
        return m.reshape(S5_TILES, S5_TILE_STATES, LANES)

    b_cat = jnp.concatenate([blockdiag_in(bb_re), blockdiag_in(bb_im)], axis=2).astype(BF16)
    c_cat = jnp.concatenate([blockdiag_out(c_re.astype(F32)), -blockdiag_out(c_im.astype(F32))], axis=1).astype(BF16)

    a_r = ab_re.reshape(S5_TILES, 1, S5_TILE_STATES)
    a_i = ab_im.reshape(S5_TILES, 1, S5_TILE_STATES)

    def cmul(xr, xi, yr, yi):
        return xr * yr - xi * yi, xr * yi + xi * yr

    a2 = cmul(a_r, a_i, a_r, a_i)
    a4 = cmul(*a2, *a2)
    row = jnp.arange(SUBLANES)[None, :, None]
    tabs = []
    for s, (pr_, pi_) in ((1, (a_r, a_i)), (2, a2), (4, a4)):
        tabs.append(jnp.where(row >= s, pr_, 0.0))
        tabs.append(jnp.where(row >= s, pi_, 0.0))
    pw = [(a_r, a_i)]
    for _ in range(SUBLANES - 1):
        pw.append(cmul(*pw[-1], a_r, a_i))
    tabs.append(jnp.concatenate([p[0] for p in pw], axis=1))
    tabs.append(jnp.concatenate([p[1] for p in pw], axis=1))
    tab = jnp.stack([jnp.broadcast_to(t, (S5_TILES, SUBLANES, S5_TILE_STATES)) for t in tabs], axis=1)
    return b_cat, c_cat, tab.astype(F32)


def _s5_scan(proj, u_cb, b_cat, c_cat, d_skip, tab, *, tc):
    seq = proj.shape[0]
    ns = S5_TILE_STATES
    d = d_skip.astype(F32).reshape(S5_TILES, 1, LANES)
    return pl.pallas_call(
        functools.partial(_s5_scan_kernel, tc=tc),
        grid=(S5_TILES, seq // tc),
        in_specs=[pl.BlockSpec((tc, LANES), lambda j, c: (c, u_cb + j)),
                  pl.BlockSpec((None, LANES, 2 * ns), lambda j, c: (j, 0, 0)),
                  pl.BlockSpec((None, 2 * ns, LANES), lambda j, c: (j, 0, 0)),
                  pl.BlockSpec((None, 1, LANES), lambda j, c: (j, 0, 0)),
                  pl.BlockSpec((None, 8, SUBLANES, ns), lambda j, c: (j, 0, 0, 0))],
        out_specs=pl.BlockSpec((tc, LANES), lambda j, c: (c, j)),
        out_shape=jax.ShapeDtypeStruct((seq, S5_WIDTH), F32),
        scratch_shapes=[pltpu.VMEM((tc, 2 * ns), F32), pltpu.VMEM((SUBLANES, 2 * ns), F32)],
        compiler_params=_cparams("parallel", "arbitrary"),
        name="s5_scan",
    )(proj, b_cat, c_cat, d, tab)


def _s5_glu_kernel(z_ref, w_ref, g_ref, o_ref, *, tn):
    j = pl.program_id(1)
    z = z_ref[...]
    a = jnp.dot(z.astype(BF16), w_ref[...].astype(BF16), preferred_element_type=F32)
    zc = z_ref[:, pl.ds(pl.multiple_of(j * tn, tn), tn)]
    o_ref[...] = (zc * jax.nn.sigmoid(a) * _silu(g_ref[...])).astype(o_ref.dtype)


def _s5_glu(z, w_glu, proj, g_cb, *, tm, tn):
    seq, n = z.shape
    return pl.pallas_call(
        functools.partial(_s5_glu_kernel, tn=tn),
        grid=(seq // tm, n // tn),
        in_specs=[pl.BlockSpec((tm, n), lambda i, j: (i, 0)),
                  pl.BlockSpec((n, tn), lambda i, j: (0, j)),
                  pl.BlockSpec((tm, tn), lambda i, j: (i, g_cb * (n // tn) + j))],
        out_specs=pl.BlockSpec((tm, tn), lambda i, j: (i, j)),
        out_shape=jax.ShapeDtypeStruct((seq, n), BF16),
        compiler_params=_cparams("parallel", "arbitrary"),
        name="s5_glu",
    )(z, w_glu, proj)


def _rope_tables(pos, freq):
    ang = pos * freq
    lane = lax.broadcasted_iota(I32, ang.shape, 1)
    half = MLA_ROPE // 2
    cos = jnp.cos(ang)
    sin = jnp.sin(ang)
    c = jnp.where(lane < MLA_ROPE, cos, 0.0)
    s1 = jnp.where(lane < half, -sin, 0.0)
    s2 = jnp.where((lane >= half) & (lane < MLA_ROPE), sin, 0.0)
    return c, s1, s2


def _rope_apply(x, c, s1, s2):
    half = MLA_ROPE // 2
    return x * c + pltpu.roll(x, LANES - half, 1) * s1 + pltpu.roll(x, half, 1) * s2


def _mla_up_kernel(cq_ref, ckv_ref, kr_ref, pos_ref, freq_ref, gq_ref, gkv_ref, wq_ref, wkv_ref, q_ref, k_ref, v_ref):
    def normed(x_ref, g_ref):
        x = x_ref[...]
        ms = jnp.mean(x * x, axis=-1, keepdims=True)
        return (x * lax.rsqrt(ms + RMS_EPS) * g_ref[...]).astype(BF16)

    qf = jnp.dot(normed(cq_ref, gq_ref), wq_ref[...], preferred_element_type=F32)
    kvf = jnp.dot(normed(ckv_ref, gkv_ref), wkv_ref[...], preferred_element_type=F32)
    c, s1, s2 = _rope_tables(pos_ref[...], freq_ref[...])
    kr = _rope_apply(kr_ref[...], c, s1, s2).astype(BF16)
    for h in range(MLA_HEADS):
        b = 2 * LANES * h
        q_ref[:, b:b + LANES] = qf[:, b:b + LANES].astype(BF16)
        q_ref[:, b + LANES:b + 2 * LANES] = _rope_apply(qf[:, b + LANES:b + 2 * LANES], c, s1, s2).astype(BF16)
        k_ref[:, b:b + LANES] = kvf[:, b:b + LANES].astype(BF16)
        k_ref[:, b + LANES:b + 2 * LANES] = kr
        v_ref[:, LANES * h:LANES * (h + 1)] = kvf[:, b + LANES:b + 2 * LANES].astype(BF16)


def _mla_up(proj, cq_cb, ckv_cb, kr_cb, g_cq, g_ckv, w_uq, w_ukv, pos_col, freq, *, t):
    seq = proj.shape[0]
    w = 2 * LANES * MLA_HEADS
    rq, rkv = w_uq.shape[0], w_ukv.shape[0]
    return pl.pallas_call(
        _mla_up_kernel,
        grid=(seq // t,),
        in_specs=[pl.BlockSpec((t, rq), lambda i: (i, cq_cb)),
                  pl.BlockSpec((t, rkv), lambda i: (i, ckv_cb)),
                  pl.BlockSpec((t, LANES), lambda i: (i, kr_cb)),
                  pl.BlockSpec((t, 1), lambda i: (i, 0)),
                  pl.BlockSpec((1, LANES), lambda i: (0, 0)),
                  pl.BlockSpec((1, rq), lambda i: (0, 0)),
                  pl.BlockSpec((1, rkv), lambda i: (0, 0)),
                  pl.BlockSpec((rq, w), lambda i: (0, 0)),
                  pl.BlockSpec((rkv, w), lambda i: (0, 0))],
        out_specs=[pl.BlockSpec((t, w), lambda i: (i, 0)),
                   pl.BlockSpec((t, w), lambda i: (i, 0)),
                   pl.BlockSpec((t, MLA_WIDTH), lambda i: (i, 0))],
        out_shape=[jax.ShapeDtypeStruct((seq, w), BF16), jax.ShapeDtypeStruct((seq, w), BF16),
                   jax.ShapeDtypeStruct((seq, MLA_WIDTH), BF16)],
        compiler_params=_cparams("parallel"),
        name="mla_up",
    )(proj, proj, proj, pos_col, freq, g_cq.reshape(1, rq), g_ckv.reshape(1, rkv), w_uq, w_ukv)


def _t5_bucket(dist):
    n = jnp.maximum(dist, 0)
    max_exact = T5_BUCKETS // 2
    log_ratio = jnp.log(jnp.maximum(n, 1).astype(F32) / max_exact) / math.log(T5_MAX_DIST / max_exact)
    large = jnp.minimum(max_exact + (log_ratio * (T5_BUCKETS - max_exact)).astype(I32), T5_BUCKETS - 1)
    return jnp.where(n < max_exact, n, large)


T5_MASK_BUCKET = T5_BUCKETS


def _t5_lookup(table_row, bucket):
    rows, width = bucket.shape
    tab = jnp.broadcast_to(table_row, (rows, LANES))
    parts = [jnp.take_along_axis(tab, bucket[:, c:c + LANES], axis=1, mode="promise_in_bounds")
             for c in range(0, width, LANES)]
    return parts[0] if len(parts) == 1 else jnp.concatenate(parts, axis=1)


def _nsa_cmp_kernel(x_ref, pe_ref, w1_ref, w2_ref, o_ref, xf_ref, *, nc):
    half = NSA_CMP_LEN // 2
    d = NSA_HEAD_DIM
    xf_ref[...] = x_ref[...].astype(F32)
    u = jnp.zeros((nc, NSA_CMP_HIDDEN), F32)
    v = jnp.zeros((nc, NSA_CMP_HIDDEN), F32)
    for r in range(half):
        a = xf_ref[pl.ds(r, nc, stride=NSA_CMP_STRIDE), :]
        u = u + jnp.dot((a + pe_ref[r:r + 1, :]).astype(BF16), w1_ref[r * d:(r + 1) * d, :].astype(BF16),
                        preferred_element_type=F32)
        v = v + jnp.dot((a + pe_ref[half + r:half + r + 1, :]).astype(BF16),
                        w1_ref[(half + r) * d:(half + r + 1) * d, :].astype(BF16), preferred_element_type=F32)
    hid = u + pltpu.roll(v, nc - 1, 0)
    o_ref[...] = jnp.dot(jax.nn.gelu(hid).astype(BF16), w2_ref[...].astype(BF16),
                         preferred_element_type=F32).astype(o_ref.dtype)


def _nsa_compress(proj, k_cb, pe, w1, w2):
    seq = proj.shape[0]
    nc = seq // NSA_CMP_STRIDE
    d = NSA_HEAD_DIM
    g = NSA_KV_GROUPS
    return pl.pallas_call(
        functools.partial(_nsa_cmp_kernel, nc=nc),
        grid=(2, g),
        in_specs=[pl.BlockSpec((seq, d), lambda a, b: (0, k_cb + a * g + b)),
                  pl.BlockSpec((None, NSA_CMP_LEN, d), lambda a, b: (a, 0, 0)),
                  pl.BlockSpec((None, NSA_CMP_LEN * d, NSA_CMP_HIDDEN), lambda a, b: (a, 0, 0)),
                  pl.BlockSpec((None, NSA_CMP_HIDDEN, d), lambda a, b: (a, 0, 0))],
        out_specs=pl.BlockSpec((None, None, nc, d), lambda a, b: (a, b, 0, 0)),
        out_shape=jax.ShapeDtypeStruct((2, g, nc, d), BF16),
        scratch_shapes=[pltpu.VMEM((seq, d), F32)],
        compiler_params=_cparams("parallel", "arbitrary"),
        name="nsa_compress",
    )(proj, pe, w1, w2)


def _nsa_select_kernel(consec_ref, q_ref, kc_ref, vc_ref, posq_ref, posc_ref, tab_ref, gate_ref, ov_ref,
                       oc_ref, sel_ref, strip_ref, *, nc, n_slc, tq):
    qi = pl.program_id(0)
    d = NSA_HEAD_DIM
    c = d ** -0.5 * LOG2E
    per_block = tq // NSA_CMP_STRIDE
    lane = lax.broadcasted_iota(I32, (1, tq), 1)
    tok = qi * tq + lane

    def select(bias):
        gates = jax.nn.sigmoid(gate_ref[...])
        ovt = ov_ref[...]
        js = lax.broadcasted_iota(I32, (LANES, tq), 0)
        jf = js.astype(F32)
        cur = tok // NSA_SLC_BLOCK
        forced = (js == 0) | (js == cur) | (js == cur - 1)
        for g in range(NSA_KV_GROUPS):
            kc = kc_ref[g]
            vct = vc_ref[g].astype(F32).T.astype(BF16)
            psum = jnp.zeros((nc, tq), F32)
            for r in range(NSA_REP):
                h = g * NSA_REP + r
                qt = (q_ref[:, h * d:(h + 1) * d].astype(F32) * c).T.astype(BF16)
                t = jnp.dot(kc, qt, preferred_element_type=F32) + bias(h)
                m = jnp.max(t, axis=0, keepdims=True)
                e = jnp.exp2(t - m)
                l = jnp.sum(e, axis=0, keepdims=True)
                p = e * jnp.where(m > 0.5 * NEG_INF, 1.0 / l, 0.0)
                o = jnp.dot(vct, p.astype(BF16), preferred_element_type=F32)
                oc_ref[:, h * d:(h + 1) * d] = gates[:, 3 * h:3 * h + 1] * o.T
                psum = psum + p
            p_hi = psum.astype(BF16)
            p_lo = (psum - p_hi.astype(F32)).astype(BF16)
            imp = jnp.dot(ovt, p_hi, preferred_element_type=F32) + jnp.dot(ovt, p_lo, preferred_element_type=F32)
            st = jnp.where(forced, FORCE_SCORE, jnp.where(js > cur, -1.0, imp))
            st = jnp.where(js < n_slc, st, -2.0)
            sel = jnp.zeros((LANES, tq), F32)
            for _ in range(NSA_SLC_TOPK):
                mx = jnp.max(st, axis=0, keepdims=True)
                first = jnp.min(jnp.where(st == mx, jf, float(LANES)), axis=0, keepdims=True)
                hit = jf == first
                sel = jnp.where(hit, 1.0, sel)
                st = jnp.where(hit, -3e38, st)
            sel_ref[g] = sel.astype(sel_ref.dtype)

    consecutive = consec_ref[0] == 1

    @pl.when(jnp.logical_and(consecutive, qi == 0))
    def _():
        for chunk in range(2 * nc // LANES):
            rel = chunk * LANES - nc + lax.broadcasted_iota(I32, (LANES, 1), 0)
            dist = lane - (rel * NSA_CMP_STRIDE + (NSA_CMP_LEN - 1))
            bucket = jnp.where(dist >= 0, _t5_bucket(dist), T5_MASK_BUCKET)
            for h in range(NSA_HEADS):
                strip_ref[h, chunk * LANES:(chunk + 1) * LANES, :] = _t5_lookup(tab_ref[h:h + 1, :], bucket)

    @pl.when(consecutive)
    def _():
        start = pl.multiple_of(nc - per_block * qi, per_block)
        select(lambda h: strip_ref[h, pl.ds(start, nc), :])

    @pl.when(jnp.logical_not(consecutive))
    def _():
        cmp_end = lax.broadcasted_iota(I32, (nc, 1), 0) * NSA_CMP_STRIDE + (NSA_CMP_LEN - 1)
        pos_c = jnp.concatenate([posc_ref[...]] * (tq // LANES), axis=1)
        bucket = jnp.where(cmp_end <= tok, _t5_bucket(posq_ref[...] - pos_c), T5_MASK_BUCKET)
        select(lambda h: _t5_lookup(tab_ref[h:h + 1, :], bucket))


def _nsa_select(proj_b, q_cb, proj_f, gate_cb, kc, vc, pos_row, pos_cmp_rep, consec, tab_t, ov_t, *, n_slc, tq):
    seq = proj_b.shape[0]
    nc = kc.shape[1]
    g = NSA_KV_GROUPS
    d = NSA_HEAD_DIM
    grid_spec = pltpu.PrefetchScalarGridSpec(
        num_scalar_prefetch=1,
        grid=(seq // tq,),
        in_specs=[pl.BlockSpec((tq, NSA_WIDTH), lambda i, *_: (i, q_cb)),
                  pl.BlockSpec((g, nc, d), lambda i, *_: (0, 0, 0)),
                  pl.BlockSpec((g, nc, d), lambda i, *_: (0, 0, 0)),
                  pl.BlockSpec((1, tq), lambda i, *_: (0, i)),
                  pl.BlockSpec((nc, LANES), lambda i, *_: (0, 0)),
                  pl.BlockSpec((SUBLANES, LANES), lambda i, *_: (0, 0)),
                  pl.BlockSpec((tq, LANES), lambda i, *_: (i, gate_cb)),
                  pl.BlockSpec((LANES, nc), lambda i, *_: (0, 0))],
        out_specs=[pl.BlockSpec((tq, NSA_WIDTH), lambda i, *_: (i, 0)),
                   pl.BlockSpec((g, LANES, tq), lambda i, *_: (0, 0, i))],
        scratch_shapes=[pltpu.VMEM((NSA_HEADS, 2 * nc, tq), F32)])
    return pl.pallas_call(
        functools.partial(_nsa_select_kernel, nc=nc, n_slc=n_slc, tq=tq),
        grid_spec=grid_spec,
        out_shape=[jax.ShapeDtypeStruct((seq, NSA_WIDTH), F32),
                   jax.ShapeDtypeStruct((g, LANES, seq), F32)],
        compiler_params=_cparams("arbitrary"),
        name="nsa_select",
    )(consec, proj_b, kc, vc, pos_row, pos_cmp_rep, tab_t * LOG2E, proj_f, ov_t)


def _nsa_slc_kernel(pqmin_ref, pkmax_ref, consec_ref, q_ref, k_ref, v_ref, sel_ref, posq_ref, posk_ref, tab_ref,
                    o_ref, vt_ref, qt_ref, s_ref, m_ref, l_ref, acc_ref, cache_ref, *, tq, tk):
    g = pl.program_id(0)
    qi = pl.program_id(1)
    d = NSA_HEAD_DIM
    scale = d ** -0.5
    seq = k_ref.shape[0]
    consecutive = consec_ref[0] == 1
    n_cached = cache_ref.shape[0]

    @pl.when(qi == 0)
    def _():
        _build_vt(v_ref, vt_ref, 0, d, seq, tk)

    @pl.when(jnp.logical_and(consecutive, qi == 0))
    def _():
        rel = lax.broadcasted_iota(I32, (1, tq), 1) - lax.broadcasted_iota(I32, (tk, 1), 0)
        for v in range(n_cached):
            dist = rel + v * tq
            bucket = jnp.where(dist >= 0, _t5_bucket(dist), T5_MASK_BUCKET)
            for r in range(NSA_REP):
                cache_ref[v, r] = _t5_lookup(tab_ref[pl.ds(g * NSA_REP + r, 1), :], bucket)

    for r in range(NSA_REP):
        qt_ref[r] = (q_ref[:, r * d:(r + 1) * d].astype(F32) * (scale * LOG2E)).T.astype(BF16)
    m_ref[...] = jnp.full(m_ref.shape, NEG_INF, F32)
    l_ref[...] = jnp.zeros(l_ref.shape, F32)
    acc_ref[...] = jnp.zeros(acc_ref.shape, F32)
    pos_q = posq_ref[...]
    per_tile = tk // NSA_SLC_BLOCK
    n_full = (qi * tq) // tk

    def scores(j, slot):
        start = pl.multiple_of(j * tk, tk)
        k = k_ref[pl.ds(start, tk), :]
        for r in range(NSA_REP):
            s_ref[slot, r] = jnp.dot(k, qt_ref[r], preferred_element_type=F32)

    def softmax_pv(j, slot, masked, mode):
        start = pl.multiple_of(j * tk, tk)
        picked = jnp.concatenate(
            [jnp.broadcast_to(sel_ref[pl.ds(j * per_tile + b, 1), :], (NSA_SLC_BLOCK, tq)) for b in range(per_tile)],
            axis=0)
        if masked and mode == "gather":
            key = start + lax.broadcasted_iota(I32, (tk, tq), 0)
            qry = qi * tq + lax.broadcasted_iota(I32, (tk, tq), 1)
            picked = jnp.where(key <= qry, picked, 0.0)
        mask = picked > 0.5
        if mode == "gather":
            pos_k = jnp.concatenate([posk_ref[pl.ds(start, tk), :]] * (tq // LANES), axis=1)
            bucket = jnp.where(mask, _t5_bucket(pos_q - pos_k), T5_MASK_BUCKET)
        for r in range(NSA_REP):
            h = g * NSA_REP + r
            t = s_ref[slot, r]
            if mode == "gather":
                t = t + _t5_lookup(tab_ref[pl.ds(h, 1), :], bucket)
            elif mode == "cached":
                t = jnp.where(mask, t + cache_ref[(qi * tq - start) // tq, r], NEG_INF)
            else:
                t = jnp.where(mask, t, NEG_INF)
            m_prev = m_ref[r]
            mx = jnp.max(t, axis=0, keepdims=True)
            if mode == "far":
                b = tab_ref[pl.ds(h, 1), T5_BUCKETS - 1:T5_BUCKETS]
                m_new = jnp.maximum(m_prev, mx + b)
                shift = m_new - b
            else:
                m_new = jnp.maximum(m_prev, mx)
                shift = m_new
            alpha = jnp.exp2(m_prev - m_new)
            p = jnp.exp2(t - shift)
            l_ref[r] = alpha * l_ref[r] + jnp.sum(p, axis=0, keepdims=True)
            acc_ref[r] = alpha * acc_ref[r] + jnp.dot(vt_ref[0, :, pl.ds(start, tk)], p.astype(BF16),
                                                       preferred_element_type=F32)
            m_ref[r] = m_new

    def is_far(j):
        return pqmin_ref[qi] - pkmax_ref[j] >= T5_MAX_DIST

    def stages(j, count, mode):
        for k in range(count):
            scores(j + k + 1, (k + 1) % 2)
            softmax_pv(j + k, k % 2, False, mode)

    def near(fn):
        @pl.when(consecutive)
        def _():
            fn("cached")

        @pl.when(jnp.logical_not(consecutive))
        def _():
            fn("gather")

    def far_or_near(far, fn):
        @pl.when(far)
        def _():
            fn("far")

        @pl.when(jnp.logical_not(far))
        def _():
            near(fn)

    scores(0, 0)

    def pair(jj, carry):
        j = 2 * jj
        far_or_near(jnp.logical_and(is_far(j), is_far(j + 1)), lambda mode: stages(j, 2, mode))
        return carry

    pairs = n_full // 2
    lax.fori_loop(0, pairs, pair, 0)
    last = 2 * pairs

    @pl.when(n_full % 2 == 1)
    def _():
        far_or_near(is_far(last), lambda mode: stages(last, 1, mode))
        near(lambda mode: softmax_pv(last + 1, 1, True, mode))

    @pl.when(n_full % 2 == 0)
    def _():
        near(lambda mode: softmax_pv(last, 0, True, mode))

    for r in range(NSA_REP):
        o_ref[:, r * d:(r + 1) * d] = (acc_ref[r] / l_ref[r]).T


def _nsa_slc(proj_b, q_off, k_off, v_off, sel, pos_row, pos_rep, tab_t, pq_min, pk_max, consec, *, tq, tk):
    seq = proj_b.shape[0]
    tab_t = tab_t * LOG2E
    d = NSA_HEAD_DIM
    gw = NSA_REP * d
    once = pl.Buffered(1)
    n_cached = (T5_MAX_DIST + tk - 2) // tq + 1 + tk // tq
    grid_spec = pltpu.PrefetchScalarGridSpec(
        num_scalar_prefetch=3,
        grid=(NSA_KV_GROUPS, seq // tq),
        in_specs=[pl.BlockSpec((tq, gw), lambda g, i, *_: (i, q_off // gw + g)),
                  pl.BlockSpec((seq, d), lambda g, i, *_: (0, k_off // d + g)),
                  pl.BlockSpec((seq, d), lambda g, i, *_: (0, v_off // d + g)),
                  pl.BlockSpec((None, LANES, tq), lambda g, i, *_: (g, 0, i)),
                  pl.BlockSpec((1, tq), lambda g, i, *_: (0, i)),
                  pl.BlockSpec((seq, LANES), lambda g, i, *_: (0, 0), pipeline_mode=once),
                  pl.BlockSpec((SUBLANES, LANES), lambda g, i, *_: (0, 0))],
        out_specs=pl.BlockSpec((tq, gw), lambda g, i, *_: (i, g)),
        scratch_shapes=[pltpu.VMEM((1, d, seq), BF16), pltpu.VMEM((NSA_REP, d, tq), BF16),
                        pltpu.VMEM((2, NSA_REP, tk, tq), F32), pltpu.VMEM((NSA_REP, 1, tq), F32),
                        pltpu.VMEM((NSA_REP, 1, tq), F32), pltpu.VMEM((NSA_REP, d, tq), F32),
                        pltpu.VMEM((n_cached, NSA_REP, tk, tq), F32)])
    return pl.pallas_call(
        functools.partial(_nsa_slc_kernel, tq=tq, tk=tk),
        grid_spec=grid_spec,
        out_shape=jax.ShapeDtypeStruct((seq, NSA_WIDTH), F32),
        compiler_params=_cparams("parallel", "arbitrary"),
        name="nsa_slc",
    )(pq_min, pk_max, consec, proj_b, proj_b, proj_b, sel, pos_row, pos_rep, tab_t)


def _nsa_win_kernel(*refs, tq, nt):
    consec_ref = refs[0]
    q_ref = refs[1]
    k_refs = refs[2:2 + nt]
    v_refs = refs[2 + nt:2 + 2 * nt]
    pk_refs = refs[2 + 2 * nt:2 + 3 * nt]
    posq_ref, tab_ref, gate_ref, gout_ref, oc_ref, os_ref, o_ref, bias_ref = refs[2 + 3 * nt:]
    qi = pl.program_id(0)
    d = NSA_HEAD_DIM
    c = d ** -0.5 * LOG2E
    sub = lax.broadcasted_iota(I32, (tq, 1), 0)
    lane = lax.broadcasted_iota(I32, (1, tq), 1)

    def band_bucket(kidx, tok, dist):
        lower = jnp.maximum(tok - (NSA_WINDOW - 1), 0)
        b = jnp.where(kidx >= lower, _t5_bucket(dist), T5_MASK_BUCKET)
        return jnp.where(kidx <= tok, b, T5_MASK_BUCKET)

    def attend(bias):
        gates = jax.nn.sigmoid(gate_ref[...])
        for g in range(NSA_KV_GROUPS):
            ks = [kr[:, g * d:(g + 1) * d] for kr in k_refs]
            vts = [vr[:, g * d:(g + 1) * d].astype(F32).T.astype(BF16) for vr in v_refs]
            for r in range(NSA_REP):
                h = g * NSA_REP + r
                hs = slice(h * d, (h + 1) * d)
                qt = (q_ref[:, hs].astype(F32) * c).T.astype(BF16)
                ts = [jnp.dot(ks[jj], qt, preferred_element_type=F32) + bias(h, jj) for jj in range(nt)]
                m = functools.reduce(jnp.maximum, [jnp.max(t, axis=0, keepdims=True) for t in ts])
                ps = [jnp.exp2(t - m) for t in ts]
                l = functools.reduce(jnp.add, [jnp.sum(p, axis=0, keepdims=True) for p in ps])
                o_t = functools.reduce(jnp.add, [jnp.dot(vt, p.astype(BF16), preferred_element_type=F32)
                                                 for vt, p in zip(vts, ps)])
                o_w = (o_t / l).T
                o = (oc_ref[:, hs] + gates[:, 3 * h + 1:3 * h + 2] * os_ref[:, hs]
                     + gates[:, 3 * h + 2:3 * h + 3] * o_w)
                o_ref[:, hs] = (o * _silu(gout_ref[:, hs])).astype(o_ref.dtype)

    consecutive = consec_ref[0] == 1

    @pl.when(jnp.logical_and(consecutive, qi == 0))
    def _():
        tok0 = (nt - 1) * tq + lane
        for jj in range(nt):
            kidx0 = jj * tq + sub
            bucket = band_bucket(kidx0, tok0, tok0 - kidx0)
            for h in range(NSA_HEADS):
                bias_ref[h, jj] = _t5_lookup(tab_ref[h:h + 1, :], bucket)

    cached = jnp.logical_and(consecutive, qi >= nt - 1)

    @pl.when(cached)
    def _():
        attend(lambda h, jj: bias_ref[h, jj])

    @pl.when(jnp.logical_not(cached))
    def _():
        tok = qi * tq + lane
        pos_q = posq_ref[...]
        buckets = []
        for jj in range(nt):
            kidx = (qi - (nt - 1) + jj) * tq + sub
            pos_k = jnp.concatenate([pk_refs[jj][...]] * (tq // LANES), axis=1)
            buckets.append(band_bucket(kidx, tok, pos_q - pos_k))
        attend(lambda h, jj: _t5_lookup(tab_ref[h:h + 1, :], buckets[jj]))


def _nsa_win(proj_b, q_cb, k_cb, v_cb, proj_f, gate_cb, gout_cb, oc, o_s, pos_row, pos_rep, consec, tab_t, *, tq):
    seq = proj_b.shape[0]
    nt = NSA_WINDOW // tq + 1

    def band_rows(jj, cb):
        return pl.BlockSpec((tq, NSA_KV), lambda i, *_: (jnp.maximum(i - (nt - 1) + jj, 0), cb))

    def band_pos(jj):
        return pl.BlockSpec((tq, LANES), lambda i, *_: (jnp.maximum(i - (nt - 1) + jj, 0), 0))

    in_specs = [pl.BlockSpec((tq, NSA_WIDTH), lambda i, *_: (i, q_cb))]
    in_specs += [band_rows(jj, k_cb) for jj in range(nt)]
    in_specs += [band_rows(jj, v_cb) for jj in range(nt)]
    in_specs += [band_pos(jj) for jj in range(nt)]
    in_specs += [pl.BlockSpec((1, tq), lambda i, *_: (0, i)),
                 pl.BlockSpec((SUBLANES, LANES), lambda i, *_: (0, 0)),
                 pl.BlockSpec((tq, LANES), lambda i, *_: (i, gate_cb)),
                 pl.BlockSpec((tq, NSA_WIDTH), lambda i, *_: (i, gout_cb)),
                 pl.BlockSpec((tq, NSA_WIDTH), lambda i, *_: (i, 0)),
                 pl.BlockSpec((tq, NSA_WIDTH), lambda i, *_: (i, 0))]
    args = [proj_b] * (1 + 2 * nt) + [pos_rep] * nt + [pos_row, tab_t * LOG2E, proj_f, proj_f, oc, o_s]
    grid_spec = pltpu.PrefetchScalarGridSpec(
        num_scalar_prefetch=1,
        grid=(seq // tq,),
        in_specs=in_specs,
        out_specs=pl.BlockSpec((tq, NSA_WIDTH), lambda i, *_: (i, 0)),
        scratch_shapes=[pltpu.VMEM((NSA_HEADS, nt, tq, tq), F32)])
    return pl.pallas_call(
        functools.partial(_nsa_win_kernel, tq=tq, nt=nt),
        grid_spec=grid_spec,
        out_shape=jax.ShapeDtypeStruct((seq, NSA_WIDTH), BF16),
        compiler_params=_cparams("arbitrary"),
        name="nsa_win",
    )(consec, *args)


def _in_proj(h, norm_g, w_b, w_f, tiles, name):
    proj_b = _norm_matmul(h, norm_g, w_b, tm=tiles["tm"], tn=tiles["tn_b"], name=name + "_b", out_dtype=BF16)
    proj_f = _norm_matmul(h, norm_g, w_f, tm=tiles["tm"], tn=tiles["tn_f"], name=name + "_f")
    return proj_b, proj_f


def _even_layer(h, mem_kv, norm_g, w_in, s5, w_glu, b_f, tiles):
    pb, pf = _in_proj(h, norm_g, *w_in, tiles, "in_proj_even")
    ob, of = EVEN_B_OFF, EVEN_F_OFF
    b_cat, c_cat, tab, d_skip = s5
    z = _s5_scan(pf, of["u"] // LANES, b_cat, c_cat, d_skip, tab, tc=tiles["s5_tc"])
    y_s5 = _s5_glu(z, w_glu, pf, of["g_s5"] // S5_WIDTH, tm=tiles["tm"], tn=tiles["tn_glu"])
    decay = _decay(pf, of["f"] // LANES, b_f, t=tiles["decay_t"])
    d = FOX_HEAD_DIM
    y_fox = _flash(pb, ob["q"], pb, ob["k"], pb, ob["v"], pf, of["g_fox"], heads=FOX_HEADS, dk=d, dv=d,
                   scale=d ** -0.5, tq=tiles["attn_tq"], tk=tiles["attn_tk"], hp=tiles["attn_hp"], decay=decay,
                   name="fox_attn")
    y_mem = _mem_attn(pb, ob["q_mem"] // MEM_HEAD_DIM, pf, of["g_mem"] // MEM_HEAD_DIM, mem_kv, t=tiles["mem_t"])
    return y_s5, y_fox, y_mem


def _odd_layer(h, mem_kv, norm_g, w_in, mla, nsa, pos, tiles):
    pb, pf = _in_proj(h, norm_g, *w_in, tiles, "in_proj_odd")
    ob, of = ODD_B_OFF, ODD_F_OFF
    g_cq, g_ckv, w_uq, w_ukv, freq = mla
    pos_col_f, pos_row, pos_cmp_rep, pos_rep, pq_min, pk_max, consec = pos
    q_r, k_r, v_r = _mla_up(pf, of["c_q"] // MLA_Q_RANK, of["c_kv"] // MLA_KV_RANK, of["k_rope"] // LANES,
                            g_cq, g_ckv, w_uq, w_ukv, pos_col_f, freq, t=tiles["prep_t"])
    y_mla = _flash(q_r, 0, k_r, 0, v_r, 0, pf, of["g_mla"], heads=MLA_HEADS, dk=2 * LANES, dv=MLA_V,
                   scale=(MLA_NOPE + MLA_ROPE) ** -0.5, tq=tiles["attn_tq"], tk=tiles["attn_tk"], hp=tiles["attn_hp"],
                   name="mla_attn")
    pe, w1, w2, tab_t, ov, n_slc = nsa
    kvc = _nsa_compress(pb, ob["k_cmp"] // NSA_HEAD_DIM, pe, w1, w2)
    oc, sel = _nsa_select(pb, ob["q_nsa"] // NSA_WIDTH, pf, of["gates"] // LANES, kvc[0], kvc[1], pos_row,
                          pos_cmp_rep, consec, tab_t, ov, n_slc=n_slc, tq=tiles["nsa_tq"])
    o_s = _nsa_slc(pb, ob["q_nsa"], ob["k_slc"], ob["v_slc"], sel, pos_row, pos_rep, tab_t, pq_min, pk_max, consec,
                   tq=tiles["slc_tq"], tk=tiles["slc_tk"])
    y_nsa = _nsa_win(pb, ob["q_nsa"] // NSA_WIDTH, ob["k_win"] // NSA_KV, ob["v_win"] // NSA_KV,
                     pf, of["gates"] // LANES, of["g_nsa"] // NSA_WIDTH, oc, o_s, pos_row, pos_rep, consec, tab_t,
                     tq=tiles["win_tq"])
    y_mem = _mem_attn(pb, ob["q_mem"] // MEM_HEAD_DIM, pf, of["g_mem"] // MEM_HEAD_DIM, mem_kv, t=tiles["mem_t"])
    return y_mla, y_nsa, y_mem


def _tiles(seq):
    return {"tm": min(seq, 1024), "tn_b": 512, "tn_f": 768, "tn_out": 512, "tn_glu": 512, "tn_mem": 512,
            "s5_tc": min(seq, 1024), "decay_t": min(seq, 512), "attn_tq": min(seq, 512),
            "attn_tk": min(seq, 512), "attn_hp": 2, "slc_tq": 256, "slc_tk": 512, "nsa_tq": 256, "win_tq": 128,
            "mem_t": min(seq, 1024), "norm_t": min(seq, 512), "prep_t": min(seq, 256)}


def _context(positions, t5_table, seq, tiles):
    pos = positions[0]
    pos_col = pos.reshape(seq, 1)
    pos_row = pos.reshape(1, seq)
    pos_rep = jnp.broadcast_to(pos_col, (seq, LANES))
    pq_min = jnp.min(pos.reshape(seq // tiles["slc_tq"], tiles["slc_tq"]), axis=1)
    pk_max = jnp.max(pos.reshape(seq // tiles["slc_tk"], tiles["slc_tk"]), axis=1)
    nc = seq // NSA_CMP_STRIDE
    pos_cmp = jnp.pad(pos[NSA_CMP_LEN - 1::NSA_CMP_STRIDE], (0, 1))
    pos_cmp_rep = jnp.broadcast_to(pos_cmp.reshape(nc, 1), (nc, LANES))
    half = MLA_ROPE // 2
    inv_freq = ROPE_THETA ** (-jnp.arange(half, dtype=F32) / half)
    freq = jnp.concatenate([inv_freq, inv_freq, jnp.zeros((LANES - MLA_ROPE,), F32)]).reshape(1, LANES)
    tab_t = jnp.pad(t5_table.astype(F32).T, ((0, SUBLANES - NSA_HEADS), (0, LANES - T5_BUCKETS)))
    tab_t = tab_t.at[:, T5_MASK_BUCKET].set(NEG_INF)
    n_slc = seq // NSA_SLC_BLOCK
    cs = np.arange(nc) * NSA_CMP_STRIDE
    ss = np.arange(LANES) * NSA_SLC_BLOCK
    ov_np = np.clip(np.minimum(cs[:, None] + NSA_CMP_LEN, ss[None, :] + NSA_SLC_BLOCK)
                    - np.maximum(cs[:, None], ss[None, :]), 0, None) / NSA_CMP_LEN
    ov_np[nc - 1, :] = 0.0
    ov_np[:, n_slc:] = 0.0
    consec = jnp.all(pos[1:] - pos[:-1] == 1).astype(I32).reshape(1)
    return {"pos": (pos_col.astype(F32), pos_row, pos_cmp_rep, pos_rep, pq_min, pk_max, consec), "freq": freq,
            "tab_t": tab_t, "ov": jnp.asarray(ov_np.T, BF16), "n_slc": n_slc}


def _odd_params(i, mla_g_cq, mla_g_ckv, mla_w_uq, mla_w_ukv, nsa_cmp_pe, nsa_cmp_w1, nsa_cmp_w2, ctx):
    dq = MLA_NOPE + MLA_ROPE
    w_uq = mla_w_uq[i].reshape(MLA_Q_RANK, MLA_HEADS, dq)
    w_uq = jnp.pad(w_uq, ((0, 0), (0, 0), (0, 2 * LANES - dq))).reshape(MLA_Q_RANK, -1).astype(BF16)
    mla = (mla_g_cq[i], mla_g_ckv[i], w_uq, mla_w_ukv[i].astype(BF16), ctx["freq"])
    nsa = (nsa_cmp_pe[i].astype(F32), nsa_cmp_w1[i], nsa_cmp_w2[i], ctx["tab_t"],
           ctx["ov"], ctx["n_slc"])
    return mla, nsa


def kernel(x, mem, positions, norm_g, mem_norm_g, final_norm_g, t5_table, w_out, mem_w_kv, even_w_in, s5_lam_re,
           s5_lam_im, s5_log_dt, s5_b_re, s5_b_im, s5_c_re, s5_c_im, s5_d, s5_w_glu, fox_b_f, odd_w_in, mla_g_cq,
           mla_g_ckv, mla_w_uq, mla_w_ukv, nsa_cmp_pe, nsa_cmp_w1, nsa_cmp_w2):
    batch, seq, _ = x.shape
    assert batch == 1 and seq % 1024 == 0 and seq // NSA_SLC_BLOCK <= LANES
    depth = norm_g.shape[0]
    tiles = _tiles(seq)
    ctx = _context(positions, t5_table, seq, tiles)
    h = x[0]
    mem_kv_all = _mem_kv_all(mem[0], mem_norm_g, mem_w_kv, tn=tiles["tn_mem"])
    for layer in range(depth):
        i = layer // 2
        mem_kv = (mem_kv_all, layer)
        if layer % 2 == 0:
            w_in = (_reorder_w_in(even_w_in[i], EVEN_SPLITS, EVEN_B_ORDER),
                    _reorder_w_in(even_w_in[i], EVEN_SPLITS, EVEN_F_ORDER))
            b_cat, c_cat, tab = _s5_prepare(s5_lam_re[i], s5_lam_im[i], s5_log_dt[i], s5_b_re[i], s5_b_im[i],
                                            s5_c_re[i], s5_c_im[i])
            ys = _even_layer(h, mem_kv, norm_g[layer], w_in, (b_cat, c_cat, tab, s5_d[i]),
                             s5_w_glu[i], fox_b_f[i], tiles)
        else:
            w_in = (_reorder_w_in(odd_w_in[i], ODD_SPLITS, ODD_B_ORDER),
                    _reorder_w_in(odd_w_in[i], ODD_SPLITS, ODD_F_ORDER))
            mla, nsa = _odd_params(i, mla_g_cq, mla_g_ckv, mla_w_uq, mla_w_ukv, nsa_cmp_pe, nsa_cmp_w1, nsa_cmp_w2,
                                   ctx)
            ys = _odd_layer(h, mem_kv, norm_g[layer], w_in, mla, nsa, ctx["pos"], tiles)
        h = _out_proj(h, ys, w_out, layer, tm=tiles["tm"], tn=tiles["tn_out"])
    return _final_norm(h, final_norm_g, tm=tiles["norm_t"])[None]
```

```python
import functools
import math

import numpy as np
import jax
import jax.numpy as jnp
from jax import lax
from jax.experimental import pallas as pl
from jax.experimental.pallas import tpu as pltpu

F32 = jnp.float32
BF16 = jnp.bfloat16
I32 = jnp.int32

RMS_EPS = 1e-6
NEG_INF = -1e30
LOG2E = math.log2(math.e)

S5_WIDTH = 1024
S5_GROUP = 16
S5_GROUPS = S5_WIDTH // S5_GROUP
S5_STATE = 64
FOX_HEADS = 8
FOX_HEAD_DIM = 128
FOX_WIDTH = FOX_HEADS * FOX_HEAD_DIM
MEM_HEADS = 4
MEM_HEAD_DIM = 128
MEM_WIDTH = MEM_HEADS * MEM_HEAD_DIM
MLA_HEADS = 8
MLA_Q_RANK = 512
MLA_KV_RANK = 512
MLA_NOPE = 128
MLA_ROPE = 64
MLA_V = 128
MLA_WIDTH = MLA_HEADS * MLA_V
ROPE_THETA = 10000.0
NSA_HEADS = 8
NSA_KV_GROUPS = 2
NSA_REP = NSA_HEADS // NSA_KV_GROUPS
NSA_HEAD_DIM = 128
NSA_WIDTH = NSA_HEADS * NSA_HEAD_DIM
NSA_KV = NSA_KV_GROUPS * NSA_HEAD_DIM
NSA_CMP_LEN = 32
NSA_CMP_STRIDE = 16
NSA_CMP_HIDDEN = 256
NSA_SLC_BLOCK = 64
NSA_SLC_TOPK = 16
NSA_WINDOW = 512
FORCE_SCORE = 1e6
T5_BUCKETS = 32
T5_MAX_DIST = 1024

EVEN_SPLITS = (S5_WIDTH, S5_WIDTH, FOX_WIDTH, FOX_WIDTH, FOX_WIDTH, FOX_HEADS, FOX_WIDTH, MEM_WIDTH, MEM_WIDTH)
ODD_SPLITS = (MLA_Q_RANK, MLA_KV_RANK, MLA_ROPE, MLA_WIDTH, NSA_WIDTH, NSA_KV, NSA_KV, NSA_KV, NSA_KV, NSA_KV,
              NSA_KV, 3 * NSA_HEADS, NSA_WIDTH, MEM_WIDTH, MEM_WIDTH)

LANES = 128
SUBLANES = 8
MXU_COLS = 256
VMEM_LIMIT_BYTES = 56 * 1024 * 1024

EVEN_B_ORDER = (("q", 2, 1024), ("k", 3, 1024), ("v", 4, 1024), ("q_mem", 7, 512))
EVEN_F_ORDER = (("u", 0, 1024), ("g_s5", 1, 1024), ("g_fox", 6, 1024), ("g_mem", 8, 512), ("f", 5, 128),
                ("pad", None, 128))
ODD_B_ORDER = (("q_nsa", 4, 1024), ("q_mem", 13, 512), ("k_cmp", 5, 256), ("v_cmp", 6, 256), ("k_slc", 7, 256),
               ("v_slc", 8, 256), ("k_win", 9, 256), ("v_win", 10, 256))
ODD_F_ORDER = (("g_nsa", 12, 1024), ("g_mla", 3, 1024), ("c_q", 0, 512), ("c_kv", 1, 512), ("g_mem", 14, 512),
               ("k_rope", 2, 128), ("gates", 11, 128))


def _layout(order):
    off, out = 0, {}
    for name, _, width in order:
        assert off % width == 0
        out[name] = off
        off += width
    return out, off


EVEN_B_OFF, EVEN_B_N = _layout(EVEN_B_ORDER)
EVEN_F_OFF, EVEN_F_N = _layout(EVEN_F_ORDER)
ODD_B_OFF, ODD_B_N = _layout(ODD_B_ORDER)
ODD_F_OFF, ODD_F_N = _layout(ODD_F_ORDER)


def _reorder_w_in(w, splits, order):
    starts = np.concatenate([[0], np.cumsum(splits)])
    cols = []
    for _, idx, width in order:
        if idx is None:
            cols.append(jnp.zeros((w.shape[0], width), w.dtype))
            continue
        seg = w[:, int(starts[idx]):int(starts[idx + 1])]
        pad = width - seg.shape[1]
        if pad:
            seg = jnp.pad(seg, ((0, 0), (0, pad)))
        cols.append(seg)
    return jnp.concatenate(cols, axis=1).astype(BF16)


def _cparams(*sem):
    return pltpu.CompilerParams(dimension_semantics=sem, vmem_limit_bytes=VMEM_LIMIT_BYTES)


def _silu(g):
    return g * jax.nn.sigmoid(g)


def _norm_matmul_kernel(x_ref, g_ref, w_ref, o_ref, xn_ref):
    @pl.when(pl.program_id(1) == 0)
    def _():
        x = x_ref[...]
        ms = jnp.mean(x * x, axis=-1, keepdims=True)
        xn_ref[...] = (x * lax.rsqrt(ms + RMS_EPS) * g_ref[...]).astype(BF16)

    o_ref[...] = jnp.dot(xn_ref[...], w_ref[...].astype(BF16), preferred_element_type=F32).astype(o_ref.dtype)


def _norm_matmul(x, g, w, *, x_cb=0, tm, tn, name, out_dtype=F32):
    m = x.shape[0]
    k, n = w.shape
    return pl.pallas_call(
        _norm_matmul_kernel,
        grid=(m // tm, n // tn),
        in_specs=[pl.BlockSpec((tm, k), lambda i, j: (i, x_cb)),
                  pl.BlockSpec((1, k), lambda i, j: (0, 0)),
                  pl.BlockSpec((k, tn), lambda i, j: (0, j))],
        out_specs=pl.BlockSpec((tm, tn), lambda i, j: (i, j)),
        out_shape=jax.ShapeDtypeStruct((m, n), out_dtype),
        scratch_shapes=[pltpu.VMEM((tm, k), BF16)],
        compiler_params=_cparams("parallel", "arbitrary"),
        name=name,
    )(x, g.reshape(1, k), w)


def _out_proj_kernel(h_ref, *refs):
    o_ref = refs[-1]
    n = (len(refs) - 1) // 2
    acc = h_ref[...]
    for y_ref, w_ref in zip(refs[:n], refs[n:2 * n]):
        acc = acc + jnp.dot(y_ref[...], w_ref[...].astype(BF16), preferred_element_type=F32)
    o_ref[...] = acc


def _out_proj(h, ys, w_all, layer, *, tm, tn):
    m, n = h.shape
    in_specs = [pl.BlockSpec((tm, tn), lambda i, j: (i, j))]
    in_specs += [pl.BlockSpec((tm, y.shape[1]), lambda i, j: (i, 0)) for y in ys]
    row = 0
    for y in ys:
        width = y.shape[1]
        assert row % width == 0
        in_specs.append(pl.BlockSpec((None, width, tn), lambda i, j, rb=row // width: (layer, rb, j)))
        row += width
    return pl.pallas_call(
        _out_proj_kernel,
        grid=(m // tm, n // tn),
        in_specs=in_specs,
        out_specs=pl.BlockSpec((tm, tn), lambda i, j: (i, j)),
        out_shape=jax.ShapeDtypeStruct((m, n), F32),
        compiler_params=_cparams("parallel", "arbitrary"),
        name="out_proj",
    )(h, *ys, *([w_all] * len(ys)))


def _final_norm_kernel(x_ref, g_ref, o_ref):
    x = x_ref[...]
    ms = jnp.mean(x * x, axis=-1, keepdims=True)
    o_ref[...] = x * lax.rsqrt(ms + RMS_EPS) * g_ref[...]


def _final_norm(h, g, *, tm):
    m, n = h.shape
    return pl.pallas_call(
        _final_norm_kernel,
        grid=(m // tm,),
        in_specs=[pl.BlockSpec((tm, n), lambda i: (i, 0)), pl.BlockSpec((1, n), lambda i: (0, 0))],
        out_specs=pl.BlockSpec((tm, n), lambda i: (i, 0)),
        out_shape=jax.ShapeDtypeStruct((m, n), F32),
        compiler_params=_cparams("parallel"),
        name="final_norm",
    )(h, g.reshape(1, n))


ONES_ROWS = 16


def _build_vt(v_ref, vt_ref, hh, dv, seq, chunk):
    def body(c, carry):
        st = pl.multiple_of(c * chunk, chunk)
        vt_ref[hh, 0:dv, pl.ds(st, chunk)] = v_ref[pl.ds(st, chunk), hh * dv:(hh + 1) * dv].astype(F32).T.astype(BF16)
        return carry

    lax.fori_loop(0, seq // chunk, body, 0)
    extra = vt_ref.shape[1] - dv
    if extra:
        vt_ref[hh, dv:dv + extra, :] = jnp.ones((extra, seq), BF16)


def _flash_kernel(*refs, scale, tq, tk, hp, dk, dv, has_decay):
    cw = min(tq, MXU_COLS)
    ncg = tq // cw
    streams = [(hh, cg) for hh in range(hp) for cg in range(ncg)]
    if has_decay:
        q_ref, k_ref, v_ref, g_ref, cq_ref, ck_ref, o_ref, vt_ref, qt_ref, s_ref, m_ref, acc_ref = refs
    else:
        q_ref, k_ref, v_ref, g_ref, o_ref, vt_ref, qt_ref, s_ref, m_ref, acc_ref = refs
    hb = pl.program_id(0)
    qi = pl.program_id(1)
    seq = k_ref.shape[0]

    @pl.when(qi == 0)
    def _():
        for hh in range(hp):
            _build_vt(v_ref, vt_ref, hh, dv, seq, tk)

    for v, (hh, cg) in enumerate(streams):
        qt_ref[v] = (q_ref[cg * cw:(cg + 1) * cw, hh * dk:(hh + 1) * dk].astype(F32) * (scale * LOG2E)).T.astype(BF16)
    m_ref[...] = jnp.full(m_ref.shape, NEG_INF, F32)
    acc_ref[...] = jnp.zeros(acc_ref.shape, F32)
    n_full = (qi * tq) // tk
    if has_decay:
        cq2 = [cq_ref[pl.ds(hb * hp + hh, 1), cg * cw:(cg + 1) * cw] for hh, cg in streams]

    def scores(j, slot):
        start = pl.multiple_of(j * tk, tk)
        for v, (hh, cg) in enumerate(streams):
            s_ref[slot, v] = jnp.dot(k_ref[pl.ds(start, tk), hh * dk:(hh + 1) * dk], qt_ref[v],
                                     preferred_element_type=F32)

    def softmax_pv(j, slot, masked):
        start = pl.multiple_of(j * tk, tk)
        for v, (hh, cg) in enumerate(streams):
            t = s_ref[slot, v]
            if has_decay:
                t = t - jnp.concatenate([ck_ref[hh, pl.ds(start, tk), :]] * (cw // LANES), axis=1)
            if masked:
                key = start + lax.broadcasted_iota(I32, (tk, cw), 0)
                qry = qi * tq + cg * cw + lax.broadcasted_iota(I32, (tk, cw), 1)
                t = jnp.where(key <= qry, t, NEG_INF)
            m_prev = m_ref[v]
            mx = jnp.max(t, axis=0, keepdims=True)
            if has_decay:
                m_new = jnp.maximum(m_prev, mx + cq2[v])
                shift = m_new - cq2[v]
            else:
                m_new = jnp.maximum(m_prev, mx)
                shift = m_new
            alpha = jnp.exp2(m_prev - m_new)
            p = jnp.exp2(t - shift).astype(BF16)
            acc_ref[v] = alpha * acc_ref[v] + jnp.dot(vt_ref[hh, :, pl.ds(start, tk)], p,
                                                       preferred_element_type=F32)
            m_ref[v] = m_new

    scores(0, 0)

    def pair(j):
        scores(j + 1, 1)
        softmax_pv(j, 0, False)
        scores(j + 2, 0)
        softmax_pv(j + 1, 1, False)

    def quad(qq, carry):
        pair(4 * qq)
        pair(4 * qq + 2)
        return carry

    quads = n_full // 4
    lax.fori_loop(0, quads, quad, 0)
    rem = n_full - 4 * quads

    @pl.when(rem >= 2)
    def _():
        pair(4 * quads)

    last = 4 * quads + 2 * (rem // 2)

    @pl.when(n_full % 2 == 1)
    def _():
        scores(last + 1, 1)
        softmax_pv(last, 0, False)
        softmax_pv(last + 1, 1, True)

    @pl.when(n_full % 2 == 0)
    def _():
        softmax_pv(last, 0, True)

    for v, (hh, cg) in enumerate(streams):
        a = acc_ref[v]
        o = (a[0:dv, :] / a[dv:dv + 1, :]).T
        rows = slice(cg * cw, (cg + 1) * cw)
        cols = slice(hh * dv, (hh + 1) * dv)
        o_ref[rows, cols] = (o * _silu(g_ref[rows, cols])).astype(o_ref.dtype)


def _flash(q_arr, q_off, k_arr, k_off, v_arr, v_off, g_arr, g_off, *, heads, dk, dv, scale, tq, tk, hp,
           decay=None, name):
    seq = q_arr.shape[0]
    cw = min(tq, MXU_COLS)
    ns = hp * (tq // cw)
    once = pl.Buffered(1)
    in_specs = [pl.BlockSpec((tq, hp * dk), lambda h, i: (i, q_off // (hp * dk) + h)),
                pl.BlockSpec((seq, hp * dk), lambda h, i: (0, k_off // (hp * dk) + h), pipeline_mode=once),
                pl.BlockSpec((seq, hp * dv), lambda h, i: (0, v_off // (hp * dv) + h), pipeline_mode=once),
                pl.BlockSpec((tq, hp * dv), lambda h, i: (i, g_off // (hp * dv) + h))]
    args = [q_arr, k_arr, v_arr, g_arr]
    if decay is not None:
        cum_t, cum_rep = decay
        in_specs += [pl.BlockSpec((SUBLANES, tq), lambda h, i: (0, i)),
                     pl.BlockSpec((hp, seq, LANES), lambda h, i: (h, 0, 0), pipeline_mode=once)]
        args += [cum_t, cum_rep]
    return pl.pallas_call(
        functools.partial(_flash_kernel, scale=scale, tq=tq, tk=tk, hp=hp, dk=dk, dv=dv, has_decay=decay is not None),
        grid=(heads // hp, seq // tq),
        in_specs=in_specs,
        out_specs=pl.BlockSpec((tq, hp * dv), lambda h, i: (i, h)),
        out_shape=jax.ShapeDtypeStruct((seq, heads * dv), BF16),
        scratch_shapes=[pltpu.VMEM((hp, dv + ONES_ROWS, seq), BF16), pltpu.VMEM((ns, dk, cw), BF16),
                        pltpu.VMEM((2, ns, tk, cw), F32), pltpu.VMEM((ns, 1, cw), F32),
                        pltpu.VMEM((ns, dv + ONES_ROWS, cw), F32)],
        compiler_params=_cparams("parallel", "arbitrary"),
        name=name,
    )(*args)


def _decay_kernel(f_ref, b_ref, ct_ref, cr_ref, carry_ref, *, t):
    i = pl.program_id(0)

    @pl.when(i == 0)
    def _():
        carry_ref[...] = jnp.zeros(carry_ref.shape, F32)

    x = f_ref[...] + b_ref[...]
    lf = jnp.minimum(x, 0.0) - jnp.log1p(jnp.exp(-jnp.abs(x)))
    row = lax.broadcasted_iota(I32, lf.shape, 0)
    s = 1
    while s < t:
        lf = lf + jnp.where(row >= s, pltpu.roll(lf, s, 0), 0.0)
        s *= 2
    lf = lf + carry_ref[...]
    carry_ref[...] = lf[t - 1:t, :]
    lf2 = lf * LOG2E
    ct_ref[...] = lf2.T[:FOX_HEADS, :]
    for h in range(FOX_HEADS):
        cr_ref[h] = jnp.broadcast_to(lf2[:, h:h + 1], (t, LANES))


def _decay(proj, f_cb, b_f, *, t):
    seq = proj.shape[0]
    b = jnp.pad(b_f.reshape(1, FOX_HEADS), ((0, 0), (0, LANES - FOX_HEADS)))
    return pl.pallas_call(
        functools.partial(_decay_kernel, t=t),
        grid=(seq // t,),
        in_specs=[pl.BlockSpec((t, LANES), lambda i: (i, f_cb)), pl.BlockSpec((1, LANES), lambda i: (0, 0))],
        out_specs=[pl.BlockSpec((FOX_HEADS, t), lambda i: (0, i)),
                   pl.BlockSpec((FOX_HEADS, t, LANES), lambda i: (0, i, 0))],
        out_shape=[jax.ShapeDtypeStruct((FOX_HEADS, seq), F32), jax.ShapeDtypeStruct((FOX_HEADS, seq, LANES), F32)],
        scratch_shapes=[pltpu.VMEM((1, LANES), F32)],
        compiler_params=_cparams("arbitrary"),
        name="fox_decay",
    )(proj, b)


def _mem_attn_kernel(q_ref, kv_ref, g_ref, o_ref):
    d = MEM_HEAD_DIM
    for h in range(MEM_HEADS):
        hs = slice(h * d, (h + 1) * d)
        k = kv_ref[:, hs].astype(BF16)
        v = kv_ref[:, MEM_WIDTH + h * d:MEM_WIDTH + (h + 1) * d].astype(BF16)
        s = lax.dot_general(q_ref[:, hs], k, (((1,), (1,)), ((), ())), preferred_element_type=F32) * (d ** -0.5)
        m = jnp.max(s, axis=1, keepdims=True)
        p = jnp.exp(s - m)
        l = jnp.sum(p, axis=1, keepdims=True)
        o = jnp.dot(p.astype(BF16), v, preferred_element_type=F32) / l
        o_ref[:, hs] = (o * _silu(g_ref[:, hs])).astype(o_ref.dtype)


def _mem_kv_all(mem2d, g, w_all, *, tn):
    depth, k, n = w_all.shape
    m = mem2d.shape[0]
    return pl.pallas_call(
        _norm_matmul_kernel,
        grid=(depth, n // tn),
        in_specs=[pl.BlockSpec((m, k), lambda l, j: (0, 0)),
                  pl.BlockSpec((1, k), lambda l, j: (0, 0)),
                  pl.BlockSpec((None, k, tn), lambda l, j: (l, 0, j))],
        out_specs=pl.BlockSpec((None, m, tn), lambda l, j: (l, 0, j)),
        out_shape=jax.ShapeDtypeStruct((depth, m, n), F32),
        scratch_shapes=[pltpu.VMEM((m, k), BF16)],
        compiler_params=_cparams("arbitrary", "arbitrary"),
        name="mem_kv",
    )(mem2d, g.reshape(1, k), w_all)


def _mem_attn(proj_b, q_cb, proj_f, g_cb, mem_kv, *, t):
    seq = proj_b.shape[0]
    mem_kv, layer = mem_kv
    nm = mem_kv.shape[1]
    w = MEM_WIDTH
    d = MEM_HEAD_DIM
    return pl.pallas_call(
        _mem_attn_kernel,
        grid=(seq // t,),
        in_specs=[pl.BlockSpec((t, w), lambda i: (i, q_cb * d // w)),
                  pl.BlockSpec((None, nm, 2 * w), lambda i: (layer, 0, 0)),
                  pl.BlockSpec((t, w), lambda i: (i, g_cb * d // w))],
        out_specs=pl.BlockSpec((t, w), lambda i: (i, 0)),
        out_shape=jax.ShapeDtypeStruct((seq, w), BF16),
        compiler_params=_cparams("parallel"),
        name="mem_attn",
    )(proj_b, mem_kv, proj_f)


S5_TILE_GROUPS = LANES // S5_GROUP
S5_TILE_STATES = S5_TILE_GROUPS * S5_STATE
S5_TILES = S5_GROUPS // S5_TILE_GROUPS


def _s5_scan_kernel(u_ref, b_ref, c_ref, d_ref, tab_ref, z_ref, bu_ref, carry_ref, *, tc):
    ns = S5_TILE_STATES

    @pl.when(pl.program_id(1) == 0)
    def _():
        carry_ref[...] = jnp.zeros(carry_ref.shape, F32)

    u = u_ref[...]
    bu_ref[...] = jnp.dot(u.astype(BF16), b_ref[...], preferred_element_type=F32)
    steps = [(1, tab_ref[0], tab_ref[1]), (2, tab_ref[2], tab_ref[3]), (4, tab_ref[4], tab_ref[5])]
    pr = tab_ref[6]
    pi = tab_ref[7]

    def body(i, carry):
        cr, ci = carry
        r0 = pl.multiple_of(i * SUBLANES, SUBLANES)
        xr = bu_ref[pl.ds(r0, SUBLANES), 0:ns]
        xi = bu_ref[pl.ds(r0, SUBLANES), ns:2 * ns]
        for s, ar, ai in steps:
            sr = pltpu.roll(xr, s, 0)
            si = pltpu.roll(xi, s, 0)
            xr, xi = xr + ar * sr - ai * si, xi + ar * si + ai * sr
        xr, xi = xr + pr * cr - pi * ci, xi + pr * ci + pi * cr
        bu_ref[pl.ds(r0, SUBLANES), 0:ns] = xr
        bu_ref[pl.ds(r0, SUBLANES), ns:2 * ns] = xi
        return xr[SUBLANES - 1:SUBLANES, :], xi[SUBLANES - 1:SUBLANES, :]

    cr, ci = lax.fori_loop(0, tc // SUBLANES, body, (carry_ref[0:1, 0:ns], carry_ref[0:1, ns:2 * ns]))
    carry_ref[0:1, 0:ns] = cr
    carry_ref[0:1, ns:2 * ns] = ci
    y = jnp.dot(bu_ref[...].astype(BF16), c_ref[...], preferred_element_type=F32) + d_ref[...] * u
    z_ref[...] = jax.nn.gelu(y)


def _s5_prepare(lam_re, lam_im, log_dt, b_re, b_im, c_re, c_im):
    dt = jnp.exp(log_dt.astype(F32))[:, None]
    lr = lam_re.astype(F32)
    li = lam_im.astype(F32)
    mag = jnp.exp(lr * dt)
    ab_re = mag * jnp.cos(li * dt)
    ab_im = mag * jnp.sin(li * dt)
    den = lr * lr + li * li
    nr = ab_re - 1.0
    f_re = (nr * lr + ab_im * li) / den
    f_im = (ab_im * lr - nr * li) / den
    br = b_re.astype(F32)
    bim = b_im.astype(F32)
    bb_re = f_re[..., None] * br - f_im[..., None] * bim
    bb_im = f_re[..., None] * bim + f_im[..., None] * br
    eye = jnp.eye(S5_TILE_GROUPS, dtype=F32)

    def blockdiag_in(bb):
        t = bb.reshape(S5_TILES, S5_TILE_GROUPS, S5_STATE, S5_GROUP)
        m = jnp.einsum("jgpc,gh->jgchp", t, eye)
        return m.reshape(S5_TILES, LANES, S5_TILE_STATES)

    def blockdiag_out(cc):
        t = cc.reshape(S5_TILES, S5_TILE_GROUPS, S5_GROUP, S5_STATE)
        m = jnp.einsum("jgcp,gh->jgphc", t, eye)
        return m.reshape(S5_TILES, S5_TILE_STATES, LANES)

    b_cat = jnp.concatenate([blockdiag_in(bb_re), blockdiag_in(bb_im)], axis=2).astype(BF16)
    c_cat = jnp.concatenate([blockdiag_out(c_re.astype(F32)), -blockdiag_out(c_im.astype(F32))], axis=1).astype(BF16)

    a_r = ab_re.reshape(S5_TILES, 1, S5_TILE_STATES)
    a_i = ab_im.reshape(S5_TILES, 1, S5_TILE_STATES)

    def cmul(xr, xi, yr, yi):
        return xr * yr - xi * yi, xr * yi + xi * yr

    a2 = cmul(a_r, a_i, a_r, a_i)
    a4 = cmul(*a2, *a2)
    row = jnp.arange(SUBLANES)[None, :, None]
    tabs = []
    for s, (pr_, pi_) in ((1, (a_r, a_i)), (2, a2), (4, a4)):
        tabs.append(jnp.where(row >= s, pr_, 0.0))
        tabs.append(jnp.where(row >= s, pi_, 0.0))
    pw = [(a_r, a_i)]
    for _ in range(SUBLANES - 1):
        pw.append(cmul(*pw[-1], a_r, a_i))
    tabs.append(jnp.concatenate([p[0] for p in pw], axis=1))
    tabs.append(jnp.concatenate([p[1] for p in pw], axis=1))
    tab = jnp.stack([jnp.broadcast_to(t, (S5_TILES, SUBLANES, S5_TILE_STATES)) for t in tabs], axis=1)
    return b_cat, c_cat, tab.astype(F32)


def _s5_scan(proj, u_cb, b_cat, c_cat, d_skip, tab, *, tc):
    seq = proj.shape[0]
    ns = S5_TILE_STATES
    d = d_skip.astype(F32).reshape(S5_TILES, 1, LANES)
    return pl.pallas_call(
        functools.partial(_s5_scan_kernel, tc=tc),
        grid=(S5_TILES, seq // tc),
        in_specs=[pl.BlockSpec((tc, LANES), lambda j, c: (c, u_cb + j)),
                  pl.BlockSpec((None, LANES, 2 * ns), lambda j, c: (j, 0, 0)),
                  pl.BlockSpec((None, 2 * ns, LANES), lambda j, c: (j, 0, 0)),
                  pl.BlockSpec((None, 1, LANES), lambda j, c: (j, 0, 0)),
                  pl.BlockSpec((None, 8, SUBLANES, ns), lambda j, c: (j, 0, 0, 0))],
        out_specs=pl.BlockSpec((tc, LANES), lambda j, c: (c, j)),
        out_shape=jax.ShapeDtypeStruct((seq, S5_WIDTH), F32),
        scratch_shapes=[pltpu.VMEM((tc, 2 * ns), F32), pltpu.VMEM((SUBLANES, 2 * ns), F32)],
        compiler_params=_cparams("parallel", "arbitrary"),
        name="s5_scan",
    )(proj, b_cat, c_cat, d, tab)


def _s5_glu_kernel(z_ref, w_ref, g_ref, o_ref, *, tn):
    j = pl.program_id(1)
    z = z_ref[...]
    a = jnp.dot(z.astype(BF16), w_ref[...].astype(BF16), preferred_element_type=F32)
    zc = z_ref[:, pl.ds(pl.multiple_of(j * tn, tn), tn)]
    o_ref[...] = (zc * jax.nn.sigmoid(a) * _silu(g_ref[...])).astype(o_ref.dtype)


def _s5_glu(z, w_glu, proj, g_cb, *, tm, tn):
    seq, n = z.shape
    return pl.pallas_call(
        functools.partial(_s5_glu_kernel, tn=tn),
        grid=(seq // tm, n // tn),
        in_specs=[pl.BlockSpec((tm, n), lambda i, j: (i, 0)),
                  pl.BlockSpec((n, tn), lambda i, j: (0, j)),
                  pl.BlockSpec((tm, tn), lambda i, j: (i, g_cb * (n // tn) + j))],
        out_specs=pl.BlockSpec((tm, tn), lambda i, j: (i, j)),
        out_shape=jax.ShapeDtypeStruct((seq, n), BF16),
        compiler_params=_cparams("parallel", "arbitrary"),
        name="s5_glu",
    )(z, w_glu, proj)


def _rope_tables(pos, freq):
    ang = pos * freq
    lane = lax.broadcasted_iota(I32, ang.shape, 1)
    half = MLA_ROPE // 2
    cos = jnp.cos(ang)
    sin = jnp.sin(ang)
    c = jnp.where(lane < MLA_ROPE, cos, 0.0)
    s1 = jnp.where(lane < half, -sin, 0.0)
    s2 = jnp.where((lane >= half) & (lane < MLA_ROPE), sin, 0.0)
    return c, s1, s2


def _rope_apply(x, c, s1, s2):
    half = MLA_ROPE // 2
    return x * c + pltpu.roll(x, LANES - half, 1) * s1 + pltpu.roll(x, half, 1) * s2


def _mla_up_kernel(cq_ref, ckv_ref, kr_ref, pos_ref, freq_ref, gq_ref, gkv_ref, wq_ref, wkv_ref, q_ref, k_ref, v_ref):
    def normed(x_ref, g_ref):
        x = x_ref[...]
        ms = jnp.mean(x * x, axis=-1, keepdims=True)
        return (x * lax.rsqrt(ms + RMS_EPS) * g_ref[...]).astype(BF16)

    qf = jnp.dot(normed(cq_ref, gq_ref), wq_ref[...], preferred_element_type=F32)
    kvf = jnp.dot(normed(ckv_ref, gkv_ref), wkv_ref[...], preferred_element_type=F32)
    c, s1, s2 = _rope_tables(pos_ref[...], freq_ref[...])
    kr = _rope_apply(kr_ref[...], c, s1, s2).astype(BF16)
    for h in range(MLA_HEADS):
        b = 2 * LANES * h
        q_ref[:, b:b + LANES] = qf[:, b:b + LANES].astype(BF16)
        q_ref[:, b + LANES:b + 2 * LANES] = _rope_apply(qf[:, b + LANES:b + 2 * LANES], c, s1, s2).astype(BF16)
        k_ref[:, b:b + LANES] = kvf[:, b:b + LANES].astype(BF16)
        k_ref[:, b + LANES:b + 2 * LANES] = kr
        v_ref[:, LANES * h:LANES * (h + 1)] = kvf[:, b + LANES:b + 2 * LANES].astype(BF16)


def _mla_up(proj, cq_cb, ckv_cb, kr_cb, g_cq, g_ckv, w_uq, w_ukv, pos_col, freq, *, t):
    seq = proj.shape[0]
    w = 2 * LANES * MLA_HEADS
    rq, rkv = w_uq.shape[0], w_ukv.shape[0]
    return pl.pallas_call(
        _mla_up_kernel,
        grid=(seq // t,),
        in_specs=[pl.BlockSpec((t, rq), lambda i: (i, cq_cb)),
                  pl.BlockSpec((t, rkv), lambda i: (i, ckv_cb)),
                  pl.BlockSpec((t, LANES), lambda i: (i, kr_cb)),
                  pl.BlockSpec((t, 1), lambda i: (i, 0)),
                  pl.BlockSpec((1, LANES), lambda i: (0, 0)),
                  pl.BlockSpec((1, rq), lambda i: (0, 0)),
                  pl.BlockSpec((1, rkv), lambda i: (0, 0)),
                  pl.BlockSpec((rq, w), lambda i: (0, 0)),
                  pl.BlockSpec((rkv, w), lambda i: (0, 0))],
        out_specs=[pl.BlockSpec((t, w), lambda i: (i, 0)),
                   pl.BlockSpec((t, w), lambda i: (i, 0)),
                   pl.BlockSpec((t, MLA_WIDTH), lambda i: (i, 0))],
        out_shape=[jax.ShapeDtypeStruct((seq, w), BF16), jax.ShapeDtypeStruct((seq, w), BF16),
                   jax.ShapeDtypeStruct((seq, MLA_WIDTH), BF16)],
        compiler_params=_cparams("parallel"),
        name="mla_up",
    )(proj, proj, proj, pos_col, freq, g_cq.reshape(1, rq), g_ckv.reshape(1, rkv), w_uq, w_ukv)


def _t5_bucket(dist):
    n = jnp.maximum(dist, 0)
    max_exact = T5_BUCKETS // 2
    log_ratio = jnp.log(jnp.maximum(n, 1).astype(F32) / max_exact) / math.log(T5_MAX_DIST / max_exact)
    large = jnp.minimum(max_exact + (log_ratio * (T5_BUCKETS - max_exact)).astype(I32), T5_BUCKETS - 1)
    return jnp.where(n < max_exact, n, large)


T5_MASK_BUCKET = T5_BUCKETS


def _t5_lookup(table_row, bucket):
    rows, width = bucket.shape
    tab = jnp.broadcast_to(table_row, (rows, LANES))
    parts = [jnp.take_along_axis(tab, bucket[:, c:c + LANES], axis=1, mode="promise_in_bounds")
             for c in range(0, width, LANES)]
    return parts[0] if len(parts) == 1 else jnp.concatenate(parts, axis=1)


def _nsa_cmp_kernel(x_ref, pe_ref, w1_ref, w2_ref, o_ref, xf_ref, *, nc):
    half = NSA_CMP_LEN // 2
    d = NSA_HEAD_DIM
    xf_ref[...] = x_ref[...].astype(F32)
    u = jnp.zeros((nc, NSA_CMP_HIDDEN), F32)
    v = jnp.zeros((nc, NSA_CMP_HIDDEN), F32)
    for r in range(half):
        a = xf_ref[pl.ds(r, nc, stride=NSA_CMP_STRIDE), :]
        u = u + jnp.dot((a + pe_ref[r:r + 1, :]).astype(BF16), w1_ref[r * d:(r + 1) * d, :].astype(BF16),
                        preferred_element_type=F32)
        v = v + jnp.dot((a + pe_ref[half + r:half + r + 1, :]).astype(BF16),
                        w1_ref[(half + r) * d:(half + r + 1) * d, :].astype(BF16), preferred_element_type=F32)
    hid = u + pltpu.roll(v, nc - 1, 0)
    o_ref[...] = jnp.dot(jax.nn.gelu(hid).astype(BF16), w2_ref[...].astype(BF16),
                         preferred_element_type=F32).astype(o_ref.dtype)


def _nsa_compress(proj, k_cb, pe, w1, w2):
    seq = proj.shape[0]
    nc = seq // NSA_CMP_STRIDE
    d = NSA_HEAD_DIM
    g = NSA_KV_GROUPS
    return pl.pallas_call(
        functools.partial(_nsa_cmp_kernel, nc=nc),
        grid=(2, g),
        in_specs=[pl.BlockSpec((seq, d), lambda a, b: (0, k_cb + a * g + b)),
                  pl.BlockSpec((None, NSA_CMP_LEN, d), lambda a, b: (a, 0, 0)),
                  pl.BlockSpec((None, NSA_CMP_LEN * d, NSA_CMP_HIDDEN), lambda a, b: (a, 0, 0)),
                  pl.BlockSpec((None, NSA_CMP_HIDDEN, d), lambda a, b: (a, 0, 0))],
        out_specs=pl.BlockSpec((None, None, nc, d), lambda a, b: (a, b, 0, 0)),
        out_shape=jax.ShapeDtypeStruct((2, g, nc, d), BF16),
        scratch_shapes=[pltpu.VMEM((seq, d), F32)],
        compiler_params=_cparams("parallel", "arbitrary"),
        name="nsa_compress",
    )(proj, pe, w1, w2)


def _nsa_select_kernel(consec_ref, q_ref, kc_ref, vc_ref, posq_ref, posc_ref, tab_ref, gate_ref, ov_ref,
                       oc_ref, sel_ref, strip_ref, *, nc, n_slc, tq):
    qi = pl.program_id(0)
    d = NSA_HEAD_DIM
    c = d ** -0.5 * LOG2E
    per_block = tq // NSA_CMP_STRIDE
    lane = lax.broadcasted_iota(I32, (1, tq), 1)
    tok = qi * tq + lane

    def select(bias):
        gates = jax.nn.sigmoid(gate_ref[...])
        ovt = ov_ref[...]
        js = lax.broadcasted_iota(I32, (LANES, tq), 0)
        jf = js.astype(F32)
        cur = tok // NSA_SLC_BLOCK
        forced = (js == 0) | (js == cur) | (js == cur - 1)
        for g in range(NSA_KV_GROUPS):
            kc = kc_ref[g]
            vct = vc_ref[g].astype(F32).T.astype(BF16)
            psum = jnp.zeros((nc, tq), F32)
            for r in range(NSA_REP):
                h = g * NSA_REP + r
                qt = (q_ref[:, h * d:(h + 1) * d].astype(F32) * c).T.astype(BF16)
                t = jnp.dot(kc, qt, preferred_element_type=F32) + bias(h)
                m = jnp.max(t, axis=0, keepdims=True)
                e = jnp.exp2(t - m)
                l = jnp.sum(e, axis=0, keepdims=True)
                p = e * jnp.where(m > 0.5 * NEG_INF, 1.0 / l, 0.0)
                o = jnp.dot(vct, p.astype(BF16), preferred_element_type=F32)
                oc_ref[:, h * d:(h + 1) * d] = gates[:, 3 * h:3 * h + 1] * o.T
                psum = psum + p
            p_hi = psum.astype(BF16)
            p_lo = (psum - p_hi.astype(F32)).astype(BF16)
            imp = jnp.dot(ovt, p_hi, preferred_element_type=F32) + jnp.dot(ovt, p_lo, preferred_element_type=F32)
            st = jnp.where(forced, FORCE_SCORE, jnp.where(js > cur, -1.0, imp))
            st = jnp.where(js < n_slc, st, -2.0)
            sel = jnp.zeros((LANES, tq), F32)
            for _ in range(NSA_SLC_TOPK):
                mx = jnp.max(st, axis=0, keepdims=True)
                first = jnp.min(jnp.where(st == mx, jf, float(LANES)), axis=0, keepdims=True)
                hit = jf == first
                sel = jnp.where(hit, 1.0, sel)
                st = jnp.where(hit, -3e38, st)
            sel_ref[g] = sel.astype(sel_ref.dtype)

    consecutive = consec_ref[0] == 1

    @pl.when(jnp.logical_and(consecutive, qi == 0))
    def _():
        for chunk in range(2 * nc // LANES):
            rel = chunk * LANES - nc + lax.broadcasted_iota(I32, (LANES, 1), 0)
            dist = lane - (rel * NSA_CMP_STRIDE + (NSA_CMP_LEN - 1))
            bucket = jnp.where(dist >= 0, _t5_bucket(dist), T5_MASK_BUCKET)
            for h in range(NSA_HEADS):
                strip_ref[h, chunk * LANES:(chunk + 1) * LANES, :] = _t5_lookup(tab_ref[h:h + 1, :], bucket)

    @pl.when(consecutive)
    def _():
        start = pl.multiple_of(nc - per_block * qi, per_block)
        select(lambda h: strip_ref[h, pl.ds(start, nc), :])

    @pl.when(jnp.logical_not(consecutive))
    def _():
        cmp_end = lax.broadcasted_iota(I32, (nc, 1), 0) * NSA_CMP_STRIDE + (NSA_CMP_LEN - 1)
        pos_c = jnp.concatenate([posc_ref[...]] * (tq // LANES), axis=1)
        bucket = jnp.where(cmp_end <= tok, _t5_bucket(posq_ref[...] - pos_c), T5_MASK_BUCKET)
        select(lambda h: _t5_lookup(tab_ref[h:h + 1, :], bucket))


def _nsa_select(proj_b, q_cb, proj_f, gate_cb, kc, vc, pos_row, pos_cmp_rep, consec, tab_t, ov_t, *, n_slc, tq):
    seq = proj_b.shape[0]
    nc = kc.shape[1]
    g = NSA_KV_GROUPS
    d = NSA_HEAD_DIM
    grid_spec = pltpu.PrefetchScalarGridSpec(
        num_scalar_prefetch=1,
        grid=(seq // tq,),
        in_specs=[pl.BlockSpec((tq, NSA_WIDTH), lambda i, *_: (i, q_cb)),
                  pl.BlockSpec((g, nc, d), lambda i, *_: (0, 0, 0)),
                  pl.BlockSpec((g, nc, d), lambda i, *_: (0, 0, 0)),
                  pl.BlockSpec((1, tq), lambda i, *_: (0, i)),
                  pl.BlockSpec((nc, LANES), lambda i, *_: (0, 0)),
                  pl.BlockSpec((SUBLANES, LANES), lambda i, *_: (0, 0)),
                  pl.BlockSpec((tq, LANES), lambda i, *_: (i, gate_cb)),
                  pl.BlockSpec((LANES, nc), lambda i, *_: (0, 0))],
        out_specs=[pl.BlockSpec((tq, NSA_WIDTH), lambda i, *_: (i, 0)),
                   pl.BlockSpec((g, LANES, tq), lambda i, *_: (0, 0, i))],
        scratch_shapes=[pltpu.VMEM((NSA_HEADS, 2 * nc, tq), F32)])
    return pl.pallas_call(
        functools.partial(_nsa_select_kernel, nc=nc, n_slc=n_slc, tq=tq),
        grid_spec=grid_spec,
        out_shape=[jax.ShapeDtypeStruct((seq, NSA_WIDTH), F32),
                   jax.ShapeDtypeStruct((g, LANES, seq), F32)],
        compiler_params=_cparams("arbitrary"),
        name="nsa_select",
    )(consec, proj_b, kc, vc, pos_row, pos_cmp_rep, tab_t * LOG2E, proj_f, ov_t)


def _nsa_slc_kernel(pqmin_ref, pkmax_ref, consec_ref, q_ref, k_ref, v_ref, sel_ref, posq_ref, posk_ref, tab_ref,
                    o_ref, vt_ref, qt_ref, s_ref, m_ref, l_ref, acc_ref, cache_ref, *, tq, tk):
    g = pl.program_id(0)
    qi = pl.program_id(1)
    d = NSA_HEAD_DIM
    scale = d ** -0.5
    seq = k_ref.shape[0]
    consecutive = consec_ref[0] == 1
    n_cached = cache_ref.shape[0]

    @pl.when(qi == 0)
    def _():
        _build_vt(v_ref, vt_ref, 0, d, seq, tk)

    @pl.when(jnp.logical_and(consecutive, qi == 0))
    def _():
        rel = lax.broadcasted_iota(I32, (1, tq), 1) - lax.broadcasted_iota(I32, (tk, 1), 0)
        for v in range(n_cached):
            dist = rel + v * tq
            bucket = jnp.where(dist >= 0, _t5_bucket(dist), T5_MASK_BUCKET)
            for r in range(NSA_REP):
                cache_ref[v, r] = _t5_lookup(tab_ref[pl.ds(g * NSA_REP + r, 1), :], bucket)

    for r in range(NSA_REP):
        qt_ref[r] = (q_ref[:, r * d:(r + 1) * d].astype(F32) * (scale * LOG2E)).T.astype(BF16)
    m_ref[...] = jnp.full(m_ref.shape, NEG_INF, F32)
    l_ref[...] = jnp.zeros(l_ref.shape, F32)
    acc_ref[...] = jnp.zeros(acc_ref.shape, F32)
    pos_q = posq_ref[...]
    per_tile = tk // NSA_SLC_BLOCK
    n_full = (qi * tq) // tk

    def scores(j, slot):
        start = pl.multiple_of(j * tk, tk)
        k = k_ref[pl.ds(start, tk), :]
        for r in range(NSA_REP):
            s_ref[slot, r] = jnp.dot(k, qt_ref[r], preferred_element_type=F32)

    def softmax_pv(j, slot, masked, mode):
        start = pl.multiple_of(j * tk, tk)
        picked = jnp.concatenate(
            [jnp.broadcast_to(sel_ref[pl.ds(j * per_tile + b, 1), :], (NSA_SLC_BLOCK, tq)) for b in range(per_tile)],
            axis=0)
        if masked and mode == "gather":
            key = start + lax.broadcasted_iota(I32, (tk, tq), 0)
            qry = qi * tq + lax.broadcasted_iota(I32, (tk, tq), 1)
            picked = jnp.where(key <= qry, picked, 0.0)
        mask = picked > 0.5
        if mode == "gather":
            pos_k = jnp.concatenate([posk_ref[pl.ds(start, tk), :]] * (tq // LANES), axis=1)
            bucket = jnp.where(mask, _t5_bucket(pos_q - pos_k), T5_MASK_BUCKET)
        for r in range(NSA_REP):
            h = g * NSA_REP + r
            t = s_ref[slot, r]
            if mode == "gather":
                t = t + _t5_lookup(tab_ref[pl.ds(h, 1), :], bucket)
            elif mode == "cached":
                t = jnp.where(mask, t + cache_ref[(qi * tq - start) // tq, r], NEG_INF)
            else:
                t = jnp.where(mask, t, NEG_INF)
            m_prev = m_ref[r]
            mx = jnp.max(t, axis=0, keepdims=True)
            if mode == "far":
                b = tab_ref[pl.ds(h, 1), T5_BUCKETS - 1:T5_BUCKETS]
                m_new = jnp.maximum(m_prev, mx + b)
                shift = m_new - b
            else:
                m_new = jnp.maximum(m_prev, mx)
                shift = m_new
            alpha = jnp.exp2(m_prev - m_new)
            p = jnp.exp2(t - shift)
            l_ref[r] = alpha * l_ref[r] + jnp.sum(p, axis=0, keepdims=True)
            acc_ref[r] = alpha * acc_ref[r] + jnp.dot(vt_ref[0, :, pl.ds(start, tk)], p.astype(BF16),
                                                       preferred_element_type=F32)
            m_ref[r] = m_new

    def is_far(j):
        return pqmin_ref[qi] - pkmax_ref[j] >= T5_MAX_DIST

    def stages(j, count, mode):
        for k in range(count):
            scores(j + k + 1, (k + 1) % 2)
            softmax_pv(j + k, k % 2, False, mode)

    def near(fn):
        @pl.when(consecutive)
        def _():
            fn("cached")

        @pl.when(jnp.logical_not(consecutive))
        def _():
            fn("gather")

    def far_or_near(far, fn):
        @pl.when(far)
        def _():
            fn("far")

        @pl.when(jnp.logical_not(far))
        def _():
            near(fn)

    scores(0, 0)

    def pair(jj, carry):
        j = 2 * jj
        far_or_near(jnp.logical_and(is_far(j), is_far(j + 1)), lambda mode: stages(j, 2, mode))
        return carry

    pairs = n_full // 2
    lax.fori_loop(0, pairs, pair, 0)
    last = 2 * pairs

    @pl.when(n_full % 2 == 1)
    def _():
        far_or_near(is_far(last), lambda mode: stages(last, 1, mode))
        near(lambda mode: softmax_pv(last + 1, 1, True, mode))

    @pl.when(n_full % 2 == 0)
    def _():
        near(lambda mode: softmax_pv(last, 0, True, mode))

    for r in range(NSA_REP):
        o_ref[:, r * d:(r + 1) * d] = (acc_ref[r] / l_ref[r]).T


def _nsa_slc(proj_b, q_off, k_off, v_off, sel, pos_row, pos_rep, tab_t, pq_min, pk_max, consec, *, tq, tk):
    seq = proj_b.shape[0]
    tab_t = tab_t * LOG2E
    d = NSA_HEAD_DIM
    gw = NSA_REP * d
    once = pl.Buffered(1)
    n_cached = (T5_MAX_DIST + tk - 2) // tq + 1 + tk // tq
    grid_spec = pltpu.PrefetchScalarGridSpec(
        num_scalar_prefetch=3,
        grid=(NSA_KV_GROUPS, seq // tq),
        in_specs=[pl.BlockSpec((tq, gw), lambda g, i, *_: (i, q_off // gw + g)),
                  pl.BlockSpec((seq, d), lambda g, i, *_: (0, k_off // d + g), pipeline_mode=once),
                  pl.BlockSpec((seq, d), lambda g, i, *_: (0, v_off // d + g), pipeline_mode=once),
                  pl.BlockSpec((None, LANES, tq), lambda g, i, *_: (g, 0, i)),
                  pl.BlockSpec((1, tq), lambda g, i, *_: (0, i)),
                  pl.BlockSpec((seq, LANES), lambda g, i, *_: (0, 0), pipeline_mode=once),
                  pl.BlockSpec((SUBLANES, LANES), lambda g, i, *_: (0, 0))],
        out_specs=pl.BlockSpec((tq, gw), lambda g, i, *_: (i, g)),
        scratch_shapes=[pltpu.VMEM((1, d, seq), BF16), pltpu.VMEM((NSA_REP, d, tq), BF16),
                        pltpu.VMEM((2, NSA_REP, tk, tq), F32), pltpu.VMEM((NSA_REP, 1, tq), F32),
                        pltpu.VMEM((NSA_REP, 1, tq), F32), pltpu.VMEM((NSA_REP, d, tq), F32),
                        pltpu.VMEM((n_cached, NSA_REP, tk, tq), F32)])
    return pl.pallas_call(
        functools.partial(_nsa_slc_kernel, tq=tq, tk=tk),
        grid_spec=grid_spec,
        out_shape=jax.ShapeDtypeStruct((seq, NSA_WIDTH), F32),
        compiler_params=_cparams("parallel", "arbitrary"),
        name="nsa_slc",
    )(pq_min, pk_max, consec, proj_b, proj_b, proj_b, sel, pos_row, pos_rep, tab_t)


def _nsa_win_kernel(*refs, tq, nt):
    consec_ref = refs[0]
    q_ref = refs[1]
    k_refs = refs[2:2 + nt]
    v_refs = refs[2 + nt:2 + 2 * nt]
    pk_refs = refs[2 + 2 * nt:2 + 3 * nt]
    posq_ref, tab_ref, gate_ref, gout_ref, oc_ref, os_ref, o_ref, bias_ref = refs[2 + 3 * nt:]
    qi = pl.program_id(0)
    d = NSA_HEAD_DIM
    c = d ** -0.5 * LOG2E
    sub = lax.broadcasted_iota(I32, (tq, 1), 0)
    lane = lax.broadcasted_iota(I32, (1, tq), 1)

    def band_bucket(kidx, tok, dist):
        lower = jnp.maximum(tok - (NSA_WINDOW - 1), 0)
        b = jnp.where(kidx >= lower, _t5_bucket(dist), T5_MASK_BUCKET)
        return jnp.where(kidx <= tok, b, T5_MASK_BUCKET)

    def attend(bias):
        gates = jax.nn.sigmoid(gate_ref[...])
        for g in range(NSA_KV_GROUPS):
            ks = [kr[:, g * d:(g + 1) * d] for kr in k_refs]
            vts = [vr[:, g * d:(g + 1) * d].astype(F32).T.astype(BF16) for vr in v_refs]
            for r in range(NSA_REP):
                h = g * NSA_REP + r
                hs = slice(h * d, (h + 1) * d)
                qt = (q_ref[:, hs].astype(F32) * c).T.astype(BF16)
                ts = [jnp.dot(ks[jj], qt, preferred_element_type=F32) + bias(h, jj) for jj in range(nt)]
                m = functools.reduce(jnp.maximum, [jnp.max(t, axis=0, keepdims=True) for t in ts])
                ps = [jnp.exp2(t - m) for t in ts]
                l = functools.reduce(jnp.add, [jnp.sum(p, axis=0, keepdims=True) for p in ps])
                o_t = functools.reduce(jnp.add, [jnp.dot(vt, p.astype(BF16), preferred_element_type=F32)
                                                 for vt, p in zip(vts, ps)])
                o_w = (o_t / l).T
                o = (oc_ref[:, hs] + gates[:, 3 * h + 1:3 * h + 2] * os_ref[:, hs]
                     + gates[:, 3 * h + 2:3 * h + 3] * o_w)
                o_ref[:, hs] = (o * _silu(gout_ref[:, hs])).astype(o_ref.dtype)

    consecutive = consec_ref[0] == 1

    @pl.when(jnp.logical_and(consecutive, qi == 0))
    def _():
        tok0 = (nt - 1) * tq + lane
        for jj in range(nt):
            kidx0 = jj * tq + sub
            bucket = band_bucket(kidx0, tok0, tok0 - kidx0)
            for h in range(NSA_HEADS):
                bias_ref[h, jj] = _t5_lookup(tab_ref[h:h + 1, :], bucket)

    cached = jnp.logical_and(consecutive, qi >= nt - 1)

    @pl.when(cached)
    def _():
        attend(lambda h, jj: bias_ref[h, jj])

    @pl.when(jnp.logical_not(cached))
    def _():
        tok = qi * tq + lane
        pos_q = posq_ref[...]
        buckets = []
        for jj in range(nt):
            kidx = (qi - (nt - 1) + jj) * tq + sub
            pos_k = jnp.concatenate([pk_refs[jj][...]] * (tq // LANES), axis=1)
            buckets.append(band_bucket(kidx, tok, pos_q - pos_k))
        attend(lambda h, jj: _t5_lookup(tab_ref[h:h + 1, :], buckets[jj]))


def _nsa_win(proj_b, q_cb, k_cb, v_cb, proj_f, gate_cb, gout_cb, oc, o_s, pos_row, pos_rep, consec, tab_t, *, tq):
    seq = proj_b.shape[0]
    nt = NSA_WINDOW // tq + 1

    def band_rows(jj, cb):
        return pl.BlockSpec((tq, NSA_KV), lambda i, *_: (jnp.maximum(i - (nt - 1) + jj, 0), cb))

    def band_pos(jj):
        return pl.BlockSpec((tq, LANES), lambda i, *_: (jnp.maximum(i - (nt - 1) + jj, 0), 0))

    in_specs = [pl.BlockSpec((tq, NSA_WIDTH), lambda i, *_: (i, q_cb))]
    in_specs += [band_rows(jj, k_cb) for jj in range(nt)]
    in_specs += [band_rows(jj, v_cb) for jj in range(nt)]
    in_specs += [band_pos(jj) for jj in range(nt)]
    in_specs += [pl.BlockSpec((1, tq), lambda i, *_: (0, i)),
                 pl.BlockSpec((SUBLANES, LANES), lambda i, *_: (0, 0)),
                 pl.BlockSpec((tq, LANES), lambda i, *_: (i, gate_cb)),
                 pl.BlockSpec((tq, NSA_WIDTH), lambda i, *_: (i, gout_cb)),
                 pl.BlockSpec((tq, NSA_WIDTH), lambda i, *_: (i, 0)),
                 pl.BlockSpec((tq, NSA_WIDTH), lambda i, *_: (i, 0))]
    args = [proj_b] * (1 + 2 * nt) + [pos_rep] * nt + [pos_row, tab_t * LOG2E, proj_f, proj_f, oc, o_s]
    grid_spec = pltpu.PrefetchScalarGridSpec(
        num_scalar_prefetch=1,
        grid=(seq // tq,),
        in_specs=in_specs,
        out_specs=pl.BlockSpec((tq, NSA_WIDTH), lambda i, *_: (i, 0)),
        scratch_shapes=[pltpu.VMEM((NSA_HEADS, nt, tq, tq), F32)])
    return pl.pallas_call(
        functools.partial(_nsa_win_kernel, tq=tq, nt=nt),
        grid_spec=grid_spec,
        out_shape=jax.ShapeDtypeStruct((seq, NSA_WIDTH), BF16),
        compiler_params=_cparams("arbitrary"),
        name="nsa_win",
    )(consec, *args)


def _in_proj(h, norm_g, w_b, w_f, tiles, name):
    tn_b = w_b.shape[1] // tiles["col_tiles_b"]
    tn_f = w_f.shape[1] // tiles["col_tiles_f"]
    assert tn_b % MXU_COLS == 0 and tn_f % MXU_COLS == 0
    proj_b = _norm_matmul(h, norm_g, w_b, tm=tiles["tm"], tn=tn_b, name=name + "_b", out_dtype=BF16)
    proj_f = _norm_matmul(h, norm_g, w_f, tm=tiles["tm"], tn=tn_f, name=name + "_f")
    return proj_b, proj_f


def _even_layer(h, mem_kv, norm_g, w_in, s5, w_glu, b_f, tiles):
    pb, pf = _in_proj(h, norm_g, *w_in, tiles, "in_proj_even")
    ob, of = EVEN_B_OFF, EVEN_F_OFF
    b_cat, c_cat, tab, d_skip = s5
    z = _s5_scan(pf, of["u"] // LANES, b_cat, c_cat, d_skip, tab, tc=tiles["s5_tc"])
    y_s5 = _s5_glu(z, w_glu, pf, of["g_s5"] // S5_WIDTH, tm=tiles["tm"], tn=tiles["tn_glu"])
    decay = _decay(pf, of["f"] // LANES, b_f, t=tiles["decay_t"])
    d = FOX_HEAD_DIM
    y_fox = _flash(pb, ob["q"], pb, ob["k"], pb, ob["v"], pf, of["g_fox"], heads=FOX_HEADS, dk=d, dv=d,
                   scale=d ** -0.5, tq=tiles["attn_tq"], tk=tiles["attn_tk"], hp=tiles["attn_hp"], decay=decay,
                   name="fox_attn")
    y_mem = _mem_attn(pb, ob["q_mem"] // MEM_HEAD_DIM, pf, of["g_mem"] // MEM_HEAD_DIM, mem_kv, t=tiles["mem_t"])
    return y_s5, y_fox, y_mem


def _odd_layer(h, mem_kv, norm_g, w_in, mla, nsa, pos, tiles):
    pb, pf = _in_proj(h, norm_g, *w_in, tiles, "in_proj_odd")
    ob, of = ODD_B_OFF, ODD_F_OFF
    g_cq, g_ckv, w_uq, w_ukv, freq = mla
    pos_col_f, pos_row, pos_cmp_rep, pos_rep, pq_min, pk_max, consec = pos
    q_r, k_r, v_r = _mla_up(pf, of["c_q"] // MLA_Q_RANK, of["c_kv"] // MLA_KV_RANK, of["k_rope"] // LANES,
                            g_cq, g_ckv, w_uq, w_ukv, pos_col_f, freq, t=tiles["prep_t"])
    y_mla = _flash(q_r, 0, k_r, 0, v_r, 0, pf, of["g_mla"], heads=MLA_HEADS, dk=2 * LANES, dv=MLA_V,
                   scale=(MLA_NOPE + MLA_ROPE) ** -0.5, tq=tiles["attn_tq"], tk=tiles["attn_tk"], hp=tiles["attn_hp"],
                   name="mla_attn")
    pe, w1, w2, tab_t, ov, n_slc = nsa
    kvc = _nsa_compress(pb, ob["k_cmp"] // NSA_HEAD_DIM, pe, w1, w2)
    oc, sel = _nsa_select(pb, ob["q_nsa"] // NSA_WIDTH, pf, of["gates"] // LANES, kvc[0], kvc[1], pos_row,
                          pos_cmp_rep, consec, tab_t, ov, n_slc=n_slc, tq=tiles["nsa_tq"])
    o_s = _nsa_slc(pb, ob["q_nsa"], ob["k_slc"], ob["v_slc"], sel, pos_row, pos_rep, tab_t, pq_min, pk_max, consec,
                   tq=tiles["slc_tq"], tk=tiles["slc_tk"])
    y_nsa = _nsa_win(pb, ob["q_nsa"] // NSA_WIDTH, ob["k_win"] // NSA_KV, ob["v_win"] // NSA_KV,
                     pf, of["gates"] // LANES, of["g_nsa"] // NSA_WIDTH, oc, o_s, pos_row, pos_rep, consec, tab_t,
                     tq=tiles["win_tq"])
    y_mem = _mem_attn(pb, ob["q_mem"] // MEM_HEAD_DIM, pf, of["g_mem"] // MEM_HEAD_DIM, mem_kv, t=tiles["mem_t"])
    return y_mla, y_nsa, y_mem


def _tiles(seq):
    return {"tm": min(seq, 1024), "col_tiles_b": 2, "col_tiles_f": 3, "tn_out": 512, "tn_glu": 512, "tn_mem": 512,
            "s5_tc": min(seq, 1024), "decay_t": min(seq, 512), "attn_tq": min(seq, 512),
            "attn_tk": min(seq, 512), "attn_hp": 2, "slc_tq": 256, "slc_tk": 512, "nsa_tq": 256, "win_tq": 128,
            "mem_t": min(seq, 1024), "norm_t": min(seq, 512), "prep_t": min(seq, 512)}


def _context(positions, t5_table, seq, tiles):
    pos = positions[0]
    pos_col = pos.reshape(seq, 1)
    pos_row = pos.reshape(1, seq)
    pos_rep = jnp.broadcast_to(pos_col, (seq, LANES))
    pq_min = jnp.min(pos.reshape(seq // tiles["slc_tq"], tiles["slc_tq"]), axis=1)
    pk_max = jnp.max(pos.reshape(seq // tiles["slc_tk"], tiles["slc_tk"]), axis=1)
    nc = seq // NSA_CMP_STRIDE
    pos_cmp = jnp.pad(pos[NSA_CMP_LEN - 1::NSA_CMP_STRIDE], (0, 1))
    pos_cmp_rep = jnp.broadcast_to(pos_cmp.reshape(nc, 1), (nc, LANES))
    half = MLA_ROPE // 2
    inv_freq = ROPE_THETA ** (-jnp.arange(half, dtype=F32) / half)
    freq = jnp.concatenate([inv_freq, inv_freq, jnp.zeros((LANES - MLA_ROPE,), F32)]).reshape(1, LANES)
    tab_t = jnp.pad(t5_table.astype(F32).T, ((0, SUBLANES - NSA_HEADS), (0, LANES - T5_BUCKETS)))
    tab_t = tab_t.at[:, T5_MASK_BUCKET].set(NEG_INF)
    n_slc = seq // NSA_SLC_BLOCK
    cs = np.arange(nc) * NSA_CMP_STRIDE
    ss = np.arange(LANES) * NSA_SLC_BLOCK
    ov_np = np.clip(np.minimum(cs[:, None] + NSA_CMP_LEN, ss[None, :] + NSA_SLC_BLOCK)
                    - np.maximum(cs[:, None], ss[None, :]), 0, None) / NSA_CMP_LEN
    ov_np[nc - 1, :] = 0.0
    ov_np[:, n_slc:] = 0.0
    consec = jnp.all(pos[1:] - pos[:-1] == 1).astype(I32).reshape(1)
    return {"pos": (pos_col.astype(F32), pos_row, pos_cmp_rep, pos_rep, pq_min, pk_max, consec), "freq": freq,
            "tab_t": tab_t, "ov": jnp.asarray(ov_np.T, BF16), "n_slc": n_slc}


def _odd_params(i, mla_g_cq, mla_g_ckv, mla_w_uq, mla_w_ukv, nsa_cmp_pe, nsa_cmp_w1, nsa_cmp_w2, ctx):
    dq = MLA_NOPE + MLA_ROPE
    w_uq = mla_w_uq[i].reshape(MLA_Q_RANK, MLA_HEADS, dq)
    w_uq = jnp.pad(w_uq, ((0, 0), (0, 0), (0, 2 * LANES - dq))).reshape(MLA_Q_RANK, -1).astype(BF16)
    mla = (mla_g_cq[i], mla_g_ckv[i], w_uq, mla_w_ukv[i].astype(BF16), ctx["freq"])
    nsa = (nsa_cmp_pe[i].astype(F32), nsa_cmp_w1[i], nsa_cmp_w2[i], ctx["tab_t"],
           ctx["ov"], ctx["n_slc"])
    return mla, nsa


def kernel(x, mem, positions, norm_g, mem_norm_g, final_norm_g, t5_table, w_out, mem_w_kv, even_w_in, s5_lam_re,
           s5_lam_im, s5_log_dt, s5_b_re, s5_b_im, s5_c_re, s5_c_im, s5_d, s5_w_glu, fox_b_f, odd_w_in, mla_g_cq,
           mla_g_ckv, mla_w_uq, mla_w_ukv, nsa_cmp_pe, nsa_cmp_w1, nsa_cmp_w2):
    batch, seq, _ = x.shape
    assert batch == 1 and seq % 1024 == 0 and seq // NSA_SLC_BLOCK <= LANES
    depth = norm_g.shape[0]
    tiles = _tiles(seq)
    ctx = _context(positions, t5_table, seq, tiles)
    h = x[0]
    mem_kv_all = _mem_kv_all(mem[0], mem_norm_g, mem_w_kv, tn=tiles["tn_mem"])
    for layer in range(depth):
        i = layer // 2
        mem_kv = (mem_kv_all, layer)
        if layer % 2 == 0:
            w_in = (_reorder_w_in(even_w_in[i], EVEN_SPLITS, EVEN_B_ORDER),
                    _reorder_w_in(even_w_in[i], EVEN_SPLITS, EVEN_F_ORDER))
            b_cat, c_cat, tab = _s5_prepare(s5_lam_re[i], s5_lam_im[i], s5_log_dt[i], s5_b_re[i], s5_b_im[i],
                                            s5_c_re[i], s5_c_im[i])
            ys = _even_layer(h, mem_kv, norm_g[layer], w_in, (b_cat, c_cat, tab, s5_d[i]),
                             s5_w_glu[i], fox_b_f[i], tiles)
        else:
            w_in = (_reorder_w_in(odd_w_in[i], ODD_SPLITS, ODD_B_ORDER),
                    _reorder_w_in(odd_w_in[i], ODD_SPLITS, ODD_F_ORDER))
            mla, nsa = _odd_params(i, mla_g_cq, mla_g_ckv, mla_w_uq, mla_w_ukv, nsa_cmp_pe, nsa_cmp_w1, nsa_cmp_w2,
                                   ctx)
            ys = _odd_layer(h, mem_kv, norm_g[layer], w_in, mla, nsa, ctx["pos"], tiles)
        h = _out_proj(h, ys, w_out, layer, tm=tiles["tm"], tn=tiles["tn_out"])
    return _final_norm(h, final_norm_g, tm=tiles["norm_t"])[None]
```

```python
import functools
import math

import numpy as np
import jax
import jax.numpy as jnp
from jax import lax
from jax.experimental import pallas as pl
from jax.experimental.pallas import tpu as pltpu

F32 = jnp.float32
BF16 = jnp.bfloat16
I32 = jnp.int32

RMS_EPS = 1e-6
NEG_INF = -1e30
LOG2E = math.log2(math.e)

S5_WIDTH = 1024
S5_GROUP = 16
S5_GROUPS = S5_WIDTH // S5_GROUP
S5_STATE = 64
FOX_HEADS = 8
FOX_HEAD_DIM = 128
FOX_WIDTH = FOX_HEADS * FOX_HEAD_DIM
MEM_HEADS = 4
MEM_HEAD_DIM = 128
MEM_WIDTH = MEM_HEADS * MEM_HEAD_DIM
MLA_HEADS = 8
MLA_Q_RANK = 512
MLA_KV_RANK = 512
MLA_NOPE = 128
MLA_ROPE = 64
MLA_V = 128
MLA_WIDTH = MLA_HEADS * MLA_V
ROPE_THETA = 10000.0
NSA_HEADS = 8
NSA_KV_GROUPS = 2
NSA_REP = NSA_HEADS // NSA_KV_GROUPS
NSA_HEAD_DIM = 128
NSA_WIDTH = NSA_HEADS * NSA_HEAD_DIM
NSA_KV = NSA_KV_GROUPS * NSA_HEAD_DIM
NSA_CMP_LEN = 32
NSA_CMP_STRIDE = 16
NSA_CMP_HIDDEN = 256
NSA_SLC_BLOCK = 64
NSA_SLC_TOPK = 16
NSA_WINDOW = 512
FORCE_SCORE = 1e6
T5_BUCKETS = 32
T5_MAX_DIST = 1024

EVEN_SPLITS = (S5_WIDTH, S5_WIDTH, FOX_WIDTH, FOX_WIDTH, FOX_WIDTH, FOX_HEADS, FOX_WIDTH, MEM_WIDTH, MEM_WIDTH)
ODD_SPLITS = (MLA_Q_RANK, MLA_KV_RANK, MLA_ROPE, MLA_WIDTH, NSA_WIDTH, NSA_KV, NSA_KV, NSA_KV, NSA_KV, NSA_KV,
              NSA_KV, 3 * NSA_HEADS, NSA_WIDTH, MEM_WIDTH, MEM_WIDTH)

LANES = 128
SUBLANES = 8
MXU_COLS = 256
VMEM_LIMIT_BYTES = 56 * 1024 * 1024

EVEN_B_ORDER = (("q", 2, 1024), ("k", 3, 1024), ("v", 4, 1024), ("q_mem", 7, 512))
EVEN_F_ORDER = (("u", 0, 1024), ("g_s5", 1, 1024), ("g_fox", 6, 1024), ("g_mem", 8, 512), ("f", 5, 128),
                ("pad", None, 128))
ODD_B_ORDER = (("q_nsa", 4, 1024), ("q_mem", 13, 512), ("k_cmp", 5, 256), ("v_cmp", 6, 256), ("k_slc", 7, 256),
               ("v_slc", 8, 256), ("k_win", 9, 256), ("v_win", 10, 256))
ODD_F_ORDER = (("g_nsa", 12, 1024), ("g_mla", 3, 1024), ("c_q", 0, 512), ("c_kv", 1, 512), ("g_mem", 14, 512),
               ("k_rope", 2, 128), ("gates", 11, 128))


def _layout(order):
    off, out = 0, {}
    for name, _, width in order:
        assert off % width == 0
        out[name] = off
        off += width
    return out, off


EVEN_B_OFF, EVEN_B_N = _layout(EVEN_B_ORDER)
EVEN_F_OFF, EVEN_F_N = _layout(EVEN_F_ORDER)
ODD_B_OFF, ODD_B_N = _layout(ODD_B_ORDER)
ODD_F_OFF, ODD_F_N = _layout(ODD_F_ORDER)


def _reorder_w_in(w, splits, order):
    starts = np.concatenate([[0], np.cumsum(splits)])
    cols = []
    for _, idx, width in order:
        if idx is None:
            cols.append(jnp.zeros((w.shape[0], width), w.dtype))
            continue
        seg = w[:, int(starts[idx]):int(starts[idx + 1])]
        pad = width - seg.shape[1]
        if pad:
            seg = jnp.pad(seg, ((0, 0), (0, pad)))
        cols.append(seg)
    return jnp.concatenate(cols, axis=1).astype(BF16)


def _cparams(*sem):
    return pltpu.CompilerParams(dimension_semantics=sem, vmem_limit_bytes=VMEM_LIMIT_BYTES)


def _silu(g):
    return g * jax.nn.sigmoid(g)


def _norm_matmul_kernel(x_ref, g_ref, w_ref, o_ref, xn_ref):
    @pl.when(pl.program_id(1) == 0)
    def _():
        x = x_ref[...]
        ms = jnp.mean(x * x, axis=-1, keepdims=True)
        xn_ref[...] = (x * lax.rsqrt(ms + RMS_EPS) * g_ref[...]).astype(BF16)

    o_ref[...] = jnp.dot(xn_ref[...], w_ref[...].astype(BF16), preferred_element_type=F32).astype(o_ref.dtype)


def _norm_matmul(x, g, w, *, x_cb=0, tm, tn, name, out_dtype=F32):
    m = x.shape[0]
    k, n = w.shape
    return pl.pallas_call(
        _norm_matmul_kernel,
        grid=(m // tm, n // tn),
        in_specs=[pl.BlockSpec((tm, k), lambda i, j: (i, x_cb)),
                  pl.BlockSpec((1, k), lambda i, j: (0, 0)),
                  pl.BlockSpec((k, tn), lambda i, j: (0, j))],
        out_specs=pl.BlockSpec((tm, tn), lambda i, j: (i, j)),
        out_shape=jax.ShapeDtypeStruct((m, n), out_dtype),
        scratch_shapes=[pltpu.VMEM((tm, k), BF16)],
        compiler_params=_cparams("parallel", "arbitrary"),
        name=name,
    )(x, g.reshape(1, k), w)


def _out_proj_kernel(h_ref, *refs):
    o_ref = refs[-1]
    n = (len(refs) - 1) // 2
    acc = h_ref[...]
    for y_ref, w_ref in zip(refs[:n], refs[n:2 * n]):
        acc = acc + jnp.dot(y_ref[...], w_ref[...].astype(BF16), preferred_element_type=F32)
    o_ref[...] = acc


def _out_proj(h, ys, w_all, layer, *, tm, tn):
    m, n = h.shape
    in_specs = [pl.BlockSpec((tm, tn), lambda i, j: (i, j))]
    in_specs += [pl.BlockSpec((tm, y.shape[1]), lambda i, j: (i, 0)) for y in ys]
    row = 0
    for y in ys:
        width = y.shape[1]
        assert row % width == 0
        in_specs.append(pl.BlockSpec((None, width, tn), lambda i, j, rb=row // width: (layer, rb, j)))
        row += width
    return pl.pallas_call(
        _out_proj_kernel,
        grid=(m // tm, n // tn),
        in_specs=in_specs,
        out_specs=pl.BlockSpec((tm, tn), lambda i, j: (i, j)),
        out_shape=jax.ShapeDtypeStruct((m, n), F32),
        compiler_params=_cparams("parallel", "arbitrary"),
        name="out_proj",
    )(h, *ys, *([w_all] * len(ys)))


def _final_norm_kernel(x_ref, g_ref, o_ref):
    x = x_ref[...]
    ms = jnp.mean(x * x, axis=-1, keepdims=True)
    o_ref[...] = x * lax.rsqrt(ms + RMS_EPS) * g_ref[...]


def _final_norm(h, g, *, tm):
    m, n = h.shape
    return pl.pallas_call(
        _final_norm_kernel,
        grid=(m // tm,),
        in_specs=[pl.BlockSpec((tm, n), lambda i: (i, 0)), pl.BlockSpec((1, n), lambda i: (0, 0))],
        out_specs=pl.BlockSpec((tm, n), lambda i: (i, 0)),
        out_shape=jax.ShapeDtypeStruct((m, n), F32),
        compiler_params=_cparams("parallel"),
        name="final_norm",
    )(h, g.reshape(1, n))


ONES_ROWS = 16


def _build_vt(v_ref, vt_ref, hh, dv, seq, chunk):
    def body(c, carry):
        st = pl.multiple_of(c * chunk, chunk)
        vt_ref[hh, 0:dv, pl.ds(st, chunk)] = v_ref[pl.ds(st, chunk), hh * dv:(hh + 1) * dv].astype(F32).T.astype(BF16)
        return carry

    lax.fori_loop(0, seq // chunk, body, 0)
    extra = vt_ref.shape[1] - dv
    if extra:
        vt_ref[hh, dv:dv + extra, :] = jnp.ones((extra, seq), BF16)


def _flash_kernel(*refs, scale, tq, tk, hp, dk, dv, has_decay):
    cw = min(tq, MXU_COLS)
    ncg = tq // cw
    streams = [(hh, cg) for hh in range(hp) for cg in range(ncg)]
    if has_decay:
        q_ref, k_ref, v_ref, g_ref, cq_ref, ck_ref, o_ref, vt_ref, qt_ref, s_ref, m_ref, acc_ref = refs
    else:
        q_ref, k_ref, v_ref, g_ref, o_ref, vt_ref, qt_ref, s_ref, m_ref, acc_ref = refs
    hb = pl.program_id(0)
    qi = pl.program_id(1)
    seq = k_ref.shape[0]

    @pl.when(qi == 0)
    def _():
        for hh in range(hp):
            _build_vt(v_ref, vt_ref, hh, dv, seq, tk)

    for v, (hh, cg) in enumerate(streams):
        qt_ref[v] = (q_ref[cg * cw:(cg + 1) * cw, hh * dk:(hh + 1) * dk].astype(F32) * (scale * LOG2E)).T.astype(BF16)
    m_ref[...] = jnp.full(m_ref.shape, NEG_INF, F32)
    acc_ref[...] = jnp.zeros(acc_ref.shape, F32)
    n_full = (qi * tq) // tk
    if has_decay:
        cq2 = [cq_ref[pl.ds(hb * hp + hh, 1), cg * cw:(cg + 1) * cw] for hh, cg in streams]

    def scores(j, slot):
        start = pl.multiple_of(j * tk, tk)
        for v, (hh, cg) in enumerate(streams):
            s_ref[slot, v] = jnp.dot(k_ref[pl.ds(start, tk), hh * dk:(hh + 1) * dk], qt_ref[v],
                                     preferred_element_type=F32)

    def softmax_pv(j, slot, masked):
        start = pl.multiple_of(j * tk, tk)
        for v, (hh, cg) in enumerate(streams):
            t = s_ref[slot, v]
            if has_decay:
                t = t - jnp.concatenate([ck_ref[hh, pl.ds(start, tk), :]] * (cw // LANES), axis=1)
            if masked:
                key = start + lax.broadcasted_iota(I32, (tk, cw), 0)
                qry = qi * tq + cg * cw + lax.broadcasted_iota(I32, (tk, cw), 1)
                t = jnp.where(key <= qry, t, NEG_INF)
            m_prev = m_ref[v]
            mx = jnp.max(t, axis=0, keepdims=True)
            if has_decay:
                m_new = jnp.maximum(m_prev, mx + cq2[v])
                shift = m_new - cq2[v]
            else:
                m_new = jnp.maximum(m_prev, mx)
                shift = m_new
            alpha = jnp.exp2(m_prev - m_new)
            p = jnp.exp2(t - shift).astype(BF16)
            acc_ref[v] = alpha * acc_ref[v] + jnp.dot(vt_ref[hh, :, pl.ds(start, tk)], p,
                                                       preferred_element_type=F32)
            m_ref[v] = m_new

    scores(0, 0)

    def pair(j):
        scores(j + 1, 1)
        softmax_pv(j, 0, False)
        scores(j + 2, 0)
        softmax_pv(j + 1, 1, False)

    def quad(qq, carry):
        pair(4 * qq)
        pair(4 * qq + 2)
        return carry

    quads = n_full // 4
    lax.fori_loop(0, quads, quad, 0)
    rem = n_full - 4 * quads

    @pl.when(rem >= 2)
    def _():
        pair(4 * quads)

    last = 4 * quads + 2 * (rem // 2)

    @pl.when(n_full % 2 == 1)
    def _():
        scores(last + 1, 1)
        softmax_pv(last, 0, False)
        softmax_pv(last + 1, 1, True)

    @pl.when(n_full % 2 == 0)
    def _():
        softmax_pv(last, 0, True)

    for v, (hh, cg) in enumerate(streams):
        a = acc_ref[v]
        o = (a[0:dv, :] / a[dv:dv + 1, :]).T
        rows = slice(cg * cw, (cg + 1) * cw)
        cols = slice(hh * dv, (hh + 1) * dv)
        o_ref[rows, cols] = (o * _silu(g_ref[rows, cols])).astype(o_ref.dtype)


def _flash(q_arr, q_off, k_arr, k_off, v_arr, v_off, g_arr, g_off, *, heads, dk, dv, scale, tq, tk, hp,
           decay=None, name):
    seq = q_arr.shape[0]
    cw = min(tq, MXU_COLS)
    ns = hp * (tq // cw)
    once = pl.Buffered(1)
    in_specs = [pl.BlockSpec((tq, hp * dk), lambda h, i: (i, q_off // (hp * dk) + h)),
                pl.BlockSpec((seq, hp * dk), lambda h, i: (0, k_off // (hp * dk) + h), pipeline_mode=once),
                pl.BlockSpec((seq, hp * dv), lambda h, i: (0, v_off // (hp * dv) + h), pipeline_mode=once),
                pl.BlockSpec((tq, hp * dv), lambda h, i: (i, g_off // (hp * dv) + h))]
    args = [q_arr, k_arr, v_arr, g_arr]
    if decay is not None:
        cum_t, cum_rep = decay
        in_specs += [pl.BlockSpec((SUBLANES, tq), lambda h, i: (0, i)),
                     pl.BlockSpec((hp, seq, LANES), lambda h, i: (h, 0, 0), pipeline_mode=once)]
        args += [cum_t, cum_rep]
    return pl.pallas_call(
        functools.partial(_flash_kernel, scale=scale, tq=tq, tk=tk, hp=hp, dk=dk, dv=dv, has_decay=decay is not None),
        grid=(heads // hp, seq // tq),
        in_specs=in_specs,
        out_specs=pl.BlockSpec((tq, hp * dv), lambda h, i: (i, h)),
        out_shape=jax.ShapeDtypeStruct((seq, heads * dv), BF16),
        scratch_shapes=[pltpu.VMEM((hp, dv + ONES_ROWS, seq), BF16), pltpu.VMEM((ns, dk, cw), BF16),
                        pltpu.VMEM((2, ns, tk, cw), F32), pltpu.VMEM((ns, 1, cw), F32),
                        pltpu.VMEM((ns, dv + ONES_ROWS, cw), F32)],
        compiler_params=_cparams("parallel", "arbitrary"),
        name=name,
    )(*args)


def _decay_kernel(f_ref, b_ref, ct_ref, cr_ref, carry_ref, *, t):
    i = pl.program_id(0)

    @pl.when(i == 0)
    def _():
        carry_ref[...] = jnp.zeros(carry_ref.shape, F32)

    x = f_ref[...] + b_ref[...]
    lf = jnp.minimum(x, 0.0) - jnp.log1p(jnp.exp(-jnp.abs(x)))
    row = lax.broadcasted_iota(I32, lf.shape, 0)
    s = 1
    while s < t:
        lf = lf + jnp.where(row >= s, pltpu.roll(lf, s, 0), 0.0)
        s *= 2
    lf = lf + carry_ref[...]
    carry_ref[...] = lf[t - 1:t, :]
    lf2 = lf * LOG2E
    ct_ref[...] = lf2.T[:FOX_HEADS, :]
    for h in range(FOX_HEADS):
        cr_ref[h] = jnp.broadcast_to(lf2[:, h:h + 1], (t, LANES))


def _decay(proj, f_cb, b_f, *, t):
    seq = proj.shape[0]
    b = jnp.pad(b_f.reshape(1, FOX_HEADS), ((0, 0), (0, LANES - FOX_HEADS)))
    return pl.pallas_call(
        functools.partial(_decay_kernel, t=t),
        grid=(seq // t,),
        in_specs=[pl.BlockSpec((t, LANES), lambda i: (i, f_cb)), pl.BlockSpec((1, LANES), lambda i: (0, 0))],
        out_specs=[pl.BlockSpec((FOX_HEADS, t), lambda i: (0, i)),
                   pl.BlockSpec((FOX_HEADS, t, LANES), lambda i: (0, i, 0))],
        out_shape=[jax.ShapeDtypeStruct((FOX_HEADS, seq), F32), jax.ShapeDtypeStruct((FOX_HEADS, seq, LANES), F32)],
        scratch_shapes=[pltpu.VMEM((1, LANES), F32)],
        compiler_params=_cparams("arbitrary"),
        name="fox_decay",
    )(proj, b)


def _mem_attn_kernel(q_ref, kv_ref, g_ref, o_ref):
    d = MEM_HEAD_DIM
    for h in range(MEM_HEADS):
        hs = slice(h * d, (h + 1) * d)
        k = kv_ref[:, hs].astype(BF16)
        v = kv_ref[:, MEM_WIDTH + h * d:MEM_WIDTH + (h + 1) * d].astype(BF16)
        s = lax.dot_general(q_ref[:, hs], k, (((1,), (1,)), ((), ())), preferred_element_type=F32) * (d ** -0.5)
        m = jnp.max(s, axis=1, keepdims=True)
        p = jnp.exp(s - m)
        l = jnp.sum(p, axis=1, keepdims=True)
        o = jnp.dot(p.astype(BF16), v, preferred_element_type=F32) / l
        o_ref[:, hs] = (o * _silu(g_ref[:, hs])).astype(o_ref.dtype)


def _mem_kv_all(mem2d, g, w_all, *, tn):
    depth, k, n = w_all.shape
    m = mem2d.shape[0]
    return pl.pallas_call(
        _norm_matmul_kernel,
        grid=(depth, n // tn),
        in_specs=[pl.BlockSpec((m, k), lambda l, j: (0, 0)),
                  pl.BlockSpec((1, k), lambda l, j: (0, 0)),
                  pl.BlockSpec((None, k, tn), lambda l, j: (l, 0, j))],
        out_specs=pl.BlockSpec((None, m, tn), lambda l, j: (l, 0, j)),
        out_shape=jax.ShapeDtypeStruct((depth, m, n), F32),
        scratch_shapes=[pltpu.VMEM((m, k), BF16)],
        compiler_params=_cparams("arbitrary", "arbitrary"),
        name="mem_kv",
    )(mem2d, g.reshape(1, k), w_all)


def _mem_attn(proj_b, q_cb, proj_f, g_cb, mem_kv, *, t):
    seq = proj_b.shape[0]
    mem_kv, layer = mem_kv
    nm = mem_kv.shape[1]
    w = MEM_WIDTH
    d = MEM_HEAD_DIM
    return pl.pallas_call(
        _mem_attn_kernel,
        grid=(seq // t,),
        in_specs=[pl.BlockSpec((t, w), lambda i: (i, q_cb * d // w)),
                  pl.BlockSpec((None, nm, 2 * w), lambda i: (layer, 0, 0)),
                  pl.BlockSpec((t, w), lambda i: (i, g_cb * d // w))],
        out_specs=pl.BlockSpec((t, w), lambda i: (i, 0)),
        out_shape=jax.ShapeDtypeStruct((seq, w), BF16),
        compiler_params=_cparams("parallel"),
        name="mem_attn",
    )(proj_b, mem_kv, proj_f)


S5_TILE_GROUPS = LANES // S5_GROUP
S5_TILE_STATES = S5_TILE_GROUPS * S5_STATE
S5_TILES = S5_GROUPS // S5_TILE_GROUPS


def _s5_scan_kernel(u_ref, b_ref, c_ref, d_ref, tab_ref, z_ref, bu_ref, carry_ref, *, tc):
    ns = S5_TILE_STATES

    @pl.when(pl.program_id(1) == 0)
    def _():
        carry_ref[...] = jnp.zeros(carry_ref.shape, F32)

    u = u_ref[...]
    bu_ref[...] = jnp.dot(u.astype(BF16), b_ref[...], preferred_element_type=F32)
    steps = [(1, tab_ref[0], tab_ref[1]), (2, tab_ref[2], tab_ref[3]), (4, tab_ref[4], tab_ref[5])]
    pr = tab_ref[6]
    pi = tab_ref[7]

    def body(i, carry):
        cr, ci = carry
        r0 = pl.multiple_of(i * SUBLANES, SUBLANES)
        xr = bu_ref[pl.ds(r0, SUBLANES), 0:ns]
        xi = bu_ref[pl.ds(r0, SUBLANES), ns:2 * ns]
        for s, ar, ai in steps:
            sr = pltpu.roll(xr, s, 0)
            si = pltpu.roll(xi, s, 0)
            xr, xi = xr + ar * sr - ai * si, xi + ar * si + ai * sr
        xr, xi = xr + pr * cr - pi * ci, xi + pr * ci + pi * cr
        bu_ref[pl.ds(r0, SUBLANES), 0:ns] = xr
        bu_ref[pl.ds(r0, SUBLANES), ns:2 * ns] = xi
        return xr[SUBLANES - 1:SUBLANES, :], xi[SUBLANES - 1:SUBLANES, :]

    cr, ci = lax.fori_loop(0, tc // SUBLANES, body, (carry_ref[0:1, 0:ns], carry_ref[0:1, ns:2 * ns]))
    carry_ref[0:1, 0:ns] = cr
    carry_ref[0:1, ns:2 * ns] = ci
    y = jnp.dot(bu_ref[...].astype(BF16), c_ref[...], preferred_element_type=F32) + d_ref[...] * u
    z_ref[...] = jax.nn.gelu(y)


def _s5_prepare(lam_re, lam_im, log_dt, b_re, b_im, c_re, c_im):
    dt = jnp.exp(log_dt.astype(F32))[:, None]
    lr = lam_re.astype(F32)
    li = lam_im.astype(F32)
    mag = jnp.exp(lr * dt)
    ab_re = mag * jnp.cos(li * dt)
    ab_im = mag * jnp.sin(li * dt)
    den = lr * lr + li * li
    nr = ab_re - 1.0
    f_re = (nr * lr + ab_im * li) / den
    f_im = (ab_im * lr - nr * li) / den
    br = b_re.astype(F32)
    bim = b_im.astype(F32)
    bb_re = f_re[..., None] * br - f_im[..., None] * bim
    bb_im = f_re[..., None] * bim + f_im[..., None] * br
    eye = jnp.eye(S5_TILE_GROUPS, dtype=F32)

    def blockdiag_in(bb):
        t = bb.reshape(S5_TILES, S5_TILE_GROUPS, S5_STATE, S5_GROUP)
        m = jnp.einsum("jgpc,gh->jgchp", t, eye)
        return m.reshape(S5_TILES, LANES, S5_TILE_STATES)

    def blockdiag_out(cc):
        t = cc.reshape(S5_TILES, S5_TILE_GROUPS, S5_GROUP, S5_STATE)
        m = jnp.einsum("jgcp,gh->jgphc", t, eye)
        return m.reshape(S5_TILES, S5_TILE_STATES, LANES)

    b_cat = jnp.concatenate([blockdiag_in(bb_re), blockdiag_in(bb_im)], axis=2).astype(BF16)
    c_cat = jnp.concatenate([blockdiag_out(c_re.astype(F32)), -blockdiag_out(c_im.astype(F32))], axis=1).astype(BF16)

    a_r = ab_re.reshape(S5_TILES, 1, S5_TILE_STATES)
    a_i = ab_im.reshape(S5_TILES, 1, S5_TILE_STATES)

    def cmul(xr, xi, yr, yi):
        return xr * yr - xi * yi, xr * yi + xi * yr

    a2 = cmul(a_r, a_i, a_r, a_i)
    a4 = cmul(*a2, *a2)
    row = jnp.arange(SUBLANES)[None, :, None]
    tabs = []
    for s, (pr_, pi_) in ((1, (a_r, a_i)), (2, a2), (4, a4)):
        tabs.append(jnp.where(row >= s, pr_, 0.0))
        tabs.append(jnp.where(row >= s, pi_, 0.0))
    pw = [(a_r, a_i)]
    for _ in range(SUBLANES - 1):
        pw.append(cmul(*pw[-1], a_r, a_i))
    tabs.append(jnp.concatenate([p[0] for p in pw], axis=1))
    tabs.append(jnp.concatenate([p[1] for p in pw], axis=1))
    tab = jnp.stack([jnp.broadcast_to(t, (S5_TILES, SUBLANES, S5_TILE_STATES)) for t in tabs], axis=1)
    return b_cat, c_cat, tab.astype(F32)


def _s5_scan(proj, u_cb, b_cat, c_cat, d_skip, tab, *, tc):
    seq = proj.shape[0]
    ns = S5_TILE_STATES
    d = d_skip.astype(F32).reshape(S5_TILES, 1, LANES)
    return pl.pallas_call(
        functools.partial(_s5_scan_kernel, tc=tc),
        grid=(S5_TILES, seq // tc),
        in_specs=[pl.BlockSpec((tc, LANES), lambda j, c: (c, u_cb + j)),
                  pl.BlockSpec((None, LANES, 2 * ns), lambda j, c: (j, 0, 0)),
                  pl.BlockSpec((None, 2 * ns, LANES), lambda j, c: (j, 0, 0)),
                  pl.BlockSpec((None, 1, LANES), lambda j, c: (j, 0, 0)),
                  pl.BlockSpec((None, 8, SUBLANES, ns), lambda j, c: (j, 0, 0, 0))],
        out_specs=pl.BlockSpec((tc, LANES), lambda j, c: (c, j)),
        out_shape=jax.ShapeDtypeStruct((seq, S5_WIDTH), F32),
        scratch_shapes=[pltpu.VMEM((tc, 2 * ns), F32), pltpu.VMEM((SUBLANES, 2 * ns), F32)],
        compiler_params=_cparams("parallel", "arbitrary"),
        name="s5_scan",
    )(proj, b_cat, c_cat, d, tab)


def _s5_glu_kernel(z_ref, w_ref, g_ref, o_ref, *, tn):
    j = pl.program_id(1)
    z = z_ref[...]
    a = jnp.dot(z.astype(BF16), w_ref[...].astype(BF16), preferred_element_type=F32)
    zc = z_ref[:, pl.ds(pl.multiple_of(j * tn, tn), tn)]
    o_ref[...] = (zc * jax.nn.sigmoid(a) * _silu(g_ref[...])).astype(o_ref.dtype)


def _s5_glu(z, w_glu, proj, g_cb, *, tm, tn):
    seq, n = z.shape
    return pl.pallas_call(
        functools.partial(_s5_glu_kernel, tn=tn),
        grid=(seq // tm, n // tn),
        in_specs=[pl.BlockSpec((tm, n), lambda i, j: (i, 0)),
                  pl.BlockSpec((n, tn), lambda i, j: (0, j)),
                  pl.BlockSpec((tm, tn), lambda i, j: (i, g_cb * (n // tn) + j))],
        out_specs=pl.BlockSpec((tm, tn), lambda i, j: (i, j)),
        out_shape=jax.ShapeDtypeStruct((seq, n), BF16),
        compiler_params=_cparams("parallel", "arbitrary"),
        name="s5_glu",
    )(z, w_glu, proj)


def _rope_tables(pos, freq):
    ang = pos * freq
    lane = lax.broadcasted_iota(I32, ang.shape, 1)
    half = MLA_ROPE // 2
    cos = jnp.cos(ang)
    sin = jnp.sin(ang)
    c = jnp.where(lane < MLA_ROPE, cos, 0.0)
    s1 = jnp.where(lane < half, -sin, 0.0)
    s2 = jnp.where((lane >= half) & (lane < MLA_ROPE), sin, 0.0)
    return c, s1, s2


def _rope_apply(x, c, s1, s2):
    half = MLA_ROPE // 2
    return x * c + pltpu.roll(x, LANES - half, 1) * s1 + pltpu.roll(x, half, 1) * s2


def _mla_up_kernel(cq_ref, ckv_ref, kr_ref, pos_ref, freq_ref, gq_ref, gkv_ref, wq_ref, wkv_ref, q_ref, k_ref, v_ref):
    def normed(x_ref, g_ref):
        x = x_ref[...]
        ms = jnp.mean(x * x, axis=-1, keepdims=True)
        return (x * lax.rsqrt(ms + RMS_EPS) * g_ref[...]).astype(BF16)

    qf = jnp.dot(normed(cq_ref, gq_ref), wq_ref[...], preferred_element_type=F32)
    kvf = jnp.dot(normed(ckv_ref, gkv_ref), wkv_ref[...], preferred_element_type=F32)
    c, s1, s2 = _rope_tables(pos_ref[...], freq_ref[...])
    kr = _rope_apply(kr_ref[...], c, s1, s2).astype(BF16)
    for h in range(MLA_HEADS):
        b = 2 * LANES * h
        q_ref[:, b:b + LANES] = qf[:, b:b + LANES].astype(BF16)
        q_ref[:, b + LANES:b + 2 * LANES] = _rope_apply(qf[:, b + LANES:b + 2 * LANES], c, s1, s2).astype(BF16)
        k_ref[:, b:b + LANES] = kvf[:, b:b + LANES].astype(BF16)
        k_ref[:, b + LANES:b + 2 * LANES] = kr
        v_ref[:, LANES * h:LANES * (h + 1)] = kvf[:, b + LANES:b + 2 * LANES].astype(BF16)


def _mla_up(proj, cq_cb, ckv_cb, kr_cb, g_cq, g_ckv, w_uq, w_ukv, pos_col, freq, *, t):
    seq = proj.shape[0]
    w = 2 * LANES * MLA_HEADS
    rq, rkv = w_uq.shape[0], w_ukv.shape[0]
    return pl.pallas_call(
        _mla_up_kernel,
        grid=(seq // t,),
        in_specs=[pl.BlockSpec((t, rq), lambda i: (i, cq_cb)),
                  pl.BlockSpec((t, rkv), lambda i: (i, ckv_cb)),
                  pl.BlockSpec((t, LANES), lambda i: (i, kr_cb)),
                  pl.BlockSpec((t, 1), lambda i: (i, 0)),
                  pl.BlockSpec((1, LANES), lambda i: (0, 0)),
                  pl.BlockSpec((1, rq), lambda i: (0, 0)),
                  pl.BlockSpec((1, rkv), lambda i: (0, 0)),
                  pl.BlockSpec((rq, w), lambda i: (0, 0)),
                  pl.BlockSpec((rkv, w), lambda i: (0, 0))],
        out_specs=[pl.BlockSpec((t, w), lambda i: (i, 0)),
                   pl.BlockSpec((t, w), lambda i: (i, 0)),
                   pl.BlockSpec((t, MLA_WIDTH), lambda i: (i, 0))],
        out_shape=[jax.ShapeDtypeStruct((seq, w), BF16), jax.ShapeDtypeStruct((seq, w), BF16),
                   jax.ShapeDtypeStruct((seq, MLA_WIDTH), BF16)],
        compiler_params=_cparams("parallel"),
        name="mla_up",
    )(proj, proj, proj, pos_col, freq, g_cq.reshape(1, rq), g_ckv.reshape(1, rkv), w_uq, w_ukv)


def _t5_bucket(dist):
    n = jnp.maximum(dist, 0)
    max_exact = T5_BUCKETS // 2
    log_ratio = jnp.log(jnp.maximum(n, 1).astype(F32) / max_exact) / math.log(T5_MAX_DIST / max_exact)
    large = jnp.minimum(max_exact + (log_ratio * (T5_BUCKETS - max_exact)).astype(I32), T5_BUCKETS - 1)
    return jnp.where(n < max_exact, n, large)


T5_MASK_BUCKET = T5_BUCKETS


def _t5_lookup(table_row, bucket):
    rows, width = bucket.shape
    tab = jnp.broadcast_to(table_row, (rows, LANES))
    parts = [jnp.take_along_axis(tab, bucket[:, c:c + LANES], axis=1, mode="promise_in_bounds")
             for c in range(0, width, LANES)]
    return parts[0] if len(parts) == 1 else jnp.concatenate(parts, axis=1)


def _nsa_cmp_kernel(x_ref, pe_ref, w1_ref, w2_ref, o_ref, xf_ref, *, nc):
    half = NSA_CMP_LEN // 2
    d = NSA_HEAD_DIM
    xf_ref[...] = x_ref[...].astype(F32)
    u = jnp.zeros((nc, NSA_CMP_HIDDEN), F32)
    v = jnp.zeros((nc, NSA_CMP_HIDDEN), F32)
    for r in range(half):
        a = xf_ref[pl.ds(r, nc, stride=NSA_CMP_STRIDE), :]
        u = u + jnp.dot((a + pe_ref[r:r + 1, :]).astype(BF16), w1_ref[r * d:(r + 1) * d, :].astype(BF16),
                        preferred_element_type=F32)
        v = v + jnp.dot((a + pe_ref[half + r:half + r + 1, :]).astype(BF16),
                        w1_ref[(half + r) * d:(half + r + 1) * d, :].astype(BF16), preferred_element_type=F32)
    hid = u + pltpu.roll(v, nc - 1, 0)
    o_ref[...] = jnp.dot(jax.nn.gelu(hid).astype(BF16), w2_ref[...].astype(BF16),
                         preferred_element_type=F32).astype(o_ref.dtype)


def _nsa_compress(proj, k_cb, pe, w1, w2):
    seq = proj.shape[0]
    nc = seq // NSA_CMP_STRIDE
    d = NSA_HEAD_DIM
    g = NSA_KV_GROUPS
    return pl.pallas_call(
        functools.partial(_nsa_cmp_kernel, nc=nc),
        grid=(2, g),
        in_specs=[pl.BlockSpec((seq, d), lambda a, b: (0, k_cb + a * g + b)),
                  pl.BlockSpec((None, NSA_CMP_LEN, d), lambda a, b: (a, 0, 0)),
                  pl.BlockSpec((None, NSA_CMP_LEN * d, NSA_CMP_HIDDEN), lambda a, b: (a, 0, 0)),
                  pl.BlockSpec((None, NSA_CMP_HIDDEN, d), lambda a, b: (a, 0, 0))],
        out_specs=pl.BlockSpec((None, None, nc, d), lambda a, b: (a, b, 0, 0)),
        out_shape=jax.ShapeDtypeStruct((2, g, nc, d), BF16),
        scratch_shapes=[pltpu.VMEM((seq, d), F32)],
        compiler_params=_cparams("parallel", "arbitrary"),
        name="nsa_compress",
    )(proj, pe, w1, w2)


def _nsa_select_kernel(consec_ref, q_ref, kc_ref, vc_ref, posq_ref, posc_ref, tab_ref, gate_ref, ov_ref,
                       oc_ref, sel_ref, strip_ref, *, nc, n_slc, tq):
    qi = pl.program_id(0)
    d = NSA_HEAD_DIM
    c = d ** -0.5 * LOG2E
    per_block = tq // NSA_CMP_STRIDE
    lane = lax.broadcasted_iota(I32, (1, tq), 1)
    tok = qi * tq + lane

    def select(bias):
        gates = jax.nn.sigmoid(gate_ref[...])
        ovt = ov_ref[...]
        js = lax.broadcasted_iota(I32, (LANES, tq), 0)
        jf = js.astype(F32)
        cur = tok // NSA_SLC_BLOCK
        forced = (js == 0) | (js == cur) | (js == cur - 1)
        for g in range(NSA_KV_GROUPS):
            kc = kc_ref[g]
            vct = vc_ref[g].astype(F32).T.astype(BF16)
            psum = jnp.zeros((nc, tq), F32)
            for r in range(NSA_REP):
                h = g * NSA_REP + r
                qt = (q_ref[:, h * d:(h + 1) * d].astype(F32) * c).T.astype(BF16)
                t = jnp.dot(kc, qt, preferred_element_type=F32) + bias(h)
                m = jnp.max(t, axis=0, keepdims=True)
                e = jnp.exp2(t - m)
                l = jnp.sum(e, axis=0, keepdims=True)
                p = e * jnp.where(m > 0.5 * NEG_INF, 1.0 / l, 0.0)
                o = jnp.dot(vct, p.astype(BF16), preferred_element_type=F32)
                oc_ref[:, h * d:(h + 1) * d] = gates[:, 3 * h:3 * h + 1] * o.T
                psum = psum + p
            p_hi = psum.astype(BF16)
            p_lo = (psum - p_hi.astype(F32)).astype(BF16)
            imp = jnp.dot(ovt, p_hi, preferred_element_type=F32) + jnp.dot(ovt, p_lo, preferred_element_type=F32)
            st = jnp.where(forced, FORCE_SCORE, jnp.where(js > cur, -1.0, imp))
            st = jnp.where(js < n_slc, st, -2.0)
            sel = jnp.zeros((LANES, tq), F32)
            for _ in range(NSA_SLC_TOPK):
                mx = jnp.max(st, axis=0, keepdims=True)
                first = jnp.min(jnp.where(st == mx, jf, float(LANES)), axis=0, keepdims=True)
                hit = jf == first
                sel = jnp.where(hit, 1.0, sel)
                st = jnp.where(hit, -3e38, st)
            sel_ref[g] = sel.astype(sel_ref.dtype)

    consecutive = consec_ref[0] == 1

    @pl.when(jnp.logical_and(consecutive, qi == 0))
    def _():
        for chunk in range(2 * nc // LANES):
            rel = chunk * LANES - nc + lax.broadcasted_iota(I32, (LANES, 1), 0)
            dist = lane - (rel * NSA_CMP_STRIDE + (NSA_CMP_LEN - 1))
            bucket = jnp.where(dist >= 0, _t5_bucket(dist), T5_MASK_BUCKET)
            for h in range(NSA_HEADS):
                strip_ref[h, chunk * LANES:(chunk + 1) * LANES, :] = _t5_lookup(tab_ref[h:h + 1, :], bucket)

    @pl.when(consecutive)
    def _():
        start = pl.multiple_of(nc - per_block * qi, per_block)
        select(lambda h: strip_ref[h, pl.ds(start, nc), :])

    @pl.when(jnp.logical_not(consecutive))
    def _():
        cmp_end = lax.broadcasted_iota(I32, (nc, 1), 0) * NSA_CMP_STRIDE + (NSA_CMP_LEN - 1)
        pos_c = jnp.concatenate([posc_ref[...]] * (tq // LANES), axis=1)
        bucket = jnp.where(cmp_end <= tok, _t5_bucket(posq_ref[...] - pos_c), T5_MASK_BUCKET)
        select(lambda h: _t5_lookup(tab_ref[h:h + 1, :], bucket))


def _nsa_select(proj_b, q_cb, proj_f, gate_cb, kc, vc, pos_row, pos_cmp_rep, consec, tab_t, ov_t, *, n_slc, tq):
    seq = proj_b.shape[0]
    nc = kc.shape[1]
    g = NSA_KV_GROUPS
    d = NSA_HEAD_DIM
    grid_spec = pltpu.PrefetchScalarGridSpec(
        num_scalar_prefetch=1,
        grid=(seq // tq,),
        in_specs=[pl.BlockSpec((tq, NSA_WIDTH), lambda i, *_: (i, q_cb)),
                  pl.BlockSpec((g, nc, d), lambda i, *_: (0, 0, 0)),
                  pl.BlockSpec((g, nc, d), lambda i, *_: (0, 0, 0)),
                  pl.BlockSpec((1, tq), lambda i, *_: (0, i)),
                  pl.BlockSpec((nc, LANES), lambda i, *_: (0, 0)),
                  pl.BlockSpec((SUBLANES, LANES), lambda i, *_: (0, 0)),
                  pl.BlockSpec((tq, LANES), lambda i, *_: (i, gate_cb)),
                  pl.BlockSpec((LANES, nc), lambda i, *_: (0, 0))],
        out_specs=[pl.BlockSpec((tq, NSA_WIDTH), lambda i, *_: (i, 0)),
                   pl.BlockSpec((g, LANES, tq), lambda i, *_: (0, 0, i))],
        scratch_shapes=[pltpu.VMEM((NSA_HEADS, 2 * nc, tq), F32)])
    return pl.pallas_call(
        functools.partial(_nsa_select_kernel, nc=nc, n_slc=n_slc, tq=tq),
        grid_spec=grid_spec,
        out_shape=[jax.ShapeDtypeStruct((seq, NSA_WIDTH), F32),
                   jax.ShapeDtypeStruct((g, LANES, seq), F32)],
        compiler_params=_cparams("arbitrary"),
        name="nsa_select",
    )(consec, proj_b, kc, vc, pos_row, pos_cmp_rep, tab_t * LOG2E, proj_f, ov_t)


def _nsa_slc_kernel(pqmin_ref, pkmax_ref, consec_ref, q_ref, k_ref, v_ref, sel_ref, posq_ref, posk_ref, tab_ref,
                    o_ref, vt_ref, qt_ref, s_ref, m_ref, l_ref, acc_ref, cache_ref, *, tq, tk):
    g = pl.program_id(0)
    qi = pl.program_id(1)
    d = NSA_HEAD_DIM
    scale = d ** -0.5
    seq = k_ref.shape[0]
    consecutive = consec_ref[0] == 1
    n_cached = cache_ref.shape[0]

    @pl.when(qi == 0)
    def _():
        _build_vt(v_ref, vt_ref, 0, d, seq, tk)

    @pl.when(jnp.logical_and(consecutive, qi == 0))
    def _():
        rel = lax.broadcasted_iota(I32, (1, tq), 1) - lax.broadcasted_iota(I32, (tk, 1), 0)
        for v in range(n_cached):
            dist = rel + v * tq
            bucket = jnp.where(dist >= 0, _t5_bucket(dist), T5_MASK_BUCKET)
            for r in range(NSA_REP):
                cache_ref[v, r] = _t5_lookup(tab_ref[pl.ds(g * NSA_REP + r, 1), :], bucket)

    for r in range(NSA_REP):
        qt_ref[r] = (q_ref[:, r * d:(r + 1) * d].astype(F32) * (scale * LOG2E)).T.astype(BF16)
    m_ref[...] = jnp.full(m_ref.shape, NEG_INF, F32)
    l_ref[...] = jnp.zeros(l_ref.shape, F32)
    acc_ref[...] = jnp.zeros(acc_ref.shape, F32)
    pos_q = posq_ref[...]
    per_tile = tk // NSA_SLC_BLOCK
    n_full = (qi * tq) // tk

    def scores(j, slot):
        start = pl.multiple_of(j * tk, tk)
        k = k_ref[pl.ds(start, tk), :]
        for r in range(NSA_REP):
            s_ref[slot, r] = jnp.dot(k, qt_ref[r], preferred_element_type=F32)

    def softmax_pv(j, slot, masked, mode):
        start = pl.multiple_of(j * tk, tk)
        picked = jnp.concatenate(
            [jnp.broadcast_to(sel_ref[pl.ds(j * per_tile + b, 1), :], (NSA_SLC_BLOCK, tq)) for b in range(per_tile)],
            axis=0)
        if masked and mode == "gather":
            key = start + lax.broadcasted_iota(I32, (tk, tq), 0)
            qry = qi * tq + lax.broadcasted_iota(I32, (tk, tq), 1)
            picked = jnp.where(key <= qry, picked, 0.0)
        mask = picked > 0.5
        if mode == "gather":
            pos_k = jnp.concatenate([posk_ref[pl.ds(start, tk), :]] * (tq // LANES), axis=1)
            bucket = jnp.where(mask, _t5_bucket(pos_q - pos_k), T5_MASK_BUCKET)
        for r in range(NSA_REP):
            h = g * NSA_REP + r
            t = s_ref[slot, r]
            if mode == "gather":
                t = t + _t5_lookup(tab_ref[pl.ds(h, 1), :], bucket)
            elif mode == "cached":
                t = jnp.where(mask, t + cache_ref[(qi * tq - start) // tq, r], NEG_INF)
            else:
                t = jnp.where(mask, t, NEG_INF)
            m_prev = m_ref[r]
            mx = jnp.max(t, axis=0, keepdims=True)
            if mode == "far":
                b = tab_ref[pl.ds(h, 1), T5_BUCKETS - 1:T5_BUCKETS]
                m_new = jnp.maximum(m_prev, mx + b)
                shift = m_new - b
            else:
                m_new = jnp.maximum(m_prev, mx)
                shift = m_new
            alpha = jnp.exp2(m_prev - m_new)
            p = jnp.exp2(t - shift)
            l_ref[r] = alpha * l_ref[r] + jnp.sum(p, axis=0, keepdims=True)
            acc_ref[r] = alpha * acc_ref[r] + jnp.dot(vt_ref[0, :, pl.ds(start, tk)], p.astype(BF16),
                                                       preferred_element_type=F32)
            m_ref[r] = m_new

    def is_far(j):
        return pqmin_ref[qi] - pkmax_ref[j] >= T5_MAX_DIST

    def stages(j, count, mode):
        for k in range(count):
            scores(j + k + 1, (k + 1) % 2)
            softmax_pv(j + k, k % 2, False, mode)

    def near(fn):
        @pl.when(consecutive)
        def _():
            fn("cached")

        @pl.when(jnp.logical_not(consecutive))
        def _():
            fn("gather")

    def far_or_near(far, fn):
        @pl.when(far)
        def _():
            fn("far")

        @pl.when(jnp.logical_not(far))
        def _():
            near(fn)

    scores(0, 0)

    def pair(jj, carry):
        j = 2 * jj
        far_or_near(jnp.logical_and(is_far(j), is_far(j + 1)), lambda mode: stages(j, 2, mode))
        return carry

    pairs = n_full // 2
    lax.fori_loop(0, pairs, pair, 0)
    last = 2 * pairs

    @pl.when(n_full % 2 == 1)
    def _():
        far_or_near(is_far(last), lambda mode: stages(last, 1, mode))
        near(lambda mode: softmax_pv(last + 1, 1, True, mode))

    @pl.when(n_full % 2 == 0)
    def _():
        near(lambda mode: softmax_pv(last, 0, True, mode))

    for r in range(NSA_REP):
        o_ref[:, r * d:(r + 1) * d] = (acc_ref[r] / l_ref[r]).T


def _nsa_slc(proj_b, q_off, k_off, v_off, sel, pos_row, pos_rep, tab_t, pq_min, pk_max, consec, *, tq, tk):
    seq = proj_b.shape[0]
    tab_t = tab_t * LOG2E
    d = NSA_HEAD_DIM
    gw = NSA_REP * d
    once = pl.Buffered(1)
    n_cached = (T5_MAX_DIST + tk - 2) // tq + 1 + tk // tq
    grid_spec = pltpu.PrefetchScalarGridSpec(
        num_scalar_prefetch=3,
        grid=(NSA_KV_GROUPS, seq // tq),
        in_specs=[pl.BlockSpec((tq, gw), lambda g, i, *_: (i, q_off // gw + g)),
                  pl.BlockSpec((seq, d), lambda g, i, *_: (0, k_off // d + g), pipeline_mode=once),
                  pl.BlockSpec((seq, d), lambda g, i, *_: (0, v_off // d + g), pipeline_mode=once),
                  pl.BlockSpec((None, LANES, tq), lambda g, i, *_: (g, 0, i)),
                  pl.BlockSpec((1, tq), lambda g, i, *_: (0, i)),
                  pl.BlockSpec((seq, LANES), lambda g, i, *_: (0, 0), pipeline_mode=once),
                  pl.BlockSpec((SUBLANES, LANES), lambda g, i, *_: (0, 0))],
        out_specs=pl.BlockSpec((tq, gw), lambda g, i, *_: (i, g)),
        scratch_shapes=[pltpu.VMEM((1, d, seq), BF16), pltpu.VMEM((NSA_REP, d, tq), BF16),
                        pltpu.VMEM((2, NSA_REP, tk, tq), F32), pltpu.VMEM((NSA_REP, 1, tq), F32),
                        pltpu.VMEM((NSA_REP, 1, tq), F32), pltpu.VMEM((NSA_REP, d, tq), F32),
                        pltpu.VMEM((n_cached, NSA_REP, tk, tq), F32)])
    return pl.pallas_call(
        functools.partial(_nsa_slc_kernel, tq=tq, tk=tk),
        grid_spec=grid_spec,
        out_shape=jax.ShapeDtypeStruct((seq, NSA_WIDTH), F32),
        compiler_params=_cparams("parallel", "arbitrary"),
        name="nsa_slc",
    )(pq_min, pk_max, consec, proj_b, proj_b, proj_b, sel, pos_row, pos_rep, tab_t)


def _nsa_win_kernel(*refs, tq, nt):
    consec_ref = refs[0]
    q_ref = refs[1]
    k_refs = refs[2:2 + nt]
    v_refs = refs[2 + nt:2 + 2 * nt]
    pk_refs = refs[2 + 2 * nt:2 + 3 * nt]
    posq_ref, tab_ref, gate_ref, gout_ref, oc_ref, os_ref, o_ref, bias_ref = refs[2 + 3 * nt:]
    qi = pl.program_id(0)
    d = NSA_HEAD_DIM
    c = d ** -0.5 * LOG2E
    sub = lax.broadcasted_iota(I32, (tq, 1), 0)
    lane = lax.broadcasted_iota(I32, (1, tq), 1)

    def band_bucket(kidx, tok, dist):
        lower = jnp.maximum(tok - (NSA_WINDOW - 1), 0)
        b = jnp.where(kidx >= lower, _t5_bucket(dist), T5_MASK_BUCKET)
        return jnp.where(kidx <= tok, b, T5_MASK_BUCKET)

    def attend(bias):
        gates = jax.nn.sigmoid(gate_ref[...])
        for g in range(NSA_KV_GROUPS):
            ks = [kr[:, g * d:(g + 1) * d] for kr in k_refs]
            vts = [vr[:, g * d:(g + 1) * d].astype(F32).T.astype(BF16) for vr in v_refs]
            for r in range(NSA_REP):
                h = g * NSA_REP + r
                hs = slice(h * d, (h + 1) * d)
                qt = (q_ref[:, hs].astype(F32) * c).T.astype(BF16)
                ts = [jnp.dot(ks[jj], qt, preferred_element_type=F32) + bias(h, jj) for jj in range(nt)]
                m = functools.reduce(jnp.maximum, [jnp.max(t, axis=0, keepdims=True) for t in ts])
                ps = [jnp.exp2(t - m) for t in ts]
                l = functools.reduce(jnp.add, [jnp.sum(p, axis=0, keepdims=True) for p in ps])
                o_t = functools.reduce(jnp.add, [jnp.dot(vt, p.astype(BF16), preferred_element_type=F32)
                                                 for vt, p in zip(vts, ps)])
                o_w = (o_t / l).T
                o = (oc_ref[:, hs] + gates[:, 3 * h + 1:3 * h + 2] * os_ref[:, hs]
                     + gates[:, 3 * h + 2:3 * h + 3] * o_w)
                o_ref[:, hs] = (o * _silu(gout_ref[:, hs])).astype(o_ref.dtype)

    consecutive = consec_ref[0] == 1

    @pl.when(jnp.logical_and(consecutive, qi == 0))
    def _():
        tok0 = (nt - 1) * tq + lane
        for jj in range(nt):
            kidx0 = jj * tq + sub
            bucket = band_bucket(kidx0, tok0, tok0 - kidx0)
            for h in range(NSA_HEADS):
                bias_ref[h, jj] = _t5_lookup(tab_ref[h:h + 1, :], bucket)

    cached = jnp.logical_and(consecutive, qi >= nt - 1)

    @pl.when(cached)
    def _():
        attend(lambda h, jj: bias_ref[h, jj])

    @pl.when(jnp.logical_not(cached))
    def _():
        tok = qi * tq + lane
        pos_q = posq_ref[...]
        buckets = []
        for jj in range(nt):
            kidx = (qi - (nt - 1) + jj) * tq + sub
            pos_k = jnp.concatenate([pk_refs[jj][...]] * (tq // LANES), axis=1)
            buckets.append(band_bucket(kidx, tok, pos_q - pos_k))
        attend(lambda h, jj: _t5_lookup(tab_ref[h:h + 1, :], buckets[jj]))


def _nsa_win(proj_b, q_cb, k_cb, v_cb, proj_f, gate_cb, gout_cb, oc, o_s, pos_row, pos_rep, consec, tab_t, *, tq):
    seq = proj_b.shape[0]
    nt = NSA_WINDOW // tq + 1

    def band_rows(jj, cb):
        return pl.BlockSpec((tq, NSA_KV), lambda i, *_: (jnp.maximum(i - (nt - 1) + jj, 0), cb))

    def band_pos(jj):
        return pl.BlockSpec((tq, LANES), lambda i, *_: (jnp.maximum(i - (nt - 1) + jj, 0), 0))

    in_specs = [pl.BlockSpec((tq, NSA_WIDTH), lambda i, *_: (i, q_cb))]
    in_specs += [band_rows(jj, k_cb) for jj in range(nt)]
    in_specs += [band_rows(jj, v_cb) for jj in range(nt)]
    in_specs += [band_pos(jj) for jj in range(nt)]
    in_specs += [pl.BlockSpec((1, tq), lambda i, *_: (0, i)),
                 pl.BlockSpec((SUBLANES, LANES), lambda i, *_: (0, 0)),
                 pl.BlockSpec((tq, LANES), lambda i, *_: (i, gate_cb)),
                 pl.BlockSpec((tq, NSA_WIDTH), lambda i, *_: (i, gout_cb)),
                 pl.BlockSpec((tq, NSA_WIDTH), lambda i, *_: (i, 0)),
                 pl.BlockSpec((tq, NSA_WIDTH), lambda i, *_: (i, 0))]
    args = [proj_b] * (1 + 2 * nt) + [pos_rep] * nt + [pos_row, tab_t * LOG2E, proj_f, proj_f, oc, o_s]
    grid_spec = pltpu.PrefetchScalarGridSpec(
        num_scalar_prefetch=1,
        grid=(seq // tq,),
        in_specs=in_specs,
        out_specs=pl.BlockSpec((tq, NSA_WIDTH), lambda i, *_: (i, 0)),
        scratch_shapes=[pltpu.VMEM((NSA_HEADS, nt, tq, tq), F32)])
    return pl.pallas_call(
        functools.partial(_nsa_win_kernel, tq=tq, nt=nt),
        grid_spec=grid_spec,
        out_shape=jax.ShapeDtypeStruct((seq, NSA_WIDTH), BF16),
        compiler_params=_cparams("arbitrary"),
        name="nsa_win",
    )(consec, *args)


def _in_proj(h, norm_g, w_b, w_f, tiles, name):
    tn_b = w_b.shape[1] // tiles["col_tiles_b"]
    tn_f = w_f.shape[1] // tiles["col_tiles_f"]
    assert tn_b % MXU_COLS == 0 and tn_f % MXU_COLS == 0
    proj_b = _norm_matmul(h, norm_g, w_b, tm=tiles["tm"], tn=tn_b, name=name + "_b", out_dtype=BF16)
    proj_f = _norm_matmul(h, norm_g, w_f, tm=tiles["tm"], tn=tn_f, name=name + "_f")
    return proj_b, proj_f


def _even_layer(h, mem_kv, norm_g, w_in, s5, w_glu, b_f, tiles):
    pb, pf = _in_proj(h, norm_g, *w_in, tiles, "in_proj_even")
    ob, of = EVEN_B_OFF, EVEN_F_OFF
    b_cat, c_cat, tab, d_skip = s5
    z = _s5_scan(pf, of["u"] // LANES, b_cat, c_cat, d_skip, tab, tc=tiles["s5_tc"])
    y_s5 = _s5_glu(z, w_glu, pf, of["g_s5"] // S5_WIDTH, tm=tiles["tm"], tn=tiles["tn_glu"])
    decay = _decay(pf, of["f"] // LANES, b_f, t=tiles["decay_t"])
    d = FOX_HEAD_DIM
    y_fox = _flash(pb, ob["q"], pb, ob["k"], pb, ob["v"], pf, of["g_fox"], heads=FOX_HEADS, dk=d, dv=d,
                   scale=d ** -0.5, tq=tiles["attn_tq"], tk=tiles["attn_tk"], hp=tiles["attn_hp"], decay=decay,
                   name="fox_attn")
    y_mem = _mem_attn(pb, ob["q_mem"] // MEM_HEAD_DIM, pf, of["g_mem"] // MEM_HEAD_DIM, mem_kv, t=tiles["mem_t"])
    return y_s5, y_fox, y_mem


def _odd_layer(h, mem_kv, norm_g, w_in, mla, nsa, pos, tiles):
    pb, pf = _in_proj(h, norm_g, *w_in, tiles, "in_proj_odd")
    ob, of = ODD_B_OFF, ODD_F_OFF
    g_cq, g_ckv, w_uq, w_ukv, freq = mla
    pos_col_f, pos_row, pos_cmp_rep, pos_rep, pq_min, pk_max, consec = pos
    q_r, k_r, v_r = _mla_up(pf, of["c_q"] // MLA_Q_RANK, of["c_kv"] // MLA_KV_RANK, of["k_rope"] // LANES,
                            g_cq, g_ckv, w_uq, w_ukv, pos_col_f, freq, t=tiles["prep_t"])
    y_mla = _flash(q_r, 0, k_r, 0, v_r, 0, pf, of["g_mla"], heads=MLA_HEADS, dk=2 * LANES, dv=MLA_V,
                   scale=(MLA_NOPE + MLA_ROPE) ** -0.5, tq=tiles["attn_tq"], tk=tiles["attn_tk"], hp=tiles["attn_hp"],
                   name="mla_attn")
    pe, w1, w2, tab_t, ov, n_slc = nsa
    kvc = _nsa_compress(pb, ob["k_cmp"] // NSA_HEAD_DIM, pe, w1, w2)
    oc, sel = _nsa_select(pb, ob["q_nsa"] // NSA_WIDTH, pf, of["gates"] // LANES, kvc[0], kvc[1], pos_row,
                          pos_cmp_rep, consec, tab_t, ov, n_slc=n_slc, tq=tiles["nsa_tq"])
    o_s = _nsa_slc(pb, ob["q_nsa"], ob["k_slc"], ob["v_slc"], sel, pos_row, pos_rep, tab_t, pq_min, pk_max, consec,
                   tq=tiles["slc_tq"], tk=tiles["slc_tk"])
    y_nsa = _nsa_win(pb, ob["q_nsa"] // NSA_WIDTH, ob["k_win"] // NSA_KV, ob["v_win"] // NSA_KV,
                     pf, of["gates"] // LANES, of["g_nsa"] // NSA_WIDTH, oc, o_s, pos_row, pos_rep, consec, tab_t,
                     tq=tiles["win_tq"])
    y_mem = _mem_attn(pb, ob["q_mem"] // MEM_HEAD_DIM, pf, of["g_mem"] // MEM_HEAD_DIM, mem_kv, t=tiles["mem_t"])
    return y_mla, y_nsa, y_mem


def _tiles(seq):
    return {"tm": min(seq, 1024), "col_tiles_b": 2, "col_tiles_f": 3, "tn_out": 1024, "tn_glu": 1024, "tn_mem": 512,
            "s5_tc": min(seq, 2048), "decay_t": min(seq, 512), "attn_tq": min(seq, 512),
            "attn_tk": min(seq, 512), "attn_hp": 2, "slc_tq": 256, "slc_tk": 512, "nsa_tq": 256, "win_tq": 128,
            "mem_t": min(seq, 1024), "norm_t": min(seq, 1024), "prep_t": min(seq, 512)}


def _context(positions, t5_table, seq, tiles):
    pos = positions[0]
    pos_col = pos.reshape(seq, 1)
    pos_row = pos.reshape(1, seq)
    pos_rep = jnp.broadcast_to(pos_col, (seq, LANES))
    pq_min = jnp.min(pos.reshape(seq // tiles["slc_tq"], tiles["slc_tq"]), axis=1)
    pk_max = jnp.max(pos.reshape(seq // tiles["slc_tk"], tiles["slc_tk"]), axis=1)
    nc = seq // NSA_CMP_STRIDE
    pos_cmp = jnp.pad(pos[NSA_CMP_LEN - 1::NSA_CMP_STRIDE], (0, 1))
    pos_cmp_rep = jnp.broadcast_to(pos_cmp.reshape(nc, 1), (nc, LANES))
    half = MLA_ROPE // 2
    inv_freq = ROPE_THETA ** (-jnp.arange(half, dtype=F32) / half)
    freq = jnp.concatenate([inv_freq, inv_freq, jnp.zeros((LANES - MLA_ROPE,), F32)]).reshape(1, LANES)
    tab_t = jnp.pad(t5_table.astype(F32).T, ((0, SUBLANES - NSA_HEADS), (0, LANES - T5_BUCKETS)))
    tab_t = tab_t.at[:, T5_MASK_BUCKET].set(NEG_INF)
    n_slc = seq // NSA_SLC_BLOCK
    cs = np.arange(nc) * NSA_CMP_STRIDE
    ss = np.arange(LANES) * NSA_SLC_BLOCK
    ov_np = np.clip(np.minimum(cs[:, None] + NSA_CMP_LEN, ss[None, :] + NSA_SLC_BLOCK)
                    - np.maximum(cs[:, None], ss[None, :]), 0, None) / NSA_CMP_LEN
    ov_np[nc - 1, :] = 0.0
    ov_np[:, n_slc:] = 0.0
    consec = jnp.all(pos[1:] - pos[:-1] == 1).astype(I32).reshape(1)
    return {"pos": (pos_col.astype(F32), pos_row, pos_cmp_rep, pos_rep, pq_min, pk_max, consec), "freq": freq,
            "tab_t": tab_t, "ov": jnp.asarray(ov_np.T, BF16), "n_slc": n_slc}


def _odd_params(i, mla_g_cq, mla_g_ckv, mla_w_uq, mla_w_ukv, nsa_cmp_pe, nsa_cmp_w1, nsa_cmp_w2, ctx):
    dq = MLA_NOPE + MLA_ROPE
    w_uq = mla_w_uq[i].reshape(MLA_Q_RANK, MLA_HEADS, dq)
    w_uq = jnp.pad(w_uq, ((0, 0), (0, 0), (0, 2 * LANES - dq))).reshape(MLA_Q_RANK, -1).astype(BF16)
    mla = (mla_g_cq[i], mla_g_ckv[i], w_uq, mla_w_ukv[i].astype(BF16), ctx["freq"])
    nsa = (nsa_cmp_pe[i].astype(F32), nsa_cmp_w1[i], nsa_cmp_w2[i], ctx["tab_t"],
           ctx["ov"], ctx["n_slc"])
    return mla, nsa


def kernel(x, mem, positions, norm_g, mem_norm_g, final_norm_g, t5_table, w_out, mem_w_kv, even_w_in, s5_lam_re,
           s5_lam_im, s5_log_dt, s5_b_re, s5_b_im, s5_c_re, s5_c_im, s5_d, s5_w_glu, fox_b_f, odd_w_in, mla_g_cq,
           mla_g_ckv, mla_w_uq, mla_w_ukv, nsa_cmp_pe, nsa_cmp_w1, nsa_cmp_w2):
    batch, seq, _ = x.shape
    assert batch == 1 and seq % 1024 == 0 and seq // NSA_SLC_BLOCK <= LANES
    depth = norm_g.shape[0]
    tiles = _tiles(seq)
    ctx = _context(positions, t5_table, seq, tiles)
    h = x[0]
    mem_kv_all = _mem_kv_all(mem[0], mem_norm_g, mem_w_kv, tn=tiles["tn_mem"])
    for layer in range(depth):
        i = layer // 2
        mem_kv = (mem_kv_all, layer)
        if layer % 2 == 0:
            w_in = (_reorder_w_in(even_w_in[i], EVEN_SPLITS, EVEN_B_ORDER),
                    _reorder_w_in(even_w_in[i], EVEN_SPLITS, EVEN_F_ORDER))
            b_cat, c_cat, tab = _s5_prepare(s5_lam_re[i], s5_lam_im[i], s5_log_dt[i], s5_b_re[i], s5_b_im[i],
                                            s5_c_re[i], s5_c_im[i])
            ys = _even_layer(h, mem_kv, norm_g[layer], w_in, (b_cat, c_cat, tab, s5_d[i]),
                             s5_w_glu[i], fox_b_f[i], tiles)
        else:
            w_in = (_reorder_w_in(odd_w_in[i], ODD_SPLITS, ODD_B_ORDER),
                    _reorder_w_in(odd_w_in[i], ODD_SPLITS, ODD_F_ORDER))
            mla, nsa = _odd_params(i, mla_g_cq, mla_g_ckv, mla_w_uq, mla_w_ukv, nsa_cmp_pe, nsa_cmp_w1, nsa_cmp_w2,
                                   ctx)
            ys = _odd_layer(h, mem_kv, norm_g[layer], w_in, mla, nsa, ctx["pos"], tiles)
        h = _out_proj(h, ys, w_out, layer, tm=tiles["tm"], tn=tiles["tn_out"])
    return _final_norm(h, final_norm_g, tm=tiles["norm_t"])[None]
```

```python
import functools
import math

import numpy as np
import jax
import jax.numpy as jnp
from jax import lax
from jax.experimental import pallas as pl
from jax.experimental.pallas import tpu as pltpu

F32 = jnp.float32
BF16 = jnp.bfloat16
I32 = jnp.int32

RMS_EPS = 1e-6
NEG_INF = -1e30
LOG2E = math.log2(math.e)

S5_WIDTH = 1024
S5_GROUP = 16
S5_GROUPS = S5_WIDTH // S5_GROUP
S5_STATE = 64
FOX_HEADS = 8
FOX_HEAD_DIM = 128
FOX_WIDTH = FOX_HEADS * FOX_HEAD_DIM
MEM_HEADS = 4
MEM_HEAD_DIM = 128
MEM_WIDTH = MEM_HEADS * MEM_HEAD_DIM
MLA_HEADS = 8
MLA_Q_RANK = 512
MLA_KV_RANK = 512
MLA_NOPE = 128
MLA_ROPE = 64
MLA_V = 128
MLA_WIDTH = MLA_HEADS * MLA_V
ROPE_THETA = 10000.0
NSA_HEADS = 8
NSA_KV_GROUPS = 2
NSA_REP = NSA_HEADS // NSA_KV_GROUPS
NSA_HEAD_DIM = 128
NSA_WIDTH = NSA_HEADS * NSA_HEAD_DIM
NSA_KV = NSA_KV_GROUPS * NSA_HEAD_DIM
NSA_CMP_LEN = 32
NSA_CMP_STRIDE = 16
NSA_CMP_HIDDEN = 256
NSA_SLC_BLOCK = 64
NSA_SLC_TOPK = 16
NSA_WINDOW = 512
FORCE_SCORE = 1e6
T5_BUCKETS = 32
T5_MAX_DIST = 1024

EVEN_SPLITS = (S5_WIDTH, S5_WIDTH, FOX_WIDTH, FOX_WIDTH, FOX_WIDTH, FOX_HEADS, FOX_WIDTH, MEM_WIDTH, MEM_WIDTH)
ODD_SPLITS = (MLA_Q_RANK, MLA_KV_RANK, MLA_ROPE, MLA_WIDTH, NSA_WIDTH, NSA_KV, NSA_KV, NSA_KV, NSA_KV, NSA_KV,
              NSA_KV, 3 * NSA_HEADS, NSA_WIDTH, MEM_WIDTH, MEM_WIDTH)

LANES = 128
SUBLANES = 8
MXU_COLS = 256
VMEM_LIMIT_BYTES = 56 * 1024 * 1024

EVEN_B_ORDER = (("q", 2, 1024), ("k", 3, 1024), ("v", 4, 1024), ("q_mem", 7, 512))
EVEN_F_ORDER = (("u", 0, 1024), ("g_s5", 1, 1024), ("g_fox", 6, 1024), ("g_mem", 8, 512), ("f", 5, 128),
                ("pad", None, 128))
ODD_B_ORDER = (("q_nsa", 4, 1024), ("q_mem", 13, 512), ("k_cmp", 5, 256), ("v_cmp", 6, 256), ("k_slc", 7, 256),
               ("v_slc", 8, 256), ("k_win", 9, 256), ("v_win", 10, 256))
ODD_F_ORDER = (("g_nsa", 12, 1024), ("g_mla", 3, 1024), ("c_q", 0, 512), ("c_kv", 1, 512), ("g_mem", 14, 512),
               ("k_rope", 2, 128), ("gates", 11, 128))


def _layout(order):
    off, out = 0, {}
    for name, _, width in order:
        assert off % width == 0
        out[name] = off
        off += width
    return out, off


EVEN_B_OFF, EVEN_B_N = _layout(EVEN_B_ORDER)
EVEN_F_OFF, EVEN_F_N = _layout(EVEN_F_ORDER)
ODD_B_OFF, ODD_B_N = _layout(ODD_B_ORDER)
ODD_F_OFF, ODD_F_N = _layout(ODD_F_ORDER)


def _reorder_w_in(w, splits, order):
    starts = np.concatenate([[0], np.cumsum(splits)])
    cols = []
    for _, idx, width in order:
        if idx is None:
            cols.append(jnp.zeros((w.shape[0], width), w.dtype))
            continue
        seg = w[:, int(starts[idx]):int(starts[idx + 1])]
        pad = width - seg.shape[1]
        if pad:
            seg = jnp.pad(seg, ((0, 0), (0, pad)))
        cols.append(seg)
    return jnp.concatenate(cols, axis=1).astype(BF16)


def _cparams(*sem):
    return pltpu.CompilerParams(dimension_semantics=sem, vmem_limit_bytes=VMEM_LIMIT_BYTES)


def _silu(g):
    return g * jax.nn.sigmoid(g)


def _norm_matmul_kernel(x_ref, g_ref, w_ref, o_ref, xn_ref):
    @pl.when(pl.program_id(1) == 0)
    def _():
        x = x_ref[...]
        ms = jnp.mean(x * x, axis=-1, keepdims=True)
        xn_ref[...] = (x * lax.rsqrt(ms + RMS_EPS) * g_ref[...]).astype(BF16)

    o_ref[...] = jnp.dot(xn_ref[...], w_ref[...].astype(BF16), preferred_element_type=F32).astype(o_ref.dtype)


def _norm_matmul(x, g, w, *, x_cb=0, tm, tn, name, out_dtype=F32):
    m = x.shape[0]
    k, n = w.shape
    return pl.pallas_call(
        _norm_matmul_kernel,
        grid=(m // tm, n // tn),
        in_specs=[pl.BlockSpec((tm, k), lambda i, j: (i, x_cb)),
                  pl.BlockSpec((1, k), lambda i, j: (0, 0)),
                  pl.BlockSpec((k, tn), lambda i, j: (0, j))],
        out_specs=pl.BlockSpec((tm, tn), lambda i, j: (i, j)),
        out_shape=jax.ShapeDtypeStruct((m, n), out_dtype),
        scratch_shapes=[pltpu.VMEM((tm, k), BF16)],
        compiler_params=_cparams("parallel", "arbitrary"),
        name=name,
    )(x, g.reshape(1, k), w)


def _out_proj_kernel(h_ref, *refs):
    o_ref = refs[-1]
    n = (len(refs) - 1) // 2
    acc = h_ref[...]
    for y_ref, w_ref in zip(refs[:n], refs[n:2 * n]):
        acc = acc + jnp.dot(y_ref[...], w_ref[...].astype(BF16), preferred_element_type=F32)
    o_ref[...] = acc


def _out_proj(h, ys, w_all, layer, *, tm, tn):
    m, n = h.shape
    in_specs = [pl.BlockSpec((tm, tn), lambda i, j: (i, j))]
    in_specs += [pl.BlockSpec((tm, y.shape[1]), lambda i, j: (i, 0)) for y in ys]
    row = 0
    for y in ys:
        width = y.shape[1]
        assert row % width == 0
        in_specs.append(pl.BlockSpec((None, width, tn), lambda i, j, rb=row // width: (layer, rb, j)))
        row += width
    return pl.pallas_call(
        _out_proj_kernel,
        grid=(m // tm, n // tn),
        in_specs=in_specs,
        out_specs=pl.BlockSpec((tm, tn), lambda i, j: (i, j)),
        out_shape=jax.ShapeDtypeStruct((m, n), F32),
        compiler_params=_cparams("parallel", "arbitrary"),
        name="out_proj",
    )(h, *ys, *([w_all] * len(ys)))


def _final_norm_kernel(x_ref, g_ref, o_ref):
    x = x_ref[...]
    ms = jnp.mean(x * x, axis=-1, keepdims=True)
    o_ref[...] = x * lax.rsqrt(ms + RMS_EPS) * g_ref[...]


def _final_norm(h, g, *, tm):
    m, n = h.shape
    return pl.pallas_call(
        _final_norm_kernel,
        grid=(m // tm,),
        in_specs=[pl.BlockSpec((tm, n), lambda i: (i, 0)), pl.BlockSpec((1, n), lambda i: (0, 0))],
        out_specs=pl.BlockSpec((tm, n), lambda i: (i, 0)),
        out_shape=jax.ShapeDtypeStruct((m, n), F32),
        compiler_params=_cparams("parallel"),
        name="final_norm",
    )(h, g.reshape(1, n))


ONES_ROWS = 16


def _build_vt(v_ref, vt_ref, hh, dv, seq, chunk):
    def body(c, carry):
        st = pl.multiple_of(c * chunk, chunk)
        vt_ref[hh, 0:dv, pl.ds(st, chunk)] = v_ref[pl.ds(st, chunk), hh * dv:(hh + 1) * dv].astype(F32).T.astype(BF16)
        return carry

    lax.fori_loop(0, seq // chunk, body, 0)
    extra = vt_ref.shape[1] - dv
    if extra:
        vt_ref[hh, dv:dv + extra, :] = jnp.ones((extra, seq), BF16)


def _flash_kernel(*refs, scale, tq, tk, hp, dk, dv, has_decay):
    cw = min(tq, MXU_COLS)
    ncg = tq // cw
    streams = [(hh, cg) for hh in range(hp) for cg in range(ncg)]
    if has_decay:
        q_ref, k_ref, v_ref, g_ref, cq_ref, ck_ref, o_ref, vt_ref, qt_ref, s_ref, m_ref, acc_ref = refs
    else:
        q_ref, k_ref, v_ref, g_ref, o_ref, vt_ref, qt_ref, s_ref, m_ref, acc_ref = refs
    hb = pl.program_id(0)
    qi = pl.program_id(1)
    seq = k_ref.shape[0]

    @pl.when(qi == 0)
    def _():
        for hh in range(hp):
            _build_vt(v_ref, vt_ref, hh, dv, seq, tk)

    for v, (hh, cg) in enumerate(streams):
        qt_ref[v] = (q_ref[cg * cw:(cg + 1) * cw, hh * dk:(hh + 1) * dk].astype(F32) * (scale * LOG2E)).T.astype(BF16)
    m_ref[...] = jnp.full(m_ref.shape, NEG_INF, F32)
    acc_ref[...] = jnp.zeros(acc_ref.shape, F32)
    n_full = (qi * tq) // tk
    if has_decay:
        cq2 = [cq_ref[pl.ds(hb * hp + hh, 1), cg * cw:(cg + 1) * cw] for hh, cg in streams]

    def scores(j, slot):
        start = pl.multiple_of(j * tk, tk)
        for v, (hh, cg) in enumerate(streams):
            s_ref[slot, v] = jnp.dot(k_ref[pl.ds(start, tk), hh * dk:(hh + 1) * dk], qt_ref[v],
                                     preferred_element_type=F32)

    def softmax_pv(j, slot, masked):
        start = pl.multiple_of(j * tk, tk)
        for v, (hh, cg) in enumerate(streams):
            t = s_ref[slot, v]
            if has_decay:
                t = t - jnp.concatenate([ck_ref[hh, pl.ds(start, tk), :]] * (cw // LANES), axis=1)
            if masked:
                key = start + lax.broadcasted_iota(I32, (tk, cw), 0)
                qry = qi * tq + cg * cw + lax.broadcasted_iota(I32, (tk, cw), 1)
                t = jnp.where(key <= qry, t, NEG_INF)
            m_prev = m_ref[v]
            mx = jnp.max(t, axis=0, keepdims=True)
            if has_decay:
                m_new = jnp.maximum(m_prev, mx + cq2[v])
                shift = m_new - cq2[v]
            else:
                m_new = jnp.maximum(m_prev, mx)
                shift = m_new
            alpha = jnp.exp2(m_prev - m_new)
            p = jnp.exp2(t - shift).astype(BF16)
            acc_ref[v] = alpha * acc_ref[v] + jnp.dot(vt_ref[hh, :, pl.ds(start, tk)], p,
                                                       preferred_element_type=F32)
            m_ref[v] = m_new

    scores(0, 0)

    def pair(j):
        scores(j + 1, 1)
        softmax_pv(j, 0, False)
        scores(j + 2, 0)
        softmax_pv(j + 1, 1, False)

    def quad(qq, carry):
        pair(4 * qq)
        pair(4 * qq + 2)
        return carry

    quads = n_full // 4
    lax.fori_loop(0, quads, quad, 0)
    rem = n_full - 4 * quads

    @pl.when(rem >= 2)
    def _():
        pair(4 * quads)

    last = 4 * quads + 2 * (rem // 2)

    @pl.when(n_full % 2 == 1)
    def _():
        scores(last + 1, 1)
        softmax_pv(last, 0, False)
        softmax_pv(last + 1, 1, True)

    @pl.when(n_full % 2 == 0)
    def _():
        softmax_pv(last, 0, True)

    for v, (hh, cg) in enumerate(streams):
        a = acc_ref[v]
        o = (a[0:dv, :] / a[dv:dv + 1, :]).T
        rows = slice(cg * cw, (cg + 1) * cw)
        cols = slice(hh * dv, (hh + 1) * dv)
        o_ref[rows, cols] = (o * _silu(g_ref[rows, cols])).astype(o_ref.dtype)


def _flash(q_arr, q_off, k_arr, k_off, v_arr, v_off, g_arr, g_off, *, heads, dk, dv, scale, tq, tk, hp,
           decay=None, name):
    seq = q_arr.shape[0]
    cw = min(tq, MXU_COLS)
    ns = hp * (tq // cw)
    once = pl.Buffered(1)
    in_specs = [pl.BlockSpec((tq, hp * dk), lambda h, i: (i, q_off // (hp * dk) + h)),
                pl.BlockSpec((seq, hp * dk), lambda h, i: (0, k_off // (hp * dk) + h), pipeline_mode=once),
                pl.BlockSpec((seq, hp * dv), lambda h, i: (0, v_off // (hp * dv) + h), pipeline_mode=once),
                pl.BlockSpec((tq, hp * dv), lambda h, i: (i, g_off // (hp * dv) + h))]
    args = [q_arr, k_arr, v_arr, g_arr]
    if decay is not None:
        cum_t, cum_rep = decay
        in_specs += [pl.BlockSpec((SUBLANES, tq), lambda h, i: (0, i)),
                     pl.BlockSpec((hp, seq, LANES), lambda h, i: (h, 0, 0), pipeline_mode=once)]
        args += [cum_t, cum_rep]
    return pl.pallas_call(
        functools.partial(_flash_kernel, scale=scale, tq=tq, tk=tk, hp=hp, dk=dk, dv=dv, has_decay=decay is not None),
        grid=(heads // hp, seq // tq),
        in_specs=in_specs,
        out_specs=pl.BlockSpec((tq, hp * dv), lambda h, i: (i, h)),
        out_shape=jax.ShapeDtypeStruct((seq, heads * dv), BF16),
        scratch_shapes=[pltpu.VMEM((hp, dv + ONES_ROWS, seq), BF16), pltpu.VMEM((ns, dk, cw), BF16),
                        pltpu.VMEM((2, ns, tk, cw), F32), pltpu.VMEM((ns, 1, cw), F32),
                        pltpu.VMEM((ns, dv + ONES_ROWS, cw), F32)],
        compiler_params=_cparams("parallel", "arbitrary"),
        name=name,
    )(*args)


def _decay_kernel(f_ref, b_ref, ct_ref, cr_ref, carry_ref, *, t):
    i = pl.program_id(0)

    @pl.when(i == 0)
    def _():
        carry_ref[...] = jnp.zeros(carry_ref.shape, F32)

    x = f_ref[...] + b_ref[...]
    lf = jnp.minimum(x, 0.0) - jnp.log1p(jnp.exp(-jnp.abs(x)))
    row = lax.broadcasted_iota(I32, lf.shape, 0)
    s = 1
    while s < t:
        lf = lf + jnp.where(row >= s, pltpu.roll(lf, s, 0), 0.0)
        s *= 2
    lf = lf + carry_ref[...]
    carry_ref[...] = lf[t - 1:t, :]
    lf2 = lf * LOG2E
    ct_ref[...] = lf2.T[:FOX_HEADS, :]
    for h in range(FOX_HEADS):
        cr_ref[h] = jnp.broadcast_to(lf2[:, h:h + 1], (t, LANES))


def _decay(proj, f_cb, b_f, *, t):
    seq = proj.shape[0]
    b = jnp.pad(b_f.reshape(1, FOX_HEADS), ((0, 0), (0, LANES - FOX_HEADS)))
    return pl.pallas_call(
        functools.partial(_decay_kernel, t=t),
        grid=(seq // t,),
        in_specs=[pl.BlockSpec((t, LANES), lambda i: (i, f_cb)), pl.BlockSpec((1, LANES), lambda i: (0, 0))],
        out_specs=[pl.BlockSpec((FOX_HEADS, t), lambda i: (0, i)),
                   pl.BlockSpec((FOX_HEADS, t, LANES), lambda i: (0, i, 0))],
        out_shape=[jax.ShapeDtypeStruct((FOX_HEADS, seq), F32), jax.ShapeDtypeStruct((FOX_HEADS, seq, LANES), F32)],
        scratch_shapes=[pltpu.VMEM((1, LANES), F32)],
        compiler_params=_cparams("arbitrary"),
        name="fox_decay",
    )(proj, b)


def _mem_attn_kernel(q_ref, kv_ref, g_ref, o_ref):
    d = MEM_HEAD_DIM
    for h in range(MEM_HEADS):
        hs = slice(h * d, (h + 1) * d)
        k = kv_ref[:, hs].astype(BF16)
        v = kv_ref[:, MEM_WIDTH + h * d:MEM_WIDTH + (h + 1) * d].astype(BF16)
        s = lax.dot_general(q_ref[:, hs], k, (((1,), (1,)), ((), ())), preferred_element_type=F32) * (d ** -0.5)
        m = jnp.max(s, axis=1, keepdims=True)
        p = jnp.exp(s - m)
        l = jnp.sum(p, axis=1, keepdims=True)
        o = jnp.dot(p.astype(BF16), v, preferred_element_type=F32) / l
        o_ref[:, hs] = (o * _silu(g_ref[:, hs])).astype(o_ref.dtype)


def _mem_kv_all(mem2d, g, w_all, *, tn):
    depth, k, n = w_all.shape
    m = mem2d.shape[0]
    return pl.pallas_call(
        _norm_matmul_kernel,
        grid=(depth, n // tn),
        in_specs=[pl.BlockSpec((m, k), lambda l, j: (0, 0)),
                  pl.BlockSpec((1, k), lambda l, j: (0, 0)),
                  pl.BlockSpec((None, k, tn), lambda l, j: (l, 0, j))],
        out_specs=pl.BlockSpec((None, m, tn), lambda l, j: (l, 0, j)),
        out_shape=jax.ShapeDtypeStruct((depth, m, n), F32),
        scratch_shapes=[pltpu.VMEM((m, k), BF16)],
        compiler_params=_cparams("arbitrary", "arbitrary"),
        name="mem_kv",
    )(mem2d, g.reshape(1, k), w_all)


def _mem_attn(proj_b, q_cb, proj_f, g_cb, mem_kv, *, t):
    seq = proj_b.shape[0]
    mem_kv, layer = mem_kv
    nm = mem_kv.shape[1]
    w = MEM_WIDTH
    d = MEM_HEAD_DIM
    return pl.pallas_call(
        _mem_attn_kernel,
        grid=(seq // t,),
        in_specs=[pl.BlockSpec((t, w), lambda i: (i, q_cb * d // w)),
                  pl.BlockSpec((None, nm, 2 * w), lambda i: (layer, 0, 0)),
                  pl.BlockSpec((t, w), lambda i: (i, g_cb * d // w))],
        out_specs=pl.BlockSpec((t, w), lambda i: (i, 0)),
        out_shape=jax.ShapeDtypeStruct((seq, w), BF16),
        compiler_params=_cparams("parallel"),
        name="mem_attn",
    )(proj_b, mem_kv, proj_f)


S5_TILE_GROUPS = LANES // S5_GROUP
S5_TILE_STATES = S5_TILE_GROUPS * S5_STATE
S5_TILES = S5_GROUPS // S5_TILE_GROUPS


def _s5_scan_kernel(u_ref, b_ref, c_ref, d_ref, tab_ref, z_ref, bu_ref, carry_ref, *, tc):
    ns = S5_TILE_STATES

    @pl.when(pl.program_id(1) == 0)
    def _():
        carry_ref[...] = jnp.zeros(carry_ref.shape, F32)

    u = u_ref[...]
    bu_ref[...] = jnp.dot(u.astype(BF16), b_ref[...], preferred_element_type=F32)
    steps = [(1, tab_ref[0], tab_ref[1]), (2, tab_ref[2], tab_ref[3]), (4, tab_ref[4], tab_ref[5])]
    pr = tab_ref[6]
    pi = tab_ref[7]

    def body(i, carry):
        cr, ci = carry
        r0 = pl.multiple_of(i * SUBLANES, SUBLANES)
        xr = bu_ref[pl.ds(r0, SUBLANES), 0:ns]
        xi = bu_ref[pl.ds(r0, SUBLANES), ns:2 * ns]
        for s, ar, ai in steps:
            sr = pltpu.roll(xr, s, 0)
            si = pltpu.roll(xi, s, 0)
            xr, xi = xr + ar * sr - ai * si, xi + ar * si + ai * sr
        xr, xi = xr + pr * cr - pi * ci, xi + pr * ci + pi * cr
        bu_ref[pl.ds(r0, SUBLANES), 0:ns] = xr
        bu_ref[pl.ds(r0, SUBLANES), ns:2 * ns] = xi
        return xr[SUBLANES - 1:SUBLANES, :], xi[SUBLANES - 1:SUBLANES, :]

    cr, ci = lax.fori_loop(0, tc // SUBLANES, body, (carry_ref[0:1, 0:ns], carry_ref[0:1, ns:2 * ns]))
    carry_ref[0:1, 0:ns] = cr
    carry_ref[0:1, ns:2 * ns] = ci
    y = jnp.dot(bu_ref[...].astype(BF16), c_ref[...], preferred_element_type=F32) + d_ref[...] * u
    z_ref[...] = jax.nn.gelu(y)


def _s5_prepare(lam_re, lam_im, log_dt, b_re, b_im, c_re, c_im):
    dt = jnp.exp(log_dt.astype(F32))[:, None]
    lr = lam_re.astype(F32)
    li = lam_im.astype(F32)
    mag = jnp.exp(lr * dt)
    ab_re = mag * jnp.cos(li * dt)
    ab_im = mag * jnp.sin(li * dt)
    den = lr * lr + li * li
    nr = ab_re - 1.0
    f_re = (nr * lr + ab_im * li) / den
    f_im = (ab_im * lr - nr * li) / den
    br = b_re.astype(F32)
    bim = b_im.astype(F32)
    bb_re = f_re[..., None] * br - f_im[..., None] * bim
    bb_im = f_re[..., None] * bim + f_im[..., None] * br
    eye = jnp.eye(S5_TILE_GROUPS, dtype=F32)

    def blockdiag_in(bb):
        t = bb.reshape(S5_TILES, S5_TILE_GROUPS, S5_STATE, S5_GROUP)
        m = jnp.einsum("jgpc,gh->jgchp", t, eye)
        return m.reshape(S5_TILES, LANES, S5_TILE_STATES)

    def blockdiag_out(cc):
        t = cc.reshape(S5_TILES, S5_TILE_GROUPS, S5_GROUP, S5_STATE)
        m = jnp.einsum("jgcp,gh->jgphc", t, eye)
        return m.reshape(S5_TILES, S5_TILE_STATES, LANES)

    b_cat = jnp.concatenate([blockdiag_in(bb_re), blockdiag_in(bb_im)], axis=2).astype(BF16)
    c_cat = jnp.concatenate([blockdiag_out(c_re.astype(F32)), -blockdiag_out(c_im.astype(F32))], axis=1).astype(BF16)

    a_r = ab_re.reshape(S5_TILES, 1, S5_TILE_STATES)
    a_i = ab_im.reshape(S5_TILES, 1, S5_TILE_STATES)

    def cmul(xr, xi, yr, yi):
        return xr * yr - xi * yi, xr * yi + xi * yr

    a2 = cmul(a_r, a_i, a_r, a_i)
    a4 = cmul(*a2, *a2)
    row = jnp.arange(SUBLANES)[None, :, None]
    tabs = []
    for s, (pr_, pi_) in ((1, (a_r, a_i)), (2, a2), (4, a4)):
        tabs.append(jnp.where(row >= s, pr_, 0.0))
        tabs.append(jnp.where(row >= s, pi_, 0.0))
    pw = [(a_r, a_i)]
    for _ in range(SUBLANES - 1):
        pw.append(cmul(*pw[-1], a_r, a_i))
    tabs.append(jnp.concatenate([p[0] for p in pw], axis=1))
    tabs.append(jnp.concatenate([p[1] for p in pw], axis=1))
    tab = jnp.stack([jnp.broadcast_to(t, (S5_TILES, SUBLANES, S5_TILE_STATES)) for t in tabs], axis=1)
    return b_cat, c_cat, tab.astype(F32)


def _s5_scan(proj, u_cb, b_cat, c_cat, d_skip, tab, *, tc):
    seq = proj.shape[0]
    ns = S5_TILE_STATES
    d = d_skip.astype(F32).reshape(S5_TILES, 1, LANES)
    return pl.pallas_call(
        functools.partial(_s5_scan_kernel, tc=tc),
        grid=(S5_TILES, seq // tc),
        in_specs=[pl.BlockSpec((tc, LANES), lambda j, c: (c, u_cb + j)),
                  pl.BlockSpec((None, LANES, 2 * ns), lambda j, c: (j, 0, 0)),
                  pl.BlockSpec((None, 2 * ns, LANES), lambda j, c: (j, 0, 0)),
                  pl.BlockSpec((None, 1, LANES), lambda j, c: (j, 0, 0)),
                  pl.BlockSpec((None, 8, SUBLANES, ns), lambda j, c: (j, 0, 0, 0))],
        out_specs=pl.BlockSpec((tc, LANES), lambda j, c: (c, j)),
        out_shape=jax.ShapeDtypeStruct((seq, S5_WIDTH), F32),
        scratch_shapes=[pltpu.VMEM((tc, 2 * ns), F32), pltpu.VMEM((SUBLANES, 2 * ns), F32)],
        compiler_params=_cparams("parallel", "arbitrary"),
        name="s5_scan",
    )(proj, b_cat, c_cat, d, tab)


def _s5_glu_kernel(z_ref, w_ref, g_ref, o_ref, *, tn):
    j = pl.program_id(1)
    z = z_ref[...]
    a = jnp.dot(z.astype(BF16), w_ref[...].astype(BF16), preferred_element_type=F32)
    zc = z_ref[:, pl.ds(pl.multiple_of(j * tn, tn), tn)]
    o_ref[...] = (zc * jax.nn.sigmoid(a) * _silu(g_ref[...])).astype(o_ref.dtype)


def _s5_glu(z, w_glu, proj, g_cb, *, tm, tn):
    seq, n = z.shape
    return pl.pallas_call(
        functools.partial(_s5_glu_kernel, tn=tn),
        grid=(seq // tm, n // tn),
        in_specs=[pl.BlockSpec((tm, n), lambda i, j: (i, 0)),
                  pl.BlockSpec((n, tn), lambda i, j: (0, j)),
                  pl.BlockSpec((tm, tn), lambda i, j: (i, g_cb * (n // tn) + j))],
        out_specs=pl.BlockSpec((tm, tn), lambda i, j: (i, j)),
        out_shape=jax.ShapeDtypeStruct((seq, n), BF16),
        compiler_params=_cparams("parallel", "arbitrary"),
        name="s5_glu",
    )(z, w_glu, proj)


def _rope_tables(pos, freq):
    ang = pos * freq
    lane = lax.broadcasted_iota(I32, ang.shape, 1)
    half = MLA_ROPE // 2
    cos = jnp.cos(ang)
    sin = jnp.sin(ang)
    c = jnp.where(lane < MLA_ROPE, cos, 0.0)
    s1 = jnp.where(lane < half, -sin, 0.0)
    s2 = jnp.where((lane >= half) & (lane < MLA_ROPE), sin, 0.0)
    return c, s1, s2


def _rope_apply(x, c, s1, s2):
    half = MLA_ROPE // 2
    return x * c + pltpu.roll(x, LANES - half, 1) * s1 + pltpu.roll(x, half, 1) * s2


def _mla_up_kernel(cq_ref, ckv_ref, kr_ref, pos_ref, freq_ref, gq_ref, gkv_ref, wq_ref, wkv_ref, q_ref, k_ref, v_ref):
    def normed(x_ref, g_ref):
        x = x_ref[...]
        ms = jnp.mean(x * x, axis=-1, keepdims=True)
        return (x * lax.rsqrt(ms + RMS_EPS) * g_ref[...]).astype(BF16)

    qf = jnp.dot(normed(cq_ref, gq_ref), wq_ref[...], preferred_element_type=F32)
    kvf = jnp.dot(normed(ckv_ref, gkv_ref), wkv_ref[...], preferred_element_type=F32)
    c, s1, s2 = _rope_tables(pos_ref[...], freq_ref[...])
    kr = _rope_apply(kr_ref[...], c, s1, s2).astype(BF16)
    for h in range(MLA_HEADS):
        b = 2 * LANES * h
        q_ref[:, b:b + LANES] = qf[:, b:b + LANES].astype(BF16)
        q_ref[:, b + LANES:b + 2 * LANES] = _rope_apply(qf[:, b + LANES:b + 2 * LANES], c, s1, s2).astype(BF16)
        k_ref[:, b:b + LANES] = kvf[:, b:b + LANES].astype(BF16)
        k_ref[:, b + LANES:b + 2 * LANES] = kr
        v_ref[:, LANES * h:LANES * (h + 1)] = kvf[:, b + LANES:b + 2 * LANES].astype(BF16)


def _mla_up(proj, cq_cb, ckv_cb, kr_cb, g_cq, g_ckv, w_uq, w_ukv, pos_col, freq, *, t):
    seq = proj.shape[0]
    w = 2 * LANES * MLA_HEADS
    rq, rkv = w_uq.shape[0], w_ukv.shape[0]
    return pl.pallas_call(
        _mla_up_kernel,
        grid=(seq // t,),
        in_specs=[pl.BlockSpec((t, rq), lambda i: (i, cq_cb)),
                  pl.BlockSpec((t, rkv), lambda i: (i, ckv_cb)),
                  pl.BlockSpec((t, LANES), lambda i: (i, kr_cb)),
                  pl.BlockSpec((t, 1), lambda i: (i, 0)),
                  pl.BlockSpec((1, LANES), lambda i: (0, 0)),
                  pl.BlockSpec((1, rq), lambda i: (0, 0)),
                  pl.BlockSpec((1, rkv), lambda i: (0, 0)),
                  pl.BlockSpec((rq, w), lambda i: (0, 0)),
                  pl.BlockSpec((rkv, w), lambda i: (0, 0))],
        out_specs=[pl.BlockSpec((t, w), lambda i: (i, 0)),
                   pl.BlockSpec((t, w), lambda i: (i, 0)),
                   pl.BlockSpec((t, MLA_WIDTH), lambda i: (i, 0))],
        out_shape=[jax.ShapeDtypeStruct((seq, w), BF16), jax.ShapeDtypeStruct((seq, w), BF16),
                   jax.ShapeDtypeStruct((seq, MLA_WIDTH), BF16)],
        compiler_params=_cparams("parallel"),
        name="mla_up",
    )(proj, proj, proj, pos_col, freq, g_cq.reshape(1, rq), g_ckv.reshape(1, rkv), w_uq, w_ukv)


def _t5_bucket(dist):
    n = jnp.maximum(dist, 0)
    max_exact = T5_BUCKETS // 2
    log_ratio = jnp.log(jnp.maximum(n, 1).astype(F32) / max_exact) / math.log(T5_MAX_DIST / max_exact)
    large = jnp.minimum(max_exact + (log_ratio * (T5_BUCKETS - max_exact)).astype(I32), T5_BUCKETS - 1)
    return jnp.where(n < max_exact, n, large)


T5_MASK_BUCKET = T5_BUCKETS


def _t5_lookup(table_row, bucket):
    rows, width = bucket.shape
    tab = jnp.broadcast_to(table_row, (rows, LANES))
    parts = [jnp.take_along_axis(tab, bucket[:, c:c + LANES], axis=1, mode="promise_in_bounds")
             for c in range(0, width, LANES)]
    return parts[0] if len(parts) == 1 else jnp.concatenate(parts, axis=1)


def _nsa_cmp_kernel(x_ref, pe_ref, w1_ref, w2_ref, o_ref, xf_ref, *, nc):
    half = NSA_CMP_LEN // 2
    d = NSA_HEAD_DIM
    xf_ref[...] = x_ref[...].astype(F32)
    u = jnp.zeros((nc, NSA_CMP_HIDDEN), F32)
    v = jnp.zeros((nc, NSA_CMP_HIDDEN), F32)
    for r in range(half):
        a = xf_ref[pl.ds(r, nc, stride=NSA_CMP_STRIDE), :]
        u = u + jnp.dot((a + pe_ref[r:r + 1, :]).astype(BF16), w1_ref[r * d:(r + 1) * d, :].astype(BF16),
                        preferred_element_type=F32)
        v = v + jnp.dot((a + pe_ref[half + r:half + r + 1, :]).astype(BF16),
                        w1_ref[(half + r) * d:(half + r + 1) * d, :].astype(BF16), preferred_element_type=F32)
    hid = u + pltpu.roll(v, nc - 1, 0)
    o_ref[...] = jnp.dot(jax.nn.gelu(hid).astype(BF16), w2_ref[...].astype(BF16),
                         preferred_element_type=F32).astype(o_ref.dtype)


def _nsa_compress(proj, k_cb, pe, w1, w2):
    seq = proj.shape[0]
    nc = seq // NSA_CMP_STRIDE
    d = NSA_HEAD_DIM
    g = NSA_KV_GROUPS
    return pl.pallas_call(
        functools.partial(_nsa_cmp_kernel, nc=nc),
        grid=(2, g),
        in_specs=[pl.BlockSpec((seq, d), lambda a, b: (0, k_cb + a * g + b)),
                  pl.BlockSpec((None, NSA_CMP_LEN, d), lambda a, b: (a, 0, 0)),
                  pl.BlockSpec((None, NSA_CMP_LEN * d, NSA_CMP_HIDDEN), lambda a, b: (a, 0, 0)),
                  pl.BlockSpec((None, NSA_CMP_HIDDEN, d), lambda a, b: (a, 0, 0))],
        out_specs=pl.BlockSpec((None, None, nc, d), lambda a, b: (a, b, 0, 0)),
        out_shape=jax.ShapeDtypeStruct((2, g, nc, d), BF16),
        scratch_shapes=[pltpu.VMEM((seq, d), F32)],
        compiler_params=_cparams("parallel", "arbitrary"),
        name="nsa_compress",
    )(proj, pe, w1, w2)


def _nsa_select_kernel(consec_ref, q_ref, kc_ref, vc_ref, posq_ref, posc_ref, tab_ref, gate_ref, ov_ref,
                       oc_ref, sel_ref, strip_ref, *, nc, n_slc, tq):
    qi = pl.program_id(0)
    d = NSA_HEAD_DIM
    c = d ** -0.5 * LOG2E
    per_block = tq // NSA_CMP_STRIDE
    lane = lax.broadcasted_iota(I32, (1, tq), 1)
    tok = qi * tq + lane

    def select(bias):
        gates = jax.nn.sigmoid(gate_ref[...])
        ovt = ov_ref[...]
        js = lax.broadcasted_iota(I32, (LANES, tq), 0)
        jf = js.astype(F32)
        cur = tok // NSA_SLC_BLOCK
        forced = (js == 0) | (js == cur) | (js == cur - 1)
        for g in range(NSA_KV_GROUPS):
            kc = kc_ref[g]
            vct = vc_ref[g].astype(F32).T.astype(BF16)
            psum = jnp.zeros((nc, tq), F32)
            for r in range(NSA_REP):
                h = g * NSA_REP + r
                qt = (q_ref[:, h * d:(h + 1) * d].astype(F32) * c).T.astype(BF16)
                t = jnp.dot(kc, qt, preferred_element_type=F32) + bias(h)
                m = jnp.max(t, axis=0, keepdims=True)
                e = jnp.exp2(t - m)
                l = jnp.sum(e, axis=0, keepdims=True)
                p = e * jnp.where(m > 0.5 * NEG_INF, 1.0 / l, 0.0)
                o = jnp.dot(vct, p.astype(BF16), preferred_element_type=F32)
                oc_ref[:, h * d:(h + 1) * d] = gates[:, 3 * h:3 * h + 1] * o.T
                psum = psum + p
            p_hi = psum.astype(BF16)
            p_lo = (psum - p_hi.astype(F32)).astype(BF16)
            imp = jnp.dot(ovt, p_hi, preferred_element_type=F32) + jnp.dot(ovt, p_lo, preferred_element_type=F32)
            st = jnp.where(forced, FORCE_SCORE, jnp.where(js > cur, -1.0, imp))
            st = jnp.where(js < n_slc, st, -2.0)
            sel = jnp.zeros((LANES, tq), F32)
            for _ in range(NSA_SLC_TOPK):
                mx = jnp.max(st, axis=0, keepdims=True)
                first = jnp.min(jnp.where(st == mx, jf, float(LANES)), axis=0, keepdims=True)
                hit = jf == first
                sel = jnp.where(hit, 1.0, sel)
                st = jnp.where(hit, -3e38, st)
            sel_ref[g] = sel.astype(sel_ref.dtype)

    consecutive = consec_ref[0] == 1

    @pl.when(jnp.logical_and(consecutive, qi == 0))
    def _():
        for chunk in range(2 * nc // LANES):
            rel = chunk * LANES - nc + lax.broadcasted_iota(I32, (LANES, 1), 0)
            dist = lane - (rel * NSA_CMP_STRIDE + (NSA_CMP_LEN - 1))
            bucket = jnp.where(dist >= 0, _t5_bucket(dist), T5_MASK_BUCKET)
            for h in range(NSA_HEADS):
                strip_ref[h, chunk * LANES:(chunk + 1) * LANES, :] = _t5_lookup(tab_ref[h:h + 1, :], bucket)

    @pl.when(consecutive)
    def _():
        start = pl.multiple_of(nc - per_block * qi, per_block)
        select(lambda h: strip_ref[h, pl.ds(start, nc), :])

    @pl.when(jnp.logical_not(consecutive))
    def _():
        cmp_end = lax.broadcasted_iota(I32, (nc, 1), 0) * NSA_CMP_STRIDE + (NSA_CMP_LEN - 1)
        pos_c = jnp.concatenate([posc_ref[...]] * (tq // LANES), axis=1)
        bucket = jnp.where(cmp_end <= tok, _t5_bucket(posq_ref[...] - pos_c), T5_MASK_BUCKET)
        select(lambda h: _t5_lookup(tab_ref[h:h + 1, :], bucket))


def _nsa_select(proj_b, q_cb, proj_f, gate_cb, kc, vc, pos_row, pos_cmp_rep, consec, tab_t, ov_t, *, n_slc, tq):
    seq = proj_b.shape[0]
    nc = kc.shape[1]
    g = NSA_KV_GROUPS
    d = NSA_HEAD_DIM
    grid_spec = pltpu.PrefetchScalarGridSpec(
        num_scalar_prefetch=1,
        grid=(seq // tq,),
        in_specs=[pl.BlockSpec((tq, NSA_WIDTH), lambda i, *_: (i, q_cb)),
                  pl.BlockSpec((g, nc, d), lambda i, *_: (0, 0, 0)),
                  pl.BlockSpec((g, nc, d), lambda i, *_: (0, 0, 0)),
                  pl.BlockSpec((1, tq), lambda i, *_: (0, i)),
                  pl.BlockSpec((nc, LANES), lambda i, *_: (0, 0)),
                  pl.BlockSpec((SUBLANES, LANES), lambda i, *_: (0, 0)),
                  pl.BlockSpec((tq, LANES), lambda i, *_: (i, gate_cb)),
                  pl.BlockSpec((LANES, nc), lambda i, *_: (0, 0))],
        out_specs=[pl.BlockSpec((tq, NSA_WIDTH), lambda i, *_: (i, 0)),
                   pl.BlockSpec((g, LANES, tq), lambda i, *_: (0, 0, i))],
        scratch_shapes=[pltpu.VMEM((NSA_HEADS, 2 * nc, tq), F32)])
    return pl.pallas_call(
        functools.partial(_nsa_select_kernel, nc=nc, n_slc=n_slc, tq=tq),
        grid_spec=grid_spec,
        out_shape=[jax.ShapeDtypeStruct((seq, NSA_WIDTH), F32),
                   jax.ShapeDtypeStruct((g, LANES, seq), F32)],
        compiler_params=_cparams("arbitrary"),
        name="nsa_select",
    )(consec, proj_b, kc, vc, pos_row, pos_cmp_rep, tab_t * LOG2E, proj_f, ov_t)


def _nsa_slc_kernel(pqmin_ref, pkmax_ref, consec_ref, q_ref, k_ref, v_ref, sel_ref, posq_ref, posk_ref, tab_ref,
                    o_ref, vt_ref, qt_ref, s_ref, m_ref, l_ref, acc_ref, cache_ref, *, tq, tk):
    g = pl.program_id(0)
    qi = pl.program_id(1)
    d = NSA_HEAD_DIM
    scale = d ** -0.5
    seq = k_ref.shape[0]
    consecutive = consec_ref[0] == 1
    n_cached = cache_ref.shape[0]

    @pl.when(qi == 0)
    def _():
        _build_vt(v_ref, vt_ref, 0, d, seq, tk)

    @pl.when(jnp.logical_and(consecutive, qi == 0))
    def _():
        rel = lax.broadcasted_iota(I32, (1, tq), 1) - lax.broadcasted_iota(I32, (tk, 1), 0)
        for v in range(n_cached):
            dist = rel + v * tq
            bucket = jnp.where(dist >= 0, _t5_bucket(dist), T5_MASK_BUCKET)
            for r in range(NSA_REP):
                cache_ref[v, r] = _t5_lookup(tab_ref[pl.ds(g * NSA_REP + r, 1), :], bucket)

    for r in range(NSA_REP):
        qt_ref[r] = (q_ref[:, r * d:(r + 1) * d].astype(F32) * (scale * LOG2E)).T.astype(BF16)
    m_ref[...] = jnp.full(m_ref.shape, NEG_INF, F32)
    l_ref[...] = jnp.zeros(l_ref.shape, F32)
    acc_ref[...] = jnp.zeros(acc_ref.shape, F32)
    pos_q = posq_ref[...]
    per_tile = tk // NSA_SLC_BLOCK
    n_full = (qi * tq) // tk

    def scores(j, slot):
        start = pl.multiple_of(j * tk, tk)
        k = k_ref[pl.ds(start, tk), :]
        for r in range(NSA_REP):
            s_ref[slot, r] = jnp.dot(k, qt_ref[r], preferred_element_type=F32)

    def softmax_pv(j, slot, masked, mode):
        start = pl.multiple_of(j * tk, tk)
        picked = jnp.concatenate(
            [jnp.broadcast_to(sel_ref[pl.ds(j * per_tile + b, 1), :], (NSA_SLC_BLOCK, tq)) for b in range(per_tile)],
            axis=0)
        if masked and mode == "gather":
            key = start + lax.broadcasted_iota(I32, (tk, tq), 0)
            qry = qi * tq + lax.broadcasted_iota(I32, (tk, tq), 1)
            picked = jnp.where(key <= qry, picked, 0.0)
        mask = picked > 0.5
        if mode == "gather":
            pos_k = jnp.concatenate([posk_ref[pl.ds(start, tk), :]] * (tq // LANES), axis=1)
            bucket = jnp.where(mask, _t5_bucket(pos_q - pos_k), T5_MASK_BUCKET)
        for r in range(NSA_REP):
            h = g * NSA_REP + r
            t = s_ref[slot, r]
            if mode == "gather":
                t = t + _t5_lookup(tab_ref[pl.ds(h, 1), :], bucket)
            elif mode == "cached":
                t = jnp.where(mask, t + cache_ref[(qi * tq - start) // tq, r], NEG_INF)
            else:
                t = jnp.where(mask, t, NEG_INF)
            m_prev = m_ref[r]
            mx = jnp.max(t, axis=0, keepdims=True)
            if mode == "far":
                b = tab_ref[pl.ds(h, 1), T5_BUCKETS - 1:T5_BUCKETS]
                m_new = jnp.maximum(m_prev, mx + b)
                shift = m_new - b
            else:
                m_new = jnp.maximum(m_prev, mx)
                shift = m_new
            alpha = jnp.exp2(m_prev - m_new)
            p = jnp.exp2(t - shift)
            l_ref[r] = alpha * l_ref[r] + jnp.sum(p, axis=0, keepdims=True)
            acc_ref[r] = alpha * acc_ref[r] + jnp.dot(vt_ref[0, :, pl.ds(start, tk)], p.astype(BF16),
                                                       preferred_element_type=F32)
            m_ref[r] = m_new

    def is_far(j):
        return pqmin_ref[qi] - pkmax_ref[j] >= T5_MAX_DIST

    def stages(j, count, mode):
        for k in range(count):
            scores(j + k + 1, (k + 1) % 2)
            softmax_pv(j + k, k % 2, False, mode)

    def near(fn):
        @pl.when(consecutive)
        def _():
            fn("cached")

        @pl.when(jnp.logical_not(consecutive))
        def _():
            fn("gather")

    def far_or_near(far, fn):
        @pl.when(far)
        def _():
            fn("far")

        @pl.when(jnp.logical_not(far))
        def _():
            near(fn)

    scores(0, 0)

    def pair(jj, carry):
        j = 2 * jj
        far_or_near(jnp.logical_and(is_far(j), is_far(j + 1)), lambda mode: stages(j, 2, mode))
        return carry

    pairs = n_full // 2
    lax.fori_loop(0, pairs, pair, 0)
    last = 2 * pairs

    @pl.when(n_full % 2 == 1)
    def _():
        far_or_near(is_far(last), lambda mode: stages(last, 1, mode))
        near(lambda mode: softmax_pv(last + 1, 1, True, mode))

    @pl.when(n_full % 2 == 0)
    def _():
        near(lambda mode: softmax_pv(last, 0, True, mode))

    for r in range(NSA_REP):
        o_ref[:, r * d:(r + 1) * d] = (acc_ref[r] / l_ref[r]).T


def _nsa_slc(proj_b, q_off, k_off, v_off, sel, pos_row, pos_rep, tab_t, pq_min, pk_max, consec, *, tq, tk):
    seq = proj_b.shape[0]
    tab_t = tab_t * LOG2E
    d = NSA_HEAD_DIM
    gw = NSA_REP * d
    once = pl.Buffered(1)
    n_cached = (T5_MAX_DIST + tk - 2) // tq + 1 + tk // tq
    grid_spec = pltpu.PrefetchScalarGridSpec(
        num_scalar_prefetch=3,
        grid=(NSA_KV_GROUPS, seq // tq),
        in_specs=[pl.BlockSpec((tq, gw), lambda g, i, *_: (i, q_off // gw + g)),
                  pl.BlockSpec((seq, d), lambda g, i, *_: (0, k_off // d + g), pipeline_mode=once),
                  pl.BlockSpec((seq, d), lambda g, i, *_: (0, v_off // d + g), pipeline_mode=once),
                  pl.BlockSpec((None, LANES, tq), lambda g, i, *_: (g, 0, i)),
                  pl.BlockSpec((1, tq), lambda g, i, *_: (0, i)),
                  pl.BlockSpec((seq, LANES), lambda g, i, *_: (0, 0), pipeline_mode=once),
                  pl.BlockSpec((SUBLANES, LANES), lambda g, i, *_: (0, 0))],
        out_specs=pl.BlockSpec((tq, gw), lambda g, i, *_: (i, g)),
        scratch_shapes=[pltpu.VMEM((1, d, seq), BF16), pltpu.VMEM((NSA_REP, d, tq), BF16),
                        pltpu.VMEM((2, NSA_REP, tk, tq), F32), pltpu.VMEM((NSA_REP, 1, tq), F32),
                        pltpu.VMEM((NSA_REP, 1, tq), F32), pltpu.VMEM((NSA_REP, d, tq), F32),
                        pltpu.VMEM((n_cached, NSA_REP, tk, tq), F32)])
    return pl.pallas_call(
        functools.partial(_nsa_slc_kernel, tq=tq, tk=tk),
        grid_spec=grid_spec,
        out_shape=jax.ShapeDtypeStruct((seq, NSA_WIDTH), F32),
        compiler_params=_cparams("parallel", "arbitrary"),
        name="nsa_slc",
    )(pq_min, pk_max, consec, proj_b, proj_b, proj_b, sel, pos_row, pos_rep, tab_t)


def _nsa_win_kernel(*refs, tq, nt):
    consec_ref = refs[0]
    q_ref = refs[1]
    k_refs = refs[2:2 + nt]
    v_refs = refs[2 + nt:2 + 2 * nt]
    pk_refs = refs[2 + 2 * nt:2 + 3 * nt]
    posq_ref, tab_ref, gate_ref, gout_ref, oc_ref, os_ref, o_ref, bias_ref = refs[2 + 3 * nt:]
    qi = pl.program_id(0)
    d = NSA_HEAD_DIM
    c = d ** -0.5 * LOG2E
    sub = lax.broadcasted_iota(I32, (tq, 1), 0)
    lane = lax.broadcasted_iota(I32, (1, tq), 1)

    def band_bucket(kidx, tok, dist):
        lower = jnp.maximum(tok - (NSA_WINDOW - 1), 0)
        b = jnp.where(kidx >= lower, _t5_bucket(dist), T5_MASK_BUCKET)
        return jnp.where(kidx <= tok, b, T5_MASK_BUCKET)

    def attend(bias):
        gates = jax.nn.sigmoid(gate_ref[...])
        for g in range(NSA_KV_GROUPS):
            ks = [kr[:, g * d:(g + 1) * d] for kr in k_refs]
            vts = [vr[:, g * d:(g + 1) * d].astype(F32).T.astype(BF16) for vr in v_refs]
            for r in range(NSA_REP):
                h = g * NSA_REP + r
                hs = slice(h * d, (h + 1) * d)
                qt = (q_ref[:, hs].astype(F32) * c).T.astype(BF16)
                ts = [jnp.dot(ks[jj], qt, preferred_element_type=F32) + bias(h, jj) for jj in range(nt)]
                m = functools.reduce(jnp.maximum, [jnp.max(t, axis=0, keepdims=True) for t in ts])
                ps = [jnp.exp2(t - m) for t in ts]
                l = functools.reduce(jnp.add, [jnp.sum(p, axis=0, keepdims=True) for p in ps])
                o_t = functools.reduce(jnp.add, [jnp.dot(vt, p.astype(BF16), preferred_element_type=F32)
                                                 for vt, p in zip(vts, ps)])
                o_w = (o_t / l).T
                o = (oc_ref[:, hs] + gates[:, 3 * h + 1:3 * h + 2] * os_ref[:, hs]
                     + gates[:, 3 * h + 2:3 * h + 3] * o_w)
                o_ref[:, hs] = (o * _silu(gout_ref[:, hs])).astype(o_ref.dtype)

    consecutive = consec_ref[0] == 1

    @pl.when(jnp.logical_and(consecutive, qi == 0))
    def _():
        tok0 = (nt - 1) * tq + lane
        for jj in range(nt):
            kidx0 = jj * tq + sub
            bucket = band_bucket(kidx0, tok0, tok0 - kidx0)
            for h in range(NSA_HEADS):
                bias_ref[h, jj] = _t5_lookup(tab_ref[h:h + 1, :], bucket)

    cached = jnp.logical_and(consecutive, qi >= nt - 1)

    @pl.when(cached)
    def _():
        attend(lambda h, jj: bias_ref[h, jj])

    @pl.when(jnp.logical_not(cached))
    def _():
        tok = qi * tq + lane
        pos_q = posq_ref[...]
        buckets = []
        for jj in range(nt):
            kidx = (qi - (nt - 1) + jj) * tq + sub
            pos_k = jnp.concatenate([pk_refs[jj][...]] * (tq // LANES), axis=1)
            buckets.append(band_bucket(kidx, tok, pos_q - pos_k))
        attend(lambda h, jj: _t5_lookup(tab_ref[h:h + 1, :], buckets[jj]))


def _nsa_win(proj_b, q_cb, k_cb, v_cb, proj_f, gate_cb, gout_cb, oc, o_s, pos_row, pos_rep, consec, tab_t, *, tq):
    seq = proj_b.shape[0]
    nt = NSA_WINDOW // tq + 1

    def band_rows(jj, cb):
        return pl.BlockSpec((tq, NSA_KV), lambda i, *_: (jnp.maximum(i - (nt - 1) + jj, 0), cb))

    def band_pos(jj):
        return pl.BlockSpec((tq, LANES), lambda i, *_: (jnp.maximum(i - (nt - 1) + jj, 0), 0))

    in_specs = [pl.BlockSpec((tq, NSA_WIDTH), lambda i, *_: (i, q_cb))]
    in_specs += [band_rows(jj, k_cb) for jj in range(nt)]
    in_specs += [band_rows(jj, v_cb) for jj in range(nt)]
    in_specs += [band_pos(jj) for jj in range(nt)]
    in_specs += [pl.BlockSpec((1, tq), lambda i, *_: (0, i)),
                 pl.BlockSpec((SUBLANES, LANES), lambda i, *_: (0, 0)),
                 pl.BlockSpec((tq, LANES), lambda i, *_: (i, gate_cb)),
                 pl.BlockSpec((tq, NSA_WIDTH), lambda i, *_: (i, gout_cb)),
                 pl.BlockSpec((tq, NSA_WIDTH), lambda i, *_: (i, 0)),
                 pl.BlockSpec((tq, NSA_WIDTH), lambda i, *_: (i, 0))]
    args = [proj_b] * (1 + 2 * nt) + [pos_rep] * nt + [pos_row, tab_t * LOG2E, proj_f, proj_f, oc, o_s]
    grid_spec = pltpu.PrefetchScalarGridSpec(
        num_scalar_prefetch=1,
        grid=(seq // tq,),
        in_specs=in_specs,
        out_specs=pl.BlockSpec((tq, NSA_WIDTH), lambda i, *_: (i, 0)),
        scratch_shapes=[pltpu.VMEM((NSA_HEADS, nt, tq, tq), F32)])
    return pl.pallas_call(
        functools.partial(_nsa_win_kernel, tq=tq, nt=nt),
        grid_spec=grid_spec,
        out_shape=jax.ShapeDtypeStruct((seq, NSA_WIDTH), BF16),
        compiler_params=_cparams("arbitrary"),
        name="nsa_win",
    )(consec, *args)


def _in_proj(h, norm_g, w_b, w_f, tiles, name):
    tn_b = w_b.shape[1] // tiles["col_tiles_b"]
    tn_f = w_f.shape[1] // tiles["col_tiles_f"]
    assert tn_b % MXU_COLS == 0 and tn_f % MXU_COLS == 0
    proj_b = _norm_matmul(h, norm_g, w_b, tm=tiles["tm"], tn=tn_b, name=name + "_b", out_dtype=BF16)
    proj_f = _norm_matmul(h, norm_g, w_f, tm=tiles["tm"], tn=tn_f, name=name + "_f")
    return proj_b, proj_f


def _even_layer(h, mem_kv, norm_g, w_in, s5, w_glu, b_f, tiles):
    pb, pf = _in_proj(h, norm_g, *w_in, tiles, "in_proj_even")
    ob, of = EVEN_B_OFF, EVEN_F_OFF
    b_cat, c_cat, tab, d_skip = s5
    z = _s5_scan(pf, of["u"] // LANES, b_cat, c_cat, d_skip, tab, tc=tiles["s5_tc"])
    y_s5 = _s5_glu(z, w_glu, pf, of["g_s5"] // S5_WIDTH, tm=tiles["tm"], tn=tiles["tn_glu"])
    decay = _decay(pf, of["f"] // LANES, b_f, t=tiles["decay_t"])
    d = FOX_HEAD_DIM
    y_fox = _flash(pb, ob["q"], pb, ob["k"], pb, ob["v"], pf, of["g_fox"], heads=FOX_HEADS, dk=d, dv=d,
                   scale=d ** -0.5, tq=tiles["attn_tq"], tk=tiles["attn_tk"], hp=tiles["attn_hp"], decay=decay,
                   name="fox_attn")
    y_mem = _mem_attn(pb, ob["q_mem"] // MEM_HEAD_DIM, pf, of["g_mem"] // MEM_HEAD_DIM, mem_kv, t=tiles["mem_t"])
    return y_s5, y_fox, y_mem


def _odd_layer(h, mem_kv, norm_g, w_in, mla, nsa, pos, tiles):
    pb, pf = _in_proj(h, norm_g, *w_in, tiles, "in_proj_odd")
    ob, of = ODD_B_OFF, ODD_F_OFF
    g_cq, g_ckv, w_uq, w_ukv, freq = mla
    pos_col_f, pos_row, pos_cmp_rep, pos_rep, pq_min, pk_max, consec = pos
    q_r, k_r, v_r = _mla_up(pf, of["c_q"] // MLA_Q_RANK, of["c_kv"] // MLA_KV_RANK, of["k_rope"] // LANES,
                            g_cq, g_ckv, w_uq, w_ukv, pos_col_f, freq, t=tiles["prep_t"])
    y_mla = _flash(q_r, 0, k_r, 0, v_r, 0, pf, of["g_mla"], heads=MLA_HEADS, dk=2 * LANES, dv=MLA_V,
                   scale=(MLA_NOPE + MLA_ROPE) ** -0.5, tq=tiles["attn_tq"], tk=tiles["attn_tk"], hp=tiles["attn_hp"],
                   name="mla_attn")
    pe, w1, w2, tab_t, ov, n_slc = nsa
    kvc = _nsa_compress(pb, ob["k_cmp"] // NSA_HEAD_DIM, pe, w1, w2)
    oc, sel = _nsa_select(pb, ob["q_nsa"] // NSA_WIDTH, pf, of["gates"] // LANES, kvc[0], kvc[1], pos_row,
                          pos_cmp_rep, consec, tab_t, ov, n_slc=n_slc, tq=tiles["nsa_tq"])
    o_s = _nsa_slc(pb, ob["q_nsa"], ob["k_slc"], ob["v_slc"], sel, pos_row, pos_rep, tab_t, pq_min, pk_max, consec,
                   tq=tiles["slc_tq"], tk=tiles["slc_tk"])
    y_nsa = _nsa_win(pb, ob["q_nsa"] // NSA_WIDTH, ob["k_win"] // NSA_KV, ob["v_win"] // NSA_KV,
                     pf, of["gates"] // LANES, of["g_nsa"] // NSA_WIDTH, oc, o_s, pos_row, pos_rep, consec, tab_t,
                     tq=tiles["win_tq"])
    y_mem = _mem_attn(pb, ob["q_mem"] // MEM_HEAD_DIM, pf, of["g_mem"] // MEM_HEAD_DIM, mem_kv, t=tiles["mem_t"])
    return y_mla, y_nsa, y_mem


def _tiles(seq):
    return {"tm": min(seq, 1024), "col_tiles_b": 2, "col_tiles_f": 3, "tn_out": 1024, "tn_glu": 1024, "tn_mem": 512,
            "s5_tc": min(seq, 4096), "decay_t": min(seq, 512), "attn_tq": min(seq, 512),
            "attn_tk": min(seq, 512), "attn_hp": 2, "slc_tq": 256, "slc_tk": 512, "nsa_tq": 256, "win_tq": 128,
            "mem_t": min(seq, 1024), "norm_t": min(seq, 1024), "prep_t": min(seq, 512)}


def _context(positions, t5_table, seq, tiles):
    pos = positions[0]
    pos_col = pos.reshape(seq, 1)
    pos_row = pos.reshape(1, seq)
    pos_rep = jnp.broadcast_to(pos_col, (seq, LANES))
    pq_min = jnp.min(pos.reshape(seq // tiles["slc_tq"], tiles["slc_tq"]), axis=1)
    pk_max = jnp.max(pos.reshape(seq // tiles["slc_tk"], tiles["slc_tk"]), axis=1)
    nc = seq // NSA_CMP_STRIDE
    pos_cmp = jnp.pad(pos[NSA_CMP_LEN - 1::NSA_CMP_STRIDE], (0, 1))
    pos_cmp_rep = jnp.broadcast_to(pos_cmp.reshape(nc, 1), (nc, LANES))
    half = MLA_ROPE // 2
    inv_freq = ROPE_THETA ** (-jnp.arange(half, dtype=F32) / half)
    freq = jnp.concatenate([inv_freq, inv_freq, jnp.zeros((LANES - MLA_ROPE,), F32)]).reshape(1, LANES)
    tab_t = jnp.pad(t5_table.astype(F32).T, ((0, SUBLANES - NSA_HEADS), (0, LANES - T5_BUCKETS)))
    tab_t = tab_t.at[:, T5_MASK_BUCKET].set(NEG_INF)
    n_slc = seq // NSA_SLC_BLOCK
    cs = np.arange(nc) * NSA_CMP_STRIDE
    ss = np.arange(LANES) * NSA_SLC_BLOCK
    ov_np = np.clip(np.minimum(cs[:, None] + NSA_CMP_LEN, ss[None, :] + NSA_SLC_BLOCK)
                    - np.maximum(cs[:, None], ss[None, :]), 0, None) / NSA_CMP_LEN
    ov_np[nc - 1, :] = 0.0
    ov_np[:, n_slc:] = 0.0
    consec = jnp.all(pos[1:] - pos[:-1] == 1).astype(I32).reshape(1)
    return {"pos": (pos_col.astype(F32), pos_row, pos_cmp_rep, pos_rep, pq_min, pk_max, consec), "freq": freq,
            "tab_t": tab_t, "ov": jnp.asarray(ov_np.T, BF16), "n_slc": n_slc}


def _odd_params(i, mla_g_cq, mla_g_ckv, mla_w_uq, mla_w_ukv, nsa_cmp_pe, nsa_cmp_w1, nsa_cmp_w2, ctx):
    dq = MLA_NOPE + MLA_ROPE
    w_uq = mla_w_uq[i].reshape(MLA_Q_RANK, MLA_HEADS, dq)
    w_uq = jnp.pad(w_uq, ((0, 0), (0, 0), (0, 2 * LANES - dq))).reshape(MLA_Q_RANK, -1).astype(BF16)
    mla = (mla_g_cq[i], mla_g_ckv[i], w_uq, mla_w_ukv[i].astype(BF16), ctx["freq"])
    nsa = (nsa_cmp_pe[i].astype(F32), nsa_cmp_w1[i], nsa_cmp_w2[i], ctx["tab_t"],
           ctx["ov"], ctx["n_slc"])
    return mla, nsa


def kernel(x, mem, positions, norm_g, mem_norm_g, final_norm_g, t5_table, w_out, mem_w_kv, even_w_in, s5_lam_re,
           s5_lam_im, s5_log_dt, s5_b_re, s5_b_im, s5_c_re, s5_c_im, s5_d, s5_w_glu, fox_b_f, odd_w_in, mla_g_cq,
           mla_g_ckv, mla_w_uq, mla_w_ukv, nsa_cmp_pe, nsa_cmp_w1, nsa_cmp_w2):
    batch, seq, _ = x.shape
    assert batch == 1 and seq % 1024 == 0 and seq // NSA_SLC_BLOCK <= LANES
    depth = norm_g.shape[0]
    tiles = _tiles(seq)
    ctx = _context(positions, t5_table, seq, tiles)
    h = x[0]
    mem_kv_all = _mem_kv_all(mem[0], mem_norm_g, mem_w_kv, tn=tiles["tn_mem"])
    for layer in range(depth):
        i = layer // 2
        mem_kv = (mem_kv_all, layer)
        if layer % 2 == 0:
            w_in = (_reorder_w_in(even_w_in[i], EVEN_SPLITS, EVEN_B_ORDER),
                    _reorder_w_in(even_w_in[i], EVEN_SPLITS, EVEN_F_ORDER))
            b_cat, c_cat, tab = _s5_prepare(s5_lam_re[i], s5_lam_im[i], s5_log_dt[i], s5_b_re[i], s5_b_im[i],
                                            s5_c_re[i], s5_c_im[i])
            ys = _even_layer(h, mem_kv, norm_g[layer], w_in, (b_cat, c_cat, tab, s5_d[i]),
                             s5_w_glu[i], fox_b_f[i], tiles)
        else:
            w_in = (_reorder_w_in(odd_w_in[i], ODD_SPLITS, ODD_B_ORDER),
                    _reorder_w_in(odd_w_in[i], ODD_SPLITS, ODD_F_ORDER))
            mla, nsa = _odd_params(i, mla_g_cq, mla_g_ckv, mla_w_uq, mla_w_ukv, nsa_cmp_pe, nsa_cmp_w1, nsa_cmp_w2,
                                   ctx)
            ys = _odd_layer(h, mem_kv, norm_g[layer], w_in, mla, nsa, ctx["pos"], tiles)
        h = _out_proj(h, ys, w_out, layer, tm=tiles["tm"], tn=tiles["tn_out"])
    return _final_norm(h, final_norm_g, tm=tiles["norm_t"])[None]
```

```python
import functools
import math

import numpy as np
import jax
import jax.numpy as jnp
from jax import lax
from jax.experimental import pallas as pl
from jax.experimental.pallas import tpu as pltpu

F32 = jnp.float32
BF16 = jnp.bfloat16
I32 = jnp.int32

RMS_EPS = 1e-6
NEG_INF = -1e30
LOG2E = math.log2(math.e)

S5_WIDTH = 1024
S5_GROUP = 16
S5_GROUPS = S5_WIDTH // S5_GROUP
S5_STATE = 64
FOX_HEADS = 8
FOX_HEAD_DIM = 128
FOX_WIDTH = FOX_HEADS * FOX_HEAD_DIM
MEM_HEADS = 4
MEM_HEAD_DIM = 128
MEM_WIDTH = MEM_HEADS * MEM_HEAD_DIM
MLA_HEADS = 8
MLA_Q_RANK = 512
MLA_KV_RANK = 512
MLA_NOPE = 128
MLA_ROPE = 64
MLA_V = 128
MLA_WIDTH = MLA_HEADS * MLA_V
ROPE_THETA = 10000.0
NSA_HEADS = 8
NSA_KV_GROUPS = 2
NSA_REP = NSA_HEADS // NSA_KV_GROUPS
NSA_HEAD_DIM = 128
NSA_WIDTH = NSA_HEADS * NSA_HEAD_DIM
NSA_KV = NSA_KV_GROUPS * NSA_HEAD_DIM
NSA_CMP_LEN = 32
NSA_CMP_STRIDE = 16
NSA_CMP_HIDDEN = 256
NSA_SLC_BLOCK = 64
NSA_SLC_TOPK = 16
NSA_WINDOW = 512
FORCE_SCORE = 1e6
T5_BUCKETS = 32
T5_MAX_DIST = 1024

EVEN_SPLITS = (S5_WIDTH, S5_WIDTH, FOX_WIDTH, FOX_WIDTH, FOX_WIDTH, FOX_HEADS, FOX_WIDTH, MEM_WIDTH, MEM_WIDTH)
ODD_SPLITS = (MLA_Q_RANK, MLA_KV_RANK, MLA_ROPE, MLA_WIDTH, NSA_WIDTH, NSA_KV, NSA_KV, NSA_KV, NSA_KV, NSA_KV,
              NSA_KV, 3 * NSA_HEADS, NSA_WIDTH, MEM_WIDTH, MEM_WIDTH)

LANES = 128
SUBLANES = 8
MXU_COLS = 256
VMEM_LIMIT_BYTES = 56 * 1024 * 1024

EVEN_B_ORDER = (("q", 2, 1024), ("k", 3, 1024), ("v", 4, 1024), ("q_mem", 7, 512))
EVEN_F_ORDER = (("u", 0, 1024), ("g_s5", 1, 1024), ("g_fox", 6, 1024), ("g_mem", 8, 512), ("f", 5, 128),
                ("pad", None, 128))
ODD_B_ORDER = (("q_nsa", 4, 1024), ("q_mem", 13, 512), ("k_cmp", 5, 256), ("v_cmp", 6, 256), ("k_slc", 7, 256),
               ("v_slc", 8, 256), ("k_win", 9, 256), ("v_win", 10, 256))
ODD_F_ORDER = (("g_nsa", 12, 1024), ("g_mla", 3, 1024), ("c_q", 0, 512), ("c_kv", 1, 512), ("g_mem", 14, 512),
               ("k_rope", 2, 128), ("gates", 11, 128))


def _layout(order):
    off, out = 0, {}
    for name, _, width in order:
        assert off % width == 0
        out[name] = off
        off += width
    return out, off


EVEN_B_OFF, EVEN_B_N = _layout(EVEN_B_ORDER)
EVEN_F_OFF, EVEN_F_N = _layout(EVEN_F_ORDER)
ODD_B_OFF, ODD_B_N = _layout(ODD_B_ORDER)
ODD_F_OFF, ODD_F_N = _layout(ODD_F_ORDER)


def _reorder_w_in(w, splits, order):
    starts = np.concatenate([[0], np.cumsum(splits)])
    cols = []
    for _, idx, width in order:
        if idx is None:
            cols.append(jnp.zeros((w.shape[0], width), w.dtype))
            continue
        seg = w[:, int(starts[idx]):int(starts[idx + 1])]
        pad = width - seg.shape[1]
        if pad:
            seg = jnp.pad(seg, ((0, 0), (0, pad)))
        cols.append(seg)
    return jnp.concatenate(cols, axis=1).astype(BF16)


def _cparams(*sem):
    return pltpu.CompilerParams(dimension_semantics=sem, vmem_limit_bytes=VMEM_LIMIT_BYTES)


def _silu(g):
    return g * jax.nn.sigmoid(g)


def _norm_matmul_kernel(x_ref, g_ref, w_ref, o_ref, xn_ref):
    @pl.when(pl.program_id(1) == 0)
    def _():
        x = x_ref[...]
        ms = jnp.mean(x * x, axis=-1, keepdims=True)
        xn_ref[...] = (x * lax.rsqrt(ms + RMS_EPS) * g_ref[...]).astype(BF16)

    o_ref[...] = jnp.dot(xn_ref[...], w_ref[...].astype(BF16), preferred_element_type=F32).astype(o_ref.dtype)


def _norm_matmul(x, g, w, *, x_cb=0, tm, tn, name, out_dtype=F32):
    m = x.shape[0]
    k, n = w.shape
    return pl.pallas_call(
        _norm_matmul_kernel,
        grid=(m // tm, n // tn),
        in_specs=[pl.BlockSpec((tm, k), lambda i, j: (i, x_cb)),
                  pl.BlockSpec((1, k), lambda i, j: (0, 0)),
                  pl.BlockSpec((k, tn), lambda i, j: (0, j))],
        out_specs=pl.BlockSpec((tm, tn), lambda i, j: (i, j)),
        out_shape=jax.ShapeDtypeStruct((m, n), out_dtype),
        scratch_shapes=[pltpu.VMEM((tm, k), BF16)],
        compiler_params=_cparams("parallel", "arbitrary"),
        name=name,
    )(x, g.reshape(1, k), w)


def _out_proj_kernel(h_ref, *refs):
    o_ref = refs[-1]
    n = (len(refs) - 1) // 2
    acc = h_ref[...]
    for y_ref, w_ref in zip(refs[:n], refs[n:2 * n]):
        acc = acc + jnp.dot(y_ref[...], w_ref[...].astype(BF16), preferred_element_type=F32)
    o_ref[...] = acc


def _out_proj(h, ys, w_all, layer, *, tm, tn):
    m, n = h.shape
    in_specs = [pl.BlockSpec((tm, tn), lambda i, j: (i, j))]
    in_specs += [pl.BlockSpec((tm, y.shape[1]), lambda i, j: (i, 0)) for y in ys]
    row = 0
    for y in ys:
        width = y.shape[1]
        assert row % width == 0
        in_specs.append(pl.BlockSpec((None, width, tn), lambda i, j, rb=row // width: (layer, rb, j)))
        row += width
    return pl.pallas_call(
        _out_proj_kernel,
        grid=(m // tm, n // tn),
        in_specs=in_specs,
        out_specs=pl.BlockSpec((tm, tn), lambda i, j: (i, j)),
        out_shape=jax.ShapeDtypeStruct((m, n), F32),
        compiler_params=_cparams("parallel", "arbitrary"),
        name="out_proj",
    )(h, *ys, *([w_all] * len(ys)))


def _final_norm_kernel(x_ref, g_ref, o_ref):
    x = x_ref[...]
    ms = jnp.mean(x * x, axis=-1, keepdims=True)
    o_ref[...] = x * lax.rsqrt(ms + RMS_EPS) * g_ref[...]


def _final_norm(h, g, *, tm):
    m, n = h.shape
    return pl.pallas_call(
        _final_norm_kernel,
        grid=(m // tm,),
        in_specs=[pl.BlockSpec((tm, n), lambda i: (i, 0)), pl.BlockSpec((1, n), lambda i: (0, 0))],
        out_specs=pl.BlockSpec((tm, n), lambda i: (i, 0)),
        out_shape=jax.ShapeDtypeStruct((m, n), F32),
        compiler_params=_cparams("parallel"),
        name="final_norm",
    )(h, g.reshape(1, n))


ONES_ROWS = 16


def _build_vt(v_ref, vt_ref, hh, dv, seq, chunk):
    def body(c, carry):
        st = pl.multiple_of(c * chunk, chunk)
        vt_ref[hh, 0:dv, pl.ds(st, chunk)] = v_ref[pl.ds(st, chunk), hh * dv:(hh + 1) * dv].astype(F32).T.astype(BF16)
        return carry

    lax.fori_loop(0, seq // chunk, body, 0)
    extra = vt_ref.shape[1] - dv
    if extra:
        vt_ref[hh, dv:dv + extra, :] = jnp.ones((extra, seq), BF16)


def _flash_kernel(*refs, scale, tq, tk, hp, dk, dv, has_decay):
    cw = min(tq, MXU_COLS)
    ncg = tq // cw
    streams = [(hh, cg) for hh in range(hp) for cg in range(ncg)]
    if has_decay:
        q_ref, k_ref, v_ref, g_ref, cq_ref, ck_ref, o_ref, vt_ref, qt_ref, s_ref, m_ref, acc_ref = refs
    else:
        q_ref, k_ref, v_ref, g_ref, o_ref, vt_ref, qt_ref, s_ref, m_ref, acc_ref = refs
    hb = pl.program_id(0)
    qi = pl.program_id(1)
    seq = k_ref.shape[0]

    @pl.when(qi == 0)
    def _():
        for hh in range(hp):
            _build_vt(v_ref, vt_ref, hh, dv, seq, tk)

    for v, (hh, cg) in enumerate(streams):
        qt_ref[v] = (q_ref[cg * cw:(cg + 1) * cw, hh * dk:(hh + 1) * dk].astype(F32) * (scale * LOG2E)).T.astype(BF16)
    m_ref[...] = jnp.full(m_ref.shape, NEG_INF, F32)
    acc_ref[...] = jnp.zeros(acc_ref.shape, F32)
    n_full = (qi * tq) // tk
    if has_decay:
        cq2 = [cq_ref[pl.ds(hb * hp + hh, 1), cg * cw:(cg + 1) * cw] for hh, cg in streams]

    def scores(j, slot):
        start = pl.multiple_of(j * tk, tk)
        for v, (hh, cg) in enumerate(streams):
            s_ref[slot, v] = jnp.dot(k_ref[pl.ds(start, tk), hh * dk:(hh + 1) * dk], qt_ref[v],
                                     preferred_element_type=F32)

    def softmax_pv(j, slot, masked):
        start = pl.multiple_of(j * tk, tk)
        for v, (hh, cg) in enumerate(streams):
            t = s_ref[slot, v]
            if has_decay:
                t = t - jnp.concatenate([ck_ref[hh, pl.ds(start, tk), :]] * (cw // LANES), axis=1)
            if masked:
                key = start + lax.broadcasted_iota(I32, (tk, cw), 0)
                qry = qi * tq + cg * cw + lax.broadcasted_iota(I32, (tk, cw), 1)
                t = jnp.where(key <= qry, t, NEG_INF)
            m_prev = m_ref[v]
            mx = jnp.max(t, axis=0, keepdims=True)
            if has_decay:
                m_new = jnp.maximum(m_prev, mx + cq2[v])
                shift = m_new - cq2[v]
            else:
                m_new = jnp.maximum(m_prev, mx)
                shift = m_new
            alpha = jnp.exp2(m_prev - m_new)
            p = jnp.exp2(t - shift).astype(BF16)
            acc_ref[v] = alpha * acc_ref[v] + jnp.dot(vt_ref[hh, :, pl.ds(start, tk)], p,
                                                       preferred_element_type=F32)
            m_ref[v] = m_new

    scores(0, 0)

    def pair(j):
        scores(j + 1, 1)
        softmax_pv(j, 0, False)
        scores(j + 2, 0)
        softmax_pv(j + 1, 1, False)

    def quad(qq, carry):
        pair(4 * qq)
        pair(4 * qq + 2)
        return carry

    quads = n_full // 4
    lax.fori_loop(0, quads, quad, 0)
    rem = n_full - 4 * quads

    @pl.when(rem >= 2)
    def _():
        pair(4 * quads)

    last = 4 * quads + 2 * (rem // 2)

    @pl.when(n_full % 2 == 1)
    def _():
        scores(last + 1, 1)
        softmax_pv(last, 0, False)
        softmax_pv(last + 1, 1, True)

    @pl.when(n_full % 2 == 0)
    def _():
        softmax_pv(last, 0, True)

    for v, (hh, cg) in enumerate(streams):
        a = acc_ref[v]
        o = (a[0:dv, :] / a[dv:dv + 1, :]).T
        rows = slice(cg * cw, (cg + 1) * cw)
        cols = slice(hh * dv, (hh + 1) * dv)
        o_ref[rows, cols] = (o * _silu(g_ref[rows, cols])).astype(o_ref.dtype)


def _flash(q_arr, q_off, k_arr, k_off, v_arr, v_off, g_arr, g_off, *, heads, dk, dv, scale, tq, tk, hp,
           decay=None, name):
    seq = q_arr.shape[0]
    cw = min(tq, MXU_COLS)
    ns = hp * (tq // cw)
    once = pl.Buffered(1)
    in_specs = [pl.BlockSpec((tq, hp * dk), lambda h, i: (i, q_off // (hp * dk) + h)),
                pl.BlockSpec((seq, hp * dk), lambda h, i: (0, k_off // (hp * dk) + h), pipeline_mode=once),
                pl.BlockSpec((seq, hp * dv), lambda h, i: (0, v_off // (hp * dv) + h), pipeline_mode=once),
                pl.BlockSpec((tq, hp * dv), lambda h, i: (i, g_off // (hp * dv) + h))]
    args = [q_arr, k_arr, v_arr, g_arr]
    if decay is not None:
        cum_t, cum_rep = decay
        in_specs += [pl.BlockSpec((SUBLANES, tq), lambda h, i: (0, i)),
                     pl.BlockSpec((hp, seq, LANES), lambda h, i: (h, 0, 0), pipeline_mode=once)]
        args += [cum_t, cum_rep]
    return pl.pallas_call(
        functools.partial(_flash_kernel, scale=scale, tq=tq, tk=tk, hp=hp, dk=dk, dv=dv, has_decay=decay is not None),
        grid=(heads // hp, seq // tq),
        in_specs=in_specs,
        out_specs=pl.BlockSpec((tq, hp * dv), lambda h, i: (i, h)),
        out_shape=jax.ShapeDtypeStruct((seq, heads * dv), BF16),
        scratch_shapes=[pltpu.VMEM((hp, dv + ONES_ROWS, seq), BF16), pltpu.VMEM((ns, dk, cw), BF16),
                        pltpu.VMEM((2, ns, tk, cw), F32), pltpu.VMEM((ns, 1, cw), F32),
                        pltpu.VMEM((ns, dv + ONES_ROWS, cw), F32)],
        compiler_params=_cparams("parallel", "arbitrary"),
        name=name,
    )(*args)


def _decay_kernel(f_ref, b_ref, ct_ref, cr_ref, carry_ref, *, t):
    i = pl.program_id(0)

    @pl.when(i == 0)
    def _():
        carry_ref[...] = jnp.zeros(carry_ref.shape, F32)

    x = f_ref[...] + b_ref[...]
    lf = jnp.minimum(x, 0.0) - jnp.log1p(jnp.exp(-jnp.abs(x)))
    row = lax.broadcasted_iota(I32, lf.shape, 0)
    s = 1
    while s < t:
        lf = lf + jnp.where(row >= s, pltpu.roll(lf, s, 0), 0.0)
        s *= 2
    lf = lf + carry_ref[...]
    carry_ref[...] = lf[t - 1:t, :]
    lf2 = lf * LOG2E
    ct_ref[...] = lf2.T[:FOX_HEADS, :]
    for h in range(FOX_HEADS):
        cr_ref[h] = jnp.broadcast_to(lf2[:, h:h + 1], (t, LANES))


def _decay(proj, f_cb, b_f, *, t):
    seq = proj.shape[0]
    b = jnp.pad(b_f.reshape(1, FOX_HEADS), ((0, 0), (0, LANES - FOX_HEADS)))
    return pl.pallas_call(
        functools.partial(_decay_kernel, t=t),
        grid=(seq // t,),
        in_specs=[pl.BlockSpec((t, LANES), lambda i: (i, f_cb)), pl.BlockSpec((1, LANES), lambda i: (0, 0))],
        out_specs=[pl.BlockSpec((FOX_HEADS, t), lambda i: (0, i)),
                   pl.BlockSpec((FOX_HEADS, t, LANES), lambda i: (0, i, 0))],
        out_shape=[jax.ShapeDtypeStruct((FOX_HEADS, seq), F32), jax.ShapeDtypeStruct((FOX_HEADS, seq, LANES), F32)],
        scratch_shapes=[pltpu.VMEM((1, LANES), F32)],
        compiler_params=_cparams("arbitrary"),
        name="fox_decay",
    )(proj, b)


def _mem_attn_kernel(q_ref, kv_ref, g_ref, o_ref):
    d = MEM_HEAD_DIM
    for h in range(MEM_HEADS):
        hs = slice(h * d, (h + 1) * d)
        k = kv_ref[:, hs].astype(BF16)
        v = kv_ref[:, MEM_WIDTH + h * d:MEM_WIDTH + (h + 1) * d].astype(BF16)
        s = lax.dot_general(q_ref[:, hs], k, (((1,), (1,)), ((), ())), preferred_element_type=F32) * (d ** -0.5)
        m = jnp.max(s, axis=1, keepdims=True)
        p = jnp.exp(s - m)
        l = jnp.sum(p, axis=1, keepdims=True)
        o = jnp.dot(p.astype(BF16), v, preferred_element_type=F32) / l
        o_ref[:, hs] = (o * _silu(g_ref[:, hs])).astype(o_ref.dtype)


def _mem_kv_all(mem2d, g, w_all, *, tn):
    depth, k, n = w_all.shape
    m = mem2d.shape[0]
    return pl.pallas_call(
        _norm_matmul_kernel,
        grid=(depth, n // tn),
        in_specs=[pl.BlockSpec((m, k), lambda l, j: (0, 0)),
                  pl.BlockSpec((1, k), lambda l, j: (0, 0)),
                  pl.BlockSpec((None, k, tn), lambda l, j: (l, 0, j))],
        out_specs=pl.BlockSpec((None, m, tn), lambda l, j: (l, 0, j)),
        out_shape=jax.ShapeDtypeStruct((depth, m, n), F32),
        scratch_shapes=[pltpu.VMEM((m, k), BF16)],
        compiler_params=_cparams("arbitrary", "arbitrary"),
        name="mem_kv",
    )(mem2d, g.reshape(1, k), w_all)


def _mem_attn(proj_b, q_cb, proj_f, g_cb, mem_kv, *, t):
    seq = proj_b.shape[0]
    mem_kv, layer = mem_kv
    nm = mem_kv.shape[1]
    w = MEM_WIDTH
    d = MEM_HEAD_DIM
    return pl.pallas_call(
        _mem_attn_kernel,
        grid=(seq // t,),
        in_specs=[pl.BlockSpec((t, w), lambda i: (i, q_cb * d // w)),
                  pl.BlockSpec((None, nm, 2 * w), lambda i: (layer, 0, 0)),
                  pl.BlockSpec((t, w), lambda i: (i, g_cb * d // w))],
        out_specs=pl.BlockSpec((t, w), lambda i: (i, 0)),
        out_shape=jax.ShapeDtypeStruct((seq, w), BF16),
        compiler_params=_cparams("parallel"),
        name="mem_attn",
    )(proj_b, mem_kv, proj_f)


S5_TILE_GROUPS = LANES // S5_GROUP
S5_TILE_STATES = S5_TILE_GROUPS * S5_STATE
S5_TILES = S5_GROUPS // S5_TILE_GROUPS


def _s5_scan_kernel(u_ref, b_ref, c_ref, d_ref, tab_ref, z_ref, bu_ref, carry_ref, *, tc):
    ns = S5_TILE_STATES

    @pl.when(pl.program_id(1) == 0)
    def _():
        carry_ref[...] = jnp.zeros(carry_ref.shape, F32)

    u = u_ref[...]
    bu_ref[...] = jnp.dot(u.astype(BF16), b_ref[...], preferred_element_type=F32)
    steps = [(1, tab_ref[0], tab_ref[1]), (2, tab_ref[2], tab_ref[3]), (4, tab_ref[4], tab_ref[5])]
    pr = tab_ref[6]
    pi = tab_ref[7]

    def body(i, carry):
        cr, ci = carry
        r0 = pl.multiple_of(i * SUBLANES, SUBLANES)
        xr = bu_ref[pl.ds(r0, SUBLANES), 0:ns]
        xi = bu_ref[pl.ds(r0, SUBLANES), ns:2 * ns]
        for s, ar, ai in steps:
            sr = pltpu.roll(xr, s, 0)
            si = pltpu.roll(xi, s, 0)
            xr, xi = xr + ar * sr - ai * si, xi + ar * si + ai * sr
        xr, xi = xr + pr * cr - pi * ci, xi + pr * ci + pi * cr
        bu_ref[pl.ds(r0, SUBLANES), 0:ns] = xr
        bu_ref[pl.ds(r0, SUBLANES), ns:2 * ns] = xi
        return xr[SUBLANES - 1:SUBLANES, :], xi[SUBLANES - 1:SUBLANES, :]

    cr, ci = lax.fori_loop(0, tc // SUBLANES, body, (carry_ref[0:1, 0:ns], carry_ref[0:1, ns:2 * ns]))
    carry_ref[0:1, 0:ns] = cr
    carry_ref[0:1, ns:2 * ns] = ci
    y = jnp.dot(bu_ref[...].astype(BF16), c_ref[...], preferred_element_type=F32) + d_ref[...] * u
    z_ref[...] = jax.nn.gelu(y)


def _s5_prepare(lam_re, lam_im, log_dt, b_re, b_im, c_re, c_im):
    dt = jnp.exp(log_dt.astype(F32))[:, None]
    lr = lam_re.astype(F32)
    li = lam_im.astype(F32)
    mag = jnp.exp(lr * dt)
    ab_re = mag * jnp.cos(li * dt)
    ab_im = mag * jnp.sin(li * dt)
    den = lr * lr + li * li
    nr = ab_re - 1.0
    f_re = (nr * lr + ab_im * li) / den
    f_im = (ab_im * lr - nr * li) / den
    br = b_re.astype(F32)
    bim = b_im.astype(F32)
    bb_re = f_re[..., None] * br - f_im[..., None] * bim
    bb_im = f_re[..., None] * bim + f_im[..., None] * br
    eye = jnp.eye(S5_TILE_GROUPS, dtype=F32)

    def blockdiag_in(bb):
        t = bb.reshape(S5_TILES, S5_TILE_GROUPS, S5_STATE, S5_GROUP)
        m = jnp.einsum("jgpc,gh->jgchp", t, eye)
        return m.reshape(S5_TILES, LANES, S5_TILE_STATES)

    def blockdiag_out(cc):
        t = cc.reshape(S5_TILES, S5_TILE_GROUPS, S5_GROUP, S5_STATE)
        m = jnp.einsum("jgcp,gh->jgphc", t, eye)
        return m.reshape(S5_TILES, S5_TILE_STATES, LANES)

    b_cat = jnp.concatenate([blockdiag_in(bb_re), blockdiag_in(bb_im)], axis=2).astype(BF16)
    c_cat = jnp.concatenate([blockdiag_out(c_re.astype(F32)), -blockdiag_out(c_im.astype(F32))], axis=1).astype(BF16)

    a_r = ab_re.reshape(S5_TILES, 1, S5_TILE_STATES)
    a_i = ab_im.reshape(S5_TILES, 1, S5_TILE_STATES)

    def cmul(xr, xi, yr, yi):
        return xr * yr - xi * yi, xr * yi + xi * yr

    a2 = cmul(a_r, a_i, a_r, a_i)
    a4 = cmul(*a2, *a2)
    row = jnp.arange(SUBLANES)[None, :, None]
    tabs = []
    for s, (pr_, pi_) in ((1, (a_r, a_i)), (2, a2), (4, a4)):
        tabs.append(jnp.where(row >= s, pr_, 0.0))
        tabs.append(jnp.where(row >= s, pi_, 0.0))
    pw = [(a_r, a_i)]
    for _ in range(SUBLANES - 1):
        pw.append(cmul(*pw[-1], a_r, a_i))
    tabs.append(jnp.concatenate([p[0] for p in pw], axis=1))
    tabs.append(jnp.concatenate([p[1] for p in pw], axis=1))
    tab = jnp.stack([jnp.broadcast_to(t, (S5_TILES, SUBLANES, S5_TILE_STATES)) for t in tabs], axis=1)
    return b_cat, c_cat, tab.astype(F32)


def _s5_scan(proj, u_cb, b_cat, c_cat, d_skip, tab, *, tc):
    seq = proj.shape[0]
    ns = S5_TILE_STATES
    d = d_skip.astype(F32).reshape(S5_TILES, 1, LANES)
    return pl.pallas_call(
        functools.partial(_s5_scan_kernel, tc=tc),
        grid=(S5_TILES, seq // tc),
        in_specs=[pl.BlockSpec((tc, LANES), lambda j, c: (c, u_cb + j)),
                  pl.BlockSpec((None, LANES, 2 * ns), lambda j, c: (j, 0, 0)),
                  pl.BlockSpec((None, 2 * ns, LANES), lambda j, c: (j, 0, 0)),
                  pl.BlockSpec((None, 1, LANES), lambda j, c: (j, 0, 0)),
                  pl.BlockSpec((None, 8, SUBLANES, ns), lambda j, c: (j, 0, 0, 0))],
        out_specs=pl.BlockSpec((tc, LANES), lambda j, c: (c, j)),
        out_shape=jax.ShapeDtypeStruct((seq, S5_WIDTH), F32),
        scratch_shapes=[pltpu.VMEM((tc, 2 * ns), F32), pltpu.VMEM((SUBLANES, 2 * ns), F32)],
        compiler_params=_cparams("parallel", "arbitrary"),
        name="s5_scan",
    )(proj, b_cat, c_cat, d, tab)


def _s5_glu_kernel(z_ref, w_ref, g_ref, o_ref, *, tn):
    j = pl.program_id(1)
    z = z_ref[...]
    a = jnp.dot(z.astype(BF16), w_ref[...].astype(BF16), preferred_element_type=F32)
    zc = z_ref[:, pl.ds(pl.multiple_of(j * tn, tn), tn)]
    o_ref[...] = (zc * jax.nn.sigmoid(a) * _silu(g_ref[...])).astype(o_ref.dtype)


def _s5_glu(z, w_glu, proj, g_cb, *, tm, tn):
    seq, n = z.shape
    return pl.pallas_call(
        functools.partial(_s5_glu_kernel, tn=tn),
        grid=(seq // tm, n // tn),
        in_specs=[pl.BlockSpec((tm, n), lambda i, j: (i, 0)),
                  pl.BlockSpec((n, tn), lambda i, j: (0, j)),
                  pl.BlockSpec((tm, tn), lambda i, j: (i, g_cb * (n // tn) + j))],
        out_specs=pl.BlockSpec((tm, tn), lambda i, j: (i, j)),
        out_shape=jax.ShapeDtypeStruct((seq, n), BF16),
        compiler_params=_cparams("parallel", "arbitrary"),
        name="s5_glu",
    )(z, w_glu, proj)


def _rope_tables(pos, freq):
    ang = pos * freq
    lane = lax.broadcasted_iota(I32, ang.shape, 1)
    half = MLA_ROPE // 2
    cos = jnp.cos(ang)
    sin = jnp.sin(ang)
    c = jnp.where(lane < MLA_ROPE, cos, 0.0)
    s1 = jnp.where(lane < half, -sin, 0.0)
    s2 = jnp.where((lane >= half) & (lane < MLA_ROPE), sin, 0.0)
    return c, s1, s2


def _rope_apply(x, c, s1, s2):
    half = MLA_ROPE // 2
    return x * c + pltpu.roll(x, LANES - half, 1) * s1 + pltpu.roll(x, half, 1) * s2


def _mla_up_kernel(cq_ref, ckv_ref, kr_ref, pos_ref, freq_ref, gq_ref, gkv_ref, wq_ref, wkv_ref, q_ref, k_ref, v_ref):
    def normed(x_ref, g_ref):
        x = x_ref[...]
        ms = jnp.mean(x * x, axis=-1, keepdims=True)
        return (x * lax.rsqrt(ms + RMS_EPS) * g_ref[...]).astype(BF16)

    qf = jnp.dot(normed(cq_ref, gq_ref), wq_ref[...], preferred_element_type=F32)
    kvf = jnp.dot(normed(ckv_ref, gkv_ref), wkv_ref[...], preferred_element_type=F32)
    c, s1, s2 = _rope_tables(pos_ref[...], freq_ref[...])
    kr = _rope_apply(kr_ref[...], c, s1, s2).astype(BF16)
    for h in range(MLA_HEADS):
        b = 2 * LANES * h
        q_ref[:, b:b + LANES] = qf[:, b:b + LANES].astype(BF16)
        q_ref[:, b + LANES:b + 2 * LANES] = _rope_apply(qf[:, b + LANES:b + 2 * LANES], c, s1, s2).astype(BF16)
        k_ref[:, b:b + LANES] = kvf[:, b:b + LANES].astype(BF16)
        k_ref[:, b + LANES:b + 2 * LANES] = kr
        v_ref[:, LANES * h:LANES * (h + 1)] = kvf[:, b + LANES:b + 2 * LANES].astype(BF16)


def _mla_up(proj, cq_cb, ckv_cb, kr_cb, g_cq, g_ckv, w_uq, w_ukv, pos_col, freq, *, t):
    seq = proj.shape[0]
    w = 2 * LANES * MLA_HEADS
    rq, rkv = w_uq.shape[0], w_ukv.shape[0]
    return pl.pallas_call(
        _mla_up_kernel,
        grid=(seq // t,),
        in_specs=[pl.BlockSpec((t, rq), lambda i: (i, cq_cb)),
                  pl.BlockSpec((t, rkv), lambda i: (i, ckv_cb)),
                  pl.BlockSpec((t, LANES), lambda i: (i, kr_cb)),
                  pl.BlockSpec((t, 1), lambda i: (i, 0)),
                  pl.BlockSpec((1, LANES), lambda i: (0, 0)),
                  pl.BlockSpec((1, rq), lambda i: (0, 0)),
                  pl.BlockSpec((1, rkv), lambda i: (0, 0)),
                  pl.BlockSpec((rq, w), lambda i: (0, 0)),
                  pl.BlockSpec((rkv, w), lambda i: (0, 0))],
        out_specs=[pl.BlockSpec((t, w), lambda i: (i, 0)),
                   pl.BlockSpec((t, w), lambda i: (i, 0)),
                   pl.BlockSpec((t, MLA_WIDTH), lambda i: (i, 0))],
        out_shape=[jax.ShapeDtypeStruct((seq, w), BF16), jax.ShapeDtypeStruct((seq, w), BF16),
                   jax.ShapeDtypeStruct((seq, MLA_WIDTH), BF16)],
        compiler_params=_cparams("parallel"),
        name="mla_up",
    )(proj, proj, proj, pos_col, freq, g_cq.reshape(1, rq), g_ckv.reshape(1, rkv), w_uq, w_ukv)


def _t5_bucket(dist):
    n = jnp.maximum(dist, 0)
    max_exact = T5_BUCKETS // 2
    log_ratio = jnp.log(jnp.maximum(n, 1).astype(F32) / max_exact) / math.log(T5_MAX_DIST / max_exact)
    large = jnp.minimum(max_exact + (log_ratio * (T5_BUCKETS - max_exact)).astype(I32), T5_BUCKETS - 1)
    return jnp.where(n < max_exact, n, large)


T5_MASK_BUCKET = T5_BUCKETS


def _t5_lookup(table_row, bucket):
    rows, width = bucket.shape
    tab = jnp.broadcast_to(table_row, (rows, LANES))
    parts = [jnp.take_along_axis(tab, bucket[:, c:c + LANES], axis=1, mode="promise_in_bounds")
             for c in range(0, width, LANES)]
    return parts[0] if len(parts) == 1 else jnp.concatenate(parts, axis=1)


def _nsa_cmp_kernel(x_ref, pe_ref, w1_ref, w2_ref, o_ref, xf_ref, *, nc):
    half = NSA_CMP_LEN // 2
    d = NSA_HEAD_DIM
    xf_ref[...] = x_ref[...].astype(F32)
    u = jnp.zeros((nc, NSA_CMP_HIDDEN), F32)
    v = jnp.zeros((nc, NSA_CMP_HIDDEN), F32)
    for r in range(half):
        a = xf_ref[pl.ds(r, nc, stride=NSA_CMP_STRIDE), :]
        u = u + jnp.dot((a + pe_ref[r:r + 1, :]).astype(BF16), w1_ref[r * d:(r + 1) * d, :].astype(BF16),
                        preferred_element_type=F32)
        v = v + jnp.dot((a + pe_ref[half + r:half + r + 1, :]).astype(BF16),
                        w1_ref[(half + r) * d:(half + r + 1) * d, :].astype(BF16), preferred_element_type=F32)
    hid = u + pltpu.roll(v, nc - 1, 0)
    o_ref[...] = jnp.dot(jax.nn.gelu(hid).astype(BF16), w2_ref[...].astype(BF16),
                         preferred_element_type=F32).astype(o_ref.dtype)


def _nsa_compress(proj, k_cb, pe, w1, w2):
    seq = proj.shape[0]
    nc = seq // NSA_CMP_STRIDE
    d = NSA_HEAD_DIM
    g = NSA_KV_GROUPS
    return pl.pallas_call(
        functools.partial(_nsa_cmp_kernel, nc=nc),
        grid=(2, g),
        in_specs=[pl.BlockSpec((seq, d), lambda a, b: (0, k_cb + a * g + b)),
                  pl.BlockSpec((None, NSA_CMP_LEN, d), lambda a, b: (a, 0, 0)),
                  pl.BlockSpec((None, NSA_CMP_LEN * d, NSA_CMP_HIDDEN), lambda a, b: (a, 0, 0)),
                  pl.BlockSpec((None, NSA_CMP_HIDDEN, d), lambda a, b: (a, 0, 0))],
        out_specs=pl.BlockSpec((None, None, nc, d), lambda a, b: (a, b, 0, 0)),
        out_shape=jax.ShapeDtypeStruct((2, g, nc, d), BF16),
        scratch_shapes=[pltpu.VMEM((seq, d), F32)],
        compiler_params=_cparams("parallel", "arbitrary"),
        name="nsa_compress",
    )(proj, pe, w1, w2)


def _nsa_select_kernel(consec_ref, q_ref, kc_ref, vc_ref, posq_ref, posc_ref, tab_ref, gate_ref, ov_ref,
                       oc_ref, sel_ref, strip_ref, *, nc, n_slc, tq):
    qi = pl.program_id(0)
    d = NSA_HEAD_DIM
    c = d ** -0.5 * LOG2E
    per_block = tq // NSA_CMP_STRIDE
    lane = lax.broadcasted_iota(I32, (1, tq), 1)
    tok = qi * tq + lane

    def select(bias):
        gates = jax.nn.sigmoid(gate_ref[...])
        ovt = ov_ref[...]
        js = lax.broadcasted_iota(I32, (LANES, tq), 0)
        jf = js.astype(F32)
        cur = tok // NSA_SLC_BLOCK
        forced = (js == 0) | (js == cur) | (js == cur - 1)
        for g in range(NSA_KV_GROUPS):
            kc = kc_ref[g]
            vct = vc_ref[g].astype(F32).T.astype(BF16)
            psum = jnp.zeros((nc, tq), F32)
            for r in range(NSA_REP):
                h = g * NSA_REP + r
                qt = (q_ref[:, h * d:(h + 1) * d].astype(F32) * c).T.astype(BF16)
                t = jnp.dot(kc, qt, preferred_element_type=F32) + bias(h)
                m = jnp.max(t, axis=0, keepdims=True)
                e = jnp.exp2(t - m)
                l = jnp.sum(e, axis=0, keepdims=True)
                p = e * jnp.where(m > 0.5 * NEG_INF, 1.0 / l, 0.0)
                o = jnp.dot(vct, p.astype(BF16), preferred_element_type=F32)
                oc_ref[:, h * d:(h + 1) * d] = gates[:, 3 * h:3 * h + 1] * o.T
                psum = psum + p
            p_hi = psum.astype(BF16)
            p_lo = (psum - p_hi.astype(F32)).astype(BF16)
            imp = jnp.dot(ovt, p_hi, preferred_element_type=F32) + jnp.dot(ovt, p_lo, preferred_element_type=F32)
            st = jnp.where(forced, FORCE_SCORE, jnp.where(js > cur, -1.0, imp))
            st = jnp.where(js < n_slc, st, -2.0)
            sel = jnp.zeros((LANES, tq), F32)
            for _ in range(NSA_SLC_TOPK):
                mx = jnp.max(st, axis=0, keepdims=True)
                first = jnp.min(jnp.where(st == mx, jf, float(LANES)), axis=0, keepdims=True)
                hit = jf == first
                sel = jnp.where(hit, 1.0, sel)
                st = jnp.where(hit, -3e38, st)
            sel_ref[g] = sel.astype(sel_ref.dtype)

    consecutive = consec_ref[0] == 1

    @pl.when(jnp.logical_and(consecutive, qi == 0))
    def _():
        for chunk in range(2 * nc // LANES):
            rel = chunk * LANES - nc + lax.broadcasted_iota(I32, (LANES, 1), 0)
            dist = lane - (rel * NSA_CMP_STRIDE + (NSA_CMP_LEN - 1))
            bucket = jnp.where(dist >= 0, _t5_bucket(dist), T5_MASK_BUCKET)
            for h in range(NSA_HEADS):
                strip_ref[h, chunk * LANES:(chunk + 1) * LANES, :] = _t5_lookup(tab_ref[h:h + 1, :], bucket)

    @pl.when(consecutive)
    def _():
        start = pl.multiple_of(nc - per_block * qi, per_block)
        select(lambda h: strip_ref[h, pl.ds(start, nc), :])

    @pl.when(jnp.logical_not(consecutive))
    def _():
        cmp_end = lax.broadcasted_iota(I32, (nc, 1), 0) * NSA_CMP_STRIDE + (NSA_CMP_LEN - 1)
        pos_c = jnp.concatenate([posc_ref[...]] * (tq // LANES), axis=1)
        bucket = jnp.where(cmp_end <= tok, _t5_bucket(posq_ref[...] - pos_c), T5_MASK_BUCKET)
        select(lambda h: _t5_lookup(tab_ref[h:h + 1, :], bucket))


def _nsa_select(proj_b, q_cb, proj_f, gate_cb, kc, vc, pos_row, pos_cmp_rep, consec, tab_t, ov_t, *, n_slc, tq):
    seq = proj_b.shape[0]
    nc = kc.shape[1]
    g = NSA_KV_GROUPS
    d = NSA_HEAD_DIM
    grid_spec = pltpu.PrefetchScalarGridSpec(
        num_scalar_prefetch=1,
        grid=(seq // tq,),
        in_specs=[pl.BlockSpec((tq, NSA_WIDTH), lambda i, *_: (i, q_cb)),
                  pl.BlockSpec((g, nc, d), lambda i, *_: (0, 0, 0)),
                  pl.BlockSpec((g, nc, d), lambda i, *_: (0, 0, 0)),
                  pl.BlockSpec((1, tq), lambda i, *_: (0, i)),
                  pl.BlockSpec((nc, LANES), lambda i, *_: (0, 0)),
                  pl.BlockSpec((SUBLANES, LANES), lambda i, *_: (0, 0)),
                  pl.BlockSpec((tq, LANES), lambda i, *_: (i, gate_cb)),
                  pl.BlockSpec((LANES, nc), lambda i, *_: (0, 0))],
        out_specs=[pl.BlockSpec((tq, NSA_WIDTH), lambda i, *_: (i, 0)),
                   pl.BlockSpec((g, LANES, tq), lambda i, *_: (0, 0, i))],
        scratch_shapes=[pltpu.VMEM((NSA_HEADS, 2 * nc, tq), F32)])
    return pl.pallas_call(
        functools.partial(_nsa_select_kernel, nc=nc, n_slc=n_slc, tq=tq),
        grid_spec=grid_spec,
        out_shape=[jax.ShapeDtypeStruct((seq, NSA_WIDTH), F32),
                   jax.ShapeDtypeStruct((g, LANES, seq), F32)],
        compiler_params=_cparams("arbitrary"),
        name="nsa_select",
    )(consec, proj_b, kc, vc, pos_row, pos_cmp_rep, tab_t * LOG2E, proj_f, ov_t)


def _nsa_slc_kernel(pqmin_ref, pkmax_ref, consec_ref, q_ref, k_ref, v_ref, sel_ref, posq_ref, posk_ref, tab_ref,
                    o_ref, vt_ref, qt_ref, s_ref, m_ref, l_ref, acc_ref, cache_ref, *, tq, tk):
    g = pl.program_id(0)
    qi = pl.program_id(1)
    d = NSA_HEAD_DIM
    scale = d ** -0.5
    seq = k_ref.shape[0]
    consecutive = consec_ref[0] == 1
    n_cached = cache_ref.shape[0]

    @pl.when(qi == 0)
    def _():
        _build_vt(v_ref, vt_ref, 0, d, seq, tk)

    @pl.when(jnp.logical_and(consecutive, qi == 0))
    def _():
        rel = lax.broadcasted_iota(I32, (1, tq), 1) - lax.broadcasted_iota(I32, (tk, 1), 0)
        for v in range(n_cached):
            dist = rel + v * tq
            bucket = jnp.where(dist >= 0, _t5_bucket(dist), T5_MASK_BUCKET)
            for r in range(NSA_REP):
                cache_ref[v, r] = _t5_lookup(tab_ref[pl.ds(g * NSA_REP + r, 1), :], bucket)

    for r in range(NSA_REP):
        qt_ref[r] = (q_ref[:, r * d:(r + 1) * d].astype(F32) * (scale * LOG2E)).T.astype(BF16)
    m_ref[...] = jnp.full(m_ref.shape, NEG_INF, F32)
    l_ref[...] = jnp.zeros(l_ref.shape, F32)
    acc_ref[...] = jnp.zeros(acc_ref.shape, F32)
    pos_q = posq_ref[...]
    per_tile = tk // NSA_SLC_BLOCK
    n_full = (qi * tq) // tk

    def scores(j, slot):
        start = pl.multiple_of(j * tk, tk)
        k = k_ref[pl.ds(start, tk), :]
        for r in range(NSA_REP):
            s_ref[slot, r] = jnp.dot(k, qt_ref[r], preferred_element_type=F32)

    def softmax_pv(j, slot, masked, mode):
        start = pl.multiple_of(j * tk, tk)
        picked = jnp.concatenate(
            [jnp.broadcast_to(sel_ref[pl.ds(j * per_tile + b, 1), :], (NSA_SLC_BLOCK, tq)) for b in range(per_tile)],
            axis=0)
        if masked and mode == "gather":
            key = start + lax.broadcasted_iota(I32, (tk, tq), 0)
            qry = qi * tq + lax.broadcasted_iota(I32, (tk, tq), 1)
            picked = jnp.where(key <= qry, picked, 0.0)
        mask = picked > 0.5
        if mode == "gather":
            pos_k = jnp.concatenate([posk_ref[pl.ds(start, tk), :]] * (tq // LANES), axis=1)
            bucket = jnp.where(mask, _t5_bucket(pos_q - pos_k), T5_MASK_BUCKET)
        for r in range(NSA_REP):
            h = g * NSA_REP + r
            t = s_ref[slot, r]
            if mode == "gather":
                t = t + _t5_lookup(tab_ref[pl.ds(h, 1), :], bucket)
            elif mode == "cached":
                t = jnp.where(mask, t + cache_ref[(qi * tq - start) // tq, r], NEG_INF)
            else:
                t = jnp.where(mask, t, NEG_INF)
            m_prev = m_ref[r]
            mx = jnp.max(t, axis=0, keepdims=True)
            if mode == "far":
                b = tab_ref[pl.ds(h, 1), T5_BUCKETS - 1:T5_BUCKETS]
                m_new = jnp.maximum(m_prev, mx + b)
                shift = m_new - b
            else:
                m_new = jnp.maximum(m_prev, mx)
                shift = m_new
            alpha = jnp.exp2(m_prev - m_new)
            p = jnp.exp2(t - shift)
            l_ref[r] = alpha * l_ref[r] + jnp.sum(p, axis=0, keepdims=True)
            acc_ref[r] = alpha * acc_ref[r] + jnp.dot(vt_ref[0, :, pl.ds(start, tk)], p.astype(BF16),
                                                       preferred_element_type=F32)
            m_ref[r] = m_new

    def is_far(j):
        return pqmin_ref[qi] - pkmax_ref[j] >= T5_MAX_DIST

    def stages(j, count, mode):
        for k in range(count):
            scores(j + k + 1, (k + 1) % 2)
            softmax_pv(j + k, k % 2, False, mode)

    def near(fn):
        @pl.when(consecutive)
        def _():
            fn("cached")

        @pl.when(jnp.logical_not(consecutive))
        def _():
            fn("gather")

    def far_or_near(far, fn):
        @pl.when(far)
        def _():
            fn("far")

        @pl.when(jnp.logical_not(far))
        def _():
            near(fn)

    scores(0, 0)

    def pair(jj, carry):
        j = 2 * jj
        far_or_near(jnp.logical_and(is_far(j), is_far(j + 1)), lambda mode: stages(j, 2, mode))
        return carry

    pairs = n_full // 2
    lax.fori_loop(0, pairs, pair, 0)
    last = 2 * pairs

    @pl.when(n_full % 2 == 1)
    def _():
        far_or_near(is_far(last), lambda mode: stages(last, 1, mode))
        near(lambda mode: softmax_pv(last + 1, 1, True, mode))

    @pl.when(n_full % 2 == 0)
    def _():
        near(lambda mode: softmax_pv(last, 0, True, mode))

    for r in range(NSA_REP):
        o_ref[:, r * d:(r + 1) * d] = (acc_ref[r] / l_ref[r]).T


def _nsa_slc(proj_b, q_off, k_off, v_off, sel, pos_row, pos_rep, tab_t, pq_min, pk_max, consec, *, tq, tk):
    seq = proj_b.shape[0]
    tab_t = tab_t * LOG2E
    d = NSA_HEAD_DIM
    gw = NSA_REP * d
    once = pl.Buffered(1)
    n_cached = (T5_MAX_DIST + tk - 2) // tq + 1 + tk // tq
    grid_spec = pltpu.PrefetchScalarGridSpec(
        num_scalar_prefetch=3,
        grid=(NSA_KV_GROUPS, seq // tq),
        in_specs=[pl.BlockSpec((tq, gw), lambda g, i, *_: (i, q_off // gw + g)),
                  pl.BlockSpec((seq, d), lambda g, i, *_: (0, k_off // d + g), pipeline_mode=once),
                  pl.BlockSpec((seq, d), lambda g, i, *_: (0, v_off // d + g), pipeline_mode=once),
                  pl.BlockSpec((None, LANES, tq), lambda g, i, *_: (g, 0, i)),
                  pl.BlockSpec((1, tq), lambda g, i, *_: (0, i)),
                  pl.BlockSpec((seq, LANES), lambda g, i, *_: (0, 0), pipeline_mode=once),
                  pl.BlockSpec((SUBLANES, LANES), lambda g, i, *_: (0, 0))],
        out_specs=pl.BlockSpec((tq, gw), lambda g, i, *_: (i, g)),
        scratch_shapes=[pltpu.VMEM((1, d, seq), BF16), pltpu.VMEM((NSA_REP, d, tq), BF16),
                        pltpu.VMEM((2, NSA_REP, tk, tq), F32), pltpu.VMEM((NSA_REP, 1, tq), F32),
                        pltpu.VMEM((NSA_REP, 1, tq), F32), pltpu.VMEM((NSA_REP, d, tq), F32),
                        pltpu.VMEM((n_cached, NSA_REP, tk, tq), F32)])
    return pl.pallas_call(
        functools.partial(_nsa_slc_kernel, tq=tq, tk=tk),
        grid_spec=grid_spec,
        out_shape=jax.ShapeDtypeStruct((seq, NSA_WIDTH), F32),
        compiler_params=_cparams("parallel", "arbitrary"),
        name="nsa_slc",
    )(pq_min, pk_max, consec, proj_b, proj_b, proj_b, sel, pos_row, pos_rep, tab_t)


def _nsa_win_kernel(*refs, tq, nt):
    consec_ref = refs[0]
    q_ref = refs[1]
    k_refs = refs[2:2 + nt]
    v_refs = refs[2 + nt:2 + 2 * nt]
    pk_refs = refs[2 + 2 * nt:2 + 3 * nt]
    posq_ref, tab_ref, gate_ref, gout_ref, oc_ref, os_ref, o_ref, bias_ref = refs[2 + 3 * nt:]
    qi = pl.program_id(0)
    d = NSA_HEAD_DIM
    c = d ** -0.5 * LOG2E
    sub = lax.broadcasted_iota(I32, (tq, 1), 0)
    lane = lax.broadcasted_iota(I32, (1, tq), 1)

    def band_bucket(kidx, tok, dist):
        lower = jnp.maximum(tok - (NSA_WINDOW - 1), 0)
        b = jnp.where(kidx >= lower, _t5_bucket(dist), T5_MASK_BUCKET)
        return jnp.where(kidx <= tok, b, T5_MASK_BUCKET)

    def attend(bias):
        gates = jax.nn.sigmoid(gate_ref[...])
        for g in range(NSA_KV_GROUPS):
            ks = [kr[:, g * d:(g + 1) * d] for kr in k_refs]
            vts = [vr[:, g * d:(g + 1) * d].astype(F32).T.astype(BF16) for vr in v_refs]
            for r in range(NSA_REP):
                h = g * NSA_REP + r
                hs = slice(h * d, (h + 1) * d)
                qt = (q_ref[:, hs].astype(F32) * c).T.astype(BF16)
                ts = [jnp.dot(ks[jj], qt, preferred_element_type=F32) + bias(h, jj) for jj in range(nt)]
                m = functools.reduce(jnp.maximum, [jnp.max(t, axis=0, keepdims=True) for t in ts])
                ps = [jnp.exp2(t - m) for t in ts]
                l = functools.reduce(jnp.add, [jnp.sum(p, axis=0, keepdims=True) for p in ps])
                o_t = functools.reduce(jnp.add, [jnp.dot(vt, p.astype(BF16), preferred_element_type=F32)
                                                 for vt, p in zip(vts, ps)])
                o_w = (o_t / l).T
                o = (oc_ref[:, hs] + gates[:, 3 * h + 1:3 * h + 2] * os_ref[:, hs]
                     + gates[:, 3 * h + 2:3 * h + 3] * o_w)
                o_ref[:, hs] = (o * _silu(gout_ref[:, hs])).astype(o_ref.dtype)

    consecutive = consec_ref[0] == 1

    @pl.when(jnp.logical_and(consecutive, qi == 0))
    def _():
        tok0 = (nt - 1) * tq + lane
        for jj in range(nt):
            kidx0 = jj * tq + sub
            bucket = band_bucket(kidx0, tok0, tok0 - kidx0)
            for h in range(NSA_HEADS):
                bias_ref[h, jj] = _t5_lookup(tab_ref[h:h + 1, :], bucket)

    cached = jnp.logical_and(consecutive, qi >= nt - 1)

    @pl.when(cached)
    def _():
        attend(lambda h, jj: bias_ref[h, jj])

    @pl.when(jnp.logical_not(cached))
    def _():
        tok = qi * tq + lane
        pos_q = posq_ref[...]
        buckets = []
        for jj in range(nt):
            kidx = (qi - (nt - 1) + jj) * tq + sub
            pos_k = jnp.concatenate([pk_refs[jj][...]] * (tq // LANES), axis=1)
            buckets.append(band_bucket(kidx, tok, pos_q - pos_k))
        attend(lambda h, jj: _t5_lookup(tab_ref[h:h + 1, :], buckets[jj]))


def _nsa_win(proj_b, q_cb, k_cb, v_cb, proj_f, gate_cb, gout_cb, oc, o_s, pos_row, pos_rep, consec, tab_t, *, tq):
    seq = proj_b.shape[0]
    nt = NSA_WINDOW // tq + 1

    def band_rows(jj, cb):
        return pl.BlockSpec((tq, NSA_KV), lambda i, *_: (jnp.maximum(i - (nt - 1) + jj, 0), cb))

    def band_pos(jj):
        return pl.BlockSpec((tq, LANES), lambda i, *_: (jnp.maximum(i - (nt - 1) + jj, 0), 0))

    in_specs = [pl.BlockSpec((tq, NSA_WIDTH), lambda i, *_: (i, q_cb))]
    in_specs += [band_rows(jj, k_cb) for jj in range(nt)]
    in_specs += [band_rows(jj, v_cb) for jj in range(nt)]
    in_specs += [band_pos(jj) for jj in range(nt)]
    in_specs += [pl.BlockSpec((1, tq), lambda i, *_: (0, i)),
                 pl.BlockSpec((SUBLANES, LANES), lambda i, *_: (0, 0)),
                 pl.BlockSpec((tq, LANES), lambda i, *_: (i, gate_cb)),
                 pl.BlockSpec((tq, NSA_WIDTH), lambda i, *_: (i, gout_cb)),
                 pl.BlockSpec((tq, NSA_WIDTH), lambda i, *_: (i, 0)),
                 pl.BlockSpec((tq, NSA_WIDTH), lambda i, *_: (i, 0))]
    args = [proj_b] * (1 + 2 * nt) + [pos_rep] * nt + [pos_row, tab_t * LOG2E, proj_f, proj_f, oc, o_s]
    grid_spec = pltpu.PrefetchScalarGridSpec(
        num_scalar_prefetch=1,
        grid=(seq // tq,),
        in_specs=in_specs,
        out_specs=pl.BlockSpec((tq, NSA_WIDTH), lambda i, *_: (i, 0)),
        scratch_shapes=[pltpu.VMEM((NSA_HEADS, nt, tq, tq), F32)])
    return pl.pallas_call(
        functools.partial(_nsa_win_kernel, tq=tq, nt=nt),
        grid_spec=grid_spec,
        out_shape=jax.ShapeDtypeStruct((seq, NSA_WIDTH), BF16),
        compiler_params=_cparams("arbitrary"),
        name="nsa_win",
    )(consec, *args)


def _in_proj(h, norm_g, w_b, w_f, tiles, name):
    tn_b = w_b.shape[1] // tiles["col_tiles_b"]
    tn_f = w_f.shape[1] // tiles["col_tiles_f"]
    assert tn_b % MXU_COLS == 0 and tn_f % MXU_COLS == 0
    proj_b = _norm_matmul(h, norm_g, w_b, tm=tiles["tm"], tn=tn_b, name=name + "_b", out_dtype=BF16)
    proj_f = _norm_matmul(h, norm_g, w_f, tm=tiles["tm"], tn=tn_f, name=name + "_f")
    return proj_b, proj_f


def _even_layer(h, mem_kv, norm_g, w_in, s5, w_glu, b_f, tiles):
    pb, pf = _in_proj(h, norm_g, *w_in, tiles, "in_proj_even")
    ob, of = EVEN_B_OFF, EVEN_F_OFF
    b_cat, c_cat, tab, d_skip = s5
    z = _s5_scan(pf, of["u"] // LANES, b_cat, c_cat, d_skip, tab, tc=tiles["s5_tc"])
    y_s5 = _s5_glu(z, w_glu, pf, of["g_s5"] // S5_WIDTH, tm=tiles["tm"], tn=tiles["tn_glu"])
    decay = _decay(pf, of["f"] // LANES, b_f, t=tiles["decay_t"])
    d = FOX_HEAD_DIM
    y_fox = _flash(pb, ob["q"], pb, ob["k"], pb, ob["v"], pf, of["g_fox"], heads=FOX_HEADS, dk=d, dv=d,
                   scale=d ** -0.5, tq=tiles["attn_tq"], tk=tiles["attn_tk"], hp=tiles["attn_hp"], decay=decay,
                   name="fox_attn")
    y_mem = _mem_attn(pb, ob["q_mem"] // MEM_HEAD_DIM, pf, of["g_mem"] // MEM_HEAD_DIM, mem_kv, t=tiles["mem_t"])
    return y_s5, y_fox, y_mem


def _odd_layer(h, mem_kv, norm_g, w_in, mla, nsa, pos, tiles):
    pb, pf = _in_proj(h, norm_g, *w_in, tiles, "in_proj_odd")
    ob, of = ODD_B_OFF, ODD_F_OFF
    g_cq, g_ckv, w_uq, w_ukv, freq = mla
    pos_col_f, pos_row, pos_cmp_rep, pos_rep, pq_min, pk_max, consec = pos
    q_r, k_r, v_r = _mla_up(pf, of["c_q"] // MLA_Q_RANK, of["c_kv"] // MLA_KV_RANK, of["k_rope"] // LANES,
                            g_cq, g_ckv, w_uq, w_ukv, pos_col_f, freq, t=tiles["prep_t"])
    y_mla = _flash(q_r, 0, k_r, 0, v_r, 0, pf, of["g_mla"], heads=MLA_HEADS, dk=2 * LANES, dv=MLA_V,
                   scale=(MLA_NOPE + MLA_ROPE) ** -0.5, tq=tiles["attn_tq"], tk=tiles["attn_tk"], hp=tiles["attn_hp"],
                   name="mla_attn")
    pe, w1, w2, tab_t, ov, n_slc = nsa
    kvc = _nsa_compress(pb, ob["k_cmp"] // NSA_HEAD_DIM, pe, w1, w2)
    oc, sel = _nsa_select(pb, ob["q_nsa"] // NSA_WIDTH, pf, of["gates"] // LANES, kvc[0], kvc[1], pos_row,
                          pos_cmp_rep, consec, tab_t, ov, n_slc=n_slc, tq=tiles["nsa_tq"])
    o_s = _nsa_slc(pb, ob["q_nsa"], ob["k_slc"], ob["v_slc"], sel, pos_row, pos_rep, tab_t, pq_min, pk_max, consec,
                   tq=tiles["slc_tq"], tk=tiles["slc_tk"])
    y_nsa = _nsa_win(pb, ob["q_nsa"] // NSA_WIDTH, ob["k_win"] // NSA_KV, ob["v_win"] // NSA_KV,
                     pf, of["gates"] // LANES, of["g_nsa"] // NSA_WIDTH, oc, o_s, pos_row, pos_rep, consec, tab_t,
                     tq=tiles["win_tq"])
    y_mem = _mem_attn(pb, ob["q_mem"] // MEM_HEAD_DIM, pf, of["g_mem"] // MEM_HEAD_DIM, mem_kv, t=tiles["mem_t"])
    return y_mla, y_nsa, y_mem


def _tiles(seq):
    return {"tm": min(seq, 1024), "col_tiles_b": 2, "col_tiles_f": 3, "tn_out": 1024, "tn_glu": 1024, "tn_mem": 512,
            "s5_tc": min(seq, 2048), "decay_t": min(seq, 512), "attn_tq": min(seq, 512),
            "attn_tk": min(seq, 512), "attn_hp": 2, "slc_tq": 256, "slc_tk": 512, "nsa_tq": 256, "win_tq": 128,
            "mem_t": min(seq, 1024), "norm_t": min(seq, 1024), "prep_t": min(seq, 512)}


def _context(positions, t5_table, seq, tiles):
    pos = positions[0]
    pos_col = pos.reshape(seq, 1)
    pos_row = pos.reshape(1, seq)
    pos_rep = jnp.broadcast_to(pos_col, (seq, LANES))
    pq_min = jnp.min(pos.reshape(seq // tiles["slc_tq"], tiles["slc_tq"]), axis=1)
    pk_max = jnp.max(pos.reshape(seq // tiles["slc_tk"], tiles["slc_tk"]), axis=1)
    nc = seq // NSA_CMP_STRIDE
    pos_cmp = jnp.pad(pos[NSA_CMP_LEN - 1::NSA_CMP_STRIDE], (0, 1))
    pos_cmp_rep = jnp.broadcast_to(pos_cmp.reshape(nc, 1), (nc, LANES))
    half = MLA_ROPE // 2
    inv_freq = ROPE_THETA ** (-jnp.arange(half, dtype=F32) / half)
    freq = jnp.concatenate([inv_freq, inv_freq, jnp.zeros((LANES - MLA_ROPE,), F32)]).reshape(1, LANES)
    tab_t = jnp.pad(t5_table.astype(F32).T, ((0, SUBLANES - NSA_HEADS), (0, LANES - T5_BUCKETS)))
    tab_t = tab_t.at[:, T5_MASK_BUCKET].set(NEG_INF)
    n_slc = seq // NSA_SLC_BLOCK
    cs = np.arange(nc) * NSA_CMP_STRIDE
    ss = np.arange(LANES) * NSA_SLC_BLOCK
    ov_np = np.clip(np.minimum(cs[:, None] + NSA_CMP_LEN, ss[None, :] + NSA_SLC_BLOCK)
                    - np.maximum(cs[:, None], ss[None, :]), 0, None) / NSA_CMP_LEN
    ov_np[nc - 1, :] = 0.0
    ov_np[:, n_slc:] = 0.0
    consec = jnp.all(pos[1:] - pos[:-1] == 1).astype(I32).reshape(1)
    return {"pos": (pos_col.astype(F32), pos_row, pos_cmp_rep, pos_rep, pq_min, pk_max, consec), "freq": freq,
            "tab_t": tab_t, "ov": jnp.asarray(ov_np.T, BF16), "n_slc": n_slc}


def _odd_params(i, mla_g_cq, mla_g_ckv, mla_w_uq, mla_w_ukv, nsa_cmp_pe, nsa_cmp_w1, nsa_cmp_w2, ctx):
    dq = MLA_NOPE + MLA_ROPE
    w_uq = mla_w_uq[i].reshape(MLA_Q_RANK, MLA_HEADS, dq)
    w_uq = jnp.pad(w_uq, ((0, 0), (0, 0), (0, 2 * LANES - dq))).reshape(MLA_Q_RANK, -1).astype(BF16)
    mla = (mla_g_cq[i], mla_g_ckv[i], w_uq, mla_w_ukv[i].astype(BF16), ctx["freq"])
    nsa = (nsa_cmp_pe[i].astype(F32), nsa_cmp_w1[i], nsa_cmp_w2[i], ctx["tab_t"],
           ctx["ov"], ctx["n_slc"])
    return mla, nsa


def kernel(x, mem, positions, norm_g, mem_norm_g, final_norm_g, t5_table, w_out, mem_w_kv, even_w_in, s5_lam_re,
           s5_lam_im, s5_log_dt, s5_b_re, s5_b_im, s5_c_re, s5_c_im, s5_d, s5_w_glu, fox_b_f, odd_w_in, mla_g_cq,
           mla_g_ckv, mla_w_uq, mla_w_ukv, nsa_cmp_pe, nsa_cmp_w1, nsa_cmp_w2):
    batch, seq, _ = x.shape
    assert batch == 1 and seq % 1024 == 0 and seq // NSA_SLC_BLOCK <= LANES
    depth = norm_g.shape[0]
    tiles = _tiles(seq)
    ctx = _context(positions, t5_table, seq, tiles)
    h = x[0]
    mem_kv_all = _mem_kv_all(mem[0], mem_norm_g, mem_w_kv, tn=tiles["tn_mem"])
    s5_b_cat, s5_c_cat, s5_tab = jax.vmap(_s5_prepare)(s5_lam_re, s5_lam_im, s5_log_dt, s5_b_re, s5_b_im,
                                                       s5_c_re, s5_c_im)
    for layer in range(depth):
        i = layer // 2
        mem_kv = (mem_kv_all, layer)
        if layer % 2 == 0:
            w_in = (_reorder_w_in(even_w_in[i], EVEN_SPLITS, EVEN_B_ORDER),
                    _reorder_w_in(even_w_in[i], EVEN_SPLITS, EVEN_F_ORDER))
            ys = _even_layer(h, mem_kv, norm_g[layer], w_in, (s5_b_cat[i], s5_c_cat[i], s5_tab[i], s5_d[i]),
                             s5_w_glu[i], fox_b_f[i], tiles)
        else:
            w_in = (_reorder_w_in(odd_w_in[i], ODD_SPLITS, ODD_B_ORDER),
                    _reorder_w_in(odd_w_in[i], ODD_SPLITS, ODD_F_ORDER))
            mla, nsa = _odd_params(i, mla_g_cq, mla_g_ckv, mla_w_uq, mla_w_ukv, nsa_cmp_pe, nsa_cmp_w1, nsa_cmp_w2,
                                   ctx)
            ys = _odd_layer(h, mem_kv, norm_g[layer], w_in, mla, nsa, ctx["pos"], tiles)
        h = _out_proj(h, ys, w_out, layer, tm=tiles["tm"], tn=tiles["tn_out"])
    return _final_norm(h, final_norm_g, tm=tiles["norm_t"])[None]
```

```python
import functools
import math

import numpy as np
import jax
import jax.numpy as jnp
from jax import lax
from jax.experimental import pallas as pl
from jax.experimental.pallas import tpu as pltpu

F32 = jnp.float32
BF16 = jnp.bfloat16
I32 = jnp.int32

RMS_EPS = 1e-6
NEG_INF = -1e30
LOG2E = math.log2(math.e)

S5_WIDTH = 1024
S5_GROUP = 16
S5_GROUPS = S5_WIDTH // S5_GROUP
S5_STATE = 64
FOX_HEADS = 8
FOX_HEAD_DIM = 128
FOX_WIDTH = FOX_HEADS * FOX_HEAD_DIM
MEM_HEADS = 4
MEM_HEAD_DIM = 128
MEM_WIDTH = MEM_HEADS * MEM_HEAD_DIM
MLA_HEADS = 8
MLA_Q_RANK = 512
MLA_KV_RANK = 512
MLA_NOPE = 128
MLA_ROPE = 64
MLA_V = 128
MLA_WIDTH = MLA_HEADS * MLA_V
ROPE_THETA = 10000.0
NSA_HEADS = 8
NSA_KV_GROUPS = 2
NSA_REP = NSA_HEADS // NSA_KV_GROUPS
NSA_HEAD_DIM = 128
NSA_WIDTH = NSA_HEADS * NSA_HEAD_DIM
NSA_KV = NSA_KV_GROUPS * NSA_HEAD_DIM
NSA_CMP_LEN = 32
NSA_CMP_STRIDE = 16
NSA_CMP_HIDDEN = 256
NSA_SLC_BLOCK = 64
NSA_SLC_TOPK = 16
NSA_WINDOW = 512
FORCE_SCORE = 1e6
T5_BUCKETS = 32
T5_MAX_DIST = 1024

EVEN_SPLITS = (S5_WIDTH, S5_WIDTH, FOX_WIDTH, FOX_WIDTH, FOX_WIDTH, FOX_HEADS, FOX_WIDTH, MEM_WIDTH, MEM_WIDTH)
ODD_SPLITS = (MLA_Q_RANK, MLA_KV_RANK, MLA_ROPE, MLA_WIDTH, NSA_WIDTH, NSA_KV, NSA_KV, NSA_KV, NSA_KV, NSA_KV,
              NSA_KV, 3 * NSA_HEADS, NSA_WIDTH, MEM_WIDTH, MEM_WIDTH)

LANES = 128
SUBLANES = 8
MXU_COLS = 256
VMEM_LIMIT_BYTES = 56 * 1024 * 1024

EVEN_B_ORDER = (("q", 2, 1024), ("k", 3, 1024), ("v", 4, 1024), ("q_mem", 7, 512))
EVEN_F_ORDER = (("u", 0, 1024), ("g_s5", 1, 1024), ("g_fox", 6, 1024), ("g_mem", 8, 512), ("f", 5, 128),
                ("pad", None, 128))
ODD_B_ORDER = (("q_nsa", 4, 1024), ("q_mem", 13, 512), ("k_cmp", 5, 256), ("v_cmp", 6, 256), ("k_slc", 7, 256),
               ("v_slc", 8, 256), ("k_win", 9, 256), ("v_win", 10, 256))
ODD_F_ORDER = (("g_nsa", 12, 1024), ("g_mla", 3, 1024), ("c_q", 0, 512), ("c_kv", 1, 512), ("g_mem", 14, 512),
               ("k_rope", 2, 128), ("gates", 11, 128))


def _layout(order):
    off, out = 0, {}
    for name, _, width in order:
        assert off % width == 0
        out[name] = off
        off += width
    return out, off


EVEN_B_OFF, EVEN_B_N = _layout(EVEN_B_ORDER)
EVEN_F_OFF, EVEN_F_N = _layout(EVEN_F_ORDER)
ODD_B_OFF, ODD_B_N = _layout(ODD_B_ORDER)
ODD_F_OFF, ODD_F_N = _layout(ODD_F_ORDER)


def _reorder_w_in(w, splits, order):
    starts = np.concatenate([[0], np.cumsum(splits)])
    pieces = []
    for _, idx, width in order:
        lo, hi = (0, 0) if idx is None else (int(starts[idx]), int(starts[idx + 1]))
        if pieces and pieces[-1][2] == 0 and pieces[-1][1] == lo:
            pieces[-1][1] = hi
            pieces[-1][2] = width - (hi - lo)
        else:
            pieces.append([lo, hi, width - (hi - lo)])
    cols = []
    for lo, hi, pad in pieces:
        if hi > lo:
            cols.append(w[:, lo:hi])
        if pad:
            cols.append(jnp.zeros((w.shape[0], pad), w.dtype))
    return jnp.concatenate(cols, axis=1).astype(BF16)


def _cparams(*sem):
    return pltpu.CompilerParams(dimension_semantics=sem, vmem_limit_bytes=VMEM_LIMIT_BYTES)


def _silu(g):
    return g * jax.nn.sigmoid(g)


def _norm_matmul_kernel(x_ref, g_ref, w_ref, o_ref, xn_ref):
    @pl.when(pl.program_id(1) == 0)
    def _():
        x = x_ref[...]
        ms = jnp.mean(x * x, axis=-1, keepdims=True)
        xn_ref[...] = (x * lax.rsqrt(ms + RMS_EPS) * g_ref[...]).astype(BF16)

    o_ref[...] = jnp.dot(xn_ref[...], w_ref[...].astype(BF16), preferred_element_type=F32).astype(o_ref.dtype)


def _norm_matmul(x, g, w, *, x_cb=0, tm, tn, name, out_dtype=F32):
    m = x.shape[0]
    k, n = w.shape
    return pl.pallas_call(
        _norm_matmul_kernel,
        grid=(m // tm, n // tn),
        in_specs=[pl.BlockSpec((tm, k), lambda i, j: (i, x_cb)),
                  pl.BlockSpec((1, k), lambda i, j: (0, 0)),
                  pl.BlockSpec((k, tn), lambda i, j: (0, j))],
        out_specs=pl.BlockSpec((tm, tn), lambda i, j: (i, j)),
        out_shape=jax.ShapeDtypeStruct((m, n), out_dtype),
        scratch_shapes=[pltpu.VMEM((tm, k), BF16)],
        compiler_params=_cparams("parallel", "arbitrary"),
        name=name,
    )(x, g.reshape(1, k), w)


def _out_proj_kernel(h_ref, *refs):
    o_ref = refs[-1]
    n = (len(refs) - 1) // 2
    acc = h_ref[...]
    for y_ref, w_ref in zip(refs[:n], refs[n:2 * n]):
        acc = acc + jnp.dot(y_ref[...], w_ref[...].astype(BF16), preferred_element_type=F32)
    o_ref[...] = acc


def _out_proj(h, ys, w_all, layer, *, tm, tn):
    m, n = h.shape
    in_specs = [pl.BlockSpec((tm, tn), lambda i, j: (i, j))]
    in_specs += [pl.BlockSpec((tm, y.shape[1]), lambda i, j: (i, 0)) for y in ys]
    row = 0
    for y in ys:
        width = y.shape[1]
        assert row % width == 0
        in_specs.append(pl.BlockSpec((None, width, tn), lambda i, j, rb=row // width: (layer, rb, j)))
        row += width
    return pl.pallas_call(
        _out_proj_kernel,
        grid=(m // tm, n // tn),
        in_specs=in_specs,
        out_specs=pl.BlockSpec((tm, tn), lambda i, j: (i, j)),
        out_shape=jax.ShapeDtypeStruct((m, n), F32),
        compiler_params=_cparams("parallel", "arbitrary"),
        name="out_proj",
    )(h, *ys, *([w_all] * len(ys)))


def _final_norm_kernel(x_ref, g_ref, o_ref):
    x = x_ref[...]
    ms = jnp.mean(x * x, axis=-1, keepdims=True)
    o_ref[...] = x * lax.rsqrt(ms + RMS_EPS) * g_ref[...]


def _final_norm(h, g, *, tm):
    m, n = h.shape
    return pl.pallas_call(
        _final_norm_kernel,
        grid=(m // tm,),
        in_specs=[pl.BlockSpec((tm, n), lambda i: (i, 0)), pl.BlockSpec((1, n), lambda i: (0, 0))],
        out_specs=pl.BlockSpec((tm, n), lambda i: (i, 0)),
        out_shape=jax.ShapeDtypeStruct((m, n), F32),
        compiler_params=_cparams("parallel"),
        name="final_norm",
    )(h, g.reshape(1, n))


ONES_ROWS = 16


def _build_vt(v_ref, vt_ref, hh, dv, seq, chunk):
    def body(c, carry):
        st = pl.multiple_of(c * chunk, chunk)
        vt_ref[hh, 0:dv, pl.ds(st, chunk)] = v_ref[pl.ds(st, chunk), hh * dv:(hh + 1) * dv].astype(F32).T.astype(BF16)
        return carry

    lax.fori_loop(0, seq // chunk, body, 0)
    extra = vt_ref.shape[1] - dv
    if extra:
        vt_ref[hh, dv:dv + extra, :] = jnp.ones((extra, seq), BF16)


def _flash_kernel(*refs, scale, tq, tk, hp, dk, dv, has_decay):
    cw = min(tq, MXU_COLS)
    ncg = tq // cw
    streams = [(hh, cg) for hh in range(hp) for cg in range(ncg)]
    if has_decay:
        q_ref, k_ref, v_ref, g_ref, cq_ref, ck_ref, o_ref, vt_ref, qt_ref, s_ref, m_ref, acc_ref = refs
    else:
        q_ref, k_ref, v_ref, g_ref, o_ref, vt_ref, qt_ref, s_ref, m_ref, acc_ref = refs
    hb = pl.program_id(0)
    qi = pl.program_id(1)
    seq = k_ref.shape[0]

    @pl.when(qi == 0)
    def _():
        for hh in range(hp):
            _build_vt(v_ref, vt_ref, hh, dv, seq, tk)

    for v, (hh, cg) in enumerate(streams):
        qt_ref[v] = (q_ref[cg * cw:(cg + 1) * cw, hh * dk:(hh + 1) * dk].astype(F32) * (scale * LOG2E)).T.astype(BF16)
    m_ref[...] = jnp.full(m_ref.shape, NEG_INF, F32)
    acc_ref[...] = jnp.zeros(acc_ref.shape, F32)
    n_full = (qi * tq) // tk
    if has_decay:
        cq2 = [cq_ref[pl.ds(hb * hp + hh, 1), cg * cw:(cg + 1) * cw] for hh, cg in streams]

    def scores(j, slot):
        start = pl.multiple_of(j * tk, tk)
        for v, (hh, cg) in enumerate(streams):
            s_ref[slot, v] = jnp.dot(k_ref[pl.ds(start, tk), hh * dk:(hh + 1) * dk], qt_ref[v],
                                     preferred_element_type=F32)

    def softmax_pv(j, slot, masked):
        start = pl.multiple_of(j * tk, tk)
        for v, (hh, cg) in enumerate(streams):
            t = s_ref[slot, v]
            if has_decay:
                t = t - jnp.concatenate([ck_ref[hh, pl.ds(start, tk), :]] * (cw // LANES), axis=1)
            if masked:
                key = start + lax.broadcasted_iota(I32, (tk, cw), 0)
                qry = qi * tq + cg * cw + lax.broadcasted_iota(I32, (tk, cw), 1)
                t = jnp.where(key <= qry, t, NEG_INF)
            m_prev = m_ref[v]
            mx = jnp.max(t, axis=0, keepdims=True)
            if has_decay:
                m_new = jnp.maximum(m_prev, mx + cq2[v])
                shift = m_new - cq2[v]
            else:
                m_new = jnp.maximum(m_prev, mx)
                shift = m_new
            alpha = jnp.exp2(m_prev - m_new)
            p = jnp.exp2(t - shift).astype(BF16)
            acc_ref[v] = alpha * acc_ref[v] + jnp.dot(vt_ref[hh, :, pl.ds(start, tk)], p,
                                                       preferred_element_type=F32)
            m_ref[v] = m_new

    scores(0, 0)

    def pair(j):
        scores(j + 1, 1)
        softmax_pv(j, 0, False)
        scores(j + 2, 0)
        softmax_pv(j + 1, 1, False)

    def quad(qq, carry):
        pair(4 * qq)
        pair(4 * qq + 2)
        return carry

    quads = n_full // 4
    lax.fori_loop(0, quads, quad, 0)
    rem = n_full - 4 * quads

    @pl.when(rem >= 2)
    def _():
        pair(4 * quads)

    last = 4 * quads + 2 * (rem // 2)

    @pl.when(n_full % 2 == 1)
    def _():
        scores(last + 1, 1)
        softmax_pv(last, 0, False)
        softmax_pv(last + 1, 1, True)

    @pl.when(n_full % 2 == 0)
    def _():
        softmax_pv(last, 0, True)

    for v, (hh, cg) in enumerate(streams):
        a = acc_ref[v]
        o = (a[0:dv, :] / a[dv:dv + 1, :]).T
        rows = slice(cg * cw, (cg + 1) * cw)
        cols = slice(hh * dv, (hh + 1) * dv)
        o_ref[rows, cols] = (o * _silu(g_ref[rows, cols])).astype(o_ref.dtype)


def _flash(q_arr, q_off, k_arr, k_off, v_arr, v_off, g_arr, g_off, *, heads, dk, dv, scale, tq, tk, hp,
           decay=None, name):
    seq = q_arr.shape[0]
    cw = min(tq, MXU_COLS)
    ns = hp * (tq // cw)
    once = pl.Buffered(1)
    in_specs = [pl.BlockSpec((tq, hp * dk), lambda h, i: (i, q_off // (hp * dk) + h)),
                pl.BlockSpec((seq, hp * dk), lambda h, i: (0, k_off // (hp * dk) + h), pipeline_mode=once),
                pl.BlockSpec((seq, hp * dv), lambda h, i: (0, v_off // (hp * dv) + h), pipeline_mode=once),
                pl.BlockSpec((tq, hp * dv), lambda h, i: (i, g_off // (hp * dv) + h))]
    args = [q_arr, k_arr, v_arr, g_arr]
    if decay is not None:
        cum_t, cum_rep = decay
        in_specs += [pl.BlockSpec((SUBLANES, tq), lambda h, i: (0, i)),
                     pl.BlockSpec((hp, seq, LANES), lambda h, i: (h, 0, 0), pipeline_mode=once)]
        args += [cum_t, cum_rep]
    return pl.pallas_call(
        functools.partial(_flash_kernel, scale=scale, tq=tq, tk=tk, hp=hp, dk=dk, dv=dv, has_decay=decay is not None),
        grid=(heads // hp, seq // tq),
        in_specs=in_specs,
        out_specs=pl.BlockSpec((tq, hp * dv), lambda h, i: (i, h)),
        out_shape=jax.ShapeDtypeStruct((seq, heads * dv), BF16),
        scratch_shapes=[pltpu.VMEM((hp, dv + ONES_ROWS, seq), BF16), pltpu.VMEM((ns, dk, cw), BF16),
                        pltpu.VMEM((2, ns, tk, cw), F32), pltpu.VMEM((ns, 1, cw), F32),
                        pltpu.VMEM((ns, dv + ONES_ROWS, cw), F32)],
        compiler_params=_cparams("parallel", "arbitrary"),
        name=name,
    )(*args)


def _decay_kernel(f_ref, b_ref, ct_ref, cr_ref, carry_ref, *, t):
    i = pl.program_id(0)

    @pl.when(i == 0)
    def _():
        carry_ref[...] = jnp.zeros(carry_ref.shape, F32)

    x = f_ref[...] + b_ref[...]
    lf = jnp.minimum(x, 0.0) - jnp.log1p(jnp.exp(-jnp.abs(x)))
    row = lax.broadcasted_iota(I32, lf.shape, 0)
    s = 1
    while s < t:
        lf = lf + jnp.where(row >= s, pltpu.roll(lf, s, 0), 0.0)
        s *= 2
    lf = lf + carry_ref[...]
    carry_ref[...] = lf[t - 1:t, :]
    lf2 = lf * LOG2E
    ct_ref[...] = lf2.T[:FOX_HEADS, :]
    for h in range(FOX_HEADS):
        cr_ref[h] = jnp.broadcast_to(lf2[:, h:h + 1], (t, LANES))


def _decay(proj, f_cb, b_f, *, t):
    seq = proj.shape[0]
    b = jnp.pad(b_f.reshape(1, FOX_HEADS), ((0, 0), (0, LANES - FOX_HEADS)))
    return pl.pallas_call(
        functools.partial(_decay_kernel, t=t),
        grid=(seq // t,),
        in_specs=[pl.BlockSpec((t, LANES), lambda i: (i, f_cb)), pl.BlockSpec((1, LANES), lambda i: (0, 0))],
        out_specs=[pl.BlockSpec((FOX_HEADS, t), lambda i: (0, i)),
                   pl.BlockSpec((FOX_HEADS, t, LANES), lambda i: (0, i, 0))],
        out_shape=[jax.ShapeDtypeStruct((FOX_HEADS, seq), F32), jax.ShapeDtypeStruct((FOX_HEADS, seq, LANES), F32)],
        scratch_shapes=[pltpu.VMEM((1, LANES), F32)],
        compiler_params=_cparams("arbitrary"),
        name="fox_decay",
    )(proj, b)


def _mem_attn_kernel(q_ref, kv_ref, g_ref, o_ref):
    d = MEM_HEAD_DIM
    for h in range(MEM_HEADS):
        hs = slice(h * d, (h + 1) * d)
        k = kv_ref[:, hs].astype(BF16)
        v = kv_ref[:, MEM_WIDTH + h * d:MEM_WIDTH + (h + 1) * d].astype(BF16)
        s = lax.dot_general(q_ref[:, hs], k, (((1,), (1,)), ((), ())), preferred_element_type=F32) * (d ** -0.5)
        m = jnp.max(s, axis=1, keepdims=True)
        p = jnp.exp(s - m)
        l = jnp.sum(p, axis=1, keepdims=True)
        o = jnp.dot(p.astype(BF16), v, preferred_element_type=F32) / l
        o_ref[:, hs] = (o * _silu(g_ref[:, hs])).astype(o_ref.dtype)


def _mem_kv_all(mem2d, g, w_all, *, tn):
    depth, k, n = w_all.shape
    m = mem2d.shape[0]
    return pl.pallas_call(
        _norm_matmul_kernel,
        grid=(depth, n // tn),
        in_specs=[pl.BlockSpec((m, k), lambda l, j: (0, 0)),
                  pl.BlockSpec((1, k), lambda l, j: (0, 0)),
                  pl.BlockSpec((None, k, tn), lambda l, j: (l, 0, j))],
        out_specs=pl.BlockSpec((None, m, tn), lambda l, j: (l, 0, j)),
        out_shape=jax.ShapeDtypeStruct((depth, m, n), F32),
        scratch_shapes=[pltpu.VMEM((m, k), BF16)],
        compiler_params=_cparams("arbitrary", "arbitrary"),
        name="mem_kv",
    )(mem2d, g.reshape(1, k), w_all)


def _mem_attn(proj_b, q_cb, proj_f, g_cb, mem_kv, *, t):
    seq = proj_b.shape[0]
    mem_kv, layer = mem_kv
    nm = mem_kv.shape[1]
    w = MEM_WIDTH
    d = MEM_HEAD_DIM
    return pl.pallas_call(
        _mem_attn_kernel,
        grid=(seq // t,),
        in_specs=[pl.BlockSpec((t, w), lambda i: (i, q_cb * d // w)),
                  pl.BlockSpec((None, nm, 2 * w), lambda i: (layer, 0, 0)),
                  pl.BlockSpec((t, w), lambda i: (i, g_cb * d // w))],
        out_specs=pl.BlockSpec((t, w), lambda i: (i, 0)),
        out_shape=jax.ShapeDtypeStruct((seq, w), BF16),
        compiler_params=_cparams("parallel"),
        name="mem_attn",
    )(proj_b, mem_kv, proj_f)


S5_TILE_GROUPS = LANES // S5_GROUP
S5_TILE_STATES = S5_TILE_GROUPS * S5_STATE
S5_TILES = S5_GROUPS // S5_TILE_GROUPS


def _s5_scan_kernel(u_ref, b_ref, c_ref, d_ref, tab_ref, z_ref, bu_ref, carry_ref, *, tc):
    ns = S5_TILE_STATES

    @pl.when(pl.program_id(1) == 0)
    def _():
        carry_ref[...] = jnp.zeros(carry_ref.shape, F32)

    u = u_ref[...]
    bu_ref[...] = jnp.dot(u.astype(BF16), b_ref[...], preferred_element_type=F32)
    steps = [(1, tab_ref[0], tab_ref[1]), (2, tab_ref[2], tab_ref[3]), (4, tab_ref[4], tab_ref[5])]
    pr = tab_ref[6]
    pi = tab_ref[7]

    def body(i, carry):
        cr, ci = carry
        r0 = pl.multiple_of(i * SUBLANES, SUBLANES)
        xr = bu_ref[pl.ds(r0, SUBLANES), 0:ns]
        xi = bu_ref[pl.ds(r0, SUBLANES), ns:2 * ns]
        for s, ar, ai in steps:
            sr = pltpu.roll(xr, s, 0)
            si = pltpu.roll(xi, s, 0)
            xr, xi = xr + ar * sr - ai * si, xi + ar * si + ai * sr
        xr, xi = xr + pr * cr - pi * ci, xi + pr * ci + pi * cr
        bu_ref[pl.ds(r0, SUBLANES), 0:ns] = xr
        bu_ref[pl.ds(r0, SUBLANES), ns:2 * ns] = xi
        return xr[SUBLANES - 1:SUBLANES, :], xi[SUBLANES - 1:SUBLANES, :]

    cr, ci = lax.fori_loop(0, tc // SUBLANES, body, (carry_ref[0:1, 0:ns], carry_ref[0:1, ns:2 * ns]))
    carry_ref[0:1, 0:ns] = cr
    carry_ref[0:1, ns:2 * ns] = ci
    y = jnp.dot(bu_ref[...].astype(BF16), c_ref[...], preferred_element_type=F32) + d_ref[...] * u
    z_ref[...] = jax.nn.gelu(y)


def _s5_prepare(lam_re, lam_im, log_dt, b_re, b_im, c_re, c_im):
    dt = jnp.exp(log_dt.astype(F32))[:, None]
    lr = lam_re.astype(F32)
    li = lam_im.astype(F32)
    mag = jnp.exp(lr * dt)
    ab_re = mag * jnp.cos(li * dt)
    ab_im = mag * jnp.sin(li * dt)
    den = lr * lr + li * li
    nr = ab_re - 1.0
    f_re = (nr * lr + ab_im * li) / den
    f_im = (ab_im * lr - nr * li) / den
    br = b_re.astype(F32)
    bim = b_im.astype(F32)
    bb_re = f_re[..., None] * br - f_im[..., None] * bim
    bb_im = f_re[..., None] * bim + f_im[..., None] * br
    eye = jnp.eye(S5_TILE_GROUPS, dtype=F32)

    def blockdiag_in(bb):
        t = bb.reshape(S5_TILES, S5_TILE_GROUPS, S5_STATE, S5_GROUP)
        m = jnp.einsum("jgpc,gh->jgchp", t, eye)
        return m.reshape(S5_TILES, LANES, S5_TILE_STATES)

    def blockdiag_out(cc):
        t = cc.reshape(S5_TILES, S5_TILE_GROUPS, S5_GROUP, S5_STATE)
        m = jnp.einsum("jgcp,gh->jgphc", t, eye)
        return m.reshape(S5_TILES, S5_TILE_STATES, LANES)

    b_cat = jnp.concatenate([blockdiag_in(bb_re), blockdiag_in(bb_im)], axis=2).astype(BF16)
    c_cat = jnp.concatenate([blockdiag_out(c_re.astype(F32)), -blockdiag_out(c_im.astype(F32))], axis=1).astype(BF16)

    a_r = ab_re.reshape(S5_TILES, 1, S5_TILE_STATES)
    a_i = ab_im.reshape(S5_TILES, 1, S5_TILE_STATES)

    def cmul(xr, xi, yr, yi):
        return xr * yr - xi * yi, xr * yi + xi * yr

    a2 = cmul(a_r, a_i, a_r, a_i)
    a4 = cmul(*a2, *a2)
    row = jnp.arange(SUBLANES)[None, :, None]
    tabs = []
    for s, (pr_, pi_) in ((1, (a_r, a_i)), (2, a2), (4, a4)):
        tabs.append(jnp.where(row >= s, pr_, 0.0))
        tabs.append(jnp.where(row >= s, pi_, 0.0))
    pw = [(a_r, a_i)]
    for _ in range(SUBLANES - 1):
        pw.append(cmul(*pw[-1], a_r, a_i))
    tabs.append(jnp.concatenate([p[0] for p in pw], axis=1))
    tabs.append(jnp.concatenate([p[1] for p in pw], axis=1))
    tab = jnp.stack([jnp.broadcast_to(t, (S5_TILES, SUBLANES, S5_TILE_STATES)) for t in tabs], axis=1)
    return b_cat, c_cat, tab.astype(F32)


def _s5_scan(proj, u_cb, b_cat, c_cat, d_skip, tab, *, tc):
    seq = proj.shape[0]
    ns = S5_TILE_STATES
    d = d_skip.astype(F32).reshape(S5_TILES, 1, LANES)
    return pl.pallas_call(
        functools.partial(_s5_scan_kernel, tc=tc),
        grid=(S5_TILES, seq // tc),
        in_specs=[pl.BlockSpec((tc, LANES), lambda j, c: (c, u_cb + j)),
                  pl.BlockSpec((None, LANES, 2 * ns), lambda j, c: (j, 0, 0)),
                  pl.BlockSpec((None, 2 * ns, LANES), lambda j, c: (j, 0, 0)),
                  pl.BlockSpec((None, 1, LANES), lambda j, c: (j, 0, 0)),
                  pl.BlockSpec((None, 8, SUBLANES, ns), lambda j, c: (j, 0, 0, 0))],
        out_specs=pl.BlockSpec((tc, LANES), lambda j, c: (c, j)),
        out_shape=jax.ShapeDtypeStruct((seq, S5_WIDTH), F32),
        scratch_shapes=[pltpu.VMEM((tc, 2 * ns), F32), pltpu.VMEM((SUBLANES, 2 * ns), F32)],
        compiler_params=_cparams("parallel", "arbitrary"),
        name="s5_scan",
    )(proj, b_cat, c_cat, d, tab)


def _s5_glu_kernel(z_ref, w_ref, g_ref, o_ref, *, tn):
    j = pl.program_id(1)
    z = z_ref[...]
    a = jnp.dot(z.astype(BF16), w_ref[...].astype(BF16), preferred_element_type=F32)
    zc = z_ref[:, pl.ds(pl.multiple_of(j * tn, tn), tn)]
    o_ref[...] = (zc * jax.nn.sigmoid(a) * _silu(g_ref[...])).astype(o_ref.dtype)


def _s5_glu(z, w_glu, proj, g_cb, *, tm, tn):
    seq, n = z.shape
    return pl.pallas_call(
        functools.partial(_s5_glu_kernel, tn=tn),
        grid=(seq // tm, n // tn),
        in_specs=[pl.BlockSpec((tm, n), lambda i, j: (i, 0)),
                  pl.BlockSpec((n, tn), lambda i, j: (0, j)),
                  pl.BlockSpec((tm, tn), lambda i, j: (i, g_cb * (n // tn) + j))],
        out_specs=pl.BlockSpec((tm, tn), lambda i, j: (i, j)),
        out_shape=jax.ShapeDtypeStruct((seq, n), BF16),
        compiler_params=_cparams("parallel", "arbitrary"),
        name="s5_glu",
    )(z, w_glu, proj)


def _rope_tables(pos, freq):
    ang = pos * freq
    lane = lax.broadcasted_iota(I32, ang.shape, 1)
    half = MLA_ROPE // 2
    cos = jnp.cos(ang)
    sin = jnp.sin(ang)
    c = jnp.where(lane < MLA_ROPE, cos, 0.0)
    s1 = jnp.where(lane < half, -sin, 0.0)
    s2 = jnp.where((lane >= half) & (lane < MLA_ROPE), sin, 0.0)
    return c, s1, s2


def _rope_apply(x, c, s1, s2):
    half = MLA_ROPE // 2
    return x * c + pltpu.roll(x, LANES - half, 1) * s1 + pltpu.roll(x, half, 1) * s2


def _mla_up_kernel(cq_ref, ckv_ref, kr_ref, pos_ref, freq_ref, gq_ref, gkv_ref, wq_ref, wkv_ref, q_ref, k_ref, v_ref):
    def normed(x_ref, g_ref):
        x = x_ref[...]
        ms = jnp.mean(x * x, axis=-1, keepdims=True)
        return (x * lax.rsqrt(ms + RMS_EPS) * g_ref[...]).astype(BF16)

    qf = jnp.dot(normed(cq_ref, gq_ref), wq_ref[...], preferred_element_type=F32)
    kvf = jnp.dot(normed(ckv_ref, gkv_ref), wkv_ref[...], preferred_element_type=F32)
    c, s1, s2 = _rope_tables(pos_ref[...], freq_ref[...])
    kr = _rope_apply(kr_ref[...], c, s1, s2).astype(BF16)
    for h in range(MLA_HEADS):
        b = 2 * LANES * h
        q_ref[:, b:b + LANES] = qf[:, b:b + LANES].astype(BF16)
        q_ref[:, b + LANES:b + 2 * LANES] = _rope_apply(qf[:, b + LANES:b + 2 * LANES], c, s1, s2).astype(BF16)
        k_ref[:, b:b + LANES] = kvf[:, b:b + LANES].astype(BF16)
        k_ref[:, b + LANES:b + 2 * LANES] = kr
        v_ref[:, LANES * h:LANES * (h + 1)] = kvf[:, b + LANES:b + 2 * LANES].astype(BF16)


def _mla_up(proj, cq_cb, ckv_cb, kr_cb, g_cq, g_ckv, w_uq, w_ukv, pos_col, freq, *, t):
    seq = proj.shape[0]
    w = 2 * LANES * MLA_HEADS
    rq, rkv = w_uq.shape[0], w_ukv.shape[0]
    return pl.pallas_call(
        _mla_up_kernel,
        grid=(seq // t,),
        in_specs=[pl.BlockSpec((t, rq), lambda i: (i, cq_cb)),
                  pl.BlockSpec((t, rkv), lambda i: (i, ckv_cb)),
                  pl.BlockSpec((t, LANES), lambda i: (i, kr_cb)),
                  pl.BlockSpec((t, 1), lambda i: (i, 0)),
                  pl.BlockSpec((1, LANES), lambda i: (0, 0)),
                  pl.BlockSpec((1, rq), lambda i: (0, 0)),
                  pl.BlockSpec((1, rkv), lambda i: (0, 0)),
                  pl.BlockSpec((rq, w), lambda i: (0, 0)),
                  pl.BlockSpec((rkv, w), lambda i: (0, 0))],
        out_specs=[pl.BlockSpec((t, w), lambda i: (i, 0)),
                   pl.BlockSpec((t, w), lambda i: (i, 0)),
                   pl.BlockSpec((t, MLA_WIDTH), lambda i: (i, 0))],
        out_shape=[jax.ShapeDtypeStruct((seq, w), BF16), jax.ShapeDtypeStruct((seq, w), BF16),
                   jax.ShapeDtypeStruct((seq, MLA_WIDTH), BF16)],
        compiler_params=_cparams("parallel"),
        name="mla_up",
    )(proj, proj, proj, pos_col, freq, g_cq.reshape(1, rq), g_ckv.reshape(1, rkv), w_uq, w_ukv)


def _t5_bucket(dist):
    n = jnp.maximum(dist, 0)
    max_exact = T5_BUCKETS // 2
    log_ratio = jnp.log(jnp.maximum(n, 1).astype(F32) / max_exact) / math.log(T5_MAX_DIST / max_exact)
    large = jnp.minimum(max_exact + (log_ratio * (T5_BUCKETS - max_exact)).astype(I32), T5_BUCKETS - 1)
    return jnp.where(n < max_exact, n, large)


T5_MASK_BUCKET = T5_BUCKETS


def _t5_lookup(table_row, bucket):
    rows, width = bucket.shape
    tab = jnp.broadcast_to(table_row, (rows, LANES))
    parts = [jnp.take_along_axis(tab, bucket[:, c:c + LANES], axis=1, mode="promise_in_bounds")
             for c in range(0, width, LANES)]
    return parts[0] if len(parts) == 1 else jnp.concatenate(parts, axis=1)


def _nsa_cmp_kernel(x_ref, pe_ref, w1_ref, w2_ref, o_ref, xf_ref, *, nc):
    half = NSA_CMP_LEN // 2
    d = NSA_HEAD_DIM
    xf_ref[...] = x_ref[...].astype(F32)
    u = jnp.zeros((nc, NSA_CMP_HIDDEN), F32)
    v = jnp.zeros((nc, NSA_CMP_HIDDEN), F32)
    for r in range(half):
        a = xf_ref[pl.ds(r, nc, stride=NSA_CMP_STRIDE), :]
        u = u + jnp.dot((a + pe_ref[r:r + 1, :]).astype(BF16), w1_ref[r * d:(r + 1) * d, :].astype(BF16),
                        preferred_element_type=F32)
        v = v + jnp.dot((a + pe_ref[half + r:half + r + 1, :]).astype(BF16),
                        w1_ref[(half + r) * d:(half + r + 1) * d, :].astype(BF16), preferred_element_type=F32)
    hid = u + pltpu.roll(v, nc - 1, 0)
    o_ref[...] = jnp.dot(jax.nn.gelu(hid).astype(BF16), w2_ref[...].astype(BF16),
                         preferred_element_type=F32).astype(o_ref.dtype)


def _nsa_compress(proj, k_cb, pe, w1, w2):
    seq = proj.shape[0]
    nc = seq // NSA_CMP_STRIDE
    d = NSA_HEAD_DIM
    g = NSA_KV_GROUPS
    return pl.pallas_call(
        functools.partial(_nsa_cmp_kernel, nc=nc),
        grid=(2, g),
        in_specs=[pl.BlockSpec((seq, d), lambda a, b: (0, k_cb + a * g + b)),
                  pl.BlockSpec((None, NSA_CMP_LEN, d), lambda a, b: (a, 0, 0)),
                  pl.BlockSpec((None, NSA_CMP_LEN * d, NSA_CMP_HIDDEN), lambda a, b: (a, 0, 0)),
                  pl.BlockSpec((None, NSA_CMP_HIDDEN, d), lambda a, b: (a, 0, 0))],
        out_specs=pl.BlockSpec((None, None, nc, d), lambda a, b: (a, b, 0, 0)),
        out_shape=jax.ShapeDtypeStruct((2, g, nc, d), BF16),
        scratch_shapes=[pltpu.VMEM((seq, d), F32)],
        compiler_params=_cparams("parallel", "arbitrary"),
        name="nsa_compress",
    )(proj, pe, w1, w2)


def _nsa_select_kernel(consec_ref, q_ref, kc_ref, vc_ref, posq_ref, posc_ref, tab_ref, gate_ref, ov_ref,
                       oc_ref, sel_ref, strip_ref, *, nc, n_slc, tq):
    qi = pl.program_id(0)
    d = NSA_HEAD_DIM
    c = d ** -0.5 * LOG2E
    per_block = tq // NSA_CMP_STRIDE
    lane = lax.broadcasted_iota(I32, (1, tq), 1)
    tok = qi * tq + lane

    def select(bias):
        gates = jax.nn.sigmoid(gate_ref[...])
        ovt = ov_ref[...]
        js = lax.broadcasted_iota(I32, (LANES, tq), 0)
        jf = js.astype(F32)
        cur = tok // NSA_SLC_BLOCK
        forced = (js == 0) | (js == cur) | (js == cur - 1)
        for g in range(NSA_KV_GROUPS):
            kc = kc_ref[g]
            vct = vc_ref[g].astype(F32).T.astype(BF16)
            psum = jnp.zeros((nc, tq), F32)
            for r in range(NSA_REP):
                h = g * NSA_REP + r
                qt = (q_ref[:, h * d:(h + 1) * d].astype(F32) * c).T.astype(BF16)
                t = jnp.dot(kc, qt, preferred_element_type=F32) + bias(h)
                m = jnp.max(t, axis=0, keepdims=True)
                e = jnp.exp2(t - m)
                l = jnp.sum(e, axis=0, keepdims=True)
                p = e * jnp.where(m > 0.5 * NEG_INF, 1.0 / l, 0.0)
                o = jnp.dot(vct, p.astype(BF16), preferred_element_type=F32)
                oc_ref[:, h * d:(h + 1) * d] = gates[:, 3 * h:3 * h + 1] * o.T
                psum = psum + p
            p_hi = psum.astype(BF16)
            p_lo = (psum - p_hi.astype(F32)).astype(BF16)
            imp = jnp.dot(ovt, p_hi, preferred_element_type=F32) + jnp.dot(ovt, p_lo, preferred_element_type=F32)
            st = jnp.where(forced, FORCE_SCORE, jnp.where(js > cur, -1.0, imp))
            st = jnp.where(js < n_slc, st, -2.0)
            sel = jnp.zeros((LANES, tq), F32)
            for _ in range(NSA_SLC_TOPK):
                mx = jnp.max(st, axis=0, keepdims=True)
                first = jnp.min(jnp.where(st == mx, jf, float(LANES)), axis=0, keepdims=True)
                hit = jf == first
                sel = jnp.where(hit, 1.0, sel)
                st = jnp.where(hit, -3e38, st)
            sel_ref[g] = sel.astype(sel_ref.dtype)

    consecutive = consec_ref[0] == 1

    @pl.when(jnp.logical_and(consecutive, qi == 0))
    def _():
        for chunk in range(2 * nc // LANES):
            rel = chunk * LANES - nc + lax.broadcasted_iota(I32, (LANES, 1), 0)
            dist = lane - (rel * NSA_CMP_STRIDE + (NSA_CMP_LEN - 1))
            bucket = jnp.where(dist >= 0, _t5_bucket(dist), T5_MASK_BUCKET)
            for h in range(NSA_HEADS):
                strip_ref[h, chunk * LANES:(chunk + 1) * LANES, :] = _t5_lookup(tab_ref[h:h + 1, :], bucket)

    @pl.when(consecutive)
    def _():
        start = pl.multiple_of(nc - per_block * qi, per_block)
        select(lambda h: strip_ref[h, pl.ds(start, nc), :])

    @pl.when(jnp.logical_not(consecutive))
    def _():
        cmp_end = lax.broadcasted_iota(I32, (nc, 1), 0) * NSA_CMP_STRIDE + (NSA_CMP_LEN - 1)
        pos_c = jnp.concatenate([posc_ref[...]] * (tq // LANES), axis=1)
        bucket = jnp.where(cmp_end <= tok, _t5_bucket(posq_ref[...] - pos_c), T5_MASK_BUCKET)
        select(lambda h: _t5_lookup(tab_ref[h:h + 1, :], bucket))


def _nsa_select(proj_b, q_cb, proj_f, gate_cb, kc, vc, pos_row, pos_cmp_rep, consec, tab_t, ov_t, *, n_slc, tq):
    seq = proj_b.shape[0]
    nc = kc.shape[1]
    g = NSA_KV_GROUPS
    d = NSA_HEAD_DIM
    grid_spec = pltpu.PrefetchScalarGridSpec(
        num_scalar_prefetch=1,
        grid=(seq // tq,),
        in_specs=[pl.BlockSpec((tq, NSA_WIDTH), lambda i, *_: (i, q_cb)),
                  pl.BlockSpec((g, nc, d), lambda i, *_: (0, 0, 0)),
                  pl.BlockSpec((g, nc, d), lambda i, *_: (0, 0, 0)),
                  pl.BlockSpec((1, tq), lambda i, *_: (0, i)),
                  pl.BlockSpec((nc, LANES), lambda i, *_: (0, 0)),
                  pl.BlockSpec((SUBLANES, LANES), lambda i, *_: (0, 0)),
                  pl.BlockSpec((tq, LANES), lambda i, *_: (i, gate_cb)),
                  pl.BlockSpec((LANES, nc), lambda i, *_: (0, 0))],
        out_specs=[pl.BlockSpec((tq, NSA_WIDTH), lambda i, *_: (i, 0)),
                   pl.BlockSpec((g, LANES, tq), lambda i, *_: (0, 0, i))],
        scratch_shapes=[pltpu.VMEM((NSA_HEADS, 2 * nc, tq), F32)])
    return pl.pallas_call(
        functools.partial(_nsa_select_kernel, nc=nc, n_slc=n_slc, tq=tq),
        grid_spec=grid_spec,
        out_shape=[jax.ShapeDtypeStruct((seq, NSA_WIDTH), F32),
                   jax.ShapeDtypeStruct((g, LANES, seq), F32)],
        compiler_params=_cparams("arbitrary"),
        name="nsa_select",
    )(consec, proj_b, kc, vc, pos_row, pos_cmp_rep, tab_t * LOG2E, proj_f, ov_t)


def _nsa_slc_kernel(pqmin_ref, pkmax_ref, consec_ref, q_ref, k_ref, v_ref, sel_ref, posq_ref, posk_ref, tab_ref,
                    o_ref, vt_ref, qt_ref, s_ref, m_ref, l_ref, acc_ref, cache_ref, *, tq, tk):
    g = pl.program_id(0)
    qi = pl.program_id(1)
    d = NSA_HEAD_DIM
    scale = d ** -0.5
    seq = k_ref.shape[0]
    consecutive = consec_ref[0] == 1
    n_cached = cache_ref.shape[0]

    @pl.when(qi == 0)
    def _():
        _build_vt(v_ref, vt_ref, 0, d, seq, tk)

    @pl.when(jnp.logical_and(consecutive, qi == 0))
    def _():
        rel = lax.broadcasted_iota(I32, (1, tq), 1) - lax.broadcasted_iota(I32, (tk, 1), 0)
        for v in range(n_cached):
            dist = rel + v * tq
            bucket = jnp.where(dist >= 0, _t5_bucket(dist), T5_MASK_BUCKET)
            for r in range(NSA_REP):
                cache_ref[v, r] = _t5_lookup(tab_ref[pl.ds(g * NSA_REP + r, 1), :], bucket)

    for r in range(NSA_REP):
        qt_ref[r] = (q_ref[:, r * d:(r + 1) * d].astype(F32) * (scale * LOG2E)).T.astype(BF16)
    m_ref[...] = jnp.full(m_ref.shape, NEG_INF, F32)
    l_ref[...] = jnp.zeros(l_ref.shape, F32)
    acc_ref[...] = jnp.zeros(acc_ref.shape, F32)
    pos_q = posq_ref[...]
    per_tile = tk // NSA_SLC_BLOCK
    n_full = (qi * tq) // tk

    def scores(j, slot):
        start = pl.multiple_of(j * tk, tk)
        k = k_ref[pl.ds(start, tk), :]
        for r in range(NSA_REP):
            s_ref[slot, r] = jnp.dot(k, qt_ref[r], preferred_element_type=F32)

    def softmax_pv(j, slot, masked, mode):
        start = pl.multiple_of(j * tk, tk)
        picked = jnp.concatenate(
            [jnp.broadcast_to(sel_ref[pl.ds(j * per_tile + b, 1), :], (NSA_SLC_BLOCK, tq)) for b in range(per_tile)],
            axis=0)
        if masked and mode == "gather":
            key = start + lax.broadcasted_iota(I32, (tk, tq), 0)
            qry = qi * tq + lax.broadcasted_iota(I32, (tk, tq), 1)
            picked = jnp.where(key <= qry, picked, 0.0)
        mask = picked > 0.5
        if mode == "gather":
            pos_k = jnp.concatenate([posk_ref[pl.ds(start, tk), :]] * (tq // LANES), axis=1)
            bucket = jnp.where(mask, _t5_bucket(pos_q - pos_k), T5_MASK_BUCKET)
        for r in range(NSA_REP):
            h = g * NSA_REP + r
            t = s_ref[slot, r]
            if mode == "gather":
                t = t + _t5_lookup(tab_ref[pl.ds(h, 1), :], bucket)
            elif mode == "cached":
                t = jnp.where(mask, t + cache_ref[(qi * tq - start) // tq, r], NEG_INF)
            else:
                t = jnp.where(mask, t, NEG_INF)
            m_prev = m_ref[r]
            mx = jnp.max(t, axis=0, keepdims=True)
            if mode == "far":
                b = tab_ref[pl.ds(h, 1), T5_BUCKETS - 1:T5_BUCKETS]
                m_new = jnp.maximum(m_prev, mx + b)
                shift = m_new - b
            else:
                m_new = jnp.maximum(m_prev, mx)
                shift = m_new
            alpha = jnp.exp2(m_prev - m_new)
            p = jnp.exp2(t - shift)
            l_ref[r] = alpha * l_ref[r] + jnp.sum(p, axis=0, keepdims=True)
            acc_ref[r] = alpha * acc_ref[r] + jnp.dot(vt_ref[0, :, pl.ds(start, tk)], p.astype(BF16),
                                                       preferred_element_type=F32)
            m_ref[r] = m_new

    def is_far(j):
        return pqmin_ref[qi] - pkmax_ref[j] >= T5_MAX_DIST

    def stages(j, count, mode):
        for k in range(count):
            scores(j + k + 1, (k + 1) % 2)
            softmax_pv(j + k, k % 2, False, mode)

    def near(fn):
        @pl.when(consecutive)
        def _():
            fn("cached")

        @pl.when(jnp.logical_not(consecutive))
        def _():
            fn("gather")

    def far_or_near(far, fn):
        @pl.when(far)
        def _():
            fn("far")

        @pl.when(jnp.logical_not(far))
        def _():
            near(fn)

    scores(0, 0)

    def pair(jj, carry):
        j = 2 * jj
        far_or_near(jnp.logical_and(is_far(j), is_far(j + 1)), lambda mode: stages(j, 2, mode))
        return carry

    pairs = n_full // 2
    lax.fori_loop(0, pairs, pair, 0)
    last = 2 * pairs

    @pl.when(n_full % 2 == 1)
    def _():
        far_or_near(is_far(last), lambda mode: stages(last, 1, mode))
        near(lambda mode: softmax_pv(last + 1, 1, True, mode))

    @pl.when(n_full % 2 == 0)
    def _():
        near(lambda mode: softmax_pv(last, 0, True, mode))

    for r in range(NSA_REP):
        o_ref[:, r * d:(r + 1) * d] = (acc_ref[r] / l_ref[r]).T


def _nsa_slc(proj_b, q_off, k_off, v_off, sel, pos_row, pos_rep, tab_t, pq_min, pk_max, consec, *, tq, tk):
    seq = proj_b.shape[0]
    tab_t = tab_t * LOG2E
    d = NSA_HEAD_DIM
    gw = NSA_REP * d
    once = pl.Buffered(1)
    n_cached = (T5_MAX_DIST + tk - 2) // tq + 1 + tk // tq
    grid_spec = pltpu.PrefetchScalarGridSpec(
        num_scalar_prefetch=3,
        grid=(NSA_KV_GROUPS, seq // tq),
        in_specs=[pl.BlockSpec((tq, gw), lambda g, i, *_: (i, q_off // gw + g)),
                  pl.BlockSpec((seq, d), lambda g, i, *_: (0, k_off // d + g), pipeline_mode=once),
                  pl.BlockSpec((seq, d), lambda g, i, *_: (0, v_off // d + g), pipeline_mode=once),
                  pl.BlockSpec((None, LANES, tq), lambda g, i, *_: (g, 0, i)),
                  pl.BlockSpec((1, tq), lambda g, i, *_: (0, i)),
                  pl.BlockSpec((seq, LANES), lambda g, i, *_: (0, 0), pipeline_mode=once),
                  pl.BlockSpec((SUBLANES, LANES), lambda g, i, *_: (0, 0))],
        out_specs=pl.BlockSpec((tq, gw), lambda g, i, *_: (i, g)),
        scratch_shapes=[pltpu.VMEM((1, d, seq), BF16), pltpu.VMEM((NSA_REP, d, tq), BF16),
                        pltpu.VMEM((2, NSA_REP, tk, tq), F32), pltpu.VMEM((NSA_REP, 1, tq), F32),
                        pltpu.VMEM((NSA_REP, 1, tq), F32), pltpu.VMEM((NSA_REP, d, tq), F32),
                        pltpu.VMEM((n_cached, NSA_REP, tk, tq), F32)])
    return pl.pallas_call(
        functools.partial(_nsa_slc_kernel, tq=tq, tk=tk),
        grid_spec=grid_spec,
        out_shape=jax.ShapeDtypeStruct((seq, NSA_WIDTH), F32),
        compiler_params=_cparams("parallel", "arbitrary"),
        name="nsa_slc",
    )(pq_min, pk_max, consec, proj_b, proj_b, proj_b, sel, pos_row, pos_rep, tab_t)


def _nsa_win_kernel(*refs, tq, nt):
    consec_ref = refs[0]
    q_ref = refs[1]
    k_refs = refs[2:2 + nt]
    v_refs = refs[2 + nt:2 + 2 * nt]
    pk_refs = refs[2 + 2 * nt:2 + 3 * nt]
    posq_ref, tab_ref, gate_ref, gout_ref, oc_ref, os_ref, o_ref, bias_ref = refs[2 + 3 * nt:]
    qi = pl.program_id(0)
    d = NSA_HEAD_DIM
    c = d ** -0.5 * LOG2E
    sub = lax.broadcasted_iota(I32, (tq, 1), 0)
    lane = lax.broadcasted_iota(I32, (1, tq), 1)

    def band_bucket(kidx, tok, dist):
        lower = jnp.maximum(tok - (NSA_WINDOW - 1), 0)
        b = jnp.where(kidx >= lower, _t5_bucket(dist), T5_MASK_BUCKET)
        return jnp.where(kidx <= tok, b, T5_MASK_BUCKET)

    def attend(bias):
        gates = jax.nn.sigmoid(gate_ref[...])
        for g in range(NSA_KV_GROUPS):
            ks = [kr[:, g * d:(g + 1) * d] for kr in k_refs]
            vts = [vr[:, g * d:(g + 1) * d].astype(F32).T.astype(BF16) for vr in v_refs]
            for r in range(NSA_REP):
                h = g * NSA_REP + r
                hs = slice(h * d, (h + 1) * d)
                qt = (q_ref[:, hs].astype(F32) * c).T.astype(BF16)
                ts = [jnp.dot(ks[jj], qt, preferred_element_type=F32) + bias(h, jj) for jj in range(nt)]
                m = functools.reduce(jnp.maximum, [jnp.max(t, axis=0, keepdims=True) for t in ts])
                ps = [jnp.exp2(t - m) for t in ts]
                l = functools.reduce(jnp.add, [jnp.sum(p, axis=0, keepdims=True) for p in ps])
                o_t = functools.reduce(jnp.add, [jnp.dot(vt, p.astype(BF16), preferred_element_type=F32)
                                                 for vt, p in zip(vts, ps)])
                o_w = (o_t / l).T
                o = (oc_ref[:, hs] + gates[:, 3 * h + 1:3 * h + 2] * os_ref[:, hs]
                     + gates[:, 3 * h + 2:3 * h + 3] * o_w)
                o_ref[:, hs] = (o * _silu(gout_ref[:, hs])).astype(o_ref.dtype)

    consecutive = consec_ref[0] == 1

    @pl.when(jnp.logical_and(consecutive, qi == 0))
    def _():
        tok0 = (nt - 1) * tq + lane
        for jj in range(nt):
            kidx0 = jj * tq + sub
            bucket = band_bucket(kidx0, tok0, tok0 - kidx0)
            for h in range(NSA_HEADS):
                bias_ref[h, jj] = _t5_lookup(tab_ref[h:h + 1, :], bucket)

    cached = jnp.logical_and(consecutive, qi >= nt - 1)

    @pl.when(cached)
    def _():
        attend(lambda h, jj: bias_ref[h, jj])

    @pl.when(jnp.logical_not(cached))
    def _():
        tok = qi * tq + lane
        pos_q = posq_ref[...]
        buckets = []
        for jj in range(nt):
            kidx = (qi - (nt - 1) + jj) * tq + sub
            pos_k = jnp.concatenate([pk_refs[jj][...]] * (tq // LANES), axis=1)
            buckets.append(band_bucket(kidx, tok, pos_q - pos_k))
        attend(lambda h, jj: _t5_lookup(tab_ref[h:h + 1, :], buckets[jj]))


def _nsa_win(proj_b, q_cb, k_cb, v_cb, proj_f, gate_cb, gout_cb, oc, o_s, pos_row, pos_rep, consec, tab_t, *, tq):
    seq = proj_b.shape[0]
    nt = NSA_WINDOW // tq + 1

    def band_rows(jj, cb):
        return pl.BlockSpec((tq, NSA_KV), lambda i, *_: (jnp.maximum(i - (nt - 1) + jj, 0), cb))

    def band_pos(jj):
        return pl.BlockSpec((tq, LANES), lambda i, *_: (jnp.maximum(i - (nt - 1) + jj, 0), 0))

    in_specs = [pl.BlockSpec((tq, NSA_WIDTH), lambda i, *_: (i, q_cb))]
    in_specs += [band_rows(jj, k_cb) for jj in range(nt)]
    in_specs += [band_rows(jj, v_cb) for jj in range(nt)]
    in_specs += [band_pos(jj) for jj in range(nt)]
    in_specs += [pl.BlockSpec((1, tq), lambda i, *_: (0, i)),
                 pl.BlockSpec((SUBLANES, LANES), lambda i, *_: (0, 0)),
                 pl.BlockSpec((tq, LANES), lambda i, *_: (i, gate_cb)),
                 pl.BlockSpec((tq, NSA_WIDTH), lambda i, *_: (i, gout_cb)),
                 pl.BlockSpec((tq, NSA_WIDTH), lambda i, *_: (i, 0)),
                 pl.BlockSpec((tq, NSA_WIDTH), lambda i, *_: (i, 0))]
    args = [proj_b] * (1 + 2 * nt) + [pos_rep] * nt + [pos_row, tab_t * LOG2E, proj_f, proj_f, oc, o_s]
    grid_spec = pltpu.PrefetchScalarGridSpec(
        num_scalar_prefetch=1,
        grid=(seq // tq,),
        in_specs=in_specs,
        out_specs=pl.BlockSpec((tq, NSA_WIDTH), lambda i, *_: (i, 0)),
        scratch_shapes=[pltpu.VMEM((NSA_HEADS, nt, tq, tq), F32)])
    return pl.pallas_call(
        functools.partial(_nsa_win_kernel, tq=tq, nt=nt),
        grid_spec=grid_spec,
        out_shape=jax.ShapeDtypeStruct((seq, NSA_WIDTH), BF16),
        compiler_params=_cparams("arbitrary"),
        name="nsa_win",
    )(consec, *args)


def _in_proj(h, norm_g, w_b, w_f, tiles, name):
    tn_b = w_b.shape[1] // tiles["col_tiles_b"]
    tn_f = w_f.shape[1] // tiles["col_tiles_f"]
    assert tn_b % MXU_COLS == 0 and tn_f % MXU_COLS == 0
    proj_b = _norm_matmul(h, norm_g, w_b, tm=tiles["tm"], tn=tn_b, name=name + "_b", out_dtype=BF16)
    proj_f = _norm_matmul(h, norm_g, w_f, tm=tiles["tm"], tn=tn_f, name=name + "_f")
    return proj_b, proj_f


def _even_layer(h, mem_kv, norm_g, w_in, s5, w_glu, b_f, tiles):
    pb, pf = _in_proj(h, norm_g, *w_in, tiles, "in_proj_even")
    ob, of = EVEN_B_OFF, EVEN_F_OFF
    b_cat, c_cat, tab, d_skip = s5
    z = _s5_scan(pf, of["u"] // LANES, b_cat, c_cat, d_skip, tab, tc=tiles["s5_tc"])
    y_s5 = _s5_glu(z, w_glu, pf, of["g_s5"] // S5_WIDTH, tm=tiles["tm"], tn=tiles["tn_glu"])
    decay = _decay(pf, of["f"] // LANES, b_f, t=tiles["decay_t"])
    d = FOX_HEAD_DIM
    y_fox = _flash(pb, ob["q"], pb, ob["k"], pb, ob["v"], pf, of["g_fox"], heads=FOX_HEADS, dk=d, dv=d,
                   scale=d ** -0.5, tq=tiles["attn_tq"], tk=tiles["attn_tk"], hp=tiles["attn_hp"], decay=decay,
                   name="fox_attn")
    y_mem = _mem_attn(pb, ob["q_mem"] // MEM_HEAD_DIM, pf, of["g_mem"] // MEM_HEAD_DIM, mem_kv, t=tiles["mem_t"])
    return y_s5, y_fox, y_mem


def _odd_layer(h, mem_kv, norm_g, w_in, mla, nsa, pos, tiles):
    pb, pf = _in_proj(h, norm_g, *w_in, tiles, "in_proj_odd")
    ob, of = ODD_B_OFF, ODD_F_OFF
    g_cq, g_ckv, w_uq, w_ukv, freq = mla
    pos_col_f, pos_row, pos_cmp_rep, pos_rep, pq_min, pk_max, consec = pos
    q_r, k_r, v_r = _mla_up(pf, of["c_q"] // MLA_Q_RANK, of["c_kv"] // MLA_KV_RANK, of["k_rope"] // LANES,
                            g_cq, g_ckv, w_uq, w_ukv, pos_col_f, freq, t=tiles["prep_t"])
    y_mla = _flash(q_r, 0, k_r, 0, v_r, 0, pf, of["g_mla"], heads=MLA_HEADS, dk=2 * LANES, dv=MLA_V,
                   scale=(MLA_NOPE + MLA_ROPE) ** -0.5, tq=tiles["attn_tq"], tk=tiles["attn_tk"], hp=tiles["attn_hp"],
                   name="mla_attn")
    pe, w1, w2, tab_t, ov, n_slc = nsa
    kvc = _nsa_compress(pb, ob["k_cmp"] // NSA_HEAD_DIM, pe, w1, w2)
    oc, sel = _nsa_select(pb, ob["q_nsa"] // NSA_WIDTH, pf, of["gates"] // LANES, kvc[0], kvc[1], pos_row,
                          pos_cmp_rep, consec, tab_t, ov, n_slc=n_slc, tq=tiles["nsa_tq"])
    o_s = _nsa_slc(pb, ob["q_nsa"], ob["k_slc"], ob["v_slc"], sel, pos_row, pos_rep, tab_t, pq_min, pk_max, consec,
                   tq=tiles["slc_tq"], tk=tiles["slc_tk"])
    y_nsa = _nsa_win(pb, ob["q_nsa"] // NSA_WIDTH, ob["k_win"] // NSA_KV, ob["v_win"] // NSA_KV,
                     pf, of["gates"] // LANES, of["g_nsa"] // NSA_WIDTH, oc, o_s, pos_row, pos_rep, consec, tab_t,
                     tq=tiles["win_tq"])
    y_mem = _mem_attn(pb, ob["q_mem"] // MEM_HEAD_DIM, pf, of["g_mem"] // MEM_HEAD_DIM, mem_kv, t=tiles["mem_t"])
    return y_mla, y_nsa, y_mem


def _tiles(seq):
    return {"tm": min(seq, 1024), "col_tiles_b": 2, "col_tiles_f": 3, "tn_out": 1024, "tn_glu": 1024, "tn_mem": 512,
            "s5_tc": min(seq, 2048), "decay_t": min(seq, 512), "attn_tq": min(seq, 512),
            "attn_tk": min(seq, 512), "attn_hp": 2, "slc_tq": 256, "slc_tk": 512, "nsa_tq": 256, "win_tq": 128,
            "mem_t": min(seq, 1024), "norm_t": min(seq, 1024), "prep_t": min(seq, 512)}


def _context(positions, t5_table, seq, tiles):
    pos = positions[0]
    pos_col = pos.reshape(seq, 1)
    pos_row = pos.reshape(1, seq)
    pos_rep = jnp.broadcast_to(pos_col, (seq, LANES))
    pq_min = jnp.min(pos.reshape(seq // tiles["slc_tq"], tiles["slc_tq"]), axis=1)
    pk_max = jnp.max(pos.reshape(seq // tiles["slc_tk"], tiles["slc_tk"]), axis=1)
    nc = seq // NSA_CMP_STRIDE
    pos_cmp = jnp.pad(pos[NSA_CMP_LEN - 1::NSA_CMP_STRIDE], (0, 1))
    pos_cmp_rep = jnp.broadcast_to(pos_cmp.reshape(nc, 1), (nc, LANES))
    half = MLA_ROPE // 2
    inv_freq = ROPE_THETA ** (-jnp.arange(half, dtype=F32) / half)
    freq = jnp.concatenate([inv_freq, inv_freq, jnp.zeros((LANES - MLA_ROPE,), F32)]).reshape(1, LANES)
    tab_t = jnp.pad(t5_table.astype(F32).T, ((0, SUBLANES - NSA_HEADS), (0, LANES - T5_BUCKETS)))
    tab_t = tab_t.at[:, T5_MASK_BUCKET].set(NEG_INF)
    n_slc = seq // NSA_SLC_BLOCK
    cs = np.arange(nc) * NSA_CMP_STRIDE
    ss = np.arange(LANES) * NSA_SLC_BLOCK
    ov_np = np.clip(np.minimum(cs[:, None] + NSA_CMP_LEN, ss[None, :] + NSA_SLC_BLOCK)
                    - np.maximum(cs[:, None], ss[None, :]), 0, None) / NSA_CMP_LEN
    ov_np[nc - 1, :] = 0.0
    ov_np[:, n_slc:] = 0.0
    consec = jnp.all(pos[1:] - pos[:-1] == 1).astype(I32).reshape(1)
    return {"pos": (pos_col.astype(F32), pos_row, pos_cmp_rep, pos_rep, pq_min, pk_max, consec), "freq": freq,
            "tab_t": tab_t, "ov": jnp.asarray(ov_np.T, BF16), "n_slc": n_slc}


def _odd_params(i, mla_g_cq, mla_g_ckv, mla_w_uq, mla_w_ukv, nsa_cmp_pe, nsa_cmp_w1, nsa_cmp_w2, ctx):
    dq = MLA_NOPE + MLA_ROPE
    w_uq = mla_w_uq[i].reshape(MLA_Q_RANK, MLA_HEADS, dq)
    w_uq = jnp.pad(w_uq, ((0, 0), (0, 0), (0, 2 * LANES - dq))).reshape(MLA_Q_RANK, -1).astype(BF16)
    mla = (mla_g_cq[i], mla_g_ckv[i], w_uq, mla_w_ukv[i].astype(BF16), ctx["freq"])
    nsa = (nsa_cmp_pe[i].astype(F32), nsa_cmp_w1[i], nsa_cmp_w2[i], ctx["tab_t"],
           ctx["ov"], ctx["n_slc"])
    return mla, nsa


def kernel(x, mem, positions, norm_g, mem_norm_g, final_norm_g, t5_table, w_out, mem_w_kv, even_w_in, s5_lam_re,
           s5_lam_im, s5_log_dt, s5_b_re, s5_b_im, s5_c_re, s5_c_im, s5_d, s5_w_glu, fox_b_f, odd_w_in, mla_g_cq,
           mla_g_ckv, mla_w_uq, mla_w_ukv, nsa_cmp_pe, nsa_cmp_w1, nsa_cmp_w2):
    batch, seq, _ = x.shape
    assert batch == 1 and seq % 1024 == 0 and seq // NSA_SLC_BLOCK <= LANES
    depth = norm_g.shape[0]
    tiles = _tiles(seq)
    ctx = _context(positions, t5_table, seq, tiles)
    h = x[0]
    mem_kv_all = _mem_kv_all(mem[0], mem_norm_g, mem_w_kv, tn=tiles["tn_mem"])
    s5_b_cat, s5_c_cat, s5_tab = jax.vmap(_s5_prepare)(s5_lam_re, s5_lam_im, s5_log_dt, s5_b_re, s5_b_im,
                                                       s5_c_re, s5_c_im)
    for layer in range(depth):
        i = layer // 2
        mem_kv = (mem_kv_all, layer)
        if layer % 2 == 0:
            w_in = (_reorder_w_in(even_w_in[i], EVEN_SPLITS, EVEN_B_ORDER),
                    _reorder_w_in(even_w_in[i], EVEN_SPLITS, EVEN_F_ORDER))
            ys = _even_layer(h, mem_kv, norm_g[layer], w_in, (s5_b_cat[i], s5_c_cat[i], s5_tab[i], s5_d[i]),
                             s5_w_glu[i], fox_b_f[i], tiles)
        else:
            w_in = (_reorder_w_in(odd_w_in[i], ODD_SPLITS, ODD_B_ORDER),
                    _reorder_w_in(odd_w_in[i], ODD_SPLITS, ODD_F_ORDER))
            mla, nsa = _odd_params(i, mla_g_cq, mla_g_ckv, mla_w_uq, mla_w_ukv, nsa_cmp_pe, nsa_cmp_w1, nsa_cmp_w2,
                                   ctx)
            ys = _odd_layer(h, mem_kv, norm_g[layer], w_in, mla, nsa, ctx["pos"], tiles)
        h = _out_proj(h, ys, w_out, layer, tm=tiles["tm"], tn=tiles["tn_out"])
    return _final_norm(h, final_norm_g, tm=tiles["norm_t"])[None]
```
